```python
import math
import jax, jax.numpy as jnp
from jax import lax
import numpy as np

D_MODEL = 1024
BATCH = 8
SEQ = 2048
DEPTH = 1

GDN_HEADS = 8
GDN_HEAD_DIM = 128
GDN_WIDTH = GDN_HEADS * GDN_HEAD_DIM
CONV_K = 5
GLA_HEADS = 4
GLA_KEY_DIM = D_MODEL // 2
GLA_VAL_DIM = D_MODEL
GLA_HEAD_K = GLA_KEY_DIM // GLA_HEADS
GLA_HEAD_V = GLA_VAL_DIM // GLA_HEADS
GLA_GATE_RANK = 16
GLA_GATE_NORMALIZER = 16.0
CHUNK = 64
NORM_EPS = 1e-6

IN_SIZES = [
    3 * GDN_WIDTH,
    GDN_WIDTH,
    GDN_HEADS,
    GDN_HEADS,
    GDN_HEADS,
    GDN_HEADS,
    GLA_KEY_DIM,
    GLA_KEY_DIM,
    GLA_VAL_DIM,
    GLA_VAL_DIM,
    GLA_GATE_RANK,
    GLA_GATE_RANK,
    D_MODEL,
    D_MODEL,
]
N_IN = int(sum(IN_SIZES))
IN_SPLITS = [int(s) for s in np.cumsum(IN_SIZES)[:-1]]

kernel_name = "bidir_gdn_gla_gated_hybrid"


def rmsnorm(x, w):
    xf = x.astype(jnp.float32)
    y = xf * lax.rsqrt(jnp.mean(xf * xf, axis=-1, keepdims=True) + NORM_EPS)
    return (y * w.astype(jnp.float32)).astype(x.dtype)


def l2norm(x):
    return x * lax.rsqrt(jnp.sum(x * x, axis=-1, keepdims=True) + NORM_EPS)


def to_heads(x, n_heads):
    b, t, _ = x.shape
    return x.reshape(b, t, n_heads, -1).transpose(0, 2, 1, 3)


def from_heads(x):
    return x.transpose(0, 2, 1, 3)


def to_chunks(x):
    b, h, t = x.shape[:3]
    return x.reshape(b, h, t // CHUNK, CHUNK, *x.shape[3:])


def centred_depthwise_conv(x, w):
    c = x.shape[-1]
    return lax.conv_general_dilated(
        x, w[:, None, :].astype(x.dtype), window_strides=(1,),
        padding=[(CONV_K // 2, CONV_K // 2)],
        dimension_numbers=("NWC", "WIO", "NWC"), feature_group_count=c)


def gated_delta_rule(q, k, v, g, beta):
    bsz, nh, t, dk = q.shape
    dv = v.shape[-1]
    q = to_chunks(q * (dk ** -0.5))
    k = to_chunks(k)
    v = to_chunks(v)
    beta = to_chunks(beta)
    g = jnp.cumsum(to_chunks(g), axis=-1)
    incl = jnp.tril(jnp.ones((CHUNK, CHUNK), dtype=bool))
    strict = jnp.tril(jnp.ones((CHUNK, CHUNK), dtype=bool), -1)
    diff = g[..., :, None] - g[..., None, :]
    decay = jnp.where(incl, jnp.exp(jnp.where(incl, diff, 0.0)), 0.0)
    kb = k * beta[..., None]
    lower = jnp.where(strict, jnp.einsum("bhnid,bhnjd->bhnij", kb, k) * decay, 0.0)
    rhs = jnp.concatenate([v * beta[..., None], kb * jnp.exp(g)[..., None]], axis=-1)
    sol = lax.linalg.triangular_solve(lower, rhs, left_side=True, lower=True,
                                      unit_diagonal=True)
    u, w = sol[..., :dv], sol[..., dv:]
    attn = jnp.einsum("bhnid,bhnjd->bhnij", q, k) * decay
    q_dec = q * jnp.exp(g)[..., None]
    g_last = g[..., -1]
    k_dec = k * jnp.exp(g_last[..., None] - g)[..., None]

    def step(S, inp):
        u_n, w_n, attn_n, qd_n, kd_n, gl_n = inp
        v_new = u_n - jnp.einsum("bhcd,bhde->bhce", w_n, S)
        o = (jnp.einsum("bhcd,bhde->bhce", qd_n, S)
             + jnp.einsum("bhij,bhje->bhie", attn_n, v_new))
        S = S * jnp.exp(gl_n)[..., None, None] + jnp.einsum("bhcd,bhce->bhde", kd_n, v_new)
        return S, o

    xs = tuple(jnp.moveaxis(a, 2, 0) for a in (u, w, attn, q_dec, k_dec, g_last))
    S0 = jnp.zeros((bsz, nh, dk, dv), q.dtype)
    _, o = lax.scan(step, S0, xs)
    return jnp.moveaxis(o, 0, 2).reshape(bsz, nh, t, dv)


def gla_chunked(q, k, v, gk):
    bsz, nh, t, dk = q.shape
    dv = v.shape[-1]
    q = to_chunks(q * (dk ** -0.5))
    k = to_chunks(k)
    v = to_chunks(v)
    G = jnp.cumsum(to_chunks(gk), axis=3)
    qg = q * jnp.exp(G)
    kg = k * jnp.exp(-G)
    incl = jnp.tril(jnp.ones((CHUNK, CHUNK), dtype=bool))
    attn = jnp.where(incl, jnp.einsum("bhnid,bhnjd->bhnij", qg, kg), 0.0)
    intra = jnp.einsum("bhnij,bhnje->bhnie", attn, v)
    G_last = G[..., -1, :]
    k_dec = k * jnp.exp(G_last[..., None, :] - G)

    def step(S, inp):
        qg_n, kd_n, v_n, gl_n = inp
        o = jnp.einsum("bhcd,bhde->bhce", qg_n, S)
        S = S * jnp.exp(gl_n)[..., :, None] + jnp.einsum("bhcd,bhce->bhde", kd_n, v_n)
        return S, o

    xs = tuple(jnp.moveaxis(a, 2, 0) for a in (qg, k_dec, v, G_last))
    S0 = jnp.zeros((bsz, nh, dk, dv), q.dtype)
    _, inter = lax.scan(step, S0, xs)
    o = jnp.moveaxis(inter, 0, 2) + intra
    return o.reshape(bsz, nh, t, dv)


def flip_t(a):
    return jnp.flip(a, axis=2)


def hybrid_layer(x, ln_pre_w, w_in, conv_w, a_log_fwd, a_log_bwd, dt_bias_fwd, dt_bias_bwd,
                 gdn_norm_w, w_proj_gdn, gk_w2_fwd, gk_b2_fwd, gk_w2_bwd, gk_b2_bwd,
                 gla_norm_w, w_proj_gla, w_out, ln_post_w):
    f32 = jnp.float32
    h = rmsnorm(x, ln_pre_w)
    proj = h @ w_in
    (qkv_a, z_a, a_f, a_b, b_f, b_b, q_b, k_b, v_b, g_b,
     r_f, r_b, gate_a, gate_b) = jnp.split(proj, IN_SPLITS, axis=-1)

    qkv_a = jax.nn.silu(centred_depthwise_conv(qkv_a, conv_w)).astype(f32)
    q_a, k_a, v_a = jnp.split(qkv_a, 3, axis=-1)
    q_a = l2norm(to_heads(q_a, GDN_HEADS))
    k_a = l2norm(to_heads(k_a, GDN_HEADS))
    v_a = to_heads(v_a, GDN_HEADS)
    lg_f = (-jnp.exp(a_log_fwd.astype(f32)) * jax.nn.softplus(a_f.astype(f32) + dt_bias_fwd.astype(f32))).transpose(0, 2, 1)
    lg_b = (-jnp.exp(a_log_bwd.astype(f32)) * jax.nn.softplus(a_b.astype(f32) + dt_bias_bwd.astype(f32))).transpose(0, 2, 1)
    beta_f = jax.nn.sigmoid(b_f.astype(f32)).transpose(0, 2, 1)
    beta_b = jax.nn.sigmoid(b_b.astype(f32)).transpose(0, 2, 1)
    o_a = (gated_delta_rule(q_a, k_a, v_a, lg_f, beta_f)
           + flip_t(gated_delta_rule(flip_t(q_a), flip_t(k_a), flip_t(v_a),
                                     flip_t(lg_b), flip_t(beta_b))))
    o_a = rmsnorm(from_heads(o_a), gdn_norm_w)
    o_a = o_a * jax.nn.silu(z_a.astype(f32)).reshape(o_a.shape)
    y_a = o_a.reshape(x.shape[0], x.shape[1], GDN_WIDTH).astype(x.dtype) @ w_proj_gdn

    q_bh = to_heads(q_b.astype(f32), GLA_HEADS)
    k_bh = to_heads(k_b.astype(f32), GLA_HEADS)
    v_bh = to_heads(v_b.astype(f32), GLA_HEADS)
    gk_f = jax.nn.log_sigmoid((r_f @ gk_w2_fwd + gk_b2_fwd).astype(f32)) / GLA_GATE_NORMALIZER
    gk_b = jax.nn.log_sigmoid((r_b @ gk_w2_bwd + gk_b2_bwd).astype(f32)) / GLA_GATE_NORMALIZER
    gk_f = to_heads(gk_f, GLA_HEADS)
    gk_b = to_heads(gk_b, GLA_HEADS)
    o_b = (gla_chunked(q_bh, k_bh, v_bh, gk_f)
           + flip_t(gla_chunked(flip_t(q_bh), flip_t(k_bh), flip_t(v_bh), flip_t(gk_b))))
    o_b = rmsnorm(from_heads(o_b), gla_norm_w)
    o_b = o_b * jax.nn.silu(g_b.astype(f32)).reshape(o_b.shape)
    y_b = o_b.reshape(x.shape[0], x.shape[1], GLA_VAL_DIM).astype(x.dtype) @ w_proj_gla

    merged = jax.nn.sigmoid(gate_a) * y_a + jax.nn.sigmoid(gate_b) * y_b
    out = merged @ w_out
    return x + rmsnorm(out, ln_post_w)


def _fwd_setup_inputs(seed: int = 0) -> dict:
    key = jax.random.key(seed)
    ks = jax.random.split(key, 20)
    L, D = DEPTH, D_MODEL

    def nrm(k, shape, scale):
        return jax.random.normal(k, shape, jnp.float32) * scale

    def dt_bias(k):
        u = jax.random.uniform(k, (L, GDN_HEADS), jnp.float32)
        dt = jnp.exp(u * (math.log(0.1) - math.log(0.001)) + math.log(0.001))
        return dt + jnp.log(-jnp.expm1(-dt))

    def a_log(k):
        return jnp.log(jax.random.uniform(k, (L, GDN_HEADS), jnp.float32, 1.0, 16.0))

    return {
        "x": nrm(ks[0], (BATCH, SEQ, D), 1.0),
        "ln_pre_w": 1.0 + nrm(ks[1], (L, D), 0.02),
        "w_in": nrm(ks[2], (L, D, N_IN), D ** -0.5),
        "conv_w": nrm(ks[3], (L, CONV_K, 3 * GDN_WIDTH), CONV_K ** -0.5),
        "a_log_fwd": a_log(ks[4]),
        "a_log_bwd": a_log(ks[5]),
        "dt_bias_fwd": dt_bias(ks[6]),
        "dt_bias_bwd": dt_bias(ks[7]),
        "gdn_norm_w": 1.0 + nrm(ks[8], (L, GDN_HEAD_DIM), 0.02),
        "w_proj_gdn": nrm(ks[9], (L, GDN_WIDTH, D), GDN_WIDTH ** -0.5),
        "gk_w2_fwd": nrm(ks[10], (L, GLA_GATE_RANK, GLA_KEY_DIM), GLA_GATE_RANK ** -0.5),
        "gk_b2_fwd": nrm(ks[11], (L, GLA_KEY_DIM), 0.01),
        "gk_w2_bwd": nrm(ks[12], (L, GLA_GATE_RANK, GLA_KEY_DIM), GLA_GATE_RANK ** -0.5),
        "gk_b2_bwd": nrm(ks[13], (L, GLA_KEY_DIM), 0.01),
        "gla_norm_w": 1.0 + nrm(ks[14], (L, GLA_HEAD_V), 0.02),
        "w_proj_gla": nrm(ks[15], (L, GLA_VAL_DIM, D), GLA_VAL_DIM ** -0.5),
        "w_out": nrm(ks[16], (L, D, D), D ** -0.5),
        "ln_post_w": 1.0 + nrm(ks[17], (L, D), 0.02),
    }


def _fwd_reference(x, ln_pre_w, w_in, conv_w, a_log_fwd, a_log_bwd, dt_bias_fwd, dt_bias_bwd,
              gdn_norm_w, w_proj_gdn, gk_w2_fwd, gk_b2_fwd, gk_w2_bwd, gk_b2_bwd,
              gla_norm_w, w_proj_gla, w_out, ln_post_w):
    h = x
    for l in range(DEPTH):
        h = hybrid_layer(h, ln_pre_w[l], w_in[l], conv_w[l], a_log_fwd[l], a_log_bwd[l],
                         dt_bias_fwd[l], dt_bias_bwd[l], gdn_norm_w[l], w_proj_gdn[l],
                         gk_w2_fwd[l], gk_b2_fwd[l], gk_w2_bwd[l], gk_b2_bwd[l],
                         gla_norm_w[l], w_proj_gla[l], w_out[l], ln_post_w[l])
    return h


import jax as _jax
import jax.numpy as _jnp

TWIN_FORMAT = 'train_step'
FWD_PARAMS = ['x', 'ln_pre_w', 'w_in', 'conv_w', 'a_log_fwd', 'a_log_bwd', 'dt_bias_fwd', 'dt_bias_bwd', 'gdn_norm_w', 'w_proj_gdn', 'gk_w2_fwd', 'gk_b2_fwd', 'gk_w2_bwd', 'gk_b2_bwd', 'gla_norm_w', 'w_proj_gla', 'w_out', 'ln_post_w']
TWIN_WEIGHTS = ['ln_pre_w', 'w_in', 'conv_w', 'a_log_fwd', 'a_log_bwd', 'dt_bias_fwd', 'dt_bias_bwd', 'gdn_norm_w', 'w_proj_gdn', 'gk_w2_fwd', 'gk_b2_fwd', 'gk_w2_bwd', 'gk_b2_bwd', 'gla_norm_w', 'w_proj_gla', 'w_out', 'ln_post_w']
TWIN_DIFF_INPUT = 'x'
TWIN_INPUTS = ['x', 'ln_pre_w', 'w_in', 'conv_w', 'a_log_fwd', 'a_log_bwd', 'dt_bias_fwd', 'dt_bias_bwd', 'gdn_norm_w', 'w_proj_gdn', 'gk_w2_fwd', 'gk_b2_fwd', 'gk_w2_bwd', 'gk_b2_bwd', 'gla_norm_w', 'w_proj_gla', 'w_out', 'ln_post_w', 'loss_target', 'm_ln_pre_w', 'm_w_in', 'm_conv_w', 'm_a_log_fwd', 'm_a_log_bwd', 'm_dt_bias_fwd', 'm_dt_bias_bwd', 'm_gdn_norm_w', 'm_w_proj_gdn', 'm_gk_w2_fwd', 'm_gk_b2_fwd', 'm_gk_w2_bwd', 'm_gk_b2_bwd', 'm_gla_norm_w', 'm_w_proj_gla', 'm_w_out', 'm_ln_post_w', 'v_ln_pre_w', 'v_w_in', 'v_conv_w', 'v_a_log_fwd', 'v_a_log_bwd', 'v_dt_bias_fwd', 'v_dt_bias_bwd', 'v_gdn_norm_w', 'v_w_proj_gdn', 'v_gk_w2_fwd', 'v_gk_b2_fwd', 'v_gk_w2_bwd', 'v_gk_b2_bwd', 'v_gla_norm_w', 'v_w_proj_gla', 'v_w_out', 'v_ln_post_w']
TWIN_OUTPUTS = ['loss', 'grad_x', 'grad_ln_pre_w', 'grad_w_in', 'grad_conv_w', 'grad_a_log_fwd', 'grad_a_log_bwd', 'grad_dt_bias_fwd', 'grad_dt_bias_bwd', 'grad_gdn_norm_w', 'grad_w_proj_gdn', 'grad_gk_w2_fwd', 'grad_gk_b2_fwd', 'grad_gk_w2_bwd', 'grad_gk_b2_bwd', 'grad_gla_norm_w', 'grad_w_proj_gla', 'grad_w_out', 'grad_ln_post_w', 'delta_ln_pre_w', 'delta_w_in', 'delta_conv_w', 'delta_a_log_fwd', 'delta_a_log_bwd', 'delta_dt_bias_fwd', 'delta_dt_bias_bwd', 'delta_gdn_norm_w', 'delta_w_proj_gdn', 'delta_gk_w2_fwd', 'delta_gk_b2_fwd', 'delta_gk_w2_bwd', 'delta_gk_b2_bwd', 'delta_gla_norm_w', 'delta_w_proj_gla', 'delta_w_out', 'delta_ln_post_w', 'new_m_ln_pre_w', 'new_m_w_in', 'new_m_conv_w', 'new_m_a_log_fwd', 'new_m_a_log_bwd', 'new_m_dt_bias_fwd', 'new_m_dt_bias_bwd', 'new_m_gdn_norm_w', 'new_m_w_proj_gdn', 'new_m_gk_w2_fwd', 'new_m_gk_b2_fwd', 'new_m_gk_w2_bwd', 'new_m_gk_b2_bwd', 'new_m_gla_norm_w', 'new_m_w_proj_gla', 'new_m_w_out', 'new_m_ln_post_w', 'new_v_ln_pre_w', 'new_v_w_in', 'new_v_conv_w', 'new_v_a_log_fwd', 'new_v_a_log_bwd', 'new_v_dt_bias_fwd', 'new_v_dt_bias_bwd', 'new_v_gdn_norm_w', 'new_v_w_proj_gdn', 'new_v_gk_w2_fwd', 'new_v_gk_b2_fwd', 'new_v_gk_w2_bwd', 'new_v_gk_b2_bwd', 'new_v_gla_norm_w', 'new_v_w_proj_gla', 'new_v_w_out', 'new_v_ln_post_w']
TWIN_LEAF_KINDS = {'loss': 'loss', 'grad_x': 'grad_x', 'grad_ln_pre_w': 'grad_w', 'grad_w_in': 'grad_w', 'grad_conv_w': 'grad_w', 'grad_a_log_fwd': 'grad_w', 'grad_a_log_bwd': 'grad_w', 'grad_dt_bias_fwd': 'grad_w', 'grad_dt_bias_bwd': 'grad_w', 'grad_gdn_norm_w': 'grad_w', 'grad_w_proj_gdn': 'grad_w', 'grad_gk_w2_fwd': 'grad_w', 'grad_gk_b2_fwd': 'grad_w', 'grad_gk_w2_bwd': 'grad_w', 'grad_gk_b2_bwd': 'grad_w', 'grad_gla_norm_w': 'grad_w', 'grad_w_proj_gla': 'grad_w', 'grad_w_out': 'grad_w', 'grad_ln_post_w': 'grad_w', 'delta_ln_pre_w': 'delta_w', 'delta_w_in': 'delta_w', 'delta_conv_w': 'delta_w', 'delta_a_log_fwd': 'delta_w', 'delta_a_log_bwd': 'delta_w', 'delta_dt_bias_fwd': 'delta_w', 'delta_dt_bias_bwd': 'delta_w', 'delta_gdn_norm_w': 'delta_w', 'delta_w_proj_gdn': 'delta_w', 'delta_gk_w2_fwd': 'delta_w', 'delta_gk_b2_fwd': 'delta_w', 'delta_gk_w2_bwd': 'delta_w', 'delta_gk_b2_bwd': 'delta_w', 'delta_gla_norm_w': 'delta_w', 'delta_w_proj_gla': 'delta_w', 'delta_w_out': 'delta_w', 'delta_ln_post_w': 'delta_w', 'new_m_ln_pre_w': 'new_m', 'new_m_w_in': 'new_m', 'new_m_conv_w': 'new_m', 'new_m_a_log_fwd': 'new_m', 'new_m_a_log_bwd': 'new_m', 'new_m_dt_bias_fwd': 'new_m', 'new_m_dt_bias_bwd': 'new_m', 'new_m_gdn_norm_w': 'new_m', 'new_m_w_proj_gdn': 'new_m', 'new_m_gk_w2_fwd': 'new_m', 'new_m_gk_b2_fwd': 'new_m', 'new_m_gk_w2_bwd': 'new_m', 'new_m_gk_b2_bwd': 'new_m', 'new_m_gla_norm_w': 'new_m', 'new_m_w_proj_gla': 'new_m', 'new_m_w_out': 'new_m', 'new_m_ln_post_w': 'new_m', 'new_v_ln_pre_w': 'new_v', 'new_v_w_in': 'new_v', 'new_v_conv_w': 'new_v', 'new_v_a_log_fwd': 'new_v', 'new_v_a_log_bwd': 'new_v', 'new_v_dt_bias_fwd': 'new_v', 'new_v_dt_bias_bwd': 'new_v', 'new_v_gdn_norm_w': 'new_v', 'new_v_w_proj_gdn': 'new_v', 'new_v_gk_w2_fwd': 'new_v', 'new_v_gk_b2_fwd': 'new_v', 'new_v_gk_w2_bwd': 'new_v', 'new_v_gk_b2_bwd': 'new_v', 'new_v_gla_norm_w': 'new_v', 'new_v_w_proj_gla': 'new_v', 'new_v_w_out': 'new_v', 'new_v_ln_post_w': 'new_v'}


def _forward(args):
    return _fwd_reference(*[args[k] for k in FWD_PARAMS])


def _output_shape():
    out = _jax.eval_shape(lambda: _forward(_fwd_setup_inputs(0)))
    return out.shape, out.dtype

N_MICROBATCH = 1
ADAM_LR = 0.001
ADAM_B1 = 0.9
ADAM_B2 = 0.999
ADAM_EPS = 1e-08
ADAM_WD = 0.01
ADAM_STEP = 10
PER_EXAMPLE_BATCH_AXIS = {'x': 0, 'loss_target': 0}
SHARED_INPUTS = []
_WEIGHT_DTYPES = {'ln_pre_w': _jnp.float32, 'w_in': _jnp.float32, 'conv_w': _jnp.float32, 'a_log_fwd': _jnp.float32, 'a_log_bwd': _jnp.float32, 'dt_bias_fwd': _jnp.float32, 'dt_bias_bwd': _jnp.float32, 'gdn_norm_w': _jnp.float32, 'w_proj_gdn': _jnp.float32, 'gk_w2_fwd': _jnp.float32, 'gk_b2_fwd': _jnp.float32, 'gk_w2_bwd': _jnp.float32, 'gk_b2_bwd': _jnp.float32, 'gla_norm_w': _jnp.float32, 'w_proj_gla': _jnp.float32, 'w_out': _jnp.float32, 'ln_post_w': _jnp.float32}
MOMENT_SCALE = {'ln_pre_w': 3.493498e-01, 'w_in': 1.195147e-01, 'conv_w': 1.026587e-01, 'a_log_fwd': 2.817774e-01, 'a_log_bwd': 3.644772e-01, 'dt_bias_fwd': 2.771013e-01, 'dt_bias_bwd': 3.595305e-01, 'gdn_norm_w': 5.306529e-01, 'w_proj_gdn': 1.922930e-01, 'gk_w2_fwd': 1.330263e-02, 'gk_b2_fwd': 5.304927e-02, 'gk_w2_bwd': 1.407285e-02, 'gk_b2_bwd': 5.822898e-02, 'gla_norm_w': 2.860301e-01, 'w_proj_gla': 1.275718e-01, 'w_out': 2.284182e-01, 'ln_post_w': 1.602214e+01}


def _to_microbatches(a, axis):
    t = _jnp.moveaxis(a, axis, 0)
    t = t.reshape((N_MICROBATCH, t.shape[0] // N_MICROBATCH) + t.shape[1:])
    return _jnp.moveaxis(t, 1, axis + 1)


def setup_inputs(seed: int = 0) -> dict:
    inp = _fwd_setup_inputs(seed)
    key = _jax.random.fold_in(_jax.random.key(seed), 7919)
    shape, _ = _output_shape()
    out = dict(inp)
    out["loss_target"] = _jax.random.normal(_jax.random.fold_in(key, 0), shape, _jnp.float32)
    for i, name in enumerate(TWIN_WEIGHTS):
        w = inp[name].astype(_jnp.float32)
        if MOMENT_SCALE is None:
            s = _jnp.sqrt(_jnp.mean(_jnp.square(w)) + 1e-30)
        else:
            s = MOMENT_SCALE[name]
        km, kv = _jax.random.split(_jax.random.fold_in(key, i + 1))
        out[name] = w
        out["m_" + name] = s * _jax.random.normal(km, w.shape, _jnp.float32)
        out["v_" + name] = (s * s) * _jax.random.uniform(kv, w.shape, _jnp.float32, 0.5, 1.5)
    if N_MICROBATCH > 1:
        for name, axis in PER_EXAMPLE_BATCH_AXIS.items():
            out[name] = _to_microbatches(out[name], axis)
    return {'x': out['x'], 'ln_pre_w': out['ln_pre_w'], 'w_in': out['w_in'], 'conv_w': out['conv_w'], 'a_log_fwd': out['a_log_fwd'], 'a_log_bwd': out['a_log_bwd'], 'dt_bias_fwd': out['dt_bias_fwd'], 'dt_bias_bwd': out['dt_bias_bwd'], 'gdn_norm_w': out['gdn_norm_w'], 'w_proj_gdn': out['w_proj_gdn'], 'gk_w2_fwd': out['gk_w2_fwd'], 'gk_b2_fwd': out['gk_b2_fwd'], 'gk_w2_bwd': out['gk_w2_bwd'], 'gk_b2_bwd': out['gk_b2_bwd'], 'gla_norm_w': out['gla_norm_w'], 'w_proj_gla': out['w_proj_gla'], 'w_out': out['w_out'], 'ln_post_w': out['ln_post_w'], 'loss_target': out['loss_target'], 'm_ln_pre_w': out['m_ln_pre_w'], 'm_w_in': out['m_w_in'], 'm_conv_w': out['m_conv_w'], 'm_a_log_fwd': out['m_a_log_fwd'], 'm_a_log_bwd': out['m_a_log_bwd'], 'm_dt_bias_fwd': out['m_dt_bias_fwd'], 'm_dt_bias_bwd': out['m_dt_bias_bwd'], 'm_gdn_norm_w': out['m_gdn_norm_w'], 'm_w_proj_gdn': out['m_w_proj_gdn'], 'm_gk_w2_fwd': out['m_gk_w2_fwd'], 'm_gk_b2_fwd': out['m_gk_b2_fwd'], 'm_gk_w2_bwd': out['m_gk_w2_bwd'], 'm_gk_b2_bwd': out['m_gk_b2_bwd'], 'm_gla_norm_w': out['m_gla_norm_w'], 'm_w_proj_gla': out['m_w_proj_gla'], 'm_w_out': out['m_w_out'], 'm_ln_post_w': out['m_ln_post_w'], 'v_ln_pre_w': out['v_ln_pre_w'], 'v_w_in': out['v_w_in'], 'v_conv_w': out['v_conv_w'], 'v_a_log_fwd': out['v_a_log_fwd'], 'v_a_log_bwd': out['v_a_log_bwd'], 'v_dt_bias_fwd': out['v_dt_bias_fwd'], 'v_dt_bias_bwd': out['v_dt_bias_bwd'], 'v_gdn_norm_w': out['v_gdn_norm_w'], 'v_w_proj_gdn': out['v_w_proj_gdn'], 'v_gk_w2_fwd': out['v_gk_w2_fwd'], 'v_gk_b2_fwd': out['v_gk_b2_fwd'], 'v_gk_w2_bwd': out['v_gk_w2_bwd'], 'v_gk_b2_bwd': out['v_gk_b2_bwd'], 'v_gla_norm_w': out['v_gla_norm_w'], 'v_w_proj_gla': out['v_w_proj_gla'], 'v_w_out': out['v_w_out'], 'v_ln_post_w': out['v_ln_post_w']}


def _loss(weights, diff, rest, loss_target):
    with _jax.named_scope("forward"):
        args = {**rest, TWIN_DIFF_INPUT: diff, **{k: w.astype(_WEIGHT_DTYPES[k]) for k, w in weights.items()}}
        y = _forward(args)
    with _jax.named_scope("loss_head"):
        err = _jnp.square(y.astype(_jnp.float32) - loss_target)
        return 0.5 * _jnp.sum(_jnp.mean(err, axis=-1)) if err.ndim else 0.5 * err


def _adamw(w, g, m, v):
    m = ADAM_B1 * m + (1.0 - ADAM_B1) * g
    v = ADAM_B2 * v + (1.0 - ADAM_B2) * _jnp.square(g)
    m_hat = m / (1.0 - ADAM_B1 ** ADAM_STEP)
    v_hat = v / (1.0 - ADAM_B2 ** ADAM_STEP)
    delta = -ADAM_LR * (m_hat / (_jnp.sqrt(v_hat) + ADAM_EPS) + ADAM_WD * w)
    return delta, m, v


def reference(x, ln_pre_w, w_in, conv_w, a_log_fwd, a_log_bwd, dt_bias_fwd, dt_bias_bwd, gdn_norm_w, w_proj_gdn, gk_w2_fwd, gk_b2_fwd, gk_w2_bwd, gk_b2_bwd, gla_norm_w, w_proj_gla, w_out, ln_post_w, loss_target, m_ln_pre_w, m_w_in, m_conv_w, m_a_log_fwd, m_a_log_bwd, m_dt_bias_fwd, m_dt_bias_bwd, m_gdn_norm_w, m_w_proj_gdn, m_gk_w2_fwd, m_gk_b2_fwd, m_gk_w2_bwd, m_gk_b2_bwd, m_gla_norm_w, m_w_proj_gla, m_w_out, m_ln_post_w, v_ln_pre_w, v_w_in, v_conv_w, v_a_log_fwd, v_a_log_bwd, v_dt_bias_fwd, v_dt_bias_bwd, v_gdn_norm_w, v_w_proj_gdn, v_gk_w2_fwd, v_gk_b2_fwd, v_gk_w2_bwd, v_gk_b2_bwd, v_gla_norm_w, v_w_proj_gla, v_w_out, v_ln_post_w):
    given = dict(x=x, ln_pre_w=ln_pre_w, w_in=w_in, conv_w=conv_w, a_log_fwd=a_log_fwd, a_log_bwd=a_log_bwd, dt_bias_fwd=dt_bias_fwd, dt_bias_bwd=dt_bias_bwd, gdn_norm_w=gdn_norm_w, w_proj_gdn=w_proj_gdn, gk_w2_fwd=gk_w2_fwd, gk_b2_fwd=gk_b2_fwd, gk_w2_bwd=gk_w2_bwd, gk_b2_bwd=gk_b2_bwd, gla_norm_w=gla_norm_w, w_proj_gla=w_proj_gla, w_out=w_out, ln_post_w=ln_post_w, loss_target=loss_target, m_ln_pre_w=m_ln_pre_w, m_w_in=m_w_in, m_conv_w=m_conv_w, m_a_log_fwd=m_a_log_fwd, m_a_log_bwd=m_a_log_bwd, m_dt_bias_fwd=m_dt_bias_fwd, m_dt_bias_bwd=m_dt_bias_bwd, m_gdn_norm_w=m_gdn_norm_w, m_w_proj_gdn=m_w_proj_gdn, m_gk_w2_fwd=m_gk_w2_fwd, m_gk_b2_fwd=m_gk_b2_fwd, m_gk_w2_bwd=m_gk_w2_bwd, m_gk_b2_bwd=m_gk_b2_bwd, m_gla_norm_w=m_gla_norm_w, m_w_proj_gla=m_w_proj_gla, m_w_out=m_w_out, m_ln_post_w=m_ln_post_w, v_ln_pre_w=v_ln_pre_w, v_w_in=v_w_in, v_conv_w=v_conv_w, v_a_log_fwd=v_a_log_fwd, v_a_log_bwd=v_a_log_bwd, v_dt_bias_fwd=v_dt_bias_fwd, v_dt_bias_bwd=v_dt_bias_bwd, v_gdn_norm_w=v_gdn_norm_w, v_w_proj_gdn=v_w_proj_gdn, v_gk_w2_fwd=v_gk_w2_fwd, v_gk_b2_fwd=v_gk_b2_fwd, v_gk_w2_bwd=v_gk_w2_bwd, v_gk_b2_bwd=v_gk_b2_bwd, v_gla_norm_w=v_gla_norm_w, v_w_proj_gla=v_w_proj_gla, v_w_out=v_w_out, v_ln_post_w=v_ln_post_w)
    weights = {n: given[n] for n in TWIN_WEIGHTS}
    shared = {n: given[n] for n in SHARED_INPUTS}
    per_example = {n: given[n] for n in ['x']}
    grad_fn = _jax.value_and_grad(_loss, argnums=(0, 1))

    def one_microbatch(ex, loss_target):
        ex = dict(ex)
        diff = ex.pop(TWIN_DIFF_INPUT)
        return grad_fn(weights, diff, {**shared, **ex}, loss_target)

    if N_MICROBATCH == 1:
        loss, (grad_w, grad_x) = one_microbatch(per_example, given["loss_target"])
    else:
        def body(carry, xs):
            loss_sum, grad_sum = carry
            l_k, (gw_k, gx_k) = one_microbatch(xs[0], xs[1])
            with _jax.named_scope("update"):
                return (loss_sum + l_k, _jax.tree.map(_jnp.add, grad_sum, gw_k)), gx_k

        init = (_jnp.zeros((), _jnp.float32), _jax.tree.map(_jnp.zeros_like, weights))
        (loss, grad_w), grad_x = _jax.lax.scan(body, init, (per_example, given["loss_target"]))
    with _jax.named_scope("update"):
        delta_w, new_m, new_v = {}, {}, {}
        for n in TWIN_WEIGHTS:
            delta_w[n], new_m[n], new_v[n] = _adamw(weights[n], grad_w[n], given["m_" + n], given["v_" + n])
    return (loss, grad_x, *[grad_w[n] for n in TWIN_WEIGHTS], *[delta_w[n] for n in TWIN_WEIGHTS],
            *[new_m[n] for n in TWIN_WEIGHTS], *[new_v[n] for n in TWIN_WEIGHTS])
```

```python
import functools

import jax
import jax.numpy as jnp
from jax import lax
from jax.experimental import pallas as pl
from jax.experimental.pallas import tpu as pltpu

F32 = jnp.float32
BF16 = jnp.bfloat16
HI = lax.Precision.HIGHEST
MESH = pl.DeviceIdType.MESH

D = 1024
CH = 64
EPS = 1e-6
NA, DA = 8, 128
NB, DKB, DVB = 4, 128, 256
NSHARD = 4
SHW = 2320
NPERM = 9728
PS_BLOCK = 72
LR, B1, B2, ADAM_EPS, WD, STEP = 0.001, 0.9, 0.999, 1e-08, 0.01, 10

ANY = pl.BlockSpec(memory_space=pl.ANY)


def _cparams(sem=None, vmem_mb=48):
    return pltpu.CompilerParams(dimension_semantics=sem, vmem_limit_bytes=vmem_mb << 20)


def _bdot(a, b, ca, cb):
    return lax.dot_general(a.astype(BF16), b.astype(BF16), (((ca,), (cb,)), ((), ())),
                           preferred_element_type=F32)


@jax.custom_vjp
def mm(a, b):
    return _bdot(a, b, 1, 0)


def _mm_fwd(a, b):
    return _bdot(a, b, 1, 0), (a, b)


def _mm_bwd(res, g):
    a, b = res
    return _bdot(g, b, 1, 1), _bdot(a, g, 0, 0)


mm.defvjp(_mm_fwd, _mm_bwd)


@jax.custom_vjp
def mm_nt(a, b):
    return _bdot(a, b, 1, 1)


def _mm_nt_fwd(a, b):
    return _bdot(a, b, 1, 1), (a, b)


def _mm_nt_bwd(res, g):
    a, b = res
    return _bdot(g, b, 1, 0), _bdot(g, a, 0, 0)


mm_nt.defvjp(_mm_nt_fwd, _mm_nt_bwd)


@jax.custom_vjp
def mm_tn(a, b):
    return _bdot(a, b, 0, 0)


def _mm_tn_fwd(a, b):
    return _bdot(a, b, 0, 0), (a, b)


def _mm_tn_bwd(res, g):
    a, b = res
    return _bdot(b, g, 1, 1), _bdot(a, g, 1, 0)


mm_tn.defvjp(_mm_tn_fwd, _mm_tn_bwd)


def dot_hi(a, b):
    return lax.dot_general(a, b, (((1,), (0,)), ((), ())), precision=HI, preferred_element_type=F32)


def _sigmoid(x):
    return 1.0 / (1.0 + jnp.exp(-x))


def _silu(x):
    return x * _sigmoid(x)


def _softplus(x):
    return jnp.maximum(x, 0.0) + jnp.log(1.0 + jnp.exp(-jnp.abs(x)))


def _rms(x, w):
    return x * lax.rsqrt(jnp.mean(x * x, axis=-1, keepdims=True) + EPS) * w


class _Consts:
    def __init__(self, rev):
        r = lax.broadcasted_iota(jnp.int32, (CH, CH), 0)
        c = lax.broadcasted_iota(jnp.int32, (CH, CH), 1)
        a = jnp.where(rev, c, r)
        b = jnp.where(rev, r, c)
        self.incl = a >= b
        self.strict = a > b
        self.incl_f = self.incl.astype(F32)
        self.eye = (r == c).astype(F32)
        self.ones = jnp.ones((CH, CH), F32)
        rows = lax.broadcasted_iota(jnp.int32, (CH, 1), 0)
        self.last_sel = (rows == jnp.where(rev, 0, CH - 1)).astype(F32)


def _onehot_rows(idx, ncols):
    r = lax.broadcasted_iota(jnp.int32, (128, ncols), 0)
    return (r == idx).astype(F32)


def _tri_inv(low, eye):
    n = -low
    acc = eye + n
    p = n
    for _ in range(5):
        p = dot_hi(p, p)
        acc = acc + dot_hi(acc, p)
    return acc


def _gdn_intra(q, k, v, gsm, e_lg, e_lg64, e_b, cs):
    lgx = dot_hi(gsm, e_lg)
    bx = dot_hi(gsm, e_b)
    g = dot_hi(cs.incl_f, lgx)
    g64 = dot_hi(cs.incl_f, dot_hi(gsm, e_lg64))
    grow = dot_hi(cs.ones, cs.eye * g64)
    decay = jnp.where(cs.incl, jnp.exp(jnp.where(cs.incl, g64 - grow, 0.0)), 0.0)
    kb = k * bx
    low = jnp.where(cs.strict, mm_nt(kb, k) * decay, 0.0)
    eg = jnp.exp(g)
    tinv = _tri_inv(low, cs.eye)
    u = dot_hi(tinv, v * bx)
    w = dot_hi(tinv, kb * eg)
    attn = mm_nt(q, k) * decay
    qd = q * eg
    glast = jnp.sum(g * cs.last_sel, axis=0, keepdims=True)
    kd = k * jnp.exp(glast - g)
    return u, w, attn, qd, kd, jnp.exp(glast)


def _gdn_scan(u, w, attn, qd, kd, el, s):
    vn = u - mm(w, s)
    o = mm(qd, s) + mm(attn, vn)
    sn = s * el + mm_tn(kd, vn)
    return o, sn


def _gla_intra(q, k, v, gk, cs):
    gc = dot_hi(cs.incl_f, gk)
    qg = q * (DKB ** -0.5) * jnp.exp(gc)
    kg = k * jnp.exp(-gc)
    attn = jnp.where(cs.incl, mm_nt(qg, kg), 0.0)
    intra = mm(attn, v)
    glast = jnp.sum(gc * cs.last_sel, axis=0, keepdims=True)
    kd = k * jnp.exp(glast - gc)
    return qg, kd, intra, jnp.exp(glast)


def _gla_scan(qg, kd, v, el, st):
    o = mm_nt(qg, st)
    stn = st * el + mm_tn(v, kd)
    return o, stn


def _shift_rows(x, s):
    if s == 0:
        return x
    t = x.shape[0]
    rolled = pltpu.roll(x, (-s) % t, 0)
    rows = lax.broadcasted_iota(jnp.int32, x.shape, 0)
    return jnp.where((rows + s >= 0) & (rows + s < t), rolled, 0.0)


@jax.custom_vjp
def _conv5(x, w):
    acc = w[0:1] * _shift_rows(x, -2)
    for j in range(1, 5):
        acc = acc + w[j:j + 1] * _shift_rows(x, j - 2)
    return acc


def _conv5_fwd(x, w):
    return _conv5(x, w), (x, w)


def _conv5_bwd(res, g):
    x, w = res
    dx = w[0:1] * _shift_rows(g, 2)
    for j in range(1, 5):
        dx = dx + w[j:j + 1] * _shift_rows(g, 2 - j)
    rows = lax.broadcasted_iota(jnp.int32, w.shape, 0)
    dw = jnp.zeros_like(w)
    for j in range(5):
        dwj = jnp.sum(g * _shift_rows(x, j - 2), axis=0, keepdims=True)
        dw = dw + jnp.where(rows == j, dwj, 0.0)
    return dx, dw


_conv5.defvjp(_conv5_fwd, _conv5_bwd)


def _qkv_act(kind):
    def f(x, w):
        c = _silu(_conv5(x, w))
        if kind == 2:
            return c
        c = c * lax.rsqrt(jnp.sum(c * c, axis=-1, keepdims=True) + EPS)
        return c * (DA ** -0.5) if kind == 0 else c
    return f


def _inproj(x, lnw, wperm, tm=512, tn=512):
    t = x.shape[0]

    def body(x_ref, lnw_ref, w_ref, p_ref, h_ref, hbuf):
        @pl.when(pl.program_id(1) == 0)
        def _():
            hb = _rms(x_ref[...], lnw_ref[...]).astype(BF16)
            hbuf[...] = hb
            h_ref[...] = hb
        p_ref[...] = jnp.dot(hbuf[...], w_ref[...], preferred_element_type=F32)

    return pl.pallas_call(
        body, name="inproj", grid=(t // tm, NPERM // tn),
        in_specs=[pl.BlockSpec((tm, D), lambda i, j: (i, 0)),
                  pl.BlockSpec((1, D), lambda i, j: (0, 0)),
                  pl.BlockSpec((D, tn), lambda i, j: (0, j))],
        out_specs=[pl.BlockSpec((tm, tn), lambda i, j: (i, j)),
                   pl.BlockSpec((tm, D), lambda i, j: (i, 0))],
        out_shape=[jax.ShapeDtypeStruct((t, NPERM), F32),
                   jax.ShapeDtypeStruct((t, D), BF16)],
        scratch_shapes=[pltpu.VMEM((tm, D), BF16)],
        compiler_params=_cparams(("parallel", "arbitrary")),
    )(x, lnw, wperm)


def _inproj_dw(h, dp, tm=512, tn=512):
    t = h.shape[0]
    ncol = dp.shape[1]

    def body(h_ref, dp_ref, dw_ref):
        @pl.when(pl.program_id(1) == 0)
        def _():
            dw_ref[...] = jnp.zeros_like(dw_ref)
        dw_ref[...] += _bdot(h_ref[...], dp_ref[...], 0, 0)

    return pl.pallas_call(
        body, name="inproj_dw", grid=(ncol // tn, t // tm),
        in_specs=[pl.BlockSpec((tm, D), lambda j, i: (i, 0)),
                  pl.BlockSpec((tm, tn), lambda j, i: (i, j))],
        out_specs=pl.BlockSpec((D, tn), lambda j, i: (0, j)),
        out_shape=jax.ShapeDtypeStruct((D, ncol), F32),
        compiler_params=_cparams(("parallel", "arbitrary")),
    )(h, dp)


def _inproj_dx(dp, wperm, x, lnw, dyres, tm=256, tn=512):
    t = x.shape[0]
    ncol = dp.shape[1]
    nj = ncol // tn

    def body(dp_ref, w_ref, x_ref, lnw_ref, dy_ref, dx_ref, dlnw_ref, acc):
        j = pl.program_id(1)

        @pl.when(j == 0)
        def _():
            acc[...] = jnp.zeros_like(acc)

        acc[...] += _bdot(dp_ref[...], w_ref[...], 1, 1)

        @pl.when(j == nj - 1)
        def _():
            _, vjp = jax.vjp(_rms, x_ref[...], lnw_ref[...])
            dx, dlnw = vjp(acc[...])
            dx_ref[...] = dx + dy_ref[...]

            @pl.when(pl.program_id(0) == 0)
            def _():
                dlnw_ref[...] = jnp.zeros_like(dlnw_ref)
            dlnw_ref[...] += jnp.broadcast_to(dlnw, dlnw_ref.shape)

    return pl.pallas_call(
        body, name="inproj_dx", grid=(t // tm, nj),
        in_specs=[pl.BlockSpec((tm, tn), lambda i, j: (i, j)),
                  pl.BlockSpec((D, tn), lambda i, j: (0, j)),
                  pl.BlockSpec((tm, D), lambda i, j: (i, 0)),
                  pl.BlockSpec((1, D), lambda i, j: (0, 0)),
                  pl.BlockSpec((tm, D), lambda i, j: (i, 0))],
        out_specs=[pl.BlockSpec((tm, D), lambda i, j: (i, 0)),
                   pl.BlockSpec((8, D), lambda i, j: (0, 0))],
        out_shape=[jax.ShapeDtypeStruct((t, D), F32), jax.ShapeDtypeStruct((8, D), F32)],
        scratch_shapes=[pltpu.VMEM((tm, D), F32)],
        compiler_params=_cparams(("arbitrary", "arbitrary")),
    )(dp, wperm, x, lnw, dyres)


def _qkv_fwd(p, convw, kind):
    t = p.shape[0]
    f = _qkv_act(kind)

    def body(p_ref, w_ref, o_ref):
        o_ref[...] = f(p_ref[...], w_ref[...])

    return pl.pallas_call(
        body, name=f"qkv_fwd{kind}", grid=(NA,),
        in_specs=[pl.BlockSpec((t, DA), lambda h: (0, kind * NA + h)),
                  pl.BlockSpec((8, DA), lambda h: (0, kind * NA + h))],
        out_specs=pl.BlockSpec((t, DA), lambda h: (0, h)),
        out_shape=jax.ShapeDtypeStruct((t, D), F32),
        compiler_params=_cparams(("parallel",)),
    )(p, convw)


def _qkv_bwd(p, convw, dout, kind):
    t = p.shape[0]
    f = _qkv_act(kind)

    def body(p_ref, w_ref, g_ref, dx_ref, dw_ref):
        _, vjp = jax.vjp(f, p_ref[...], w_ref[...])
        dx, dw = vjp(g_ref[...])
        dx_ref[...] = dx
        dw_ref[...] = dw

    return pl.pallas_call(
        body, name=f"qkv_bwd{kind}", grid=(NA,),
        in_specs=[pl.BlockSpec((t, DA), lambda h: (0, kind * NA + h)),
                  pl.BlockSpec((8, DA), lambda h: (0, kind * NA + h)),
                  pl.BlockSpec((t, DA), lambda h: (0, h))],
        out_specs=[pl.BlockSpec((t, DA), lambda h: (0, h)),
                   pl.BlockSpec((8, DA), lambda h: (0, h))],
        out_shape=[jax.ShapeDtypeStruct((t, D), F32), jax.ShapeDtypeStruct((8, D), F32)],
        compiler_params=_cparams(("parallel",)),
    )(p, convw, dout)


def _gates_f(ps, alog_row, dt_row, w2f, b2f, w2b, b2b):
    lane = lax.broadcasted_iota(jnp.int32, ps.shape, 1)
    lg = -jnp.exp(alog_row) * _softplus(ps + dt_row)
    gsm = jnp.where(lane < 16, lg, jnp.where(lane < 32, _sigmoid(ps), 0.0))
    gkf = -_softplus(-(mm(ps, w2f) + b2f)) * (1.0 / 16.0)
    gkb = -_softplus(-(mm(ps, w2b) + b2b)) * (1.0 / 16.0)
    return gsm, gkf, gkb


def _gates_fwd(ps, alog_row, dt_row, w2f, b2f, w2b, b2b, tm=512):
    t = ps.shape[0]

    def body(ps_ref, a_ref, d_ref, wf_ref, bf_ref, wb_ref, bb_ref, gsm_ref, gk_ref):
        gsm, gkf, gkb = _gates_f(ps_ref[...], a_ref[...], d_ref[...], wf_ref[...], bf_ref[...],
                                 wb_ref[...], bb_ref[...])
        gsm_ref[...] = gsm
        gk_ref[0] = gkf
        gk_ref[1] = gkb

    row = lambda n: pl.BlockSpec((1, n), lambda i: (0, 0))
    mat = pl.BlockSpec((128, 512), lambda i: (0, 0))
    return pl.pallas_call(
        body, name="gates_fwd", grid=(t // tm,),
        in_specs=[pl.BlockSpec((tm, 128), lambda i: (i, PS_BLOCK)), row(128), row(128), mat, row(512), mat, row(512)],
        out_specs=[pl.BlockSpec((tm, 128), lambda i: (i, 0)),
                   pl.BlockSpec((2, tm, 512), lambda i: (0, i, 0))],
        out_shape=[jax.ShapeDtypeStruct((t, 128), F32), jax.ShapeDtypeStruct((2, t, 512), F32)],
        compiler_params=_cparams(("parallel",)),
    )(ps, alog_row, dt_row, w2f, b2f, w2b, b2b)


def _gates_bwd(ps, alog_row, dt_row, w2f, b2f, w2b, b2b, dgsm, dgk, tm=512):
    t = ps.shape[0]

    def body(ps_ref, a_ref, d_ref, wf_ref, bf_ref, wb_ref, bb_ref, dgsm_ref, dgk_ref,
             dps_ref, da_ref, dd_ref, dwf_ref, dbf_ref, dwb_ref, dbb_ref):
        _, vjp = jax.vjp(_gates_f, ps_ref[...], a_ref[...], d_ref[...], wf_ref[...], bf_ref[...],
                         wb_ref[...], bb_ref[...])
        dps, da, dd, dwf, dbf, dwb, dbb = vjp((dgsm_ref[...], dgk_ref[0], dgk_ref[1]))
        dps_ref[...] = dps
        accs = ((da_ref, da), (dd_ref, dd), (dwf_ref, dwf), (dbf_ref, dbf), (dwb_ref, dwb), (dbb_ref, dbb))

        @pl.when(pl.program_id(0) == 0)
        def _():
            for ref, _ in accs:
                ref[...] = jnp.zeros_like(ref)
        for ref, val in accs:
            ref[...] += jnp.broadcast_to(val, ref.shape)

    row = lambda n: pl.BlockSpec((1, n), lambda i: (0, 0))
    row8 = lambda n: pl.BlockSpec((8, n), lambda i: (0, 0))
    mat = pl.BlockSpec((128, 512), lambda i: (0, 0))
    return pl.pallas_call(
        body, name="gates_bwd", grid=(t // tm,),
        in_specs=[pl.BlockSpec((tm, 128), lambda i: (i, PS_BLOCK)), row(128), row(128), mat, row(512), mat, row(512),
                  pl.BlockSpec((tm, 128), lambda i: (i, 0)),
                  pl.BlockSpec((2, tm, 512), lambda i: (0, i, 0))],
        out_specs=[pl.BlockSpec((tm, 128), lambda i: (i, 0)), row8(128), row8(128), mat, row8(512), mat, row8(512)],
        out_shape=[jax.ShapeDtypeStruct((t, 128), F32),
                   jax.ShapeDtypeStruct((8, 128), F32), jax.ShapeDtypeStruct((8, 128), F32),
                   jax.ShapeDtypeStruct((128, 512), F32), jax.ShapeDtypeStruct((8, 512), F32),
                   jax.ShapeDtypeStruct((128, 512), F32), jax.ShapeDtypeStruct((8, 512), F32)],
        compiler_params=_cparams(("arbitrary",)),
    )(ps, alog_row, dt_row, w2f, b2f, w2b, b2b, dgsm, dgk)


def _rows(i):
    return pl.ds(pl.multiple_of(i * CH, CH), CH)


def _gdn_heads(d, h):
    return (_onehot_rows(8 * d + h, 128), _onehot_rows(8 * d + h, CH), _onehot_rows(16 + 8 * d + h, 128))


def _gdn_intra_fwd(qn, kn, vc, gsm):
    t = qn.shape[0]
    n = t // CH

    def body(q_ref, k_ref, v_ref, g_ref, u_ref, w_ref, a_ref, qd_ref, kd_ref, e_ref):
        d, h = pl.program_id(0), pl.program_id(1)
        cs = _Consts(d == 1)
        e_lg, e_lg64, e_b = _gdn_heads(d, h)

        def step(i, carry):
            r = _rows(i)
            u, w, a, qd, kd, el = _gdn_intra(q_ref[r, :], k_ref[r, :], v_ref[r, :], g_ref[r, :],
                                             e_lg, e_lg64, e_b, cs)
            u_ref[r, :] = u
            w_ref[r, :] = w
            a_ref[r, :] = a
            qd_ref[r, :] = qd
            kd_ref[r, :] = kd
            e_ref[i] = jnp.broadcast_to(el, (8, 128))
            return carry

        lax.fori_loop(0, n, step, 0)

    head = pl.BlockSpec((t, DA), lambda d, h: (0, h))
    dh = pl.BlockSpec((None, t, DA), lambda d, h: (d, 0, h))
    big = jax.ShapeDtypeStruct((2, t, D), F32)
    return pl.pallas_call(
        body, name="gdn_intra_fwd", grid=(2, NA),
        in_specs=[head, head, head, pl.BlockSpec((t, 128), lambda d, h: (0, 0))],
        out_specs=[dh, dh, pl.BlockSpec((None, None, t, CH), lambda d, h: (d, h, 0, 0)), dh, dh,
                   pl.BlockSpec((None, None, n, 8, 128), lambda d, h: (d, h, 0, 0, 0))],
        out_shape=[big, big, jax.ShapeDtypeStruct((2, NA, t, CH), F32), big, big,
                   jax.ShapeDtypeStruct((2, NA, n, 8, 128), F32)],
        compiler_params=_cparams(("parallel", "parallel")),
    )(qn, kn, vc, gsm)


def _gdn_scan_fwd(u, w, a, qd, kd, e):
    t = u.shape[1]
    n = t // CH

    def body(u_ref, w_ref, a_ref, qd_ref, kd_ref, e_ref, o_ref, s_ref):
        rev = pl.program_id(0) == 1

        def step(i, s):
            ci = jnp.where(rev, n - 1 - i, i)
            r = _rows(ci)
            s_ref[ci] = s
            o, sn = _gdn_scan(u_ref[r, :], w_ref[r, :], a_ref[r, :], qd_ref[r, :], kd_ref[r, :],
                              e_ref[ci][0:1], s)
            o_ref[r, :] = o
            return sn

        lax.fori_loop(0, n, step, jnp.zeros((DA, DA), F32))

    dh = pl.BlockSpec((None, t, DA), lambda d, h: (d, 0, h))
    return pl.pallas_call(
        body, name="gdn_scan_fwd", grid=(2, NA),
        in_specs=[dh, dh, pl.BlockSpec((None, None, t, CH), lambda d, h: (d, h, 0, 0)), dh, dh,
                  pl.BlockSpec((None, None, n, 8, 128), lambda d, h: (d, h, 0, 0, 0))],
        out_specs=[dh, pl.BlockSpec((None, None, n, DA, DA), lambda d, h: (d, h, 0, 0, 0))],
        out_shape=[jax.ShapeDtypeStruct((2, t, D), F32), jax.ShapeDtypeStruct((2, NA, n, DA, DA), F32)],
        compiler_params=_cparams(("parallel", "parallel")),
    )(u, w, a, qd, kd, e)


def _gdn_scan_bwd(u, w, a, qd, kd, e, ssave, do):
    t = u.shape[1]
    n = t // CH

    def body(u_ref, w_ref, a_ref, qd_ref, kd_ref, e_ref, s_ref, do_ref,
             du_ref, dw_ref, da_ref, dqd_ref, dkd_ref, de_ref):
        rev = pl.program_id(0) == 1

        def step(i, ds):
            ci = jnp.where(rev, i, n - 1 - i)
            r = _rows(ci)
            _, vjp = jax.vjp(_gdn_scan, u_ref[r, :], w_ref[r, :], a_ref[r, :], qd_ref[r, :], kd_ref[r, :],
                             e_ref[ci][0:1], s_ref[ci])
            du, dw, da, dqd, dkd, de, dsn = vjp((do_ref[r, :], ds))
            du_ref[r, :] = du
            dw_ref[r, :] = dw
            da_ref[r, :] = da
            dqd_ref[r, :] = dqd
            dkd_ref[r, :] = dkd
            de_ref[ci] = jnp.broadcast_to(de, (8, 128))
            return dsn

        lax.fori_loop(0, n, step, jnp.zeros((DA, DA), F32))

    dh = pl.BlockSpec((None, t, DA), lambda d, h: (d, 0, h))
    at = pl.BlockSpec((None, None, t, CH), lambda d, h: (d, h, 0, 0))
    et = pl.BlockSpec((None, None, n, 8, 128), lambda d, h: (d, h, 0, 0, 0))
    big = jax.ShapeDtypeStruct((2, t, D), F32)
    return pl.pallas_call(
        body, name="gdn_scan_bwd", grid=(2, NA),
        in_specs=[dh, dh, at, dh, dh, et,
                  pl.BlockSpec((None, None, n, DA, DA), lambda d, h: (d, h, 0, 0, 0)),
                  pl.BlockSpec((t, DA), lambda d, h: (0, h))],
        out_specs=[dh, dh, at, dh, dh, et],
        out_shape=[big, big, jax.ShapeDtypeStruct((2, NA, t, CH), F32), big, big,
                   jax.ShapeDtypeStruct((2, NA, n, 8, 128), F32)],
        compiler_params=_cparams(("parallel", "parallel")),
    )(u, w, a, qd, kd, e, ssave, do)


def _gdn_intra_bwd(qn, kn, vc, gsm, du, dw, da, dqd, dkd, de):
    t = qn.shape[0]
    n = t // CH

    def body(q_ref, k_ref, v_ref, g_ref, du_ref, dw_ref, da_ref, dqd_ref, dkd_ref, de_ref,
             dq_ref, dk_ref, dv_ref, dg_ref):
        h, d = pl.program_id(0), pl.program_id(1)
        cs = _Consts(d == 1)
        e_lg, e_lg64, e_b = _gdn_heads(d, h)

        @pl.when((h == 0) & (d == 0))
        def _():
            dg_ref[...] = jnp.zeros_like(dg_ref)

        @pl.when(d == 0)
        def _():
            dq_ref[...] = jnp.zeros_like(dq_ref)
            dk_ref[...] = jnp.zeros_like(dk_ref)
            dv_ref[...] = jnp.zeros_like(dv_ref)

        def step(i, carry):
            r = _rows(i)
            f = lambda q, k, v, g: _gdn_intra(q, k, v, g, e_lg, e_lg64, e_b, cs)
            _, vjp = jax.vjp(f, q_ref[r, :], k_ref[r, :], v_ref[r, :], g_ref[r, :])
            dq, dk, dv, dg = vjp((du_ref[r, :], dw_ref[r, :], da_ref[r, :], dqd_ref[r, :], dkd_ref[r, :],
                                  de_ref[i][0:1]))
            dq_ref[r, :] += dq
            dk_ref[r, :] += dk
            dv_ref[r, :] += dv
            dg_ref[r, :] += dg
            return carry

        lax.fori_loop(0, n, step, 0)

    head = pl.BlockSpec((t, DA), lambda h, d: (0, h))
    dh = pl.BlockSpec((None, t, DA), lambda h, d: (d, 0, h))
    full = jax.ShapeDtypeStruct((t, D), F32)
    return pl.pallas_call(
        body, name="gdn_intra_bwd", grid=(NA, 2),
        in_specs=[head, head, head, pl.BlockSpec((t, 128), lambda h, d: (0, 0)),
                  dh, dh, pl.BlockSpec((None, None, t, CH), lambda h, d: (d, h, 0, 0)), dh, dh,
                  pl.BlockSpec((None, None, n, 8, 128), lambda h, d: (d, h, 0, 0, 0))],
        out_specs=[head, head, head, pl.BlockSpec((t, 128), lambda h, d: (0, 0))],
        out_shape=[full, full, full, jax.ShapeDtypeStruct((t, 128), F32)],
        compiler_params=_cparams(("arbitrary", "arbitrary")),
    )(qn, kn, vc, gsm, du, dw, da, dqd, dkd, de)


def _gla_specs(t, order):
    ix = (lambda d, h: (d, h)) if order == "dh" else (lambda h, d: (d, h))

    def mk(fn):
        return lambda a, b: fn(*ix(a, b))
    q = pl.BlockSpec((t, DKB), mk(lambda d, h: (0, 32 + h)))
    k = pl.BlockSpec((t, DKB), mk(lambda d, h: (0, 36 + h)))
    v = pl.BlockSpec((t, DVB), mk(lambda d, h: (0, 20 + h)))
    dk = pl.BlockSpec((None, t, DKB), mk(lambda d, h: (d, 0, h)))
    dv = pl.BlockSpec((None, t, DVB), mk(lambda d, h: (d, 0, h)))
    e = pl.BlockSpec((None, None, t // CH, 8, 128), mk(lambda d, h: (d, h, 0, 0, 0)))
    s = pl.BlockSpec((None, None, t // CH, DVB, DKB), mk(lambda d, h: (d, h, 0, 0, 0)))
    return q, k, v, dk, dv, e, s


def _gla_intra_fwd(p, gk):
    t = p.shape[0]
    n = t // CH

    def body(q_ref, k_ref, v_ref, g_ref, qg_ref, kd_ref, in_ref, e_ref):
        cs = _Consts(pl.program_id(0) == 1)

        def step(i, carry):
            r = _rows(i)
            qg, kd, intra, el = _gla_intra(q_ref[r, :], k_ref[r, :], v_ref[r, :], g_ref[r, :], cs)
            qg_ref[r, :] = qg
            kd_ref[r, :] = kd
            in_ref[r, :] = intra
            e_ref[i] = jnp.broadcast_to(el, (8, 128))
            return carry

        lax.fori_loop(0, n, step, 0)

    q, k, v, dk, dv, e, _ = _gla_specs(t, "dh")
    return pl.pallas_call(
        body, name="gla_intra_fwd", grid=(2, NB),
        in_specs=[q, k, v, dk], out_specs=[dk, dk, dv, e],
        out_shape=[jax.ShapeDtypeStruct((2, t, NB * DKB), F32), jax.ShapeDtypeStruct((2, t, NB * DKB), F32),
                   jax.ShapeDtypeStruct((2, t, D), F32), jax.ShapeDtypeStruct((2, NB, n, 8, 128), F32)],
        compiler_params=_cparams(("parallel", "parallel")),
    )(p, p, p, gk)


def _gla_scan_fwd(p, qg, kd, intra, e):
    t = p.shape[0]
    n = t // CH

    def body(v_ref, qg_ref, kd_ref, in_ref, e_ref, o_ref, s_ref):
        rev = pl.program_id(0) == 1

        def step(i, st):
            ci = jnp.where(rev, n - 1 - i, i)
            r = _rows(ci)
            s_ref[ci] = st
            o, stn = _gla_scan(qg_ref[r, :], kd_ref[r, :], v_ref[r, :], e_ref[ci][0:1], st)
            o_ref[r, :] = o + in_ref[r, :]
            return stn

        lax.fori_loop(0, n, step, jnp.zeros((DVB, DKB), F32))

    _, _, v, dk, dv, e_s, s = _gla_specs(t, "dh")
    return pl.pallas_call(
        body, name="gla_scan_fwd", grid=(2, NB),
        in_specs=[v, dk, dk, dv, e_s], out_specs=[dv, s],
        out_shape=[jax.ShapeDtypeStruct((2, t, D), F32), jax.ShapeDtypeStruct((2, NB, n, DVB, DKB), F32)],
        compiler_params=_cparams(("parallel", "parallel")),
    )(p, qg, kd, intra, e)


def _gla_scan_bwd(p, qg, kd, e, ssave, do):
    t = p.shape[0]
    n = t // CH

    def body(v_ref, qg_ref, kd_ref, e_ref, s_ref, do_ref, dqg_ref, dkd_ref, dv_ref, de_ref):
        rev = pl.program_id(0) == 1

        def step(i, dst):
            ci = jnp.where(rev, i, n - 1 - i)
            r = _rows(ci)
            _, vjp = jax.vjp(_gla_scan, qg_ref[r, :], kd_ref[r, :], v_ref[r, :], e_ref[ci][0:1], s_ref[ci])
            dqg, dkd, dv, de, dstn = vjp((do_ref[r, :], dst))
            dqg_ref[r, :] = dqg
            dkd_ref[r, :] = dkd
            dv_ref[r, :] = dv
            de_ref[ci] = jnp.broadcast_to(de, (8, 128))
            return dstn

        lax.fori_loop(0, n, step, jnp.zeros((DVB, DKB), F32))

    _, _, v, dk, dv, e_s, s = _gla_specs(t, "dh")
    return pl.pallas_call(
        body, name="gla_scan_bwd", grid=(2, NB),
        in_specs=[v, dk, dk, e_s, s, pl.BlockSpec((t, DVB), lambda d, h: (0, h))],
        out_specs=[dk, dk, dv, e_s],
        out_shape=[jax.ShapeDtypeStruct((2, t, NB * DKB), F32), jax.ShapeDtypeStruct((2, t, NB * DKB), F32),
                   jax.ShapeDtypeStruct((2, t, D), F32), jax.ShapeDtypeStruct((2, NB, n, 8, 128), F32)],
        compiler_params=_cparams(("parallel", "parallel")),
    )(p, qg, kd, e, ssave, do)


def _gla_intra_bwd(p, gk, dqg, dkd, dvs, de, do):
    t = p.shape[0]
    n = t // CH

    def body(q_ref, k_ref, v_ref, g_ref, dqg_ref, dkd_ref, dvs_ref, de_ref, do_ref,
             dq_ref, dk_ref, dv_ref, dg_ref):
        d = pl.program_id(1)
        cs = _Consts(d == 1)

        @pl.when(d == 0)
        def _():
            dq_ref[...] = jnp.zeros_like(dq_ref)
            dk_ref[...] = jnp.zeros_like(dk_ref)
            dv_ref[...] = jnp.zeros_like(dv_ref)

        def step(i, carry):
            r = _rows(i)
            f = lambda q, k, v, g: _gla_intra(q, k, v, g, cs)
            _, vjp = jax.vjp(f, q_ref[r, :], k_ref[r, :], v_ref[r, :], g_ref[r, :])
            dq, dk, dv, dg = vjp((dqg_ref[r, :], dkd_ref[r, :], do_ref[r, :], de_ref[i][0:1]))
            dq_ref[r, :] += dq
            dk_ref[r, :] += dk
            dv_ref[r, :] += dv + dvs_ref[r, :]
            dg_ref[r, :] = dg
            return carry

        lax.fori_loop(0, n, step, 0)

    q, k, v, dk, dv, e_s, _ = _gla_specs(t, "hd")
    hk = pl.BlockSpec((t, DKB), lambda h, d: (0, h))
    hv = pl.BlockSpec((t, DVB), lambda h, d: (0, h))
    return pl.pallas_call(
        body, name="gla_intra_bwd", grid=(NB, 2),
        in_specs=[q, k, v, dk, dk, dk, dv, e_s, hv],
        out_specs=[hk, hk, hv, dk],
        out_shape=[jax.ShapeDtypeStruct((t, NB * DKB), F32), jax.ShapeDtypeStruct((t, NB * DKB), F32),
                   jax.ShapeDtypeStruct((t, D), F32), jax.ShapeDtypeStruct((2, t, NB * DKB), F32)],
        compiler_params=_cparams(("arbitrary", "arbitrary")),
    )(p, p, p, gk, dqg, dkd, dvs, de, do)


def _seg_gate(o, z, w):
    return _rms(o, w) * _silu(z)


def _seg_merge(ya, yb, ga, gb):
    return _sigmoid(ga) * ya + _sigmoid(gb) * yb


def _seg_loss(out, x, tgt, w):
    err = x + _rms(out, w) - tgt
    return 0.5 * jnp.sum(jnp.mean(err * err, axis=-1, keepdims=True), axis=0, keepdims=True)


def _post(oa2, ob2, p, x, tgt, gdn_w, gla_w, lnpost, w3, tm=128):
    t = x.shape[0]

    def body(oa_ref, ob_ref, z_ref, gb_ref, ga_ref, gB_ref, x_ref, t_ref, aw_ref, bw_ref, lw_ref, w_ref,
             loss_ref, doa_ref, dob_ref, dz_ref, dgb_ref, dga_ref, dgB_ref, dy_ref,
             dw_ref, daw_ref, dbw_ref, dlw_ref):
        first = pl.program_id(0) == 0
        oa = oa_ref[0] + oa_ref[1]
        ob = ob_ref[0] + ob_ref[1]
        z, gb = z_ref[...], gb_ref[...]
        aw, bw = aw_ref[...], bw_ref[...]

        pa = [jax.vjp(_seg_gate, oa[:, h * DA:(h + 1) * DA], z[:, h * DA:(h + 1) * DA], aw) for h in range(NA)]
        pb = [jax.vjp(_seg_gate, ob[:, h * DVB:(h + 1) * DVB], gb[:, h * DVB:(h + 1) * DVB], bw)
              for h in range(NB)]
        a1 = jnp.concatenate([v for v, _ in pa], axis=1).astype(BF16)
        a2 = jnp.concatenate([v for v, _ in pb], axis=1).astype(BF16)
        ya = jnp.dot(a1, w_ref[0], preferred_element_type=F32)
        yb = jnp.dot(a2, w_ref[1], preferred_element_type=F32)
        merged, vjp_m = jax.vjp(_seg_merge, ya, yb, ga_ref[...], gB_ref[...])
        mb = merged.astype(BF16)
        out = jnp.dot(mb, w_ref[2], preferred_element_type=F32)
        loss, vjp_l = jax.vjp(_seg_loss, out, x_ref[...], t_ref[...], lw_ref[...])
        dout, dyres, _, dlw = vjp_l(jnp.ones((1, 1), F32))
        dy_ref[...] = dyres
        doutb = dout.astype(BF16)
        dmerged = _bdot(doutb, w_ref[2], 1, 1)
        dya, dyb, dga, dgB = vjp_m(dmerged)
        dga_ref[...] = dga
        dgB_ref[...] = dgB
        dyab, dybb = dya.astype(BF16), dyb.astype(BF16)
        da1 = _bdot(dyab, w_ref[0], 1, 1)
        da2 = _bdot(dybb, w_ref[1], 1, 1)

        daw = jnp.zeros_like(aw)
        for h in range(NA):
            sl = slice(h * DA, (h + 1) * DA)
            do, dz, dw = pa[h][1](da1[:, sl])
            doa_ref[:, sl] = do
            dz_ref[:, sl] = dz
            daw = daw + dw
        dbw = jnp.zeros_like(bw)
        for h in range(NB):
            sl = slice(h * DVB, (h + 1) * DVB)
            do, dg, dw = pb[h][1](da2[:, sl])
            dob_ref[:, sl] = do
            dgb_ref[:, sl] = dg
            dbw = dbw + dw

        @pl.when(first)
        def _():
            loss_ref[...] = jnp.zeros_like(loss_ref)
            dw_ref[...] = jnp.zeros_like(dw_ref)
            daw_ref[...] = jnp.zeros_like(daw_ref)
            dbw_ref[...] = jnp.zeros_like(dbw_ref)
            dlw_ref[...] = jnp.zeros_like(dlw_ref)

        loss_ref[...] += jnp.broadcast_to(loss, loss_ref.shape)
        dw_ref[0] += _bdot(a1, dyab, 0, 0)
        dw_ref[1] += _bdot(a2, dybb, 0, 0)
        dw_ref[2] += _bdot(mb, doutb, 0, 0)
        daw_ref[...] += jnp.broadcast_to(daw, daw_ref.shape)
        dbw_ref[...] += jnp.broadcast_to(dbw, dbw_ref.shape)
        dlw_ref[...] += jnp.broadcast_to(dlw, dlw_ref.shape)

    two = pl.BlockSpec((2, tm, D), lambda i: (0, i, 0))
    pcol = lambda c: pl.BlockSpec((tm, D), lambda i: (i, c))
    tok = pl.BlockSpec((tm, D), lambda i: (i, 0))
    row = lambda n: pl.BlockSpec((1, n), lambda i: (0, 0))
    row8 = lambda n: pl.BlockSpec((8, n), lambda i: (0, 0))
    once = pl.Buffered(1)
    tokf = jax.ShapeDtypeStruct((t, D), F32)
    return pl.pallas_call(
        body, name="post", grid=(t // tm,),
        in_specs=[two, two, pcol(3), pcol(6), pcol(7), pcol(8), tok, tok, row(DA), row(DVB), row(D),
                  pl.BlockSpec((3, D, D), lambda i: (0, 0, 0), pipeline_mode=once)],
        out_specs=[row8(128), tok, tok, tok, tok, tok, tok, tok,
                   pl.BlockSpec((3, D, D), lambda i: (0, 0, 0), pipeline_mode=once),
                   row8(DA), row8(DVB), row8(D)],
        out_shape=[jax.ShapeDtypeStruct((8, 128), F32), tokf, tokf, tokf, tokf, tokf, tokf, tokf,
                   jax.ShapeDtypeStruct((3, D, D), F32),
                   jax.ShapeDtypeStruct((8, DA), F32), jax.ShapeDtypeStruct((8, DVB), F32),
                   jax.ShapeDtypeStruct((8, D), F32)],
        compiler_params=_cparams(("arbitrary",), vmem_mb=56),
    )(oa2, ob2, p, p, p, p, x, tgt, gdn_w, gla_w, lnpost, w3)


def _adam(w, g, m, v, tr):
    rows, cols = w.shape

    def body(w_ref, g_ref, m_ref, v_ref, d_ref, nm_ref, nv_ref):
        gg = g_ref[...]
        nm = B1 * m_ref[...] + (1.0 - B1) * gg
        nv = B2 * v_ref[...] + (1.0 - B2) * (gg * gg)
        m_hat = nm / (1.0 - B1 ** STEP)
        v_hat = nv / (1.0 - B2 ** STEP)
        d_ref[...] = -LR * (m_hat / (jnp.sqrt(v_hat) + ADAM_EPS) + WD * w_ref[...])
        nm_ref[...] = nm
        nv_ref[...] = nv

    blk = pl.BlockSpec((tr, cols), lambda i: (i, 0))
    shp = jax.ShapeDtypeStruct((rows, cols), F32)
    return pl.pallas_call(
        body, name=f"adam_{rows}x{cols}", grid=(rows // tr,),
        in_specs=[blk] * 4, out_specs=[blk] * 3, out_shape=[shp] * 3,
        compiler_params=_cparams(("parallel",)),
    )(w, g, m, v)


def _sum_cast(own, got):
    _, ns, r, c = own.shape
    tr = r // 4 if r >= 64 else r

    def body(c_ref, a_ref, b_ref, f_ref, h_ref):
        s = a_ref[...] + b_ref[...]
        f_ref[...] = s
        h_ref[...] = s.astype(BF16)

    return pl.pallas_call(
        body, name=f"sum_cast_{r}x{c}",
        grid_spec=pltpu.PrefetchScalarGridSpec(
            num_scalar_prefetch=1, grid=(ns, r // tr),
            in_specs=[pl.BlockSpec((None, None, tr, c), lambda s, i, cc: (cc[0], s, i, 0)),
                      pl.BlockSpec((None, tr, c), lambda s, i, cc: (s, i, 0))],
            out_specs=[pl.BlockSpec((None, tr, c), lambda s, i, cc: (s, i, 0)),
                       pl.BlockSpec((None, tr, c), lambda s, i, cc: (s, i, 0))]),
        out_shape=[jax.ShapeDtypeStruct((ns, r, c), F32), jax.ShapeDtypeStruct((ns, r, c), BF16)],
        compiler_params=_cparams(("parallel", "parallel")),
    )(lax.axis_index("c").reshape(1), own, got)


def _sum4(mine, got):
    _, r, c = mine.shape
    tr = r // 4 if r >= 64 else r

    def body(s_ref, a_ref, g_ref, o_ref):
        acc = a_ref[...] + g_ref[0].astype(F32)
        acc = acc + g_ref[1].astype(F32)
        o_ref[...] = acc + g_ref[2].astype(F32)

    shard = (2 * lax.axis_index("x") + lax.axis_index("y")).reshape(1)
    return pl.pallas_call(
        body, name=f"sum4_{r}x{c}",
        grid_spec=pltpu.PrefetchScalarGridSpec(
            num_scalar_prefetch=1, grid=(r // tr,),
            in_specs=[pl.BlockSpec((None, tr, c), lambda i, ss: (ss[0], i, 0)),
                      pl.BlockSpec((3, tr, c), lambda i, ss: (0, i, 0))],
            out_specs=pl.BlockSpec((tr, c), lambda i, ss: (i, 0))),
        out_shape=jax.ShapeDtypeStruct((r, c), F32),
        compiler_params=_cparams(("parallel",)),
    )(shard, mine, got)


def _place():
    x, y, c = lax.axis_index("x"), lax.axis_index("y"), lax.axis_index("c")
    chips = [(1 - x, y), (x, 1 - y), (1 - x, 1 - y)]
    return x, y, c, chips


def _gather_weights(parts):
    npart = len(parts)

    def body(*refs):
        ins, outs = refs[:npart], refs[npart:2 * npart]
        send_sems, recv_sems, local_sems = refs[2 * npart:]
        x, y, c, chips = _place()
        sibling = (x, y, 1 - c)
        mine = 2 * x + y

        def remote(k, p, shard, half, to, src=None):
            dst = outs[p].at[shard, half]
            return pltpu.make_async_remote_copy(
                src_ref=dst if src is None else src, dst_ref=dst,
                send_sem=send_sems.at[k], recv_sem=recv_sems.at[k], device_id=to, device_id_type=MESH)

        local = [pltpu.make_async_copy(ins[p], outs[p].at[mine], local_sems.at[p]) for p in range(npart)]
        for cp in local:
            cp.start()
        first = [remote(j * npart + p, p, mine, c, (*chip, c), src=ins[p].at[c])
                 for j, chip in enumerate(chips) for p in range(npart)]
        for cp in first:
            cp.start()
        passed = []
        for j, (cx, cy) in enumerate(chips):
            for p in range(npart):
                remote(j * npart + p, p, 2 * cx + cy, c, (x, y, c)).wait_recv()
                fw = remote((3 + j) * npart + p, p, 2 * cx + cy, c, sibling)
                fw.start()
                passed.append(fw)
        for j, (cx, cy) in enumerate(chips):
            for p in range(npart):
                remote((3 + j) * npart + p, p, 2 * cx + cy, 1 - c, (x, y, c)).wait_recv()
        for cp in first + passed:
            cp.wait_send()
        for cp in local:
            cp.wait()

    return pl.pallas_call(
        body, name="gather_weights",
        in_specs=[ANY] * npart, out_specs=[ANY] * npart,
        out_shape=[jax.ShapeDtypeStruct((NSHARD,) + a.shape, a.dtype) for a in parts],
        scratch_shapes=[pltpu.SemaphoreType.DMA((6 * npart,)), pltpu.SemaphoreType.DMA((6 * npart,)),
                        pltpu.SemaphoreType.DMA((npart,))],
    )(*parts)


def _swap_halves(parts):
    npart = len(parts)

    def body(*refs):
        ins, outs = refs[:npart], refs[npart:2 * npart]
        send_sems, recv_sems = refs[2 * npart:]
        x, y, c, _ = _place()
        cps = [pltpu.make_async_remote_copy(
            src_ref=ins[p].at[1 - c], dst_ref=outs[p], send_sem=send_sems.at[p], recv_sem=recv_sems.at[p],
            device_id=(x, y, 1 - c), device_id_type=MESH) for p in range(npart)]
        for cp in cps:
            cp.start()
        for cp in cps:
            cp.wait()

    return pl.pallas_call(
        body, name="swap_halves", in_specs=[ANY] * npart, out_specs=[ANY] * npart,
        out_shape=[jax.ShapeDtypeStruct(a.shape[1:], a.dtype) for a in parts],
        scratch_shapes=[pltpu.SemaphoreType.DMA((npart,)), pltpu.SemaphoreType.DMA((npart,))],
    )(*parts)


def _scatter_shards(parts):
    npart = len(parts)

    def body(*refs):
        ins, outs = refs[:npart], refs[npart:2 * npart]
        send_sems, recv_sems = refs[2 * npart:]
        x, y, c, chips = _place()
        cps = [pltpu.make_async_remote_copy(
            src_ref=ins[p].at[2 * cx + cy], dst_ref=outs[p].at[j],
            send_sem=send_sems.at[j * npart + p], recv_sem=recv_sems.at[j * npart + p],
            device_id=(cx, cy, c), device_id_type=MESH)
            for j, (cx, cy) in enumerate(chips) for p in range(npart)]
        for cp in cps:
            cp.start()
        for cp in cps:
            cp.wait()

    return pl.pallas_call(
        body, name="scatter_shards", in_specs=[ANY] * npart, out_specs=[ANY] * npart,
        out_shape=[jax.ShapeDtypeStruct((3,) + a.shape[1:], a.dtype) for a in parts],
        scratch_shapes=[pltpu.SemaphoreType.DMA((3 * npart,)), pltpu.SemaphoreType.DMA((3 * npart,))],
    )(*parts)


def _join_halves(parts):
    npart = len(parts)

    def body(*refs):
        ins, outs = refs[:npart], refs[npart:2 * npart]
        send_sems, recv_sems, local_sems = refs[2 * npart:]
        x, y, c, _ = _place()
        local = [pltpu.make_async_copy(ins[p], outs[p].at[c], local_sems.at[p]) for p in range(npart)]
        cps = [pltpu.make_async_remote_copy(
            src_ref=ins[p], dst_ref=outs[p].at[c], send_sem=send_sems.at[p], recv_sem=recv_sems.at[p],
            device_id=(x, y, 1 - c), device_id_type=MESH) for p in range(npart)]
        for cp in local + cps:
            cp.start()
        for cp in cps:
            cp.wait_send()
        for p in range(npart):
            pltpu.make_async_remote_copy(
                src_ref=ins[p], dst_ref=outs[p].at[1 - c], send_sem=send_sems.at[p], recv_sem=recv_sems.at[p],
                device_id=(x, y, 1 - c), device_id_type=MESH).wait_recv()
        for cp in local:
            cp.wait()

    return pl.pallas_call(
        body, name="join_halves", in_specs=[ANY] * npart, out_specs=[ANY] * npart,
        out_shape=[jax.ShapeDtypeStruct((2,) + a.shape, a.dtype) for a in parts],
        scratch_shapes=[pltpu.SemaphoreType.DMA((npart,)), pltpu.SemaphoreType.DMA((npart,)),
                        pltpu.SemaphoreType.DMA((npart,))],
    )(*parts)


def _allreduce_small(v):
    r = v.shape[0]

    def body(v_ref, o_ref, buf, send_sems, recv_sems):
        x, y, c, _ = _place()
        me = 4 * x + 2 * y + c
        buf[me] = v_ref[...]
        cps = []
        for k in range(1, 8):
            px, py, pc = x ^ (k >> 2), y ^ ((k >> 1) & 1), c ^ (k & 1)
            cps.append(pltpu.make_async_remote_copy(
                src_ref=v_ref, dst_ref=buf.at[me], send_sem=send_sems.at[k - 1], recv_sem=recv_sems.at[k - 1],
                device_id=(px, py, pc), device_id_type=MESH))
        for cp in cps:
            cp.start()
        for k in range(1, 8):
            px, py, pc = x ^ (k >> 2), y ^ ((k >> 1) & 1), c ^ (k & 1)
            pltpu.make_async_remote_copy(
                src_ref=v_ref, dst_ref=buf.at[4 * px + 2 * py + pc], send_sem=send_sems.at[k - 1],
                recv_sem=recv_sems.at[k - 1], device_id=(px, py, pc), device_id_type=MESH).wait_recv()
        for cp in cps:
            cp.wait_send()
        acc = buf[0]
        for d in range(1, 8):
            acc = acc + buf[d]
        o_ref[...] = acc

    return pl.pallas_call(
        body, name="allreduce_small",
        in_specs=[pl.BlockSpec(memory_space=pltpu.VMEM)], out_specs=pl.BlockSpec(memory_space=pltpu.VMEM),
        out_shape=jax.ShapeDtypeStruct((r, 128), F32),
        scratch_shapes=[pltpu.VMEM((8, r, 128), F32), pltpu.SemaphoreType.DMA((7,)), pltpu.SemaphoreType.DMA((7,))],
    )(v)


def _permute_cols(w):
    zeros = jnp.zeros((w.shape[0], NPERM - 9280), w.dtype)
    return jnp.concatenate([w[:, 0:4096], w[:, 4128:7200], w[:, 7232:9280], w[:, 4096:4128], w[:, 7200:7232], zeros],
                           axis=1)


def _unpermute_cols(g):
    return jnp.concatenate([g[:, 0:4096], g[:, 9216:9248], g[:, 4096:7168], g[:, 9248:9280], g[:, 7168:9216]],
                           axis=1)


SMALL_NAMES = ("ln_pre_w", "a_log_fwd", "a_log_bwd", "dt_bias_fwd", "dt_bias_bwd", "gdn_norm_w",
               "gk_b2_fwd", "gk_b2_bwd", "gla_norm_w", "ln_post_w")
SMALL_SIZES = (1024, 8, 8, 8, 8, 128, 512, 512, 256, 1024)
SMALL_ROWS = 32


def _pack_small(vals):
    flat = jnp.concatenate([v.reshape(-1) for v in vals])
    return jnp.pad(flat, (0, SMALL_ROWS * 128 - flat.shape[0])).reshape(SMALL_ROWS, 128)


def _unpack_small(packed):
    flat = packed.reshape(-1)
    out, o = [], 0
    for n in SMALL_SIZES:
        out.append(flat[o:o + n].reshape(1, n))
        o += n
    return out


def _pack_shard_small(conv, w2f, w2b):
    top = jnp.pad(conv, ((0, 8 - conv.shape[0]), (0, 0)))
    mid = jnp.pad(jnp.concatenate([w2f, w2b], axis=1), ((0, 0), (0, 768 - 256)))
    return jnp.concatenate([top, mid, jnp.zeros((8, 768), conv.dtype)], axis=0)


def _unpack_shard_small(a):
    return a[0:5], a[8:24, 0:128], a[8:24, 128:256]


def kernel(x, ln_pre_w, w_in, conv_w, a_log_fwd, a_log_bwd, dt_bias_fwd, dt_bias_bwd, gdn_norm_w, w_proj_gdn, gk_w2_fwd, gk_b2_fwd, gk_w2_bwd, gk_b2_bwd, gla_norm_w, w_proj_gla, w_out, ln_post_w, loss_target, m_ln_pre_w, m_w_in, m_conv_w, m_a_log_fwd, m_a_log_bwd, m_dt_bias_fwd, m_dt_bias_bwd, m_gdn_norm_w, m_w_proj_gdn, m_gk_w2_fwd, m_gk_b2_fwd, m_gk_w2_bwd, m_gk_b2_bwd, m_gla_norm_w, m_w_proj_gla, m_w_out, m_ln_post_w, v_ln_pre_w, v_w_in, v_conv_w, v_a_log_fwd, v_a_log_bwd, v_dt_bias_fwd, v_dt_bias_bwd, v_gdn_norm_w, v_w_proj_gdn, v_gk_w2_fwd, v_gk_b2_fwd, v_gk_w2_bwd, v_gk_b2_bwd, v_gla_norm_w, v_w_proj_gla, v_w_out, v_ln_post_w):
    t = x.shape[1]
    x2, tgt = x[0], loss_target[0]

    win_l = w_in[0].astype(BF16).reshape(2, D // 2, SHW)
    proj_l = jnp.concatenate([w_proj_gdn[0], w_proj_gla[0], w_out[0]], axis=0).astype(BF16).reshape(2, 384, D)
    small_l = _pack_shard_small(conv_w[0], gk_w2_fwd[0], gk_w2_bwd[0]).reshape(2, 16, 768)
    win_g, proj_g, small_g = _gather_weights([win_l, proj_l, small_l])
    w_full = win_g.reshape(NSHARD, D, SHW).transpose(1, 0, 2).reshape(D, NSHARD * SHW)
    wperm = _permute_cols(w_full)
    w3 = proj_g.reshape(NSHARD, 3, D // NSHARD, D).transpose(1, 0, 2, 3).reshape(3, D, D)
    small_g = small_g.reshape(NSHARD, 32, 768)
    convw = small_g[:, 0:8, :].transpose(1, 0, 2).reshape(8, 3 * D)
    w2f = small_g[:, 8:24, 0:128].transpose(1, 0, 2).reshape(16, 512)
    w2b = small_g[:, 8:24, 128:256].transpose(1, 0, 2).reshape(16, 512)
    w2f_pad = jnp.pad(w2f, ((32, 80), (0, 0)))
    w2b_pad = jnp.pad(w2b, ((48, 64), (0, 0)))
    alog_row = jnp.pad(jnp.concatenate([a_log_fwd, a_log_bwd], axis=1), ((0, 0), (0, 112)))
    dt_row = jnp.pad(jnp.concatenate([dt_bias_fwd, dt_bias_bwd], axis=1), ((0, 0), (0, 112)))

    p, h = _inproj(x2, ln_pre_w, wperm)
    qn, kn, vc = (_qkv_fwd(p, convw, kind) for kind in range(3))
    gsm, gk = _gates_fwd(p, alog_row, dt_row, w2f_pad, gk_b2_fwd, w2b_pad, gk_b2_bwd)
    u, w, at, qd, kd, el = _gdn_intra_fwd(qn, kn, vc, gsm)
    oa2, sa = _gdn_scan_fwd(u, w, at, qd, kd, el)
    qg, kdb, intra, elb = _gla_intra_fwd(p, gk)
    ob2, sb = _gla_scan_fwd(p, qg, kdb, intra, elb)

    (loss8, doa, dob, dz, dgb, dga, dgB, dyres, dw3, dgdn_w, dgla_w, dlnpost) = _post(
        oa2, ob2, p, x2, tgt, gdn_norm_w, gla_norm_w, ln_post_w, w3)

    du, dw, dat, dqd, dkd, del_ = _gdn_scan_bwd(u, w, at, qd, kd, el, sa, doa)
    dqn, dkn, dvc, dgsm = _gdn_intra_bwd(qn, kn, vc, gsm, du, dw, dat, dqd, dkd, del_)
    dqg, dkdb, dvs, delb = _gla_scan_bwd(p, qg, kdb, elb, sb, dob)
    dqb, dkb, dvb, dgk = _gla_intra_bwd(p, gk, dqg, dkdb, dvs, delb, dob)
    (dps, dalog8, ddt8, dw2f_pad, db2f8, dw2b_pad, db2b8) = _gates_bwd(
        p, alog_row, dt_row, w2f_pad, gk_b2_fwd, w2b_pad, gk_b2_bwd, dgsm, dgk)
    dpre, dconv = zip(*[_qkv_bwd(p, convw, g, kind) for kind, g in enumerate((dqn, dkn, dvc))])

    dp = jnp.concatenate([a.astype(BF16) for a in (*dpre, dz, dqb, dkb, dvb, dgb, dga, dgB, dps)]
                         + [jnp.zeros((t, NPERM - 9344), BF16)], axis=1)
    dwperm = _inproj_dw(h, dp)
    dx, dlnpre8 = _inproj_dx(dp, wperm, x2, ln_pre_w, dyres)

    dw_in_full = _unpermute_cols(dwperm)
    g_in = dw_in_full.reshape(2, D // 2, NSHARD, SHW).transpose(0, 2, 1, 3)
    g_proj = dw3.reshape(3, NSHARD, D // NSHARD, D).transpose(1, 0, 2, 3).reshape(NSHARD, 2, 384, D)
    g_proj = g_proj.transpose(1, 0, 2, 3)
    dconv_full = jnp.concatenate(dconv, axis=1)
    dw2f, dw2b = dw2f_pad[32:48], dw2b_pad[48:64]
    g_small = jnp.stack([_pack_shard_small(dconv_full[0:5, 768 * s:768 * (s + 1)],
                                           dw2f[:, 128 * s:128 * (s + 1)], dw2b[:, 128 * s:128 * (s + 1)])
                         for s in range(NSHARD)])
    g_small = g_small.reshape(NSHARD, 2, 16, 768).transpose(1, 0, 2, 3)
    parts = [g_in, g_proj, g_small]
    got = _swap_halves(parts)
    sums = [_sum_cast(a, b) for a, b in zip(parts, got)]
    landed = _scatter_shards([hb for _, hb in sums])
    halves = [_sum4(f, g) for (f, _), g in zip(sums, landed)]
    r_in, r_proj, r_small = _join_halves(halves)
    grad_w_in = r_in.reshape(D, SHW)
    r_proj = r_proj.reshape(768, D)
    r_small = r_small.reshape(32, 768)

    small_grads = (dlnpre8[0:1], dalog8[0:1, 0:8], dalog8[0:1, 8:16], ddt8[0:1, 0:8], ddt8[0:1, 8:16],
                   dgdn_w[0:1], db2f8[0:1], db2b8[0:1], dgla_w[0:1], dlnpost[0:1])
    gsmall = _allreduce_small(_pack_small(small_grads))

    d_in, nm_in, nv_in = _adam(w_in[0], grad_w_in, m_w_in[0], v_w_in[0], 128)
    stack3 = lambda a, b, c: jnp.concatenate([a[0], b[0], c[0]], axis=0)
    d_pr, nm_pr, nv_pr = _adam(stack3(w_proj_gdn, w_proj_gla, w_out), r_proj,
                               stack3(m_w_proj_gdn, m_w_proj_gla, m_w_out),
                               stack3(v_w_proj_gdn, v_w_proj_gla, v_w_out), 256)
    d_ss, nm_ss, nv_ss = _adam(_pack_shard_small(conv_w[0], gk_w2_fwd[0], gk_w2_bwd[0]), r_small,
                               _pack_shard_small(m_conv_w[0], m_gk_w2_fwd[0], m_gk_w2_bwd[0]),
                               _pack_shard_small(v_conv_w[0], v_gk_w2_fwd[0], v_gk_w2_bwd[0]), 32)
    smalls = dict(ln_pre_w=(ln_pre_w, m_ln_pre_w, v_ln_pre_w), a_log_fwd=(a_log_fwd, m_a_log_fwd, v_a_log_fwd),
                  a_log_bwd=(a_log_bwd, m_a_log_bwd, v_a_log_bwd),
                  dt_bias_fwd=(dt_bias_fwd, m_dt_bias_fwd, v_dt_bias_fwd),
                  dt_bias_bwd=(dt_bias_bwd, m_dt_bias_bwd, v_dt_bias_bwd),
                  gdn_norm_w=(gdn_norm_w, m_gdn_norm_w, v_gdn_norm_w),
                  gk_b2_fwd=(gk_b2_fwd, m_gk_b2_fwd, v_gk_b2_fwd), gk_b2_bwd=(gk_b2_bwd, m_gk_b2_bwd, v_gk_b2_bwd),
                  gla_norm_w=(gla_norm_w, m_gla_norm_w, v_gla_norm_w), ln_post_w=(ln_post_w, m_ln_post_w, v_ln_post_w))
    ws, ms, vs = (_pack_small([smalls[n][i] for n in SMALL_NAMES]) for i in range(3))
    d_sm, nm_sm, nv_sm = _adam(ws, gsmall, ms, vs, SMALL_ROWS)

    def family(in_, pr, ss, sm):
        conv, w2f_, w2b_ = _unpack_shard_small(ss)
        s = dict(zip(SMALL_NAMES, _unpack_small(sm)))
        return [s["ln_pre_w"], in_[None], conv[None], s["a_log_fwd"], s["a_log_bwd"], s["dt_bias_fwd"],
                s["dt_bias_bwd"], s["gdn_norm_w"], pr[None, 0:256], w2f_[None], s["gk_b2_fwd"], w2b_[None],
                s["gk_b2_bwd"], s["gla_norm_w"], pr[None, 256:512], pr[None, 512:768], s["ln_post_w"]]

    loss = lax.psum(loss8[0, 0], ("x", "y", "c"))
    outs = [loss, dx[None]]
    outs += family(grad_w_in, r_proj, r_small, gsmall)
    outs += family(d_in, d_pr, d_ss, d_sm)
    outs += family(nm_in, nm_pr, nm_ss, nm_sm)
    outs += family(nv_in, nv_pr, nv_ss, nv_sm)
    return tuple(outs)
```

```python
import functools

import jax
import jax.numpy as jnp
from jax import lax
from jax.experimental import pallas as pl
from jax.experimental.pallas import tpu as pltpu

F32 = jnp.float32
BF16 = jnp.bfloat16
HI = lax.Precision.HIGHEST
MESH = pl.DeviceIdType.MESH

D = 1024
CH = 64
EPS = 1e-6
NA, DA = 8, 128
NB, DKB, DVB = 4, 128, 256
NSHARD = 4
SHW = 2320
NPERM = 9728
PS_BLOCK = 72
LR, B1, B2, ADAM_EPS, WD, STEP = 0.001, 0.9, 0.999, 1e-08, 0.01, 10

ANY = pl.BlockSpec(memory_space=pl.ANY)


def _cparams(sem=None, vmem_mb=48):
    return pltpu.CompilerParams(dimension_semantics=sem, vmem_limit_bytes=vmem_mb << 20)


def _bdot(a, b, ca, cb):
    return lax.dot_general(a.astype(BF16), b.astype(BF16), (((ca,), (cb,)), ((), ())),
                           preferred_element_type=F32)


@jax.custom_vjp
def mm(a, b):
    return _bdot(a, b, 1, 0)


def _mm_fwd(a, b):
    return _bdot(a, b, 1, 0), (a, b)


def _mm_bwd(res, g):
    a, b = res
    return _bdot(g, b, 1, 1), _bdot(a, g, 0, 0)


mm.defvjp(_mm_fwd, _mm_bwd)


@jax.custom_vjp
def mm_nt(a, b):
    return _bdot(a, b, 1, 1)


def _mm_nt_fwd(a, b):
    return _bdot(a, b, 1, 1), (a, b)


def _mm_nt_bwd(res, g):
    a, b = res
    return _bdot(g, b, 1, 0), _bdot(g, a, 0, 0)


mm_nt.defvjp(_mm_nt_fwd, _mm_nt_bwd)


@jax.custom_vjp
def mm_tn(a, b):
    return _bdot(a, b, 0, 0)


def _mm_tn_fwd(a, b):
    return _bdot(a, b, 0, 0), (a, b)


def _mm_tn_bwd(res, g):
    a, b = res
    return _bdot(b, g, 1, 1), _bdot(a, g, 1, 0)


mm_tn.defvjp(_mm_tn_fwd, _mm_tn_bwd)


def dot_hi(a, b):
    return lax.dot_general(a, b, (((1,), (0,)), ((), ())), precision=HI, preferred_element_type=F32)


def _split3(x):
    x1 = x.astype(BF16)
    r = x - x1.astype(F32)
    x2 = r.astype(BF16)
    return x1, x2, (r - x2.astype(F32)).astype(BF16)


def _cdot(c, x, cc, cx, c_first=True):
    parts = _split3(x)
    if c_first:
        return _bdot(c, parts[0], cc, cx) + _bdot(c, parts[1], cc, cx) + _bdot(c, parts[2], cc, cx)
    return _bdot(parts[0], c, cx, cc) + _bdot(parts[1], c, cx, cc) + _bdot(parts[2], c, cx, cc)


@jax.custom_vjp
def cmm(c, x):
    return _cdot(c, x, 1, 0)


def _cmm_fwd(c, x):
    return _cdot(c, x, 1, 0), c


def _cmm_bwd(c, g):
    return jnp.zeros_like(c), _cdot(c, g, 0, 0)


cmm.defvjp(_cmm_fwd, _cmm_bwd)


@jax.custom_vjp
def mmc(x, c):
    return _cdot(c, x, 0, 1, c_first=False)


def _mmc_fwd(x, c):
    return _cdot(c, x, 0, 1, c_first=False), c


def _mmc_bwd(c, g):
    return _cdot(c, g, 1, 1, c_first=False), jnp.zeros_like(c)


mmc.defvjp(_mmc_fwd, _mmc_bwd)


def _sigmoid(x):
    return 1.0 / (1.0 + jnp.exp(-x))


def _silu(x):
    return x * _sigmoid(x)


def _softplus(x):
    return jnp.maximum(x, 0.0) + jnp.log(1.0 + jnp.exp(-jnp.abs(x)))


def _rms(x, w):
    return x * lax.rsqrt(jnp.mean(x * x, axis=-1, keepdims=True) + EPS) * w


class _Consts:
    def __init__(self, rev):
        r = lax.broadcasted_iota(jnp.int32, (CH, CH), 0)
        c = lax.broadcasted_iota(jnp.int32, (CH, CH), 1)
        a = jnp.where(rev, c, r)
        b = jnp.where(rev, r, c)
        self.incl = a >= b
        self.strict = a > b
        self.incl_f = self.incl.astype(F32)
        self.eye = (r == c).astype(F32)
        self.ones = jnp.ones((CH, CH), F32)
        rows = lax.broadcasted_iota(jnp.int32, (CH, 1), 0)
        self.last_sel = (rows == jnp.where(rev, 0, CH - 1)).astype(F32)


def _dot3(a, b, ca=1, cb=0):
    ah, bh = a.astype(BF16), b.astype(BF16)
    al, bl = (a - ah.astype(F32)).astype(BF16), (b - bh.astype(F32)).astype(BF16)
    return _bdot(ah, bh, ca, cb) + (_bdot(ah, bl, ca, cb) + _bdot(al, bh, ca, cb))


def _tri_inv(low, eye):
    n = -low
    acc = eye + n
    p = n
    for _ in range(5):
        p = _dot3(p, p)
        acc = acc + _dot3(acc, p)
    return acc


@jax.custom_vjp
def _solve(low, rhs, tinv):
    return _dot3(tinv, rhs)


def _solve_fwd(low, rhs, tinv):
    x = _dot3(tinv, rhs)
    return x, (x, tinv, low)


def _solve_bwd(res, g):
    x, tinv, low = res
    drhs = _dot3(tinv, g, 0, 0)
    return -_dot3(drhs, x, 1, 1), drhs, jnp.zeros_like(tinv)


_solve.defvjp(_solve_fwd, _solve_bwd)


def _gdn_decay(g64, cs):
    grow = cmm(cs.ones, cs.eye * g64)
    return jnp.where(cs.incl, jnp.exp(jnp.where(cs.incl, g64 - grow, 0.0)), 0.0)


def _gdn_low(k, g64, bx, cs):
    return jnp.where(cs.strict, mm_nt(k * bx, k) * _gdn_decay(g64, cs), 0.0)


def _gdn_intra(q, k, v, g, g64, bx, tinv, cs):
    decay = _gdn_decay(g64, cs)
    kb = k * bx
    low = jnp.where(cs.strict, mm_nt(kb, k) * decay, 0.0)
    eg = jnp.exp(g)
    u = _solve(low, v * bx, tinv)
    w = _solve(low, kb * eg, tinv)
    attn = mm_nt(q, k) * decay
    qd = q * eg
    glast = jnp.sum(g * cs.last_sel, axis=0, keepdims=True)
    kd = k * jnp.exp(glast - g)
    return u, w, attn, qd, kd, jnp.exp(glast)


def _gdn_scan(u, w, attn, qd, kd, el, s):
    vn = u - mm(w, s)
    o = mm(qd, s) + mm(attn, vn)
    sn = s * el + mm_tn(kd, vn)
    return o, sn


def _gla_intra(q, k, v, gk, cs):
    gc = cmm(cs.incl_f, gk)
    qg = q * (DKB ** -0.5) * jnp.exp(gc)
    kg = k * jnp.exp(-gc)
    attn = jnp.where(cs.incl, mm_nt(qg, kg), 0.0)
    intra = mm(attn, v)
    glast = jnp.sum(gc * cs.last_sel, axis=0, keepdims=True)
    kd = k * jnp.exp(glast - gc)
    return qg, kd, intra, jnp.exp(glast)


def _gla_scan(qg, kd, v, el, st):
    o = mm_nt(qg, st)
    stn = st * el + mm_tn(v, kd)
    return o, stn


def _shift_rows(x, s):
    if s == 0:
        return x
    t = x.shape[0]
    rolled = pltpu.roll(x, (-s) % t, 0)
    rows = lax.broadcasted_iota(jnp.int32, x.shape, 0)
    return jnp.where((rows + s >= 0) & (rows + s < t), rolled, 0.0)


@jax.custom_vjp
def _conv5(x, w):
    acc = w[0:1] * _shift_rows(x, -2)
    for j in range(1, 5):
        acc = acc + w[j:j + 1] * _shift_rows(x, j - 2)
    return acc


def _conv5_fwd(x, w):
    return _conv5(x, w), (x, w)


def _conv5_bwd(res, g):
    x, w = res
    dx = w[0:1] * _shift_rows(g, 2)
    for j in range(1, 5):
        dx = dx + w[j:j + 1] * _shift_rows(g, 2 - j)
    rows = lax.broadcasted_iota(jnp.int32, w.shape, 0)
    dw = jnp.zeros_like(w)
    for j in range(5):
        dwj = jnp.sum(g * _shift_rows(x, j - 2), axis=0, keepdims=True)
        dw = dw + jnp.where(rows == j, dwj, 0.0)
    return dx, dw


_conv5.defvjp(_conv5_fwd, _conv5_bwd)


def _qkv_act(kind):
    def f(x, w):
        c = _silu(_conv5(x, w))
        if kind == 2:
            return c
        c = c * lax.rsqrt(jnp.sum(c * c, axis=-1, keepdims=True) + EPS)
        return c * (DA ** -0.5) if kind == 0 else c
    return f


def _inproj(x, lnw, wperm, tm=512, tn=512):
    t = x.shape[0]

    def body(x_ref, lnw_ref, w_ref, p_ref, h_ref, hbuf):
        @pl.when(pl.program_id(1) == 0)
        def _():
            hb = _rms(x_ref[...], lnw_ref[...]).astype(BF16)
            hbuf[...] = hb
            h_ref[...] = hb
        p_ref[...] = jnp.dot(hbuf[...], w_ref[...], preferred_element_type=F32)

    return pl.pallas_call(
        body, name="inproj", grid=(t // tm, NPERM // tn),
        in_specs=[pl.BlockSpec((tm, D), lambda i, j: (i, 0)),
                  pl.BlockSpec((1, D), lambda i, j: (0, 0)),
                  pl.BlockSpec((D, tn), lambda i, j: (0, j))],
        out_specs=[pl.BlockSpec((tm, tn), lambda i, j: (i, j)),
                   pl.BlockSpec((tm, D), lambda i, j: (i, 0))],
        out_shape=[jax.ShapeDtypeStruct((t, NPERM), F32),
                   jax.ShapeDtypeStruct((t, D), BF16)],
        scratch_shapes=[pltpu.VMEM((tm, D), BF16)],
        compiler_params=_cparams(("parallel", "arbitrary")),
    )(x, lnw, wperm)


def _inproj_dw(h, dp, tm=512, tn=512):
    t = h.shape[0]
    ncol = dp.shape[1]

    def body(h_ref, dp_ref, dw_ref):
        @pl.when(pl.program_id(1) == 0)
        def _():
            dw_ref[...] = jnp.zeros_like(dw_ref)
        dw_ref[...] += _bdot(h_ref[...], dp_ref[...], 0, 0)

    return pl.pallas_call(
        body, name="inproj_dw", grid=(ncol // tn, t // tm),
        in_specs=[pl.BlockSpec((tm, D), lambda j, i: (i, 0)),
                  pl.BlockSpec((tm, tn), lambda j, i: (i, j))],
        out_specs=pl.BlockSpec((D, tn), lambda j, i: (0, j)),
        out_shape=jax.ShapeDtypeStruct((D, ncol), F32),
        compiler_params=_cparams(("parallel", "arbitrary")),
    )(h, dp)


def _inproj_dx(dp, wperm, x, lnw, dyres, tm=256, tn=512):
    t = x.shape[0]
    ncol = dp.shape[1]
    nj = ncol // tn

    def body(dp_ref, w_ref, x_ref, lnw_ref, dy_ref, dx_ref, dlnw_ref, acc):
        j = pl.program_id(1)

        @pl.when(j == 0)
        def _():
            acc[...] = jnp.zeros_like(acc)

        acc[...] += _bdot(dp_ref[...], w_ref[...], 1, 1)

        @pl.when(j == nj - 1)
        def _():
            _, vjp = jax.vjp(_rms, x_ref[...], lnw_ref[...])
            dx, dlnw = vjp(acc[...])
            dx_ref[...] = dx + dy_ref[...]

            @pl.when(pl.program_id(0) == 0)
            def _():
                dlnw_ref[...] = jnp.zeros_like(dlnw_ref)
            dlnw_ref[...] += jnp.broadcast_to(dlnw, dlnw_ref.shape)

    return pl.pallas_call(
        body, name="inproj_dx", grid=(t // tm, nj),
        in_specs=[pl.BlockSpec((tm, tn), lambda i, j: (i, j)),
                  pl.BlockSpec((D, tn), lambda i, j: (0, j)),
                  pl.BlockSpec((tm, D), lambda i, j: (i, 0)),
                  pl.BlockSpec((1, D), lambda i, j: (0, 0)),
                  pl.BlockSpec((tm, D), lambda i, j: (i, 0))],
        out_specs=[pl.BlockSpec((tm, D), lambda i, j: (i, 0)),
                   pl.BlockSpec((8, D), lambda i, j: (0, 0))],
        out_shape=[jax.ShapeDtypeStruct((t, D), F32), jax.ShapeDtypeStruct((8, D), F32)],
        scratch_shapes=[pltpu.VMEM((tm, D), F32)],
        compiler_params=_cparams(("arbitrary", "arbitrary")),
    )(dp, wperm, x, lnw, dyres)


def _qkv_fwd(p, convw, kind):
    t = p.shape[0]
    f = _qkv_act(kind)

    def body(p_ref, w_ref, o_ref):
        o_ref[...] = f(p_ref[...], w_ref[...])

    return pl.pallas_call(
        body, name=f"qkv_fwd{kind}", grid=(NA,),
        in_specs=[pl.BlockSpec((t, DA), lambda h: (0, kind * NA + h)),
                  pl.BlockSpec((8, DA), lambda h: (0, kind * NA + h))],
        out_specs=pl.BlockSpec((t, DA), lambda h: (0, h)),
        out_shape=jax.ShapeDtypeStruct((t, D), F32),
        compiler_params=_cparams(("parallel",)),
    )(p, convw)


def _qkv_bwd(p, convw, dout, kind):
    t = p.shape[0]
    f = _qkv_act(kind)

    def body(p_ref, w_ref, g_ref, dx_ref, dw_ref):
        _, vjp = jax.vjp(f, p_ref[...], w_ref[...])
        dx, dw = vjp(g_ref[...])
        dx_ref[...] = dx
        dw_ref[...] = dw

    return pl.pallas_call(
        body, name=f"qkv_bwd{kind}", grid=(NA,),
        in_specs=[pl.BlockSpec((t, DA), lambda h: (0, kind * NA + h)),
                  pl.BlockSpec((8, DA), lambda h: (0, kind * NA + h)),
                  pl.BlockSpec((t, DA), lambda h: (0, h))],
        out_specs=[pl.BlockSpec((t, DA), lambda h: (0, h)),
                   pl.BlockSpec((8, DA), lambda h: (0, h))],
        out_shape=[jax.ShapeDtypeStruct((t, D), F32), jax.ShapeDtypeStruct((8, D), F32)],
        compiler_params=_cparams(("parallel",)),
    )(p, convw, dout)


def _gates_f(ps, alog_row, dt_row, w2f, b2f, w2b, b2b):
    lane = lax.broadcasted_iota(jnp.int32, ps.shape, 1)
    lg = -jnp.exp(alog_row) * _softplus(ps + dt_row)
    gsm = jnp.where(lane < 16, lg, jnp.where(lane < 32, _sigmoid(ps), 0.0))
    gkf = -_softplus(-(mm(ps, w2f) + b2f)) * (1.0 / 16.0)
    gkb = -_softplus(-(mm(ps, w2b) + b2b)) * (1.0 / 16.0)
    return gsm, gkf, gkb


def _gates_fwd(ps, alog_row, dt_row, w2f, b2f, w2b, b2b, tm=512):
    t = ps.shape[0]

    def body(ps_ref, a_ref, d_ref, wf_ref, bf_ref, wb_ref, bb_ref, gsm_ref, gk_ref):
        gsm, gkf, gkb = _gates_f(ps_ref[...], a_ref[...], d_ref[...], wf_ref[...], bf_ref[...],
                                 wb_ref[...], bb_ref[...])
        gsm_ref[...] = gsm
        gk_ref[0] = gkf
        gk_ref[1] = gkb

    row = lambda n: pl.BlockSpec((1, n), lambda i: (0, 0))
    mat = pl.BlockSpec((128, 512), lambda i: (0, 0))
    return pl.pallas_call(
        body, name="gates_fwd", grid=(t // tm,),
        in_specs=[pl.BlockSpec((tm, 128), lambda i: (i, PS_BLOCK)), row(128), row(128), mat, row(512), mat, row(512)],
        out_specs=[pl.BlockSpec((tm, 128), lambda i: (i, 0)),
                   pl.BlockSpec((2, tm, 512), lambda i: (0, i, 0))],
        out_shape=[jax.ShapeDtypeStruct((t, 128), F32), jax.ShapeDtypeStruct((2, t, 512), F32)],
        compiler_params=_cparams(("parallel",)),
    )(ps, alog_row, dt_row, w2f, b2f, w2b, b2b)


def _gates_bwd(ps, alog_row, dt_row, w2f, b2f, w2b, b2b, dgsm, dgk, tm=512):
    t = ps.shape[0]

    def body(ps_ref, a_ref, d_ref, wf_ref, bf_ref, wb_ref, bb_ref, dgsm_ref, dgk_ref,
             dps_ref, da_ref, dd_ref, dwf_ref, dbf_ref, dwb_ref, dbb_ref):
        _, vjp = jax.vjp(_gates_f, ps_ref[...], a_ref[...], d_ref[...], wf_ref[...], bf_ref[...],
                         wb_ref[...], bb_ref[...])
        dps, da, dd, dwf, dbf, dwb, dbb = vjp((dgsm_ref[...], dgk_ref[0], dgk_ref[1]))
        dps_ref[...] = dps
        accs = ((da_ref, da), (dd_ref, dd), (dwf_ref, dwf), (dbf_ref, dbf), (dwb_ref, dwb), (dbb_ref, dbb))

        @pl.when(pl.program_id(0) == 0)
        def _():
            for ref, _ in accs:
                ref[...] = jnp.zeros_like(ref)
        for ref, val in accs:
            ref[...] += jnp.broadcast_to(val, ref.shape)

    row = lambda n: pl.BlockSpec((1, n), lambda i: (0, 0))
    row8 = lambda n: pl.BlockSpec((8, n), lambda i: (0, 0))
    mat = pl.BlockSpec((128, 512), lambda i: (0, 0))
    return pl.pallas_call(
        body, name="gates_bwd", grid=(t // tm,),
        in_specs=[pl.BlockSpec((tm, 128), lambda i: (i, PS_BLOCK)), row(128), row(128), mat, row(512), mat, row(512),
                  pl.BlockSpec((tm, 128), lambda i: (i, 0)),
                  pl.BlockSpec((2, tm, 512), lambda i: (0, i, 0))],
        out_specs=[pl.BlockSpec((tm, 128), lambda i: (i, 0)), row8(128), row8(128), mat, row8(512), mat, row8(512)],
        out_shape=[jax.ShapeDtypeStruct((t, 128), F32),
                   jax.ShapeDtypeStruct((8, 128), F32), jax.ShapeDtypeStruct((8, 128), F32),
                   jax.ShapeDtypeStruct((128, 512), F32), jax.ShapeDtypeStruct((8, 512), F32),
                   jax.ShapeDtypeStruct((128, 512), F32), jax.ShapeDtypeStruct((8, 512), F32)],
        compiler_params=_cparams(("arbitrary",)),
    )(ps, alog_row, dt_row, w2f, b2f, w2b, b2b, dgsm, dgk)


def _rows(i):
    return pl.ds(pl.multiple_of(i * CH, CH), CH)


def _gcum_f(gsm, tm):
    i = lax.broadcasted_iota(jnp.int32, (tm, tm), 0)
    j = lax.broadcasted_iota(jnp.int32, (tm, tm), 1)
    same = (i >> 6) == (j >> 6)
    lower = (same & (i >= j)).astype(F32)
    upper = (same & (i <= j)).astype(F32)
    r = lax.broadcasted_iota(jnp.int32, (128, D), 0)
    head = lax.broadcasted_iota(jnp.int32, (128, D), 1) >> 7
    pick = lambda off: (r == head + off).astype(F32)
    lane = lax.broadcasted_iota(jnp.int32, gsm.shape, 1)
    run = jnp.where(lane < 8, cmm(lower, gsm), cmm(upper, gsm))
    return mmc(run, pick(0)), mmc(run, pick(8)), mmc(gsm, pick(16)), mmc(gsm, pick(24))


def _gcum_fwd(gsm, tm=256):
    t = gsm.shape[0]

    def body(s_ref, g_ref, b_ref):
        gf, gb, bf, bb = _gcum_f(s_ref[...], tm)
        g_ref[0] = gf
        g_ref[1] = gb
        b_ref[0] = bf
        b_ref[1] = bb

    two = pl.BlockSpec((2, tm, D), lambda i: (0, i, 0))
    return pl.pallas_call(
        body, name="gcum_fwd", grid=(t // tm,),
        in_specs=[pl.BlockSpec((tm, 128), lambda i: (i, 0))], out_specs=[two, two],
        out_shape=[jax.ShapeDtypeStruct((2, t, D), F32)] * 2,
        compiler_params=_cparams(("parallel",)),
    )(gsm)


def _gcum_bwd(gsm, dg2, db2, tm=256):
    t = gsm.shape[0]

    def body(s_ref, dg_ref, db_ref, ds_ref):
        _, vjp = jax.vjp(lambda s: _gcum_f(s, tm), s_ref[...])
        ds_ref[...] = vjp((dg_ref[0], dg_ref[1], db_ref[0], db_ref[1]))[0]

    two = pl.BlockSpec((2, tm, D), lambda i: (0, i, 0))
    tile = pl.BlockSpec((tm, 128), lambda i: (i, 0))
    return pl.pallas_call(
        body, name="gcum_bwd", grid=(t // tm,),
        in_specs=[tile, two, two], out_specs=tile,
        out_shape=jax.ShapeDtypeStruct((t, 128), F32),
        compiler_params=_cparams(("parallel",)),
    )(gsm, dg2, db2)


def _gdn_intra_fwd(qn, kn, vc, g2, b2):
    t = qn.shape[0]
    n = t // CH

    def body(q_ref, k_ref, v_ref, g_ref, b_ref, u_ref, w_ref, a_ref, qd_ref, kd_ref, e_ref, t_ref):
        cs = _Consts(pl.program_id(0) == 1)

        def step(i, carry):
            r = _rows(i)
            q, k, v, g, g64, bx = q_ref[r, :], k_ref[r, :], v_ref[r, :], g_ref[r, :], g_ref[r, 0:CH], b_ref[r, :]
            tinv = _tri_inv(_gdn_low(k, g64, bx, cs), cs.eye)
            u, w, a, qd, kd, el = _gdn_intra(q, k, v, g, g64, bx, tinv, cs)
            u_ref[r, :] = u
            w_ref[r, :] = w
            a_ref[r, :] = a
            qd_ref[r, :] = qd
            kd_ref[r, :] = kd
            t_ref[r, :] = tinv
            e_ref[i] = jnp.broadcast_to(el, (8, 128))
            return carry

        lax.fori_loop(0, n, step, 0, unroll=2)

    head = pl.BlockSpec((t, DA), lambda d, h: (0, h))
    dh = pl.BlockSpec((None, t, DA), lambda d, h: (d, 0, h))
    sq = pl.BlockSpec((None, None, t, CH), lambda d, h: (d, h, 0, 0))
    big = jax.ShapeDtypeStruct((2, t, D), F32)
    sqs = jax.ShapeDtypeStruct((2, NA, t, CH), F32)
    return pl.pallas_call(
        body, name="gdn_intra_fwd", grid=(2, NA),
        in_specs=[head, head, head, dh, dh],
        out_specs=[dh, dh, sq, dh, dh, pl.BlockSpec((None, None, n, 8, 128), lambda d, h: (d, h, 0, 0, 0)), sq],
        out_shape=[big, big, sqs, big, big, jax.ShapeDtypeStruct((2, NA, n, 8, 128), F32), sqs],
        compiler_params=_cparams(("parallel", "parallel")),
    )(qn, kn, vc, g2, b2)


def _gdn_scan_fwd(u, w, a, qd, kd, e):
    t = u.shape[1]
    n = t // CH

    def body(u_ref, w_ref, a_ref, qd_ref, kd_ref, e_ref, o_ref, s_ref):
        rev = pl.program_id(0) == 1

        def step(i, s):
            ci = jnp.where(rev, n - 1 - i, i)
            r = _rows(ci)
            s_ref[ci] = s
            o, sn = _gdn_scan(u_ref[r, :], w_ref[r, :], a_ref[r, :], qd_ref[r, :], kd_ref[r, :],
                              e_ref[ci][0:1], s)
            o_ref[r, :] = o
            return sn

        lax.fori_loop(0, n, step, jnp.zeros((DA, DA), F32))

    dh = pl.BlockSpec((None, t, DA), lambda d, h: (d, 0, h))
    return pl.pallas_call(
        body, name="gdn_scan_fwd", grid=(2, NA),
        in_specs=[dh, dh, pl.BlockSpec((None, None, t, CH), lambda d, h: (d, h, 0, 0)), dh, dh,
                  pl.BlockSpec((None, None, n, 8, 128), lambda d, h: (d, h, 0, 0, 0))],
        out_specs=[dh, pl.BlockSpec((None, None, n, DA, DA), lambda d, h: (d, h, 0, 0, 0))],
        out_shape=[jax.ShapeDtypeStruct((2, t, D), F32), jax.ShapeDtypeStruct((2, NA, n, DA, DA), F32)],
        compiler_params=_cparams(("parallel", "parallel")),
    )(u, w, a, qd, kd, e)


def _gdn_scan_bwd(u, w, a, qd, kd, e, ssave, do):
    t = u.shape[1]
    n = t // CH

    def body(u_ref, w_ref, a_ref, qd_ref, kd_ref, e_ref, s_ref, do_ref,
             du_ref, dw_ref, da_ref, dqd_ref, dkd_ref, de_ref):
        rev = pl.program_id(0) == 1

        def step(i, ds):
            ci = jnp.where(rev, i, n - 1 - i)
            r = _rows(ci)
            _, vjp = jax.vjp(_gdn_scan, u_ref[r, :], w_ref[r, :], a_ref[r, :], qd_ref[r, :], kd_ref[r, :],
                             e_ref[ci][0:1], s_ref[ci])
            du, dw, da, dqd, dkd, de, dsn = vjp((do_ref[r, :], ds))
            du_ref[r, :] = du
            dw_ref[r, :] = dw
            da_ref[r, :] = da
            dqd_ref[r, :] = dqd
            dkd_ref[r, :] = dkd
            de_ref[ci] = jnp.broadcast_to(de, (8, 128))
            return dsn

        lax.fori_loop(0, n, step, jnp.zeros((DA, DA), F32))

    dh = pl.BlockSpec((None, t, DA), lambda d, h: (d, 0, h))
    at = pl.BlockSpec((None, None, t, CH), lambda d, h: (d, h, 0, 0))
    et = pl.BlockSpec((None, None, n, 8, 128), lambda d, h: (d, h, 0, 0, 0))
    big = jax.ShapeDtypeStruct((2, t, D), F32)
    return pl.pallas_call(
        body, name="gdn_scan_bwd", grid=(2, NA),
        in_specs=[dh, dh, at, dh, dh, et,
                  pl.BlockSpec((None, None, n, DA, DA), lambda d, h: (d, h, 0, 0, 0)),
                  pl.BlockSpec((t, DA), lambda d, h: (0, h))],
        out_specs=[dh, dh, at, dh, dh, et],
        out_shape=[big, big, jax.ShapeDtypeStruct((2, NA, t, CH), F32), big, big,
                   jax.ShapeDtypeStruct((2, NA, n, 8, 128), F32)],
        compiler_params=_cparams(("parallel", "parallel")),
    )(u, w, a, qd, kd, e, ssave, do)


def _gdn_intra_bwd(qn, kn, vc, g2, b2, tinv, du, dw, da, dqd, dkd, de):
    t = qn.shape[0]
    n = t // CH

    def body(q_ref, k_ref, v_ref, g_ref, b_ref, t_ref, du_ref, dw_ref, da_ref, dqd_ref, dkd_ref, de_ref,
             dq_ref, dk_ref, dv_ref, dg_ref, db_ref):
        d = pl.program_id(1)
        cs = _Consts(d == 1)
        lane0 = lax.broadcasted_iota(jnp.int32, (CH, DA), 1) == 0

        @pl.when(d == 0)
        def _():
            dq_ref[...] = jnp.zeros_like(dq_ref)
            dk_ref[...] = jnp.zeros_like(dk_ref)
            dv_ref[...] = jnp.zeros_like(dv_ref)

        def step(i, carry):
            r = _rows(i)
            tinv_c = t_ref[r, :]
            f = lambda q, k, v, g, g64, bx: _gdn_intra(q, k, v, g, g64, bx, tinv_c, cs)
            _, vjp = jax.vjp(f, q_ref[r, :], k_ref[r, :], v_ref[r, :], g_ref[r, :], g_ref[r, 0:CH], b_ref[r, :])
            dq, dk, dv, dg, dg64, dbx = vjp((du_ref[r, :], dw_ref[r, :], da_ref[r, :], dqd_ref[r, :],
                                             dkd_ref[r, :], de_ref[i][0:1]))
            dq_ref[r, :] += dq
            dk_ref[r, :] += dk
            dv_ref[r, :] += dv
            dg_ref[r, :] = dg + jnp.where(lane0, jnp.sum(dg64, axis=1, keepdims=True), 0.0)
            db_ref[r, :] = dbx
            return carry

        lax.fori_loop(0, n, step, 0, unroll=2)

    head = pl.BlockSpec((t, DA), lambda h, d: (0, h))
    dh = pl.BlockSpec((None, t, DA), lambda h, d: (d, 0, h))
    sq = pl.BlockSpec((None, None, t, CH), lambda h, d: (d, h, 0, 0))
    full = jax.ShapeDtypeStruct((t, D), F32)
    big = jax.ShapeDtypeStruct((2, t, D), F32)
    return pl.pallas_call(
        body, name="gdn_intra_bwd", grid=(NA, 2),
        in_specs=[head, head, head, dh, dh, sq, dh, dh, sq, dh, dh,
                  pl.BlockSpec((None, None, n, 8, 128), lambda h, d: (d, h, 0, 0, 0))],
        out_specs=[head, head, head, dh, dh],
        out_shape=[full, full, full, big, big],
        compiler_params=_cparams(("arbitrary", "arbitrary")),
    )(qn, kn, vc, g2, b2, tinv, du, dw, da, dqd, dkd, de)


def _gla_specs(t, order):
    ix = (lambda d, h: (d, h)) if order == "dh" else (lambda h, d: (d, h))

    def mk(fn):
        return lambda a, b: fn(*ix(a, b))
    q = pl.BlockSpec((t, DKB), mk(lambda d, h: (0, 32 + h)))
    k = pl.BlockSpec((t, DKB), mk(lambda d, h: (0, 36 + h)))
    v = pl.BlockSpec((t, DVB), mk(lambda d, h: (0, 20 + h)))
    dk = pl.BlockSpec((None, t, DKB), mk(lambda d, h: (d, 0, h)))
    dv = pl.BlockSpec((None, t, DVB), mk(lambda d, h: (d, 0, h)))
    e = pl.BlockSpec((None, None, t // CH, 8, 128), mk(lambda d, h: (d, h, 0, 0, 0)))
    s = pl.BlockSpec((None, None, t // CH, DVB, DKB), mk(lambda d, h: (d, h, 0, 0, 0)))
    return q, k, v, dk, dv, e, s


def _gla_intra_fwd(p, gk):
    t = p.shape[0]
    n = t // CH

    def body(q_ref, k_ref, v_ref, g_ref, qg_ref, kd_ref, in_ref, e_ref):
        cs = _Consts(pl.program_id(0) == 1)

        def step(i, carry):
            r = _rows(i)
            qg, kd, intra, el = _gla_intra(q_ref[r, :], k_ref[r, :], v_ref[r, :], g_ref[r, :], cs)
            qg_ref[r, :] = qg
            kd_ref[r, :] = kd
            in_ref[r, :] = intra
            e_ref[i] = jnp.broadcast_to(el, (8, 128))
            return carry

        lax.fori_loop(0, n, step, 0)

    q, k, v, dk, dv, e, _ = _gla_specs(t, "dh")
    return pl.pallas_call(
        body, name="gla_intra_fwd", grid=(2, NB),
        in_specs=[q, k, v, dk], out_specs=[dk, dk, dv, e],
        out_shape=[jax.ShapeDtypeStruct((2, t, NB * DKB), F32), jax.ShapeDtypeStruct((2, t, NB * DKB), F32),
                   jax.ShapeDtypeStruct((2, t, D), F32), jax.ShapeDtypeStruct((2, NB, n, 8, 128), F32)],
        compiler_params=_cparams(("parallel", "parallel")),
    )(p, p, p, gk)


def _gla_scan_fwd(p, qg, kd, intra, e):
    t = p.shape[0]
    n = t // CH

    def body(v_ref, qg_ref, kd_ref, in_ref, e_ref, o_ref, s_ref):
        rev = pl.program_id(0) == 1

        def step(i, st):
            ci = jnp.where(rev, n - 1 - i, i)
            r = _rows(ci)
            s_ref[ci] = st
            o, stn = _gla_scan(qg_ref[r, :], kd_ref[r, :], v_ref[r, :], e_ref[ci][0:1], st)
            o_ref[r, :] = o + in_ref[r, :]
            return stn

        lax.fori_loop(0, n, step, jnp.zeros((DVB, DKB), F32))

    _, _, v, dk, dv, e_s, s = _gla_specs(t, "dh")
    return pl.pallas_call(
        body, name="gla_scan_fwd", grid=(2, NB),
        in_specs=[v, dk, dk, dv, e_s], out_specs=[dv, s],
        out_shape=[jax.ShapeDtypeStruct((2, t, D), F32), jax.ShapeDtypeStruct((2, NB, n, DVB, DKB), F32)],
        compiler_params=_cparams(("parallel", "parallel")),
    )(p, qg, kd, intra, e)


def _gla_scan_bwd(p, qg, kd, e, ssave, do):
    t = p.shape[0]
    n = t // CH

    def body(v_ref, qg_ref, kd_ref, e_ref, s_ref, do_ref, dqg_ref, dkd_ref, dv_ref, de_ref):
        rev = pl.program_id(0) == 1

        def step(i, dst):
            ci = jnp.where(rev, i, n - 1 - i)
            r = _rows(ci)
            _, vjp = jax.vjp(_gla_scan, qg_ref[r, :], kd_ref[r, :], v_ref[r, :], e_ref[ci][0:1], s_ref[ci])
            dqg, dkd, dv, de, dstn = vjp((do_ref[r, :], dst))
            dqg_ref[r, :] = dqg
            dkd_ref[r, :] = dkd
            dv_ref[r, :] = dv
            de_ref[ci] = jnp.broadcast_to(de, (8, 128))
            return dstn

        lax.fori_loop(0, n, step, jnp.zeros((DVB, DKB), F32))

    _, _, v, dk, dv, e_s, s = _gla_specs(t, "dh")
    return pl.pallas_call(
        body, name="gla_scan_bwd", grid=(2, NB),
        in_specs=[v, dk, dk, e_s, s, pl.BlockSpec((t, DVB), lambda d, h: (0, h))],
        out_specs=[dk, dk, dv, e_s],
        out_shape=[jax.ShapeDtypeStruct((2, t, NB * DKB), F32), jax.ShapeDtypeStruct((2, t, NB * DKB), F32),
                   jax.ShapeDtypeStruct((2, t, D), F32), jax.ShapeDtypeStruct((2, NB, n, 8, 128), F32)],
        compiler_params=_cparams(("parallel", "parallel")),
    )(p, qg, kd, e, ssave, do)


def _gla_intra_bwd(p, gk, dqg, dkd, dvs, de, do):
    t = p.shape[0]
    n = t // CH

    def body(q_ref, k_ref, v_ref, g_ref, dqg_ref, dkd_ref, dvs_ref, de_ref, do_ref,
             dq_ref, dk_ref, dv_ref, dg_ref):
        d = pl.program_id(1)
        cs = _Consts(d == 1)

        @pl.when(d == 0)
        def _():
            dq_ref[...] = jnp.zeros_like(dq_ref)
            dk_ref[...] = jnp.zeros_like(dk_ref)
            dv_ref[...] = jnp.zeros_like(dv_ref)

        def step(i, carry):
            r = _rows(i)
            f = lambda q, k, v, g: _gla_intra(q, k, v, g, cs)
            _, vjp = jax.vjp(f, q_ref[r, :], k_ref[r, :], v_ref[r, :], g_ref[r, :])
            dq, dk, dv, dg = vjp((dqg_ref[r, :], dkd_ref[r, :], do_ref[r, :], de_ref[i][0:1]))
            dq_ref[r, :] += dq
            dk_ref[r, :] += dk
            dv_ref[r, :] += dv + dvs_ref[r, :]
            dg_ref[r, :] = dg
            return carry

        lax.fori_loop(0, n, step, 0)

    q, k, v, dk, dv, e_s, _ = _gla_specs(t, "hd")
    hk = pl.BlockSpec((t, DKB), lambda h, d: (0, h))
    hv = pl.BlockSpec((t, DVB), lambda h, d: (0, h))
    return pl.pallas_call(
        body, name="gla_intra_bwd", grid=(NB, 2),
        in_specs=[q, k, v, dk, dk, dk, dv, e_s, hv],
        out_specs=[hk, hk, hv, dk],
        out_shape=[jax.ShapeDtypeStruct((t, NB * DKB), F32), jax.ShapeDtypeStruct((t, NB * DKB), F32),
                   jax.ShapeDtypeStruct((t, D), F32), jax.ShapeDtypeStruct((2, t, NB * DKB), F32)],
        compiler_params=_cparams(("arbitrary", "arbitrary")),
    )(p, p, p, gk, dqg, dkd, dvs, de, do)


def _seg_gate(o, z, w):
    return _rms(o, w) * _silu(z)


def _seg_merge(ya, yb, ga, gb):
    return _sigmoid(ga) * ya + _sigmoid(gb) * yb


def _seg_loss(out, x, tgt, w):
    err = x + _rms(out, w) - tgt
    return 0.5 * jnp.sum(jnp.mean(err * err, axis=-1, keepdims=True), axis=0, keepdims=True)


def _post(oa2, ob2, p, x, tgt, gdn_w, gla_w, lnpost, w3, tm=128):
    t = x.shape[0]

    def body(oa_ref, ob_ref, z_ref, gb_ref, ga_ref, gB_ref, x_ref, t_ref, aw_ref, bw_ref, lw_ref, w_ref,
             loss_ref, doa_ref, dob_ref, dz_ref, dgb_ref, dga_ref, dgB_ref, dy_ref,
             dw_ref, daw_ref, dbw_ref, dlw_ref):
        first = pl.program_id(0) == 0
        oa = oa_ref[0] + oa_ref[1]
        ob = ob_ref[0] + ob_ref[1]
        z, gb = z_ref[...], gb_ref[...]
        aw, bw = aw_ref[...], bw_ref[...]

        pa = [jax.vjp(_seg_gate, oa[:, h * DA:(h + 1) * DA], z[:, h * DA:(h + 1) * DA], aw) for h in range(NA)]
        pb = [jax.vjp(_seg_gate, ob[:, h * DVB:(h + 1) * DVB], gb[:, h * DVB:(h + 1) * DVB], bw)
              for h in range(NB)]
        a1 = jnp.concatenate([v for v, _ in pa], axis=1).astype(BF16)
        a2 = jnp.concatenate([v for v, _ in pb], axis=1).astype(BF16)
        ya = jnp.dot(a1, w_ref[0], preferred_element_type=F32)
        yb = jnp.dot(a2, w_ref[1], preferred_element_type=F32)
        merged, vjp_m = jax.vjp(_seg_merge, ya, yb, ga_ref[...], gB_ref[...])
        mb = merged.astype(BF16)
        out = jnp.dot(mb, w_ref[2], preferred_element_type=F32)
        loss, vjp_l = jax.vjp(_seg_loss, out, x_ref[...], t_ref[...], lw_ref[...])
        dout, dyres, _, dlw = vjp_l(jnp.ones((1, 1), F32))
        dy_ref[...] = dyres
        doutb = dout.astype(BF16)
        dmerged = _bdot(doutb, w_ref[2], 1, 1)
        dya, dyb, dga, dgB = vjp_m(dmerged)
        dga_ref[...] = dga
        dgB_ref[...] = dgB
        dyab, dybb = dya.astype(BF16), dyb.astype(BF16)
        da1 = _bdot(dyab, w_ref[0], 1, 1)
        da2 = _bdot(dybb, w_ref[1], 1, 1)

        daw = jnp.zeros_like(aw)
        for h in range(NA):
            sl = slice(h * DA, (h + 1) * DA)
            do, dz, dw = pa[h][1](da1[:, sl])
            doa_ref[:, sl] = do
            dz_ref[:, sl] = dz
            daw = daw + dw
        dbw = jnp.zeros_like(bw)
        for h in range(NB):
            sl = slice(h * DVB, (h + 1) * DVB)
            do, dg, dw = pb[h][1](da2[:, sl])
            dob_ref[:, sl] = do
            dgb_ref[:, sl] = dg
            dbw = dbw + dw

        @pl.when(first)
        def _():
            loss_ref[...] = jnp.zeros_like(loss_ref)
            dw_ref[...] = jnp.zeros_like(dw_ref)
            daw_ref[...] = jnp.zeros_like(daw_ref)
            dbw_ref[...] = jnp.zeros_like(dbw_ref)
            dlw_ref[...] = jnp.zeros_like(dlw_ref)

        loss_ref[...] += jnp.broadcast_to(loss, loss_ref.shape)
        dw_ref[0] += _bdot(a1, dyab, 0, 0)
        dw_ref[1] += _bdot(a2, dybb, 0, 0)
        dw_ref[2] += _bdot(mb, doutb, 0, 0)
        daw_ref[...] += jnp.broadcast_to(daw, daw_ref.shape)
        dbw_ref[...] += jnp.broadcast_to(dbw, dbw_ref.shape)
        dlw_ref[...] += jnp.broadcast_to(dlw, dlw_ref.shape)

    two = pl.BlockSpec((2, tm, D), lambda i: (0, i, 0))
    pcol = lambda c: pl.BlockSpec((tm, D), lambda i: (i, c))
    tok = pl.BlockSpec((tm, D), lambda i: (i, 0))
    row = lambda n: pl.BlockSpec((1, n), lambda i: (0, 0))
    row8 = lambda n: pl.BlockSpec((8, n), lambda i: (0, 0))
    once = pl.Buffered(1)
    tokf = jax.ShapeDtypeStruct((t, D), F32)
    return pl.pallas_call(
        body, name="post", grid=(t // tm,),
        in_specs=[two, two, pcol(3), pcol(6), pcol(7), pcol(8), tok, tok, row(DA), row(DVB), row(D),
                  pl.BlockSpec((3, D, D), lambda i: (0, 0, 0), pipeline_mode=once)],
        out_specs=[row8(128), tok, tok, tok, tok, tok, tok, tok,
                   pl.BlockSpec((3, D, D), lambda i: (0, 0, 0), pipeline_mode=once),
                   row8(DA), row8(DVB), row8(D)],
        out_shape=[jax.ShapeDtypeStruct((8, 128), F32), tokf, tokf, tokf, tokf, tokf, tokf, tokf,
                   jax.ShapeDtypeStruct((3, D, D), F32),
                   jax.ShapeDtypeStruct((8, DA), F32), jax.ShapeDtypeStruct((8, DVB), F32),
                   jax.ShapeDtypeStruct((8, D), F32)],
        compiler_params=_cparams(("arbitrary",), vmem_mb=56),
    )(oa2, ob2, p, p, p, p, x, tgt, gdn_w, gla_w, lnpost, w3)


def _adam(w, g, m, v, tr):
    rows, cols = w.shape

    def body(w_ref, g_ref, m_ref, v_ref, d_ref, nm_ref, nv_ref):
        gg = g_ref[...]
        nm = B1 * m_ref[...] + (1.0 - B1) * gg
        nv = B2 * v_ref[...] + (1.0 - B2) * (gg * gg)
        m_hat = nm / (1.0 - B1 ** STEP)
        v_hat = nv / (1.0 - B2 ** STEP)
        d_ref[...] = -LR * (m_hat / (jnp.sqrt(v_hat) + ADAM_EPS) + WD * w_ref[...])
        nm_ref[...] = nm
        nv_ref[...] = nv

    blk = pl.BlockSpec((tr, cols), lambda i: (i, 0))
    shp = jax.ShapeDtypeStruct((rows, cols), F32)
    return pl.pallas_call(
        body, name=f"adam_{rows}x{cols}", grid=(rows // tr,),
        in_specs=[blk] * 4, out_specs=[blk] * 3, out_shape=[shp] * 3,
        compiler_params=_cparams(("parallel",)),
    )(w, g, m, v)


def _sum_cast(own, got):
    _, ns, r, c = own.shape
    tr = r // 4 if r >= 64 else r

    def body(c_ref, a_ref, b_ref, f_ref, h_ref):
        s = a_ref[...] + b_ref[...]
        f_ref[...] = s
        h_ref[...] = s.astype(BF16)

    return pl.pallas_call(
        body, name=f"sum_cast_{r}x{c}",
        grid_spec=pltpu.PrefetchScalarGridSpec(
            num_scalar_prefetch=1, grid=(ns, r // tr),
            in_specs=[pl.BlockSpec((None, None, tr, c), lambda s, i, cc: (cc[0], s, i, 0)),
                      pl.BlockSpec((None, tr, c), lambda s, i, cc: (s, i, 0))],
            out_specs=[pl.BlockSpec((None, tr, c), lambda s, i, cc: (s, i, 0)),
                       pl.BlockSpec((None, tr, c), lambda s, i, cc: (s, i, 0))]),
        out_shape=[jax.ShapeDtypeStruct((ns, r, c), F32), jax.ShapeDtypeStruct((ns, r, c), BF16)],
        compiler_params=_cparams(("parallel", "parallel")),
    )(lax.axis_index("c").reshape(1), own, got)


def _sum4(mine, got):
    _, r, c = mine.shape
    tr = r // 4 if r >= 64 else r

    def body(s_ref, a_ref, g_ref, o_ref):
        acc = a_ref[...] + g_ref[0].astype(F32)
        acc = acc + g_ref[1].astype(F32)
        o_ref[...] = acc + g_ref[2].astype(F32)

    shard = (2 * lax.axis_index("x") + lax.axis_index("y")).reshape(1)
    return pl.pallas_call(
        body, name=f"sum4_{r}x{c}",
        grid_spec=pltpu.PrefetchScalarGridSpec(
            num_scalar_prefetch=1, grid=(r // tr,),
            in_specs=[pl.BlockSpec((None, tr, c), lambda i, ss: (ss[0], i, 0)),
                      pl.BlockSpec((3, tr, c), lambda i, ss: (0, i, 0))],
            out_specs=pl.BlockSpec((tr, c), lambda i, ss: (i, 0))),
        out_shape=jax.ShapeDtypeStruct((r, c), F32),
        compiler_params=_cparams(("parallel",)),
    )(shard, mine, got)


def _place():
    x, y, c = lax.axis_index("x"), lax.axis_index("y"), lax.axis_index("c")
    chips = [(1 - x, y), (x, 1 - y), (1 - x, 1 - y)]
    return x, y, c, chips


def _gather_weights(parts):
    npart = len(parts)

    def body(*refs):
        ins, outs = refs[:npart], refs[npart:2 * npart]
        send_sems, recv_sems, local_sems = refs[2 * npart:]
        x, y, c, chips = _place()
        sibling = (x, y, 1 - c)
        mine = 2 * x + y

        def remote(k, p, shard, half, to, src=None):
            dst = outs[p].at[shard, half]
            return pltpu.make_async_remote_copy(
                src_ref=dst if src is None else src, dst_ref=dst,
                send_sem=send_sems.at[k], recv_sem=recv_sems.at[k], device_id=to, device_id_type=MESH)

        local = [pltpu.make_async_copy(ins[p], outs[p].at[mine], local_sems.at[p]) for p in range(npart)]
        for cp in local:
            cp.start()
        first = [remote(j * npart + p, p, mine, c, (*chip, c), src=ins[p].at[c])
                 for j, chip in enumerate(chips) for p in range(npart)]
        for cp in first:
            cp.start()
        passed = []
        for j, (cx, cy) in enumerate(chips):
            for p in range(npart):
                remote(j * npart + p, p, 2 * cx + cy, c, (x, y, c)).wait_recv()
                fw = remote((3 + j) * npart + p, p, 2 * cx + cy, c, sibling)
                fw.start()
                passed.append(fw)
        for j, (cx, cy) in enumerate(chips):
            for p in range(npart):
                remote((3 + j) * npart + p, p, 2 * cx + cy, 1 - c, (x, y, c)).wait_recv()
        for cp in first + passed:
            cp.wait_send()
        for cp in local:
            cp.wait()

    return pl.pallas_call(
        body, name="gather_weights",
        in_specs=[ANY] * npart, out_specs=[ANY] * npart,
        out_shape=[jax.ShapeDtypeStruct((NSHARD,) + a.shape, a.dtype) for a in parts],
        scratch_shapes=[pltpu.SemaphoreType.DMA((6 * npart,)), pltpu.SemaphoreType.DMA((6 * npart,)),
                        pltpu.SemaphoreType.DMA((npart,))],
    )(*parts)


def _swap_halves(parts):
    npart = len(parts)

    def body(*refs):
        ins, outs = refs[:npart], refs[npart:2 * npart]
        send_sems, recv_sems = refs[2 * npart:]
        x, y, c, _ = _place()
        cps = [pltpu.make_async_remote_copy(
            src_ref=ins[p].at[1 - c], dst_ref=outs[p], send_sem=send_sems.at[p], recv_sem=recv_sems.at[p],
            device_id=(x, y, 1 - c), device_id_type=MESH) for p in range(npart)]
        for cp in cps:
            cp.start()
        for cp in cps:
            cp.wait()

    return pl.pallas_call(
        body, name="swap_halves", in_specs=[ANY] * npart, out_specs=[ANY] * npart,
        out_shape=[jax.ShapeDtypeStruct(a.shape[1:], a.dtype) for a in parts],
        scratch_shapes=[pltpu.SemaphoreType.DMA((npart,)), pltpu.SemaphoreType.DMA((npart,))],
    )(*parts)


def _scatter_shards(parts):
    npart = len(parts)

    def body(*refs):
        ins, outs = refs[:npart], refs[npart:2 * npart]
        send_sems, recv_sems = refs[2 * npart:]
        x, y, c, chips = _place()
        cps = [pltpu.make_async_remote_copy(
            src_ref=ins[p].at[2 * cx + cy], dst_ref=outs[p].at[j],
            send_sem=send_sems.at[j * npart + p], recv_sem=recv_sems.at[j * npart + p],
            device_id=(cx, cy, c), device_id_type=MESH)
            for j, (cx, cy) in enumerate(chips) for p in range(npart)]
        for cp in cps:
            cp.start()
        for cp in cps:
            cp.wait()

    return pl.pallas_call(
        body, name="scatter_shards", in_specs=[ANY] * npart, out_specs=[ANY] * npart,
        out_shape=[jax.ShapeDtypeStruct((3,) + a.shape[1:], a.dtype) for a in parts],
        scratch_shapes=[pltpu.SemaphoreType.DMA((3 * npart,)), pltpu.SemaphoreType.DMA((3 * npart,))],
    )(*parts)


def _join_halves(parts):
    npart = len(parts)

    def body(*refs):
        ins, outs = refs[:npart], refs[npart:2 * npart]
        send_sems, recv_sems, local_sems = refs[2 * npart:]
        x, y, c, _ = _place()
        local = [pltpu.make_async_copy(ins[p], outs[p].at[c], local_sems.at[p]) for p in range(npart)]
        cps = [pltpu.make_async_remote_copy(
            src_ref=ins[p], dst_ref=outs[p].at[c], send_sem=send_sems.at[p], recv_sem=recv_sems.at[p],
            device_id=(x, y, 1 - c), device_id_type=MESH) for p in range(npart)]
        for cp in local + cps:
            cp.start()
        for cp in cps:
            cp.wait_send()
        for p in range(npart):
            pltpu.make_async_remote_copy(
                src_ref=ins[p], dst_ref=outs[p].at[1 - c], send_sem=send_sems.at[p], recv_sem=recv_sems.at[p],
                device_id=(x, y, 1 - c), device_id_type=MESH).wait_recv()
        for cp in local:
            cp.wait()

    return pl.pallas_call(
        body, name="join_halves", in_specs=[ANY] * npart, out_specs=[ANY] * npart,
        out_shape=[jax.ShapeDtypeStruct((2,) + a.shape, a.dtype) for a in parts],
        scratch_shapes=[pltpu.SemaphoreType.DMA((npart,)), pltpu.SemaphoreType.DMA((npart,)),
                        pltpu.SemaphoreType.DMA((npart,))],
    )(*parts)


def _allreduce_small(v):
    r = v.shape[0]

    def body(v_ref, o_ref, buf, send_sems, recv_sems):
        x, y, c, _ = _place()
        me = 4 * x + 2 * y + c
        buf[me] = v_ref[...]
        cps = []
        for k in range(1, 8):
            px, py, pc = x ^ (k >> 2), y ^ ((k >> 1) & 1), c ^ (k & 1)
            cps.append(pltpu.make_async_remote_copy(
                src_ref=v_ref, dst_ref=buf.at[me], send_sem=send_sems.at[k - 1], recv_sem=recv_sems.at[k - 1],
                device_id=(px, py, pc), device_id_type=MESH))
        for cp in cps:
            cp.start()
        for k in range(1, 8):
            px, py, pc = x ^ (k >> 2), y ^ ((k >> 1) & 1), c ^ (k & 1)
            pltpu.make_async_remote_copy(
                src_ref=v_ref, dst_ref=buf.at[4 * px + 2 * py + pc], send_sem=send_sems.at[k - 1],
                recv_sem=recv_sems.at[k - 1], device_id=(px, py, pc), device_id_type=MESH).wait_recv()
        for cp in cps:
            cp.wait_send()
        acc = buf[0]
        for d in range(1, 8):
            acc = acc + buf[d]
        o_ref[...] = acc

    return pl.pallas_call(
        body, name="allreduce_small",
        in_specs=[pl.BlockSpec(memory_space=pltpu.VMEM)], out_specs=pl.BlockSpec(memory_space=pltpu.VMEM),
        out_shape=jax.ShapeDtypeStruct((r, 128), F32),
        scratch_shapes=[pltpu.VMEM((8, r, 128), F32), pltpu.SemaphoreType.DMA((7,)), pltpu.SemaphoreType.DMA((7,))],
    )(v)


def _permute_cols(w):
    zeros = jnp.zeros((w.shape[0], NPERM - 9280), w.dtype)
    return jnp.concatenate([w[:, 0:4096], w[:, 4128:7200], w[:, 7232:9280], w[:, 4096:4128], w[:, 7200:7232], zeros],
                           axis=1)


def _unpermute_cols(g):
    return jnp.concatenate([g[:, 0:4096], g[:, 9216:9248], g[:, 4096:7168], g[:, 9248:9280], g[:, 7168:9216]],
                           axis=1)


SMALL_NAMES = ("ln_pre_w", "a_log_fwd", "a_log_bwd", "dt_bias_fwd", "dt_bias_bwd", "gdn_norm_w",
               "gk_b2_fwd", "gk_b2_bwd", "gla_norm_w", "ln_post_w")
SMALL_SIZES = (1024, 8, 8, 8, 8, 128, 512, 512, 256, 1024)
SMALL_ROWS = 32


def _pack_small(vals):
    flat = jnp.concatenate([v.reshape(-1) for v in vals])
    return jnp.pad(flat, (0, SMALL_ROWS * 128 - flat.shape[0])).reshape(SMALL_ROWS, 128)


def _unpack_small(packed):
    flat = packed.reshape(-1)
    out, o = [], 0
    for n in SMALL_SIZES:
        out.append(flat[o:o + n].reshape(1, n))
        o += n
    return out


def _pack_shard_small(conv, w2f, w2b):
    top = jnp.pad(conv, ((0, 8 - conv.shape[0]), (0, 0)))
    mid = jnp.pad(jnp.concatenate([w2f, w2b], axis=1), ((0, 0), (0, 768 - 256)))
    return jnp.concatenate([top, mid, jnp.zeros((8, 768), conv.dtype)], axis=0)


def _unpack_shard_small(a):
    return a[0:5], a[8:24, 0:128], a[8:24, 128:256]


def kernel(x, ln_pre_w, w_in, conv_w, a_log_fwd, a_log_bwd, dt_bias_fwd, dt_bias_bwd, gdn_norm_w, w_proj_gdn, gk_w2_fwd, gk_b2_fwd, gk_w2_bwd, gk_b2_bwd, gla_norm_w, w_proj_gla, w_out, ln_post_w, loss_target, m_ln_pre_w, m_w_in, m_conv_w, m_a_log_fwd, m_a_log_bwd, m_dt_bias_fwd, m_dt_bias_bwd, m_gdn_norm_w, m_w_proj_gdn, m_gk_w2_fwd, m_gk_b2_fwd, m_gk_w2_bwd, m_gk_b2_bwd, m_gla_norm_w, m_w_proj_gla, m_w_out, m_ln_post_w, v_ln_pre_w, v_w_in, v_conv_w, v_a_log_fwd, v_a_log_bwd, v_dt_bias_fwd, v_dt_bias_bwd, v_gdn_norm_w, v_w_proj_gdn, v_gk_w2_fwd, v_gk_b2_fwd, v_gk_w2_bwd, v_gk_b2_bwd, v_gla_norm_w, v_w_proj_gla, v_w_out, v_ln_post_w):
    t = x.shape[1]
    x2, tgt = x[0], loss_target[0]

    win_l = w_in[0].astype(BF16).reshape(2, D // 2, SHW)
    proj_l = jnp.concatenate([w_proj_gdn[0], w_proj_gla[0], w_out[0]], axis=0).astype(BF16).reshape(2, 384, D)
    small_l = _pack_shard_small(conv_w[0], gk_w2_fwd[0], gk_w2_bwd[0]).reshape(2, 16, 768)
    win_g, proj_g, small_g = _gather_weights([win_l, proj_l, small_l])
    w_full = win_g.reshape(NSHARD, D, SHW).transpose(1, 0, 2).reshape(D, NSHARD * SHW)
    wperm = _permute_cols(w_full)
    w3 = proj_g.reshape(NSHARD, 3, D // NSHARD, D).transpose(1, 0, 2, 3).reshape(3, D, D)
    small_g = small_g.reshape(NSHARD, 32, 768)
    convw = small_g[:, 0:8, :].transpose(1, 0, 2).reshape(8, 3 * D)
    w2f = small_g[:, 8:24, 0:128].transpose(1, 0, 2).reshape(16, 512)
    w2b = small_g[:, 8:24, 128:256].transpose(1, 0, 2).reshape(16, 512)
    w2f_pad = jnp.pad(w2f, ((32, 80), (0, 0)))
    w2b_pad = jnp.pad(w2b, ((48, 64), (0, 0)))
    alog_row = jnp.pad(jnp.concatenate([a_log_fwd, a_log_bwd], axis=1), ((0, 0), (0, 112)))
    dt_row = jnp.pad(jnp.concatenate([dt_bias_fwd, dt_bias_bwd], axis=1), ((0, 0), (0, 112)))

    p, h = _inproj(x2, ln_pre_w, wperm)
    qn, kn, vc = (_qkv_fwd(p, convw, kind) for kind in range(3))
    gsm, gk = _gates_fwd(p, alog_row, dt_row, w2f_pad, gk_b2_fwd, w2b_pad, gk_b2_bwd)
    g2, b2 = _gcum_fwd(gsm)
    u, w, at, qd, kd, el, tinv = _gdn_intra_fwd(qn, kn, vc, g2, b2)
    oa2, sa = _gdn_scan_fwd(u, w, at, qd, kd, el)
    qg, kdb, intra, elb = _gla_intra_fwd(p, gk)
    ob2, sb = _gla_scan_fwd(p, qg, kdb, intra, elb)

    (loss8, doa, dob, dz, dgb, dga, dgB, dyres, dw3, dgdn_w, dgla_w, dlnpost) = _post(
        oa2, ob2, p, x2, tgt, gdn_norm_w, gla_norm_w, ln_post_w, w3)

    du, dw, dat, dqd, dkd, del_ = _gdn_scan_bwd(u, w, at, qd, kd, el, sa, doa)
    dqn, dkn, dvc, dg2, db2 = _gdn_intra_bwd(qn, kn, vc, g2, b2, tinv, du, dw, dat, dqd, dkd, del_)
    dgsm = _gcum_bwd(gsm, dg2, db2)
    dqg, dkdb, dvs, delb = _gla_scan_bwd(p, qg, kdb, elb, sb, dob)
    dqb, dkb, dvb, dgk = _gla_intra_bwd(p, gk, dqg, dkdb, dvs, delb, dob)
    (dps, dalog8, ddt8, dw2f_pad, db2f8, dw2b_pad, db2b8) = _gates_bwd(
        p, alog_row, dt_row, w2f_pad, gk_b2_fwd, w2b_pad, gk_b2_bwd, dgsm, dgk)
    dpre, dconv = zip(*[_qkv_bwd(p, convw, g, kind) for kind, g in enumerate((dqn, dkn, dvc))])

    dp = jnp.concatenate([a.astype(BF16) for a in (*dpre, dz, dqb, dkb, dvb, dgb, dga, dgB, dps)]
                         + [jnp.zeros((t, NPERM - 9344), BF16)], axis=1)
    dwperm = _inproj_dw(h, dp)
    dx, dlnpre8 = _inproj_dx(dp, wperm, x2, ln_pre_w, dyres)

    dw_in_full = _unpermute_cols(dwperm)
    g_in = dw_in_full.reshape(2, D // 2, NSHARD, SHW).transpose(0, 2, 1, 3)
    g_proj = dw3.reshape(3, NSHARD, D // NSHARD, D).transpose(1, 0, 2, 3).reshape(NSHARD, 2, 384, D)
    g_proj = g_proj.transpose(1, 0, 2, 3)
    dconv_full = jnp.concatenate(dconv, axis=1)
    dw2f, dw2b = dw2f_pad[32:48], dw2b_pad[48:64]
    g_small = jnp.stack([_pack_shard_small(dconv_full[0:5, 768 * s:768 * (s + 1)],
                                           dw2f[:, 128 * s:128 * (s + 1)], dw2b[:, 128 * s:128 * (s + 1)])
                         for s in range(NSHARD)])
    g_small = g_small.reshape(NSHARD, 2, 16, 768).transpose(1, 0, 2, 3)
    parts = [g_in, g_proj, g_small]
    got = _swap_halves(parts)
    sums = [_sum_cast(a, b) for a, b in zip(parts, got)]
    landed = _scatter_shards([hb for _, hb in sums])
    halves = [_sum4(f, g) for (f, _), g in zip(sums, landed)]
    r_in, r_proj, r_small = _join_halves(halves)
    grad_w_in = r_in.reshape(D, SHW)
    r_proj = r_proj.reshape(768, D)
    r_small = r_small.reshape(32, 768)

    small_grads = (dlnpre8[0:1], dalog8[0:1, 0:8], dalog8[0:1, 8:16], ddt8[0:1, 0:8], ddt8[0:1, 8:16],
                   dgdn_w[0:1], db2f8[0:1], db2b8[0:1], dgla_w[0:1], dlnpost[0:1])
    gsmall = _allreduce_small(_pack_small(small_grads))

    d_in, nm_in, nv_in = _adam(w_in[0], grad_w_in, m_w_in[0], v_w_in[0], 128)
    stack3 = lambda a, b, c: jnp.concatenate([a[0], b[0], c[0]], axis=0)
    d_pr, nm_pr, nv_pr = _adam(stack3(w_proj_gdn, w_proj_gla, w_out), r_proj,
                               stack3(m_w_proj_gdn, m_w_proj_gla, m_w_out),
                               stack3(v_w_proj_gdn, v_w_proj_gla, v_w_out), 256)
    d_ss, nm_ss, nv_ss = _adam(_pack_shard_small(conv_w[0], gk_w2_fwd[0], gk_w2_bwd[0]), r_small,
                               _pack_shard_small(m_conv_w[0], m_gk_w2_fwd[0], m_gk_w2_bwd[0]),
                               _pack_shard_small(v_conv_w[0], v_gk_w2_fwd[0], v_gk_w2_bwd[0]), 32)
    smalls = dict(ln_pre_w=(ln_pre_w, m_ln_pre_w, v_ln_pre_w), a_log_fwd=(a_log_fwd, m_a_log_fwd, v_a_log_fwd),
                  a_log_bwd=(a_log_bwd, m_a_log_bwd, v_a_log_bwd),
                  dt_bias_fwd=(dt_bias_fwd, m_dt_bias_fwd, v_dt_bias_fwd),
                  dt_bias_bwd=(dt_bias_bwd, m_dt_bias_bwd, v_dt_bias_bwd),
                  gdn_norm_w=(gdn_norm_w, m_gdn_norm_w, v_gdn_norm_w),
                  gk_b2_fwd=(gk_b2_fwd, m_gk_b2_fwd, v_gk_b2_fwd), gk_b2_bwd=(gk_b2_bwd, m_gk_b2_bwd, v_gk_b2_bwd),
                  gla_norm_w=(gla_norm_w, m_gla_norm_w, v_gla_norm_w), ln_post_w=(ln_post_w, m_ln_post_w, v_ln_post_w))
    ws, ms, vs = (_pack_small([smalls[n][i] for n in SMALL_NAMES]) for i in range(3))
    d_sm, nm_sm, nv_sm = _adam(ws, gsmall, ms, vs, SMALL_ROWS)

    def family(in_, pr, ss, sm):
        conv, w2f_, w2b_ = _unpack_shard_small(ss)
        s = dict(zip(SMALL_NAMES, _unpack_small(sm)))
        return [s["ln_pre_w"], in_[None], conv[None], s["a_log_fwd"], s["a_log_bwd"], s["dt_bias_fwd"],
                s["dt_bias_bwd"], s["gdn_norm_w"], pr[None, 0:256], w2f_[None], s["gk_b2_fwd"], w2b_[None],
                s["gk_b2_bwd"], s["gla_norm_w"], pr[None, 256:512], pr[None, 512:768], s["ln_post_w"]]

    loss = lax.psum(loss8[0, 0], ("x", "y", "c"))
    outs = [loss, dx[None]]
    outs += family(grad_w_in, r_proj, r_small, gsmall)
    outs += family(d_in, d_pr, d_ss, d_sm)
    outs += family(nm_in, nm_pr, nm_ss, nm_sm)
    outs += family(nv_in, nv_pr, nv_ss, nv_sm)
    return tuple(outs)
```

```python
import functools

import jax
import jax.numpy as jnp
from jax import lax
from jax.experimental import pallas as pl
from jax.experimental.pallas import tpu as pltpu

F32 = jnp.float32
BF16 = jnp.bfloat16
HI = lax.Precision.HIGHEST
MESH = pl.DeviceIdType.MESH

D = 1024
CH = 64
EPS = 1e-6
NA, DA = 8, 128
NB, DKB, DVB = 4, 128, 256
NSHARD = 4
SHW = 2320
NPERM = 9728
PS_BLOCK = 72
LR, B1, B2, ADAM_EPS, WD, STEP = 0.001, 0.9, 0.999, 1e-08, 0.01, 10

ANY = pl.BlockSpec(memory_space=pl.ANY)


def _cparams(sem=None, vmem_mb=48):
    return pltpu.CompilerParams(dimension_semantics=sem, vmem_limit_bytes=vmem_mb << 20)


def _bdot(a, b, ca, cb):
    return lax.dot_general(a.astype(BF16), b.astype(BF16), (((ca,), (cb,)), ((), ())),
                           preferred_element_type=F32)


@jax.custom_vjp
def mm(a, b):
    return _bdot(a, b, 1, 0)


def _mm_fwd(a, b):
    return _bdot(a, b, 1, 0), (a, b)


def _mm_bwd(res, g):
    a, b = res
    return _bdot(g, b, 1, 1), _bdot(a, g, 0, 0)


mm.defvjp(_mm_fwd, _mm_bwd)


@jax.custom_vjp
def mm_nt(a, b):
    return _bdot(a, b, 1, 1)


def _mm_nt_fwd(a, b):
    return _bdot(a, b, 1, 1), (a, b)


def _mm_nt_bwd(res, g):
    a, b = res
    return _bdot(g, b, 1, 0), _bdot(g, a, 0, 0)


mm_nt.defvjp(_mm_nt_fwd, _mm_nt_bwd)


@jax.custom_vjp
def mm_tn(a, b):
    return _bdot(a, b, 0, 0)


def _mm_tn_fwd(a, b):
    return _bdot(a, b, 0, 0), (a, b)


def _mm_tn_bwd(res, g):
    a, b = res
    return _bdot(b, g, 1, 1), _bdot(a, g, 1, 0)


mm_tn.defvjp(_mm_tn_fwd, _mm_tn_bwd)


def dot_hi(a, b):
    return lax.dot_general(a, b, (((1,), (0,)), ((), ())), precision=HI, preferred_element_type=F32)


def _split3(x):
    x1 = x.astype(BF16)
    r = x - x1.astype(F32)
    x2 = r.astype(BF16)
    return x1, x2, (r - x2.astype(F32)).astype(BF16)


def _cdot(c, x, cc, cx, c_first=True):
    parts = _split3(x)
    if c_first:
        return _bdot(c, parts[0], cc, cx) + _bdot(c, parts[1], cc, cx) + _bdot(c, parts[2], cc, cx)
    return _bdot(parts[0], c, cx, cc) + _bdot(parts[1], c, cx, cc) + _bdot(parts[2], c, cx, cc)


@jax.custom_vjp
def cmm(c, x):
    return _cdot(c, x, 1, 0)


def _cmm_fwd(c, x):
    return _cdot(c, x, 1, 0), c


def _cmm_bwd(c, g):
    return jnp.zeros_like(c), _cdot(c, g, 0, 0)


cmm.defvjp(_cmm_fwd, _cmm_bwd)


@jax.custom_vjp
def mmc(x, c):
    return _cdot(c, x, 0, 1, c_first=False)


def _mmc_fwd(x, c):
    return _cdot(c, x, 0, 1, c_first=False), c


def _mmc_bwd(c, g):
    return _cdot(c, g, 1, 1, c_first=False), jnp.zeros_like(c)


mmc.defvjp(_mmc_fwd, _mmc_bwd)


def _sigmoid(x):
    return 1.0 / (1.0 + jnp.exp(-x))


def _silu(x):
    return x * _sigmoid(x)


def _softplus(x):
    return jnp.maximum(x, 0.0) + jnp.log(1.0 + jnp.exp(-jnp.abs(x)))


def _rms(x, w):
    return x * lax.rsqrt(jnp.mean(x * x, axis=-1, keepdims=True) + EPS) * w


SC = 256


class _Consts:
    def __init__(self, rev):
        r = lax.broadcasted_iota(jnp.int32, (SC, SC), 0)
        c = lax.broadcasted_iota(jnp.int32, (SC, SC), 1)
        same = (r >> 6) == (c >> 6)
        a = jnp.where(rev, c, r)
        b = jnp.where(rev, r, c)
        self.incl = same & (a >= b)
        self.strict = same & (a > b)
        self.incl_f = self.incl.astype(F32)
        self.eye = (r == c).astype(F32)
        rows = lax.broadcasted_iota(jnp.int32, (SC, 1), 0)
        self.last_col = ((rows & (CH - 1)) == jnp.where(rev, 0, CH - 1)).astype(F32)
        rr = lax.broadcasted_iota(jnp.int32, (SC, CH), 0)
        cc = lax.broadcasted_iota(jnp.int32, (SC, CH), 1)
        self.fold = ((rr & (CH - 1)) == cc).astype(F32)


def _dot3(a, b, ca=1, cb=0):
    ah, bh = a.astype(BF16), b.astype(BF16)
    al, bl = (a - ah.astype(F32)).astype(BF16), (b - bh.astype(F32)).astype(BF16)
    return _bdot(ah, bh, ca, cb) + (_bdot(ah, bl, ca, cb) + _bdot(al, bh, ca, cb))


TRI_SPLIT_LEVELS = 2


def _tri_inv(low, eye):
    n = -low
    acc = eye + n
    p = n
    for level in range(5):
        dot = _dot3 if level < TRI_SPLIT_LEVELS else (lambda a, b: _bdot(a, b, 1, 0))
        p = dot(p, p)
        acc = acc + dot(acc, p)
    return acc


@jax.custom_vjp
def _solve2(low, rv, rk, tinv):
    x = _dot3(tinv, jnp.concatenate([rv, rk], axis=1))
    return x[:, :DA], x[:, DA:]


def _solve2_fwd(low, rv, rk, tinv):
    x = _dot3(tinv, jnp.concatenate([rv, rk], axis=1))
    return (x[:, :DA], x[:, DA:]), (x, tinv)


def _solve2_bwd(res, g):
    x, tinv = res
    drhs = _dot3(tinv, jnp.concatenate(g, axis=1), 0, 0)
    return -_dot3(drhs, x, 1, 1), drhs[:, :DA], drhs[:, DA:], jnp.zeros_like(tinv)


_solve2.defvjp(_solve2_fwd, _solve2_bwd)


def _chunk_last(x, cs):
    xs = (x * cs.last_col).reshape(SC // CH, CH, x.shape[1])
    return jnp.broadcast_to(jnp.sum(xs, axis=1, keepdims=True), xs.shape).reshape(x.shape)


def _gdn_decay(g, cs):
    gw = jnp.concatenate([g] * (SC // DA), axis=1)
    grow = jnp.sum(cs.eye * gw, axis=0, keepdims=True)
    return jnp.where(cs.incl, jnp.exp(jnp.where(cs.incl, gw - grow, 0.0)), 0.0)


def _gdn_intra(q, k, v, g, bx, tinv, cs):
    decay = _gdn_decay(g, cs)
    kb = k * bx
    low = jnp.where(cs.strict, mm_nt(kb, k) * decay, 0.0)
    eg = jnp.exp(g)
    made = tinv is None
    if made:
        tinv = _tri_inv(low, cs.eye)
    u, w = _solve2(low, v * bx, kb * eg, tinv)
    attn = mmc(mm_nt(q, k) * decay, cs.fold)
    qd = q * eg
    glast = _chunk_last(g, cs)
    kd = k * jnp.exp(glast - g)
    outs = (u, w, attn, qd, kd, jnp.exp(glast))
    return outs + (tinv,) if made else outs


def _gdn_scan(u, w, attn, qd, kd, el, s):
    vn = u - mm(w, s)
    o = mm(qd, s) + mm(attn, vn)
    sn = s * el + mm_tn(kd, vn)
    return o, sn


def _gla_intra(q, k, v, gk, cs):
    gc = cmm(cs.incl_f, gk)
    qg = q * (DKB ** -0.5) * jnp.exp(gc)
    kg = k * jnp.exp(-gc)
    attn = jnp.where(cs.incl, mm_nt(qg, kg), 0.0)
    intra = mm(attn, v)
    glast = _chunk_last(gc, cs)
    kd = k * jnp.exp(glast - gc)
    return qg, kd, intra, jnp.exp(glast)


def _gla_scan(qg, kd, v, el, st):
    o = mm_nt(qg, st)
    stn = st * el + mm_tn(v, kd)
    return o, stn


def _shift_rows(x, s):
    if s == 0:
        return x
    t = x.shape[0]
    rolled = pltpu.roll(x, (-s) % t, 0)
    rows = lax.broadcasted_iota(jnp.int32, x.shape, 0)
    return jnp.where((rows + s >= 0) & (rows + s < t), rolled, 0.0)


@jax.custom_vjp
def _conv5(x, w):
    acc = w[0:1] * _shift_rows(x, -2)
    for j in range(1, 5):
        acc = acc + w[j:j + 1] * _shift_rows(x, j - 2)
    return acc


def _conv5_fwd(x, w):
    return _conv5(x, w), (x, w)


def _conv5_bwd(res, g):
    x, w = res
    dx = w[0:1] * _shift_rows(g, 2)
    for j in range(1, 5):
        dx = dx + w[j:j + 1] * _shift_rows(g, 2 - j)
    rows = lax.broadcasted_iota(jnp.int32, w.shape, 0)
    dw = jnp.zeros_like(w)
    for j in range(5):
        dwj = jnp.sum(g * _shift_rows(x, j - 2), axis=0, keepdims=True)
        dw = dw + jnp.where(rows == j, dwj, 0.0)
    return dx, dw


_conv5.defvjp(_conv5_fwd, _conv5_bwd)


def _qkv_act(kind):
    def f(x, w):
        c = _silu(_conv5(x, w))
        if kind == 2:
            return c
        c = c * lax.rsqrt(jnp.sum(c * c, axis=-1, keepdims=True) + EPS)
        return c * (DA ** -0.5) if kind == 0 else c
    return f


def _inproj(x, lnw, wperm, tm=512, tn=512):
    t = x.shape[0]

    def body(x_ref, lnw_ref, w_ref, p_ref, h_ref, hbuf):
        @pl.when(pl.program_id(1) == 0)
        def _():
            hb = _rms(x_ref[...], lnw_ref[...]).astype(BF16)
            hbuf[...] = hb
            h_ref[...] = hb
        p_ref[...] = jnp.dot(hbuf[...], w_ref[...], preferred_element_type=F32)

    return pl.pallas_call(
        body, name="inproj", grid=(t // tm, NPERM // tn),
        in_specs=[pl.BlockSpec((tm, D), lambda i, j: (i, 0)),
                  pl.BlockSpec((1, D), lambda i, j: (0, 0)),
                  pl.BlockSpec((D, tn), lambda i, j: (0, j))],
        out_specs=[pl.BlockSpec((tm, tn), lambda i, j: (i, j)),
                   pl.BlockSpec((tm, D), lambda i, j: (i, 0))],
        out_shape=[jax.ShapeDtypeStruct((t, NPERM), F32),
                   jax.ShapeDtypeStruct((t, D), BF16)],
        scratch_shapes=[pltpu.VMEM((tm, D), BF16)],
        compiler_params=_cparams(("parallel", "arbitrary")),
    )(x, lnw, wperm)


def _inproj_dw(h, dp, tm=512, tn=512):
    t = h.shape[0]
    ncol = dp.shape[1]

    def body(h_ref, dp_ref, dw_ref):
        @pl.when(pl.program_id(1) == 0)
        def _():
            dw_ref[...] = jnp.zeros_like(dw_ref)
        dw_ref[...] += _bdot(h_ref[...], dp_ref[...], 0, 0)

    return pl.pallas_call(
        body, name="inproj_dw", grid=(ncol // tn, t // tm),
        in_specs=[pl.BlockSpec((tm, D), lambda j, i: (i, 0)),
                  pl.BlockSpec((tm, tn), lambda j, i: (i, j))],
        out_specs=pl.BlockSpec((D, tn), lambda j, i: (0, j)),
        out_shape=jax.ShapeDtypeStruct((D, ncol), F32),
        compiler_params=_cparams(("parallel", "arbitrary")),
    )(h, dp)


def _inproj_dx(dp, wperm, x, lnw, dyres, tm=256, tn=512):
    t = x.shape[0]
    ncol = dp.shape[1]
    nj = ncol // tn

    def body(dp_ref, w_ref, x_ref, lnw_ref, dy_ref, dx_ref, dlnw_ref, acc):
        j = pl.program_id(1)

        @pl.when(j == 0)
        def _():
            acc[...] = jnp.zeros_like(acc)

        acc[...] += _bdot(dp_ref[...], w_ref[...], 1, 1)

        @pl.when(j == nj - 1)
        def _():
            _, vjp = jax.vjp(_rms, x_ref[...], lnw_ref[...])
            dx, dlnw = vjp(acc[...])
            dx_ref[...] = dx + dy_ref[...]

            @pl.when(pl.program_id(0) == 0)
            def _():
                dlnw_ref[...] = jnp.zeros_like(dlnw_ref)
            dlnw_ref[...] += jnp.broadcast_to(dlnw, dlnw_ref.shape)

    return pl.pallas_call(
        body, name="inproj_dx", grid=(t // tm, nj),
        in_specs=[pl.BlockSpec((tm, tn), lambda i, j: (i, j)),
                  pl.BlockSpec((D, tn), lambda i, j: (0, j)),
                  pl.BlockSpec((tm, D), lambda i, j: (i, 0)),
                  pl.BlockSpec((1, D), lambda i, j: (0, 0)),
                  pl.BlockSpec((tm, D), lambda i, j: (i, 0))],
        out_specs=[pl.BlockSpec((tm, D), lambda i, j: (i, 0)),
                   pl.BlockSpec((8, D), lambda i, j: (0, 0))],
        out_shape=[jax.ShapeDtypeStruct((t, D), F32), jax.ShapeDtypeStruct((8, D), F32)],
        scratch_shapes=[pltpu.VMEM((tm, D), F32)],
        compiler_params=_cparams(("arbitrary", "arbitrary")),
    )(dp, wperm, x, lnw, dyres)


def _qkv_fwd(p, convw, kind):
    t = p.shape[0]
    f = _qkv_act(kind)

    def body(p_ref, w_ref, o_ref):
        o_ref[...] = f(p_ref[...], w_ref[...])

    return pl.pallas_call(
        body, name=f"qkv_fwd{kind}", grid=(NA,),
        in_specs=[pl.BlockSpec((t, DA), lambda h: (0, kind * NA + h)),
                  pl.BlockSpec((8, DA), lambda h: (0, kind * NA + h))],
        out_specs=pl.BlockSpec((t, DA), lambda h: (0, h)),
        out_shape=jax.ShapeDtypeStruct((t, D), F32),
        compiler_params=_cparams(("parallel",)),
    )(p, convw)


def _qkv_bwd(p, convw, dout, kind):
    t = p.shape[0]
    f = _qkv_act(kind)

    def body(p_ref, w_ref, g_ref, dx_ref, dw_ref):
        _, vjp = jax.vjp(f, p_ref[...], w_ref[...])
        dx, dw = vjp(g_ref[...])
        dx_ref[...] = dx
        dw_ref[...] = dw

    return pl.pallas_call(
        body, name=f"qkv_bwd{kind}", grid=(NA,),
        in_specs=[pl.BlockSpec((t, DA), lambda h: (0, kind * NA + h)),
                  pl.BlockSpec((8, DA), lambda h: (0, kind * NA + h)),
                  pl.BlockSpec((t, DA), lambda h: (0, h))],
        out_specs=[pl.BlockSpec((t, DA), lambda h: (0, h)),
                   pl.BlockSpec((8, DA), lambda h: (0, h))],
        out_shape=[jax.ShapeDtypeStruct((t, D), F32), jax.ShapeDtypeStruct((8, D), F32)],
        compiler_params=_cparams(("parallel",)),
    )(p, convw, dout)


def _gates_f(ps, alog_row, dt_row, w2f, b2f, w2b, b2b):
    lane = lax.broadcasted_iota(jnp.int32, ps.shape, 1)
    lg = -jnp.exp(alog_row) * _softplus(ps + dt_row)
    gsm = jnp.where(lane < 16, lg, jnp.where(lane < 32, _sigmoid(ps), 0.0))
    gkf = -_softplus(-(mm(ps, w2f) + b2f)) * (1.0 / 16.0)
    gkb = -_softplus(-(mm(ps, w2b) + b2b)) * (1.0 / 16.0)
    return gsm, gkf, gkb


def _gates_fwd(ps, alog_row, dt_row, w2f, b2f, w2b, b2b, tm=512):
    t = ps.shape[0]

    def body(ps_ref, a_ref, d_ref, wf_ref, bf_ref, wb_ref, bb_ref, gsm_ref, gk_ref):
        gsm, gkf, gkb = _gates_f(ps_ref[...], a_ref[...], d_ref[...], wf_ref[...], bf_ref[...],
                                 wb_ref[...], bb_ref[...])
        gsm_ref[...] = gsm
        gk_ref[0] = gkf
        gk_ref[1] = gkb

    row = lambda n: pl.BlockSpec((1, n), lambda i: (0, 0))
    mat = pl.BlockSpec((128, 512), lambda i: (0, 0))
    return pl.pallas_call(
        body, name="gates_fwd", grid=(t // tm,),
        in_specs=[pl.BlockSpec((tm, 128), lambda i: (i, PS_BLOCK)), row(128), row(128), mat, row(512), mat, row(512)],
        out_specs=[pl.BlockSpec((tm, 128), lambda i: (i, 0)),
                   pl.BlockSpec((2, tm, 512), lambda i: (0, i, 0))],
        out_shape=[jax.ShapeDtypeStruct((t, 128), F32), jax.ShapeDtypeStruct((2, t, 512), F32)],
        compiler_params=_cparams(("parallel",)),
    )(ps, alog_row, dt_row, w2f, b2f, w2b, b2b)


def _gates_bwd(ps, alog_row, dt_row, w2f, b2f, w2b, b2b, dgsm, dgk, tm=512):
    t = ps.shape[0]

    def body(ps_ref, a_ref, d_ref, wf_ref, bf_ref, wb_ref, bb_ref, dgsm_ref, dgk_ref,
             dps_ref, da_ref, dd_ref, dwf_ref, dbf_ref, dwb_ref, dbb_ref):
        _, vjp = jax.vjp(_gates_f, ps_ref[...], a_ref[...], d_ref[...], wf_ref[...], bf_ref[...],
                         wb_ref[...], bb_ref[...])
        dps, da, dd, dwf, dbf, dwb, dbb = vjp((dgsm_ref[...], dgk_ref[0], dgk_ref[1]))
        dps_ref[...] = dps
        accs = ((da_ref, da), (dd_ref, dd), (dwf_ref, dwf), (dbf_ref, dbf), (dwb_ref, dwb), (dbb_ref, dbb))

        @pl.when(pl.program_id(0) == 0)
        def _():
            for ref, _ in accs:
                ref[...] = jnp.zeros_like(ref)
        for ref, val in accs:
            ref[...] += jnp.broadcast_to(val, ref.shape)

    row = lambda n: pl.BlockSpec((1, n), lambda i: (0, 0))
    row8 = lambda n: pl.BlockSpec((8, n), lambda i: (0, 0))
    mat = pl.BlockSpec((128, 512), lambda i: (0, 0))
    return pl.pallas_call(
        body, name="gates_bwd", grid=(t // tm,),
        in_specs=[pl.BlockSpec((tm, 128), lambda i: (i, PS_BLOCK)), row(128), row(128), mat, row(512), mat, row(512),
                  pl.BlockSpec((tm, 128), lambda i: (i, 0)),
                  pl.BlockSpec((2, tm, 512), lambda i: (0, i, 0))],
        out_specs=[pl.BlockSpec((tm, 128), lambda i: (i, 0)), row8(128), row8(128), mat, row8(512), mat, row8(512)],
        out_shape=[jax.ShapeDtypeStruct((t, 128), F32),
                   jax.ShapeDtypeStruct((8, 128), F32), jax.ShapeDtypeStruct((8, 128), F32),
                   jax.ShapeDtypeStruct((128, 512), F32), jax.ShapeDtypeStruct((8, 512), F32),
                   jax.ShapeDtypeStruct((128, 512), F32), jax.ShapeDtypeStruct((8, 512), F32)],
        compiler_params=_cparams(("arbitrary",)),
    )(ps, alog_row, dt_row, w2f, b2f, w2b, b2b, dgsm, dgk)


def _rows(i):
    return pl.ds(pl.multiple_of(i * CH, CH), CH)


def _srows(i):
    return pl.ds(pl.multiple_of(i * SC, SC), SC)


def _first_row(x):
    row = lax.broadcasted_iota(jnp.int32, (8, x.shape[1]), 0)
    return jnp.where(row == 0, jnp.broadcast_to(x, (8, x.shape[1])), 0.0)


def _chunk_rows(e_ref, i):
    pad = jnp.zeros((CH - 8, 128), F32)
    return jnp.concatenate([x for c in range(SC // CH) for x in (e_ref[(SC // CH) * i + c], pad)], axis=0)


def _gcum_f(gsm, tm):
    i = lax.broadcasted_iota(jnp.int32, (tm, tm), 0)
    j = lax.broadcasted_iota(jnp.int32, (tm, tm), 1)
    same = (i >> 6) == (j >> 6)
    lower = (same & (i >= j)).astype(F32)
    upper = (same & (i <= j)).astype(F32)
    r = lax.broadcasted_iota(jnp.int32, (128, D), 0)
    head = lax.broadcasted_iota(jnp.int32, (128, D), 1) >> 7
    pick = lambda off: (r == head + off).astype(F32)
    lane = lax.broadcasted_iota(jnp.int32, gsm.shape, 1)
    run = jnp.where(lane < 8, cmm(lower, gsm), cmm(upper, gsm))
    return mmc(run, pick(0)), mmc(run, pick(8)), mmc(gsm, pick(16)), mmc(gsm, pick(24))


def _gcum_fwd(gsm, tm=256):
    t = gsm.shape[0]

    def body(s_ref, g_ref, b_ref):
        gf, gb, bf, bb = _gcum_f(s_ref[...], tm)
        g_ref[0] = gf
        g_ref[1] = gb
        b_ref[0] = bf
        b_ref[1] = bb

    two = pl.BlockSpec((2, tm, D), lambda i: (0, i, 0))
    return pl.pallas_call(
        body, name="gcum_fwd", grid=(t // tm,),
        in_specs=[pl.BlockSpec((tm, 128), lambda i: (i, 0))], out_specs=[two, two],
        out_shape=[jax.ShapeDtypeStruct((2, t, D), F32)] * 2,
        compiler_params=_cparams(("parallel",)),
    )(gsm)


def _gcum_bwd(gsm, dg2, db2, tm=256):
    t = gsm.shape[0]

    def body(s_ref, dg_ref, db_ref, ds_ref):
        _, vjp = jax.vjp(lambda s: _gcum_f(s, tm), s_ref[...])
        ds_ref[...] = vjp((dg_ref[0], dg_ref[1], db_ref[0], db_ref[1]))[0]

    two = pl.BlockSpec((2, tm, D), lambda i: (0, i, 0))
    tile = pl.BlockSpec((tm, 128), lambda i: (i, 0))
    return pl.pallas_call(
        body, name="gcum_bwd", grid=(t // tm,),
        in_specs=[tile, two, two], out_specs=tile,
        out_shape=jax.ShapeDtypeStruct((t, 128), F32),
        compiler_params=_cparams(("parallel",)),
    )(gsm, dg2, db2)


def _gdn_intra_fwd(qn, kn, vc, g2, b2):
    t = qn.shape[0]
    n = t // CH

    def body(q_ref, k_ref, v_ref, g_ref, b_ref, u_ref, w_ref, a_ref, qd_ref, kd_ref, e_ref, t_ref):
        cs = _Consts(pl.program_id(0) == 1)

        def step(i, carry):
            r = _srows(i)
            q, k, v, g, bx = q_ref[r, :], k_ref[r, :], v_ref[r, :], g_ref[r, :], b_ref[r, :]
            u, w, a, qd, kd, el, tinv = _gdn_intra(q, k, v, g, bx, None, cs)
            u_ref[r, :] = u
            w_ref[r, :] = w
            a_ref[r, :] = a
            qd_ref[r, :] = qd
            kd_ref[r, :] = kd
            t_ref[r, :] = tinv
            for c in range(SC // CH):
                e_ref[(SC // CH) * i + c] = el[c * CH:c * CH + 8]
            return carry

        lax.fori_loop(0, t // SC, step, 0)

    head = pl.BlockSpec((t, DA), lambda d, h: (0, h))
    dh = pl.BlockSpec((None, t, DA), lambda d, h: (d, 0, h))
    sq = lambda w: pl.BlockSpec((None, None, t, w), lambda d, h: (d, h, 0, 0))
    big = jax.ShapeDtypeStruct((2, t, D), F32)
    return pl.pallas_call(
        body, name="gdn_intra_fwd", grid=(2, NA),
        in_specs=[head, head, head, dh, dh],
        out_specs=[dh, dh, sq(CH), dh, dh, pl.BlockSpec((None, None, n, 8, 128), lambda d, h: (d, h, 0, 0, 0)),
                   sq(SC)],
        out_shape=[big, big, jax.ShapeDtypeStruct((2, NA, t, CH), F32), big, big,
                   jax.ShapeDtypeStruct((2, NA, n, 8, 128), F32), jax.ShapeDtypeStruct((2, NA, t, SC), F32)],
        compiler_params=_cparams(("parallel", "parallel")),
    )(qn, kn, vc, g2, b2)


SCAN_TB = 512
SCAN_HB = 4


def _scan_specs(t, width, nheads, hb, along):
    nt = t // SCAN_TB
    nb = SCAN_TB // CH

    def tmap(d, tt):
        fwd = tt + d * (nt - 1 - 2 * tt)
        return fwd if along > 0 else nt - 1 - fwd

    tok = pl.BlockSpec((None, SCAN_TB, hb * width), lambda d, h, tt: (d, tmap(d, tt), h))
    per = lambda *tail: pl.BlockSpec((None, hb, nb) + tail, lambda d, h, tt: (d, h, tmap(d, tt)) + (0,) * len(tail))
    sq = pl.BlockSpec((None, hb, SCAN_TB, CH), lambda d, h, tt: (d, h, tmap(d, tt), 0))
    shared = lambda w: pl.BlockSpec((SCAN_TB, hb * w), lambda d, h, tt: (tmap(d, tt), h))
    return tok, per, sq, shared, (2, nheads // hb, nt), nb


def _gdn_scan_fwd(u, w, a, qd, kd, e):
    t = u.shape[1]
    tok, per, sq, _, grid, nb = _scan_specs(t, DA, NA, SCAN_HB, +1)

    def body(u_ref, w_ref, a_ref, qd_ref, kd_ref, e_ref, o_ref, s_ref, state):
        rev = pl.program_id(0) == 1

        @pl.when(pl.program_id(2) == 0)
        def _():
            state[...] = jnp.zeros_like(state)

        def step(i, ss):
            ci = jnp.where(rev, nb - 1 - i, i)
            r = _rows(ci)
            out = []
            for hh, s in enumerate(ss):
                c = slice(hh * DA, (hh + 1) * DA)
                s_ref[hh, ci] = s
                o, sn = _gdn_scan(u_ref[r, c], w_ref[r, c], a_ref[hh, r, :], qd_ref[r, c], kd_ref[r, c],
                                  e_ref[hh, ci][0:1], s)
                o_ref[r, c] = o
                out.append(sn)
            return tuple(out)

        ss = lax.fori_loop(0, nb, step, tuple(state[hh] for hh in range(SCAN_HB)))
        for hh, s in enumerate(ss):
            state[hh] = s

    return pl.pallas_call(
        body, name="gdn_scan_fwd", grid=grid,
        in_specs=[tok, tok, sq, tok, tok, per(8, 128)],
        out_specs=[tok, per(DA, DA)],
        out_shape=[jax.ShapeDtypeStruct((2, t, D), F32), jax.ShapeDtypeStruct((2, NA, t // CH, DA, DA), F32)],
        scratch_shapes=[pltpu.VMEM((SCAN_HB, DA, DA), F32)],
        compiler_params=_cparams(("parallel", "parallel", "arbitrary")),
    )(u, w, a, qd, kd, e)


def _gdn_scan_bwd(u, w, a, qd, kd, e, ssave, do):
    t = u.shape[1]
    tok, per, sq, shared, grid, nb = _scan_specs(t, DA, NA, SCAN_HB, -1)

    def body(u_ref, w_ref, a_ref, qd_ref, kd_ref, e_ref, s_ref, do_ref,
             du_ref, dw_ref, da_ref, dqd_ref, dkd_ref, de_ref, state):
        rev = pl.program_id(0) == 1

        @pl.when(pl.program_id(2) == 0)
        def _():
            state[...] = jnp.zeros_like(state)

        def step(i, dss):
            ci = jnp.where(rev, i, nb - 1 - i)
            r = _rows(ci)
            out = []
            for hh, ds in enumerate(dss):
                c = slice(hh * DA, (hh + 1) * DA)
                _, vjp = jax.vjp(_gdn_scan, u_ref[r, c], w_ref[r, c], a_ref[hh, r, :], qd_ref[r, c], kd_ref[r, c],
                                 e_ref[hh, ci][0:1], s_ref[hh, ci])
                du, dw, da, dqd, dkd, de, dsn = vjp((do_ref[r, c], ds))
                du_ref[r, c] = du
                dw_ref[r, c] = dw
                da_ref[hh, r, :] = da
                dqd_ref[r, c] = dqd
                dkd_ref[r, c] = dkd
                de_ref[hh, ci] = _first_row(de)
                out.append(dsn)
            return tuple(out)

        dss = lax.fori_loop(0, nb, step, tuple(state[hh] for hh in range(SCAN_HB)))
        for hh, ds in enumerate(dss):
            state[hh] = ds

    big = jax.ShapeDtypeStruct((2, t, D), F32)
    return pl.pallas_call(
        body, name="gdn_scan_bwd", grid=grid,
        in_specs=[tok, tok, sq, tok, tok, per(8, 128), per(DA, DA), shared(DA)],
        out_specs=[tok, tok, sq, tok, tok, per(8, 128)],
        out_shape=[big, big, jax.ShapeDtypeStruct((2, NA, t, CH), F32), big, big,
                   jax.ShapeDtypeStruct((2, NA, t // CH, 8, 128), F32)],
        scratch_shapes=[pltpu.VMEM((SCAN_HB, DA, DA), F32)],
        compiler_params=_cparams(("parallel", "parallel", "arbitrary")),
    )(u, w, a, qd, kd, e, ssave, do)


def _gdn_intra_bwd(qn, kn, vc, g2, b2, tinv, du, dw, da, dqd, dkd, de):
    t = qn.shape[0]
    n = t // CH

    def body(q_ref, k_ref, v_ref, g_ref, b_ref, t_ref, du_ref, dw_ref, da_ref, dqd_ref, dkd_ref, de_ref,
             dq_ref, dk_ref, dv_ref, dg_ref, db_ref):
        d = pl.program_id(1)
        cs = _Consts(d == 1)

        @pl.when(d == 0)
        def _():
            dq_ref[...] = jnp.zeros_like(dq_ref)
            dk_ref[...] = jnp.zeros_like(dk_ref)
            dv_ref[...] = jnp.zeros_like(dv_ref)

        def step(i, carry):
            r = _srows(i)
            tinv_c = t_ref[r, :]
            f = lambda q, k, v, g, bx: _gdn_intra(q, k, v, g, bx, tinv_c, cs)
            _, vjp = jax.vjp(f, q_ref[r, :], k_ref[r, :], v_ref[r, :], g_ref[r, :], b_ref[r, :])
            dq, dk, dv, dg, dbx = vjp((du_ref[r, :], dw_ref[r, :], da_ref[r, :], dqd_ref[r, :],
                                       dkd_ref[r, :], _chunk_rows(de_ref, i)))
            dq_ref[r, :] += dq
            dk_ref[r, :] += dk
            dv_ref[r, :] += dv
            dg_ref[r, :] = dg
            db_ref[r, :] = dbx
            return carry

        lax.fori_loop(0, t // SC, step, 0)

    head = pl.BlockSpec((t, DA), lambda h, d: (0, h))
    dh = pl.BlockSpec((None, t, DA), lambda h, d: (d, 0, h))
    sq = pl.BlockSpec((None, None, t, CH), lambda h, d: (d, h, 0, 0))
    tq = pl.BlockSpec((None, None, t, SC), lambda h, d: (d, h, 0, 0))
    full = jax.ShapeDtypeStruct((t, D), F32)
    big = jax.ShapeDtypeStruct((2, t, D), F32)
    return pl.pallas_call(
        body, name="gdn_intra_bwd", grid=(NA, 2),
        in_specs=[head, head, head, dh, dh, tq, dh, dh, sq, dh, dh,
                  pl.BlockSpec((None, None, n, 8, 128), lambda h, d: (d, h, 0, 0, 0))],
        out_specs=[head, head, head, dh, dh],
        out_shape=[full, full, full, big, big],
        compiler_params=_cparams(("arbitrary", "arbitrary")),
    )(qn, kn, vc, g2, b2, tinv, du, dw, da, dqd, dkd, de)


def _gla_specs(t, order):
    ix = (lambda d, h: (d, h)) if order == "dh" else (lambda h, d: (d, h))

    def mk(fn):
        return lambda a, b: fn(*ix(a, b))
    q = pl.BlockSpec((t, DKB), mk(lambda d, h: (0, 32 + h)))
    k = pl.BlockSpec((t, DKB), mk(lambda d, h: (0, 36 + h)))
    v = pl.BlockSpec((t, DVB), mk(lambda d, h: (0, 20 + h)))
    dk = pl.BlockSpec((None, t, DKB), mk(lambda d, h: (d, 0, h)))
    dv = pl.BlockSpec((None, t, DVB), mk(lambda d, h: (d, 0, h)))
    e = pl.BlockSpec((None, None, t // CH, 8, 128), mk(lambda d, h: (d, h, 0, 0, 0)))
    s = pl.BlockSpec((None, None, t // CH, DVB, DKB), mk(lambda d, h: (d, h, 0, 0, 0)))
    return q, k, v, dk, dv, e, s


def _gla_intra_fwd(p, gk):
    t = p.shape[0]
    n = t // CH

    def body(q_ref, k_ref, v_ref, g_ref, qg_ref, kd_ref, in_ref, e_ref):
        cs = _Consts(pl.program_id(0) == 1)

        def step(i, carry):
            r = _srows(i)
            qg, kd, intra, el = _gla_intra(q_ref[r, :], k_ref[r, :], v_ref[r, :], g_ref[r, :], cs)
            qg_ref[r, :] = qg
            kd_ref[r, :] = kd
            in_ref[r, :] = intra
            for c in range(SC // CH):
                e_ref[(SC // CH) * i + c] = el[c * CH:c * CH + 8]
            return carry

        lax.fori_loop(0, t // SC, step, 0)

    q, k, v, dk, dv, e, _ = _gla_specs(t, "dh")
    return pl.pallas_call(
        body, name="gla_intra_fwd", grid=(2, NB),
        in_specs=[q, k, v, dk], out_specs=[dk, dk, dv, e],
        out_shape=[jax.ShapeDtypeStruct((2, t, NB * DKB), F32), jax.ShapeDtypeStruct((2, t, NB * DKB), F32),
                   jax.ShapeDtypeStruct((2, t, D), F32), jax.ShapeDtypeStruct((2, NB, n, 8, 128), F32)],
        compiler_params=_cparams(("parallel", "parallel")),
    )(p, p, p, gk)


GLA_HB = 2


def _gla_v_spec(t, along):
    nt = t // SCAN_TB

    def tmap(d, tt):
        fwd = tt + d * (nt - 1 - 2 * tt)
        return fwd if along > 0 else nt - 1 - fwd

    return pl.BlockSpec((SCAN_TB, GLA_HB * DVB), lambda d, h, tt: (tmap(d, tt), 5120 // (GLA_HB * DVB) + h))


def _gla_scan_fwd(p, qg, kd, intra, e):
    t = p.shape[0]
    tokk, per, _, _, grid, nb = _scan_specs(t, DKB, NB, GLA_HB, +1)
    tokv = _scan_specs(t, DVB, NB, GLA_HB, +1)[0]

    def body(v_ref, qg_ref, kd_ref, in_ref, e_ref, o_ref, s_ref, state):
        rev = pl.program_id(0) == 1

        @pl.when(pl.program_id(2) == 0)
        def _():
            state[...] = jnp.zeros_like(state)

        def step(i, sts):
            ci = jnp.where(rev, nb - 1 - i, i)
            r = _rows(ci)
            out = []
            for hh, st in enumerate(sts):
                ck = slice(hh * DKB, (hh + 1) * DKB)
                cv = slice(hh * DVB, (hh + 1) * DVB)
                s_ref[hh, ci] = st
                o, stn = _gla_scan(qg_ref[r, ck], kd_ref[r, ck], v_ref[r, cv], e_ref[hh, ci][0:1], st)
                o_ref[r, cv] = o + in_ref[r, cv]
                out.append(stn)
            return tuple(out)

        sts = lax.fori_loop(0, nb, step, tuple(state[hh] for hh in range(GLA_HB)))
        for hh, st in enumerate(sts):
            state[hh] = st

    return pl.pallas_call(
        body, name="gla_scan_fwd", grid=grid,
        in_specs=[_gla_v_spec(t, +1), tokk, tokk, tokv, per(8, 128)], out_specs=[tokv, per(DVB, DKB)],
        out_shape=[jax.ShapeDtypeStruct((2, t, D), F32), jax.ShapeDtypeStruct((2, NB, t // CH, DVB, DKB), F32)],
        scratch_shapes=[pltpu.VMEM((GLA_HB, DVB, DKB), F32)],
        compiler_params=_cparams(("parallel", "parallel", "arbitrary")),
    )(p, qg, kd, intra, e)


def _gla_scan_bwd(p, qg, kd, e, ssave, do):
    t = p.shape[0]
    tokk, per, _, shared, grid, nb = _scan_specs(t, DKB, NB, GLA_HB, -1)
    tokv = _scan_specs(t, DVB, NB, GLA_HB, -1)[0]

    def body(v_ref, qg_ref, kd_ref, e_ref, s_ref, do_ref, dqg_ref, dkd_ref, dv_ref, de_ref, state):
        rev = pl.program_id(0) == 1

        @pl.when(pl.program_id(2) == 0)
        def _():
            state[...] = jnp.zeros_like(state)

        def step(i, dsts):
            ci = jnp.where(rev, i, nb - 1 - i)
            r = _rows(ci)
            out = []
            for hh, dst in enumerate(dsts):
                ck = slice(hh * DKB, (hh + 1) * DKB)
                cv = slice(hh * DVB, (hh + 1) * DVB)
                _, vjp = jax.vjp(_gla_scan, qg_ref[r, ck], kd_ref[r, ck], v_ref[r, cv], e_ref[hh, ci][0:1],
                                 s_ref[hh, ci])
                dqg, dkd, dv, de, dstn = vjp((do_ref[r, cv], dst))
                dqg_ref[r, ck] = dqg
                dkd_ref[r, ck] = dkd
                dv_ref[r, cv] = dv
                de_ref[hh, ci] = _first_row(de)
                out.append(dstn)
            return tuple(out)

        dsts = lax.fori_loop(0, nb, step, tuple(state[hh] for hh in range(GLA_HB)))
        for hh, dst in enumerate(dsts):
            state[hh] = dst

    return pl.pallas_call(
        body, name="gla_scan_bwd", grid=grid,
        in_specs=[_gla_v_spec(t, -1), tokk, tokk, per(8, 128), per(DVB, DKB), shared(DVB)],
        out_specs=[tokk, tokk, tokv, per(8, 128)],
        out_shape=[jax.ShapeDtypeStruct((2, t, NB * DKB), F32), jax.ShapeDtypeStruct((2, t, NB * DKB), F32),
                   jax.ShapeDtypeStruct((2, t, D), F32), jax.ShapeDtypeStruct((2, NB, t // CH, 8, 128), F32)],
        scratch_shapes=[pltpu.VMEM((GLA_HB, DVB, DKB), F32)],
        compiler_params=_cparams(("parallel", "parallel", "arbitrary")),
    )(p, qg, kd, e, ssave, do)


def _gla_intra_bwd(p, gk, dqg, dkd, dvs, de, do):
    t = p.shape[0]
    n = t // CH

    def body(q_ref, k_ref, v_ref, g_ref, dqg_ref, dkd_ref, dvs_ref, de_ref, do_ref,
             dq_ref, dk_ref, dv_ref, dg_ref):
        d = pl.program_id(1)
        cs = _Consts(d == 1)

        @pl.when(d == 0)
        def _():
            dq_ref[...] = jnp.zeros_like(dq_ref)
            dk_ref[...] = jnp.zeros_like(dk_ref)
            dv_ref[...] = jnp.zeros_like(dv_ref)

        def step(i, carry):
            r = _srows(i)
            f = lambda q, k, v, g: _gla_intra(q, k, v, g, cs)
            _, vjp = jax.vjp(f, q_ref[r, :], k_ref[r, :], v_ref[r, :], g_ref[r, :])
            dq, dk, dv, dg = vjp((dqg_ref[r, :], dkd_ref[r, :], do_ref[r, :], _chunk_rows(de_ref, i)))
            dq_ref[r, :] += dq
            dk_ref[r, :] += dk
            dv_ref[r, :] += dv + dvs_ref[r, :]
            dg_ref[r, :] = dg
            return carry

        lax.fori_loop(0, t // SC, step, 0)

    q, k, v, dk, dv, e_s, _ = _gla_specs(t, "hd")
    hk = pl.BlockSpec((t, DKB), lambda h, d: (0, h))
    hv = pl.BlockSpec((t, DVB), lambda h, d: (0, h))
    return pl.pallas_call(
        body, name="gla_intra_bwd", grid=(NB, 2),
        in_specs=[q, k, v, dk, dk, dk, dv, e_s, hv],
        out_specs=[hk, hk, hv, dk],
        out_shape=[jax.ShapeDtypeStruct((t, NB * DKB), F32), jax.ShapeDtypeStruct((t, NB * DKB), F32),
                   jax.ShapeDtypeStruct((t, D), F32), jax.ShapeDtypeStruct((2, t, NB * DKB), F32)],
        compiler_params=_cparams(("arbitrary", "arbitrary")),
    )(p, p, p, gk, dqg, dkd, dvs, de, do)


def _seg_gate(o, z, w):
    return _rms(o, w) * _silu(z)


def _seg_merge(ya, yb, ga, gb):
    return _sigmoid(ga) * ya + _sigmoid(gb) * yb


def _seg_loss(out, x, tgt, w):
    err = x + _rms(out, w) - tgt
    return 0.5 * jnp.sum(jnp.mean(err * err, axis=-1, keepdims=True), axis=0, keepdims=True)


def _post(oa2, ob2, p, x, tgt, gdn_w, gla_w, lnpost, w3, tm=128):
    t = x.shape[0]

    def body(oa_ref, ob_ref, z_ref, gb_ref, ga_ref, gB_ref, x_ref, t_ref, aw_ref, bw_ref, lw_ref, w_ref,
             loss_ref, doa_ref, dob_ref, dz_ref, dgb_ref, dga_ref, dgB_ref, dy_ref,
             dw_ref, daw_ref, dbw_ref, dlw_ref):
        first = pl.program_id(0) == 0
        oa = oa_ref[0] + oa_ref[1]
        ob = ob_ref[0] + ob_ref[1]
        z, gb = z_ref[...], gb_ref[...]
        aw, bw = aw_ref[...], bw_ref[...]

        pa = [jax.vjp(_seg_gate, oa[:, h * DA:(h + 1) * DA], z[:, h * DA:(h + 1) * DA], aw) for h in range(NA)]
        pb = [jax.vjp(_seg_gate, ob[:, h * DVB:(h + 1) * DVB], gb[:, h * DVB:(h + 1) * DVB], bw)
              for h in range(NB)]
        a1 = jnp.concatenate([v for v, _ in pa], axis=1).astype(BF16)
        a2 = jnp.concatenate([v for v, _ in pb], axis=1).astype(BF16)
        ya = jnp.dot(a1, w_ref[0], preferred_element_type=F32)
        yb = jnp.dot(a2, w_ref[1], preferred_element_type=F32)
        merged, vjp_m = jax.vjp(_seg_merge, ya, yb, ga_ref[...], gB_ref[...])
        mb = merged.astype(BF16)
        out = jnp.dot(mb, w_ref[2], preferred_element_type=F32)
        loss, vjp_l = jax.vjp(_seg_loss, out, x_ref[...], t_ref[...], lw_ref[...])
        dout, dyres, _, dlw = vjp_l(jnp.ones((1, 1), F32))
        dy_ref[...] = dyres
        doutb = dout.astype(BF16)
        dmerged = _bdot(doutb, w_ref[2], 1, 1)
        dya, dyb, dga, dgB = vjp_m(dmerged)
        dga_ref[...] = dga
        dgB_ref[...] = dgB
        dyab, dybb = dya.astype(BF16), dyb.astype(BF16)
        da1 = _bdot(dyab, w_ref[0], 1, 1)
        da2 = _bdot(dybb, w_ref[1], 1, 1)

        daw = jnp.zeros_like(aw)
        for h in range(NA):
            sl = slice(h * DA, (h + 1) * DA)
            do, dz, dw = pa[h][1](da1[:, sl])
            doa_ref[:, sl] = do
            dz_ref[:, sl] = dz
            daw = daw + dw
        dbw = jnp.zeros_like(bw)
        for h in range(NB):
            sl = slice(h * DVB, (h + 1) * DVB)
            do, dg, dw = pb[h][1](da2[:, sl])
            dob_ref[:, sl] = do
            dgb_ref[:, sl] = dg
            dbw = dbw + dw

        @pl.when(first)
        def _():
            loss_ref[...] = jnp.zeros_like(loss_ref)
            dw_ref[...] = jnp.zeros_like(dw_ref)
            daw_ref[...] = jnp.zeros_like(daw_ref)
            dbw_ref[...] = jnp.zeros_like(dbw_ref)
            dlw_ref[...] = jnp.zeros_like(dlw_ref)

        loss_ref[...] += jnp.broadcast_to(loss, loss_ref.shape)
        dw_ref[0] += _bdot(a1, dyab, 0, 0)
        dw_ref[1] += _bdot(a2, dybb, 0, 0)
        dw_ref[2] += _bdot(mb, doutb, 0, 0)
        daw_ref[...] += jnp.broadcast_to(daw, daw_ref.shape)
        dbw_ref[...] += jnp.broadcast_to(dbw, dbw_ref.shape)
        dlw_ref[...] += jnp.broadcast_to(dlw, dlw_ref.shape)

    two = pl.BlockSpec((2, tm, D), lambda i: (0, i, 0))
    pcol = lambda c: pl.BlockSpec((tm, D), lambda i: (i, c))
    tok = pl.BlockSpec((tm, D), lambda i: (i, 0))
    row = lambda n: pl.BlockSpec((1, n), lambda i: (0, 0))
    row8 = lambda n: pl.BlockSpec((8, n), lambda i: (0, 0))
    once = pl.Buffered(1)
    tokf = jax.ShapeDtypeStruct((t, D), F32)
    return pl.pallas_call(
        body, name="post", grid=(t // tm,),
        in_specs=[two, two, pcol(3), pcol(6), pcol(7), pcol(8), tok, tok, row(DA), row(DVB), row(D),
                  pl.BlockSpec((3, D, D), lambda i: (0, 0, 0), pipeline_mode=once)],
        out_specs=[row8(128), tok, tok, tok, tok, tok, tok, tok,
                   pl.BlockSpec((3, D, D), lambda i: (0, 0, 0), pipeline_mode=once),
                   row8(DA), row8(DVB), row8(D)],
        out_shape=[jax.ShapeDtypeStruct((8, 128), F32), tokf, tokf, tokf, tokf, tokf, tokf, tokf,
                   jax.ShapeDtypeStruct((3, D, D), F32),
                   jax.ShapeDtypeStruct((8, DA), F32), jax.ShapeDtypeStruct((8, DVB), F32),
                   jax.ShapeDtypeStruct((8, D), F32)],
        compiler_params=_cparams(("arbitrary",), vmem_mb=56),
    )(oa2, ob2, p, p, p, p, x, tgt, gdn_w, gla_w, lnpost, w3)


def _adam(w, g, m, v, tr):
    rows, cols = w.shape

    def body(w_ref, g_ref, m_ref, v_ref, d_ref, nm_ref, nv_ref):
        gg = g_ref[...]
        nm = B1 * m_ref[...] + (1.0 - B1) * gg
        nv = B2 * v_ref[...] + (1.0 - B2) * (gg * gg)
        m_hat = nm / (1.0 - B1 ** STEP)
        v_hat = nv / (1.0 - B2 ** STEP)
        d_ref[...] = -LR * (m_hat / (jnp.sqrt(v_hat) + ADAM_EPS) + WD * w_ref[...])
        nm_ref[...] = nm
        nv_ref[...] = nv

    blk = pl.BlockSpec((tr, cols), lambda i: (i, 0))
    shp = jax.ShapeDtypeStruct((rows, cols), F32)
    return pl.pallas_call(
        body, name=f"adam_{rows}x{cols}", grid=(rows // tr,),
        in_specs=[blk] * 4, out_specs=[blk] * 3, out_shape=[shp] * 3,
        compiler_params=_cparams(("parallel",)),
    )(w, g, m, v)


def _sum_cast(own, got):
    _, ns, r, c = own.shape
    tr = r // 4 if r >= 64 else r

    def body(c_ref, a_ref, b_ref, f_ref, h_ref):
        s = a_ref[...] + b_ref[...]
        f_ref[...] = s
        h_ref[...] = s.astype(BF16)

    return pl.pallas_call(
        body, name=f"sum_cast_{r}x{c}",
        grid_spec=pltpu.PrefetchScalarGridSpec(
            num_scalar_prefetch=1, grid=(ns, r // tr),
            in_specs=[pl.BlockSpec((None, None, tr, c), lambda s, i, cc: (cc[0], s, i, 0)),
                      pl.BlockSpec((None, tr, c), lambda s, i, cc: (s, i, 0))],
            out_specs=[pl.BlockSpec((None, tr, c), lambda s, i, cc: (s, i, 0)),
                       pl.BlockSpec((None, tr, c), lambda s, i, cc: (s, i, 0))]),
        out_shape=[jax.ShapeDtypeStruct((ns, r, c), F32), jax.ShapeDtypeStruct((ns, r, c), BF16)],
        compiler_params=_cparams(("parallel", "parallel")),
    )(lax.axis_index("c").reshape(1), own, got)


def _sum4(mine, got):
    _, r, c = mine.shape
    tr = r // 4 if r >= 64 else r

    def body(s_ref, a_ref, g_ref, o_ref):
        acc = a_ref[...] + g_ref[0].astype(F32)
        acc = acc + g_ref[1].astype(F32)
        o_ref[...] = acc + g_ref[2].astype(F32)

    shard = (2 * lax.axis_index("x") + lax.axis_index("y")).reshape(1)
    return pl.pallas_call(
        body, name=f"sum4_{r}x{c}",
        grid_spec=pltpu.PrefetchScalarGridSpec(
            num_scalar_prefetch=1, grid=(r // tr,),
            in_specs=[pl.BlockSpec((None, tr, c), lambda i, ss: (ss[0], i, 0)),
                      pl.BlockSpec((3, tr, c), lambda i, ss: (0, i, 0))],
            out_specs=pl.BlockSpec((tr, c), lambda i, ss: (i, 0))),
        out_shape=jax.ShapeDtypeStruct((r, c), F32),
        compiler_params=_cparams(("parallel",)),
    )(shard, mine, got)


def _place():
    x, y, c = lax.axis_index("x"), lax.axis_index("y"), lax.axis_index("c")
    chips = [(1 - x, y), (x, 1 - y), (1 - x, 1 - y)]
    return x, y, c, chips


def _gather_weights(parts):
    npart = len(parts)

    def body(*refs):
        ins, outs = refs[:npart], refs[npart:2 * npart]
        send_sems, recv_sems, local_sems = refs[2 * npart:]
        x, y, c, chips = _place()
        sibling = (x, y, 1 - c)
        mine = 2 * x + y

        def remote(k, p, shard, half, to, src=None):
            dst = outs[p].at[shard, half]
            return pltpu.make_async_remote_copy(
                src_ref=dst if src is None else src, dst_ref=dst,
                send_sem=send_sems.at[k], recv_sem=recv_sems.at[k], device_id=to, device_id_type=MESH)

        local = [pltpu.make_async_copy(ins[p], outs[p].at[mine], local_sems.at[p]) for p in range(npart)]
        for cp in local:
            cp.start()
        first = [remote(j * npart + p, p, mine, c, (*chip, c), src=ins[p].at[c])
                 for j, chip in enumerate(chips) for p in range(npart)]
        for cp in first:
            cp.start()
        passed = []
        for j, (cx, cy) in enumerate(chips):
            for p in range(npart):
                remote(j * npart + p, p, 2 * cx + cy, c, (x, y, c)).wait_recv()
                fw = remote((3 + j) * npart + p, p, 2 * cx + cy, c, sibling)
                fw.start()
                passed.append(fw)
        for j, (cx, cy) in enumerate(chips):
            for p in range(npart):
                remote((3 + j) * npart + p, p, 2 * cx + cy, 1 - c, (x, y, c)).wait_recv()
        for cp in first + passed:
            cp.wait_send()
        for cp in local:
            cp.wait()

    return pl.pallas_call(
        body, name="gather_weights",
        in_specs=[ANY] * npart, out_specs=[ANY] * npart,
        out_shape=[jax.ShapeDtypeStruct((NSHARD,) + a.shape, a.dtype) for a in parts],
        scratch_shapes=[pltpu.SemaphoreType.DMA((6 * npart,)), pltpu.SemaphoreType.DMA((6 * npart,)),
                        pltpu.SemaphoreType.DMA((npart,))],
    )(*parts)


def _swap_halves(parts):
    npart = len(parts)

    def body(*refs):
        ins, outs = refs[:npart], refs[npart:2 * npart]
        send_sems, recv_sems = refs[2 * npart:]
        x, y, c, _ = _place()
        cps = [pltpu.make_async_remote_copy(
            src_ref=ins[p].at[1 - c], dst_ref=outs[p], send_sem=send_sems.at[p], recv_sem=recv_sems.at[p],
            device_id=(x, y, 1 - c), device_id_type=MESH) for p in range(npart)]
        for cp in cps:
            cp.start()
        for cp in cps:
            cp.wait()

    return pl.pallas_call(
        body, name="swap_halves", in_specs=[ANY] * npart, out_specs=[ANY] * npart,
        out_shape=[jax.ShapeDtypeStruct(a.shape[1:], a.dtype) for a in parts],
        scratch_shapes=[pltpu.SemaphoreType.DMA((npart,)), pltpu.SemaphoreType.DMA((npart,))],
    )(*parts)


def _scatter_shards(parts):
    npart = len(parts)

    def body(*refs):
        ins, outs = refs[:npart], refs[npart:2 * npart]
        send_sems, recv_sems = refs[2 * npart:]
        x, y, c, chips = _place()
        cps = [pltpu.make_async_remote_copy(
            src_ref=ins[p].at[2 * cx + cy], dst_ref=outs[p].at[j],
            send_sem=send_sems.at[j * npart + p], recv_sem=recv_sems.at[j * npart + p],
            device_id=(cx, cy, c), device_id_type=MESH)
            for j, (cx, cy) in enumerate(chips) for p in range(npart)]
        for cp in cps:
            cp.start()
        for cp in cps:
            cp.wait()

    return pl.pallas_call(
        body, name="scatter_shards", in_specs=[ANY] * npart, out_specs=[ANY] * npart,
        out_shape=[jax.ShapeDtypeStruct((3,) + a.shape[1:], a.dtype) for a in parts],
        scratch_shapes=[pltpu.SemaphoreType.DMA((3 * npart,)), pltpu.SemaphoreType.DMA((3 * npart,))],
    )(*parts)


def _join_halves(parts):
    npart = len(parts)

    def body(*refs):
        ins, outs = refs[:npart], refs[npart:2 * npart]
        send_sems, recv_sems, local_sems = refs[2 * npart:]
        x, y, c, _ = _place()
        local = [pltpu.make_async_copy(ins[p], outs[p].at[c], local_sems.at[p]) for p in range(npart)]
        cps = [pltpu.make_async_remote_copy(
            src_ref=ins[p], dst_ref=outs[p].at[c], send_sem=send_sems.at[p], recv_sem=recv_sems.at[p],
            device_id=(x, y, 1 - c), device_id_type=MESH) for p in range(npart)]
        for cp in local + cps:
            cp.start()
        for cp in cps:
            cp.wait_send()
        for p in range(npart):
            pltpu.make_async_remote_copy(
                src_ref=ins[p], dst_ref=outs[p].at[1 - c], send_sem=send_sems.at[p], recv_sem=recv_sems.at[p],
                device_id=(x, y, 1 - c), device_id_type=MESH).wait_recv()
        for cp in local:
            cp.wait()

    return pl.pallas_call(
        body, name="join_halves", in_specs=[ANY] * npart, out_specs=[ANY] * npart,
        out_shape=[jax.ShapeDtypeStruct((2,) + a.shape, a.dtype) for a in parts],
        scratch_shapes=[pltpu.SemaphoreType.DMA((npart,)), pltpu.SemaphoreType.DMA((npart,)),
                        pltpu.SemaphoreType.DMA((npart,))],
    )(*parts)


def _allreduce_small(v):
    r = v.shape[0]

    def body(v_ref, o_ref, buf, send_sems, recv_sems):
        x, y, c, _ = _place()
        me = 4 * x + 2 * y + c
        buf[me] = v_ref[...]
        cps = []
        for k in range(1, 8):
            px, py, pc = x ^ (k >> 2), y ^ ((k >> 1) & 1), c ^ (k & 1)
            cps.append(pltpu.make_async_remote_copy(
                src_ref=v_ref, dst_ref=buf.at[me], send_sem=send_sems.at[k - 1], recv_sem=recv_sems.at[k - 1],
                device_id=(px, py, pc), device_id_type=MESH))
        for cp in cps:
            cp.start()
        for k in range(1, 8):
            px, py, pc = x ^ (k >> 2), y ^ ((k >> 1) & 1), c ^ (k & 1)
            pltpu.make_async_remote_copy(
                src_ref=v_ref, dst_ref=buf.at[4 * px + 2 * py + pc], send_sem=send_sems.at[k - 1],
                recv_sem=recv_sems.at[k - 1], device_id=(px, py, pc), device_id_type=MESH).wait_recv()
        for cp in cps:
            cp.wait_send()
        acc = buf[0]
        for d in range(1, 8):
            acc = acc + buf[d]
        o_ref[...] = acc

    return pl.pallas_call(
        body, name="allreduce_small",
        in_specs=[pl.BlockSpec(memory_space=pltpu.VMEM)], out_specs=pl.BlockSpec(memory_space=pltpu.VMEM),
        out_shape=jax.ShapeDtypeStruct((r, 128), F32),
        scratch_shapes=[pltpu.VMEM((8, r, 128), F32), pltpu.SemaphoreType.DMA((7,)), pltpu.SemaphoreType.DMA((7,))],
    )(v)


def _permute_cols(w):
    zeros = jnp.zeros((w.shape[0], NPERM - 9280), w.dtype)
    return jnp.concatenate([w[:, 0:4096], w[:, 4128:7200], w[:, 7232:9280], w[:, 4096:4128], w[:, 7200:7232], zeros],
                           axis=1)


def _unpermute_cols(g):
    return jnp.concatenate([g[:, 0:4096], g[:, 9216:9248], g[:, 4096:7168], g[:, 9248:9280], g[:, 7168:9216]],
                           axis=1)


SMALL_NAMES = ("ln_pre_w", "a_log_fwd", "a_log_bwd", "dt_bias_fwd", "dt_bias_bwd", "gdn_norm_w",
               "gk_b2_fwd", "gk_b2_bwd", "gla_norm_w", "ln_post_w")
SMALL_SIZES = (1024, 8, 8, 8, 8, 128, 512, 512, 256, 1024)
SMALL_ROWS = 32


def _pack_small(vals):
    flat = jnp.concatenate([v.reshape(-1) for v in vals])
    return jnp.pad(flat, (0, SMALL_ROWS * 128 - flat.shape[0])).reshape(SMALL_ROWS, 128)


def _unpack_small(packed):
    flat = packed.reshape(-1)
    out, o = [], 0
    for n in SMALL_SIZES:
        out.append(flat[o:o + n].reshape(1, n))
        o += n
    return out


def _pack_shard_small(conv, w2f, w2b):
    top = jnp.pad(conv, ((0, 8 - conv.shape[0]), (0, 0)))
    mid = jnp.pad(jnp.concatenate([w2f, w2b], axis=1), ((0, 0), (0, 768 - 256)))
    return jnp.concatenate([top, mid, jnp.zeros((8, 768), conv.dtype)], axis=0)


def _unpack_shard_small(a):
    return a[0:5], a[8:24, 0:128], a[8:24, 128:256]


def kernel(x, ln_pre_w, w_in, conv_w, a_log_fwd, a_log_bwd, dt_bias_fwd, dt_bias_bwd, gdn_norm_w, w_proj_gdn, gk_w2_fwd, gk_b2_fwd, gk_w2_bwd, gk_b2_bwd, gla_norm_w, w_proj_gla, w_out, ln_post_w, loss_target, m_ln_pre_w, m_w_in, m_conv_w, m_a_log_fwd, m_a_log_bwd, m_dt_bias_fwd, m_dt_bias_bwd, m_gdn_norm_w, m_w_proj_gdn, m_gk_w2_fwd, m_gk_b2_fwd, m_gk_w2_bwd, m_gk_b2_bwd, m_gla_norm_w, m_w_proj_gla, m_w_out, m_ln_post_w, v_ln_pre_w, v_w_in, v_conv_w, v_a_log_fwd, v_a_log_bwd, v_dt_bias_fwd, v_dt_bias_bwd, v_gdn_norm_w, v_w_proj_gdn, v_gk_w2_fwd, v_gk_b2_fwd, v_gk_w2_bwd, v_gk_b2_bwd, v_gla_norm_w, v_w_proj_gla, v_w_out, v_ln_post_w):
    t = x.shape[1]
    x2, tgt = x[0], loss_target[0]

    win_l = w_in[0].astype(BF16).reshape(2, D // 2, SHW)
    proj_l = jnp.concatenate([w_proj_gdn[0], w_proj_gla[0], w_out[0]], axis=0).astype(BF16).reshape(2, 384, D)
    small_l = _pack_shard_small(conv_w[0], gk_w2_fwd[0], gk_w2_bwd[0]).reshape(2, 16, 768)
    win_g, proj_g, small_g = _gather_weights([win_l, proj_l, small_l])
    w_full = win_g.reshape(NSHARD, D, SHW).transpose(1, 0, 2).reshape(D, NSHARD * SHW)
    wperm = _permute_cols(w_full)
    w3 = proj_g.reshape(NSHARD, 3, D // NSHARD, D).transpose(1, 0, 2, 3).reshape(3, D, D)
    small_g = small_g.reshape(NSHARD, 32, 768)
    convw = small_g[:, 0:8, :].transpose(1, 0, 2).reshape(8, 3 * D)
    w2f = small_g[:, 8:24, 0:128].transpose(1, 0, 2).reshape(16, 512)
    w2b = small_g[:, 8:24, 128:256].transpose(1, 0, 2).reshape(16, 512)
    w2f_pad = jnp.pad(w2f, ((32, 80), (0, 0)))
    w2b_pad = jnp.pad(w2b, ((48, 64), (0, 0)))
    alog_row = jnp.pad(jnp.concatenate([a_log_fwd, a_log_bwd], axis=1), ((0, 0), (0, 112)))
    dt_row = jnp.pad(jnp.concatenate([dt_bias_fwd, dt_bias_bwd], axis=1), ((0, 0), (0, 112)))

    p, h = _inproj(x2, ln_pre_w, wperm)
    qn, kn, vc = (_qkv_fwd(p, convw, kind) for kind in range(3))
    gsm, gk = _gates_fwd(p, alog_row, dt_row, w2f_pad, gk_b2_fwd, w2b_pad, gk_b2_bwd)
    g2, b2 = _gcum_fwd(gsm)
    u, w, at, qd, kd, el, tinv = _gdn_intra_fwd(qn, kn, vc, g2, b2)
    oa2, sa = _gdn_scan_fwd(u, w, at, qd, kd, el)
    qg, kdb, intra, elb = _gla_intra_fwd(p, gk)
    ob2, sb = _gla_scan_fwd(p, qg, kdb, intra, elb)

    (loss8, doa, dob, dz, dgb, dga, dgB, dyres, dw3, dgdn_w, dgla_w, dlnpost) = _post(
        oa2, ob2, p, x2, tgt, gdn_norm_w, gla_norm_w, ln_post_w, w3)

    du, dw, dat, dqd, dkd, del_ = _gdn_scan_bwd(u, w, at, qd, kd, el, sa, doa)
    dqn, dkn, dvc, dg2, db2 = _gdn_intra_bwd(qn, kn, vc, g2, b2, tinv, du, dw, dat, dqd, dkd, del_)
    dgsm = _gcum_bwd(gsm, dg2, db2)
    dqg, dkdb, dvs, delb = _gla_scan_bwd(p, qg, kdb, elb, sb, dob)
    dqb, dkb, dvb, dgk = _gla_intra_bwd(p, gk, dqg, dkdb, dvs, delb, dob)
    (dps, dalog8, ddt8, dw2f_pad, db2f8, dw2b_pad, db2b8) = _gates_bwd(
        p, alog_row, dt_row, w2f_pad, gk_b2_fwd, w2b_pad, gk_b2_bwd, dgsm, dgk)
    dpre, dconv = zip(*[_qkv_bwd(p, convw, g, kind) for kind, g in enumerate((dqn, dkn, dvc))])

    dp = jnp.concatenate([a.astype(BF16) for a in (*dpre, dz, dqb, dkb, dvb, dgb, dga, dgB, dps)]
                         + [jnp.zeros((t, NPERM - 9344), BF16)], axis=1)
    dwperm = _inproj_dw(h, dp)
    dx, dlnpre8 = _inproj_dx(dp, wperm, x2, ln_pre_w, dyres)

    dw_in_full = _unpermute_cols(dwperm)
    g_in = dw_in_full.reshape(2, D // 2, NSHARD, SHW).transpose(0, 2, 1, 3)
    g_proj = dw3.reshape(3, NSHARD, D // NSHARD, D).transpose(1, 0, 2, 3).reshape(NSHARD, 2, 384, D)
    g_proj = g_proj.transpose(1, 0, 2, 3)
    dconv_full = jnp.concatenate(dconv, axis=1)
    dw2f, dw2b = dw2f_pad[32:48], dw2b_pad[48:64]
    g_small = jnp.stack([_pack_shard_small(dconv_full[0:5, 768 * s:768 * (s + 1)],
                                           dw2f[:, 128 * s:128 * (s + 1)], dw2b[:, 128 * s:128 * (s + 1)])
                         for s in range(NSHARD)])
    g_small = g_small.reshape(NSHARD, 2, 16, 768).transpose(1, 0, 2, 3)
    parts = [g_in, g_proj, g_small]
    got = _swap_halves(parts)
    sums = [_sum_cast(a, b) for a, b in zip(parts, got)]
    landed = _scatter_shards([hb for _, hb in sums])
    halves = [_sum4(f, g) for (f, _), g in zip(sums, landed)]
    r_in, r_proj, r_small = _join_halves(halves)
    grad_w_in = r_in.reshape(D, SHW)
    r_proj = r_proj.reshape(768, D)
    r_small = r_small.reshape(32, 768)

    small_grads = (dlnpre8[0:1], dalog8[0:1, 0:8], dalog8[0:1, 8:16], ddt8[0:1, 0:8], ddt8[0:1, 8:16],
                   dgdn_w[0:1], db2f8[0:1], db2b8[0:1], dgla_w[0:1], dlnpost[0:1])
    gsmall = _allreduce_small(_pack_small(small_grads))

    d_in, nm_in, nv_in = _adam(w_in[0], grad_w_in, m_w_in[0], v_w_in[0], 128)
    stack3 = lambda a, b, c: jnp.concatenate([a[0], b[0], c[0]], axis=0)
    d_pr, nm_pr, nv_pr = _adam(stack3(w_proj_gdn, w_proj_gla, w_out), r_proj,
                               stack3(m_w_proj_gdn, m_w_proj_gla, m_w_out),
                               stack3(v_w_proj_gdn, v_w_proj_gla, v_w_out), 256)
    d_ss, nm_ss, nv_ss = _adam(_pack_shard_small(conv_w[0], gk_w2_fwd[0], gk_w2_bwd[0]), r_small,
                               _pack_shard_small(m_conv_w[0], m_gk_w2_fwd[0], m_gk_w2_bwd[0]),
                               _pack_shard_small(v_conv_w[0], v_gk_w2_fwd[0], v_gk_w2_bwd[0]), 32)
    smalls = dict(ln_pre_w=(ln_pre_w, m_ln_pre_w, v_ln_pre_w), a_log_fwd=(a_log_fwd, m_a_log_fwd, v_a_log_fwd),
                  a_log_bwd=(a_log_bwd, m_a_log_bwd, v_a_log_bwd),
                  dt_bias_fwd=(dt_bias_fwd, m_dt_bias_fwd, v_dt_bias_fwd),
                  dt_bias_bwd=(dt_bias_bwd, m_dt_bias_bwd, v_dt_bias_bwd),
                  gdn_norm_w=(gdn_norm_w, m_gdn_norm_w, v_gdn_norm_w),
                  gk_b2_fwd=(gk_b2_fwd, m_gk_b2_fwd, v_gk_b2_fwd), gk_b2_bwd=(gk_b2_bwd, m_gk_b2_bwd, v_gk_b2_bwd),
                  gla_norm_w=(gla_norm_w, m_gla_norm_w, v_gla_norm_w), ln_post_w=(ln_post_w, m_ln_post_w, v_ln_post_w))
    ws, ms, vs = (_pack_small([smalls[n][i] for n in SMALL_NAMES]) for i in range(3))
    d_sm, nm_sm, nv_sm = _adam(ws, gsmall, ms, vs, SMALL_ROWS)

    def family(in_, pr, ss, sm):
        conv, w2f_, w2b_ = _unpack_shard_small(ss)
        s = dict(zip(SMALL_NAMES, _unpack_small(sm)))
        return [s["ln_pre_w"], in_[None], conv[None], s["a_log_fwd"], s["a_log_bwd"], s["dt_bias_fwd"],
                s["dt_bias_bwd"], s["gdn_norm_w"], pr[None, 0:256], w2f_[None], s["gk_b2_fwd"], w2b_[None],
                s["gk_b2_bwd"], s["gla_norm_w"], pr[None, 256:512], pr[None, 512:768], s["ln_post_w"]]

    loss = lax.psum(loss8[0, 0], ("x", "y", "c"))
    outs = [loss, dx[None]]
    outs += family(grad_w_in, r_proj, r_small, gsmall)
    outs += family(d_in, d_pr, d_ss, d_sm)
    outs += family(nm_in, nm_pr, nm_ss, nm_sm)
    outs += family(nv_in, nv_pr, nv_ss, nv_sm)
    return tuple(outs)
```

```python
import functools

import jax
import jax.numpy as jnp
from jax import lax
from jax.experimental import pallas as pl
from jax.experimental.pallas import tpu as pltpu

F32 = jnp.float32
BF16 = jnp.bfloat16
HI = lax.Precision.HIGHEST
MESH = pl.DeviceIdType.MESH

D = 1024
CH = 64
EPS = 1e-6
NA, DA = 8, 128
NB, DKB, DVB = 4, 128, 256
NSHARD = 4
SHW = 2320
NPERM = 9728
PS_BLOCK = 72
LR, B1, B2, ADAM_EPS, WD, STEP = 0.001, 0.9, 0.999, 1e-08, 0.01, 10

ANY = pl.BlockSpec(memory_space=pl.ANY)


def _cparams(sem=None, vmem_mb=48):
    return pltpu.CompilerParams(dimension_semantics=sem, vmem_limit_bytes=vmem_mb << 20)


def _bdot(a, b, ca, cb):
    return lax.dot_general(a.astype(BF16), b.astype(BF16), (((ca,), (cb,)), ((), ())),
                           preferred_element_type=F32)


@jax.custom_vjp
def mm(a, b):
    return _bdot(a, b, 1, 0)


def _mm_fwd(a, b):
    return _bdot(a, b, 1, 0), (a, b)


def _mm_bwd(res, g):
    a, b = res
    return _bdot(g, b, 1, 1), _bdot(a, g, 0, 0)


mm.defvjp(_mm_fwd, _mm_bwd)


@jax.custom_vjp
def mm_nt(a, b):
    return _bdot(a, b, 1, 1)


def _mm_nt_fwd(a, b):
    return _bdot(a, b, 1, 1), (a, b)


def _mm_nt_bwd(res, g):
    a, b = res
    return _bdot(g, b, 1, 0), _bdot(g, a, 0, 0)


mm_nt.defvjp(_mm_nt_fwd, _mm_nt_bwd)


@jax.custom_vjp
def mm_tn(a, b):
    return _bdot(a, b, 0, 0)


def _mm_tn_fwd(a, b):
    return _bdot(a, b, 0, 0), (a, b)


def _mm_tn_bwd(res, g):
    a, b = res
    return _bdot(b, g, 1, 1), _bdot(a, g, 1, 0)


mm_tn.defvjp(_mm_tn_fwd, _mm_tn_bwd)


def dot_hi(a, b):
    return lax.dot_general(a, b, (((1,), (0,)), ((), ())), precision=HI, preferred_element_type=F32)


def _split3(x):
    x1 = x.astype(BF16)
    r = x - x1.astype(F32)
    x2 = r.astype(BF16)
    return x1, x2, (r - x2.astype(F32)).astype(BF16)


def _cdot(c, x, cc, cx, c_first=True):
    parts = _split3(x)
    if c_first:
        return _bdot(c, parts[0], cc, cx) + _bdot(c, parts[1], cc, cx) + _bdot(c, parts[2], cc, cx)
    return _bdot(parts[0], c, cx, cc) + _bdot(parts[1], c, cx, cc) + _bdot(parts[2], c, cx, cc)


@jax.custom_vjp
def cmm(c, x):
    return _cdot(c, x, 1, 0)


def _cmm_fwd(c, x):
    return _cdot(c, x, 1, 0), c


def _cmm_bwd(c, g):
    return jnp.zeros_like(c), _cdot(c, g, 0, 0)


cmm.defvjp(_cmm_fwd, _cmm_bwd)


@jax.custom_vjp
def mmc(x, c):
    return _cdot(c, x, 0, 1, c_first=False)


def _mmc_fwd(x, c):
    return _cdot(c, x, 0, 1, c_first=False), c


def _mmc_bwd(c, g):
    return _cdot(c, g, 1, 1, c_first=False), jnp.zeros_like(c)


mmc.defvjp(_mmc_fwd, _mmc_bwd)


def _sigmoid(x):
    return 1.0 / (1.0 + jnp.exp(-x))


def _silu(x):
    return x * _sigmoid(x)


def _softplus(x):
    return jnp.maximum(x, 0.0) + jnp.log(1.0 + jnp.exp(-jnp.abs(x)))


def _rms(x, w):
    return x * lax.rsqrt(jnp.mean(x * x, axis=-1, keepdims=True) + EPS) * w


SC = 256


class _Consts:
    def __init__(self, rev):
        r = lax.broadcasted_iota(jnp.int32, (SC, SC), 0)
        c = lax.broadcasted_iota(jnp.int32, (SC, SC), 1)
        same = (r >> 6) == (c >> 6)
        a = jnp.where(rev, c, r)
        b = jnp.where(rev, r, c)
        self.incl = same & (a >= b)
        self.strict = same & (a > b)
        self.incl_f = self.incl.astype(F32)
        self.eye = (r == c).astype(F32)
        rows = lax.broadcasted_iota(jnp.int32, (SC, 1), 0)
        self.last_col = ((rows & (CH - 1)) == jnp.where(rev, 0, CH - 1)).astype(F32)
        rr = lax.broadcasted_iota(jnp.int32, (SC, CH), 0)
        cc = lax.broadcasted_iota(jnp.int32, (SC, CH), 1)
        self.fold = ((rr & (CH - 1)) == cc).astype(F32)


def _dot3(a, b, ca=1, cb=0):
    ah, bh = a.astype(BF16), b.astype(BF16)
    al, bl = (a - ah.astype(F32)).astype(BF16), (b - bh.astype(F32)).astype(BF16)
    return _bdot(ah, bh, ca, cb) + (_bdot(ah, bl, ca, cb) + _bdot(al, bh, ca, cb))


TRI_SPLIT_LEVELS = 2


def _tri_inv(low, eye):
    n = -low
    acc = eye + n
    p = n
    for level in range(5):
        dot = _dot3 if level < TRI_SPLIT_LEVELS else (lambda a, b: _bdot(a, b, 1, 0))
        p = dot(p, p)
        acc = acc + dot(acc, p)
    return acc


@jax.custom_vjp
def _solve2(low, rv, rk, tinv):
    x = _dot3(tinv, jnp.concatenate([rv, rk], axis=1))
    return x[:, :DA], x[:, DA:]


def _solve2_fwd(low, rv, rk, tinv):
    x = _dot3(tinv, jnp.concatenate([rv, rk], axis=1))
    return (x[:, :DA], x[:, DA:]), (x, tinv)


def _solve2_bwd(res, g):
    x, tinv = res
    drhs = _dot3(tinv, jnp.concatenate(g, axis=1), 0, 0)
    return -_dot3(drhs, x, 1, 1), drhs[:, :DA], drhs[:, DA:], jnp.zeros_like(tinv)


_solve2.defvjp(_solve2_fwd, _solve2_bwd)


def _chunk_last(x, cs):
    xs = (x * cs.last_col).reshape(SC // CH, CH, x.shape[1])
    return jnp.broadcast_to(jnp.sum(xs, axis=1, keepdims=True), xs.shape).reshape(x.shape)


def _gdn_decay(g, cs):
    gw = jnp.concatenate([g] * (SC // DA), axis=1)
    grow = jnp.sum(cs.eye * gw, axis=0, keepdims=True)
    return jnp.where(cs.incl, jnp.exp(jnp.where(cs.incl, gw - grow, 0.0)), 0.0)


def _gdn_intra(q, k, v, g, bx, tinv, cs):
    decay = _gdn_decay(g, cs)
    kb = k * bx
    low = jnp.where(cs.strict, mm_nt(kb, k) * decay, 0.0)
    eg = jnp.exp(g)
    made = tinv is None
    if made:
        tinv = _tri_inv(low, cs.eye)
    u, w = _solve2(low, v * bx, kb * eg, tinv)
    attn = mmc(mm_nt(q, k) * decay, cs.fold)
    qd = q * eg
    glast = _chunk_last(g, cs)
    kd = k * jnp.exp(glast - g)
    outs = (u, w, attn, qd, kd, jnp.exp(glast))
    return outs + (tinv,) if made else outs


def _gdn_scan(u, w, attn, qd, kd, el, s):
    vn = u - mm(w, s)
    o = mm(qd, s) + mm(attn, vn)
    sn = s * el + mm_tn(kd, vn)
    return o, sn


def _gla_intra(q, k, v, gk, cs):
    gc = cmm(cs.incl_f, gk)
    qg = q * (DKB ** -0.5) * jnp.exp(gc)
    kg = k * jnp.exp(-gc)
    attn = jnp.where(cs.incl, mm_nt(qg, kg), 0.0)
    intra = mm(attn, v)
    glast = _chunk_last(gc, cs)
    kd = k * jnp.exp(glast - gc)
    return qg, kd, intra, jnp.exp(glast)


def _gla_scan(qg, kd, v, el, st):
    o = mm_nt(qg, st)
    stn = st * el + mm_tn(v, kd)
    return o, stn


def _shift_rows(x, s):
    if s == 0:
        return x
    t = x.shape[0]
    rolled = pltpu.roll(x, (-s) % t, 0)
    rows = lax.broadcasted_iota(jnp.int32, x.shape, 0)
    return jnp.where((rows + s >= 0) & (rows + s < t), rolled, 0.0)


@jax.custom_vjp
def _conv5(x, w):
    acc = w[0:1] * _shift_rows(x, -2)
    for j in range(1, 5):
        acc = acc + w[j:j + 1] * _shift_rows(x, j - 2)
    return acc


def _conv5_fwd(x, w):
    return _conv5(x, w), (x, w)


def _conv5_bwd(res, g):
    x, w = res
    dx = w[0:1] * _shift_rows(g, 2)
    for j in range(1, 5):
        dx = dx + w[j:j + 1] * _shift_rows(g, 2 - j)
    rows = lax.broadcasted_iota(jnp.int32, w.shape, 0)
    dw = jnp.zeros_like(w)
    for j in range(5):
        dwj = jnp.sum(g * _shift_rows(x, j - 2), axis=0, keepdims=True)
        dw = dw + jnp.where(rows == j, dwj, 0.0)
    return dx, dw


_conv5.defvjp(_conv5_fwd, _conv5_bwd)


def _qkv_act(kind):
    def f(x, w):
        c = _silu(_conv5(x, w))
        if kind == 2:
            return c
        c = c * lax.rsqrt(jnp.sum(c * c, axis=-1, keepdims=True) + EPS)
        return c * (DA ** -0.5) if kind == 0 else c
    return f


def _inproj(x, lnw, wperm, tm=512, tn=512):
    t = x.shape[0]

    def body(x_ref, lnw_ref, w_ref, p_ref, h_ref, hbuf):
        @pl.when(pl.program_id(1) == 0)
        def _():
            hb = _rms(x_ref[...], lnw_ref[...]).astype(BF16)
            hbuf[...] = hb
            h_ref[...] = hb
        p_ref[...] = jnp.dot(hbuf[...], w_ref[...], preferred_element_type=F32)

    return pl.pallas_call(
        body, name="inproj", grid=(t // tm, NPERM // tn),
        in_specs=[pl.BlockSpec((tm, D), lambda i, j: (i, 0)),
                  pl.BlockSpec((1, D), lambda i, j: (0, 0)),
                  pl.BlockSpec((D, tn), lambda i, j: (0, j))],
        out_specs=[pl.BlockSpec((tm, tn), lambda i, j: (i, j)),
                   pl.BlockSpec((tm, D), lambda i, j: (i, 0))],
        out_shape=[jax.ShapeDtypeStruct((t, NPERM), F32),
                   jax.ShapeDtypeStruct((t, D), BF16)],
        scratch_shapes=[pltpu.VMEM((tm, D), BF16)],
        compiler_params=_cparams(("parallel", "arbitrary")),
    )(x, lnw, wperm)


def _inproj_dw(h, dp, tm=512, tn=512):
    t = h.shape[0]
    ncol = dp.shape[1]

    def body(h_ref, dp_ref, dw_ref):
        @pl.when(pl.program_id(1) == 0)
        def _():
            dw_ref[...] = jnp.zeros_like(dw_ref)
        dw_ref[...] += _bdot(h_ref[...], dp_ref[...], 0, 0)

    return pl.pallas_call(
        body, name="inproj_dw", grid=(ncol // tn, t // tm),
        in_specs=[pl.BlockSpec((tm, D), lambda j, i: (i, 0)),
                  pl.BlockSpec((tm, tn), lambda j, i: (i, j))],
        out_specs=pl.BlockSpec((D, tn), lambda j, i: (0, j)),
        out_shape=jax.ShapeDtypeStruct((D, ncol), F32),
        compiler_params=_cparams(("parallel", "arbitrary")),
    )(h, dp)


def _inproj_dx(dp, wperm, x, lnw, dyres, tm=512, tn=512):
    t = x.shape[0]
    ncol = dp.shape[1]
    nj = ncol // tn

    def body(dp_ref, w_ref, x_ref, lnw_ref, dy_ref, dx_ref, dlnw_ref, acc):
        j = pl.program_id(1)

        @pl.when(j == 0)
        def _():
            acc[...] = jnp.zeros_like(acc)

        acc[...] += _bdot(dp_ref[...], w_ref[...], 1, 1)

        @pl.when(j == nj - 1)
        def _():
            _, vjp = jax.vjp(_rms, x_ref[...], lnw_ref[...])
            dx, dlnw = vjp(acc[...])
            dx_ref[...] = dx + dy_ref[...]

            @pl.when(pl.program_id(0) == 0)
            def _():
                dlnw_ref[...] = jnp.zeros_like(dlnw_ref)
            dlnw_ref[...] += jnp.broadcast_to(dlnw, dlnw_ref.shape)

    return pl.pallas_call(
        body, name="inproj_dx", grid=(t // tm, nj),
        in_specs=[pl.BlockSpec((tm, tn), lambda i, j: (i, j)),
                  pl.BlockSpec((D, tn), lambda i, j: (0, j)),
                  pl.BlockSpec((tm, D), lambda i, j: (i, 0)),
                  pl.BlockSpec((1, D), lambda i, j: (0, 0)),
                  pl.BlockSpec((tm, D), lambda i, j: (i, 0))],
        out_specs=[pl.BlockSpec((tm, D), lambda i, j: (i, 0)),
                   pl.BlockSpec((8, D), lambda i, j: (0, 0))],
        out_shape=[jax.ShapeDtypeStruct((t, D), F32), jax.ShapeDtypeStruct((8, D), F32)],
        scratch_shapes=[pltpu.VMEM((tm, D), F32)],
        compiler_params=_cparams(("arbitrary", "arbitrary")),
    )(dp, wperm, x, lnw, dyres)


def _qkv_fwd(p, convw, kind):
    t = p.shape[0]
    f = _qkv_act(kind)

    def body(p_ref, w_ref, o_ref):
        o_ref[...] = f(p_ref[...], w_ref[...])

    return pl.pallas_call(
        body, name=f"qkv_fwd{kind}", grid=(NA,),
        in_specs=[pl.BlockSpec((t, DA), lambda h: (0, kind * NA + h)),
                  pl.BlockSpec((8, DA), lambda h: (0, kind * NA + h))],
        out_specs=pl.BlockSpec((t, DA), lambda h: (0, h)),
        out_shape=jax.ShapeDtypeStruct((t, D), F32),
        compiler_params=_cparams(("parallel",)),
    )(p, convw)


def _qkv_bwd(p, convw, dout, kind):
    t = p.shape[0]
    f = _qkv_act(kind)

    def body(p_ref, w_ref, g_ref, dx_ref, dw_ref):
        _, vjp = jax.vjp(f, p_ref[...], w_ref[...])
        dx, dw = vjp(g_ref[...])
        dx_ref[...] = dx
        dw_ref[...] = dw

    return pl.pallas_call(
        body, name=f"qkv_bwd{kind}", grid=(NA,),
        in_specs=[pl.BlockSpec((t, DA), lambda h: (0, kind * NA + h)),
                  pl.BlockSpec((8, DA), lambda h: (0, kind * NA + h)),
                  pl.BlockSpec((t, DA), lambda h: (0, h))],
        out_specs=[pl.BlockSpec((t, DA), lambda h: (0, h)),
                   pl.BlockSpec((8, DA), lambda h: (0, h))],
        out_shape=[jax.ShapeDtypeStruct((t, D), F32), jax.ShapeDtypeStruct((8, D), F32)],
        compiler_params=_cparams(("parallel",)),
    )(p, convw, dout)


def _gates_f(ps, alog_row, dt_row, w2f, b2f, w2b, b2b):
    lane = lax.broadcasted_iota(jnp.int32, ps.shape, 1)
    lg = -jnp.exp(alog_row) * _softplus(ps + dt_row)
    gsm = jnp.where(lane < 16, lg, jnp.where(lane < 32, _sigmoid(ps), 0.0))
    gkf = -_softplus(-(mm(ps, w2f) + b2f)) * (1.0 / 16.0)
    gkb = -_softplus(-(mm(ps, w2b) + b2b)) * (1.0 / 16.0)
    return gsm, gkf, gkb


def _gates_fwd(ps, alog_row, dt_row, w2f, b2f, w2b, b2b, tm=512):
    t = ps.shape[0]

    def body(ps_ref, a_ref, d_ref, wf_ref, bf_ref, wb_ref, bb_ref, gsm_ref, gk_ref):
        gsm, gkf, gkb = _gates_f(ps_ref[...], a_ref[...], d_ref[...], wf_ref[...], bf_ref[...],
                                 wb_ref[...], bb_ref[...])
        gsm_ref[...] = gsm
        gk_ref[0] = gkf
        gk_ref[1] = gkb

    row = lambda n: pl.BlockSpec((1, n), lambda i: (0, 0))
    mat = pl.BlockSpec((128, 512), lambda i: (0, 0))
    return pl.pallas_call(
        body, name="gates_fwd", grid=(t // tm,),
        in_specs=[pl.BlockSpec((tm, 128), lambda i: (i, PS_BLOCK)), row(128), row(128), mat, row(512), mat, row(512)],
        out_specs=[pl.BlockSpec((tm, 128), lambda i: (i, 0)),
                   pl.BlockSpec((2, tm, 512), lambda i: (0, i, 0))],
        out_shape=[jax.ShapeDtypeStruct((t, 128), F32), jax.ShapeDtypeStruct((2, t, 512), F32)],
        compiler_params=_cparams(("parallel",)),
    )(ps, alog_row, dt_row, w2f, b2f, w2b, b2b)


def _gates_bwd(ps, alog_row, dt_row, w2f, b2f, w2b, b2b, dgsm, dgk, tm=512):
    t = ps.shape[0]

    def body(ps_ref, a_ref, d_ref, wf_ref, bf_ref, wb_ref, bb_ref, dgsm_ref, dgk_ref,
             dps_ref, da_ref, dd_ref, dwf_ref, dbf_ref, dwb_ref, dbb_ref):
        _, vjp = jax.vjp(_gates_f, ps_ref[...], a_ref[...], d_ref[...], wf_ref[...], bf_ref[...],
                         wb_ref[...], bb_ref[...])
        dps, da, dd, dwf, dbf, dwb, dbb = vjp((dgsm_ref[...], dgk_ref[0], dgk_ref[1]))
        dps_ref[...] = dps
        accs = ((da_ref, da), (dd_ref, dd), (dwf_ref, dwf), (dbf_ref, dbf), (dwb_ref, dwb), (dbb_ref, dbb))

        @pl.when(pl.program_id(0) == 0)
        def _():
            for ref, _ in accs:
                ref[...] = jnp.zeros_like(ref)
        for ref, val in accs:
            ref[...] += jnp.broadcast_to(val, ref.shape)

    row = lambda n: pl.BlockSpec((1, n), lambda i: (0, 0))
    row8 = lambda n: pl.BlockSpec((8, n), lambda i: (0, 0))
    mat = pl.BlockSpec((128, 512), lambda i: (0, 0))
    return pl.pallas_call(
        body, name="gates_bwd", grid=(t // tm,),
        in_specs=[pl.BlockSpec((tm, 128), lambda i: (i, PS_BLOCK)), row(128), row(128), mat, row(512), mat, row(512),
                  pl.BlockSpec((tm, 128), lambda i: (i, 0)),
                  pl.BlockSpec((2, tm, 512), lambda i: (0, i, 0))],
        out_specs=[pl.BlockSpec((tm, 128), lambda i: (i, 0)), row8(128), row8(128), mat, row8(512), mat, row8(512)],
        out_shape=[jax.ShapeDtypeStruct((t, 128), F32),
                   jax.ShapeDtypeStruct((8, 128), F32), jax.ShapeDtypeStruct((8, 128), F32),
                   jax.ShapeDtypeStruct((128, 512), F32), jax.ShapeDtypeStruct((8, 512), F32),
                   jax.ShapeDtypeStruct((128, 512), F32), jax.ShapeDtypeStruct((8, 512), F32)],
        compiler_params=_cparams(("arbitrary",)),
    )(ps, alog_row, dt_row, w2f, b2f, w2b, b2b, dgsm, dgk)


def _rows(i):
    return pl.ds(pl.multiple_of(i * CH, CH), CH)


def _srows(i):
    return pl.ds(pl.multiple_of(i * SC, SC), SC)


def _first_row(x):
    row = lax.broadcasted_iota(jnp.int32, (8, x.shape[1]), 0)
    return jnp.where(row == 0, jnp.broadcast_to(x, (8, x.shape[1])), 0.0)


def _chunk_rows(e_ref, i):
    pad = jnp.zeros((CH - 8, 128), F32)
    return jnp.concatenate([x for c in range(SC // CH) for x in (e_ref[(SC // CH) * i + c], pad)], axis=0)


def _gcum_f(gsm, tm):
    i = lax.broadcasted_iota(jnp.int32, (tm, tm), 0)
    j = lax.broadcasted_iota(jnp.int32, (tm, tm), 1)
    same = (i >> 6) == (j >> 6)
    lower = (same & (i >= j)).astype(F32)
    upper = (same & (i <= j)).astype(F32)
    r = lax.broadcasted_iota(jnp.int32, (128, D), 0)
    head = lax.broadcasted_iota(jnp.int32, (128, D), 1) >> 7
    pick = lambda off: (r == head + off).astype(F32)
    lane = lax.broadcasted_iota(jnp.int32, gsm.shape, 1)
    run = jnp.where(lane < 8, cmm(lower, gsm), cmm(upper, gsm))
    return mmc(run, pick(0)), mmc(run, pick(8)), mmc(gsm, pick(16)), mmc(gsm, pick(24))


def _gcum_fwd(gsm, tm=256):
    t = gsm.shape[0]

    def body(s_ref, g_ref, b_ref):
        gf, gb, bf, bb = _gcum_f(s_ref[...], tm)
        g_ref[0] = gf
        g_ref[1] = gb
        b_ref[0] = bf
        b_ref[1] = bb

    two = pl.BlockSpec((2, tm, D), lambda i: (0, i, 0))
    return pl.pallas_call(
        body, name="gcum_fwd", grid=(t // tm,),
        in_specs=[pl.BlockSpec((tm, 128), lambda i: (i, 0))], out_specs=[two, two],
        out_shape=[jax.ShapeDtypeStruct((2, t, D), F32)] * 2,
        compiler_params=_cparams(("parallel",)),
    )(gsm)


def _gcum_bwd(gsm, dg2, db2, tm=256):
    t = gsm.shape[0]

    def body(s_ref, dg_ref, db_ref, ds_ref):
        _, vjp = jax.vjp(lambda s: _gcum_f(s, tm), s_ref[...])
        ds_ref[...] = vjp((dg_ref[0], dg_ref[1], db_ref[0], db_ref[1]))[0]

    two = pl.BlockSpec((2, tm, D), lambda i: (0, i, 0))
    tile = pl.BlockSpec((tm, 128), lambda i: (i, 0))
    return pl.pallas_call(
        body, name="gcum_bwd", grid=(t // tm,),
        in_specs=[tile, two, two], out_specs=tile,
        out_shape=jax.ShapeDtypeStruct((t, 128), F32),
        compiler_params=_cparams(("parallel",)),
    )(gsm, dg2, db2)


def _gdn_intra_fwd(qn, kn, vc, g2, b2):
    t = qn.shape[0]
    n = t // CH

    def body(q_ref, k_ref, v_ref, g_ref, b_ref, u_ref, w_ref, a_ref, qd_ref, kd_ref, e_ref, t_ref):
        cs = _Consts(pl.program_id(0) == 1)

        def step(i, carry):
            r = _srows(i)
            q, k, v, g, bx = q_ref[r, :], k_ref[r, :], v_ref[r, :], g_ref[r, :], b_ref[r, :]
            u, w, a, qd, kd, el, tinv = _gdn_intra(q, k, v, g, bx, None, cs)
            u_ref[r, :] = u
            w_ref[r, :] = w
            a_ref[r, :] = a
            qd_ref[r, :] = qd
            kd_ref[r, :] = kd
            t_ref[r, :] = tinv
            for c in range(SC // CH):
                e_ref[(SC // CH) * i + c] = el[c * CH:c * CH + 8]
            return carry

        lax.fori_loop(0, t // SC, step, 0)

    head = pl.BlockSpec((t, DA), lambda d, h: (0, h))
    dh = pl.BlockSpec((None, t, DA), lambda d, h: (d, 0, h))
    sq = lambda w: pl.BlockSpec((None, None, t, w), lambda d, h: (d, h, 0, 0))
    big = jax.ShapeDtypeStruct((2, t, D), F32)
    return pl.pallas_call(
        body, name="gdn_intra_fwd", grid=(2, NA),
        in_specs=[head, head, head, dh, dh],
        out_specs=[dh, dh, sq(CH), dh, dh, pl.BlockSpec((None, None, n, 8, 128), lambda d, h: (d, h, 0, 0, 0)),
                   sq(SC)],
        out_shape=[big, big, jax.ShapeDtypeStruct((2, NA, t, CH), F32), big, big,
                   jax.ShapeDtypeStruct((2, NA, n, 8, 128), F32), jax.ShapeDtypeStruct((2, NA, t, SC), F32)],
        compiler_params=_cparams(("parallel", "parallel")),
    )(qn, kn, vc, g2, b2)


SCAN_TB = 256
SCAN_HB = 8


def _scan_specs(t, width, nheads, hb, along):
    nt = t // SCAN_TB
    nb = SCAN_TB // CH

    def tmap(d, tt):
        fwd = tt + d * (nt - 1 - 2 * tt)
        return fwd if along > 0 else nt - 1 - fwd

    tok = pl.BlockSpec((None, SCAN_TB, hb * width), lambda d, h, tt: (d, tmap(d, tt), h))
    per = lambda *tail: pl.BlockSpec((None, hb, nb) + tail, lambda d, h, tt: (d, h, tmap(d, tt)) + (0,) * len(tail))
    sq = pl.BlockSpec((None, hb, SCAN_TB, CH), lambda d, h, tt: (d, h, tmap(d, tt), 0))
    shared = lambda w: pl.BlockSpec((SCAN_TB, hb * w), lambda d, h, tt: (tmap(d, tt), h))
    return tok, per, sq, shared, (2, nheads // hb, nt), nb


def _gdn_scan_fwd(u, w, a, qd, kd, e):
    t = u.shape[1]
    tok, per, sq, _, grid, nb = _scan_specs(t, DA, NA, SCAN_HB, +1)

    def body(u_ref, w_ref, a_ref, qd_ref, kd_ref, e_ref, o_ref, s_ref, state):
        rev = pl.program_id(0) == 1

        @pl.when(pl.program_id(2) == 0)
        def _():
            state[...] = jnp.zeros_like(state)

        def step(i, ss):
            ci = jnp.where(rev, nb - 1 - i, i)
            r = _rows(ci)
            out = []
            for hh, s in enumerate(ss):
                c = slice(hh * DA, (hh + 1) * DA)
                s_ref[hh, ci] = s
                o, sn = _gdn_scan(u_ref[r, c], w_ref[r, c], a_ref[hh, r, :], qd_ref[r, c], kd_ref[r, c],
                                  e_ref[hh, ci][0:1], s)
                o_ref[r, c] = o
                out.append(sn)
            return tuple(out)

        ss = lax.fori_loop(0, nb, step, tuple(state[hh] for hh in range(SCAN_HB)))
        for hh, s in enumerate(ss):
            state[hh] = s

    return pl.pallas_call(
        body, name="gdn_scan_fwd", grid=grid,
        in_specs=[tok, tok, sq, tok, tok, per(8, 128)],
        out_specs=[tok, per(DA, DA)],
        out_shape=[jax.ShapeDtypeStruct((2, t, D), F32), jax.ShapeDtypeStruct((2, NA, t // CH, DA, DA), F32)],
        scratch_shapes=[pltpu.VMEM((SCAN_HB, DA, DA), F32)],
        compiler_params=_cparams(("parallel", "parallel", "arbitrary")),
    )(u, w, a, qd, kd, e)


def _gdn_scan_bwd(u, w, a, qd, kd, e, ssave, do):
    t = u.shape[1]
    tok, per, sq, shared, grid, nb = _scan_specs(t, DA, NA, SCAN_HB, -1)

    def body(u_ref, w_ref, a_ref, qd_ref, kd_ref, e_ref, s_ref, do_ref,
             du_ref, dw_ref, da_ref, dqd_ref, dkd_ref, de_ref, state):
        rev = pl.program_id(0) == 1

        @pl.when(pl.program_id(2) == 0)
        def _():
            state[...] = jnp.zeros_like(state)

        def step(i, dss):
            ci = jnp.where(rev, i, nb - 1 - i)
            r = _rows(ci)
            out = []
            for hh, ds in enumerate(dss):
                c = slice(hh * DA, (hh + 1) * DA)
                _, vjp = jax.vjp(_gdn_scan, u_ref[r, c], w_ref[r, c], a_ref[hh, r, :], qd_ref[r, c], kd_ref[r, c],
                                 e_ref[hh, ci][0:1], s_ref[hh, ci])
                du, dw, da, dqd, dkd, de, dsn = vjp((do_ref[r, c], ds))
                du_ref[r, c] = du
                dw_ref[r, c] = dw
                da_ref[hh, r, :] = da
                dqd_ref[r, c] = dqd
                dkd_ref[r, c] = dkd
                de_ref[hh, ci] = _first_row(de)
                out.append(dsn)
            return tuple(out)

        dss = lax.fori_loop(0, nb, step, tuple(state[hh] for hh in range(SCAN_HB)))
        for hh, ds in enumerate(dss):
            state[hh] = ds

    big = jax.ShapeDtypeStruct((2, t, D), F32)
    return pl.pallas_call(
        body, name="gdn_scan_bwd", grid=grid,
        in_specs=[tok, tok, sq, tok, tok, per(8, 128), per(DA, DA), shared(DA)],
        out_specs=[tok, tok, sq, tok, tok, per(8, 128)],
        out_shape=[big, big, jax.ShapeDtypeStruct((2, NA, t, CH), F32), big, big,
                   jax.ShapeDtypeStruct((2, NA, t // CH, 8, 128), F32)],
        scratch_shapes=[pltpu.VMEM((SCAN_HB, DA, DA), F32)],
        compiler_params=_cparams(("parallel", "parallel", "arbitrary")),
    )(u, w, a, qd, kd, e, ssave, do)


def _gdn_intra_bwd(qn, kn, vc, g2, b2, tinv, du, dw, da, dqd, dkd, de):
    t = qn.shape[0]
    n = t // CH

    def body(q_ref, k_ref, v_ref, g_ref, b_ref, t_ref, du_ref, dw_ref, da_ref, dqd_ref, dkd_ref, de_ref,
             dq_ref, dk_ref, dv_ref, dg_ref, db_ref):
        d = pl.program_id(1)
        cs = _Consts(d == 1)

        @pl.when(d == 0)
        def _():
            dq_ref[...] = jnp.zeros_like(dq_ref)
            dk_ref[...] = jnp.zeros_like(dk_ref)
            dv_ref[...] = jnp.zeros_like(dv_ref)

        def step(i, carry):
            r = _srows(i)
            tinv_c = t_ref[r, :]
            f = lambda q, k, v, g, bx: _gdn_intra(q, k, v, g, bx, tinv_c, cs)
            _, vjp = jax.vjp(f, q_ref[r, :], k_ref[r, :], v_ref[r, :], g_ref[r, :], b_ref[r, :])
            dq, dk, dv, dg, dbx = vjp((du_ref[r, :], dw_ref[r, :], da_ref[r, :], dqd_ref[r, :],
                                       dkd_ref[r, :], _chunk_rows(de_ref, i)))
            dq_ref[r, :] += dq
            dk_ref[r, :] += dk
            dv_ref[r, :] += dv
            dg_ref[r, :] = dg
            db_ref[r, :] = dbx
            return carry

        lax.fori_loop(0, t // SC, step, 0)

    head = pl.BlockSpec((t, DA), lambda h, d: (0, h))
    dh = pl.BlockSpec((None, t, DA), lambda h, d: (d, 0, h))
    sq = pl.BlockSpec((None, None, t, CH), lambda h, d: (d, h, 0, 0))
    tq = pl.BlockSpec((None, None, t, SC), lambda h, d: (d, h, 0, 0))
    full = jax.ShapeDtypeStruct((t, D), F32)
    big = jax.ShapeDtypeStruct((2, t, D), F32)
    return pl.pallas_call(
        body, name="gdn_intra_bwd", grid=(NA, 2),
        in_specs=[head, head, head, dh, dh, tq, dh, dh, sq, dh, dh,
                  pl.BlockSpec((None, None, n, 8, 128), lambda h, d: (d, h, 0, 0, 0))],
        out_specs=[head, head, head, dh, dh],
        out_shape=[full, full, full, big, big],
        compiler_params=_cparams(("arbitrary", "arbitrary")),
    )(qn, kn, vc, g2, b2, tinv, du, dw, da, dqd, dkd, de)


def _gla_specs(t, order):
    ix = (lambda d, h: (d, h)) if order == "dh" else (lambda h, d: (d, h))

    def mk(fn):
        return lambda a, b: fn(*ix(a, b))
    q = pl.BlockSpec((t, DKB), mk(lambda d, h: (0, 32 + h)))
    k = pl.BlockSpec((t, DKB), mk(lambda d, h: (0, 36 + h)))
    v = pl.BlockSpec((t, DVB), mk(lambda d, h: (0, 20 + h)))
    dk = pl.BlockSpec((None, t, DKB), mk(lambda d, h: (d, 0, h)))
    dv = pl.BlockSpec((None, t, DVB), mk(lambda d, h: (d, 0, h)))
    e = pl.BlockSpec((None, None, t // CH, 8, 128), mk(lambda d, h: (d, h, 0, 0, 0)))
    s = pl.BlockSpec((None, None, t // CH, DVB, DKB), mk(lambda d, h: (d, h, 0, 0, 0)))
    return q, k, v, dk, dv, e, s


def _gla_intra_fwd(p, gk):
    t = p.shape[0]
    n = t // CH

    def body(q_ref, k_ref, v_ref, g_ref, qg_ref, kd_ref, in_ref, e_ref):
        cs = _Consts(pl.program_id(0) == 1)

        def step(i, carry):
            r = _srows(i)
            qg, kd, intra, el = _gla_intra(q_ref[r, :], k_ref[r, :], v_ref[r, :], g_ref[r, :], cs)
            qg_ref[r, :] = qg
            kd_ref[r, :] = kd
            in_ref[r, :] = intra
            for c in range(SC // CH):
                e_ref[(SC // CH) * i + c] = el[c * CH:c * CH + 8]
            return carry

        lax.fori_loop(0, t // SC, step, 0)

    q, k, v, dk, dv, e, _ = _gla_specs(t, "dh")
    return pl.pallas_call(
        body, name="gla_intra_fwd", grid=(2, NB),
        in_specs=[q, k, v, dk], out_specs=[dk, dk, dv, e],
        out_shape=[jax.ShapeDtypeStruct((2, t, NB * DKB), F32), jax.ShapeDtypeStruct((2, t, NB * DKB), F32),
                   jax.ShapeDtypeStruct((2, t, D), F32), jax.ShapeDtypeStruct((2, NB, n, 8, 128), F32)],
        compiler_params=_cparams(("parallel", "parallel")),
    )(p, p, p, gk)


GLA_HB = 2


def _gla_v_spec(t, along):
    nt = t // SCAN_TB

    def tmap(d, tt):
        fwd = tt + d * (nt - 1 - 2 * tt)
        return fwd if along > 0 else nt - 1 - fwd

    return pl.BlockSpec((SCAN_TB, GLA_HB * DVB), lambda d, h, tt: (tmap(d, tt), 5120 // (GLA_HB * DVB) + h))


def _gla_scan_fwd(p, qg, kd, intra, e):
    t = p.shape[0]
    tokk, per, _, _, grid, nb = _scan_specs(t, DKB, NB, GLA_HB, +1)
    tokv = _scan_specs(t, DVB, NB, GLA_HB, +1)[0]

    def body(v_ref, qg_ref, kd_ref, in_ref, e_ref, o_ref, s_ref, state):
        rev = pl.program_id(0) == 1

        @pl.when(pl.program_id(2) == 0)
        def _():
            state[...] = jnp.zeros_like(state)

        def step(i, sts):
            ci = jnp.where(rev, nb - 1 - i, i)
            r = _rows(ci)
            out = []
            for hh, st in enumerate(sts):
                ck = slice(hh * DKB, (hh + 1) * DKB)
                cv = slice(hh * DVB, (hh + 1) * DVB)
                s_ref[hh, ci] = st
                o, stn = _gla_scan(qg_ref[r, ck], kd_ref[r, ck], v_ref[r, cv], e_ref[hh, ci][0:1], st)
                o_ref[r, cv] = o + in_ref[r, cv]
                out.append(stn)
            return tuple(out)

        sts = lax.fori_loop(0, nb, step, tuple(state[hh] for hh in range(GLA_HB)))
        for hh, st in enumerate(sts):
            state[hh] = st

    return pl.pallas_call(
        body, name="gla_scan_fwd", grid=grid,
        in_specs=[_gla_v_spec(t, +1), tokk, tokk, tokv, per(8, 128)], out_specs=[tokv, per(DVB, DKB)],
        out_shape=[jax.ShapeDtypeStruct((2, t, D), F32), jax.ShapeDtypeStruct((2, NB, t // CH, DVB, DKB), F32)],
        scratch_shapes=[pltpu.VMEM((GLA_HB, DVB, DKB), F32)],
        compiler_params=_cparams(("parallel", "parallel", "arbitrary")),
    )(p, qg, kd, intra, e)


def _gla_scan_bwd(p, qg, kd, e, ssave, do):
    t = p.shape[0]
    tokk, per, _, shared, grid, nb = _scan_specs(t, DKB, NB, GLA_HB, -1)
    tokv = _scan_specs(t, DVB, NB, GLA_HB, -1)[0]

    def body(v_ref, qg_ref, kd_ref, e_ref, s_ref, do_ref, dqg_ref, dkd_ref, dv_ref, de_ref, state):
        rev = pl.program_id(0) == 1

        @pl.when(pl.program_id(2) == 0)
        def _():
            state[...] = jnp.zeros_like(state)

        def step(i, dsts):
            ci = jnp.where(rev, i, nb - 1 - i)
            r = _rows(ci)
            out = []
            for hh, dst in enumerate(dsts):
                ck = slice(hh * DKB, (hh + 1) * DKB)
                cv = slice(hh * DVB, (hh + 1) * DVB)
                _, vjp = jax.vjp(_gla_scan, qg_ref[r, ck], kd_ref[r, ck], v_ref[r, cv], e_ref[hh, ci][0:1],
                                 s_ref[hh, ci])
                dqg, dkd, dv, de, dstn = vjp((do_ref[r, cv], dst))
                dqg_ref[r, ck] = dqg
                dkd_ref[r, ck] = dkd
                dv_ref[r, cv] = dv
                de_ref[hh, ci] = _first_row(de)
                out.append(dstn)
            return tuple(out)

        dsts = lax.fori_loop(0, nb, step, tuple(state[hh] for hh in range(GLA_HB)))
        for hh, dst in enumerate(dsts):
            state[hh] = dst

    return pl.pallas_call(
        body, name="gla_scan_bwd", grid=grid,
        in_specs=[_gla_v_spec(t, -1), tokk, tokk, per(8, 128), per(DVB, DKB), shared(DVB)],
        out_specs=[tokk, tokk, tokv, per(8, 128)],
        out_shape=[jax.ShapeDtypeStruct((2, t, NB * DKB), F32), jax.ShapeDtypeStruct((2, t, NB * DKB), F32),
                   jax.ShapeDtypeStruct((2, t, D), F32), jax.ShapeDtypeStruct((2, NB, t // CH, 8, 128), F32)],
        scratch_shapes=[pltpu.VMEM((GLA_HB, DVB, DKB), F32)],
        compiler_params=_cparams(("parallel", "parallel", "arbitrary")),
    )(p, qg, kd, e, ssave, do)


def _gla_intra_bwd(p, gk, dqg, dkd, dvs, de, do):
    t = p.shape[0]
    n = t // CH

    def body(q_ref, k_ref, v_ref, g_ref, dqg_ref, dkd_ref, dvs_ref, de_ref, do_ref,
             dq_ref, dk_ref, dv_ref, dg_ref):
        d = pl.program_id(1)
        cs = _Consts(d == 1)

        @pl.when(d == 0)
        def _():
            dq_ref[...] = jnp.zeros_like(dq_ref)
            dk_ref[...] = jnp.zeros_like(dk_ref)
            dv_ref[...] = jnp.zeros_like(dv_ref)

        def step(i, carry):
            r = _srows(i)
            f = lambda q, k, v, g: _gla_intra(q, k, v, g, cs)
            _, vjp = jax.vjp(f, q_ref[r, :], k_ref[r, :], v_ref[r, :], g_ref[r, :])
            dq, dk, dv, dg = vjp((dqg_ref[r, :], dkd_ref[r, :], do_ref[r, :], _chunk_rows(de_ref, i)))
            dq_ref[r, :] += dq
            dk_ref[r, :] += dk
            dv_ref[r, :] += dv + dvs_ref[r, :]
            dg_ref[r, :] = dg
            return carry

        lax.fori_loop(0, t // SC, step, 0)

    q, k, v, dk, dv, e_s, _ = _gla_specs(t, "hd")
    hk = pl.BlockSpec((t, DKB), lambda h, d: (0, h))
    hv = pl.BlockSpec((t, DVB), lambda h, d: (0, h))
    return pl.pallas_call(
        body, name="gla_intra_bwd", grid=(NB, 2),
        in_specs=[q, k, v, dk, dk, dk, dv, e_s, hv],
        out_specs=[hk, hk, hv, dk],
        out_shape=[jax.ShapeDtypeStruct((t, NB * DKB), F32), jax.ShapeDtypeStruct((t, NB * DKB), F32),
                   jax.ShapeDtypeStruct((t, D), F32), jax.ShapeDtypeStruct((2, t, NB * DKB), F32)],
        compiler_params=_cparams(("arbitrary", "arbitrary")),
    )(p, p, p, gk, dqg, dkd, dvs, de, do)


def _seg_gate(o, z, w):
    return _rms(o, w) * _silu(z)


def _seg_merge(ya, yb, ga, gb):
    return _sigmoid(ga) * ya + _sigmoid(gb) * yb


def _seg_loss(out, x, tgt, w):
    err = x + _rms(out, w) - tgt
    return 0.5 * jnp.sum(jnp.mean(err * err, axis=-1, keepdims=True), axis=0, keepdims=True)


def _post(oa2, ob2, p, x, tgt, gdn_w, gla_w, lnpost, w3, tm=128):
    t = x.shape[0]

    def body(oa_ref, ob_ref, z_ref, gb_ref, ga_ref, gB_ref, x_ref, t_ref, aw_ref, bw_ref, lw_ref, w_ref,
             loss_ref, doa_ref, dob_ref, dz_ref, dgb_ref, dga_ref, dgB_ref, dy_ref,
             dw_ref, daw_ref, dbw_ref, dlw_ref):
        first = pl.program_id(0) == 0
        oa = oa_ref[0] + oa_ref[1]
        ob = ob_ref[0] + ob_ref[1]
        z, gb = z_ref[...], gb_ref[...]
        aw, bw = aw_ref[...], bw_ref[...]

        pa = [jax.vjp(_seg_gate, oa[:, h * DA:(h + 1) * DA], z[:, h * DA:(h + 1) * DA], aw) for h in range(NA)]
        pb = [jax.vjp(_seg_gate, ob[:, h * DVB:(h + 1) * DVB], gb[:, h * DVB:(h + 1) * DVB], bw)
              for h in range(NB)]
        a1 = jnp.concatenate([v for v, _ in pa], axis=1).astype(BF16)
        a2 = jnp.concatenate([v for v, _ in pb], axis=1).astype(BF16)
        ya = jnp.dot(a1, w_ref[0], preferred_element_type=F32)
        yb = jnp.dot(a2, w_ref[1], preferred_element_type=F32)
        merged, vjp_m = jax.vjp(_seg_merge, ya, yb, ga_ref[...], gB_ref[...])
        mb = merged.astype(BF16)
        out = jnp.dot(mb, w_ref[2], preferred_element_type=F32)
        loss, vjp_l = jax.vjp(_seg_loss, out, x_ref[...], t_ref[...], lw_ref[...])
        dout, dyres, _, dlw = vjp_l(jnp.ones((1, 1), F32))
        dy_ref[...] = dyres
        doutb = dout.astype(BF16)
        dmerged = _bdot(doutb, w_ref[2], 1, 1)
        dya, dyb, dga, dgB = vjp_m(dmerged)
        dga_ref[...] = dga
        dgB_ref[...] = dgB
        dyab, dybb = dya.astype(BF16), dyb.astype(BF16)
        da1 = _bdot(dyab, w_ref[0], 1, 1)
        da2 = _bdot(dybb, w_ref[1], 1, 1)

        daw = jnp.zeros_like(aw)
        for h in range(NA):
            sl = slice(h * DA, (h + 1) * DA)
            do, dz, dw = pa[h][1](da1[:, sl])
            doa_ref[:, sl] = do
            dz_ref[:, sl] = dz
            daw = daw + dw
        dbw = jnp.zeros_like(bw)
        for h in range(NB):
            sl = slice(h * DVB, (h + 1) * DVB)
            do, dg, dw = pb[h][1](da2[:, sl])
            dob_ref[:, sl] = do
            dgb_ref[:, sl] = dg
            dbw = dbw + dw

        @pl.when(first)
        def _():
            loss_ref[...] = jnp.zeros_like(loss_ref)
            dw_ref[...] = jnp.zeros_like(dw_ref)
            daw_ref[...] = jnp.zeros_like(daw_ref)
            dbw_ref[...] = jnp.zeros_like(dbw_ref)
            dlw_ref[...] = jnp.zeros_like(dlw_ref)

        loss_ref[...] += jnp.broadcast_to(loss, loss_ref.shape)
        dw_ref[0] += _bdot(a1, dyab, 0, 0)
        dw_ref[1] += _bdot(a2, dybb, 0, 0)
        dw_ref[2] += _bdot(mb, doutb, 0, 0)
        daw_ref[...] += jnp.broadcast_to(daw, daw_ref.shape)
        dbw_ref[...] += jnp.broadcast_to(dbw, dbw_ref.shape)
        dlw_ref[...] += jnp.broadcast_to(dlw, dlw_ref.shape)

    two = pl.BlockSpec((2, tm, D), lambda i: (0, i, 0))
    pcol = lambda c: pl.BlockSpec((tm, D), lambda i: (i, c))
    tok = pl.BlockSpec((tm, D), lambda i: (i, 0))
    row = lambda n: pl.BlockSpec((1, n), lambda i: (0, 0))
    row8 = lambda n: pl.BlockSpec((8, n), lambda i: (0, 0))
    once = pl.Buffered(1)
    tokf = jax.ShapeDtypeStruct((t, D), F32)
    return pl.pallas_call(
        body, name="post", grid=(t // tm,),
        in_specs=[two, two, pcol(3), pcol(6), pcol(7), pcol(8), tok, tok, row(DA), row(DVB), row(D),
                  pl.BlockSpec((3, D, D), lambda i: (0, 0, 0), pipeline_mode=once)],
        out_specs=[row8(128), tok, tok, tok, tok, tok, tok, tok,
                   pl.BlockSpec((3, D, D), lambda i: (0, 0, 0), pipeline_mode=once),
                   row8(DA), row8(DVB), row8(D)],
        out_shape=[jax.ShapeDtypeStruct((8, 128), F32), tokf, tokf, tokf, tokf, tokf, tokf, tokf,
                   jax.ShapeDtypeStruct((3, D, D), F32),
                   jax.ShapeDtypeStruct((8, DA), F32), jax.ShapeDtypeStruct((8, DVB), F32),
                   jax.ShapeDtypeStruct((8, D), F32)],
        compiler_params=_cparams(("arbitrary",), vmem_mb=56),
    )(oa2, ob2, p, p, p, p, x, tgt, gdn_w, gla_w, lnpost, w3)


def _adam(w, g, m, v, tr):
    rows, cols = w.shape

    def body(w_ref, g_ref, m_ref, v_ref, d_ref, nm_ref, nv_ref):
        gg = g_ref[...]
        nm = B1 * m_ref[...] + (1.0 - B1) * gg
        nv = B2 * v_ref[...] + (1.0 - B2) * (gg * gg)
        m_hat = nm / (1.0 - B1 ** STEP)
        v_hat = nv / (1.0 - B2 ** STEP)
        d_ref[...] = -LR * (m_hat / (jnp.sqrt(v_hat) + ADAM_EPS) + WD * w_ref[...])
        nm_ref[...] = nm
        nv_ref[...] = nv

    blk = pl.BlockSpec((tr, cols), lambda i: (i, 0))
    shp = jax.ShapeDtypeStruct((rows, cols), F32)
    return pl.pallas_call(
        body, name=f"adam_{rows}x{cols}", grid=(rows // tr,),
        in_specs=[blk] * 4, out_specs=[blk] * 3, out_shape=[shp] * 3,
        compiler_params=_cparams(("parallel",)),
    )(w, g, m, v)


def _sum_cast(own, got):
    _, ns, r, c = own.shape
    tr = r // 4 if r >= 64 else r

    def body(c_ref, a_ref, b_ref, f_ref, h_ref):
        s = a_ref[...] + b_ref[...]
        f_ref[...] = s
        h_ref[...] = s.astype(BF16)

    return pl.pallas_call(
        body, name=f"sum_cast_{r}x{c}",
        grid_spec=pltpu.PrefetchScalarGridSpec(
            num_scalar_prefetch=1, grid=(ns, r // tr),
            in_specs=[pl.BlockSpec((None, None, tr, c), lambda s, i, cc: (cc[0], s, i, 0)),
                      pl.BlockSpec((None, tr, c), lambda s, i, cc: (s, i, 0))],
            out_specs=[pl.BlockSpec((None, tr, c), lambda s, i, cc: (s, i, 0)),
                       pl.BlockSpec((None, tr, c), lambda s, i, cc: (s, i, 0))]),
        out_shape=[jax.ShapeDtypeStruct((ns, r, c), F32), jax.ShapeDtypeStruct((ns, r, c), BF16)],
        compiler_params=_cparams(("parallel", "parallel")),
    )(lax.axis_index("c").reshape(1), own, got)


def _sum4(mine, got):
    _, r, c = mine.shape
    tr = r // 4 if r >= 64 else r

    def body(s_ref, a_ref, g_ref, o_ref):
        acc = a_ref[...] + g_ref[0].astype(F32)
        acc = acc + g_ref[1].astype(F32)
        o_ref[...] = acc + g_ref[2].astype(F32)

    shard = (2 * lax.axis_index("x") + lax.axis_index("y")).reshape(1)
    return pl.pallas_call(
        body, name=f"sum4_{r}x{c}",
        grid_spec=pltpu.PrefetchScalarGridSpec(
            num_scalar_prefetch=1, grid=(r // tr,),
            in_specs=[pl.BlockSpec((None, tr, c), lambda i, ss: (ss[0], i, 0)),
                      pl.BlockSpec((3, tr, c), lambda i, ss: (0, i, 0))],
            out_specs=pl.BlockSpec((tr, c), lambda i, ss: (i, 0))),
        out_shape=jax.ShapeDtypeStruct((r, c), F32),
        compiler_params=_cparams(("parallel",)),
    )(shard, mine, got)


def _place():
    x, y, c = lax.axis_index("x"), lax.axis_index("y"), lax.axis_index("c")
    chips = [(1 - x, y), (x, 1 - y), (1 - x, 1 - y)]
    return x, y, c, chips


def _gather_weights(parts):
    npart = len(parts)

    def body(*refs):
        ins, outs = refs[:npart], refs[npart:2 * npart]
        send_sems, recv_sems = refs[2 * npart:]
        x, y, c, chips = _place()
        sibling = (x, y, 1 - c)
        mine = 2 * x + y

        def remote(k, p, shard, half, to, src=None):
            dst = outs[p].at[shard, half]
            return pltpu.make_async_remote_copy(
                src_ref=dst if src is None else src, dst_ref=dst,
                send_sem=send_sems.at[k], recv_sem=recv_sems.at[k], device_id=to, device_id_type=MESH)

        first = [remote(j * npart + p, p, mine, c, (*chip, c), src=ins[p].at[c])
                 for j, chip in enumerate(chips) for p in range(npart)]
        for cp in first:
            cp.start()
        passed = []
        for j, (cx, cy) in enumerate(chips):
            for p in range(npart):
                remote(j * npart + p, p, 2 * cx + cy, c, (x, y, c)).wait_recv()
                fw = remote((3 + j) * npart + p, p, 2 * cx + cy, c, sibling)
                fw.start()
                passed.append(fw)
        for j, (cx, cy) in enumerate(chips):
            for p in range(npart):
                remote((3 + j) * npart + p, p, 2 * cx + cy, 1 - c, (x, y, c)).wait_recv()
        for cp in first + passed:
            cp.wait_send()

    got = pl.pallas_call(
        body, name="gather_weights",
        in_specs=[ANY] * npart, out_specs=[ANY] * npart,
        out_shape=[jax.ShapeDtypeStruct((NSHARD,) + a.shape, a.dtype) for a in parts],
        scratch_shapes=[pltpu.SemaphoreType.DMA((6 * npart,)), pltpu.SemaphoreType.DMA((6 * npart,))],
    )(*parts)
    mine = 2 * lax.axis_index("x") + lax.axis_index("y")
    return [lax.dynamic_update_index_in_dim(g, a, mine, 0) for g, a in zip(got, parts)]


def _swap_halves(parts):
    npart = len(parts)

    def body(*refs):
        ins, outs = refs[:npart], refs[npart:2 * npart]
        send_sems, recv_sems = refs[2 * npart:]
        x, y, c, _ = _place()
        cps = [pltpu.make_async_remote_copy(
            src_ref=ins[p].at[1 - c], dst_ref=outs[p], send_sem=send_sems.at[p], recv_sem=recv_sems.at[p],
            device_id=(x, y, 1 - c), device_id_type=MESH) for p in range(npart)]
        for cp in cps:
            cp.start()
        for cp in cps:
            cp.wait()

    return pl.pallas_call(
        body, name="swap_halves", in_specs=[ANY] * npart, out_specs=[ANY] * npart,
        out_shape=[jax.ShapeDtypeStruct(a.shape[1:], a.dtype) for a in parts],
        scratch_shapes=[pltpu.SemaphoreType.DMA((npart,)), pltpu.SemaphoreType.DMA((npart,))],
    )(*parts)


def _scatter_shards(parts):
    npart = len(parts)

    def body(*refs):
        ins, outs = refs[:npart], refs[npart:2 * npart]
        send_sems, recv_sems = refs[2 * npart:]
        x, y, c, chips = _place()
        cps = [pltpu.make_async_remote_copy(
            src_ref=ins[p].at[2 * cx + cy], dst_ref=outs[p].at[j],
            send_sem=send_sems.at[j * npart + p], recv_sem=recv_sems.at[j * npart + p],
            device_id=(cx, cy, c), device_id_type=MESH)
            for j, (cx, cy) in enumerate(chips) for p in range(npart)]
        for cp in cps:
            cp.start()
        for cp in cps:
            cp.wait()

    return pl.pallas_call(
        body, name="scatter_shards", in_specs=[ANY] * npart, out_specs=[ANY] * npart,
        out_shape=[jax.ShapeDtypeStruct((3,) + a.shape[1:], a.dtype) for a in parts],
        scratch_shapes=[pltpu.SemaphoreType.DMA((3 * npart,)), pltpu.SemaphoreType.DMA((3 * npart,))],
    )(*parts)


def _join_halves(parts):
    npart = len(parts)

    def body(*refs):
        ins, outs = refs[:npart], refs[npart:2 * npart]
        send_sems, recv_sems = refs[2 * npart:]
        x, y, c, _ = _place()
        cps = [pltpu.make_async_remote_copy(
            src_ref=ins[p], dst_ref=outs[p], send_sem=send_sems.at[p], recv_sem=recv_sems.at[p],
            device_id=(x, y, 1 - c), device_id_type=MESH) for p in range(npart)]
        for cp in cps:
            cp.start()
        for cp in cps:
            cp.wait()

    got = pl.pallas_call(
        body, name="join_halves", in_specs=[ANY] * npart, out_specs=[ANY] * npart,
        out_shape=[jax.ShapeDtypeStruct(a.shape, a.dtype) for a in parts],
        scratch_shapes=[pltpu.SemaphoreType.DMA((npart,)), pltpu.SemaphoreType.DMA((npart,))],
    )(*parts)
    south = lax.axis_index("c") == 0
    return [jnp.where(south, jnp.stack([a, g]), jnp.stack([g, a])) for a, g in zip(parts, got)]


def _allreduce_small(v):
    r = v.shape[0]

    def body(v_ref, o_ref, buf, send_sems, recv_sems):
        x, y, c, _ = _place()
        me = 4 * x + 2 * y + c
        buf[me] = v_ref[...]
        cps = []
        for k in range(1, 8):
            px, py, pc = x ^ (k >> 2), y ^ ((k >> 1) & 1), c ^ (k & 1)
            cps.append(pltpu.make_async_remote_copy(
                src_ref=v_ref, dst_ref=buf.at[me], send_sem=send_sems.at[k - 1], recv_sem=recv_sems.at[k - 1],
                device_id=(px, py, pc), device_id_type=MESH))
        for cp in cps:
            cp.start()
        for k in range(1, 8):
            px, py, pc = x ^ (k >> 2), y ^ ((k >> 1) & 1), c ^ (k & 1)
            pltpu.make_async_remote_copy(
                src_ref=v_ref, dst_ref=buf.at[4 * px + 2 * py + pc], send_sem=send_sems.at[k - 1],
                recv_sem=recv_sems.at[k - 1], device_id=(px, py, pc), device_id_type=MESH).wait_recv()
        for cp in cps:
            cp.wait_send()
        acc = buf[0]
        for d in range(1, 8):
            acc = acc + buf[d]
        o_ref[...] = acc

    return pl.pallas_call(
        body, name="allreduce_small",
        in_specs=[pl.BlockSpec(memory_space=pltpu.VMEM)], out_specs=pl.BlockSpec(memory_space=pltpu.VMEM),
        out_shape=jax.ShapeDtypeStruct((r, 128), F32),
        scratch_shapes=[pltpu.VMEM((8, r, 128), F32), pltpu.SemaphoreType.DMA((7,)), pltpu.SemaphoreType.DMA((7,))],
    )(v)


def _permute_cols(w):
    zeros = jnp.zeros((w.shape[0], NPERM - 9280), w.dtype)
    return jnp.concatenate([w[:, 0:4096], w[:, 4128:7200], w[:, 7232:9280], w[:, 4096:4128], w[:, 7200:7232], zeros],
                           axis=1)


def _unpermute_cols(g):
    return jnp.concatenate([g[:, 0:4096], g[:, 9216:9248], g[:, 4096:7168], g[:, 9248:9280], g[:, 7168:9216]],
                           axis=1)


SMALL_NAMES = ("ln_pre_w", "a_log_fwd", "a_log_bwd", "dt_bias_fwd", "dt_bias_bwd", "gdn_norm_w",
               "gk_b2_fwd", "gk_b2_bwd", "gla_norm_w", "ln_post_w")
SMALL_SIZES = (1024, 8, 8, 8, 8, 128, 512, 512, 256, 1024)
SMALL_ROWS = 32


def _pack_small(vals):
    flat = jnp.concatenate([v.reshape(-1) for v in vals])
    return jnp.pad(flat, (0, SMALL_ROWS * 128 - flat.shape[0])).reshape(SMALL_ROWS, 128)


def _unpack_small(packed):
    flat = packed.reshape(-1)
    out, o = [], 0
    for n in SMALL_SIZES:
        out.append(flat[o:o + n].reshape(1, n))
        o += n
    return out


def _pack_shard_small(conv, w2f, w2b):
    top = jnp.pad(conv, ((0, 8 - conv.shape[0]), (0, 0)))
    mid = jnp.pad(jnp.concatenate([w2f, w2b], axis=1), ((0, 0), (0, 768 - 256)))
    return jnp.concatenate([top, mid, jnp.zeros((8, 768), conv.dtype)], axis=0)


def _unpack_shard_small(a):
    return a[0:5], a[8:24, 0:128], a[8:24, 128:256]


def kernel(x, ln_pre_w, w_in, conv_w, a_log_fwd, a_log_bwd, dt_bias_fwd, dt_bias_bwd, gdn_norm_w, w_proj_gdn, gk_w2_fwd, gk_b2_fwd, gk_w2_bwd, gk_b2_bwd, gla_norm_w, w_proj_gla, w_out, ln_post_w, loss_target, m_ln_pre_w, m_w_in, m_conv_w, m_a_log_fwd, m_a_log_bwd, m_dt_bias_fwd, m_dt_bias_bwd, m_gdn_norm_w, m_w_proj_gdn, m_gk_w2_fwd, m_gk_b2_fwd, m_gk_w2_bwd, m_gk_b2_bwd, m_gla_norm_w, m_w_proj_gla, m_w_out, m_ln_post_w, v_ln_pre_w, v_w_in, v_conv_w, v_a_log_fwd, v_a_log_bwd, v_dt_bias_fwd, v_dt_bias_bwd, v_gdn_norm_w, v_w_proj_gdn, v_gk_w2_fwd, v_gk_b2_fwd, v_gk_w2_bwd, v_gk_b2_bwd, v_gla_norm_w, v_w_proj_gla, v_w_out, v_ln_post_w):
    t = x.shape[1]
    x2, tgt = x[0], loss_target[0]

    win_l = w_in[0].astype(BF16).reshape(2, D // 2, SHW)
    proj_l = jnp.concatenate([w_proj_gdn[0], w_proj_gla[0], w_out[0]], axis=0).astype(BF16).reshape(2, 384, D)
    small_l = _pack_shard_small(conv_w[0], gk_w2_fwd[0], gk_w2_bwd[0]).reshape(2, 16, 768)
    win_g, proj_g, small_g = _gather_weights([win_l, proj_l, small_l])
    w_full = win_g.reshape(NSHARD, D, SHW).transpose(1, 0, 2).reshape(D, NSHARD * SHW)
    wperm = _permute_cols(w_full)
    w3 = proj_g.reshape(NSHARD, 3, D // NSHARD, D).transpose(1, 0, 2, 3).reshape(3, D, D)
    small_g = small_g.reshape(NSHARD, 32, 768)
    convw = small_g[:, 0:8, :].transpose(1, 0, 2).reshape(8, 3 * D)
    w2f = small_g[:, 8:24, 0:128].transpose(1, 0, 2).reshape(16, 512)
    w2b = small_g[:, 8:24, 128:256].transpose(1, 0, 2).reshape(16, 512)
    w2f_pad = jnp.pad(w2f, ((32, 80), (0, 0)))
    w2b_pad = jnp.pad(w2b, ((48, 64), (0, 0)))
    alog_row = jnp.pad(jnp.concatenate([a_log_fwd, a_log_bwd], axis=1), ((0, 0), (0, 112)))
    dt_row = jnp.pad(jnp.concatenate([dt_bias_fwd, dt_bias_bwd], axis=1), ((0, 0), (0, 112)))

    p, h = _inproj(x2, ln_pre_w, wperm)
    qn, kn, vc = (_qkv_fwd(p, convw, kind) for kind in range(3))
    gsm, gk = _gates_fwd(p, alog_row, dt_row, w2f_pad, gk_b2_fwd, w2b_pad, gk_b2_bwd)
    g2, b2 = _gcum_fwd(gsm)
    u, w, at, qd, kd, el, tinv = _gdn_intra_fwd(qn, kn, vc, g2, b2)
    oa2, sa = _gdn_scan_fwd(u, w, at, qd, kd, el)
    qg, kdb, intra, elb = _gla_intra_fwd(p, gk)
    ob2, sb = _gla_scan_fwd(p, qg, kdb, intra, elb)

    (loss8, doa, dob, dz, dgb, dga, dgB, dyres, dw3, dgdn_w, dgla_w, dlnpost) = _post(
        oa2, ob2, p, x2, tgt, gdn_norm_w, gla_norm_w, ln_post_w, w3)

    du, dw, dat, dqd, dkd, del_ = _gdn_scan_bwd(u, w, at, qd, kd, el, sa, doa)
    dqn, dkn, dvc, dg2, db2 = _gdn_intra_bwd(qn, kn, vc, g2, b2, tinv, du, dw, dat, dqd, dkd, del_)
    dgsm = _gcum_bwd(gsm, dg2, db2)
    dqg, dkdb, dvs, delb = _gla_scan_bwd(p, qg, kdb, elb, sb, dob)
    dqb, dkb, dvb, dgk = _gla_intra_bwd(p, gk, dqg, dkdb, dvs, delb, dob)
    (dps, dalog8, ddt8, dw2f_pad, db2f8, dw2b_pad, db2b8) = _gates_bwd(
        p, alog_row, dt_row, w2f_pad, gk_b2_fwd, w2b_pad, gk_b2_bwd, dgsm, dgk)
    dpre, dconv = zip(*[_qkv_bwd(p, convw, g, kind) for kind, g in enumerate((dqn, dkn, dvc))])

    dp = jnp.concatenate([a.astype(BF16) for a in (*dpre, dz, dqb, dkb, dvb, dgb, dga, dgB, dps)]
                         + [jnp.zeros((t, NPERM - 9344), BF16)], axis=1)
    dwperm = _inproj_dw(h, dp)
    dx, dlnpre8 = _inproj_dx(dp, wperm, x2, ln_pre_w, dyres)

    dw_in_full = _unpermute_cols(dwperm)
    g_in = dw_in_full.reshape(2, D // 2, NSHARD, SHW).transpose(0, 2, 1, 3)
    g_proj = dw3.reshape(3, NSHARD, D // NSHARD, D).transpose(1, 0, 2, 3).reshape(NSHARD, 2, 384, D)
    g_proj = g_proj.transpose(1, 0, 2, 3)
    dconv_full = jnp.concatenate(dconv, axis=1)
    dw2f, dw2b = dw2f_pad[32:48], dw2b_pad[48:64]
    g_small = jnp.stack([_pack_shard_small(dconv_full[0:5, 768 * s:768 * (s + 1)],
                                           dw2f[:, 128 * s:128 * (s + 1)], dw2b[:, 128 * s:128 * (s + 1)])
                         for s in range(NSHARD)])
    g_small = g_small.reshape(NSHARD, 2, 16, 768).transpose(1, 0, 2, 3)
    parts = [g_in, g_proj, g_small]
    got = _swap_halves(parts)
    sums = [_sum_cast(a, b) for a, b in zip(parts, got)]
    landed = _scatter_shards([hb for _, hb in sums])
    halves = [_sum4(f, g) for (f, _), g in zip(sums, landed)]
    r_in, r_proj, r_small = _join_halves(halves)
    grad_w_in = r_in.reshape(D, SHW)
    r_proj = r_proj.reshape(768, D)
    r_small = r_small.reshape(32, 768)

    small_grads = (dlnpre8[0:1], dalog8[0:1, 0:8], dalog8[0:1, 8:16], ddt8[0:1, 0:8], ddt8[0:1, 8:16],
                   dgdn_w[0:1], db2f8[0:1], db2b8[0:1], dgla_w[0:1], dlnpost[0:1])
    gsmall = _allreduce_small(_pack_small(small_grads))

    d_in, nm_in, nv_in = _adam(w_in[0], grad_w_in, m_w_in[0], v_w_in[0], 128)
    stack3 = lambda a, b, c: jnp.concatenate([a[0], b[0], c[0]], axis=0)
    d_pr, nm_pr, nv_pr = _adam(stack3(w_proj_gdn, w_proj_gla, w_out), r_proj,
                               stack3(m_w_proj_gdn, m_w_proj_gla, m_w_out),
                               stack3(v_w_proj_gdn, v_w_proj_gla, v_w_out), 256)
    d_ss, nm_ss, nv_ss = _adam(_pack_shard_small(conv_w[0], gk_w2_fwd[0], gk_w2_bwd[0]), r_small,
                               _pack_shard_small(m_conv_w[0], m_gk_w2_fwd[0], m_gk_w2_bwd[0]),
                               _pack_shard_small(v_conv_w[0], v_gk_w2_fwd[0], v_gk_w2_bwd[0]), 32)
    smalls = dict(ln_pre_w=(ln_pre_w, m_ln_pre_w, v_ln_pre_w), a_log_fwd=(a_log_fwd, m_a_log_fwd, v_a_log_fwd),
                  a_log_bwd=(a_log_bwd, m_a_log_bwd, v_a_log_bwd),
                  dt_bias_fwd=(dt_bias_fwd, m_dt_bias_fwd, v_dt_bias_fwd),
                  dt_bias_bwd=(dt_bias_bwd, m_dt_bias_bwd, v_dt_bias_bwd),
                  gdn_norm_w=(gdn_norm_w, m_gdn_norm_w, v_gdn_norm_w),
                  gk_b2_fwd=(gk_b2_fwd, m_gk_b2_fwd, v_gk_b2_fwd), gk_b2_bwd=(gk_b2_bwd, m_gk_b2_bwd, v_gk_b2_bwd),
                  gla_norm_w=(gla_norm_w, m_gla_norm_w, v_gla_norm_w), ln_post_w=(ln_post_w, m_ln_post_w, v_ln_post_w))
    ws, ms, vs = (_pack_small([smalls[n][i] for n in SMALL_NAMES]) for i in range(3))
    d_sm, nm_sm, nv_sm = _adam(ws, gsmall, ms, vs, SMALL_ROWS)

    def family(in_, pr, ss, sm):
        conv, w2f_, w2b_ = _unpack_shard_small(ss)
        s = dict(zip(SMALL_NAMES, _unpack_small(sm)))
        return [s["ln_pre_w"], in_[None], conv[None], s["a_log_fwd"], s["a_log_bwd"], s["dt_bias_fwd"],
                s["dt_bias_bwd"], s["gdn_norm_w"], pr[None, 0:256], w2f_[None], s["gk_b2_fwd"], w2b_[None],
                s["gk_b2_bwd"], s["gla_norm_w"], pr[None, 256:512], pr[None, 512:768], s["ln_post_w"]]

    loss = lax.psum(loss8[0, 0], ("x", "y", "c"))
    outs = [loss, dx[None]]
    outs += family(grad_w_in, r_proj, r_small, gsmall)
    outs += family(d_in, d_pr, d_ss, d_sm)
    outs += family(nm_in, nm_pr, nm_ss, nm_sm)
    outs += family(nv_in, nv_pr, nv_ss, nv_sm)
    return tuple(outs)
```

```python
import functools

import jax
import jax.numpy as jnp
from jax import lax
from jax.experimental import pallas as pl
from jax.experimental.pallas import tpu as pltpu

F32 = jnp.float32
BF16 = jnp.bfloat16
HI = lax.Precision.HIGHEST
MESH = pl.DeviceIdType.MESH

D = 1024
CH = 64
EPS = 1e-6
NA, DA = 8, 128
NB, DKB, DVB = 4, 128, 256
NSHARD = 4
SHW = 2320
NPERM = 9728
PS_BLOCK = 72
LR, B1, B2, ADAM_EPS, WD, STEP = 0.001, 0.9, 0.999, 1e-08, 0.01, 10

ANY = pl.BlockSpec(memory_space=pl.ANY)


def _cparams(sem=None, vmem_mb=48):
    return pltpu.CompilerParams(dimension_semantics=sem, vmem_limit_bytes=vmem_mb << 20)


def _bdot(a, b, ca, cb):
    return lax.dot_general(a.astype(BF16), b.astype(BF16), (((ca,), (cb,)), ((), ())),
                           preferred_element_type=F32)


@jax.custom_vjp
def mm(a, b):
    return _bdot(a, b, 1, 0)


def _mm_fwd(a, b):
    return _bdot(a, b, 1, 0), (a, b)


def _mm_bwd(res, g):
    a, b = res
    return _bdot(g, b, 1, 1), _bdot(a, g, 0, 0)


mm.defvjp(_mm_fwd, _mm_bwd)


@jax.custom_vjp
def mm_nt(a, b):
    return _bdot(a, b, 1, 1)


def _mm_nt_fwd(a, b):
    return _bdot(a, b, 1, 1), (a, b)


def _mm_nt_bwd(res, g):
    a, b = res
    return _bdot(g, b, 1, 0), _bdot(g, a, 0, 0)


mm_nt.defvjp(_mm_nt_fwd, _mm_nt_bwd)


@jax.custom_vjp
def mm_tn(a, b):
    return _bdot(a, b, 0, 0)


def _mm_tn_fwd(a, b):
    return _bdot(a, b, 0, 0), (a, b)


def _mm_tn_bwd(res, g):
    a, b = res
    return _bdot(b, g, 1, 1), _bdot(a, g, 1, 0)


mm_tn.defvjp(_mm_tn_fwd, _mm_tn_bwd)


def dot_hi(a, b):
    return lax.dot_general(a, b, (((1,), (0,)), ((), ())), precision=HI, preferred_element_type=F32)


def _split3(x):
    x1 = x.astype(BF16)
    r = x - x1.astype(F32)
    x2 = r.astype(BF16)
    return x1, x2, (r - x2.astype(F32)).astype(BF16)


def _cdot(c, x, cc, cx, c_first=True):
    parts = _split3(x)
    if c_first:
        return _bdot(c, parts[0], cc, cx) + _bdot(c, parts[1], cc, cx) + _bdot(c, parts[2], cc, cx)
    return _bdot(parts[0], c, cx, cc) + _bdot(parts[1], c, cx, cc) + _bdot(parts[2], c, cx, cc)


@jax.custom_vjp
def cmm(c, x):
    return _cdot(c, x, 1, 0)


def _cmm_fwd(c, x):
    return _cdot(c, x, 1, 0), c


def _cmm_bwd(c, g):
    return jnp.zeros_like(c), _cdot(c, g, 0, 0)


cmm.defvjp(_cmm_fwd, _cmm_bwd)


@jax.custom_vjp
def mmc(x, c):
    return _cdot(c, x, 0, 1, c_first=False)


def _mmc_fwd(x, c):
    return _cdot(c, x, 0, 1, c_first=False), c


def _mmc_bwd(c, g):
    return _cdot(c, g, 1, 1, c_first=False), jnp.zeros_like(c)


mmc.defvjp(_mmc_fwd, _mmc_bwd)


def _sigmoid(x):
    return 1.0 / (1.0 + jnp.exp(-x))


def _silu(x):
    return x * _sigmoid(x)


def _softplus(x):
    return jnp.maximum(x, 0.0) + jnp.log(1.0 + jnp.exp(-jnp.abs(x)))


def _rms(x, w):
    return x * lax.rsqrt(jnp.mean(x * x, axis=-1, keepdims=True) + EPS) * w


SC = 256


class _Consts:
    def __init__(self, rev):
        r = lax.broadcasted_iota(jnp.int32, (SC, SC), 0)
        c = lax.broadcasted_iota(jnp.int32, (SC, SC), 1)
        same = (r >> 6) == (c >> 6)
        a = jnp.where(rev, c, r)
        b = jnp.where(rev, r, c)
        self.incl = same & (a >= b)
        self.strict = same & (a > b)
        self.incl_f = self.incl.astype(F32)
        self.eye = (r == c).astype(F32)
        rows = lax.broadcasted_iota(jnp.int32, (SC, 1), 0)
        self.last_col = ((rows & (CH - 1)) == jnp.where(rev, 0, CH - 1)).astype(F32)
        rr = lax.broadcasted_iota(jnp.int32, (SC, CH), 0)
        cc = lax.broadcasted_iota(jnp.int32, (SC, CH), 1)
        self.fold = ((rr & (CH - 1)) == cc).astype(F32)


def _dot3(a, b, ca=1, cb=0):
    ah, bh = a.astype(BF16), b.astype(BF16)
    al, bl = (a - ah.astype(F32)).astype(BF16), (b - bh.astype(F32)).astype(BF16)
    return _bdot(ah, bh, ca, cb) + (_bdot(ah, bl, ca, cb) + _bdot(al, bh, ca, cb))


TRI_SPLIT_LEVELS = 2


def _tri_inv(low, eye):
    n = -low
    acc = eye + n
    p = n
    for level in range(5):
        dot = _dot3 if level < TRI_SPLIT_LEVELS else (lambda a, b: _bdot(a, b, 1, 0))
        p = dot(p, p)
        acc = acc + dot(acc, p)
    return acc


@jax.custom_vjp
def _solve2(low, rv, rk, tinv):
    x = _dot3(tinv, jnp.concatenate([rv, rk], axis=1))
    return x[:, :DA], x[:, DA:]


def _solve2_fwd(low, rv, rk, tinv):
    x = _dot3(tinv, jnp.concatenate([rv, rk], axis=1))
    return (x[:, :DA], x[:, DA:]), (x, tinv)


def _solve2_bwd(res, g):
    x, tinv = res
    drhs = _dot3(tinv, jnp.concatenate(g, axis=1), 0, 0)
    return -_dot3(drhs, x, 1, 1), drhs[:, :DA], drhs[:, DA:], jnp.zeros_like(tinv)


_solve2.defvjp(_solve2_fwd, _solve2_bwd)


def _chunk_last(x, cs):
    xs = (x * cs.last_col).reshape(SC // CH, CH, x.shape[1])
    return jnp.broadcast_to(jnp.sum(xs, axis=1, keepdims=True), xs.shape).reshape(x.shape)


def _gdn_decay(g, cs):
    gw = jnp.concatenate([g] * (SC // DA), axis=1)
    grow = jnp.sum(cs.eye * gw, axis=0, keepdims=True)
    return jnp.where(cs.incl, jnp.exp(jnp.where(cs.incl, gw - grow, 0.0)), 0.0)


def _gdn_intra(q, k, v, g, bx, tinv, cs):
    decay = _gdn_decay(g, cs)
    kb = k * bx
    low = jnp.where(cs.strict, mm_nt(kb, k) * decay, 0.0)
    eg = jnp.exp(g)
    made = tinv is None
    if made:
        tinv = _tri_inv(low, cs.eye)
    u, w = _solve2(low, v * bx, kb * eg, tinv)
    attn = mmc(mm_nt(q, k) * decay, cs.fold)
    qd = q * eg
    glast = _chunk_last(g, cs)
    kd = k * jnp.exp(glast - g)
    outs = (u, w, attn, qd, kd, jnp.exp(glast))
    return outs + (tinv,) if made else outs


def _gdn_scan(u, w, attn, qd, kd, el, s):
    vn = u - mm(w, s)
    o = mm(qd, s) + mm(attn, vn)
    sn = s * el + mm_tn(kd, vn)
    return o, sn


def _gla_intra(q, k, v, gk, cs):
    gc = cmm(cs.incl_f, gk)
    qg = q * (DKB ** -0.5) * jnp.exp(gc)
    kg = k * jnp.exp(-gc)
    attn = jnp.where(cs.incl, mm_nt(qg, kg), 0.0)
    intra = mm(attn, v)
    glast = _chunk_last(gc, cs)
    kd = k * jnp.exp(glast - gc)
    return qg, kd, intra, jnp.exp(glast)


def _gla_scan(qg, kd, v, el, st):
    o = mm_nt(qg, st)
    stn = st * el + mm_tn(v, kd)
    return o, stn


def _shift_rows(x, s):
    if s == 0:
        return x
    t = x.shape[0]
    rolled = pltpu.roll(x, (-s) % t, 0)
    rows = lax.broadcasted_iota(jnp.int32, x.shape, 0)
    return jnp.where((rows + s >= 0) & (rows + s < t), rolled, 0.0)


@jax.custom_vjp
def _conv5(x, w):
    acc = w[0:1] * _shift_rows(x, -2)
    for j in range(1, 5):
        acc = acc + w[j:j + 1] * _shift_rows(x, j - 2)
    return acc


def _conv5_fwd(x, w):
    return _conv5(x, w), (x, w)


def _conv5_bwd(res, g):
    x, w = res
    dx = w[0:1] * _shift_rows(g, 2)
    for j in range(1, 5):
        dx = dx + w[j:j + 1] * _shift_rows(g, 2 - j)
    rows = lax.broadcasted_iota(jnp.int32, w.shape, 0)
    dw = jnp.zeros_like(w)
    for j in range(5):
        dwj = jnp.sum(g * _shift_rows(x, j - 2), axis=0, keepdims=True)
        dw = dw + jnp.where(rows == j, dwj, 0.0)
    return dx, dw


_conv5.defvjp(_conv5_fwd, _conv5_bwd)


def _qkv_act(kind):
    def f(x, w):
        c = _silu(_conv5(x, w))
        if kind == 2:
            return c
        c = c * lax.rsqrt(jnp.sum(c * c, axis=-1, keepdims=True) + EPS)
        return c * (DA ** -0.5) if kind == 0 else c
    return f


def _inproj(x, lnw, wperm, tm=512, tn=512):
    t = x.shape[0]

    def body(x_ref, lnw_ref, w_ref, p_ref, h_ref, hbuf):
        @pl.when(pl.program_id(1) == 0)
        def _():
            hb = _rms(x_ref[...], lnw_ref[...]).astype(BF16)
            hbuf[...] = hb
            h_ref[...] = hb
        p_ref[...] = jnp.dot(hbuf[...], w_ref[...], preferred_element_type=F32)

    return pl.pallas_call(
        body, name="inproj", grid=(t // tm, NPERM // tn),
        in_specs=[pl.BlockSpec((tm, D), lambda i, j: (i, 0)),
                  pl.BlockSpec((1, D), lambda i, j: (0, 0)),
                  pl.BlockSpec((D, tn), lambda i, j: (0, j))],
        out_specs=[pl.BlockSpec((tm, tn), lambda i, j: (i, j)),
                   pl.BlockSpec((tm, D), lambda i, j: (i, 0))],
        out_shape=[jax.ShapeDtypeStruct((t, NPERM), F32),
                   jax.ShapeDtypeStruct((t, D), BF16)],
        scratch_shapes=[pltpu.VMEM((tm, D), BF16)],
        compiler_params=_cparams(("parallel", "arbitrary")),
    )(x, lnw, wperm)


DP_TILE = 512
DP_PIECES = ((0, 2), (2, 2), (4, 2), (6, 2), (8, 1), (9, 1), (10, 2), (12, 2), (14, 2), (16, 2), (18, 1))


def _piece_specs(tm, j_first):
    specs = []
    for j0, n in DP_PIECES:
        def imap(a, b, j0=j0, n=n):
            j, i = (a, b) if j_first else (b, a)
            inside = (j >= j0) & (j < j0 + n)
            return jnp.where(inside, i, 0), jnp.clip(j - j0, 0, n - 1)
        specs.append(pl.BlockSpec((tm, DP_TILE), imap))
    return specs


def _for_piece(j, refs, fn):
    for (j0, n), ref in zip(DP_PIECES, refs):
        @pl.when((j >= j0) & (j < j0 + n))
        def _(ref=ref):
            fn(ref[...])


def _inproj_dw(h, pieces, tm=512):
    t = h.shape[0]
    npc = len(pieces)

    def body(h_ref, *refs):
        dw_ref = refs[npc]

        @pl.when(pl.program_id(1) == 0)
        def _():
            dw_ref[...] = jnp.zeros_like(dw_ref)

        def add(dp):
            dw_ref[...] += _bdot(h_ref[...], dp, 0, 0)
        _for_piece(pl.program_id(0), refs[:npc], add)

    return pl.pallas_call(
        body, name="inproj_dw", grid=(NPERM // DP_TILE, t // tm),
        in_specs=[pl.BlockSpec((tm, D), lambda j, i: (i, 0))] + _piece_specs(tm, True),
        out_specs=pl.BlockSpec((D, DP_TILE), lambda j, i: (0, j)),
        out_shape=jax.ShapeDtypeStruct((D, NPERM), F32),
        compiler_params=_cparams(("parallel", "arbitrary")),
    )(h, *pieces)


def _inproj_dx(pieces, wperm, x, lnw, dyres, tm=512):
    t = x.shape[0]
    tn = DP_TILE
    nj = NPERM // tn
    npc = len(pieces)

    def body(*refs):
        w_ref, x_ref, lnw_ref, dy_ref, dx_ref, dlnw_ref, acc = refs[npc:]
        j = pl.program_id(1)

        @pl.when(j == 0)
        def _():
            acc[...] = jnp.zeros_like(acc)

        def add(dp):
            acc[...] += _bdot(dp, w_ref[...], 1, 1)
        _for_piece(j, refs[:npc], add)

        @pl.when(j == nj - 1)
        def _():
            _, vjp = jax.vjp(_rms, x_ref[...], lnw_ref[...])
            dx, dlnw = vjp(acc[...])
            dx_ref[...] = dx + dy_ref[...]

            @pl.when(pl.program_id(0) == 0)
            def _():
                dlnw_ref[...] = jnp.zeros_like(dlnw_ref)
            dlnw_ref[...] += jnp.broadcast_to(dlnw, dlnw_ref.shape)

    return pl.pallas_call(
        body, name="inproj_dx", grid=(t // tm, nj),
        in_specs=_piece_specs(tm, False) + [
                  pl.BlockSpec((D, tn), lambda i, j: (0, j)),
                  pl.BlockSpec((tm, D), lambda i, j: (i, 0)),
                  pl.BlockSpec((1, D), lambda i, j: (0, 0)),
                  pl.BlockSpec((tm, D), lambda i, j: (i, 0))],
        out_specs=[pl.BlockSpec((tm, D), lambda i, j: (i, 0)),
                   pl.BlockSpec((8, D), lambda i, j: (0, 0))],
        out_shape=[jax.ShapeDtypeStruct((t, D), F32), jax.ShapeDtypeStruct((8, D), F32)],
        scratch_shapes=[pltpu.VMEM((tm, D), F32)],
        compiler_params=_cparams(("arbitrary", "arbitrary")),
    )(*pieces, wperm, x, lnw, dyres)


def _qkv_fwd(p, convw, kind):
    t = p.shape[0]
    f = _qkv_act(kind)

    def body(p_ref, w_ref, o_ref):
        o_ref[...] = f(p_ref[...], w_ref[...])

    return pl.pallas_call(
        body, name=f"qkv_fwd{kind}", grid=(NA,),
        in_specs=[pl.BlockSpec((t, DA), lambda h: (0, kind * NA + h)),
                  pl.BlockSpec((8, DA), lambda h: (0, kind * NA + h))],
        out_specs=pl.BlockSpec((t, DA), lambda h: (0, h)),
        out_shape=jax.ShapeDtypeStruct((t, D), F32),
        compiler_params=_cparams(("parallel",)),
    )(p, convw)


def _qkv_bwd(p, convw, dout, kind):
    t = p.shape[0]
    f = _qkv_act(kind)

    def body(p_ref, w_ref, g_ref, dx_ref, dw_ref):
        _, vjp = jax.vjp(f, p_ref[...], w_ref[...])
        dx, dw = vjp(g_ref[...])
        dx_ref[...] = dx
        dw_ref[...] = dw

    return pl.pallas_call(
        body, name=f"qkv_bwd{kind}", grid=(NA,),
        in_specs=[pl.BlockSpec((t, DA), lambda h: (0, kind * NA + h)),
                  pl.BlockSpec((8, DA), lambda h: (0, kind * NA + h)),
                  pl.BlockSpec((t, DA), lambda h: (0, h))],
        out_specs=[pl.BlockSpec((t, DA), lambda h: (0, h)),
                   pl.BlockSpec((8, DA), lambda h: (0, h))],
        out_shape=[jax.ShapeDtypeStruct((t, D), F32), jax.ShapeDtypeStruct((8, D), F32)],
        compiler_params=_cparams(("parallel",)),
    )(p, convw, dout)


def _gates_f(ps, alog_row, dt_row, w2f, b2f, w2b, b2b):
    lane = lax.broadcasted_iota(jnp.int32, ps.shape, 1)
    lg = -jnp.exp(alog_row) * _softplus(ps + dt_row)
    gsm = jnp.where(lane < 16, lg, jnp.where(lane < 32, _sigmoid(ps), 0.0))
    gkf = -_softplus(-(mm(ps, w2f) + b2f)) * (1.0 / 16.0)
    gkb = -_softplus(-(mm(ps, w2b) + b2b)) * (1.0 / 16.0)
    return gsm, gkf, gkb


def _gates_fwd(ps, alog_row, dt_row, w2f, b2f, w2b, b2b, tm=512):
    t = ps.shape[0]

    def body(ps_ref, a_ref, d_ref, wf_ref, bf_ref, wb_ref, bb_ref, gsm_ref, gk_ref):
        gsm, gkf, gkb = _gates_f(ps_ref[...], a_ref[...], d_ref[...], wf_ref[...], bf_ref[...],
                                 wb_ref[...], bb_ref[...])
        gsm_ref[...] = gsm
        gk_ref[0] = gkf
        gk_ref[1] = gkb

    row = lambda n: pl.BlockSpec((1, n), lambda i: (0, 0))
    mat = pl.BlockSpec((128, 512), lambda i: (0, 0))
    return pl.pallas_call(
        body, name="gates_fwd", grid=(t // tm,),
        in_specs=[pl.BlockSpec((tm, 128), lambda i: (i, PS_BLOCK)), row(128), row(128), mat, row(512), mat, row(512)],
        out_specs=[pl.BlockSpec((tm, 128), lambda i: (i, 0)),
                   pl.BlockSpec((2, tm, 512), lambda i: (0, i, 0))],
        out_shape=[jax.ShapeDtypeStruct((t, 128), F32), jax.ShapeDtypeStruct((2, t, 512), F32)],
        compiler_params=_cparams(("parallel",)),
    )(ps, alog_row, dt_row, w2f, b2f, w2b, b2b)


def _gates_bwd(ps, alog_row, dt_row, w2f, b2f, w2b, b2b, dgsm, dgk, tm=512):
    t = ps.shape[0]

    def body(ps_ref, a_ref, d_ref, wf_ref, bf_ref, wb_ref, bb_ref, dgsm_ref, dgk_ref,
             dps_ref, da_ref, dd_ref, dwf_ref, dbf_ref, dwb_ref, dbb_ref):
        _, vjp = jax.vjp(_gates_f, ps_ref[...], a_ref[...], d_ref[...], wf_ref[...], bf_ref[...],
                         wb_ref[...], bb_ref[...])
        dps, da, dd, dwf, dbf, dwb, dbb = vjp((dgsm_ref[...], dgk_ref[0], dgk_ref[1]))
        dps_ref[:, 0:128] = dps
        dps_ref[:, 128:DP_TILE] = jnp.zeros((tm, DP_TILE - 128), F32)
        accs = ((da_ref, da), (dd_ref, dd), (dwf_ref, dwf), (dbf_ref, dbf), (dwb_ref, dwb), (dbb_ref, dbb))

        @pl.when(pl.program_id(0) == 0)
        def _():
            for ref, _ in accs:
                ref[...] = jnp.zeros_like(ref)
        for ref, val in accs:
            ref[...] += jnp.broadcast_to(val, ref.shape)

    row = lambda n: pl.BlockSpec((1, n), lambda i: (0, 0))
    row8 = lambda n: pl.BlockSpec((8, n), lambda i: (0, 0))
    mat = pl.BlockSpec((128, 512), lambda i: (0, 0))
    return pl.pallas_call(
        body, name="gates_bwd", grid=(t // tm,),
        in_specs=[pl.BlockSpec((tm, 128), lambda i: (i, PS_BLOCK)), row(128), row(128), mat, row(512), mat, row(512),
                  pl.BlockSpec((tm, 128), lambda i: (i, 0)),
                  pl.BlockSpec((2, tm, 512), lambda i: (0, i, 0))],
        out_specs=[pl.BlockSpec((tm, DP_TILE), lambda i: (i, 0)), row8(128), row8(128), mat, row8(512), mat,
                   row8(512)],
        out_shape=[jax.ShapeDtypeStruct((t, DP_TILE), F32),
                   jax.ShapeDtypeStruct((8, 128), F32), jax.ShapeDtypeStruct((8, 128), F32),
                   jax.ShapeDtypeStruct((128, 512), F32), jax.ShapeDtypeStruct((8, 512), F32),
                   jax.ShapeDtypeStruct((128, 512), F32), jax.ShapeDtypeStruct((8, 512), F32)],
        compiler_params=_cparams(("arbitrary",)),
    )(ps, alog_row, dt_row, w2f, b2f, w2b, b2b, dgsm, dgk)


def _rows(i):
    return pl.ds(pl.multiple_of(i * CH, CH), CH)


def _srows(i):
    return pl.ds(pl.multiple_of(i * SC, SC), SC)


def _first_row(x):
    row = lax.broadcasted_iota(jnp.int32, (8, x.shape[1]), 0)
    return jnp.where(row == 0, jnp.broadcast_to(x, (8, x.shape[1])), 0.0)


def _chunk_rows(e_ref, i):
    pad = jnp.zeros((CH - 8, 128), F32)
    return jnp.concatenate([x for c in range(SC // CH) for x in (e_ref[(SC // CH) * i + c], pad)], axis=0)


def _gcum_f(gsm, tm):
    i = lax.broadcasted_iota(jnp.int32, (tm, tm), 0)
    j = lax.broadcasted_iota(jnp.int32, (tm, tm), 1)
    same = (i >> 6) == (j >> 6)
    lower = (same & (i >= j)).astype(F32)
    upper = (same & (i <= j)).astype(F32)
    r = lax.broadcasted_iota(jnp.int32, (128, D), 0)
    head = lax.broadcasted_iota(jnp.int32, (128, D), 1) >> 7
    pick = lambda off: (r == head + off).astype(F32)
    lane = lax.broadcasted_iota(jnp.int32, gsm.shape, 1)
    run = jnp.where(lane < 8, cmm(lower, gsm), cmm(upper, gsm))
    return mmc(run, pick(0)), mmc(run, pick(8)), mmc(gsm, pick(16)), mmc(gsm, pick(24))


def _gcum_fwd(gsm, tm=256):
    t = gsm.shape[0]

    def body(s_ref, g_ref, b_ref):
        gf, gb, bf, bb = _gcum_f(s_ref[...], tm)
        g_ref[0] = gf
        g_ref[1] = gb
        b_ref[0] = bf
        b_ref[1] = bb

    two = pl.BlockSpec((2, tm, D), lambda i: (0, i, 0))
    return pl.pallas_call(
        body, name="gcum_fwd", grid=(t // tm,),
        in_specs=[pl.BlockSpec((tm, 128), lambda i: (i, 0))], out_specs=[two, two],
        out_shape=[jax.ShapeDtypeStruct((2, t, D), F32)] * 2,
        compiler_params=_cparams(("parallel",)),
    )(gsm)


def _gcum_bwd(gsm, dg2, db2, tm=256):
    t = gsm.shape[0]

    def body(s_ref, dg_ref, db_ref, ds_ref):
        _, vjp = jax.vjp(lambda s: _gcum_f(s, tm), s_ref[...])
        ds_ref[...] = vjp((dg_ref[0], dg_ref[1], db_ref[0], db_ref[1]))[0]

    two = pl.BlockSpec((2, tm, D), lambda i: (0, i, 0))
    tile = pl.BlockSpec((tm, 128), lambda i: (i, 0))
    return pl.pallas_call(
        body, name="gcum_bwd", grid=(t // tm,),
        in_specs=[tile, two, two], out_specs=tile,
        out_shape=jax.ShapeDtypeStruct((t, 128), F32),
        compiler_params=_cparams(("parallel",)),
    )(gsm, dg2, db2)


def _gdn_intra_fwd(qn, kn, vc, g2, b2):
    t = qn.shape[0]
    n = t // CH

    def body(q_ref, k_ref, v_ref, g_ref, b_ref, u_ref, w_ref, a_ref, qd_ref, kd_ref, e_ref, t_ref):
        cs = _Consts(pl.program_id(0) == 1)

        def step(i, carry):
            r = _srows(i)
            q, k, v, g, bx = q_ref[r, :], k_ref[r, :], v_ref[r, :], g_ref[r, :], b_ref[r, :]
            u, w, a, qd, kd, el, tinv = _gdn_intra(q, k, v, g, bx, None, cs)
            u_ref[r, :] = u
            w_ref[r, :] = w
            a_ref[r, :] = a
            qd_ref[r, :] = qd
            kd_ref[r, :] = kd
            t_ref[r, :] = tinv
            for c in range(SC // CH):
                e_ref[(SC // CH) * i + c] = el[c * CH:c * CH + 8]
            return carry

        lax.fori_loop(0, t // SC, step, 0)

    head = pl.BlockSpec((t, DA), lambda d, h: (0, h))
    dh = pl.BlockSpec((None, t, DA), lambda d, h: (d, 0, h))
    sq = lambda w: pl.BlockSpec((None, None, t, w), lambda d, h: (d, h, 0, 0))
    big = jax.ShapeDtypeStruct((2, t, D), F32)
    return pl.pallas_call(
        body, name="gdn_intra_fwd", grid=(2, NA),
        in_specs=[head, head, head, dh, dh],
        out_specs=[dh, dh, sq(CH), dh, dh, pl.BlockSpec((None, None, n, 8, 128), lambda d, h: (d, h, 0, 0, 0)),
                   sq(SC)],
        out_shape=[big, big, jax.ShapeDtypeStruct((2, NA, t, CH), F32), big, big,
                   jax.ShapeDtypeStruct((2, NA, n, 8, 128), F32), jax.ShapeDtypeStruct((2, NA, t, SC), F32)],
        compiler_params=_cparams(("parallel", "parallel")),
    )(qn, kn, vc, g2, b2)


SCAN_TB = 256
SCAN_HB = 8


def _scan_specs(t, width, nheads, hb, along):
    nt = t // SCAN_TB
    nb = SCAN_TB // CH

    def tmap(d, tt):
        fwd = tt + d * (nt - 1 - 2 * tt)
        return fwd if along > 0 else nt - 1 - fwd

    tok = pl.BlockSpec((None, SCAN_TB, hb * width), lambda d, h, tt: (d, tmap(d, tt), h))
    per = lambda *tail: pl.BlockSpec((None, hb, nb) + tail, lambda d, h, tt: (d, h, tmap(d, tt)) + (0,) * len(tail))
    sq = pl.BlockSpec((None, hb, SCAN_TB, CH), lambda d, h, tt: (d, h, tmap(d, tt), 0))
    shared = lambda w: pl.BlockSpec((SCAN_TB, hb * w), lambda d, h, tt: (tmap(d, tt), h))
    return tok, per, sq, shared, (2, nheads // hb, nt), nb


def _gdn_scan_fwd(u, w, a, qd, kd, e):
    t = u.shape[1]
    tok, per, sq, _, grid, nb = _scan_specs(t, DA, NA, SCAN_HB, +1)

    def body(u_ref, w_ref, a_ref, qd_ref, kd_ref, e_ref, o_ref, s_ref, state):
        rev = pl.program_id(0) == 1

        @pl.when(pl.program_id(2) == 0)
        def _():
            state[...] = jnp.zeros_like(state)

        def step(i, ss):
            ci = jnp.where(rev, nb - 1 - i, i)
            r = _rows(ci)
            out = []
            for hh, s in enumerate(ss):
                c = slice(hh * DA, (hh + 1) * DA)
                s_ref[hh, ci] = s
                o, sn = _gdn_scan(u_ref[r, c], w_ref[r, c], a_ref[hh, r, :], qd_ref[r, c], kd_ref[r, c],
                                  e_ref[hh, ci][0:1], s)
                o_ref[r, c] = o
                out.append(sn)
            return tuple(out)

        ss = lax.fori_loop(0, nb, step, tuple(state[hh] for hh in range(SCAN_HB)))
        for hh, s in enumerate(ss):
            state[hh] = s

    return pl.pallas_call(
        body, name="gdn_scan_fwd", grid=grid,
        in_specs=[tok, tok, sq, tok, tok, per(8, 128)],
        out_specs=[tok, per(DA, DA)],
        out_shape=[jax.ShapeDtypeStruct((2, t, D), F32), jax.ShapeDtypeStruct((2, NA, t // CH, DA, DA), F32)],
        scratch_shapes=[pltpu.VMEM((SCAN_HB, DA, DA), F32)],
        compiler_params=_cparams(("parallel", "parallel", "arbitrary")),
    )(u, w, a, qd, kd, e)


def _gdn_scan_bwd(u, w, a, qd, kd, e, ssave, do):
    t = u.shape[1]
    tok, per, sq, shared, grid, nb = _scan_specs(t, DA, NA, SCAN_HB, -1)

    def body(u_ref, w_ref, a_ref, qd_ref, kd_ref, e_ref, s_ref, do_ref,
             du_ref, dw_ref, da_ref, dqd_ref, dkd_ref, de_ref, state):
        rev = pl.program_id(0) == 1

        @pl.when(pl.program_id(2) == 0)
        def _():
            state[...] = jnp.zeros_like(state)

        def step(i, dss):
            ci = jnp.where(rev, i, nb - 1 - i)
            r = _rows(ci)
            out = []
            for hh, ds in enumerate(dss):
                c = slice(hh * DA, (hh + 1) * DA)
                _, vjp = jax.vjp(_gdn_scan, u_ref[r, c], w_ref[r, c], a_ref[hh, r, :], qd_ref[r, c], kd_ref[r, c],
                                 e_ref[hh, ci][0:1], s_ref[hh, ci])
                du, dw, da, dqd, dkd, de, dsn = vjp((do_ref[r, c], ds))
                du_ref[r, c] = du
                dw_ref[r, c] = dw
                da_ref[hh, r, :] = da
                dqd_ref[r, c] = dqd
                dkd_ref[r, c] = dkd
                de_ref[hh, ci] = _first_row(de)
                out.append(dsn)
            return tuple(out)

        dss = lax.fori_loop(0, nb, step, tuple(state[hh] for hh in range(SCAN_HB)))
        for hh, ds in enumerate(dss):
            state[hh] = ds

    big = jax.ShapeDtypeStruct((2, t, D), F32)
    return pl.pallas_call(
        body, name="gdn_scan_bwd", grid=grid,
        in_specs=[tok, tok, sq, tok, tok, per(8, 128), per(DA, DA), shared(DA)],
        out_specs=[tok, tok, sq, tok, tok, per(8, 128)],
        out_shape=[big, big, jax.ShapeDtypeStruct((2, NA, t, CH), F32), big, big,
                   jax.ShapeDtypeStruct((2, NA, t // CH, 8, 128), F32)],
        scratch_shapes=[pltpu.VMEM((SCAN_HB, DA, DA), F32)],
        compiler_params=_cparams(("parallel", "parallel", "arbitrary")),
    )(u, w, a, qd, kd, e, ssave, do)


def _gdn_intra_bwd(qn, kn, vc, g2, b2, tinv, du, dw, da, dqd, dkd, de):
    t = qn.shape[0]
    n = t // CH

    def body(q_ref, k_ref, v_ref, g_ref, b_ref, t_ref, du_ref, dw_ref, da_ref, dqd_ref, dkd_ref, de_ref,
             dq_ref, dk_ref, dv_ref, dg_ref, db_ref):
        d = pl.program_id(1)
        cs = _Consts(d == 1)

        @pl.when(d == 0)
        def _():
            dq_ref[...] = jnp.zeros_like(dq_ref)
            dk_ref[...] = jnp.zeros_like(dk_ref)
            dv_ref[...] = jnp.zeros_like(dv_ref)

        def step(i, carry):
            r = _srows(i)
            tinv_c = t_ref[r, :]
            f = lambda q, k, v, g, bx: _gdn_intra(q, k, v, g, bx, tinv_c, cs)
            _, vjp = jax.vjp(f, q_ref[r, :], k_ref[r, :], v_ref[r, :], g_ref[r, :], b_ref[r, :])
            dq, dk, dv, dg, dbx = vjp((du_ref[r, :], dw_ref[r, :], da_ref[r, :], dqd_ref[r, :],
                                       dkd_ref[r, :], _chunk_rows(de_ref, i)))
            dq_ref[r, :] += dq
            dk_ref[r, :] += dk
            dv_ref[r, :] += dv
            dg_ref[r, :] = dg
            db_ref[r, :] = dbx
            return carry

        lax.fori_loop(0, t // SC, step, 0)

    head = pl.BlockSpec((t, DA), lambda h, d: (0, h))
    dh = pl.BlockSpec((None, t, DA), lambda h, d: (d, 0, h))
    sq = pl.BlockSpec((None, None, t, CH), lambda h, d: (d, h, 0, 0))
    tq = pl.BlockSpec((None, None, t, SC), lambda h, d: (d, h, 0, 0))
    full = jax.ShapeDtypeStruct((t, D), F32)
    big = jax.ShapeDtypeStruct((2, t, D), F32)
    return pl.pallas_call(
        body, name="gdn_intra_bwd", grid=(NA, 2),
        in_specs=[head, head, head, dh, dh, tq, dh, dh, sq, dh, dh,
                  pl.BlockSpec((None, None, n, 8, 128), lambda h, d: (d, h, 0, 0, 0))],
        out_specs=[head, head, head, dh, dh],
        out_shape=[full, full, full, big, big],
        compiler_params=_cparams(("arbitrary", "arbitrary")),
    )(qn, kn, vc, g2, b2, tinv, du, dw, da, dqd, dkd, de)


def _gla_specs(t, order):
    ix = (lambda d, h: (d, h)) if order == "dh" else (lambda h, d: (d, h))

    def mk(fn):
        return lambda a, b: fn(*ix(a, b))
    q = pl.BlockSpec((t, DKB), mk(lambda d, h: (0, 32 + h)))
    k = pl.BlockSpec((t, DKB), mk(lambda d, h: (0, 36 + h)))
    v = pl.BlockSpec((t, DVB), mk(lambda d, h: (0, 20 + h)))
    dk = pl.BlockSpec((None, t, DKB), mk(lambda d, h: (d, 0, h)))
    dv = pl.BlockSpec((None, t, DVB), mk(lambda d, h: (d, 0, h)))
    e = pl.BlockSpec((None, None, t // CH, 8, 128), mk(lambda d, h: (d, h, 0, 0, 0)))
    s = pl.BlockSpec((None, None, t // CH, DVB, DKB), mk(lambda d, h: (d, h, 0, 0, 0)))
    return q, k, v, dk, dv, e, s


def _gla_intra_fwd(p, gk):
    t = p.shape[0]
    n = t // CH

    def body(q_ref, k_ref, v_ref, g_ref, qg_ref, kd_ref, in_ref, e_ref):
        cs = _Consts(pl.program_id(0) == 1)

        def step(i, carry):
            r = _srows(i)
            qg, kd, intra, el = _gla_intra(q_ref[r, :], k_ref[r, :], v_ref[r, :], g_ref[r, :], cs)
            qg_ref[r, :] = qg
            kd_ref[r, :] = kd
            in_ref[r, :] = intra
            for c in range(SC // CH):
                e_ref[(SC // CH) * i + c] = el[c * CH:c * CH + 8]
            return carry

        lax.fori_loop(0, t // SC, step, 0)

    q, k, v, dk, dv, e, _ = _gla_specs(t, "dh")
    return pl.pallas_call(
        body, name="gla_intra_fwd", grid=(2, NB),
        in_specs=[q, k, v, dk], out_specs=[dk, dk, dv, e],
        out_shape=[jax.ShapeDtypeStruct((2, t, NB * DKB), F32), jax.ShapeDtypeStruct((2, t, NB * DKB), F32),
                   jax.ShapeDtypeStruct((2, t, D), F32), jax.ShapeDtypeStruct((2, NB, n, 8, 128), F32)],
        compiler_params=_cparams(("parallel", "parallel")),
    )(p, p, p, gk)


GLA_HB = 2


def _gla_v_spec(t, along):
    nt = t // SCAN_TB

    def tmap(d, tt):
        fwd = tt + d * (nt - 1 - 2 * tt)
        return fwd if along > 0 else nt - 1 - fwd

    return pl.BlockSpec((SCAN_TB, GLA_HB * DVB), lambda d, h, tt: (tmap(d, tt), 5120 // (GLA_HB * DVB) + h))


def _gla_scan_fwd(p, qg, kd, intra, e):
    t = p.shape[0]
    tokk, per, _, _, grid, nb = _scan_specs(t, DKB, NB, GLA_HB, +1)
    tokv = _scan_specs(t, DVB, NB, GLA_HB, +1)[0]

    def body(v_ref, qg_ref, kd_ref, in_ref, e_ref, o_ref, s_ref, state):
        rev = pl.program_id(0) == 1

        @pl.when(pl.program_id(2) == 0)
        def _():
            state[...] = jnp.zeros_like(state)

        def step(i, sts):
            ci = jnp.where(rev, nb - 1 - i, i)
            r = _rows(ci)
            out = []
            for hh, st in enumerate(sts):
                ck = slice(hh * DKB, (hh + 1) * DKB)
                cv = slice(hh * DVB, (hh + 1) * DVB)
                s_ref[hh, ci] = st
                o, stn = _gla_scan(qg_ref[r, ck], kd_ref[r, ck], v_ref[r, cv], e_ref[hh, ci][0:1], st)
                o_ref[r, cv] = o + in_ref[r, cv]
                out.append(stn)
            return tuple(out)

        sts = lax.fori_loop(0, nb, step, tuple(state[hh] for hh in range(GLA_HB)))
        for hh, st in enumerate(sts):
            state[hh] = st

    return pl.pallas_call(
        body, name="gla_scan_fwd", grid=grid,
        in_specs=[_gla_v_spec(t, +1), tokk, tokk, tokv, per(8, 128)], out_specs=[tokv, per(DVB, DKB)],
        out_shape=[jax.ShapeDtypeStruct((2, t, D), F32), jax.ShapeDtypeStruct((2, NB, t // CH, DVB, DKB), F32)],
        scratch_shapes=[pltpu.VMEM((GLA_HB, DVB, DKB), F32)],
        compiler_params=_cparams(("parallel", "parallel", "arbitrary")),
    )(p, qg, kd, intra, e)


def _gla_scan_bwd(p, qg, kd, e, ssave, do):
    t = p.shape[0]
    tokk, per, _, shared, grid, nb = _scan_specs(t, DKB, NB, GLA_HB, -1)
    tokv = _scan_specs(t, DVB, NB, GLA_HB, -1)[0]

    def body(v_ref, qg_ref, kd_ref, e_ref, s_ref, do_ref, dqg_ref, dkd_ref, dv_ref, de_ref, state):
        rev = pl.program_id(0) == 1

        @pl.when(pl.program_id(2) == 0)
        def _():
            state[...] = jnp.zeros_like(state)

        def step(i, dsts):
            ci = jnp.where(rev, i, nb - 1 - i)
            r = _rows(ci)
            out = []
            for hh, dst in enumerate(dsts):
                ck = slice(hh * DKB, (hh + 1) * DKB)
                cv = slice(hh * DVB, (hh + 1) * DVB)
                _, vjp = jax.vjp(_gla_scan, qg_ref[r, ck], kd_ref[r, ck], v_ref[r, cv], e_ref[hh, ci][0:1],
                                 s_ref[hh, ci])
                dqg, dkd, dv, de, dstn = vjp((do_ref[r, cv], dst))
                dqg_ref[r, ck] = dqg
                dkd_ref[r, ck] = dkd
                dv_ref[r, cv] = dv
                de_ref[hh, ci] = _first_row(de)
                out.append(dstn)
            return tuple(out)

        dsts = lax.fori_loop(0, nb, step, tuple(state[hh] for hh in range(GLA_HB)))
        for hh, dst in enumerate(dsts):
            state[hh] = dst

    return pl.pallas_call(
        body, name="gla_scan_bwd", grid=grid,
        in_specs=[_gla_v_spec(t, -1), tokk, tokk, per(8, 128), per(DVB, DKB), shared(DVB)],
        out_specs=[tokk, tokk, tokv, per(8, 128)],
        out_shape=[jax.ShapeDtypeStruct((2, t, NB * DKB), F32), jax.ShapeDtypeStruct((2, t, NB * DKB), F32),
                   jax.ShapeDtypeStruct((2, t, D), F32), jax.ShapeDtypeStruct((2, NB, t // CH, 8, 128), F32)],
        scratch_shapes=[pltpu.VMEM((GLA_HB, DVB, DKB), F32)],
        compiler_params=_cparams(("parallel", "parallel", "arbitrary")),
    )(p, qg, kd, e, ssave, do)


def _gla_intra_bwd(p, gk, dqg, dkd, dvs, de, do):
    t = p.shape[0]
    n = t // CH

    def body(q_ref, k_ref, v_ref, g_ref, dqg_ref, dkd_ref, dvs_ref, de_ref, do_ref,
             dq_ref, dk_ref, dv_ref, dg_ref):
        d = pl.program_id(1)
        cs = _Consts(d == 1)

        @pl.when(d == 0)
        def _():
            dq_ref[...] = jnp.zeros_like(dq_ref)
            dk_ref[...] = jnp.zeros_like(dk_ref)
            dv_ref[...] = jnp.zeros_like(dv_ref)

        def step(i, carry):
            r = _srows(i)
            f = lambda q, k, v, g: _gla_intra(q, k, v, g, cs)
            _, vjp = jax.vjp(f, q_ref[r, :], k_ref[r, :], v_ref[r, :], g_ref[r, :])
            dq, dk, dv, dg = vjp((dqg_ref[r, :], dkd_ref[r, :], do_ref[r, :], _chunk_rows(de_ref, i)))
            dq_ref[r, :] += dq
            dk_ref[r, :] += dk
            dv_ref[r, :] += dv + dvs_ref[r, :]
            dg_ref[r, :] = dg
            return carry

        lax.fori_loop(0, t // SC, step, 0)

    q, k, v, dk, dv, e_s, _ = _gla_specs(t, "hd")
    hk = pl.BlockSpec((t, DKB), lambda h, d: (0, h))
    hv = pl.BlockSpec((t, DVB), lambda h, d: (0, h))
    return pl.pallas_call(
        body, name="gla_intra_bwd", grid=(NB, 2),
        in_specs=[q, k, v, dk, dk, dk, dv, e_s, hv],
        out_specs=[hk, hk, hv, dk],
        out_shape=[jax.ShapeDtypeStruct((t, NB * DKB), F32), jax.ShapeDtypeStruct((t, NB * DKB), F32),
                   jax.ShapeDtypeStruct((t, D), F32), jax.ShapeDtypeStruct((2, t, NB * DKB), F32)],
        compiler_params=_cparams(("arbitrary", "arbitrary")),
    )(p, p, p, gk, dqg, dkd, dvs, de, do)


def _seg_gate(o, z, w):
    return _rms(o, w) * _silu(z)


def _seg_merge(ya, yb, ga, gb):
    return _sigmoid(ga) * ya + _sigmoid(gb) * yb


def _seg_loss(out, x, tgt, w):
    err = x + _rms(out, w) - tgt
    return 0.5 * jnp.sum(jnp.mean(err * err, axis=-1, keepdims=True), axis=0, keepdims=True)


def _post(oa2, ob2, p, x, tgt, gdn_w, gla_w, lnpost, w3, tm=128):
    t = x.shape[0]

    def body(oa_ref, ob_ref, z_ref, gb_ref, ga_ref, gB_ref, x_ref, t_ref, aw_ref, bw_ref, lw_ref, w_ref,
             loss_ref, doa_ref, dob_ref, dz_ref, dgb_ref, dga_ref, dgB_ref, dy_ref,
             dw_ref, daw_ref, dbw_ref, dlw_ref):
        first = pl.program_id(0) == 0
        oa = oa_ref[0] + oa_ref[1]
        ob = ob_ref[0] + ob_ref[1]
        z, gb = z_ref[...], gb_ref[...]
        aw, bw = aw_ref[...], bw_ref[...]

        pa = [jax.vjp(_seg_gate, oa[:, h * DA:(h + 1) * DA], z[:, h * DA:(h + 1) * DA], aw) for h in range(NA)]
        pb = [jax.vjp(_seg_gate, ob[:, h * DVB:(h + 1) * DVB], gb[:, h * DVB:(h + 1) * DVB], bw)
              for h in range(NB)]
        a1 = jnp.concatenate([v for v, _ in pa], axis=1).astype(BF16)
        a2 = jnp.concatenate([v for v, _ in pb], axis=1).astype(BF16)
        ya = jnp.dot(a1, w_ref[0], preferred_element_type=F32)
        yb = jnp.dot(a2, w_ref[1], preferred_element_type=F32)
        merged, vjp_m = jax.vjp(_seg_merge, ya, yb, ga_ref[...], gB_ref[...])
        mb = merged.astype(BF16)
        out = jnp.dot(mb, w_ref[2], preferred_element_type=F32)
        loss, vjp_l = jax.vjp(_seg_loss, out, x_ref[...], t_ref[...], lw_ref[...])
        dout, dyres, _, dlw = vjp_l(jnp.ones((1, 1), F32))
        dy_ref[...] = dyres
        doutb = dout.astype(BF16)
        dmerged = _bdot(doutb, w_ref[2], 1, 1)
        dya, dyb, dga, dgB = vjp_m(dmerged)
        dga_ref[...] = dga
        dgB_ref[...] = dgB
        dyab, dybb = dya.astype(BF16), dyb.astype(BF16)
        da1 = _bdot(dyab, w_ref[0], 1, 1)
        da2 = _bdot(dybb, w_ref[1], 1, 1)

        daw = jnp.zeros_like(aw)
        for h in range(NA):
            sl = slice(h * DA, (h + 1) * DA)
            do, dz, dw = pa[h][1](da1[:, sl])
            doa_ref[:, sl] = do
            dz_ref[:, sl] = dz
            daw = daw + dw
        dbw = jnp.zeros_like(bw)
        for h in range(NB):
            sl = slice(h * DVB, (h + 1) * DVB)
            do, dg, dw = pb[h][1](da2[:, sl])
            dob_ref[:, sl] = do
            dgb_ref[:, sl] = dg
            dbw = dbw + dw

        @pl.when(first)
        def _():
            loss_ref[...] = jnp.zeros_like(loss_ref)
            dw_ref[...] = jnp.zeros_like(dw_ref)
            daw_ref[...] = jnp.zeros_like(daw_ref)
            dbw_ref[...] = jnp.zeros_like(dbw_ref)
            dlw_ref[...] = jnp.zeros_like(dlw_ref)

        loss_ref[...] += jnp.broadcast_to(loss, loss_ref.shape)
        dw_ref[0] += _bdot(a1, dyab, 0, 0)
        dw_ref[1] += _bdot(a2, dybb, 0, 0)
        dw_ref[2] += _bdot(mb, doutb, 0, 0)
        daw_ref[...] += jnp.broadcast_to(daw, daw_ref.shape)
        dbw_ref[...] += jnp.broadcast_to(dbw, dbw_ref.shape)
        dlw_ref[...] += jnp.broadcast_to(dlw, dlw_ref.shape)

    two = pl.BlockSpec((2, tm, D), lambda i: (0, i, 0))
    pcol = lambda c: pl.BlockSpec((tm, D), lambda i: (i, c))
    tok = pl.BlockSpec((tm, D), lambda i: (i, 0))
    row = lambda n: pl.BlockSpec((1, n), lambda i: (0, 0))
    row8 = lambda n: pl.BlockSpec((8, n), lambda i: (0, 0))
    once = pl.Buffered(1)
    tokf = jax.ShapeDtypeStruct((t, D), F32)
    return pl.pallas_call(
        body, name="post", grid=(t // tm,),
        in_specs=[two, two, pcol(3), pcol(6), pcol(7), pcol(8), tok, tok, row(DA), row(DVB), row(D),
                  pl.BlockSpec((3, D, D), lambda i: (0, 0, 0), pipeline_mode=once)],
        out_specs=[row8(128), tok, tok, tok, tok, tok, tok, tok,
                   pl.BlockSpec((3, D, D), lambda i: (0, 0, 0), pipeline_mode=once),
                   row8(DA), row8(DVB), row8(D)],
        out_shape=[jax.ShapeDtypeStruct((8, 128), F32), tokf, tokf, tokf, tokf, tokf, tokf, tokf,
                   jax.ShapeDtypeStruct((3, D, D), F32),
                   jax.ShapeDtypeStruct((8, DA), F32), jax.ShapeDtypeStruct((8, DVB), F32),
                   jax.ShapeDtypeStruct((8, D), F32)],
        compiler_params=_cparams(("arbitrary",), vmem_mb=56),
    )(oa2, ob2, p, p, p, p, x, tgt, gdn_w, gla_w, lnpost, w3)


def _adam_math(w, g, m, v):
    nm = B1 * m + (1.0 - B1) * g
    nv = B2 * v + (1.0 - B2) * (g * g)
    m_hat = nm / (1.0 - B1 ** STEP)
    v_hat = nv / (1.0 - B2 ** STEP)
    return -LR * (m_hat / (jnp.sqrt(v_hat) + ADAM_EPS) + WD * w), nm, nv


SMALL_SLOTS = (("ln_pre_w", 0, 1024, 0), ("a_log_fwd", 1024, 8, 0), ("a_log_bwd", 1024, 8, 8),
               ("dt_bias_fwd", 1152, 8, 0), ("dt_bias_bwd", 1152, 8, 8), ("gdn_norm_w", 1280, 128, 0),
               ("gk_b2_fwd", 1408, 512, 0), ("gk_b2_bwd", 1920, 512, 0), ("gla_norm_w", 2432, 256, 0),
               ("ln_post_w", 2688, 1024, 0))
SMALL_W = 3712


def _adam_small(gsum, ws, ms, vs):
    nw = len(SMALL_SLOTS)

    def body(g_ref, *refs):
        w_refs, m_refs, v_refs, outs = refs[0:nw], refs[nw:2 * nw], refs[2 * nw:3 * nw], refs[3 * nw:]
        for i, (_, off, n, shift) in enumerate(SMALL_SLOTS):
            slot = g_ref[0:1, off:off + max(n, 128)]
            if shift:
                slot = pltpu.roll(slot, 128 - shift, 1)
            g = slot[:, 0:n]
            d, nm, nv = _adam_math(w_refs[i][...], g, m_refs[i][...], v_refs[i][...])
            for k, val in enumerate((g, d, nm, nv)):
                outs[4 * i + k][...] = val

    vm = pl.BlockSpec(memory_space=pltpu.VMEM)
    res = pl.pallas_call(
        body, name="adam_small", in_specs=[vm] * (1 + 3 * nw), out_specs=[vm] * (4 * nw),
        out_shape=[jax.ShapeDtypeStruct((1, n), F32) for _, _, n, _ in SMALL_SLOTS for _ in range(4)],
    )(gsum, *ws, *ms, *vs)
    return {name: res[4 * i:4 * i + 4] for i, (name, _, _, _) in enumerate(SMALL_SLOTS)}


def _adam(w, g, m, v, tr):
    rows, cols = w.shape

    def body(w_ref, g_ref, m_ref, v_ref, d_ref, nm_ref, nv_ref):
        d, nm, nv = _adam_math(w_ref[...], g_ref[...], m_ref[...], v_ref[...])
        d_ref[...] = d
        nm_ref[...] = nm
        nv_ref[...] = nv

    blk = pl.BlockSpec((tr, cols), lambda i: (i, 0))
    shp = jax.ShapeDtypeStruct((rows, cols), F32)
    return pl.pallas_call(
        body, name=f"adam_{rows}x{cols}", grid=(rows // tr,),
        in_specs=[blk] * 4, out_specs=[blk] * 3, out_shape=[shp] * 3,
        compiler_params=_cparams(("parallel",)),
    )(w, g, m, v)


def _sum_cast(own, got):
    _, ns, r, c = own.shape
    tr = r // 4 if r >= 64 else r

    def body(c_ref, a_ref, b_ref, f_ref, h_ref):
        s = a_ref[...] + b_ref[...]
        f_ref[...] = s
        h_ref[...] = s.astype(BF16)

    return pl.pallas_call(
        body, name=f"sum_cast_{r}x{c}",
        grid_spec=pltpu.PrefetchScalarGridSpec(
            num_scalar_prefetch=1, grid=(ns, r // tr),
            in_specs=[pl.BlockSpec((None, None, tr, c), lambda s, i, cc: (cc[0], s, i, 0)),
                      pl.BlockSpec((None, tr, c), lambda s, i, cc: (s, i, 0))],
            out_specs=[pl.BlockSpec((None, tr, c), lambda s, i, cc: (s, i, 0)),
                       pl.BlockSpec((None, tr, c), lambda s, i, cc: (s, i, 0))]),
        out_shape=[jax.ShapeDtypeStruct((ns, r, c), F32), jax.ShapeDtypeStruct((ns, r, c), BF16)],
        compiler_params=_cparams(("parallel", "parallel")),
    )(lax.axis_index("c").reshape(1), own, got)


def _sum4(mine, got):
    _, r, c = mine.shape
    tr = r // 4 if r >= 64 else r

    def body(s_ref, a_ref, g_ref, o_ref):
        acc = a_ref[...] + g_ref[0].astype(F32)
        acc = acc + g_ref[1].astype(F32)
        o_ref[...] = acc + g_ref[2].astype(F32)

    shard = (2 * lax.axis_index("x") + lax.axis_index("y")).reshape(1)
    return pl.pallas_call(
        body, name=f"sum4_{r}x{c}",
        grid_spec=pltpu.PrefetchScalarGridSpec(
            num_scalar_prefetch=1, grid=(r // tr,),
            in_specs=[pl.BlockSpec((None, tr, c), lambda i, ss: (ss[0], i, 0)),
                      pl.BlockSpec((3, tr, c), lambda i, ss: (0, i, 0))],
            out_specs=pl.BlockSpec((tr, c), lambda i, ss: (i, 0))),
        out_shape=jax.ShapeDtypeStruct((r, c), F32),
        compiler_params=_cparams(("parallel",)),
    )(shard, mine, got)


def _place():
    x, y, c = lax.axis_index("x"), lax.axis_index("y"), lax.axis_index("c")
    chips = [(1 - x, y), (x, 1 - y), (1 - x, 1 - y)]
    return x, y, c, chips


def _gather_weights(parts):
    npart = len(parts)

    def body(*refs):
        ins, outs = refs[:npart], refs[npart:2 * npart]
        send_sems, recv_sems = refs[2 * npart:]
        x, y, c, chips = _place()
        sibling = (x, y, 1 - c)
        mine = 2 * x + y

        def remote(k, p, shard, half, to, src=None):
            dst = outs[p].at[shard, half]
            return pltpu.make_async_remote_copy(
                src_ref=dst if src is None else src, dst_ref=dst,
                send_sem=send_sems.at[k], recv_sem=recv_sems.at[k], device_id=to, device_id_type=MESH)

        first = [remote(j * npart + p, p, mine, c, (*chip, c), src=ins[p].at[c])
                 for j, chip in enumerate(chips) for p in range(npart)]
        for cp in first:
            cp.start()
        passed = []
        for j, (cx, cy) in enumerate(chips):
            for p in range(npart):
                remote(j * npart + p, p, 2 * cx + cy, c, (x, y, c)).wait_recv()
                fw = remote((3 + j) * npart + p, p, 2 * cx + cy, c, sibling)
                fw.start()
                passed.append(fw)
        for j, (cx, cy) in enumerate(chips):
            for p in range(npart):
                remote((3 + j) * npart + p, p, 2 * cx + cy, 1 - c, (x, y, c)).wait_recv()
        for cp in first + passed:
            cp.wait_send()

    got = pl.pallas_call(
        body, name="gather_weights",
        in_specs=[ANY] * npart, out_specs=[ANY] * npart,
        out_shape=[jax.ShapeDtypeStruct((NSHARD,) + a.shape, a.dtype) for a in parts],
        scratch_shapes=[pltpu.SemaphoreType.DMA((6 * npart,)), pltpu.SemaphoreType.DMA((6 * npart,))],
    )(*parts)
    mine = 2 * lax.axis_index("x") + lax.axis_index("y")
    return [lax.dynamic_update_index_in_dim(g, a, mine, 0) for g, a in zip(got, parts)]


def _swap_halves(parts):
    npart = len(parts)

    def body(*refs):
        ins, outs = refs[:npart], refs[npart:2 * npart]
        send_sems, recv_sems = refs[2 * npart:]
        x, y, c, _ = _place()
        cps = [pltpu.make_async_remote_copy(
            src_ref=ins[p].at[1 - c], dst_ref=outs[p], send_sem=send_sems.at[p], recv_sem=recv_sems.at[p],
            device_id=(x, y, 1 - c), device_id_type=MESH) for p in range(npart)]
        for cp in cps:
            cp.start()
        for cp in cps:
            cp.wait()

    return pl.pallas_call(
        body, name="swap_halves", in_specs=[ANY] * npart, out_specs=[ANY] * npart,
        out_shape=[jax.ShapeDtypeStruct(a.shape[1:], a.dtype) for a in parts],
        scratch_shapes=[pltpu.SemaphoreType.DMA((npart,)), pltpu.SemaphoreType.DMA((npart,))],
    )(*parts)


def _scatter_shards(parts):
    npart = len(parts)

    def body(*refs):
        ins, outs = refs[:npart], refs[npart:2 * npart]
        send_sems, recv_sems = refs[2 * npart:]
        x, y, c, chips = _place()
        cps = [pltpu.make_async_remote_copy(
            src_ref=ins[p].at[2 * cx + cy], dst_ref=outs[p].at[j],
            send_sem=send_sems.at[j * npart + p], recv_sem=recv_sems.at[j * npart + p],
            device_id=(cx, cy, c), device_id_type=MESH)
            for j, (cx, cy) in enumerate(chips) for p in range(npart)]
        for cp in cps:
            cp.start()
        for cp in cps:
            cp.wait()

    return pl.pallas_call(
        body, name="scatter_shards", in_specs=[ANY] * npart, out_specs=[ANY] * npart,
        out_shape=[jax.ShapeDtypeStruct((3,) + a.shape[1:], a.dtype) for a in parts],
        scratch_shapes=[pltpu.SemaphoreType.DMA((3 * npart,)), pltpu.SemaphoreType.DMA((3 * npart,))],
    )(*parts)


def _join_halves(parts):
    npart = len(parts)

    def body(*refs):
        ins, outs = refs[:npart], refs[npart:2 * npart]
        send_sems, recv_sems = refs[2 * npart:]
        x, y, c, _ = _place()
        cps = [pltpu.make_async_remote_copy(
            src_ref=ins[p], dst_ref=outs[p], send_sem=send_sems.at[p], recv_sem=recv_sems.at[p],
            device_id=(x, y, 1 - c), device_id_type=MESH) for p in range(npart)]
        for cp in cps:
            cp.start()
        for cp in cps:
            cp.wait()

    got = pl.pallas_call(
        body, name="join_halves", in_specs=[ANY] * npart, out_specs=[ANY] * npart,
        out_shape=[jax.ShapeDtypeStruct(a.shape, a.dtype) for a in parts],
        scratch_shapes=[pltpu.SemaphoreType.DMA((npart,)), pltpu.SemaphoreType.DMA((npart,))],
    )(*parts)
    south = lax.axis_index("c") == 0
    return [jnp.where(south, jnp.stack([a, g]), jnp.stack([g, a])) for a, g in zip(parts, got)]


def _allreduce_small(v):
    r, ncol = v.shape

    def body(v_ref, o_ref, buf, send_sems, recv_sems):
        x, y, c, _ = _place()
        me = 4 * x + 2 * y + c
        buf[me] = v_ref[...]
        cps = []
        for k in range(1, 8):
            px, py, pc = x ^ (k >> 2), y ^ ((k >> 1) & 1), c ^ (k & 1)
            cps.append(pltpu.make_async_remote_copy(
                src_ref=v_ref, dst_ref=buf.at[me], send_sem=send_sems.at[k - 1], recv_sem=recv_sems.at[k - 1],
                device_id=(px, py, pc), device_id_type=MESH))
        for cp in cps:
            cp.start()
        for k in range(1, 8):
            px, py, pc = x ^ (k >> 2), y ^ ((k >> 1) & 1), c ^ (k & 1)
            pltpu.make_async_remote_copy(
                src_ref=v_ref, dst_ref=buf.at[4 * px + 2 * py + pc], send_sem=send_sems.at[k - 1],
                recv_sem=recv_sems.at[k - 1], device_id=(px, py, pc), device_id_type=MESH).wait_recv()
        for cp in cps:
            cp.wait_send()
        acc = buf[0]
        for d in range(1, 8):
            acc = acc + buf[d]
        o_ref[...] = acc

    return pl.pallas_call(
        body, name="allreduce_small",
        in_specs=[pl.BlockSpec(memory_space=pltpu.VMEM)], out_specs=pl.BlockSpec(memory_space=pltpu.VMEM),
        out_shape=jax.ShapeDtypeStruct((r, ncol), F32),
        scratch_shapes=[pltpu.VMEM((8, r, ncol), F32), pltpu.SemaphoreType.DMA((7,)), pltpu.SemaphoreType.DMA((7,))],
    )(v)


def _permute_cols(w):
    zeros = jnp.zeros((w.shape[0], NPERM - 9280), w.dtype)
    return jnp.concatenate([w[:, 0:4096], w[:, 4128:7200], w[:, 7232:9280], w[:, 4096:4128], w[:, 7200:7232], zeros],
                           axis=1)


def _unpermute_cols(g):
    return jnp.concatenate([g[:, 0:4096], g[:, 9216:9248], g[:, 4096:7168], g[:, 9248:9280], g[:, 7168:9216]],
                           axis=1)


def _pack_shard_small(conv, w2f, w2b):
    top = jnp.pad(conv, ((0, 8 - conv.shape[0]), (0, 0)))
    mid = jnp.pad(jnp.concatenate([w2f, w2b], axis=1), ((0, 0), (0, 768 - 256)))
    return jnp.concatenate([top, mid, jnp.zeros((8, 768), conv.dtype)], axis=0)


def _unpack_shard_small(a):
    return a[0:5], a[8:24, 0:128], a[8:24, 128:256]


def kernel(x, ln_pre_w, w_in, conv_w, a_log_fwd, a_log_bwd, dt_bias_fwd, dt_bias_bwd, gdn_norm_w, w_proj_gdn, gk_w2_fwd, gk_b2_fwd, gk_w2_bwd, gk_b2_bwd, gla_norm_w, w_proj_gla, w_out, ln_post_w, loss_target, m_ln_pre_w, m_w_in, m_conv_w, m_a_log_fwd, m_a_log_bwd, m_dt_bias_fwd, m_dt_bias_bwd, m_gdn_norm_w, m_w_proj_gdn, m_gk_w2_fwd, m_gk_b2_fwd, m_gk_w2_bwd, m_gk_b2_bwd, m_gla_norm_w, m_w_proj_gla, m_w_out, m_ln_post_w, v_ln_pre_w, v_w_in, v_conv_w, v_a_log_fwd, v_a_log_bwd, v_dt_bias_fwd, v_dt_bias_bwd, v_gdn_norm_w, v_w_proj_gdn, v_gk_w2_fwd, v_gk_b2_fwd, v_gk_w2_bwd, v_gk_b2_bwd, v_gla_norm_w, v_w_proj_gla, v_w_out, v_ln_post_w):
    t = x.shape[1]
    x2, tgt = x[0], loss_target[0]

    win_l = w_in[0].astype(BF16).reshape(2, D // 2, SHW)
    proj_l = jnp.concatenate([w_proj_gdn[0], w_proj_gla[0], w_out[0]], axis=0).astype(BF16).reshape(2, 384, D)
    small_l = _pack_shard_small(conv_w[0], gk_w2_fwd[0], gk_w2_bwd[0]).reshape(2, 16, 768)
    win_g, proj_g, small_g = _gather_weights([win_l, proj_l, small_l])
    w_full = win_g.reshape(NSHARD, D, SHW).transpose(1, 0, 2).reshape(D, NSHARD * SHW)
    wperm = _permute_cols(w_full)
    w3 = proj_g.reshape(NSHARD, 3, D // NSHARD, D).transpose(1, 0, 2, 3).reshape(3, D, D)
    small_g = small_g.reshape(NSHARD, 32, 768)
    convw = small_g[:, 0:8, :].transpose(1, 0, 2).reshape(8, 3 * D)
    w2f = small_g[:, 8:24, 0:128].transpose(1, 0, 2).reshape(16, 512)
    w2b = small_g[:, 8:24, 128:256].transpose(1, 0, 2).reshape(16, 512)
    w2f_pad = jnp.pad(w2f, ((32, 80), (0, 0)))
    w2b_pad = jnp.pad(w2b, ((48, 64), (0, 0)))
    alog_row = jnp.pad(jnp.concatenate([a_log_fwd, a_log_bwd], axis=1), ((0, 0), (0, 112)))
    dt_row = jnp.pad(jnp.concatenate([dt_bias_fwd, dt_bias_bwd], axis=1), ((0, 0), (0, 112)))

    p, h = _inproj(x2, ln_pre_w, wperm)
    qn, kn, vc = (_qkv_fwd(p, convw, kind) for kind in range(3))
    gsm, gk = _gates_fwd(p, alog_row, dt_row, w2f_pad, gk_b2_fwd, w2b_pad, gk_b2_bwd)
    g2, b2 = _gcum_fwd(gsm)
    u, w, at, qd, kd, el, tinv = _gdn_intra_fwd(qn, kn, vc, g2, b2)
    oa2, sa = _gdn_scan_fwd(u, w, at, qd, kd, el)
    qg, kdb, intra, elb = _gla_intra_fwd(p, gk)
    ob2, sb = _gla_scan_fwd(p, qg, kdb, intra, elb)

    (loss8, doa, dob, dz, dgb, dga, dgB, dyres, dw3, dgdn_w, dgla_w, dlnpost) = _post(
        oa2, ob2, p, x2, tgt, gdn_norm_w, gla_norm_w, ln_post_w, w3)

    du, dw, dat, dqd, dkd, del_ = _gdn_scan_bwd(u, w, at, qd, kd, el, sa, doa)
    dqn, dkn, dvc, dg2, db2 = _gdn_intra_bwd(qn, kn, vc, g2, b2, tinv, du, dw, dat, dqd, dkd, del_)
    dgsm = _gcum_bwd(gsm, dg2, db2)
    dqg, dkdb, dvs, delb = _gla_scan_bwd(p, qg, kdb, elb, sb, dob)
    dqb, dkb, dvb, dgk = _gla_intra_bwd(p, gk, dqg, dkdb, dvs, delb, dob)
    (dps, dalog8, ddt8, dw2f_pad, db2f8, dw2b_pad, db2b8) = _gates_bwd(
        p, alog_row, dt_row, w2f_pad, gk_b2_fwd, w2b_pad, gk_b2_bwd, dgsm, dgk)
    dpre, dconv = zip(*[_qkv_bwd(p, convw, g, kind) for kind, g in enumerate((dqn, dkn, dvc))])

    pieces = (*dpre, dz, dqb, dkb, dvb, dgb, dga, dgB, dps)
    dwperm = _inproj_dw(h, pieces)
    dx, dlnpre8 = _inproj_dx(pieces, wperm, x2, ln_pre_w, dyres)

    dw_in_full = _unpermute_cols(dwperm)
    g_in = dw_in_full.reshape(2, D // 2, NSHARD, SHW).transpose(0, 2, 1, 3)
    g_proj = dw3.reshape(3, NSHARD, D // NSHARD, D).transpose(1, 0, 2, 3).reshape(NSHARD, 2, 384, D)
    g_proj = g_proj.transpose(1, 0, 2, 3)
    dconv_full = jnp.concatenate(dconv, axis=1)
    dw2f, dw2b = dw2f_pad[32:48], dw2b_pad[48:64]
    g_small = jnp.stack([_pack_shard_small(dconv_full[0:5, 768 * s:768 * (s + 1)],
                                           dw2f[:, 128 * s:128 * (s + 1)], dw2b[:, 128 * s:128 * (s + 1)])
                         for s in range(NSHARD)])
    g_small = g_small.reshape(NSHARD, 2, 16, 768).transpose(1, 0, 2, 3)
    parts = [g_in, g_proj, g_small]
    got = _swap_halves(parts)
    sums = [_sum_cast(a, b) for a, b in zip(parts, got)]
    landed = _scatter_shards([hb for _, hb in sums])
    halves = [_sum4(f, g) for (f, _), g in zip(sums, landed)]
    r_in, r_proj, r_small = _join_halves(halves)
    grad_w_in = r_in.reshape(D, SHW)
    r_proj = r_proj.reshape(768, D)
    r_small = r_small.reshape(32, 768)

    gsmall = _allreduce_small(jnp.concatenate(
        [dlnpre8, dalog8, ddt8, dgdn_w, db2f8, db2b8, dgla_w, dlnpost], axis=1))

    d_in, nm_in, nv_in = _adam(w_in[0], grad_w_in, m_w_in[0], v_w_in[0], 128)
    stack3 = lambda a, b, c: jnp.concatenate([a[0], b[0], c[0]], axis=0)
    d_pr, nm_pr, nv_pr = _adam(stack3(w_proj_gdn, w_proj_gla, w_out), r_proj,
                               stack3(m_w_proj_gdn, m_w_proj_gla, m_w_out),
                               stack3(v_w_proj_gdn, v_w_proj_gla, v_w_out), 256)
    d_ss, nm_ss, nv_ss = _adam(_pack_shard_small(conv_w[0], gk_w2_fwd[0], gk_w2_bwd[0]), r_small,
                               _pack_shard_small(m_conv_w[0], m_gk_w2_fwd[0], m_gk_w2_bwd[0]),
                               _pack_shard_small(v_conv_w[0], v_gk_w2_fwd[0], v_gk_w2_bwd[0]), 32)
    smalls = dict(ln_pre_w=(ln_pre_w, m_ln_pre_w, v_ln_pre_w), a_log_fwd=(a_log_fwd, m_a_log_fwd, v_a_log_fwd),
                  a_log_bwd=(a_log_bwd, m_a_log_bwd, v_a_log_bwd),
                  dt_bias_fwd=(dt_bias_fwd, m_dt_bias_fwd, v_dt_bias_fwd),
                  dt_bias_bwd=(dt_bias_bwd, m_dt_bias_bwd, v_dt_bias_bwd),
                  gdn_norm_w=(gdn_norm_w, m_gdn_norm_w, v_gdn_norm_w),
                  gk_b2_fwd=(gk_b2_fwd, m_gk_b2_fwd, v_gk_b2_fwd), gk_b2_bwd=(gk_b2_bwd, m_gk_b2_bwd, v_gk_b2_bwd),
                  gla_norm_w=(gla_norm_w, m_gla_norm_w, v_gla_norm_w), ln_post_w=(ln_post_w, m_ln_post_w, v_ln_post_w))
    names = [name for name, _, _, _ in SMALL_SLOTS]
    small = _adam_small(gsmall, *([smalls[n][i] for n in names] for i in range(3)))

    def family(k, in_, pr, ss):
        conv, w2f_, w2b_ = _unpack_shard_small(ss)
        s = {n: small[n][k] for n in names}
        return [s["ln_pre_w"], in_[None], conv[None], s["a_log_fwd"], s["a_log_bwd"], s["dt_bias_fwd"],
                s["dt_bias_bwd"], s["gdn_norm_w"], pr[None, 0:256], w2f_[None], s["gk_b2_fwd"], w2b_[None],
                s["gk_b2_bwd"], s["gla_norm_w"], pr[None, 256:512], pr[None, 512:768], s["ln_post_w"]]

    loss = lax.psum(loss8[0, 0], ("x", "y", "c"))
    outs = [loss, dx[None]]
    outs += family(0, grad_w_in, r_proj, r_small)
    outs += family(1, d_in, d_pr, d_ss)
    outs += family(2, nm_in, nm_pr, nm_ss)
    outs += family(3, nv_in, nv_pr, nv_ss)
    return tuple(outs)
```

```python
import functools

import jax
import jax.numpy as jnp
from jax import lax
from jax.experimental import pallas as pl
from jax.experimental.pallas import tpu as pltpu

F32 = jnp.float32
BF16 = jnp.bfloat16
HI = lax.Precision.HIGHEST
MESH = pl.DeviceIdType.MESH

D = 1024
CH = 64
EPS = 1e-6
NA, DA = 8, 128
NB, DKB, DVB = 4, 128, 256
NSHARD = 4
SHW = 2320
NPERM = 9728
PS_BLOCK = 72
LR, B1, B2, ADAM_EPS, WD, STEP = 0.001, 0.9, 0.999, 1e-08, 0.01, 10

ANY = pl.BlockSpec(memory_space=pl.ANY)


def _cparams(sem=None, vmem_mb=48):
    return pltpu.CompilerParams(dimension_semantics=sem, vmem_limit_bytes=vmem_mb << 20)


def _bdot(a, b, ca, cb):
    return lax.dot_general(a.astype(BF16), b.astype(BF16), (((ca,), (cb,)), ((), ())),
                           preferred_element_type=F32)


@jax.custom_vjp
def mm(a, b):
    return _bdot(a, b, 1, 0)


def _mm_fwd(a, b):
    return _bdot(a, b, 1, 0), (a, b)


def _mm_bwd(res, g):
    a, b = res
    return _bdot(g, b, 1, 1), _bdot(a, g, 0, 0)


mm.defvjp(_mm_fwd, _mm_bwd)


@jax.custom_vjp
def mm_nt(a, b):
    return _bdot(a, b, 1, 1)


def _mm_nt_fwd(a, b):
    return _bdot(a, b, 1, 1), (a, b)


def _mm_nt_bwd(res, g):
    a, b = res
    return _bdot(g, b, 1, 0), _bdot(g, a, 0, 0)


mm_nt.defvjp(_mm_nt_fwd, _mm_nt_bwd)


@jax.custom_vjp
def mm_tn(a, b):
    return _bdot(a, b, 0, 0)


def _mm_tn_fwd(a, b):
    return _bdot(a, b, 0, 0), (a, b)


def _mm_tn_bwd(res, g):
    a, b = res
    return _bdot(b, g, 1, 1), _bdot(a, g, 1, 0)


mm_tn.defvjp(_mm_tn_fwd, _mm_tn_bwd)


def dot_hi(a, b):
    return lax.dot_general(a, b, (((1,), (0,)), ((), ())), precision=HI, preferred_element_type=F32)


def _split3(x):
    x1 = x.astype(BF16)
    r = x - x1.astype(F32)
    x2 = r.astype(BF16)
    return x1, x2, (r - x2.astype(F32)).astype(BF16)


def _cdot(c, x, cc, cx, c_first=True):
    parts = _split3(x)
    if c_first:
        return _bdot(c, parts[0], cc, cx) + _bdot(c, parts[1], cc, cx) + _bdot(c, parts[2], cc, cx)
    return _bdot(parts[0], c, cx, cc) + _bdot(parts[1], c, cx, cc) + _bdot(parts[2], c, cx, cc)


@jax.custom_vjp
def cmm(c, x):
    return _cdot(c, x, 1, 0)


def _cmm_fwd(c, x):
    return _cdot(c, x, 1, 0), c


def _cmm_bwd(c, g):
    return jnp.zeros_like(c), _cdot(c, g, 0, 0)


cmm.defvjp(_cmm_fwd, _cmm_bwd)


@jax.custom_vjp
def mmc(x, c):
    return _cdot(c, x, 0, 1, c_first=False)


def _mmc_fwd(x, c):
    return _cdot(c, x, 0, 1, c_first=False), c


def _mmc_bwd(c, g):
    return _cdot(c, g, 1, 1, c_first=False), jnp.zeros_like(c)


mmc.defvjp(_mmc_fwd, _mmc_bwd)


def _sigmoid(x):
    return 1.0 / (1.0 + jnp.exp(-x))


def _silu(x):
    return x * _sigmoid(x)


def _softplus(x):
    return jnp.maximum(x, 0.0) + jnp.log(1.0 + jnp.exp(-jnp.abs(x)))


def _rms(x, w):
    return x * lax.rsqrt(jnp.mean(x * x, axis=-1, keepdims=True) + EPS) * w


SC = 256


class _Consts:
    def __init__(self, rev):
        r = lax.broadcasted_iota(jnp.int32, (SC, SC), 0)
        c = lax.broadcasted_iota(jnp.int32, (SC, SC), 1)
        same = (r >> 6) == (c >> 6)
        a = jnp.where(rev, c, r)
        b = jnp.where(rev, r, c)
        self.incl = same & (a >= b)
        self.strict = same & (a > b)
        self.incl_f = self.incl.astype(F32)
        self.eye = (r == c).astype(F32)
        rows = lax.broadcasted_iota(jnp.int32, (SC, 1), 0)
        self.last_col = ((rows & (CH - 1)) == jnp.where(rev, 0, CH - 1)).astype(F32)
        rr = lax.broadcasted_iota(jnp.int32, (SC, CH), 0)
        cc = lax.broadcasted_iota(jnp.int32, (SC, CH), 1)
        self.fold = ((rr & (CH - 1)) == cc).astype(F32)


def _dot3(a, b, ca=1, cb=0):
    ah, bh = a.astype(BF16), b.astype(BF16)
    al, bl = (a - ah.astype(F32)).astype(BF16), (b - bh.astype(F32)).astype(BF16)
    return _bdot(ah, bh, ca, cb) + (_bdot(ah, bl, ca, cb) + _bdot(al, bh, ca, cb))


TRI_SPLIT_LEVELS = 2


def _tri_inv(low, eye):
    n = -low
    acc = eye + n
    p = n
    for level in range(5):
        dot = _dot3 if level < TRI_SPLIT_LEVELS else (lambda a, b: _bdot(a, b, 1, 0))
        p = dot(p, p)
        acc = acc + dot(acc, p)
    return acc


@jax.custom_vjp
def _solve2(low, rv, rk, tinv):
    x = _dot3(tinv, jnp.concatenate([rv, rk], axis=1))
    return x[:, :DA], x[:, DA:]


def _solve2_fwd(low, rv, rk, tinv):
    x = _dot3(tinv, jnp.concatenate([rv, rk], axis=1))
    return (x[:, :DA], x[:, DA:]), (x, tinv)


def _solve2_bwd(res, g):
    x, tinv = res
    drhs = _dot3(tinv, jnp.concatenate(g, axis=1), 0, 0)
    return -_dot3(drhs, x, 1, 1), drhs[:, :DA], drhs[:, DA:], jnp.zeros_like(tinv)


_solve2.defvjp(_solve2_fwd, _solve2_bwd)


def _chunk_last(x, cs):
    xs = (x * cs.last_col).reshape(SC // CH, CH, x.shape[1])
    return jnp.broadcast_to(jnp.sum(xs, axis=1, keepdims=True), xs.shape).reshape(x.shape)


def _gdn_decay(g, cs):
    gw = jnp.concatenate([g] * (SC // DA), axis=1)
    grow = jnp.sum(cs.eye * gw, axis=0, keepdims=True)
    return jnp.where(cs.incl, jnp.exp(jnp.where(cs.incl, gw - grow, 0.0)), 0.0)


def _gdn_intra(q, k, v, g, bx, tinv, cs):
    decay = _gdn_decay(g, cs)
    kb = k * bx
    low = jnp.where(cs.strict, mm_nt(kb, k) * decay, 0.0)
    eg = jnp.exp(g)
    made = tinv is None
    if made:
        tinv = _tri_inv(low, cs.eye)
    u, w = _solve2(low, v * bx, kb * eg, tinv)
    attn = mmc(mm_nt(q, k) * decay, cs.fold)
    qd = q * eg
    glast = _chunk_last(g, cs)
    kd = k * jnp.exp(glast - g)
    outs = (u, w, attn, qd, kd, jnp.exp(glast))
    return outs + (tinv,) if made else outs


def _gdn_scan(u, w, attn, qd, kd, el, s):
    vn = u - mm(w, s)
    o = mm(qd, s) + mm(attn, vn)
    sn = s * el + mm_tn(kd, vn)
    return o, sn


def _gla_intra(q, k, v, gk, cs):
    gc = cmm(cs.incl_f, gk)
    qg = q * (DKB ** -0.5) * jnp.exp(gc)
    kg = k * jnp.exp(-gc)
    attn = jnp.where(cs.incl, mm_nt(qg, kg), 0.0)
    intra = mm(attn, v)
    glast = _chunk_last(gc, cs)
    kd = k * jnp.exp(glast - gc)
    return qg, kd, intra, jnp.exp(glast)


def _gla_scan(qg, kd, v, el, st):
    o = mm_nt(qg, st)
    stn = st * el + mm_tn(v, kd)
    return o, stn


def _shift_rows(x, s):
    if s == 0:
        return x
    t = x.shape[0]
    rolled = pltpu.roll(x, (-s) % t, 0)
    rows = lax.broadcasted_iota(jnp.int32, x.shape, 0)
    return jnp.where((rows + s >= 0) & (rows + s < t), rolled, 0.0)


@jax.custom_vjp
def _conv5(x, w):
    acc = w[0:1] * _shift_rows(x, -2)
    for j in range(1, 5):
        acc = acc + w[j:j + 1] * _shift_rows(x, j - 2)
    return acc


def _conv5_fwd(x, w):
    return _conv5(x, w), (x, w)


def _conv5_bwd(res, g):
    x, w = res
    dx = w[0:1] * _shift_rows(g, 2)
    for j in range(1, 5):
        dx = dx + w[j:j + 1] * _shift_rows(g, 2 - j)
    rows = lax.broadcasted_iota(jnp.int32, w.shape, 0)
    dw = jnp.zeros_like(w)
    for j in range(5):
        dwj = jnp.sum(g * _shift_rows(x, j - 2), axis=0, keepdims=True)
        dw = dw + jnp.where(rows == j, dwj, 0.0)
    return dx, dw


_conv5.defvjp(_conv5_fwd, _conv5_bwd)


def _qkv_act(kind):
    def f(x, w):
        c = _silu(_conv5(x, w))
        if kind == 2:
            return c
        c = c * lax.rsqrt(jnp.sum(c * c, axis=-1, keepdims=True) + EPS)
        return c * (DA ** -0.5) if kind == 0 else c
    return f


def _inproj(x, lnw, wperm, tm=512, tn=512):
    t = x.shape[0]

    def body(x_ref, lnw_ref, w_ref, p_ref, h_ref, hbuf):
        @pl.when(pl.program_id(1) == 0)
        def _():
            hb = _rms(x_ref[...], lnw_ref[...]).astype(BF16)
            hbuf[...] = hb
            h_ref[...] = hb
        p_ref[...] = jnp.dot(hbuf[...], w_ref[...], preferred_element_type=F32)

    return pl.pallas_call(
        body, name="inproj", grid=(t // tm, NPERM // tn),
        in_specs=[pl.BlockSpec((tm, D), lambda i, j: (i, 0)),
                  pl.BlockSpec((1, D), lambda i, j: (0, 0)),
                  pl.BlockSpec((D, tn), lambda i, j: (0, j))],
        out_specs=[pl.BlockSpec((tm, tn), lambda i, j: (i, j)),
                   pl.BlockSpec((tm, D), lambda i, j: (i, 0))],
        out_shape=[jax.ShapeDtypeStruct((t, NPERM), F32),
                   jax.ShapeDtypeStruct((t, D), BF16)],
        scratch_shapes=[pltpu.VMEM((tm, D), BF16)],
        compiler_params=_cparams(("parallel", "arbitrary")),
    )(x, lnw, wperm)


DP_TILE = 512
DP_PIECES = ((0, 2), (2, 2), (4, 2), (6, 2), (8, 1), (9, 1), (10, 2), (12, 2), (14, 2), (16, 2), (18, 1))


def _piece_specs(tm, j_first):
    specs = []
    for j0, n in DP_PIECES:
        def imap(a, b, j0=j0, n=n):
            j, i = (a, b) if j_first else (b, a)
            inside = (j >= j0) & (j < j0 + n)
            return jnp.where(inside, i, 0), jnp.clip(j - j0, 0, n - 1)
        specs.append(pl.BlockSpec((tm, DP_TILE), imap))
    return specs


def _for_piece(j, refs, fn):
    for (j0, n), ref in zip(DP_PIECES, refs):
        @pl.when((j >= j0) & (j < j0 + n))
        def _(ref=ref):
            fn(ref[...])


def _inproj_dw(h, pieces, tm=512):
    t = h.shape[0]
    npc = len(pieces)

    def body(h_ref, *refs):
        dw_ref = refs[npc]

        @pl.when(pl.program_id(1) == 0)
        def _():
            dw_ref[...] = jnp.zeros_like(dw_ref)

        def add(dp):
            dw_ref[...] += _bdot(h_ref[...], dp, 0, 0)
        _for_piece(pl.program_id(0), refs[:npc], add)

    return pl.pallas_call(
        body, name="inproj_dw", grid=(NPERM // DP_TILE, t // tm),
        in_specs=[pl.BlockSpec((tm, D), lambda j, i: (i, 0))] + _piece_specs(tm, True),
        out_specs=pl.BlockSpec((D, DP_TILE), lambda j, i: (0, j)),
        out_shape=jax.ShapeDtypeStruct((D, NPERM), F32),
        compiler_params=_cparams(("parallel", "arbitrary")),
    )(h, *pieces)


def _inproj_dx(pieces, wperm, x, lnw, dyres, tm=512):
    t = x.shape[0]
    tn = DP_TILE
    nj = NPERM // tn
    npc = len(pieces)

    def body(*refs):
        w_ref, x_ref, lnw_ref, dy_ref, dx_ref, dlnw_ref, acc = refs[npc:]
        j = pl.program_id(1)

        @pl.when(j == 0)
        def _():
            acc[...] = jnp.zeros_like(acc)

        def add(dp):
            acc[...] += _bdot(dp, w_ref[...], 1, 1)
        _for_piece(j, refs[:npc], add)

        @pl.when(j == nj - 1)
        def _():
            _, vjp = jax.vjp(_rms, x_ref[...], lnw_ref[...])
            dx, dlnw = vjp(acc[...])
            dx_ref[...] = dx + dy_ref[...]

            @pl.when(pl.program_id(0) == 0)
            def _():
                dlnw_ref[...] = jnp.zeros_like(dlnw_ref)
            dlnw_ref[...] += jnp.broadcast_to(dlnw, dlnw_ref.shape)

    return pl.pallas_call(
        body, name="inproj_dx", grid=(t // tm, nj),
        in_specs=_piece_specs(tm, False) + [
                  pl.BlockSpec((D, tn), lambda i, j: (0, j)),
                  pl.BlockSpec((tm, D), lambda i, j: (i, 0)),
                  pl.BlockSpec((1, D), lambda i, j: (0, 0)),
                  pl.BlockSpec((tm, D), lambda i, j: (i, 0))],
        out_specs=[pl.BlockSpec((tm, D), lambda i, j: (i, 0)),
                   pl.BlockSpec((8, D), lambda i, j: (0, 0))],
        out_shape=[jax.ShapeDtypeStruct((t, D), F32), jax.ShapeDtypeStruct((8, D), F32)],
        scratch_shapes=[pltpu.VMEM((tm, D), F32)],
        compiler_params=_cparams(("arbitrary", "arbitrary")),
    )(*pieces, wperm, x, lnw, dyres)


def _qkv_fwd(p, convw, kind):
    t = p.shape[0]
    f = _qkv_act(kind)

    def body(p_ref, w_ref, o_ref):
        o_ref[...] = f(p_ref[...], w_ref[...])

    return pl.pallas_call(
        body, name=f"qkv_fwd{kind}", grid=(NA,),
        in_specs=[pl.BlockSpec((t, DA), lambda h: (0, kind * NA + h)),
                  pl.BlockSpec((8, DA), lambda h: (0, kind * NA + h))],
        out_specs=pl.BlockSpec((t, DA), lambda h: (0, h)),
        out_shape=jax.ShapeDtypeStruct((t, D), F32),
        compiler_params=_cparams(("parallel",)),
    )(p, convw)


def _qkv_bwd(p, convw, dout, kind):
    t = p.shape[0]
    f = _qkv_act(kind)

    def body(p_ref, w_ref, g_ref, dx_ref, dw_ref):
        _, vjp = jax.vjp(f, p_ref[...], w_ref[...])
        dx, dw = vjp(g_ref[...])
        dx_ref[...] = dx.astype(BF16)
        dw_ref[...] = dw

    return pl.pallas_call(
        body, name=f"qkv_bwd{kind}", grid=(NA,),
        in_specs=[pl.BlockSpec((t, DA), lambda h: (0, kind * NA + h)),
                  pl.BlockSpec((8, DA), lambda h: (0, kind * NA + h)),
                  pl.BlockSpec((t, DA), lambda h: (0, h))],
        out_specs=[pl.BlockSpec((t, DA), lambda h: (0, h)),
                   pl.BlockSpec((8, DA), lambda h: (0, h))],
        out_shape=[jax.ShapeDtypeStruct((t, D), BF16), jax.ShapeDtypeStruct((8, D), F32)],
        compiler_params=_cparams(("parallel",)),
    )(p, convw, dout)


def _gates_f(ps, alog_row, dt_row, w2f, b2f, w2b, b2b):
    lane = lax.broadcasted_iota(jnp.int32, ps.shape, 1)
    lg = -jnp.exp(alog_row) * _softplus(ps + dt_row)
    gsm = jnp.where(lane < 16, lg, jnp.where(lane < 32, _sigmoid(ps), 0.0))
    gkf = -_softplus(-(mm(ps, w2f) + b2f)) * (1.0 / 16.0)
    gkb = -_softplus(-(mm(ps, w2b) + b2b)) * (1.0 / 16.0)
    return gsm, gkf, gkb


def _gates_fwd(ps, alog_row, dt_row, w2f, b2f, w2b, b2b, tm=512):
    t = ps.shape[0]

    def body(ps_ref, a_ref, d_ref, wf_ref, bf_ref, wb_ref, bb_ref, gsm_ref, gk_ref):
        gsm, gkf, gkb = _gates_f(ps_ref[...], a_ref[...], d_ref[...], wf_ref[...], bf_ref[...],
                                 wb_ref[...], bb_ref[...])
        gsm_ref[...] = gsm
        gk_ref[0] = gkf
        gk_ref[1] = gkb

    row = lambda n: pl.BlockSpec((1, n), lambda i: (0, 0))
    mat = pl.BlockSpec((128, 512), lambda i: (0, 0))
    return pl.pallas_call(
        body, name="gates_fwd", grid=(t // tm,),
        in_specs=[pl.BlockSpec((tm, 128), lambda i: (i, PS_BLOCK)), row(128), row(128), mat, row(512), mat, row(512)],
        out_specs=[pl.BlockSpec((tm, 128), lambda i: (i, 0)),
                   pl.BlockSpec((2, tm, 512), lambda i: (0, i, 0))],
        out_shape=[jax.ShapeDtypeStruct((t, 128), F32), jax.ShapeDtypeStruct((2, t, 512), F32)],
        compiler_params=_cparams(("parallel",)),
    )(ps, alog_row, dt_row, w2f, b2f, w2b, b2b)


def _gates_bwd(ps, alog_row, dt_row, w2f, b2f, w2b, b2b, dgsm, dgk, tm=512):
    t = ps.shape[0]

    def body(ps_ref, a_ref, d_ref, wf_ref, bf_ref, wb_ref, bb_ref, dgsm_ref, dgk_ref,
             dps_ref, da_ref, dd_ref, dwf_ref, dbf_ref, dwb_ref, dbb_ref):
        _, vjp = jax.vjp(_gates_f, ps_ref[...], a_ref[...], d_ref[...], wf_ref[...], bf_ref[...],
                         wb_ref[...], bb_ref[...])
        dps, da, dd, dwf, dbf, dwb, dbb = vjp((dgsm_ref[...], dgk_ref[0], dgk_ref[1]))
        dps_ref[:, 0:128] = dps.astype(BF16)
        dps_ref[:, 128:DP_TILE] = jnp.zeros((tm, DP_TILE - 128), BF16)
        accs = ((da_ref, da), (dd_ref, dd), (dwf_ref, dwf), (dbf_ref, dbf), (dwb_ref, dwb), (dbb_ref, dbb))

        @pl.when(pl.program_id(0) == 0)
        def _():
            for ref, _ in accs:
                ref[...] = jnp.zeros_like(ref)
        for ref, val in accs:
            ref[...] += jnp.broadcast_to(val, ref.shape)

    row = lambda n: pl.BlockSpec((1, n), lambda i: (0, 0))
    row8 = lambda n: pl.BlockSpec((8, n), lambda i: (0, 0))
    mat = pl.BlockSpec((128, 512), lambda i: (0, 0))
    return pl.pallas_call(
        body, name="gates_bwd", grid=(t // tm,),
        in_specs=[pl.BlockSpec((tm, 128), lambda i: (i, PS_BLOCK)), row(128), row(128), mat, row(512), mat, row(512),
                  pl.BlockSpec((tm, 128), lambda i: (i, 0)),
                  pl.BlockSpec((2, tm, 512), lambda i: (0, i, 0))],
        out_specs=[pl.BlockSpec((tm, DP_TILE), lambda i: (i, 0)), row8(128), row8(128), mat, row8(512), mat,
                   row8(512)],
        out_shape=[jax.ShapeDtypeStruct((t, DP_TILE), BF16),
                   jax.ShapeDtypeStruct((8, 128), F32), jax.ShapeDtypeStruct((8, 128), F32),
                   jax.ShapeDtypeStruct((128, 512), F32), jax.ShapeDtypeStruct((8, 512), F32),
                   jax.ShapeDtypeStruct((128, 512), F32), jax.ShapeDtypeStruct((8, 512), F32)],
        compiler_params=_cparams(("arbitrary",)),
    )(ps, alog_row, dt_row, w2f, b2f, w2b, b2b, dgsm, dgk)


def _rows(i):
    return pl.ds(pl.multiple_of(i * CH, CH), CH)


def _srows(i):
    return pl.ds(pl.multiple_of(i * SC, SC), SC)


def _first_row(x):
    row = lax.broadcasted_iota(jnp.int32, (8, x.shape[1]), 0)
    return jnp.where(row == 0, jnp.broadcast_to(x, (8, x.shape[1])), 0.0)


def _chunk_rows(e_ref, i):
    pad = jnp.zeros((CH - 8, 128), F32)
    return jnp.concatenate([x for c in range(SC // CH) for x in (e_ref[(SC // CH) * i + c], pad)], axis=0)


def _gcum_f(gsm, tm):
    i = lax.broadcasted_iota(jnp.int32, (tm, tm), 0)
    j = lax.broadcasted_iota(jnp.int32, (tm, tm), 1)
    same = (i >> 6) == (j >> 6)
    lower = (same & (i >= j)).astype(F32)
    upper = (same & (i <= j)).astype(F32)
    r = lax.broadcasted_iota(jnp.int32, (128, D), 0)
    head = lax.broadcasted_iota(jnp.int32, (128, D), 1) >> 7
    pick = lambda off: (r == head + off).astype(F32)
    lane = lax.broadcasted_iota(jnp.int32, gsm.shape, 1)
    run = jnp.where(lane < 8, cmm(lower, gsm), cmm(upper, gsm))
    return mmc(run, pick(0)), mmc(run, pick(8)), mmc(gsm, pick(16)), mmc(gsm, pick(24))


def _gcum_fwd(gsm, tm=256):
    t = gsm.shape[0]

    def body(s_ref, g_ref, b_ref):
        gf, gb, bf, bb = _gcum_f(s_ref[...], tm)
        g_ref[0] = gf
        g_ref[1] = gb
        b_ref[0] = bf
        b_ref[1] = bb

    two = pl.BlockSpec((2, tm, D), lambda i: (0, i, 0))
    return pl.pallas_call(
        body, name="gcum_fwd", grid=(t // tm,),
        in_specs=[pl.BlockSpec((tm, 128), lambda i: (i, 0))], out_specs=[two, two],
        out_shape=[jax.ShapeDtypeStruct((2, t, D), F32)] * 2,
        compiler_params=_cparams(("parallel",)),
    )(gsm)


def _gcum_bwd(gsm, dg2, db2, tm=256):
    t = gsm.shape[0]

    def body(s_ref, dg_ref, db_ref, ds_ref):
        _, vjp = jax.vjp(lambda s: _gcum_f(s, tm), s_ref[...])
        ds_ref[...] = vjp((dg_ref[0], dg_ref[1], db_ref[0], db_ref[1]))[0]

    two = pl.BlockSpec((2, tm, D), lambda i: (0, i, 0))
    tile = pl.BlockSpec((tm, 128), lambda i: (i, 0))
    return pl.pallas_call(
        body, name="gcum_bwd", grid=(t // tm,),
        in_specs=[tile, two, two], out_specs=tile,
        out_shape=jax.ShapeDtypeStruct((t, 128), F32),
        compiler_params=_cparams(("parallel",)),
    )(gsm, dg2, db2)


def _gdn_intra_fwd(qn, kn, vc, g2, b2):
    t = qn.shape[0]
    n = t // CH

    def body(q_ref, k_ref, v_ref, g_ref, b_ref, u_ref, w_ref, a_ref, qd_ref, kd_ref, e_ref, t_ref):
        cs = _Consts(pl.program_id(0) == 1)

        def step(i, carry):
            r = _srows(i)
            q, k, v, g, bx = q_ref[r, :], k_ref[r, :], v_ref[r, :], g_ref[r, :], b_ref[r, :]
            u, w, a, qd, kd, el, tinv = _gdn_intra(q, k, v, g, bx, None, cs)
            u_ref[r, :] = u
            w_ref[r, :] = w
            a_ref[r, :] = a
            qd_ref[r, :] = qd
            kd_ref[r, :] = kd
            t_ref[r, :] = tinv
            for c in range(SC // CH):
                e_ref[(SC // CH) * i + c] = el[c * CH:c * CH + 8]
            return carry

        lax.fori_loop(0, t // SC, step, 0)

    head = pl.BlockSpec((t, DA), lambda d, h: (0, h))
    dh = pl.BlockSpec((None, t, DA), lambda d, h: (d, 0, h))
    sq = lambda w: pl.BlockSpec((None, None, t, w), lambda d, h: (d, h, 0, 0))
    big = jax.ShapeDtypeStruct((2, t, D), F32)
    return pl.pallas_call(
        body, name="gdn_intra_fwd", grid=(2, NA),
        in_specs=[head, head, head, dh, dh],
        out_specs=[dh, dh, sq(CH), dh, dh, pl.BlockSpec((None, None, n, 8, 128), lambda d, h: (d, h, 0, 0, 0)),
                   sq(SC)],
        out_shape=[big, big, jax.ShapeDtypeStruct((2, NA, t, CH), F32), big, big,
                   jax.ShapeDtypeStruct((2, NA, n, 8, 128), F32), jax.ShapeDtypeStruct((2, NA, t, SC), F32)],
        compiler_params=_cparams(("parallel", "parallel")),
    )(qn, kn, vc, g2, b2)


SCAN_TB = 256
SCAN_HB = 8


def _scan_specs(t, width, nheads, hb, along):
    nt = t // SCAN_TB
    nb = SCAN_TB // CH

    def tmap(d, tt):
        fwd = tt + d * (nt - 1 - 2 * tt)
        return fwd if along > 0 else nt - 1 - fwd

    tok = pl.BlockSpec((None, SCAN_TB, hb * width), lambda d, h, tt: (d, tmap(d, tt), h))
    per = lambda *tail: pl.BlockSpec((None, hb, nb) + tail, lambda d, h, tt: (d, h, tmap(d, tt)) + (0,) * len(tail))
    sq = pl.BlockSpec((None, hb, SCAN_TB, CH), lambda d, h, tt: (d, h, tmap(d, tt), 0))
    shared = lambda w: pl.BlockSpec((SCAN_TB, hb * w), lambda d, h, tt: (tmap(d, tt), h))
    return tok, per, sq, shared, (2, nheads // hb, nt), nb


def _gdn_scan_fwd(u, w, a, qd, kd, e):
    t = u.shape[1]
    tok, per, sq, _, grid, nb = _scan_specs(t, DA, NA, SCAN_HB, +1)

    def body(u_ref, w_ref, a_ref, qd_ref, kd_ref, e_ref, o_ref, s_ref, state):
        rev = pl.program_id(0) == 1

        @pl.when(pl.program_id(2) == 0)
        def _():
            state[...] = jnp.zeros_like(state)

        def step(i, ss):
            ci = jnp.where(rev, nb - 1 - i, i)
            r = _rows(ci)
            out = []
            for hh, s in enumerate(ss):
                c = slice(hh * DA, (hh + 1) * DA)
                s_ref[hh, ci] = s
                o, sn = _gdn_scan(u_ref[r, c], w_ref[r, c], a_ref[hh, r, :], qd_ref[r, c], kd_ref[r, c],
                                  e_ref[hh, ci][0:1], s)
                o_ref[r, c] = o
                out.append(sn)
            return tuple(out)

        ss = lax.fori_loop(0, nb, step, tuple(state[hh] for hh in range(SCAN_HB)))
        for hh, s in enumerate(ss):
            state[hh] = s

    return pl.pallas_call(
        body, name="gdn_scan_fwd", grid=grid,
        in_specs=[tok, tok, sq, tok, tok, per(8, 128)],
        out_specs=[tok, per(DA, DA)],
        out_shape=[jax.ShapeDtypeStruct((2, t, D), F32), jax.ShapeDtypeStruct((2, NA, t // CH, DA, DA), F32)],
        scratch_shapes=[pltpu.VMEM((SCAN_HB, DA, DA), F32)],
        compiler_params=_cparams(("parallel", "parallel", "arbitrary")),
    )(u, w, a, qd, kd, e)


def _gdn_scan_bwd(u, w, a, qd, kd, e, ssave, do):
    t = u.shape[1]
    tok, per, sq, shared, grid, nb = _scan_specs(t, DA, NA, SCAN_HB, -1)

    def body(u_ref, w_ref, a_ref, qd_ref, kd_ref, e_ref, s_ref, do_ref,
             du_ref, dw_ref, da_ref, dqd_ref, dkd_ref, de_ref, state):
        rev = pl.program_id(0) == 1

        @pl.when(pl.program_id(2) == 0)
        def _():
            state[...] = jnp.zeros_like(state)

        def step(i, dss):
            ci = jnp.where(rev, i, nb - 1 - i)
            r = _rows(ci)
            out = []
            for hh, ds in enumerate(dss):
                c = slice(hh * DA, (hh + 1) * DA)
                _, vjp = jax.vjp(_gdn_scan, u_ref[r, c], w_ref[r, c], a_ref[hh, r, :], qd_ref[r, c], kd_ref[r, c],
                                 e_ref[hh, ci][0:1], s_ref[hh, ci])
                du, dw, da, dqd, dkd, de, dsn = vjp((do_ref[r, c], ds))
                du_ref[r, c] = du
                dw_ref[r, c] = dw
                da_ref[hh, r, :] = da
                dqd_ref[r, c] = dqd
                dkd_ref[r, c] = dkd
                de_ref[hh, ci] = _first_row(de)
                out.append(dsn)
            return tuple(out)

        dss = lax.fori_loop(0, nb, step, tuple(state[hh] for hh in range(SCAN_HB)))
        for hh, ds in enumerate(dss):
            state[hh] = ds

    big = jax.ShapeDtypeStruct((2, t, D), F32)
    return pl.pallas_call(
        body, name="gdn_scan_bwd", grid=grid,
        in_specs=[tok, tok, sq, tok, tok, per(8, 128), per(DA, DA), shared(DA)],
        out_specs=[tok, tok, sq, tok, tok, per(8, 128)],
        out_shape=[big, big, jax.ShapeDtypeStruct((2, NA, t, CH), F32), big, big,
                   jax.ShapeDtypeStruct((2, NA, t // CH, 8, 128), F32)],
        scratch_shapes=[pltpu.VMEM((SCAN_HB, DA, DA), F32)],
        compiler_params=_cparams(("parallel", "parallel", "arbitrary")),
    )(u, w, a, qd, kd, e, ssave, do)


def _gdn_intra_bwd(qn, kn, vc, g2, b2, tinv, du, dw, da, dqd, dkd, de):
    t = qn.shape[0]
    n = t // CH

    def body(q_ref, k_ref, v_ref, g_ref, b_ref, t_ref, du_ref, dw_ref, da_ref, dqd_ref, dkd_ref, de_ref,
             dq_ref, dk_ref, dv_ref, dg_ref, db_ref):
        d = pl.program_id(1)
        cs = _Consts(d == 1)

        @pl.when(d == 0)
        def _():
            dq_ref[...] = jnp.zeros_like(dq_ref)
            dk_ref[...] = jnp.zeros_like(dk_ref)
            dv_ref[...] = jnp.zeros_like(dv_ref)

        def step(i, carry):
            r = _srows(i)
            tinv_c = t_ref[r, :]
            f = lambda q, k, v, g, bx: _gdn_intra(q, k, v, g, bx, tinv_c, cs)
            _, vjp = jax.vjp(f, q_ref[r, :], k_ref[r, :], v_ref[r, :], g_ref[r, :], b_ref[r, :])
            dq, dk, dv, dg, dbx = vjp((du_ref[r, :], dw_ref[r, :], da_ref[r, :], dqd_ref[r, :],
                                       dkd_ref[r, :], _chunk_rows(de_ref, i)))
            dq_ref[r, :] += dq
            dk_ref[r, :] += dk
            dv_ref[r, :] += dv
            dg_ref[r, :] = dg
            db_ref[r, :] = dbx
            return carry

        lax.fori_loop(0, t // SC, step, 0)

    head = pl.BlockSpec((t, DA), lambda h, d: (0, h))
    dh = pl.BlockSpec((None, t, DA), lambda h, d: (d, 0, h))
    sq = pl.BlockSpec((None, None, t, CH), lambda h, d: (d, h, 0, 0))
    tq = pl.BlockSpec((None, None, t, SC), lambda h, d: (d, h, 0, 0))
    full = jax.ShapeDtypeStruct((t, D), F32)
    big = jax.ShapeDtypeStruct((2, t, D), F32)
    return pl.pallas_call(
        body, name="gdn_intra_bwd", grid=(NA, 2),
        in_specs=[head, head, head, dh, dh, tq, dh, dh, sq, dh, dh,
                  pl.BlockSpec((None, None, n, 8, 128), lambda h, d: (d, h, 0, 0, 0))],
        out_specs=[head, head, head, dh, dh],
        out_shape=[full, full, full, big, big],
        compiler_params=_cparams(("arbitrary", "arbitrary")),
    )(qn, kn, vc, g2, b2, tinv, du, dw, da, dqd, dkd, de)


def _gla_specs(t, order):
    ix = (lambda d, h: (d, h)) if order == "dh" else (lambda h, d: (d, h))

    def mk(fn):
        return lambda a, b: fn(*ix(a, b))
    q = pl.BlockSpec((t, DKB), mk(lambda d, h: (0, 32 + h)))
    k = pl.BlockSpec((t, DKB), mk(lambda d, h: (0, 36 + h)))
    v = pl.BlockSpec((t, DVB), mk(lambda d, h: (0, 20 + h)))
    dk = pl.BlockSpec((None, t, DKB), mk(lambda d, h: (d, 0, h)))
    dv = pl.BlockSpec((None, t, DVB), mk(lambda d, h: (d, 0, h)))
    e = pl.BlockSpec((None, None, t // CH, 8, 128), mk(lambda d, h: (d, h, 0, 0, 0)))
    s = pl.BlockSpec((None, None, t // CH, DVB, DKB), mk(lambda d, h: (d, h, 0, 0, 0)))
    return q, k, v, dk, dv, e, s


def _gla_intra_fwd(p, gk):
    t = p.shape[0]
    n = t // CH

    def body(q_ref, k_ref, v_ref, g_ref, qg_ref, kd_ref, in_ref, e_ref):
        cs = _Consts(pl.program_id(0) == 1)

        def step(i, carry):
            r = _srows(i)
            qg, kd, intra, el = _gla_intra(q_ref[r, :], k_ref[r, :], v_ref[r, :], g_ref[r, :], cs)
            qg_ref[r, :] = qg
            kd_ref[r, :] = kd
            in_ref[r, :] = intra
            for c in range(SC // CH):
                e_ref[(SC // CH) * i + c] = el[c * CH:c * CH + 8]
            return carry

        lax.fori_loop(0, t // SC, step, 0)

    q, k, v, dk, dv, e, _ = _gla_specs(t, "dh")
    return pl.pallas_call(
        body, name="gla_intra_fwd", grid=(2, NB),
        in_specs=[q, k, v, dk], out_specs=[dk, dk, dv, e],
        out_shape=[jax.ShapeDtypeStruct((2, t, NB * DKB), F32), jax.ShapeDtypeStruct((2, t, NB * DKB), F32),
                   jax.ShapeDtypeStruct((2, t, D), F32), jax.ShapeDtypeStruct((2, NB, n, 8, 128), F32)],
        compiler_params=_cparams(("parallel", "parallel")),
    )(p, p, p, gk)


GLA_HB = 2


def _gla_v_spec(t, along):
    nt = t // SCAN_TB

    def tmap(d, tt):
        fwd = tt + d * (nt - 1 - 2 * tt)
        return fwd if along > 0 else nt - 1 - fwd

    return pl.BlockSpec((SCAN_TB, GLA_HB * DVB), lambda d, h, tt: (tmap(d, tt), 5120 // (GLA_HB * DVB) + h))


def _gla_scan_fwd(p, qg, kd, intra, e):
    t = p.shape[0]
    tokk, per, _, _, grid, nb = _scan_specs(t, DKB, NB, GLA_HB, +1)
    tokv = _scan_specs(t, DVB, NB, GLA_HB, +1)[0]

    def body(v_ref, qg_ref, kd_ref, in_ref, e_ref, o_ref, s_ref, state):
        rev = pl.program_id(0) == 1

        @pl.when(pl.program_id(2) == 0)
        def _():
            state[...] = jnp.zeros_like(state)

        def step(i, sts):
            ci = jnp.where(rev, nb - 1 - i, i)
            r = _rows(ci)
            out = []
            for hh, st in enumerate(sts):
                ck = slice(hh * DKB, (hh + 1) * DKB)
                cv = slice(hh * DVB, (hh + 1) * DVB)
                s_ref[hh, ci] = st
                o, stn = _gla_scan(qg_ref[r, ck], kd_ref[r, ck], v_ref[r, cv], e_ref[hh, ci][0:1], st)
                o_ref[r, cv] = o + in_ref[r, cv]
                out.append(stn)
            return tuple(out)

        sts = lax.fori_loop(0, nb, step, tuple(state[hh] for hh in range(GLA_HB)))
        for hh, st in enumerate(sts):
            state[hh] = st

    return pl.pallas_call(
        body, name="gla_scan_fwd", grid=grid,
        in_specs=[_gla_v_spec(t, +1), tokk, tokk, tokv, per(8, 128)], out_specs=[tokv, per(DVB, DKB)],
        out_shape=[jax.ShapeDtypeStruct((2, t, D), F32), jax.ShapeDtypeStruct((2, NB, t // CH, DVB, DKB), F32)],
        scratch_shapes=[pltpu.VMEM((GLA_HB, DVB, DKB), F32)],
        compiler_params=_cparams(("parallel", "parallel", "arbitrary")),
    )(p, qg, kd, intra, e)


def _gla_scan_bwd(p, qg, kd, e, ssave, do):
    t = p.shape[0]
    tokk, per, _, shared, grid, nb = _scan_specs(t, DKB, NB, GLA_HB, -1)
    tokv = _scan_specs(t, DVB, NB, GLA_HB, -1)[0]

    def body(v_ref, qg_ref, kd_ref, e_ref, s_ref, do_ref, dqg_ref, dkd_ref, dv_ref, de_ref, state):
        rev = pl.program_id(0) == 1

        @pl.when(pl.program_id(2) == 0)
        def _():
            state[...] = jnp.zeros_like(state)

        def step(i, dsts):
            ci = jnp.where(rev, i, nb - 1 - i)
            r = _rows(ci)
            out = []
            for hh, dst in enumerate(dsts):
                ck = slice(hh * DKB, (hh + 1) * DKB)
                cv = slice(hh * DVB, (hh + 1) * DVB)
                _, vjp = jax.vjp(_gla_scan, qg_ref[r, ck], kd_ref[r, ck], v_ref[r, cv], e_ref[hh, ci][0:1],
                                 s_ref[hh, ci])
                dqg, dkd, dv, de, dstn = vjp((do_ref[r, cv], dst))
                dqg_ref[r, ck] = dqg
                dkd_ref[r, ck] = dkd
                dv_ref[r, cv] = dv
                de_ref[hh, ci] = _first_row(de)
                out.append(dstn)
            return tuple(out)

        dsts = lax.fori_loop(0, nb, step, tuple(state[hh] for hh in range(GLA_HB)))
        for hh, dst in enumerate(dsts):
            state[hh] = dst

    return pl.pallas_call(
        body, name="gla_scan_bwd", grid=grid,
        in_specs=[_gla_v_spec(t, -1), tokk, tokk, per(8, 128), per(DVB, DKB), shared(DVB)],
        out_specs=[tokk, tokk, tokv, per(8, 128)],
        out_shape=[jax.ShapeDtypeStruct((2, t, NB * DKB), F32), jax.ShapeDtypeStruct((2, t, NB * DKB), F32),
                   jax.ShapeDtypeStruct((2, t, D), F32), jax.ShapeDtypeStruct((2, NB, t // CH, 8, 128), F32)],
        scratch_shapes=[pltpu.VMEM((GLA_HB, DVB, DKB), F32)],
        compiler_params=_cparams(("parallel", "parallel", "arbitrary")),
    )(p, qg, kd, e, ssave, do)


def _gla_intra_bwd(p, gk, dqg, dkd, dvs, de, do):
    t = p.shape[0]
    n = t // CH

    def body(q_ref, k_ref, v_ref, g_ref, dqg_ref, dkd_ref, dvs_ref, de_ref, do_ref,
             dq_ref, dk_ref, dv_ref, dg_ref):
        d = pl.program_id(1)
        cs = _Consts(d == 1)

        @pl.when(d == 0)
        def _():
            dq_ref[...] = jnp.zeros_like(dq_ref)
            dk_ref[...] = jnp.zeros_like(dk_ref)
            dv_ref[...] = jnp.zeros_like(dv_ref)

        def step(i, carry):
            r = _srows(i)
            f = lambda q, k, v, g: _gla_intra(q, k, v, g, cs)
            _, vjp = jax.vjp(f, q_ref[r, :], k_ref[r, :], v_ref[r, :], g_ref[r, :])
            dq, dk, dv, dg = vjp((dqg_ref[r, :], dkd_ref[r, :], do_ref[r, :], _chunk_rows(de_ref, i)))
            dq_ref[r, :] += dq
            dk_ref[r, :] += dk
            dv_ref[r, :] += dv + dvs_ref[r, :]
            dg_ref[r, :] = dg
            return carry

        lax.fori_loop(0, t // SC, step, 0)

    q, k, v, dk, dv, e_s, _ = _gla_specs(t, "hd")
    hk = pl.BlockSpec((t, DKB), lambda h, d: (0, h))
    hv = pl.BlockSpec((t, DVB), lambda h, d: (0, h))
    return pl.pallas_call(
        body, name="gla_intra_bwd", grid=(NB, 2),
        in_specs=[q, k, v, dk, dk, dk, dv, e_s, hv],
        out_specs=[hk, hk, hv, dk],
        out_shape=[jax.ShapeDtypeStruct((t, NB * DKB), F32), jax.ShapeDtypeStruct((t, NB * DKB), F32),
                   jax.ShapeDtypeStruct((t, D), F32), jax.ShapeDtypeStruct((2, t, NB * DKB), F32)],
        compiler_params=_cparams(("arbitrary", "arbitrary")),
    )(p, p, p, gk, dqg, dkd, dvs, de, do)


def _seg_gate(o, z, w):
    return _rms(o, w) * _silu(z)


def _seg_merge(ya, yb, ga, gb):
    return _sigmoid(ga) * ya + _sigmoid(gb) * yb


def _seg_loss(out, x, tgt, w):
    err = x + _rms(out, w) - tgt
    return 0.5 * jnp.sum(jnp.mean(err * err, axis=-1, keepdims=True), axis=0, keepdims=True)


def _post(oa2, ob2, p, x, tgt, gdn_w, gla_w, lnpost, w3, tm=128):
    t = x.shape[0]

    def body(oa_ref, ob_ref, z_ref, gb_ref, ga_ref, gB_ref, x_ref, t_ref, aw_ref, bw_ref, lw_ref, w_ref,
             loss_ref, doa_ref, dob_ref, dz_ref, dgb_ref, dga_ref, dgB_ref, dy_ref,
             dw_ref, daw_ref, dbw_ref, dlw_ref):
        first = pl.program_id(0) == 0
        oa = oa_ref[0] + oa_ref[1]
        ob = ob_ref[0] + ob_ref[1]
        z, gb = z_ref[...], gb_ref[...]
        aw, bw = aw_ref[...], bw_ref[...]
        rs = D // NSHARD

        def mat(a, m):
            return sum(jnp.dot(a[:, s * rs:(s + 1) * rs], w_ref[s, m], preferred_element_type=F32)
                       for s in range(NSHARD))

        def mat_t(g, m):
            return jnp.concatenate([_bdot(g, w_ref[s, m], 1, 1) for s in range(NSHARD)], axis=1)

        def add_dw(a, g, m):
            for s in range(NSHARD):
                dw_ref[s, m] += _bdot(a[:, s * rs:(s + 1) * rs], g, 0, 0)

        pa = [jax.vjp(_seg_gate, oa[:, h * DA:(h + 1) * DA], z[:, h * DA:(h + 1) * DA], aw) for h in range(NA)]
        pb = [jax.vjp(_seg_gate, ob[:, h * DVB:(h + 1) * DVB], gb[:, h * DVB:(h + 1) * DVB], bw)
              for h in range(NB)]
        a1 = jnp.concatenate([v for v, _ in pa], axis=1).astype(BF16)
        a2 = jnp.concatenate([v for v, _ in pb], axis=1).astype(BF16)
        ya = mat(a1, 0)
        yb = mat(a2, 1)
        merged, vjp_m = jax.vjp(_seg_merge, ya, yb, ga_ref[...], gB_ref[...])
        mb = merged.astype(BF16)
        out = mat(mb, 2)
        loss, vjp_l = jax.vjp(_seg_loss, out, x_ref[...], t_ref[...], lw_ref[...])
        dout, dyres, _, dlw = vjp_l(jnp.ones((1, 1), F32))
        dy_ref[...] = dyres
        doutb = dout.astype(BF16)
        dmerged = mat_t(doutb, 2)
        dya, dyb, dga, dgB = vjp_m(dmerged)
        dga_ref[...] = dga.astype(BF16)
        dgB_ref[...] = dgB.astype(BF16)
        dyab, dybb = dya.astype(BF16), dyb.astype(BF16)
        da1 = mat_t(dyab, 0)
        da2 = mat_t(dybb, 1)

        daw = jnp.zeros_like(aw)
        for h in range(NA):
            sl = slice(h * DA, (h + 1) * DA)
            do, dz, dw = pa[h][1](da1[:, sl])
            doa_ref[:, sl] = do
            dz_ref[:, sl] = dz.astype(BF16)
            daw = daw + dw
        dbw = jnp.zeros_like(bw)
        for h in range(NB):
            sl = slice(h * DVB, (h + 1) * DVB)
            do, dg, dw = pb[h][1](da2[:, sl])
            dob_ref[:, sl] = do
            dgb_ref[:, sl] = dg.astype(BF16)
            dbw = dbw + dw

        @pl.when(first)
        def _():
            loss_ref[...] = jnp.zeros_like(loss_ref)
            dw_ref[...] = jnp.zeros_like(dw_ref)
            daw_ref[...] = jnp.zeros_like(daw_ref)
            dbw_ref[...] = jnp.zeros_like(dbw_ref)
            dlw_ref[...] = jnp.zeros_like(dlw_ref)

        loss_ref[...] += jnp.broadcast_to(loss, loss_ref.shape)
        add_dw(a1, dyab, 0)
        add_dw(a2, dybb, 1)
        add_dw(mb, doutb, 2)
        daw_ref[...] += jnp.broadcast_to(daw, daw_ref.shape)
        dbw_ref[...] += jnp.broadcast_to(dbw, dbw_ref.shape)
        dlw_ref[...] += jnp.broadcast_to(dlw, dlw_ref.shape)

    two = pl.BlockSpec((2, tm, D), lambda i: (0, i, 0))
    pcol = lambda c: pl.BlockSpec((tm, D), lambda i: (i, c))
    tok = pl.BlockSpec((tm, D), lambda i: (i, 0))
    row = lambda n: pl.BlockSpec((1, n), lambda i: (0, 0))
    row8 = lambda n: pl.BlockSpec((8, n), lambda i: (0, 0))
    once = pl.Buffered(1)
    tokf = jax.ShapeDtypeStruct((t, D), F32)
    tokb = jax.ShapeDtypeStruct((t, D), BF16)
    wspec = pl.BlockSpec((NSHARD, 3, D // NSHARD, D), lambda i: (0, 0, 0, 0), pipeline_mode=once)
    return pl.pallas_call(
        body, name="post", grid=(t // tm,),
        in_specs=[two, two, pcol(3), pcol(6), pcol(7), pcol(8), tok, tok, row(DA), row(DVB), row(D), wspec],
        out_specs=[row8(128), tok, tok, tok, tok, tok, tok, tok, wspec, row8(DA), row8(DVB), row8(D)],
        out_shape=[jax.ShapeDtypeStruct((8, 128), F32), tokf, tokf, tokb, tokb, tokb, tokb, tokf,
                   jax.ShapeDtypeStruct((NSHARD, 3, D // NSHARD, D), F32),
                   jax.ShapeDtypeStruct((8, DA), F32), jax.ShapeDtypeStruct((8, DVB), F32),
                   jax.ShapeDtypeStruct((8, D), F32)],
        compiler_params=_cparams(("arbitrary",), vmem_mb=56),
    )(oa2, ob2, p, p, p, p, x, tgt, gdn_w, gla_w, lnpost, w3)


def _adam_math(w, g, m, v):
    nm = B1 * m + (1.0 - B1) * g
    nv = B2 * v + (1.0 - B2) * (g * g)
    m_hat = nm / (1.0 - B1 ** STEP)
    v_hat = nv / (1.0 - B2 ** STEP)
    return -LR * (m_hat / (jnp.sqrt(v_hat) + ADAM_EPS) + WD * w), nm, nv


SMALL_SLOTS = (("ln_pre_w", 0, 1024, 0), ("a_log_fwd", 1024, 8, 0), ("a_log_bwd", 1024, 8, 8),
               ("dt_bias_fwd", 1152, 8, 0), ("dt_bias_bwd", 1152, 8, 8), ("gdn_norm_w", 1280, 128, 0),
               ("gk_b2_fwd", 1408, 512, 0), ("gk_b2_bwd", 1920, 512, 0), ("gla_norm_w", 2432, 256, 0),
               ("ln_post_w", 2688, 1024, 0))
SMALL_W = 3712


def _adam_small(gsum, ws, ms, vs):
    nw = len(SMALL_SLOTS)

    def body(g_ref, *refs):
        w_refs, m_refs, v_refs, outs = refs[0:nw], refs[nw:2 * nw], refs[2 * nw:3 * nw], refs[3 * nw:]
        for i, (_, off, n, shift) in enumerate(SMALL_SLOTS):
            slot = g_ref[0:1, off:off + max(n, 128)]
            if shift:
                slot = pltpu.roll(slot, 128 - shift, 1)
            g = slot[:, 0:n]
            d, nm, nv = _adam_math(w_refs[i][...], g, m_refs[i][...], v_refs[i][...])
            for k, val in enumerate((g, d, nm, nv)):
                outs[4 * i + k][...] = val

    vm = pl.BlockSpec(memory_space=pltpu.VMEM)
    res = pl.pallas_call(
        body, name="adam_small", in_specs=[vm] * (1 + 3 * nw), out_specs=[vm] * (4 * nw),
        out_shape=[jax.ShapeDtypeStruct((1, n), F32) for _, _, n, _ in SMALL_SLOTS for _ in range(4)],
    )(gsum, *ws, *ms, *vs)
    return {name: res[4 * i:4 * i + 4] for i, (name, _, _, _) in enumerate(SMALL_SLOTS)}


def _adam(w, g, m, v, tr):
    rows, cols = w.shape

    def body(w_ref, g_ref, m_ref, v_ref, d_ref, nm_ref, nv_ref):
        d, nm, nv = _adam_math(w_ref[...], g_ref[...], m_ref[...], v_ref[...])
        d_ref[...] = d
        nm_ref[...] = nm
        nv_ref[...] = nv

    blk = pl.BlockSpec((tr, cols), lambda i: (i, 0))
    shp = jax.ShapeDtypeStruct((rows, cols), F32)
    return pl.pallas_call(
        body, name=f"adam_{rows}x{cols}", grid=(rows // tr,),
        in_specs=[blk] * 4, out_specs=[blk] * 3, out_shape=[shp] * 3,
        compiler_params=_cparams(("parallel",)),
    )(w, g, m, v)


def _sum_cast(own, got):
    ns, _, r, c = own.shape
    tr = r // 4 if r >= 64 else r

    def body(c_ref, a_ref, b_ref, f_ref, h_ref):
        s = a_ref[...] + b_ref[...]
        f_ref[...] = s
        h_ref[...] = s.astype(BF16)

    return pl.pallas_call(
        body, name=f"sum_cast_{r}x{c}",
        grid_spec=pltpu.PrefetchScalarGridSpec(
            num_scalar_prefetch=1, grid=(ns, r // tr),
            in_specs=[pl.BlockSpec((None, None, tr, c), lambda s, i, cc: (s, cc[0], i, 0)),
                      pl.BlockSpec((None, tr, c), lambda s, i, cc: (s, i, 0))],
            out_specs=[pl.BlockSpec((None, tr, c), lambda s, i, cc: (s, i, 0)),
                       pl.BlockSpec((None, tr, c), lambda s, i, cc: (s, i, 0))]),
        out_shape=[jax.ShapeDtypeStruct((ns, r, c), F32), jax.ShapeDtypeStruct((ns, r, c), BF16)],
        compiler_params=_cparams(("parallel", "parallel")),
    )(lax.axis_index("c").reshape(1), own, got)


def _sum4(mine, got):
    _, r, c = mine.shape
    tr = r // 4 if r >= 64 else r

    def body(s_ref, a_ref, g_ref, o_ref):
        acc = a_ref[...] + g_ref[0].astype(F32)
        acc = acc + g_ref[1].astype(F32)
        o_ref[...] = acc + g_ref[2].astype(F32)

    shard = (2 * lax.axis_index("x") + lax.axis_index("y")).reshape(1)
    return pl.pallas_call(
        body, name=f"sum4_{r}x{c}",
        grid_spec=pltpu.PrefetchScalarGridSpec(
            num_scalar_prefetch=1, grid=(r // tr,),
            in_specs=[pl.BlockSpec((None, tr, c), lambda i, ss: (ss[0], i, 0)),
                      pl.BlockSpec((3, tr, c), lambda i, ss: (0, i, 0))],
            out_specs=pl.BlockSpec((tr, c), lambda i, ss: (i, 0))),
        out_shape=jax.ShapeDtypeStruct((r, c), F32),
        compiler_params=_cparams(("parallel",)),
    )(shard, mine, got)


def _place():
    x, y, c = lax.axis_index("x"), lax.axis_index("y"), lax.axis_index("c")
    chips = [(1 - x, y), (x, 1 - y), (1 - x, 1 - y)]
    return x, y, c, chips


def _gather_weights(parts):
    npart = len(parts)

    def body(*refs):
        ins, outs = refs[:npart], refs[npart:2 * npart]
        send_sems, recv_sems = refs[2 * npart:]
        x, y, c, chips = _place()
        sibling = (x, y, 1 - c)
        mine = 2 * x + y

        def remote(k, p, shard, half, to, src=None):
            dst = outs[p].at[shard, half]
            return pltpu.make_async_remote_copy(
                src_ref=dst if src is None else src, dst_ref=dst,
                send_sem=send_sems.at[k], recv_sem=recv_sems.at[k], device_id=to, device_id_type=MESH)

        first = [remote(j * npart + p, p, mine, c, (*chip, c), src=ins[p].at[c])
                 for j, chip in enumerate(chips) for p in range(npart)]
        for cp in first:
            cp.start()
        passed = []
        for j, (cx, cy) in enumerate(chips):
            for p in range(npart):
                remote(j * npart + p, p, 2 * cx + cy, c, (x, y, c)).wait_recv()
                fw = remote((3 + j) * npart + p, p, 2 * cx + cy, c, sibling)
                fw.start()
                passed.append(fw)
        for j, (cx, cy) in enumerate(chips):
            for p in range(npart):
                remote((3 + j) * npart + p, p, 2 * cx + cy, 1 - c, (x, y, c)).wait_recv()
        for cp in first + passed:
            cp.wait_send()

    got = pl.pallas_call(
        body, name="gather_weights",
        in_specs=[ANY] * npart, out_specs=[ANY] * npart,
        out_shape=[jax.ShapeDtypeStruct((NSHARD,) + a.shape, a.dtype) for a in parts],
        scratch_shapes=[pltpu.SemaphoreType.DMA((6 * npart,)), pltpu.SemaphoreType.DMA((6 * npart,))],
    )(*parts)
    mine = 2 * lax.axis_index("x") + lax.axis_index("y")
    return [lax.dynamic_update_index_in_dim(g, a, mine, 0) for g, a in zip(got, parts)]


def _swap_halves(parts):
    npart = len(parts)

    def body(*refs):
        ins, outs = refs[:npart], refs[npart:2 * npart]
        send_sems, recv_sems = refs[2 * npart:]
        x, y, c, _ = _place()
        cps = [pltpu.make_async_remote_copy(
            src_ref=ins[p].at[s, 1 - c], dst_ref=outs[p].at[s],
            send_sem=send_sems.at[s * npart + p], recv_sem=recv_sems.at[s * npart + p],
            device_id=(x, y, 1 - c), device_id_type=MESH) for s in range(NSHARD) for p in range(npart)]
        for cp in cps:
            cp.start()
        for cp in cps:
            cp.wait()

    return pl.pallas_call(
        body, name="swap_halves", in_specs=[ANY] * npart, out_specs=[ANY] * npart,
        out_shape=[jax.ShapeDtypeStruct((NSHARD,) + a.shape[2:], a.dtype) for a in parts],
        scratch_shapes=[pltpu.SemaphoreType.DMA((NSHARD * npart,)), pltpu.SemaphoreType.DMA((NSHARD * npart,))],
    )(*parts)


def _scatter_shards(parts):
    npart = len(parts)

    def body(*refs):
        ins, outs = refs[:npart], refs[npart:2 * npart]
        send_sems, recv_sems = refs[2 * npart:]
        x, y, c, chips = _place()
        cps = [pltpu.make_async_remote_copy(
            src_ref=ins[p].at[2 * cx + cy], dst_ref=outs[p].at[j],
            send_sem=send_sems.at[j * npart + p], recv_sem=recv_sems.at[j * npart + p],
            device_id=(cx, cy, c), device_id_type=MESH)
            for j, (cx, cy) in enumerate(chips) for p in range(npart)]
        for cp in cps:
            cp.start()
        for cp in cps:
            cp.wait()

    return pl.pallas_call(
        body, name="scatter_shards", in_specs=[ANY] * npart, out_specs=[ANY] * npart,
        out_shape=[jax.ShapeDtypeStruct((3,) + a.shape[1:], a.dtype) for a in parts],
        scratch_shapes=[pltpu.SemaphoreType.DMA((3 * npart,)), pltpu.SemaphoreType.DMA((3 * npart,))],
    )(*parts)


HBM = pl.BlockSpec(memory_space=pltpu.HBM)
SEM = pl.BlockSpec(memory_space=pltpu.SEMAPHORE)
EFFECT = pltpu.SideEffectType.DATAFLOW_SIDE_EFFECTING


def _scatter_copies(srcs, lands, send_sems, recv_sems):
    x, y, c, chips = _place()
    n = len(srcs)
    return [pltpu.make_async_remote_copy(
        src_ref=srcs[p].at[2 * cx + cy], dst_ref=lands[p].at[j],
        send_sem=send_sems.at[j * n + p], recv_sem=recv_sems.at[j * n + p],
        device_id=(cx, cy, c), device_id_type=MESH)
        for j, (cx, cy) in enumerate(chips) for p in range(n)]


def _scatter_start(parts, name):
    n = len(parts)
    lands = [lax.empty((3,) + a.shape[1:], a.dtype) for a in parts]

    def body(*refs):
        send_sems, recv_sems = refs[2 * n], refs[2 * n + 1]
        for cp in _scatter_copies(refs[:n], refs[n:2 * n], send_sems, recv_sems):
            cp.start()
        refs[-1][...] = jnp.zeros_like(refs[-1])

    res = pl.pallas_call(
        body, name=name,
        out_shape=(pltpu.SemaphoreType.DMA((3 * n,)), pltpu.SemaphoreType.DMA((3 * n,)),
                   *[pltpu.HBM(a.shape, a.dtype) for a in parts], *[pltpu.HBM(a.shape, a.dtype) for a in lands],
                   jax.ShapeDtypeStruct((8, 128), F32)),
        in_specs=[HBM] * (2 * n), out_specs=(SEM, SEM, *[HBM] * (2 * n), pl.BlockSpec(memory_space=pltpu.VMEM)),
        input_output_aliases={i: 2 + i for i in range(2 * n)},
        compiler_params=pltpu.CompilerParams(has_side_effects=EFFECT),
    )(*[pltpu.with_memory_space_constraint(a, pltpu.HBM) for a in parts],
      *[pltpu.with_memory_space_constraint(a, pltpu.HBM) for a in lands])
    return res[0], res[1], res[2:2 + n], res[2 + n:2 + 2 * n], res[-1]


def _scatter_wait(started, after, name):
    send_sems, recv_sems, srcs, lands, _ = started
    n = len(srcs)

    def body(*refs):
        for cp in _scatter_copies(refs[:n], refs[n:2 * n], refs[2 * n], refs[2 * n + 1]):
            cp.wait_send()
            cp.wait_recv()

    res = pl.pallas_call(
        body, name=name,
        out_shape=tuple(pltpu.HBM(a.shape, a.dtype) for a in (*srcs, *lands)),
        in_specs=[HBM] * (2 * n) + [SEM, SEM, ANY], out_specs=tuple([HBM] * (2 * n)),
        input_output_aliases={i: i for i in range(2 * n)},
        compiler_params=pltpu.CompilerParams(has_side_effects=EFFECT),
    )(*srcs, *lands, send_sems, recv_sems, after)
    return res[n:]


def _join_halves(parts):
    npart = len(parts)

    def body(*refs):
        ins, outs = refs[:npart], refs[npart:2 * npart]
        send_sems, recv_sems = refs[2 * npart:]
        x, y, c, _ = _place()
        cps = [pltpu.make_async_remote_copy(
            src_ref=ins[p], dst_ref=outs[p], send_sem=send_sems.at[p], recv_sem=recv_sems.at[p],
            device_id=(x, y, 1 - c), device_id_type=MESH) for p in range(npart)]
        for cp in cps:
            cp.start()
        for cp in cps:
            cp.wait()

    got = pl.pallas_call(
        body, name="join_halves", in_specs=[ANY] * npart, out_specs=[ANY] * npart,
        out_shape=[jax.ShapeDtypeStruct(a.shape, a.dtype) for a in parts],
        scratch_shapes=[pltpu.SemaphoreType.DMA((npart,)), pltpu.SemaphoreType.DMA((npart,))],
    )(*parts)
    south = lax.axis_index("c") == 0
    return [jnp.where(south, jnp.stack([a, g]), jnp.stack([g, a])) for a, g in zip(parts, got)]


def _allreduce_small(v):
    r, ncol = v.shape

    def body(v_ref, o_ref, buf, send_sems, recv_sems):
        x, y, c, _ = _place()
        me = 4 * x + 2 * y + c
        buf[me] = v_ref[...]
        cps = []
        for k in range(1, 8):
            px, py, pc = x ^ (k >> 2), y ^ ((k >> 1) & 1), c ^ (k & 1)
            cps.append(pltpu.make_async_remote_copy(
                src_ref=v_ref, dst_ref=buf.at[me], send_sem=send_sems.at[k - 1], recv_sem=recv_sems.at[k - 1],
                device_id=(px, py, pc), device_id_type=MESH))
        for cp in cps:
            cp.start()
        for k in range(1, 8):
            px, py, pc = x ^ (k >> 2), y ^ ((k >> 1) & 1), c ^ (k & 1)
            pltpu.make_async_remote_copy(
                src_ref=v_ref, dst_ref=buf.at[4 * px + 2 * py + pc], send_sem=send_sems.at[k - 1],
                recv_sem=recv_sems.at[k - 1], device_id=(px, py, pc), device_id_type=MESH).wait_recv()
        for cp in cps:
            cp.wait_send()
        acc = buf[0]
        for d in range(1, 8):
            acc = acc + buf[d]
        o_ref[...] = acc

    return pl.pallas_call(
        body, name="allreduce_small",
        in_specs=[pl.BlockSpec(memory_space=pltpu.VMEM)], out_specs=pl.BlockSpec(memory_space=pltpu.VMEM),
        out_shape=jax.ShapeDtypeStruct((r, ncol), F32),
        scratch_shapes=[pltpu.VMEM((8, r, ncol), F32), pltpu.SemaphoreType.DMA((7,)), pltpu.SemaphoreType.DMA((7,))],
    )(v)


def _permute_cols(shards):
    w0, w1, w2, w3 = shards
    zeros = jnp.zeros((w0.shape[0], NPERM - 9280), w0.dtype)
    return jnp.concatenate([w0, w1[:, 0:1776], w1[:, 1808:2320], w2, w3[:, 0:240], w3[:, 272:2320],
                            w1[:, 1776:1808], w3[:, 240:272], zeros], axis=1)


def _unpermute_cols(g):
    s1 = jnp.concatenate([g[..., 2320:4096], g[..., 9216:9248], g[..., 4096:4608]], axis=-1)
    s3 = jnp.concatenate([g[..., 6928:7168], g[..., 9248:9280], g[..., 7168:9216]], axis=-1)
    return jnp.stack([g[..., 0:2320], s1, g[..., 4608:6928], s3], axis=0)


def _pack_shard_small(conv, w2f, w2b):
    top = jnp.pad(conv, ((0, 8 - conv.shape[0]), (0, 0)))
    mid = jnp.pad(jnp.concatenate([w2f, w2b], axis=1), ((0, 0), (0, 768 - 256)))
    return jnp.concatenate([top, mid, jnp.zeros((8, 768), conv.dtype)], axis=0)


def _unpack_shard_small(a):
    return a[0:5], a[8:24, 0:128], a[8:24, 128:256]


def kernel(x, ln_pre_w, w_in, conv_w, a_log_fwd, a_log_bwd, dt_bias_fwd, dt_bias_bwd, gdn_norm_w, w_proj_gdn, gk_w2_fwd, gk_b2_fwd, gk_w2_bwd, gk_b2_bwd, gla_norm_w, w_proj_gla, w_out, ln_post_w, loss_target, m_ln_pre_w, m_w_in, m_conv_w, m_a_log_fwd, m_a_log_bwd, m_dt_bias_fwd, m_dt_bias_bwd, m_gdn_norm_w, m_w_proj_gdn, m_gk_w2_fwd, m_gk_b2_fwd, m_gk_w2_bwd, m_gk_b2_bwd, m_gla_norm_w, m_w_proj_gla, m_w_out, m_ln_post_w, v_ln_pre_w, v_w_in, v_conv_w, v_a_log_fwd, v_a_log_bwd, v_dt_bias_fwd, v_dt_bias_bwd, v_gdn_norm_w, v_w_proj_gdn, v_gk_w2_fwd, v_gk_b2_fwd, v_gk_w2_bwd, v_gk_b2_bwd, v_gla_norm_w, v_w_proj_gla, v_w_out, v_ln_post_w):
    t = x.shape[1]
    x2, tgt = x[0], loss_target[0]

    win_l = w_in[0].astype(BF16).reshape(2, D // 2, SHW)
    proj_l = jnp.concatenate([w_proj_gdn[0], w_proj_gla[0], w_out[0]], axis=0).astype(BF16).reshape(2, 384, D)
    small_l = _pack_shard_small(conv_w[0], gk_w2_fwd[0], gk_w2_bwd[0]).reshape(2, 16, 768)
    win_g, proj_g, small_g = _gather_weights([win_l, proj_l, small_l])
    wperm = _permute_cols(win_g.reshape(NSHARD, D, SHW))
    w3 = proj_g.reshape(NSHARD, 3, D // NSHARD, D)
    small_g = small_g.reshape(NSHARD, 32, 768)
    convw = small_g[:, 0:8, :].transpose(1, 0, 2).reshape(8, 3 * D)
    w2f = small_g[:, 8:24, 0:128].transpose(1, 0, 2).reshape(16, 512)
    w2b = small_g[:, 8:24, 128:256].transpose(1, 0, 2).reshape(16, 512)
    w2f_pad = jnp.pad(w2f, ((32, 80), (0, 0)))
    w2b_pad = jnp.pad(w2b, ((48, 64), (0, 0)))
    alog_row = jnp.pad(jnp.concatenate([a_log_fwd, a_log_bwd], axis=1), ((0, 0), (0, 112)))
    dt_row = jnp.pad(jnp.concatenate([dt_bias_fwd, dt_bias_bwd], axis=1), ((0, 0), (0, 112)))

    p, h = _inproj(x2, ln_pre_w, wperm)
    qn, kn, vc = (_qkv_fwd(p, convw, kind) for kind in range(3))
    gsm, gk = _gates_fwd(p, alog_row, dt_row, w2f_pad, gk_b2_fwd, w2b_pad, gk_b2_bwd)
    g2, b2 = _gcum_fwd(gsm)
    u, w, at, qd, kd, el, tinv = _gdn_intra_fwd(qn, kn, vc, g2, b2)
    oa2, sa = _gdn_scan_fwd(u, w, at, qd, kd, el)
    qg, kdb, intra, elb = _gla_intra_fwd(p, gk)
    ob2, sb = _gla_scan_fwd(p, qg, kdb, intra, elb)

    (loss8, doa, dob, dz, dgb, dga, dgB, dyres, dw3, dgdn_w, dgla_w, dlnpost) = _post(
        oa2, ob2, p, x2, tgt, gdn_norm_w, gla_norm_w, ln_post_w, w3)

    du, dw, dat, dqd, dkd, del_ = _gdn_scan_bwd(u, w, at, qd, kd, el, sa, doa)
    dqn, dkn, dvc, dg2, db2 = _gdn_intra_bwd(qn, kn, vc, g2, b2, tinv, du, dw, dat, dqd, dkd, del_)
    dgsm = _gcum_bwd(gsm, dg2, db2)
    dqg, dkdb, dvs, delb = _gla_scan_bwd(p, qg, kdb, elb, sb, dob)
    dqb, dkb, dvb, dgk = _gla_intra_bwd(p, gk, dqg, dkdb, dvs, delb, dob)
    (dps, dalog8, ddt8, dw2f_pad, db2f8, dw2b_pad, db2b8) = _gates_bwd(
        p, alog_row, dt_row, w2f_pad, gk_b2_fwd, w2b_pad, gk_b2_bwd, dgsm, dgk)
    dpre, dconv = zip(*[_qkv_bwd(p, convw, g, kind) for kind, g in enumerate((dqn, dkn, dvc))])

    pieces = (*dpre, dz, dqb, dkb, dvb, dgb, dga, dgB, dps)
    dwperm = _inproj_dw(h, pieces)

    g_in = _unpermute_cols(dwperm.reshape(2, D // 2, NPERM))
    g_proj = dw3.reshape(NSHARD, 2, 384, D)
    dconv_full = jnp.concatenate(dconv, axis=1)
    dw2f, dw2b = dw2f_pad[32:48], dw2b_pad[48:64]
    g_small = jnp.stack([_pack_shard_small(dconv_full[0:5, 768 * s:768 * (s + 1)],
                                           dw2f[:, 128 * s:128 * (s + 1)], dw2b[:, 128 * s:128 * (s + 1)])
                         for s in range(NSHARD)])
    g_small = g_small.reshape(NSHARD, 2, 16, 768)
    parts = [g_in, g_proj, g_small]
    got = _swap_halves(parts)
    sums = [_sum_cast(a, b) for a, b in zip(parts, got)]
    started = _scatter_start([hb for _, hb in sums], "scatter_start")
    dx, dlnpre8 = _inproj_dx(pieces, wperm, x2, ln_pre_w + started[4][0:1, 0:1], dyres)
    landed = _scatter_wait(started, dx, "scatter_wait")
    halves = [_sum4(f, g) for (f, _), g in zip(sums, landed)]
    r_in, r_proj, r_small = _join_halves(halves)
    grad_w_in = r_in.reshape(D, SHW)
    r_proj = r_proj.reshape(768, D)
    r_small = r_small.reshape(32, 768)

    gsmall = _allreduce_small(jnp.concatenate(
        [dlnpre8, dalog8, ddt8, dgdn_w, db2f8, db2b8, dgla_w, dlnpost], axis=1))

    d_in, nm_in, nv_in = _adam(w_in[0], grad_w_in, m_w_in[0], v_w_in[0], 128)
    stack3 = lambda a, b, c: jnp.concatenate([a[0], b[0], c[0]], axis=0)
    d_pr, nm_pr, nv_pr = _adam(stack3(w_proj_gdn, w_proj_gla, w_out), r_proj,
                               stack3(m_w_proj_gdn, m_w_proj_gla, m_w_out),
                               stack3(v_w_proj_gdn, v_w_proj_gla, v_w_out), 256)
    d_ss, nm_ss, nv_ss = _adam(_pack_shard_small(conv_w[0], gk_w2_fwd[0], gk_w2_bwd[0]), r_small,
                               _pack_shard_small(m_conv_w[0], m_gk_w2_fwd[0], m_gk_w2_bwd[0]),
                               _pack_shard_small(v_conv_w[0], v_gk_w2_fwd[0], v_gk_w2_bwd[0]), 32)
    smalls = dict(ln_pre_w=(ln_pre_w, m_ln_pre_w, v_ln_pre_w), a_log_fwd=(a_log_fwd, m_a_log_fwd, v_a_log_fwd),
                  a_log_bwd=(a_log_bwd, m_a_log_bwd, v_a_log_bwd),
                  dt_bias_fwd=(dt_bias_fwd, m_dt_bias_fwd, v_dt_bias_fwd),
                  dt_bias_bwd=(dt_bias_bwd, m_dt_bias_bwd, v_dt_bias_bwd),
                  gdn_norm_w=(gdn_norm_w, m_gdn_norm_w, v_gdn_norm_w),
                  gk_b2_fwd=(gk_b2_fwd, m_gk_b2_fwd, v_gk_b2_fwd), gk_b2_bwd=(gk_b2_bwd, m_gk_b2_bwd, v_gk_b2_bwd),
                  gla_norm_w=(gla_norm_w, m_gla_norm_w, v_gla_norm_w), ln_post_w=(ln_post_w, m_ln_post_w, v_ln_post_w))
    names = [name for name, _, _, _ in SMALL_SLOTS]
    small = _adam_small(gsmall, *([smalls[n][i] for n in names] for i in range(3)))

    def family(k, in_, pr, ss):
        conv, w2f_, w2b_ = _unpack_shard_small(ss)
        s = {n: small[n][k] for n in names}
        return [s["ln_pre_w"], in_[None], conv[None], s["a_log_fwd"], s["a_log_bwd"], s["dt_bias_fwd"],
                s["dt_bias_bwd"], s["gdn_norm_w"], pr[None, 0:256], w2f_[None], s["gk_b2_fwd"], w2b_[None],
                s["gk_b2_bwd"], s["gla_norm_w"], pr[None, 256:512], pr[None, 512:768], s["ln_post_w"]]

    loss = lax.psum(loss8[0, 0], ("x", "y", "c"))
    outs = [loss, dx[None]]
    outs += family(0, grad_w_in, r_proj, r_small)
    outs += family(1, d_in, d_pr, d_ss)
    outs += family(2, nm_in, nm_pr, nm_ss)
    outs += family(3, nv_in, nv_pr, nv_ss)
    return tuple(outs)
```

```python
import functools

import jax
import jax.numpy as jnp
from jax import lax
from jax.experimental import pallas as pl
from jax.experimental.pallas import tpu as pltpu

F32 = jnp.float32
BF16 = jnp.bfloat16
HI = lax.Precision.HIGHEST
MESH = pl.DeviceIdType.MESH

D = 1024
CH = 64
EPS = 1e-6
NA, DA = 8, 128
NB, DKB, DVB = 4, 128, 256
NSHARD = 4
SHW = 2320
NPERM = 9728
PS_BLOCK = 72
LR, B1, B2, ADAM_EPS, WD, STEP = 0.001, 0.9, 0.999, 1e-08, 0.01, 10

ANY = pl.BlockSpec(memory_space=pl.ANY)


def _cparams(sem=None, vmem_mb=48):
    return pltpu.CompilerParams(dimension_semantics=sem, vmem_limit_bytes=vmem_mb << 20)


def _bdot(a, b, ca, cb):
    return lax.dot_general(a.astype(BF16), b.astype(BF16), (((ca,), (cb,)), ((), ())),
                           preferred_element_type=F32)


@jax.custom_vjp
def mm(a, b):
    return _bdot(a, b, 1, 0)


def _mm_fwd(a, b):
    return _bdot(a, b, 1, 0), (a, b)


def _mm_bwd(res, g):
    a, b = res
    return _bdot(g, b, 1, 1), _bdot(a, g, 0, 0)


mm.defvjp(_mm_fwd, _mm_bwd)


@jax.custom_vjp
def mm_nt(a, b):
    return _bdot(a, b, 1, 1)


def _mm_nt_fwd(a, b):
    return _bdot(a, b, 1, 1), (a, b)


def _mm_nt_bwd(res, g):
    a, b = res
    return _bdot(g, b, 1, 0), _bdot(g, a, 0, 0)


mm_nt.defvjp(_mm_nt_fwd, _mm_nt_bwd)


@jax.custom_vjp
def mm_tn(a, b):
    return _bdot(a, b, 0, 0)


def _mm_tn_fwd(a, b):
    return _bdot(a, b, 0, 0), (a, b)


def _mm_tn_bwd(res, g):
    a, b = res
    return _bdot(b, g, 1, 1), _bdot(a, g, 1, 0)


mm_tn.defvjp(_mm_tn_fwd, _mm_tn_bwd)


def dot_hi(a, b):
    return lax.dot_general(a, b, (((1,), (0,)), ((), ())), precision=HI, preferred_element_type=F32)


def _split3(x):
    x1 = x.astype(BF16)
    r = x - x1.astype(F32)
    x2 = r.astype(BF16)
    return x1, x2, (r - x2.astype(F32)).astype(BF16)


def _cdot(c, x, cc, cx, c_first=True):
    parts = _split3(x)
    if c_first:
        return _bdot(c, parts[0], cc, cx) + _bdot(c, parts[1], cc, cx) + _bdot(c, parts[2], cc, cx)
    return _bdot(parts[0], c, cx, cc) + _bdot(parts[1], c, cx, cc) + _bdot(parts[2], c, cx, cc)


@jax.custom_vjp
def cmm(c, x):
    return _cdot(c, x, 1, 0)


def _cmm_fwd(c, x):
    return _cdot(c, x, 1, 0), c


def _cmm_bwd(c, g):
    return jnp.zeros_like(c), _cdot(c, g, 0, 0)


cmm.defvjp(_cmm_fwd, _cmm_bwd)


@jax.custom_vjp
def mmc(x, c):
    return _cdot(c, x, 0, 1, c_first=False)


def _mmc_fwd(x, c):
    return _cdot(c, x, 0, 1, c_first=False), c


def _mmc_bwd(c, g):
    return _cdot(c, g, 1, 1, c_first=False), jnp.zeros_like(c)


mmc.defvjp(_mmc_fwd, _mmc_bwd)


def _sigmoid(x):
    return 1.0 / (1.0 + jnp.exp(-x))


def _silu(x):
    return x * _sigmoid(x)


def _softplus(x):
    return jnp.maximum(x, 0.0) + jnp.log(1.0 + jnp.exp(-jnp.abs(x)))


def _rms(x, w):
    return x * lax.rsqrt(jnp.mean(x * x, axis=-1, keepdims=True) + EPS) * w


SC = 256


class _Consts:
    def __init__(self, rev):
        r = lax.broadcasted_iota(jnp.int32, (SC, SC), 0)
        c = lax.broadcasted_iota(jnp.int32, (SC, SC), 1)
        same = (r >> 6) == (c >> 6)
        a = jnp.where(rev, c, r)
        b = jnp.where(rev, r, c)
        self.incl = same & (a >= b)
        self.strict = same & (a > b)
        self.incl_f = self.incl.astype(F32)
        self.eye = (r == c).astype(F32)
        rows = lax.broadcasted_iota(jnp.int32, (SC, 1), 0)
        self.last_col = ((rows & (CH - 1)) == jnp.where(rev, 0, CH - 1)).astype(F32)
        rr = lax.broadcasted_iota(jnp.int32, (SC, CH), 0)
        cc = lax.broadcasted_iota(jnp.int32, (SC, CH), 1)
        self.fold = ((rr & (CH - 1)) == cc).astype(F32)


def _dot3(a, b, ca=1, cb=0):
    ah, bh = a.astype(BF16), b.astype(BF16)
    al, bl = (a - ah.astype(F32)).astype(BF16), (b - bh.astype(F32)).astype(BF16)
    return _bdot(ah, bh, ca, cb) + (_bdot(ah, bl, ca, cb) + _bdot(al, bh, ca, cb))


TRI_SPLIT_LEVELS = 2


def _tri_inv(low, eye):
    n = -low
    acc = eye + n
    p = n
    for level in range(5):
        dot = _dot3 if level < TRI_SPLIT_LEVELS else (lambda a, b: _bdot(a, b, 1, 0))
        p = dot(p, p)
        acc = acc + dot(acc, p)
    return acc


@jax.custom_vjp
def _solve2(low, rv, rk, tinv):
    x = _dot3(tinv, jnp.concatenate([rv, rk], axis=1))
    return x[:, :DA], x[:, DA:]


def _solve2_fwd(low, rv, rk, tinv):
    x = _dot3(tinv, jnp.concatenate([rv, rk], axis=1))
    return (x[:, :DA], x[:, DA:]), (x, tinv)


def _solve2_bwd(res, g):
    x, tinv = res
    drhs = _dot3(tinv, jnp.concatenate(g, axis=1), 0, 0)
    return -_dot3(drhs, x, 1, 1), drhs[:, :DA], drhs[:, DA:], jnp.zeros_like(tinv)


_solve2.defvjp(_solve2_fwd, _solve2_bwd)


def _chunk_last(x, cs):
    xs = (x * cs.last_col).reshape(SC // CH, CH, x.shape[1])
    return jnp.broadcast_to(jnp.sum(xs, axis=1, keepdims=True), xs.shape).reshape(x.shape)


def _gdn_decay(g, cs):
    gw = jnp.concatenate([g] * (SC // DA), axis=1)
    grow = jnp.sum(cs.eye * gw, axis=0, keepdims=True)
    return jnp.where(cs.incl, jnp.exp(jnp.where(cs.incl, gw - grow, 0.0)), 0.0)


def _gdn_intra(q, k, v, g, bx, tinv, cs):
    decay = _gdn_decay(g, cs)
    kb = k * bx
    low = jnp.where(cs.strict, mm_nt(kb, k) * decay, 0.0)
    eg = jnp.exp(g)
    made = tinv is None
    if made:
        tinv = _tri_inv(low, cs.eye)
    u, w = _solve2(low, v * bx, kb * eg, tinv)
    attn = mmc(mm_nt(q, k) * decay, cs.fold)
    qd = q * eg
    glast = _chunk_last(g, cs)
    kd = k * jnp.exp(glast - g)
    outs = (u, w, attn, qd, kd, jnp.exp(glast))
    return outs + (tinv,) if made else outs


def _gdn_scan(u, w, attn, qd, kd, el, s):
    vn = u - mm(w, s)
    o = mm(qd, s) + mm(attn, vn)
    sn = s * el + mm_tn(kd, vn)
    return o, sn


def _gla_intra(q, k, v, gk, cs):
    gc = cmm(cs.incl_f, gk)
    qg = q * (DKB ** -0.5) * jnp.exp(gc)
    kg = k * jnp.exp(-gc)
    attn = jnp.where(cs.incl, mm_nt(qg, kg), 0.0)
    intra = mm(attn, v)
    glast = _chunk_last(gc, cs)
    kd = k * jnp.exp(glast - gc)
    return qg, kd, intra, jnp.exp(glast)


def _gla_scan(qg, kd, v, el, st):
    o = mm_nt(qg, st)
    stn = st * el + mm_tn(v, kd)
    return o, stn


def _shift_rows(x, s):
    if s == 0:
        return x
    t = x.shape[0]
    rolled = pltpu.roll(x, (-s) % t, 0)
    rows = lax.broadcasted_iota(jnp.int32, x.shape, 0)
    return jnp.where((rows + s >= 0) & (rows + s < t), rolled, 0.0)


@jax.custom_vjp
def _conv5(x, w):
    acc = w[0:1] * _shift_rows(x, -2)
    for j in range(1, 5):
        acc = acc + w[j:j + 1] * _shift_rows(x, j - 2)
    return acc


def _conv5_fwd(x, w):
    return _conv5(x, w), (x, w)


def _conv5_bwd(res, g):
    x, w = res
    dx = w[0:1] * _shift_rows(g, 2)
    for j in range(1, 5):
        dx = dx + w[j:j + 1] * _shift_rows(g, 2 - j)
    rows = lax.broadcasted_iota(jnp.int32, w.shape, 0)
    dw = jnp.zeros_like(w)
    for j in range(5):
        dwj = jnp.sum(g * _shift_rows(x, j - 2), axis=0, keepdims=True)
        dw = dw + jnp.where(rows == j, dwj, 0.0)
    return dx, dw


_conv5.defvjp(_conv5_fwd, _conv5_bwd)


def _qkv_act(kind):
    def f(x, w):
        c = _silu(_conv5(x, w))
        if kind == 2:
            return c
        c = c * lax.rsqrt(jnp.sum(c * c, axis=-1, keepdims=True) + EPS)
        return c * (DA ** -0.5) if kind == 0 else c
    return f


def _inproj(x, lnw, wperm, tm=512, tn=512):
    t = x.shape[0]

    def body(x_ref, lnw_ref, w_ref, p_ref, h_ref, hbuf):
        @pl.when(pl.program_id(1) == 0)
        def _():
            hb = _rms(x_ref[...], lnw_ref[...]).astype(BF16)
            hbuf[...] = hb
            h_ref[...] = hb
        p_ref[...] = jnp.dot(hbuf[...], w_ref[...], preferred_element_type=F32)

    return pl.pallas_call(
        body, name="inproj", grid=(t // tm, NPERM // tn),
        in_specs=[pl.BlockSpec((tm, D), lambda i, j: (i, 0)),
                  pl.BlockSpec((1, D), lambda i, j: (0, 0)),
                  pl.BlockSpec((D, tn), lambda i, j: (0, j))],
        out_specs=[pl.BlockSpec((tm, tn), lambda i, j: (i, j)),
                   pl.BlockSpec((tm, D), lambda i, j: (i, 0))],
        out_shape=[jax.ShapeDtypeStruct((t, NPERM), F32),
                   jax.ShapeDtypeStruct((t, D), BF16)],
        scratch_shapes=[pltpu.VMEM((tm, D), BF16)],
        compiler_params=_cparams(("parallel", "arbitrary")),
    )(x, lnw, wperm)


DP_TILE = 512
DP_PIECES = ((0, 2), (2, 2), (4, 2), (6, 2), (8, 1), (9, 1), (10, 2), (12, 2), (14, 2), (16, 2), (18, 1))


def _piece_specs(tm, j_first):
    specs = []
    for j0, n in DP_PIECES:
        def imap(a, b, j0=j0, n=n):
            j, i = (a, b) if j_first else (b, a)
            inside = (j >= j0) & (j < j0 + n)
            return jnp.where(inside, i, 0), jnp.clip(j - j0, 0, n - 1)
        specs.append(pl.BlockSpec((tm, DP_TILE), imap))
    return specs


def _for_piece(j, refs, fn):
    for (j0, n), ref in zip(DP_PIECES, refs):
        @pl.when((j >= j0) & (j < j0 + n))
        def _(ref=ref):
            fn(ref[...])


def _inproj_dw(h, pieces, tm=512):
    t = h.shape[0]
    npc = len(pieces)

    def body(h_ref, *refs):
        dw_ref = refs[npc]

        @pl.when(pl.program_id(1) == 0)
        def _():
            dw_ref[...] = jnp.zeros_like(dw_ref)

        def add(dp):
            dw_ref[...] += _bdot(h_ref[...], dp, 0, 0)
        _for_piece(pl.program_id(0), refs[:npc], add)

    return pl.pallas_call(
        body, name="inproj_dw", grid=(NPERM // DP_TILE, t // tm),
        in_specs=[pl.BlockSpec((tm, D), lambda j, i: (i, 0))] + _piece_specs(tm, True),
        out_specs=pl.BlockSpec((D, DP_TILE), lambda j, i: (0, j)),
        out_shape=jax.ShapeDtypeStruct((D, NPERM), F32),
        compiler_params=_cparams(("parallel", "arbitrary")),
    )(h, *pieces)


def _inproj_dx(pieces, wperm, x, lnw, dyres, tm=512):
    t = x.shape[0]
    tn = DP_TILE
    nj = NPERM // tn
    npc = len(pieces)

    def body(*refs):
        w_ref, x_ref, lnw_ref, dy_ref, dx_ref, dlnw_ref, acc = refs[npc:]
        j = pl.program_id(1)

        @pl.when(j == 0)
        def _():
            acc[...] = jnp.zeros_like(acc)

        def add(dp):
            acc[...] += _bdot(dp, w_ref[...], 1, 1)
        _for_piece(j, refs[:npc], add)

        @pl.when(j == nj - 1)
        def _():
            _, vjp = jax.vjp(_rms, x_ref[...], lnw_ref[...])
            dx, dlnw = vjp(acc[...])
            dx_ref[...] = dx + dy_ref[...]

            @pl.when(pl.program_id(0) == 0)
            def _():
                dlnw_ref[...] = jnp.zeros_like(dlnw_ref)
            dlnw_ref[...] += jnp.broadcast_to(dlnw, dlnw_ref.shape)

    return pl.pallas_call(
        body, name="inproj_dx", grid=(t // tm, nj),
        in_specs=_piece_specs(tm, False) + [
                  pl.BlockSpec((D, tn), lambda i, j: (0, j)),
                  pl.BlockSpec((tm, D), lambda i, j: (i, 0)),
                  pl.BlockSpec((1, D), lambda i, j: (0, 0)),
                  pl.BlockSpec((tm, D), lambda i, j: (i, 0))],
        out_specs=[pl.BlockSpec((tm, D), lambda i, j: (i, 0)),
                   pl.BlockSpec((8, D), lambda i, j: (0, 0))],
        out_shape=[jax.ShapeDtypeStruct((t, D), F32), jax.ShapeDtypeStruct((8, D), F32)],
        scratch_shapes=[pltpu.VMEM((tm, D), F32)],
        compiler_params=_cparams(("arbitrary", "arbitrary")),
    )(*pieces, wperm, x, lnw, dyres)


def _qkv_fwd(p, convw, kind):
    t = p.shape[0]
    f = _qkv_act(kind)

    def body(p_ref, w_ref, o_ref):
        o_ref[...] = f(p_ref[...], w_ref[...])

    return pl.pallas_call(
        body, name=f"qkv_fwd{kind}", grid=(NA,),
        in_specs=[pl.BlockSpec((t, DA), lambda h: (0, kind * NA + h)),
                  pl.BlockSpec((8, DA), lambda h: (0, kind * NA + h))],
        out_specs=pl.BlockSpec((t, DA), lambda h: (0, h)),
        out_shape=jax.ShapeDtypeStruct((t, D), F32),
        compiler_params=_cparams(("parallel",)),
    )(p, convw)


def _qkv_bwd(p, convw, dout, kind):
    t = p.shape[0]
    f = _qkv_act(kind)

    def body(p_ref, w_ref, g_ref, dx_ref, dw_ref):
        _, vjp = jax.vjp(f, p_ref[...], w_ref[...])
        dx, dw = vjp(g_ref[...])
        dx_ref[...] = dx.astype(BF16)
        dw_ref[...] = dw

    return pl.pallas_call(
        body, name=f"qkv_bwd{kind}", grid=(NA,),
        in_specs=[pl.BlockSpec((t, DA), lambda h: (0, kind * NA + h)),
                  pl.BlockSpec((8, DA), lambda h: (0, kind * NA + h)),
                  pl.BlockSpec((t, DA), lambda h: (0, h))],
        out_specs=[pl.BlockSpec((t, DA), lambda h: (0, h)),
                   pl.BlockSpec((8, DA), lambda h: (0, h))],
        out_shape=[jax.ShapeDtypeStruct((t, D), BF16), jax.ShapeDtypeStruct((8, D), F32)],
        compiler_params=_cparams(("parallel",)),
    )(p, convw, dout)


def _gates_f(ps, alog_row, dt_row, w2f, b2f, w2b, b2b):
    lane = lax.broadcasted_iota(jnp.int32, ps.shape, 1)
    lg = -jnp.exp(alog_row) * _softplus(ps + dt_row)
    gsm = jnp.where(lane < 16, lg, jnp.where(lane < 32, _sigmoid(ps), 0.0))
    gkf = -_softplus(-(mm(ps, w2f) + b2f)) * (1.0 / 16.0)
    gkb = -_softplus(-(mm(ps, w2b) + b2b)) * (1.0 / 16.0)
    return gsm, gkf, gkb


def _gates_fwd(ps, alog_row, dt_row, w2f, b2f, w2b, b2b, tm=512):
    t = ps.shape[0]

    def body(ps_ref, a_ref, d_ref, wf_ref, bf_ref, wb_ref, bb_ref, gsm_ref, gk_ref):
        gsm, gkf, gkb = _gates_f(ps_ref[...], a_ref[...], d_ref[...], wf_ref[...], bf_ref[...],
                                 wb_ref[...], bb_ref[...])
        gsm_ref[...] = gsm
        gk_ref[0] = gkf
        gk_ref[1] = gkb

    row = lambda n: pl.BlockSpec((1, n), lambda i: (0, 0))
    mat = pl.BlockSpec((128, 512), lambda i: (0, 0))
    return pl.pallas_call(
        body, name="gates_fwd", grid=(t // tm,),
        in_specs=[pl.BlockSpec((tm, 128), lambda i: (i, PS_BLOCK)), row(128), row(128), mat, row(512), mat, row(512)],
        out_specs=[pl.BlockSpec((tm, 128), lambda i: (i, 0)),
                   pl.BlockSpec((2, tm, 512), lambda i: (0, i, 0))],
        out_shape=[jax.ShapeDtypeStruct((t, 128), F32), jax.ShapeDtypeStruct((2, t, 512), F32)],
        compiler_params=_cparams(("parallel",)),
    )(ps, alog_row, dt_row, w2f, b2f, w2b, b2b)


def _gates_bwd(ps, alog_row, dt_row, w2f, b2f, w2b, b2b, dgsm, dgk, tm=512):
    t = ps.shape[0]

    def body(ps_ref, a_ref, d_ref, wf_ref, bf_ref, wb_ref, bb_ref, dgsm_ref, dgk_ref,
             dps_ref, da_ref, dd_ref, dwf_ref, dbf_ref, dwb_ref, dbb_ref):
        _, vjp = jax.vjp(_gates_f, ps_ref[...], a_ref[...], d_ref[...], wf_ref[...], bf_ref[...],
                         wb_ref[...], bb_ref[...])
        dps, da, dd, dwf, dbf, dwb, dbb = vjp((dgsm_ref[...], dgk_ref[0], dgk_ref[1]))
        dps_ref[:, 0:128] = dps.astype(BF16)
        dps_ref[:, 128:DP_TILE] = jnp.zeros((tm, DP_TILE - 128), BF16)
        accs = ((da_ref, da), (dd_ref, dd), (dwf_ref, dwf), (dbf_ref, dbf), (dwb_ref, dwb), (dbb_ref, dbb))

        @pl.when(pl.program_id(0) == 0)
        def _():
            for ref, _ in accs:
                ref[...] = jnp.zeros_like(ref)
        for ref, val in accs:
            ref[...] += jnp.broadcast_to(val, ref.shape)

    row = lambda n: pl.BlockSpec((1, n), lambda i: (0, 0))
    row8 = lambda n: pl.BlockSpec((8, n), lambda i: (0, 0))
    mat = pl.BlockSpec((128, 512), lambda i: (0, 0))
    return pl.pallas_call(
        body, name="gates_bwd", grid=(t // tm,),
        in_specs=[pl.BlockSpec((tm, 128), lambda i: (i, PS_BLOCK)), row(128), row(128), mat, row(512), mat, row(512),
                  pl.BlockSpec((tm, 128), lambda i: (i, 0)),
                  pl.BlockSpec((2, tm, 512), lambda i: (0, i, 0))],
        out_specs=[pl.BlockSpec((tm, DP_TILE), lambda i: (i, 0)), row8(128), row8(128), mat, row8(512), mat,
                   row8(512)],
        out_shape=[jax.ShapeDtypeStruct((t, DP_TILE), BF16),
                   jax.ShapeDtypeStruct((8, 128), F32), jax.ShapeDtypeStruct((8, 128), F32),
                   jax.ShapeDtypeStruct((128, 512), F32), jax.ShapeDtypeStruct((8, 512), F32),
                   jax.ShapeDtypeStruct((128, 512), F32), jax.ShapeDtypeStruct((8, 512), F32)],
        compiler_params=_cparams(("arbitrary",)),
    )(ps, alog_row, dt_row, w2f, b2f, w2b, b2b, dgsm, dgk)


def _rows(i):
    return pl.ds(pl.multiple_of(i * CH, CH), CH)


def _srows(i):
    return pl.ds(pl.multiple_of(i * SC, SC), SC)


def _first_row(x):
    row = lax.broadcasted_iota(jnp.int32, (8, x.shape[1]), 0)
    return jnp.where(row == 0, jnp.broadcast_to(x, (8, x.shape[1])), 0.0)


def _chunk_rows(e_ref, i):
    pad = jnp.zeros((CH - 8, 128), F32)
    return jnp.concatenate([x for c in range(SC // CH) for x in (e_ref[(SC // CH) * i + c], pad)], axis=0)


def _gcum_f(gsm, tm):
    i = lax.broadcasted_iota(jnp.int32, (tm, tm), 0)
    j = lax.broadcasted_iota(jnp.int32, (tm, tm), 1)
    same = (i >> 6) == (j >> 6)
    lower = (same & (i >= j)).astype(F32)
    upper = (same & (i <= j)).astype(F32)
    r = lax.broadcasted_iota(jnp.int32, (128, D), 0)
    head = lax.broadcasted_iota(jnp.int32, (128, D), 1) >> 7
    pick = lambda off: (r == head + off).astype(F32)
    lane = lax.broadcasted_iota(jnp.int32, gsm.shape, 1)
    run = jnp.where(lane < 8, cmm(lower, gsm), cmm(upper, gsm))
    return mmc(run, pick(0)), mmc(run, pick(8)), mmc(gsm, pick(16)), mmc(gsm, pick(24))


def _gcum_fwd(gsm, tm=256):
    t = gsm.shape[0]

    def body(s_ref, g_ref, b_ref):
        gf, gb, bf, bb = _gcum_f(s_ref[...], tm)
        g_ref[0] = gf
        g_ref[1] = gb
        b_ref[0] = bf
        b_ref[1] = bb

    two = pl.BlockSpec((2, tm, D), lambda i: (0, i, 0))
    return pl.pallas_call(
        body, name="gcum_fwd", grid=(t // tm,),
        in_specs=[pl.BlockSpec((tm, 128), lambda i: (i, 0))], out_specs=[two, two],
        out_shape=[jax.ShapeDtypeStruct((2, t, D), F32)] * 2,
        compiler_params=_cparams(("parallel",)),
    )(gsm)


def _gcum_bwd(gsm, dg2, db2, tm=256):
    t = gsm.shape[0]

    def body(s_ref, dg_ref, db_ref, ds_ref):
        _, vjp = jax.vjp(lambda s: _gcum_f(s, tm), s_ref[...])
        ds_ref[...] = vjp((dg_ref[0], dg_ref[1], db_ref[0], db_ref[1]))[0]

    two = pl.BlockSpec((2, tm, D), lambda i: (0, i, 0))
    tile = pl.BlockSpec((tm, 128), lambda i: (i, 0))
    return pl.pallas_call(
        body, name="gcum_bwd", grid=(t // tm,),
        in_specs=[tile, two, two], out_specs=tile,
        out_shape=jax.ShapeDtypeStruct((t, 128), F32),
        compiler_params=_cparams(("parallel",)),
    )(gsm, dg2, db2)


def _gdn_intra_fwd(qn, kn, vc, g2, b2):
    t = qn.shape[0]
    n = t // CH

    def body(q_ref, k_ref, v_ref, g_ref, b_ref, u_ref, w_ref, a_ref, qd_ref, kd_ref, e_ref, t_ref):
        cs = _Consts(pl.program_id(0) == 1)

        def step(i, carry):
            r = _srows(i)
            q, k, v, g, bx = q_ref[r, :], k_ref[r, :], v_ref[r, :], g_ref[r, :], b_ref[r, :]
            u, w, a, qd, kd, el, tinv = _gdn_intra(q, k, v, g, bx, None, cs)
            u_ref[r, :] = u
            w_ref[r, :] = w
            a_ref[r, :] = a
            qd_ref[r, :] = qd
            kd_ref[r, :] = kd
            t_ref[r, :] = tinv
            for c in range(SC // CH):
                e_ref[(SC // CH) * i + c] = el[c * CH:c * CH + 8]
            return carry

        lax.fori_loop(0, t // SC, step, 0)

    head = pl.BlockSpec((t, DA), lambda d, h: (0, h))
    dh = pl.BlockSpec((None, t, DA), lambda d, h: (d, 0, h))
    sq = lambda w: pl.BlockSpec((None, None, t, w), lambda d, h: (d, h, 0, 0))
    big = jax.ShapeDtypeStruct((2, t, D), F32)
    return pl.pallas_call(
        body, name="gdn_intra_fwd", grid=(2, NA),
        in_specs=[head, head, head, dh, dh],
        out_specs=[dh, dh, sq(CH), dh, dh, pl.BlockSpec((None, None, n, 8, 128), lambda d, h: (d, h, 0, 0, 0)),
                   sq(SC)],
        out_shape=[big, big, jax.ShapeDtypeStruct((2, NA, t, CH), F32), big, big,
                   jax.ShapeDtypeStruct((2, NA, n, 8, 128), F32), jax.ShapeDtypeStruct((2, NA, t, SC), F32)],
        compiler_params=_cparams(("parallel", "parallel")),
    )(qn, kn, vc, g2, b2)


SCAN_TB = 256
SCAN_HB = 8


def _scan_specs(t, width, nheads, hb, along):
    nt = t // SCAN_TB
    nb = SCAN_TB // CH

    def tmap(d, tt):
        fwd = tt + d * (nt - 1 - 2 * tt)
        return fwd if along > 0 else nt - 1 - fwd

    tok = pl.BlockSpec((None, SCAN_TB, hb * width), lambda d, h, tt: (d, tmap(d, tt), h))
    per = lambda *tail: pl.BlockSpec((None, hb, nb) + tail, lambda d, h, tt: (d, h, tmap(d, tt)) + (0,) * len(tail))
    sq = pl.BlockSpec((None, hb, SCAN_TB, CH), lambda d, h, tt: (d, h, tmap(d, tt), 0))
    shared = lambda w: pl.BlockSpec((SCAN_TB, hb * w), lambda d, h, tt: (tmap(d, tt), h))
    return tok, per, sq, shared, (2, nheads // hb, nt), nb


def _gdn_scan_fwd(u, w, a, qd, kd, e):
    t = u.shape[1]
    tok, per, sq, _, grid, nb = _scan_specs(t, DA, NA, SCAN_HB, +1)

    def body(u_ref, w_ref, a_ref, qd_ref, kd_ref, e_ref, o_ref, s_ref, state):
        rev = pl.program_id(0) == 1

        @pl.when(pl.program_id(2) == 0)
        def _():
            state[...] = jnp.zeros_like(state)

        def step(i, ss):
            ci = jnp.where(rev, nb - 1 - i, i)
            r = _rows(ci)
            out = []
            for hh, s in enumerate(ss):
                c = slice(hh * DA, (hh + 1) * DA)
                s_ref[hh, ci] = s
                o, sn = _gdn_scan(u_ref[r, c], w_ref[r, c], a_ref[hh, r, :], qd_ref[r, c], kd_ref[r, c],
                                  e_ref[hh, ci][0:1], s)
                o_ref[r, c] = o
                out.append(sn)
            return tuple(out)

        ss = lax.fori_loop(0, nb, step, tuple(state[hh] for hh in range(SCAN_HB)))
        for hh, s in enumerate(ss):
            state[hh] = s

    return pl.pallas_call(
        body, name="gdn_scan_fwd", grid=grid,
        in_specs=[tok, tok, sq, tok, tok, per(8, 128)],
        out_specs=[tok, per(DA, DA)],
        out_shape=[jax.ShapeDtypeStruct((2, t, D), F32), jax.ShapeDtypeStruct((2, NA, t // CH, DA, DA), F32)],
        scratch_shapes=[pltpu.VMEM((SCAN_HB, DA, DA), F32)],
        compiler_params=_cparams(("parallel", "parallel", "arbitrary")),
    )(u, w, a, qd, kd, e)


def _gdn_scan_bwd(u, w, a, qd, kd, e, ssave, do):
    t = u.shape[1]
    tok, per, sq, shared, grid, nb = _scan_specs(t, DA, NA, SCAN_HB, -1)

    def body(u_ref, w_ref, a_ref, qd_ref, kd_ref, e_ref, s_ref, do_ref,
             du_ref, dw_ref, da_ref, dqd_ref, dkd_ref, de_ref, state):
        rev = pl.program_id(0) == 1

        @pl.when(pl.program_id(2) == 0)
        def _():
            state[...] = jnp.zeros_like(state)

        def step(i, dss):
            ci = jnp.where(rev, i, nb - 1 - i)
            r = _rows(ci)
            out = []
            for hh, ds in enumerate(dss):
                c = slice(hh * DA, (hh + 1) * DA)
                _, vjp = jax.vjp(_gdn_scan, u_ref[r, c], w_ref[r, c], a_ref[hh, r, :], qd_ref[r, c], kd_ref[r, c],
                                 e_ref[hh, ci][0:1], s_ref[hh, ci])
                du, dw, da, dqd, dkd, de, dsn = vjp((do_ref[r, c], ds))
                du_ref[r, c] = du
                dw_ref[r, c] = dw
                da_ref[hh, r, :] = da
                dqd_ref[r, c] = dqd
                dkd_ref[r, c] = dkd
                de_ref[hh, ci] = _first_row(de)
                out.append(dsn)
            return tuple(out)

        dss = lax.fori_loop(0, nb, step, tuple(state[hh] for hh in range(SCAN_HB)))
        for hh, ds in enumerate(dss):
            state[hh] = ds

    big = jax.ShapeDtypeStruct((2, t, D), F32)
    return pl.pallas_call(
        body, name="gdn_scan_bwd", grid=grid,
        in_specs=[tok, tok, sq, tok, tok, per(8, 128), per(DA, DA), shared(DA)],
        out_specs=[tok, tok, sq, tok, tok, per(8, 128)],
        out_shape=[big, big, jax.ShapeDtypeStruct((2, NA, t, CH), F32), big, big,
                   jax.ShapeDtypeStruct((2, NA, t // CH, 8, 128), F32)],
        scratch_shapes=[pltpu.VMEM((SCAN_HB, DA, DA), F32)],
        compiler_params=_cparams(("parallel", "parallel", "arbitrary")),
    )(u, w, a, qd, kd, e, ssave, do)


def _gdn_intra_bwd(qn, kn, vc, g2, b2, tinv, du, dw, da, dqd, dkd, de):
    t = qn.shape[0]
    n = t // CH

    def body(q_ref, k_ref, v_ref, g_ref, b_ref, t_ref, du_ref, dw_ref, da_ref, dqd_ref, dkd_ref, de_ref,
             dq_ref, dk_ref, dv_ref, dg_ref, db_ref):
        d = pl.program_id(1)
        cs = _Consts(d == 1)

        @pl.when(d == 0)
        def _():
            dq_ref[...] = jnp.zeros_like(dq_ref)
            dk_ref[...] = jnp.zeros_like(dk_ref)
            dv_ref[...] = jnp.zeros_like(dv_ref)

        def step(i, carry):
            r = _srows(i)
            tinv_c = t_ref[r, :]
            f = lambda q, k, v, g, bx: _gdn_intra(q, k, v, g, bx, tinv_c, cs)
            _, vjp = jax.vjp(f, q_ref[r, :], k_ref[r, :], v_ref[r, :], g_ref[r, :], b_ref[r, :])
            dq, dk, dv, dg, dbx = vjp((du_ref[r, :], dw_ref[r, :], da_ref[r, :], dqd_ref[r, :],
                                       dkd_ref[r, :], _chunk_rows(de_ref, i)))
            dq_ref[r, :] += dq
            dk_ref[r, :] += dk
            dv_ref[r, :] += dv
            dg_ref[r, :] = dg
            db_ref[r, :] = dbx
            return carry

        lax.fori_loop(0, t // SC, step, 0)

    head = pl.BlockSpec((t, DA), lambda h, d: (0, h))
    dh = pl.BlockSpec((None, t, DA), lambda h, d: (d, 0, h))
    sq = pl.BlockSpec((None, None, t, CH), lambda h, d: (d, h, 0, 0))
    tq = pl.BlockSpec((None, None, t, SC), lambda h, d: (d, h, 0, 0))
    full = jax.ShapeDtypeStruct((t, D), F32)
    big = jax.ShapeDtypeStruct((2, t, D), F32)
    return pl.pallas_call(
        body, name="gdn_intra_bwd", grid=(NA, 2),
        in_specs=[head, head, head, dh, dh, tq, dh, dh, sq, dh, dh,
                  pl.BlockSpec((None, None, n, 8, 128), lambda h, d: (d, h, 0, 0, 0))],
        out_specs=[head, head, head, dh, dh],
        out_shape=[full, full, full, big, big],
        compiler_params=_cparams(("arbitrary", "arbitrary")),
    )(qn, kn, vc, g2, b2, tinv, du, dw, da, dqd, dkd, de)


def _gla_specs(t, order):
    ix = (lambda d, h: (d, h)) if order == "dh" else (lambda h, d: (d, h))

    def mk(fn):
        return lambda a, b: fn(*ix(a, b))
    q = pl.BlockSpec((t, DKB), mk(lambda d, h: (0, 32 + h)))
    k = pl.BlockSpec((t, DKB), mk(lambda d, h: (0, 36 + h)))
    v = pl.BlockSpec((t, DVB), mk(lambda d, h: (0, 20 + h)))
    dk = pl.BlockSpec((None, t, DKB), mk(lambda d, h: (d, 0, h)))
    dv = pl.BlockSpec((None, t, DVB), mk(lambda d, h: (d, 0, h)))
    e = pl.BlockSpec((None, None, t // CH, 8, 128), mk(lambda d, h: (d, h, 0, 0, 0)))
    s = pl.BlockSpec((None, None, t // CH, DVB, DKB), mk(lambda d, h: (d, h, 0, 0, 0)))
    return q, k, v, dk, dv, e, s


def _gla_intra_fwd(p, gk):
    t = p.shape[0]
    n = t // CH

    def body(q_ref, k_ref, v_ref, g_ref, qg_ref, kd_ref, in_ref, e_ref):
        cs = _Consts(pl.program_id(0) == 1)

        def step(i, carry):
            r = _srows(i)
            qg, kd, intra, el = _gla_intra(q_ref[r, :], k_ref[r, :], v_ref[r, :], g_ref[r, :], cs)
            qg_ref[r, :] = qg
            kd_ref[r, :] = kd
            in_ref[r, :] = intra
            for c in range(SC // CH):
                e_ref[(SC // CH) * i + c] = el[c * CH:c * CH + 8]
            return carry

        lax.fori_loop(0, t // SC, step, 0)

    q, k, v, dk, dv, e, _ = _gla_specs(t, "dh")
    return pl.pallas_call(
        body, name="gla_intra_fwd", grid=(2, NB),
        in_specs=[q, k, v, dk], out_specs=[dk, dk, dv, e],
        out_shape=[jax.ShapeDtypeStruct((2, t, NB * DKB), F32), jax.ShapeDtypeStruct((2, t, NB * DKB), F32),
                   jax.ShapeDtypeStruct((2, t, D), F32), jax.ShapeDtypeStruct((2, NB, n, 8, 128), F32)],
        compiler_params=_cparams(("parallel", "parallel")),
    )(p, p, p, gk)


GLA_HB = 2


def _gla_v_spec(t, along):
    nt = t // SCAN_TB

    def tmap(d, tt):
        fwd = tt + d * (nt - 1 - 2 * tt)
        return fwd if along > 0 else nt - 1 - fwd

    return pl.BlockSpec((SCAN_TB, GLA_HB * DVB), lambda d, h, tt: (tmap(d, tt), 5120 // (GLA_HB * DVB) + h))


def _gla_scan_fwd(p, qg, kd, intra, e):
    t = p.shape[0]
    tokk, per, _, _, grid, nb = _scan_specs(t, DKB, NB, GLA_HB, +1)
    tokv = _scan_specs(t, DVB, NB, GLA_HB, +1)[0]

    def body(v_ref, qg_ref, kd_ref, in_ref, e_ref, o_ref, s_ref, state):
        rev = pl.program_id(0) == 1

        @pl.when(pl.program_id(2) == 0)
        def _():
            state[...] = jnp.zeros_like(state)

        def step(i, sts):
            ci = jnp.where(rev, nb - 1 - i, i)
            r = _rows(ci)
            out = []
            for hh, st in enumerate(sts):
                ck = slice(hh * DKB, (hh + 1) * DKB)
                cv = slice(hh * DVB, (hh + 1) * DVB)
                s_ref[hh, ci] = st
                o, stn = _gla_scan(qg_ref[r, ck], kd_ref[r, ck], v_ref[r, cv], e_ref[hh, ci][0:1], st)
                o_ref[r, cv] = o + in_ref[r, cv]
                out.append(stn)
            return tuple(out)

        sts = lax.fori_loop(0, nb, step, tuple(state[hh] for hh in range(GLA_HB)))
        for hh, st in enumerate(sts):
            state[hh] = st

    return pl.pallas_call(
        body, name="gla_scan_fwd", grid=grid,
        in_specs=[_gla_v_spec(t, +1), tokk, tokk, tokv, per(8, 128)], out_specs=[tokv, per(DVB, DKB)],
        out_shape=[jax.ShapeDtypeStruct((2, t, D), F32), jax.ShapeDtypeStruct((2, NB, t // CH, DVB, DKB), F32)],
        scratch_shapes=[pltpu.VMEM((GLA_HB, DVB, DKB), F32)],
        compiler_params=_cparams(("parallel", "parallel", "arbitrary")),
    )(p, qg, kd, intra, e)


def _gla_scan_bwd(p, qg, kd, e, ssave, do):
    t = p.shape[0]
    tokk, per, _, shared, grid, nb = _scan_specs(t, DKB, NB, GLA_HB, -1)
    tokv = _scan_specs(t, DVB, NB, GLA_HB, -1)[0]

    def body(v_ref, qg_ref, kd_ref, e_ref, s_ref, do_ref, dqg_ref, dkd_ref, dv_ref, de_ref, state):
        rev = pl.program_id(0) == 1

        @pl.when(pl.program_id(2) == 0)
        def _():
            state[...] = jnp.zeros_like(state)

        def step(i, dsts):
            ci = jnp.where(rev, i, nb - 1 - i)
            r = _rows(ci)
            out = []
            for hh, dst in enumerate(dsts):
                ck = slice(hh * DKB, (hh + 1) * DKB)
                cv = slice(hh * DVB, (hh + 1) * DVB)
                _, vjp = jax.vjp(_gla_scan, qg_ref[r, ck], kd_ref[r, ck], v_ref[r, cv], e_ref[hh, ci][0:1],
                                 s_ref[hh, ci])
                dqg, dkd, dv, de, dstn = vjp((do_ref[r, cv], dst))
                dqg_ref[r, ck] = dqg
                dkd_ref[r, ck] = dkd
                dv_ref[r, cv] = dv
                de_ref[hh, ci] = _first_row(de)
                out.append(dstn)
            return tuple(out)

        dsts = lax.fori_loop(0, nb, step, tuple(state[hh] for hh in range(GLA_HB)))
        for hh, dst in enumerate(dsts):
            state[hh] = dst

    return pl.pallas_call(
        body, name="gla_scan_bwd", grid=grid,
        in_specs=[_gla_v_spec(t, -1), tokk, tokk, per(8, 128), per(DVB, DKB), shared(DVB)],
        out_specs=[tokk, tokk, tokv, per(8, 128)],
        out_shape=[jax.ShapeDtypeStruct((2, t, NB * DKB), F32), jax.ShapeDtypeStruct((2, t, NB * DKB), F32),
                   jax.ShapeDtypeStruct((2, t, D), F32), jax.ShapeDtypeStruct((2, NB, t // CH, 8, 128), F32)],
        scratch_shapes=[pltpu.VMEM((GLA_HB, DVB, DKB), F32)],
        compiler_params=_cparams(("parallel", "parallel", "arbitrary")),
    )(p, qg, kd, e, ssave, do)


def _gla_intra_bwd(p, gk, dqg, dkd, dvs, de, do):
    t = p.shape[0]
    n = t // CH

    def body(q_ref, k_ref, v_ref, g_ref, dqg_ref, dkd_ref, dvs_ref, de_ref, do_ref,
             dq_ref, dk_ref, dv_ref, dg_ref):
        d = pl.program_id(1)
        cs = _Consts(d == 1)

        @pl.when(d == 0)
        def _():
            dq_ref[...] = jnp.zeros_like(dq_ref)
            dk_ref[...] = jnp.zeros_like(dk_ref)
            dv_ref[...] = jnp.zeros_like(dv_ref)

        def step(i, carry):
            r = _srows(i)
            f = lambda q, k, v, g: _gla_intra(q, k, v, g, cs)
            _, vjp = jax.vjp(f, q_ref[r, :], k_ref[r, :], v_ref[r, :], g_ref[r, :])
            dq, dk, dv, dg = vjp((dqg_ref[r, :], dkd_ref[r, :], do_ref[r, :], _chunk_rows(de_ref, i)))
            dq_ref[r, :] += dq
            dk_ref[r, :] += dk
            dv_ref[r, :] += dv + dvs_ref[r, :]
            dg_ref[r, :] = dg
            return carry

        lax.fori_loop(0, t // SC, step, 0)

    q, k, v, dk, dv, e_s, _ = _gla_specs(t, "hd")
    hk = pl.BlockSpec((t, DKB), lambda h, d: (0, h))
    hv = pl.BlockSpec((t, DVB), lambda h, d: (0, h))
    return pl.pallas_call(
        body, name="gla_intra_bwd", grid=(NB, 2),
        in_specs=[q, k, v, dk, dk, dk, dv, e_s, hv],
        out_specs=[hk, hk, hv, dk],
        out_shape=[jax.ShapeDtypeStruct((t, NB * DKB), F32), jax.ShapeDtypeStruct((t, NB * DKB), F32),
                   jax.ShapeDtypeStruct((t, D), F32), jax.ShapeDtypeStruct((2, t, NB * DKB), F32)],
        compiler_params=_cparams(("arbitrary", "arbitrary")),
    )(p, p, p, gk, dqg, dkd, dvs, de, do)


def _seg_gate(o, z, w):
    return _rms(o, w) * _silu(z)


def _seg_merge(ya, yb, ga, gb):
    return _sigmoid(ga) * ya + _sigmoid(gb) * yb


def _seg_loss(out, x, tgt, w):
    err = x + _rms(out, w) - tgt
    return 0.5 * jnp.sum(jnp.mean(err * err, axis=-1, keepdims=True), axis=0, keepdims=True)


def _post(oa2, ob2, p, x, tgt, gdn_w, gla_w, lnpost, w3, tm=128):
    t = x.shape[0]

    def body(oa_ref, ob_ref, z_ref, gb_ref, ga_ref, gB_ref, x_ref, t_ref, aw_ref, bw_ref, lw_ref, w_ref,
             loss_ref, doa_ref, dob_ref, dz_ref, dgb_ref, dga_ref, dgB_ref, dy_ref,
             dw_ref, daw_ref, dbw_ref, dlw_ref):
        first = pl.program_id(0) == 0
        oa = oa_ref[0] + oa_ref[1]
        ob = ob_ref[0] + ob_ref[1]
        z, gb = z_ref[...], gb_ref[...]
        aw, bw = aw_ref[...], bw_ref[...]
        rs = D // NSHARD

        def mat(a, m):
            return sum(jnp.dot(a[:, s * rs:(s + 1) * rs], w_ref[s, m], preferred_element_type=F32)
                       for s in range(NSHARD))

        def mat_t(g, m):
            return jnp.concatenate([_bdot(g, w_ref[s, m], 1, 1) for s in range(NSHARD)], axis=1)

        def add_dw(a, g, m):
            for s in range(NSHARD):
                dw_ref[s, m] += _bdot(a[:, s * rs:(s + 1) * rs], g, 0, 0)

        pa = [jax.vjp(_seg_gate, oa[:, h * DA:(h + 1) * DA], z[:, h * DA:(h + 1) * DA], aw) for h in range(NA)]
        pb = [jax.vjp(_seg_gate, ob[:, h * DVB:(h + 1) * DVB], gb[:, h * DVB:(h + 1) * DVB], bw)
              for h in range(NB)]
        a1 = jnp.concatenate([v for v, _ in pa], axis=1).astype(BF16)
        a2 = jnp.concatenate([v for v, _ in pb], axis=1).astype(BF16)
        ya = mat(a1, 0)
        yb = mat(a2, 1)
        merged, vjp_m = jax.vjp(_seg_merge, ya, yb, ga_ref[...], gB_ref[...])
        mb = merged.astype(BF16)
        out = mat(mb, 2)
        loss, vjp_l = jax.vjp(_seg_loss, out, x_ref[...], t_ref[...], lw_ref[...])
        dout, dyres, _, dlw = vjp_l(jnp.ones((1, 1), F32))
        dy_ref[...] = dyres
        doutb = dout.astype(BF16)
        dmerged = mat_t(doutb, 2)
        dya, dyb, dga, dgB = vjp_m(dmerged)
        dga_ref[...] = dga.astype(BF16)
        dgB_ref[...] = dgB.astype(BF16)
        dyab, dybb = dya.astype(BF16), dyb.astype(BF16)
        da1 = mat_t(dyab, 0)
        da2 = mat_t(dybb, 1)

        daw = jnp.zeros_like(aw)
        for h in range(NA):
            sl = slice(h * DA, (h + 1) * DA)
            do, dz, dw = pa[h][1](da1[:, sl])
            doa_ref[:, sl] = do
            dz_ref[:, sl] = dz.astype(BF16)
            daw = daw + dw
        dbw = jnp.zeros_like(bw)
        for h in range(NB):
            sl = slice(h * DVB, (h + 1) * DVB)
            do, dg, dw = pb[h][1](da2[:, sl])
            dob_ref[:, sl] = do
            dgb_ref[:, sl] = dg.astype(BF16)
            dbw = dbw + dw

        @pl.when(first)
        def _():
            loss_ref[...] = jnp.zeros_like(loss_ref)
            dw_ref[...] = jnp.zeros_like(dw_ref)
            daw_ref[...] = jnp.zeros_like(daw_ref)
            dbw_ref[...] = jnp.zeros_like(dbw_ref)
            dlw_ref[...] = jnp.zeros_like(dlw_ref)

        loss_ref[...] += jnp.broadcast_to(loss, loss_ref.shape)
        add_dw(a1, dyab, 0)
        add_dw(a2, dybb, 1)
        add_dw(mb, doutb, 2)
        daw_ref[...] += jnp.broadcast_to(daw, daw_ref.shape)
        dbw_ref[...] += jnp.broadcast_to(dbw, dbw_ref.shape)
        dlw_ref[...] += jnp.broadcast_to(dlw, dlw_ref.shape)

    two = pl.BlockSpec((2, tm, D), lambda i: (0, i, 0))
    pcol = lambda c: pl.BlockSpec((tm, D), lambda i: (i, c))
    tok = pl.BlockSpec((tm, D), lambda i: (i, 0))
    row = lambda n: pl.BlockSpec((1, n), lambda i: (0, 0))
    row8 = lambda n: pl.BlockSpec((8, n), lambda i: (0, 0))
    once = pl.Buffered(1)
    tokf = jax.ShapeDtypeStruct((t, D), F32)
    tokb = jax.ShapeDtypeStruct((t, D), BF16)
    wspec = pl.BlockSpec((NSHARD, 3, D // NSHARD, D), lambda i: (0, 0, 0, 0), pipeline_mode=once)
    return pl.pallas_call(
        body, name="post", grid=(t // tm,),
        in_specs=[two, two, pcol(3), pcol(6), pcol(7), pcol(8), tok, tok, row(DA), row(DVB), row(D), wspec],
        out_specs=[row8(128), tok, tok, tok, tok, tok, tok, tok, wspec, row8(DA), row8(DVB), row8(D)],
        out_shape=[jax.ShapeDtypeStruct((8, 128), F32), tokf, tokf, tokb, tokb, tokb, tokb, tokf,
                   jax.ShapeDtypeStruct((NSHARD, 3, D // NSHARD, D), F32),
                   jax.ShapeDtypeStruct((8, DA), F32), jax.ShapeDtypeStruct((8, DVB), F32),
                   jax.ShapeDtypeStruct((8, D), F32)],
        compiler_params=_cparams(("arbitrary",), vmem_mb=56),
    )(oa2, ob2, p, p, p, p, x, tgt, gdn_w, gla_w, lnpost, w3)


def _adam_math(w, g, m, v):
    nm = B1 * m + (1.0 - B1) * g
    nv = B2 * v + (1.0 - B2) * (g * g)
    m_hat = nm / (1.0 - B1 ** STEP)
    v_hat = nv / (1.0 - B2 ** STEP)
    return -LR * (m_hat / (jnp.sqrt(v_hat) + ADAM_EPS) + WD * w), nm, nv


SMALL_SLOTS = (("ln_pre_w", 0, 1024, 0), ("a_log_fwd", 1024, 8, 0), ("a_log_bwd", 1024, 8, 8),
               ("dt_bias_fwd", 1152, 8, 0), ("dt_bias_bwd", 1152, 8, 8), ("gdn_norm_w", 1280, 128, 0),
               ("gk_b2_fwd", 1408, 512, 0), ("gk_b2_bwd", 1920, 512, 0), ("gla_norm_w", 2432, 256, 0),
               ("ln_post_w", 2688, 1024, 0))
SMALL_W = 3712


def _adam_small(gsum, ws, ms, vs):
    nw = len(SMALL_SLOTS)

    def body(g_ref, *refs):
        w_refs, m_refs, v_refs, outs = refs[0:nw], refs[nw:2 * nw], refs[2 * nw:3 * nw], refs[3 * nw:]
        for i, (_, off, n, shift) in enumerate(SMALL_SLOTS):
            slot = g_ref[0:1, off:off + max(n, 128)]
            if shift:
                slot = pltpu.roll(slot, 128 - shift, 1)
            g = slot[:, 0:n]
            d, nm, nv = _adam_math(w_refs[i][...], g, m_refs[i][...], v_refs[i][...])
            for k, val in enumerate((g, d, nm, nv)):
                outs[4 * i + k][...] = val

    vm = pl.BlockSpec(memory_space=pltpu.VMEM)
    res = pl.pallas_call(
        body, name="adam_small", in_specs=[vm] * (1 + 3 * nw), out_specs=[vm] * (4 * nw),
        out_shape=[jax.ShapeDtypeStruct((1, n), F32) for _, _, n, _ in SMALL_SLOTS for _ in range(4)],
    )(gsum, *ws, *ms, *vs)
    return {name: res[4 * i:4 * i + 4] for i, (name, _, _, _) in enumerate(SMALL_SLOTS)}


def _adam(w, mine, got, m, v, tr, tile0=0, name=""):
    rows, cols = w.shape
    nh = mine.shape[0] // tr

    def body(c_ref, w_ref, a_ref, b_ref, m_ref, v_ref, g_ref, d_ref, nm_ref, nv_ref):
        half = (tile0 + pl.program_id(0)) // nh
        g = jnp.where(half == c_ref[0], a_ref[...], b_ref[...])
        d, nm, nv = _adam_math(w_ref[...], g, m_ref[...], v_ref[...])
        g_ref[...] = g
        d_ref[...] = d
        nm_ref[...] = nm
        nv_ref[...] = nv

    blk = pl.BlockSpec((tr, cols), lambda i, cc: (i, 0))
    half = pl.BlockSpec((tr, cols), lambda i, cc: ((tile0 + i) % nh, 0))
    shp = jax.ShapeDtypeStruct((rows, cols), F32)
    return pl.pallas_call(
        body, name=f"adam_{name}{rows}x{cols}",
        grid_spec=pltpu.PrefetchScalarGridSpec(
            num_scalar_prefetch=1, grid=(rows // tr,),
            in_specs=[blk, half, half, blk, blk], out_specs=[blk] * 4),
        out_shape=[shp] * 4,
        compiler_params=_cparams(("parallel",)),
    )(lax.axis_index("c").reshape(1), w, mine, got, m, v)


def _sum_cast(own, got):
    ns, _, r, c = own.shape
    tr = r // 4 if r >= 64 else r

    def body(c_ref, a_ref, b_ref, f_ref, h_ref):
        s = a_ref[...] + b_ref[...]
        f_ref[...] = s
        h_ref[...] = s.astype(BF16)

    return pl.pallas_call(
        body, name=f"sum_cast_{r}x{c}",
        grid_spec=pltpu.PrefetchScalarGridSpec(
            num_scalar_prefetch=1, grid=(ns, r // tr),
            in_specs=[pl.BlockSpec((None, None, tr, c), lambda s, i, cc: (s, cc[0], i, 0)),
                      pl.BlockSpec((None, tr, c), lambda s, i, cc: (s, i, 0))],
            out_specs=[pl.BlockSpec((None, tr, c), lambda s, i, cc: (s, i, 0)),
                       pl.BlockSpec((None, tr, c), lambda s, i, cc: (s, i, 0))]),
        out_shape=[jax.ShapeDtypeStruct((ns, r, c), F32), jax.ShapeDtypeStruct((ns, r, c), BF16)],
        compiler_params=_cparams(("parallel", "parallel")),
    )(lax.axis_index("c").reshape(1), own, got)


def _sum4(mine, got):
    _, r, c = mine.shape
    tr = r // 4 if r >= 64 else r

    def body(s_ref, a_ref, g_ref, o_ref):
        acc = a_ref[...] + g_ref[0].astype(F32)
        acc = acc + g_ref[1].astype(F32)
        o_ref[...] = acc + g_ref[2].astype(F32)

    shard = (2 * lax.axis_index("x") + lax.axis_index("y")).reshape(1)
    return pl.pallas_call(
        body, name=f"sum4_{r}x{c}",
        grid_spec=pltpu.PrefetchScalarGridSpec(
            num_scalar_prefetch=1, grid=(r // tr,),
            in_specs=[pl.BlockSpec((None, tr, c), lambda i, ss: (ss[0], i, 0)),
                      pl.BlockSpec((3, tr, c), lambda i, ss: (0, i, 0))],
            out_specs=pl.BlockSpec((tr, c), lambda i, ss: (i, 0))),
        out_shape=jax.ShapeDtypeStruct((r, c), F32),
        compiler_params=_cparams(("parallel",)),
    )(shard, mine, got)


def _place():
    x, y, c = lax.axis_index("x"), lax.axis_index("y"), lax.axis_index("c")
    chips = [(1 - x, y), (x, 1 - y), (1 - x, 1 - y)]
    return x, y, c, chips


def _gather_weights(parts):
    npart = len(parts)

    def body(*refs):
        ins, outs = refs[:npart], refs[npart:2 * npart]
        send_sems, recv_sems = refs[2 * npart:]
        x, y, c, chips = _place()
        sibling = (x, y, 1 - c)
        mine = 2 * x + y

        def remote(k, p, shard, half, to, src=None):
            dst = outs[p].at[shard, half]
            return pltpu.make_async_remote_copy(
                src_ref=dst if src is None else src, dst_ref=dst,
                send_sem=send_sems.at[k], recv_sem=recv_sems.at[k], device_id=to, device_id_type=MESH)

        first = [remote(j * npart + p, p, mine, c, (*chip, c), src=ins[p].at[c])
                 for j, chip in enumerate(chips) for p in range(npart)]
        for cp in first:
            cp.start()
        passed = []
        for j, (cx, cy) in enumerate(chips):
            for p in range(npart):
                remote(j * npart + p, p, 2 * cx + cy, c, (x, y, c)).wait_recv()
                fw = remote((3 + j) * npart + p, p, 2 * cx + cy, c, sibling)
                fw.start()
                passed.append(fw)
        for j, (cx, cy) in enumerate(chips):
            for p in range(npart):
                remote((3 + j) * npart + p, p, 2 * cx + cy, 1 - c, (x, y, c)).wait_recv()
        for cp in first + passed:
            cp.wait_send()

    got = pl.pallas_call(
        body, name="gather_weights",
        in_specs=[ANY] * npart, out_specs=[ANY] * npart,
        out_shape=[jax.ShapeDtypeStruct((NSHARD,) + a.shape, a.dtype) for a in parts],
        scratch_shapes=[pltpu.SemaphoreType.DMA((6 * npart,)), pltpu.SemaphoreType.DMA((6 * npart,))],
    )(*parts)
    mine = 2 * lax.axis_index("x") + lax.axis_index("y")
    return [lax.dynamic_update_index_in_dim(g, a, mine, 0) for g, a in zip(got, parts)]


def _swap_halves(parts):
    npart = len(parts)

    def body(*refs):
        ins, outs = refs[:npart], refs[npart:2 * npart]
        send_sems, recv_sems = refs[2 * npart:]
        x, y, c, _ = _place()
        cps = [pltpu.make_async_remote_copy(
            src_ref=ins[p].at[s, 1 - c], dst_ref=outs[p].at[s],
            send_sem=send_sems.at[s * npart + p], recv_sem=recv_sems.at[s * npart + p],
            device_id=(x, y, 1 - c), device_id_type=MESH) for s in range(NSHARD) for p in range(npart)]
        for cp in cps:
            cp.start()
        for cp in cps:
            cp.wait()

    return pl.pallas_call(
        body, name="swap_halves", in_specs=[ANY] * npart, out_specs=[ANY] * npart,
        out_shape=[jax.ShapeDtypeStruct((NSHARD,) + a.shape[2:], a.dtype) for a in parts],
        scratch_shapes=[pltpu.SemaphoreType.DMA((NSHARD * npart,)), pltpu.SemaphoreType.DMA((NSHARD * npart,))],
    )(*parts)


def _scatter_shards(parts):
    npart = len(parts)

    def body(*refs):
        ins, outs = refs[:npart], refs[npart:2 * npart]
        send_sems, recv_sems = refs[2 * npart:]
        x, y, c, chips = _place()
        cps = [pltpu.make_async_remote_copy(
            src_ref=ins[p].at[2 * cx + cy], dst_ref=outs[p].at[j],
            send_sem=send_sems.at[j * npart + p], recv_sem=recv_sems.at[j * npart + p],
            device_id=(cx, cy, c), device_id_type=MESH)
            for j, (cx, cy) in enumerate(chips) for p in range(npart)]
        for cp in cps:
            cp.start()
        for cp in cps:
            cp.wait()

    return pl.pallas_call(
        body, name="scatter_shards", in_specs=[ANY] * npart, out_specs=[ANY] * npart,
        out_shape=[jax.ShapeDtypeStruct((3,) + a.shape[1:], a.dtype) for a in parts],
        scratch_shapes=[pltpu.SemaphoreType.DMA((3 * npart,)), pltpu.SemaphoreType.DMA((3 * npart,))],
    )(*parts)


HBM = pl.BlockSpec(memory_space=pltpu.HBM)
SEM = pl.BlockSpec(memory_space=pltpu.SEMAPHORE)
EFFECT = pltpu.SideEffectType.DATAFLOW_SIDE_EFFECTING


def _scatter_copies(srcs, lands, send_sems, recv_sems, waiting):
    x, y, c, chips = _place()
    n = len(srcs)
    return [pltpu.make_async_remote_copy(
        src_ref=srcs[p].at[2 * cx + cy], dst_ref=lands[p].at[j],
        send_sem=send_sems.at[j * n + p], recv_sem=recv_sems.at[j * n + p],
        device_id=(cx, cy, c), device_id_type=MESH)
        for j, (cx, cy) in enumerate(chips) for p in range(n)]


def _proj_copies(srcs, lands, send_sems, recv_sems, waiting):
    x, y, c, chips = _place()
    mine = 2 * x + y
    return [pltpu.make_async_remote_copy(
        src_ref=srcs[0].at[c], dst_ref=lands[0].at[mine, c],
        send_sem=send_sems.at[2 * j + to], recv_sem=recv_sems.at[2 * j + (to if waiting else c)],
        device_id=(cx, cy, to), device_id_type=MESH)
        for j, (cx, cy) in enumerate(chips) for to in range(2)]


def _start_copies(copies, nsem, parts, lands, name, after=None):
    n = len(parts)
    extra = [] if after is None else [after]

    def body(*refs):
        outs = refs[2 * n + len(extra):]
        for cp in copies(refs[:n], refs[n:2 * n], outs[0], outs[1], False):
            cp.start()
        outs[-1][...] = jnp.zeros_like(outs[-1])

    res = pl.pallas_call(
        body, name=name,
        out_shape=(pltpu.SemaphoreType.DMA((nsem,)), pltpu.SemaphoreType.DMA((nsem,)),
                   *[pltpu.HBM(a.shape, a.dtype) for a in parts], *[pltpu.HBM(a.shape, a.dtype) for a in lands],
                   jax.ShapeDtypeStruct((8, 128), F32)),
        in_specs=[HBM] * (2 * n) + [ANY] * len(extra),
        out_specs=(SEM, SEM, *[HBM] * (2 * n), pl.BlockSpec(memory_space=pltpu.VMEM)),
        input_output_aliases={i: 2 + i for i in range(2 * n)},
        compiler_params=pltpu.CompilerParams(has_side_effects=EFFECT),
    )(*[pltpu.with_memory_space_constraint(a, pltpu.HBM) for a in parts],
      *[pltpu.with_memory_space_constraint(a, pltpu.HBM) for a in lands], *extra)
    return res[0], res[1], res[2:2 + n], res[2 + n:2 + 2 * n], res[-1]


def _wait_copies(copies, started, after, name):
    send_sems, recv_sems, srcs, lands, _ = started
    n = len(srcs)

    def body(*refs):
        for cp in copies(refs[:n], refs[n:2 * n], refs[2 * n], refs[2 * n + 1], True):
            cp.wait_send()
            cp.wait_recv()

    res = pl.pallas_call(
        body, name=name,
        out_shape=tuple(pltpu.HBM(a.shape, a.dtype) for a in (*srcs, *lands)),
        in_specs=[HBM] * (2 * n) + [SEM, SEM, ANY], out_specs=tuple([HBM] * (2 * n)),
        input_output_aliases={i: i for i in range(2 * n)},
        compiler_params=pltpu.CompilerParams(has_side_effects=EFFECT),
    )(*srcs, *lands, send_sems, recv_sems, after)
    return res[n:]


def _join_halves(parts):
    npart = len(parts)

    def body(*refs):
        ins, outs = refs[:npart], refs[npart:2 * npart]
        send_sems, recv_sems = refs[2 * npart:]
        x, y, c, _ = _place()
        cps = [pltpu.make_async_remote_copy(
            src_ref=ins[p], dst_ref=outs[p], send_sem=send_sems.at[p], recv_sem=recv_sems.at[p],
            device_id=(x, y, 1 - c), device_id_type=MESH) for p in range(npart)]
        for cp in cps:
            cp.start()
        for cp in cps:
            cp.wait()

    return pl.pallas_call(
        body, name="join_halves", in_specs=[ANY] * npart, out_specs=[ANY] * npart,
        out_shape=[jax.ShapeDtypeStruct(a.shape, a.dtype) for a in parts],
        scratch_shapes=[pltpu.SemaphoreType.DMA((npart,)), pltpu.SemaphoreType.DMA((npart,))],
    )(*parts)


def _allreduce_small(v):
    r, ncol = v.shape

    def body(v_ref, o_ref, buf, send_sems, recv_sems):
        x, y, c, _ = _place()
        me = 4 * x + 2 * y + c
        buf[me] = v_ref[...]
        cps = []
        for k in range(1, 8):
            px, py, pc = x ^ (k >> 2), y ^ ((k >> 1) & 1), c ^ (k & 1)
            cps.append(pltpu.make_async_remote_copy(
                src_ref=v_ref, dst_ref=buf.at[me], send_sem=send_sems.at[k - 1], recv_sem=recv_sems.at[k - 1],
                device_id=(px, py, pc), device_id_type=MESH))
        for cp in cps:
            cp.start()
        for k in range(1, 8):
            px, py, pc = x ^ (k >> 2), y ^ ((k >> 1) & 1), c ^ (k & 1)
            pltpu.make_async_remote_copy(
                src_ref=v_ref, dst_ref=buf.at[4 * px + 2 * py + pc], send_sem=send_sems.at[k - 1],
                recv_sem=recv_sems.at[k - 1], device_id=(px, py, pc), device_id_type=MESH).wait_recv()
        for cp in cps:
            cp.wait_send()
        acc = buf[0]
        for d in range(1, 8):
            acc = acc + buf[d]
        o_ref[...] = acc

    return pl.pallas_call(
        body, name="allreduce_small",
        in_specs=[pl.BlockSpec(memory_space=pltpu.VMEM)], out_specs=pl.BlockSpec(memory_space=pltpu.VMEM),
        out_shape=jax.ShapeDtypeStruct((r, ncol), F32),
        scratch_shapes=[pltpu.VMEM((8, r, ncol), F32), pltpu.SemaphoreType.DMA((7,)), pltpu.SemaphoreType.DMA((7,))],
    )(v)


def _permute_cols(shards):
    w0, w1, w2, w3 = shards
    zeros = jnp.zeros((w0.shape[0], NPERM - 9280), w0.dtype)
    return jnp.concatenate([w0, w1[:, 0:1776], w1[:, 1808:2320], w2, w3[:, 0:240], w3[:, 272:2320],
                            w1[:, 1776:1808], w3[:, 240:272], zeros], axis=1)


def _unpermute_cols(g):
    s1 = jnp.concatenate([g[..., 2320:4096], g[..., 9216:9248], g[..., 4096:4608]], axis=-1)
    s3 = jnp.concatenate([g[..., 6928:7168], g[..., 9248:9280], g[..., 7168:9216]], axis=-1)
    return jnp.stack([g[..., 0:2320], s1, g[..., 4608:6928], s3], axis=0)


def _pack_shard_small(conv, w2f, w2b):
    top = jnp.pad(conv, ((0, 8 - conv.shape[0]), (0, 0)))
    mid = jnp.pad(jnp.concatenate([w2f, w2b], axis=1), ((0, 0), (0, 768 - 256)))
    return jnp.concatenate([top, mid, jnp.zeros((8, 768), conv.dtype)], axis=0)


def _unpack_shard_small(a):
    return a[0:5], a[8:24, 0:128], a[8:24, 128:256]


def kernel(x, ln_pre_w, w_in, conv_w, a_log_fwd, a_log_bwd, dt_bias_fwd, dt_bias_bwd, gdn_norm_w, w_proj_gdn, gk_w2_fwd, gk_b2_fwd, gk_w2_bwd, gk_b2_bwd, gla_norm_w, w_proj_gla, w_out, ln_post_w, loss_target, m_ln_pre_w, m_w_in, m_conv_w, m_a_log_fwd, m_a_log_bwd, m_dt_bias_fwd, m_dt_bias_bwd, m_gdn_norm_w, m_w_proj_gdn, m_gk_w2_fwd, m_gk_b2_fwd, m_gk_w2_bwd, m_gk_b2_bwd, m_gla_norm_w, m_w_proj_gla, m_w_out, m_ln_post_w, v_ln_pre_w, v_w_in, v_conv_w, v_a_log_fwd, v_a_log_bwd, v_dt_bias_fwd, v_dt_bias_bwd, v_gdn_norm_w, v_w_proj_gdn, v_gk_w2_fwd, v_gk_b2_fwd, v_gk_w2_bwd, v_gk_b2_bwd, v_gla_norm_w, v_w_proj_gla, v_w_out, v_ln_post_w):
    t = x.shape[1]
    x2, tgt = x[0], loss_target[0]

    win_l = w_in[0].astype(BF16).reshape(2, D // 2, SHW)
    proj_l = jnp.concatenate([w_proj_gdn[0], w_proj_gla[0], w_out[0]], axis=0).astype(BF16).reshape(2, 384, D)
    small_l = _pack_shard_small(conv_w[0], gk_w2_fwd[0], gk_w2_bwd[0]).reshape(2, 16, 768)
    win_g, small_g = _gather_weights([win_l, small_l])
    proj_started = _start_copies(_proj_copies, 6, [proj_l], [lax.empty((NSHARD, 2, 384, D), BF16)],
                                 "gather_proj_start", after=small_g)
    wperm = _permute_cols(win_g.reshape(NSHARD, D, SHW))
    small_g = small_g.reshape(NSHARD, 32, 768)
    convw = small_g[:, 0:8, :].transpose(1, 0, 2).reshape(8, 3 * D)
    w2f = small_g[:, 8:24, 0:128].transpose(1, 0, 2).reshape(16, 512)
    w2b = small_g[:, 8:24, 128:256].transpose(1, 0, 2).reshape(16, 512)
    w2f_pad = jnp.pad(w2f, ((32, 80), (0, 0)))
    w2b_pad = jnp.pad(w2b, ((48, 64), (0, 0)))
    alog_row = jnp.pad(jnp.concatenate([a_log_fwd, a_log_bwd], axis=1), ((0, 0), (0, 112)))
    dt_row = jnp.pad(jnp.concatenate([dt_bias_fwd, dt_bias_bwd], axis=1), ((0, 0), (0, 112)))

    p, h = _inproj(x2, ln_pre_w + proj_started[4][0:1, 0:1], wperm)
    qn, kn, vc = (_qkv_fwd(p, convw, kind) for kind in range(3))
    gsm, gk = _gates_fwd(p, alog_row, dt_row, w2f_pad, gk_b2_fwd, w2b_pad, gk_b2_bwd)
    g2, b2 = _gcum_fwd(gsm)
    u, w, at, qd, kd, el, tinv = _gdn_intra_fwd(qn, kn, vc, g2, b2)
    oa2, sa = _gdn_scan_fwd(u, w, at, qd, kd, el)
    qg, kdb, intra, elb = _gla_intra_fwd(p, gk)
    ob2, sb = _gla_scan_fwd(p, qg, kdb, intra, elb)

    (proj_land,) = _wait_copies(_proj_copies, proj_started, ob2, "gather_proj_wait")
    mine = 2 * lax.axis_index("x") + lax.axis_index("y")
    w3 = lax.dynamic_update_index_in_dim(proj_land, proj_l, mine, 0).reshape(NSHARD, 3, D // NSHARD, D)
    (loss8, doa, dob, dz, dgb, dga, dgB, dyres, dw3, dgdn_w, dgla_w, dlnpost) = _post(
        oa2, ob2, p, x2, tgt, gdn_norm_w, gla_norm_w, ln_post_w, w3)

    du, dw, dat, dqd, dkd, del_ = _gdn_scan_bwd(u, w, at, qd, kd, el, sa, doa)
    dqn, dkn, dvc, dg2, db2 = _gdn_intra_bwd(qn, kn, vc, g2, b2, tinv, du, dw, dat, dqd, dkd, del_)
    dgsm = _gcum_bwd(gsm, dg2, db2)
    dqg, dkdb, dvs, delb = _gla_scan_bwd(p, qg, kdb, elb, sb, dob)
    dqb, dkb, dvb, dgk = _gla_intra_bwd(p, gk, dqg, dkdb, dvs, delb, dob)
    (dps, dalog8, ddt8, dw2f_pad, db2f8, dw2b_pad, db2b8) = _gates_bwd(
        p, alog_row, dt_row, w2f_pad, gk_b2_fwd, w2b_pad, gk_b2_bwd, dgsm, dgk)
    dpre, dconv = zip(*[_qkv_bwd(p, convw, g, kind) for kind, g in enumerate((dqn, dkn, dvc))])

    pieces = (*dpre, dz, dqb, dkb, dvb, dgb, dga, dgB, dps)
    dwperm = _inproj_dw(h, pieces)

    g_in = _unpermute_cols(dwperm.reshape(2, D // 2, NPERM))
    g_proj = dw3.reshape(NSHARD, 2, 384, D)
    dconv_full = jnp.concatenate(dconv, axis=1)
    dw2f, dw2b = dw2f_pad[32:48], dw2b_pad[48:64]
    g_small = jnp.stack([_pack_shard_small(dconv_full[0:5, 768 * s:768 * (s + 1)],
                                           dw2f[:, 128 * s:128 * (s + 1)], dw2b[:, 128 * s:128 * (s + 1)])
                         for s in range(NSHARD)])
    g_small = g_small.reshape(NSHARD, 2, 16, 768)
    parts = [g_in, g_proj, g_small]
    got = _swap_halves(parts)
    sums = [_sum_cast(a, b) for a, b in zip(parts, got)]
    hbs = [hb for _, hb in sums]
    started = _start_copies(_scatter_copies, 3 * len(hbs), hbs,
                            [lax.empty((3,) + a.shape[1:], a.dtype) for a in hbs], "scatter_start")
    dx, dlnpre8 = _inproj_dx(pieces, wperm, x2, ln_pre_w + started[4][0:1, 0:1], dyres)
    landed = _wait_copies(_scatter_copies, started, dx, "scatter_wait")
    halves = [_sum4(f, g) for (f, _), g in zip(sums, landed)]
    theirs = _join_halves(halves)

    gsmall = _allreduce_small(jnp.concatenate(
        [dlnpre8, dalog8, ddt8, dgdn_w, db2f8, db2b8, dgla_w, dlnpost], axis=1))

    a_in = _adam(w_in[0], halves[0], theirs[0], m_w_in[0], v_w_in[0], 128, name="in")
    a_pr = [_adam(w[0], halves[1], theirs[1], m[0], v[0], 128, tile0=2 * i, name=f"proj{i}")
            for i, (w, m, v) in enumerate(((w_proj_gdn, m_w_proj_gdn, v_w_proj_gdn),
                                           (w_proj_gla, m_w_proj_gla, v_w_proj_gla), (w_out, m_w_out, v_w_out)))]
    a_ss = _adam(_pack_shard_small(conv_w[0], gk_w2_fwd[0], gk_w2_bwd[0]), halves[2], theirs[2],
                 _pack_shard_small(m_conv_w[0], m_gk_w2_fwd[0], m_gk_w2_bwd[0]),
                 _pack_shard_small(v_conv_w[0], v_gk_w2_fwd[0], v_gk_w2_bwd[0]), 16, name="small")
    smalls = dict(ln_pre_w=(ln_pre_w, m_ln_pre_w, v_ln_pre_w), a_log_fwd=(a_log_fwd, m_a_log_fwd, v_a_log_fwd),
                  a_log_bwd=(a_log_bwd, m_a_log_bwd, v_a_log_bwd),
                  dt_bias_fwd=(dt_bias_fwd, m_dt_bias_fwd, v_dt_bias_fwd),
                  dt_bias_bwd=(dt_bias_bwd, m_dt_bias_bwd, v_dt_bias_bwd),
                  gdn_norm_w=(gdn_norm_w, m_gdn_norm_w, v_gdn_norm_w),
                  gk_b2_fwd=(gk_b2_fwd, m_gk_b2_fwd, v_gk_b2_fwd), gk_b2_bwd=(gk_b2_bwd, m_gk_b2_bwd, v_gk_b2_bwd),
                  gla_norm_w=(gla_norm_w, m_gla_norm_w, v_gla_norm_w), ln_post_w=(ln_post_w, m_ln_post_w, v_ln_post_w))
    names = [name for name, _, _, _ in SMALL_SLOTS]
    small = _adam_small(gsmall, *([smalls[n][i] for n in names] for i in range(3)))

    def family(k):
        conv, w2f_, w2b_ = _unpack_shard_small(a_ss[k])
        s = {n: small[n][k] for n in names}
        return [s["ln_pre_w"], a_in[k][None], conv[None], s["a_log_fwd"], s["a_log_bwd"], s["dt_bias_fwd"],
                s["dt_bias_bwd"], s["gdn_norm_w"], a_pr[0][k][None], w2f_[None], s["gk_b2_fwd"], w2b_[None],
                s["gk_b2_bwd"], s["gla_norm_w"], a_pr[1][k][None], a_pr[2][k][None], s["ln_post_w"]]

    loss = lax.psum(loss8[0, 0], ("x", "y", "c"))
    return (loss, dx[None], *family(0), *family(1), *family(2), *family(3))
```

```python
import functools

import jax
import jax.numpy as jnp
from jax import lax
from jax.experimental import pallas as pl
from jax.experimental.pallas import tpu as pltpu

F32 = jnp.float32
BF16 = jnp.bfloat16
HI = lax.Precision.HIGHEST
MESH = pl.DeviceIdType.MESH

D = 1024
CH = 64
EPS = 1e-6
NA, DA = 8, 128
NB, DKB, DVB = 4, 128, 256
NSHARD = 4
SHW = 2320
NPERM = 9728
PS_BLOCK = 72
LR, B1, B2, ADAM_EPS, WD, STEP = 0.001, 0.9, 0.999, 1e-08, 0.01, 10

ANY = pl.BlockSpec(memory_space=pl.ANY)


def _cparams(sem=None, vmem_mb=48):
    return pltpu.CompilerParams(dimension_semantics=sem, vmem_limit_bytes=vmem_mb << 20)


def _bdot(a, b, ca, cb):
    return lax.dot_general(a.astype(BF16), b.astype(BF16), (((ca,), (cb,)), ((), ())),
                           preferred_element_type=F32)


@jax.custom_vjp
def mm(a, b):
    return _bdot(a, b, 1, 0)


def _mm_fwd(a, b):
    return _bdot(a, b, 1, 0), (a, b)


def _mm_bwd(res, g):
    a, b = res
    return _bdot(g, b, 1, 1), _bdot(a, g, 0, 0)


mm.defvjp(_mm_fwd, _mm_bwd)


@jax.custom_vjp
def mm_nt(a, b):
    return _bdot(a, b, 1, 1)


def _mm_nt_fwd(a, b):
    return _bdot(a, b, 1, 1), (a, b)


def _mm_nt_bwd(res, g):
    a, b = res
    return _bdot(g, b, 1, 0), _bdot(g, a, 0, 0)


mm_nt.defvjp(_mm_nt_fwd, _mm_nt_bwd)


@jax.custom_vjp
def mm_tn(a, b):
    return _bdot(a, b, 0, 0)


def _mm_tn_fwd(a, b):
    return _bdot(a, b, 0, 0), (a, b)


def _mm_tn_bwd(res, g):
    a, b = res
    return _bdot(b, g, 1, 1), _bdot(a, g, 1, 0)


mm_tn.defvjp(_mm_tn_fwd, _mm_tn_bwd)


def dot_hi(a, b):
    return lax.dot_general(a, b, (((1,), (0,)), ((), ())), precision=HI, preferred_element_type=F32)


def _split3(x):
    x1 = x.astype(BF16)
    r = x - x1.astype(F32)
    x2 = r.astype(BF16)
    return x1, x2, (r - x2.astype(F32)).astype(BF16)


def _cdot(c, x, cc, cx, c_first=True):
    parts = _split3(x)
    if c_first:
        return _bdot(c, parts[0], cc, cx) + _bdot(c, parts[1], cc, cx) + _bdot(c, parts[2], cc, cx)
    return _bdot(parts[0], c, cx, cc) + _bdot(parts[1], c, cx, cc) + _bdot(parts[2], c, cx, cc)


@jax.custom_vjp
def cmm(c, x):
    return _cdot(c, x, 1, 0)


def _cmm_fwd(c, x):
    return _cdot(c, x, 1, 0), c


def _cmm_bwd(c, g):
    return jnp.zeros_like(c), _cdot(c, g, 0, 0)


cmm.defvjp(_cmm_fwd, _cmm_bwd)


@jax.custom_vjp
def mmc(x, c):
    return _cdot(c, x, 0, 1, c_first=False)


def _mmc_fwd(x, c):
    return _cdot(c, x, 0, 1, c_first=False), c


def _mmc_bwd(c, g):
    return _cdot(c, g, 1, 1, c_first=False), jnp.zeros_like(c)


mmc.defvjp(_mmc_fwd, _mmc_bwd)


def _sigmoid(x):
    return 1.0 / (1.0 + jnp.exp(-x))


def _silu(x):
    return x * _sigmoid(x)


def _softplus(x):
    return jnp.maximum(x, 0.0) + jnp.log(1.0 + jnp.exp(-jnp.abs(x)))


def _rms(x, w):
    return x * lax.rsqrt(jnp.mean(x * x, axis=-1, keepdims=True) + EPS) * w


SC = 256


class _Consts:
    def __init__(self, rev):
        r = lax.broadcasted_iota(jnp.int32, (SC, SC), 0)
        c = lax.broadcasted_iota(jnp.int32, (SC, SC), 1)
        same = (r >> 6) == (c >> 6)
        a = jnp.where(rev, c, r)
        b = jnp.where(rev, r, c)
        self.incl = same & (a >= b)
        self.strict = same & (a > b)
        self.incl_f = self.incl.astype(F32)
        self.eye = (r == c).astype(F32)
        rows = lax.broadcasted_iota(jnp.int32, (SC, 1), 0)
        self.last_col = ((rows & (CH - 1)) == jnp.where(rev, 0, CH - 1)).astype(F32)
        rr = lax.broadcasted_iota(jnp.int32, (SC, CH), 0)
        cc = lax.broadcasted_iota(jnp.int32, (SC, CH), 1)
        self.fold = ((rr & (CH - 1)) == cc).astype(F32)


def _dot3(a, b, ca=1, cb=0):
    ah, bh = a.astype(BF16), b.astype(BF16)
    al, bl = (a - ah.astype(F32)).astype(BF16), (b - bh.astype(F32)).astype(BF16)
    return _bdot(ah, bh, ca, cb) + (_bdot(ah, bl, ca, cb) + _bdot(al, bh, ca, cb))


TRI_SPLIT_LEVELS = 2


def _tri_inv(low, eye):
    n = -low
    acc = eye + n
    p = n
    for level in range(5):
        dot = _dot3 if level < TRI_SPLIT_LEVELS else (lambda a, b: _bdot(a, b, 1, 0))
        p = dot(p, p)
        acc = acc + dot(acc, p)
    return acc


@jax.custom_vjp
def _solve2(low, rv, rk, tinv):
    x = _dot3(tinv, jnp.concatenate([rv, rk], axis=1))
    return x[:, :DA], x[:, DA:]


def _solve2_fwd(low, rv, rk, tinv):
    x = _dot3(tinv, jnp.concatenate([rv, rk], axis=1))
    return (x[:, :DA], x[:, DA:]), (x, tinv)


def _solve2_bwd(res, g):
    x, tinv = res
    drhs = _dot3(tinv, jnp.concatenate(g, axis=1), 0, 0)
    return -_dot3(drhs, x, 1, 1), drhs[:, :DA], drhs[:, DA:], jnp.zeros_like(tinv)


_solve2.defvjp(_solve2_fwd, _solve2_bwd)


def _chunk_last(x, cs):
    xs = (x * cs.last_col).reshape(SC // CH, CH, x.shape[1])
    return jnp.broadcast_to(jnp.sum(xs, axis=1, keepdims=True), xs.shape).reshape(x.shape)


def _gdn_decay(g, cs):
    gw = jnp.concatenate([g] * (SC // DA), axis=1)
    grow = jnp.sum(cs.eye * gw, axis=0, keepdims=True)
    return jnp.where(cs.incl, jnp.exp(jnp.where(cs.incl, gw - grow, 0.0)), 0.0)


def _gdn_intra(q, k, v, g, bx, tinv, cs):
    decay = _gdn_decay(g, cs)
    kb = k * bx
    low = jnp.where(cs.strict, mm_nt(kb, k) * decay, 0.0)
    eg = jnp.exp(g)
    made = tinv is None
    if made:
        tinv = _tri_inv(low, cs.eye)
    u, w = _solve2(low, v * bx, kb * eg, tinv)
    attn = mmc(mm_nt(q, k) * decay, cs.fold)
    qd = q * eg
    glast = _chunk_last(g, cs)
    kd = k * jnp.exp(glast - g)
    outs = (u, w, attn, qd, kd, jnp.exp(glast))
    return outs + (tinv,) if made else outs


def _gdn_scan(u, w, attn, qd, kd, el, s):
    vn = u - mm(w, s)
    o = mm(qd, s) + mm(attn, vn)
    sn = s * el + mm_tn(kd, vn)
    return o, sn


def _gla_intra(q, k, v, gk, cs):
    gc = cmm(cs.incl_f, gk)
    qg = q * (DKB ** -0.5) * jnp.exp(gc)
    kg = k * jnp.exp(-gc)
    attn = jnp.where(cs.incl, mm_nt(qg, kg), 0.0)
    intra = mm(attn, v)
    glast = _chunk_last(gc, cs)
    kd = k * jnp.exp(glast - gc)
    return qg, kd, intra, jnp.exp(glast)


def _gla_scan(qg, kd, v, el, st):
    o = mm_nt(qg, st)
    stn = st * el + mm_tn(v, kd)
    return o, stn


def _shift_rows(x, s):
    if s == 0:
        return x
    t = x.shape[0]
    rolled = pltpu.roll(x, (-s) % t, 0)
    rows = lax.broadcasted_iota(jnp.int32, x.shape, 0)
    return jnp.where((rows + s >= 0) & (rows + s < t), rolled, 0.0)


@jax.custom_vjp
def _conv5(x, w):
    acc = w[0:1] * _shift_rows(x, -2)
    for j in range(1, 5):
        acc = acc + w[j:j + 1] * _shift_rows(x, j - 2)
    return acc


def _conv5_fwd(x, w):
    return _conv5(x, w), (x, w)


def _conv5_bwd(res, g):
    x, w = res
    dx = w[0:1] * _shift_rows(g, 2)
    for j in range(1, 5):
        dx = dx + w[j:j + 1] * _shift_rows(g, 2 - j)
    rows = lax.broadcasted_iota(jnp.int32, w.shape, 0)
    dw = jnp.zeros_like(w)
    for j in range(5):
        dwj = jnp.sum(g * _shift_rows(x, j - 2), axis=0, keepdims=True)
        dw = dw + jnp.where(rows == j, dwj, 0.0)
    return dx, dw


_conv5.defvjp(_conv5_fwd, _conv5_bwd)


def _qkv_act(kind):
    def f(x, w):
        c = _silu(_conv5(x, w))
        if kind == 2:
            return c
        c = c * lax.rsqrt(jnp.sum(c * c, axis=-1, keepdims=True) + EPS)
        return c * (DA ** -0.5) if kind == 0 else c
    return f


def _inproj(x, lnw, wperm, tm=512, tn=512):
    t = x.shape[0]

    def body(x_ref, lnw_ref, w_ref, p_ref, h_ref, hbuf):
        @pl.when(pl.program_id(1) == 0)
        def _():
            hb = _rms(x_ref[...], lnw_ref[...]).astype(BF16)
            hbuf[...] = hb
            h_ref[...] = hb
        p_ref[...] = _bdot(hbuf[...], w_ref[...], 1, 1)

    return pl.pallas_call(
        body, name="inproj", grid=(t // tm, NPERM // tn),
        in_specs=[pl.BlockSpec((tm, D), lambda i, j: (i, 0)),
                  pl.BlockSpec((1, D), lambda i, j: (0, 0)),
                  pl.BlockSpec((tn, D), lambda i, j: (j, 0))],
        out_specs=[pl.BlockSpec((tm, tn), lambda i, j: (i, j)),
                   pl.BlockSpec((tm, D), lambda i, j: (i, 0))],
        out_shape=[jax.ShapeDtypeStruct((t, NPERM), F32),
                   jax.ShapeDtypeStruct((t, D), BF16)],
        scratch_shapes=[pltpu.VMEM((tm, D), BF16)],
        compiler_params=_cparams(("parallel", "arbitrary")),
    )(x, lnw, wperm)


DP_TILE = 512
DP_PIECES = ((0, 2), (2, 2), (4, 2), (6, 2), (8, 1), (9, 1), (10, 2), (12, 2), (14, 2), (16, 2), (18, 1))


def _piece_specs(tm, j_first):
    specs = []
    for j0, n in DP_PIECES:
        def imap(a, b, j0=j0, n=n):
            j, i = (a, b) if j_first else (b, a)
            inside = (j >= j0) & (j < j0 + n)
            return jnp.where(inside, i, 0), jnp.clip(j - j0, 0, n - 1)
        specs.append(pl.BlockSpec((tm, DP_TILE), imap))
    return specs


def _for_piece(j, refs, fn):
    for (j0, n), ref in zip(DP_PIECES, refs):
        @pl.when((j >= j0) & (j < j0 + n))
        def _(ref=ref):
            fn(ref[...])


def _inproj_dw(h, pieces, tm=512):
    t = h.shape[0]
    npc = len(pieces)

    def body(h_ref, *refs):
        dw_ref = refs[npc]

        @pl.when(pl.program_id(1) == 0)
        def _():
            dw_ref[...] = jnp.zeros_like(dw_ref)

        def add(dp):
            dw_ref[...] += _bdot(dp, h_ref[...], 0, 0)
        _for_piece(pl.program_id(0), refs[:npc], add)

    return pl.pallas_call(
        body, name="inproj_dw", grid=(NPERM // DP_TILE, t // tm),
        in_specs=[pl.BlockSpec((tm, D), lambda j, i: (i, 0))] + _piece_specs(tm, True),
        out_specs=pl.BlockSpec((DP_TILE, D), lambda j, i: (j, 0)),
        out_shape=jax.ShapeDtypeStruct((NPERM, D), F32),
        compiler_params=_cparams(("parallel", "arbitrary")),
    )(h, *pieces)


def _inproj_dx(pieces, wperm, x, lnw, dyres, tm=512):
    t = x.shape[0]
    tn = DP_TILE
    nj = NPERM // tn
    npc = len(pieces)

    def body(*refs):
        w_ref, x_ref, lnw_ref, dy_ref, dx_ref, dlnw_ref, acc = refs[npc:]
        j = pl.program_id(1)

        @pl.when(j == 0)
        def _():
            acc[...] = jnp.zeros_like(acc)

        def add(dp):
            acc[...] += _bdot(dp, w_ref[...], 1, 0)
        _for_piece(j, refs[:npc], add)

        @pl.when(j == nj - 1)
        def _():
            _, vjp = jax.vjp(_rms, x_ref[...], lnw_ref[...])
            dx, dlnw = vjp(acc[...])
            dx_ref[...] = dx + dy_ref[...]

            @pl.when(pl.program_id(0) == 0)
            def _():
                dlnw_ref[...] = jnp.zeros_like(dlnw_ref)
            dlnw_ref[...] += jnp.broadcast_to(dlnw, dlnw_ref.shape)

    return pl.pallas_call(
        body, name="inproj_dx", grid=(t // tm, nj),
        in_specs=_piece_specs(tm, False) + [
                  pl.BlockSpec((tn, D), lambda i, j: (j, 0)),
                  pl.BlockSpec((tm, D), lambda i, j: (i, 0)),
                  pl.BlockSpec((1, D), lambda i, j: (0, 0)),
                  pl.BlockSpec((tm, D), lambda i, j: (i, 0))],
        out_specs=[pl.BlockSpec((tm, D), lambda i, j: (i, 0)),
                   pl.BlockSpec((8, D), lambda i, j: (0, 0))],
        out_shape=[jax.ShapeDtypeStruct((t, D), F32), jax.ShapeDtypeStruct((8, D), F32)],
        scratch_shapes=[pltpu.VMEM((tm, D), F32)],
        compiler_params=_cparams(("arbitrary", "arbitrary")),
    )(*pieces, wperm, x, lnw, dyres)


def _qkv_fwd(p, convw, kind):
    t = p.shape[0]
    f = _qkv_act(kind)

    def body(p_ref, w_ref, o_ref):
        o_ref[...] = f(p_ref[...], w_ref[...])

    return pl.pallas_call(
        body, name=f"qkv_fwd{kind}", grid=(NA,),
        in_specs=[pl.BlockSpec((t, DA), lambda h: (0, kind * NA + h)),
                  pl.BlockSpec((8, DA), lambda h: (0, kind * NA + h))],
        out_specs=pl.BlockSpec((t, DA), lambda h: (0, h)),
        out_shape=jax.ShapeDtypeStruct((t, D), F32),
        compiler_params=_cparams(("parallel",)),
    )(p, convw)


def _qkv_bwd(p, convw, dout, kind):
    t = p.shape[0]
    f = _qkv_act(kind)

    def body(p_ref, w_ref, g_ref, dx_ref, dw_ref):
        _, vjp = jax.vjp(f, p_ref[...], w_ref[...])
        dx, dw = vjp(g_ref[...])
        dx_ref[...] = dx.astype(BF16)
        dw_ref[...] = dw

    return pl.pallas_call(
        body, name=f"qkv_bwd{kind}", grid=(NA,),
        in_specs=[pl.BlockSpec((t, DA), lambda h: (0, kind * NA + h)),
                  pl.BlockSpec((8, DA), lambda h: (0, kind * NA + h)),
                  pl.BlockSpec((t, DA), lambda h: (0, h))],
        out_specs=[pl.BlockSpec((t, DA), lambda h: (0, h)),
                   pl.BlockSpec((8, DA), lambda h: (0, h))],
        out_shape=[jax.ShapeDtypeStruct((t, D), BF16), jax.ShapeDtypeStruct((8, D), F32)],
        compiler_params=_cparams(("parallel",)),
    )(p, convw, dout)


def _gates_f(ps, alog_row, dt_row, w2f, b2f, w2b, b2b):
    lane = lax.broadcasted_iota(jnp.int32, ps.shape, 1)
    lg = -jnp.exp(alog_row) * _softplus(ps + dt_row)
    gsm = jnp.where(lane < 16, lg, jnp.where(lane < 32, _sigmoid(ps), 0.0))
    gkf = -_softplus(-(mm(ps, w2f) + b2f)) * (1.0 / 16.0)
    gkb = -_softplus(-(mm(ps, w2b) + b2b)) * (1.0 / 16.0)
    return gsm, gkf, gkb


def _gates_fwd(ps, alog_row, dt_row, w2f, b2f, w2b, b2b, tm=512):
    t = ps.shape[0]

    def body(ps_ref, a_ref, d_ref, wf_ref, bf_ref, wb_ref, bb_ref, gsm_ref, gk_ref):
        gsm, gkf, gkb = _gates_f(ps_ref[...], a_ref[...], d_ref[...], wf_ref[...], bf_ref[...],
                                 wb_ref[...], bb_ref[...])
        gsm_ref[...] = gsm
        gk_ref[0] = gkf
        gk_ref[1] = gkb

    row = lambda n: pl.BlockSpec((1, n), lambda i: (0, 0))
    mat = pl.BlockSpec((128, 512), lambda i: (0, 0))
    return pl.pallas_call(
        body, name="gates_fwd", grid=(t // tm,),
        in_specs=[pl.BlockSpec((tm, 128), lambda i: (i, PS_BLOCK)), row(128), row(128), mat, row(512), mat, row(512)],
        out_specs=[pl.BlockSpec((tm, 128), lambda i: (i, 0)),
                   pl.BlockSpec((2, tm, 512), lambda i: (0, i, 0))],
        out_shape=[jax.ShapeDtypeStruct((t, 128), F32), jax.ShapeDtypeStruct((2, t, 512), F32)],
        compiler_params=_cparams(("parallel",)),
    )(ps, alog_row, dt_row, w2f, b2f, w2b, b2b)


def _gates_bwd(ps, alog_row, dt_row, w2f, b2f, w2b, b2b, dgsm, dgk, tm=512):
    t = ps.shape[0]

    def body(ps_ref, a_ref, d_ref, wf_ref, bf_ref, wb_ref, bb_ref, dgsm_ref, dgk_ref,
             dps_ref, da_ref, dd_ref, dwf_ref, dbf_ref, dwb_ref, dbb_ref):
        _, vjp = jax.vjp(_gates_f, ps_ref[...], a_ref[...], d_ref[...], wf_ref[...], bf_ref[...],
                         wb_ref[...], bb_ref[...])
        dps, da, dd, dwf, dbf, dwb, dbb = vjp((dgsm_ref[...], dgk_ref[0], dgk_ref[1]))
        dps_ref[:, 0:128] = dps.astype(BF16)
        dps_ref[:, 128:DP_TILE] = jnp.zeros((tm, DP_TILE - 128), BF16)
        accs = ((da_ref, da), (dd_ref, dd), (dwf_ref, dwf), (dbf_ref, dbf), (dwb_ref, dwb), (dbb_ref, dbb))

        @pl.when(pl.program_id(0) == 0)
        def _():
            for ref, _ in accs:
                ref[...] = jnp.zeros_like(ref)
        for ref, val in accs:
            ref[...] += jnp.broadcast_to(val, ref.shape)

    row = lambda n: pl.BlockSpec((1, n), lambda i: (0, 0))
    row8 = lambda n: pl.BlockSpec((8, n), lambda i: (0, 0))
    mat = pl.BlockSpec((128, 512), lambda i: (0, 0))
    return pl.pallas_call(
        body, name="gates_bwd", grid=(t // tm,),
        in_specs=[pl.BlockSpec((tm, 128), lambda i: (i, PS_BLOCK)), row(128), row(128), mat, row(512), mat, row(512),
                  pl.BlockSpec((tm, 128), lambda i: (i, 0)),
                  pl.BlockSpec((2, tm, 512), lambda i: (0, i, 0))],
        out_specs=[pl.BlockSpec((tm, DP_TILE), lambda i: (i, 0)), row8(128), row8(128), mat, row8(512), mat,
                   row8(512)],
        out_shape=[jax.ShapeDtypeStruct((t, DP_TILE), BF16),
                   jax.ShapeDtypeStruct((8, 128), F32), jax.ShapeDtypeStruct((8, 128), F32),
                   jax.ShapeDtypeStruct((128, 512), F32), jax.ShapeDtypeStruct((8, 512), F32),
                   jax.ShapeDtypeStruct((128, 512), F32), jax.ShapeDtypeStruct((8, 512), F32)],
        compiler_params=_cparams(("arbitrary",)),
    )(ps, alog_row, dt_row, w2f, b2f, w2b, b2b, dgsm, dgk)


def _rows(i):
    return pl.ds(pl.multiple_of(i * CH, CH), CH)


def _srows(i):
    return pl.ds(pl.multiple_of(i * SC, SC), SC)


def _first_row(x):
    row = lax.broadcasted_iota(jnp.int32, (8, x.shape[1]), 0)
    return jnp.where(row == 0, jnp.broadcast_to(x, (8, x.shape[1])), 0.0)


def _chunk_rows(e_ref, i):
    pad = jnp.zeros((CH - 8, 128), F32)
    return jnp.concatenate([x for c in range(SC // CH) for x in (e_ref[(SC // CH) * i + c], pad)], axis=0)


def _gcum_f(gsm, tm):
    i = lax.broadcasted_iota(jnp.int32, (tm, tm), 0)
    j = lax.broadcasted_iota(jnp.int32, (tm, tm), 1)
    same = (i >> 6) == (j >> 6)
    lower = (same & (i >= j)).astype(F32)
    upper = (same & (i <= j)).astype(F32)
    r = lax.broadcasted_iota(jnp.int32, (128, D), 0)
    head = lax.broadcasted_iota(jnp.int32, (128, D), 1) >> 7
    pick = lambda off: (r == head + off).astype(F32)
    lane = lax.broadcasted_iota(jnp.int32, gsm.shape, 1)
    run = jnp.where(lane < 8, cmm(lower, gsm), cmm(upper, gsm))
    return mmc(run, pick(0)), mmc(run, pick(8)), mmc(gsm, pick(16)), mmc(gsm, pick(24))


def _gcum_fwd(gsm, tm=256):
    t = gsm.shape[0]

    def body(s_ref, g_ref, b_ref):
        gf, gb, bf, bb = _gcum_f(s_ref[...], tm)
        g_ref[0] = gf
        g_ref[1] = gb
        b_ref[0] = bf
        b_ref[1] = bb

    two = pl.BlockSpec((2, tm, D), lambda i: (0, i, 0))
    return pl.pallas_call(
        body, name="gcum_fwd", grid=(t // tm,),
        in_specs=[pl.BlockSpec((tm, 128), lambda i: (i, 0))], out_specs=[two, two],
        out_shape=[jax.ShapeDtypeStruct((2, t, D), F32)] * 2,
        compiler_params=_cparams(("parallel",)),
    )(gsm)


def _gcum_bwd(gsm, dg2, db2, tm=256):
    t = gsm.shape[0]

    def body(s_ref, dg_ref, db_ref, ds_ref):
        _, vjp = jax.vjp(lambda s: _gcum_f(s, tm), s_ref[...])
        ds_ref[...] = vjp((dg_ref[0], dg_ref[1], db_ref[0], db_ref[1]))[0]

    two = pl.BlockSpec((2, tm, D), lambda i: (0, i, 0))
    tile = pl.BlockSpec((tm, 128), lambda i: (i, 0))
    return pl.pallas_call(
        body, name="gcum_bwd", grid=(t // tm,),
        in_specs=[tile, two, two], out_specs=tile,
        out_shape=jax.ShapeDtypeStruct((t, 128), F32),
        compiler_params=_cparams(("parallel",)),
    )(gsm, dg2, db2)


def _gdn_intra_fwd(qn, kn, vc, g2, b2):
    t = qn.shape[0]
    n = t // CH

    def body(q_ref, k_ref, v_ref, g_ref, b_ref, u_ref, w_ref, a_ref, qd_ref, kd_ref, e_ref, t_ref):
        cs = _Consts(pl.program_id(0) == 1)

        def step(i, carry):
            r = _srows(i)
            q, k, v, g, bx = q_ref[r, :], k_ref[r, :], v_ref[r, :], g_ref[r, :], b_ref[r, :]
            u, w, a, qd, kd, el, tinv = _gdn_intra(q, k, v, g, bx, None, cs)
            u_ref[r, :] = u
            w_ref[r, :] = w
            a_ref[r, :] = a
            qd_ref[r, :] = qd
            kd_ref[r, :] = kd
            t_ref[r, :] = tinv
            for c in range(SC // CH):
                e_ref[(SC // CH) * i + c] = el[c * CH:c * CH + 8]
            return carry

        lax.fori_loop(0, t // SC, step, 0)

    head = pl.BlockSpec((t, DA), lambda d, h: (0, h))
    dh = pl.BlockSpec((None, t, DA), lambda d, h: (d, 0, h))
    sq = lambda w: pl.BlockSpec((None, None, t, w), lambda d, h: (d, h, 0, 0))
    big = jax.ShapeDtypeStruct((2, t, D), F32)
    return pl.pallas_call(
        body, name="gdn_intra_fwd", grid=(2, NA),
        in_specs=[head, head, head, dh, dh],
        out_specs=[dh, dh, sq(CH), dh, dh, pl.BlockSpec((None, None, n, 8, 128), lambda d, h: (d, h, 0, 0, 0)),
                   sq(SC)],
        out_shape=[big, big, jax.ShapeDtypeStruct((2, NA, t, CH), F32), big, big,
                   jax.ShapeDtypeStruct((2, NA, n, 8, 128), F32), jax.ShapeDtypeStruct((2, NA, t, SC), F32)],
        compiler_params=_cparams(("parallel", "parallel")),
    )(qn, kn, vc, g2, b2)


SCAN_TB = 256
SCAN_HB = 8


def _scan_specs(t, width, nheads, hb, along):
    nt = t // SCAN_TB
    nb = SCAN_TB // CH

    def tmap(d, tt):
        fwd = tt + d * (nt - 1 - 2 * tt)
        return fwd if along > 0 else nt - 1 - fwd

    tok = pl.BlockSpec((None, SCAN_TB, hb * width), lambda d, h, tt: (d, tmap(d, tt), h))
    per = lambda *tail: pl.BlockSpec((None, hb, nb) + tail, lambda d, h, tt: (d, h, tmap(d, tt)) + (0,) * len(tail))
    sq = pl.BlockSpec((None, hb, SCAN_TB, CH), lambda d, h, tt: (d, h, tmap(d, tt), 0))
    shared = lambda w: pl.BlockSpec((SCAN_TB, hb * w), lambda d, h, tt: (tmap(d, tt), h))
    return tok, per, sq, shared, (2, nheads // hb, nt), nb


def _gdn_scan_fwd(u, w, a, qd, kd, e):
    t = u.shape[1]
    tok, per, sq, _, grid, nb = _scan_specs(t, DA, NA, SCAN_HB, +1)

    def body(u_ref, w_ref, a_ref, qd_ref, kd_ref, e_ref, o_ref, s_ref, state):
        rev = pl.program_id(0) == 1

        @pl.when(pl.program_id(2) == 0)
        def _():
            state[...] = jnp.zeros_like(state)

        def step(i, ss):
            ci = jnp.where(rev, nb - 1 - i, i)
            r = _rows(ci)
            out = []
            for hh, s in enumerate(ss):
                c = slice(hh * DA, (hh + 1) * DA)
                s_ref[hh, ci] = s
                o, sn = _gdn_scan(u_ref[r, c], w_ref[r, c], a_ref[hh, r, :], qd_ref[r, c], kd_ref[r, c],
                                  e_ref[hh, ci][0:1], s)
                o_ref[r, c] = o
                out.append(sn)
            return tuple(out)

        ss = lax.fori_loop(0, nb, step, tuple(state[hh] for hh in range(SCAN_HB)))
        for hh, s in enumerate(ss):
            state[hh] = s

    return pl.pallas_call(
        body, name="gdn_scan_fwd", grid=grid,
        in_specs=[tok, tok, sq, tok, tok, per(8, 128)],
        out_specs=[tok, per(DA, DA)],
        out_shape=[jax.ShapeDtypeStruct((2, t, D), F32), jax.ShapeDtypeStruct((2, NA, t // CH, DA, DA), F32)],
        scratch_shapes=[pltpu.VMEM((SCAN_HB, DA, DA), F32)],
        compiler_params=_cparams(("parallel", "parallel", "arbitrary")),
    )(u, w, a, qd, kd, e)


def _gdn_scan_bwd(u, w, a, qd, kd, e, ssave, do):
    t = u.shape[1]
    tok, per, sq, shared, grid, nb = _scan_specs(t, DA, NA, SCAN_HB, -1)

    def body(u_ref, w_ref, a_ref, qd_ref, kd_ref, e_ref, s_ref, do_ref,
             du_ref, dw_ref, da_ref, dqd_ref, dkd_ref, de_ref, state):
        rev = pl.program_id(0) == 1

        @pl.when(pl.program_id(2) == 0)
        def _():
            state[...] = jnp.zeros_like(state)

        def step(i, dss):
            ci = jnp.where(rev, i, nb - 1 - i)
            r = _rows(ci)
            out = []
            for hh, ds in enumerate(dss):
                c = slice(hh * DA, (hh + 1) * DA)
                _, vjp = jax.vjp(_gdn_scan, u_ref[r, c], w_ref[r, c], a_ref[hh, r, :], qd_ref[r, c], kd_ref[r, c],
                                 e_ref[hh, ci][0:1], s_ref[hh, ci])
                du, dw, da, dqd, dkd, de, dsn = vjp((do_ref[r, c], ds))
                du_ref[r, c] = du
                dw_ref[r, c] = dw
                da_ref[hh, r, :] = da
                dqd_ref[r, c] = dqd
                dkd_ref[r, c] = dkd
                de_ref[hh, ci] = _first_row(de)
                out.append(dsn)
            return tuple(out)

        dss = lax.fori_loop(0, nb, step, tuple(state[hh] for hh in range(SCAN_HB)))
        for hh, ds in enumerate(dss):
            state[hh] = ds

    big = jax.ShapeDtypeStruct((2, t, D), F32)
    return pl.pallas_call(
        body, name="gdn_scan_bwd", grid=grid,
        in_specs=[tok, tok, sq, tok, tok, per(8, 128), per(DA, DA), shared(DA)],
        out_specs=[tok, tok, sq, tok, tok, per(8, 128)],
        out_shape=[big, big, jax.ShapeDtypeStruct((2, NA, t, CH), F32), big, big,
                   jax.ShapeDtypeStruct((2, NA, t // CH, 8, 128), F32)],
        scratch_shapes=[pltpu.VMEM((SCAN_HB, DA, DA), F32)],
        compiler_params=_cparams(("parallel", "parallel", "arbitrary")),
    )(u, w, a, qd, kd, e, ssave, do)


def _gdn_intra_bwd(qn, kn, vc, g2, b2, tinv, du, dw, da, dqd, dkd, de):
    t = qn.shape[0]
    n = t // CH

    def body(q_ref, k_ref, v_ref, g_ref, b_ref, t_ref, du_ref, dw_ref, da_ref, dqd_ref, dkd_ref, de_ref,
             dq_ref, dk_ref, dv_ref, dg_ref, db_ref):
        d = pl.program_id(1)
        cs = _Consts(d == 1)

        @pl.when(d == 0)
        def _():
            dq_ref[...] = jnp.zeros_like(dq_ref)
            dk_ref[...] = jnp.zeros_like(dk_ref)
            dv_ref[...] = jnp.zeros_like(dv_ref)

        def step(i, carry):
            r = _srows(i)
            tinv_c = t_ref[r, :]
            f = lambda q, k, v, g, bx: _gdn_intra(q, k, v, g, bx, tinv_c, cs)
            _, vjp = jax.vjp(f, q_ref[r, :], k_ref[r, :], v_ref[r, :], g_ref[r, :], b_ref[r, :])
            dq, dk, dv, dg, dbx = vjp((du_ref[r, :], dw_ref[r, :], da_ref[r, :], dqd_ref[r, :],
                                       dkd_ref[r, :], _chunk_rows(de_ref, i)))
            dq_ref[r, :] += dq
            dk_ref[r, :] += dk
            dv_ref[r, :] += dv
            dg_ref[r, :] = dg
            db_ref[r, :] = dbx
            return carry

        lax.fori_loop(0, t // SC, step, 0)

    head = pl.BlockSpec((t, DA), lambda h, d: (0, h))
    dh = pl.BlockSpec((None, t, DA), lambda h, d: (d, 0, h))
    sq = pl.BlockSpec((None, None, t, CH), lambda h, d: (d, h, 0, 0))
    tq = pl.BlockSpec((None, None, t, SC), lambda h, d: (d, h, 0, 0))
    full = jax.ShapeDtypeStruct((t, D), F32)
    big = jax.ShapeDtypeStruct((2, t, D), F32)
    return pl.pallas_call(
        body, name="gdn_intra_bwd", grid=(NA, 2),
        in_specs=[head, head, head, dh, dh, tq, dh, dh, sq, dh, dh,
                  pl.BlockSpec((None, None, n, 8, 128), lambda h, d: (d, h, 0, 0, 0))],
        out_specs=[head, head, head, dh, dh],
        out_shape=[full, full, full, big, big],
        compiler_params=_cparams(("arbitrary", "arbitrary")),
    )(qn, kn, vc, g2, b2, tinv, du, dw, da, dqd, dkd, de)


def _gla_specs(t, order):
    ix = (lambda d, h: (d, h)) if order == "dh" else (lambda h, d: (d, h))

    def mk(fn):
        return lambda a, b: fn(*ix(a, b))
    q = pl.BlockSpec((t, DKB), mk(lambda d, h: (0, 32 + h)))
    k = pl.BlockSpec((t, DKB), mk(lambda d, h: (0, 36 + h)))
    v = pl.BlockSpec((t, DVB), mk(lambda d, h: (0, 20 + h)))
    dk = pl.BlockSpec((None, t, DKB), mk(lambda d, h: (d, 0, h)))
    dv = pl.BlockSpec((None, t, DVB), mk(lambda d, h: (d, 0, h)))
    e = pl.BlockSpec((None, None, t // CH, 8, 128), mk(lambda d, h: (d, h, 0, 0, 0)))
    s = pl.BlockSpec((None, None, t // CH, DVB, DKB), mk(lambda d, h: (d, h, 0, 0, 0)))
    return q, k, v, dk, dv, e, s


def _gla_intra_fwd(p, gk):
    t = p.shape[0]
    n = t // CH

    def body(q_ref, k_ref, v_ref, g_ref, qg_ref, kd_ref, in_ref, e_ref):
        cs = _Consts(pl.program_id(0) == 1)

        def step(i, carry):
            r = _srows(i)
            qg, kd, intra, el = _gla_intra(q_ref[r, :], k_ref[r, :], v_ref[r, :], g_ref[r, :], cs)
            qg_ref[r, :] = qg
            kd_ref[r, :] = kd
            in_ref[r, :] = intra
            for c in range(SC // CH):
                e_ref[(SC // CH) * i + c] = el[c * CH:c * CH + 8]
            return carry

        lax.fori_loop(0, t // SC, step, 0)

    q, k, v, dk, dv, e, _ = _gla_specs(t, "dh")
    return pl.pallas_call(
        body, name="gla_intra_fwd", grid=(2, NB),
        in_specs=[q, k, v, dk], out_specs=[dk, dk, dv, e],
        out_shape=[jax.ShapeDtypeStruct((2, t, NB * DKB), F32), jax.ShapeDtypeStruct((2, t, NB * DKB), F32),
                   jax.ShapeDtypeStruct((2, t, D), F32), jax.ShapeDtypeStruct((2, NB, n, 8, 128), F32)],
        compiler_params=_cparams(("parallel", "parallel")),
    )(p, p, p, gk)


GLA_HB = 2


def _gla_v_spec(t, along):
    nt = t // SCAN_TB

    def tmap(d, tt):
        fwd = tt + d * (nt - 1 - 2 * tt)
        return fwd if along > 0 else nt - 1 - fwd

    return pl.BlockSpec((SCAN_TB, GLA_HB * DVB), lambda d, h, tt: (tmap(d, tt), 5120 // (GLA_HB * DVB) + h))


def _gla_scan_fwd(p, qg, kd, intra, e):
    t = p.shape[0]
    tokk, per, _, _, grid, nb = _scan_specs(t, DKB, NB, GLA_HB, +1)
    tokv = _scan_specs(t, DVB, NB, GLA_HB, +1)[0]

    def body(v_ref, qg_ref, kd_ref, in_ref, e_ref, o_ref, s_ref, state):
        rev = pl.program_id(0) == 1

        @pl.when(pl.program_id(2) == 0)
        def _():
            state[...] = jnp.zeros_like(state)

        def step(i, sts):
            ci = jnp.where(rev, nb - 1 - i, i)
            r = _rows(ci)
            out = []
            for hh, st in enumerate(sts):
                ck = slice(hh * DKB, (hh + 1) * DKB)
                cv = slice(hh * DVB, (hh + 1) * DVB)
                s_ref[hh, ci] = st
                o, stn = _gla_scan(qg_ref[r, ck], kd_ref[r, ck], v_ref[r, cv], e_ref[hh, ci][0:1], st)
                o_ref[r, cv] = o + in_ref[r, cv]
                out.append(stn)
            return tuple(out)

        sts = lax.fori_loop(0, nb, step, tuple(state[hh] for hh in range(GLA_HB)))
        for hh, st in enumerate(sts):
            state[hh] = st

    return pl.pallas_call(
        body, name="gla_scan_fwd", grid=grid,
        in_specs=[_gla_v_spec(t, +1), tokk, tokk, tokv, per(8, 128)], out_specs=[tokv, per(DVB, DKB)],
        out_shape=[jax.ShapeDtypeStruct((2, t, D), F32), jax.ShapeDtypeStruct((2, NB, t // CH, DVB, DKB), F32)],
        scratch_shapes=[pltpu.VMEM((GLA_HB, DVB, DKB), F32)],
        compiler_params=_cparams(("parallel", "parallel", "arbitrary")),
    )(p, qg, kd, intra, e)


def _gla_scan_bwd(p, qg, kd, e, ssave, do):
    t = p.shape[0]
    tokk, per, _, shared, grid, nb = _scan_specs(t, DKB, NB, GLA_HB, -1)
    tokv = _scan_specs(t, DVB, NB, GLA_HB, -1)[0]

    def body(v_ref, qg_ref, kd_ref, e_ref, s_ref, do_ref, dqg_ref, dkd_ref, dv_ref, de_ref, state):
        rev = pl.program_id(0) == 1

        @pl.when(pl.program_id(2) == 0)
        def _():
            state[...] = jnp.zeros_like(state)

        def step(i, dsts):
            ci = jnp.where(rev, i, nb - 1 - i)
            r = _rows(ci)
            out = []
            for hh, dst in enumerate(dsts):
                ck = slice(hh * DKB, (hh + 1) * DKB)
                cv = slice(hh * DVB, (hh + 1) * DVB)
                _, vjp = jax.vjp(_gla_scan, qg_ref[r, ck], kd_ref[r, ck], v_ref[r, cv], e_ref[hh, ci][0:1],
                                 s_ref[hh, ci])
                dqg, dkd, dv, de, dstn = vjp((do_ref[r, cv], dst))
                dqg_ref[r, ck] = dqg
                dkd_ref[r, ck] = dkd
                dv_ref[r, cv] = dv
                de_ref[hh, ci] = _first_row(de)
                out.append(dstn)
            return tuple(out)

        dsts = lax.fori_loop(0, nb, step, tuple(state[hh] for hh in range(GLA_HB)))
        for hh, dst in enumerate(dsts):
            state[hh] = dst

    return pl.pallas_call(
        body, name="gla_scan_bwd", grid=grid,
        in_specs=[_gla_v_spec(t, -1), tokk, tokk, per(8, 128), per(DVB, DKB), shared(DVB)],
        out_specs=[tokk, tokk, tokv, per(8, 128)],
        out_shape=[jax.ShapeDtypeStruct((2, t, NB * DKB), F32), jax.ShapeDtypeStruct((2, t, NB * DKB), F32),
                   jax.ShapeDtypeStruct((2, t, D), F32), jax.ShapeDtypeStruct((2, NB, t // CH, 8, 128), F32)],
        scratch_shapes=[pltpu.VMEM((GLA_HB, DVB, DKB), F32)],
        compiler_params=_cparams(("parallel", "parallel", "arbitrary")),
    )(p, qg, kd, e, ssave, do)


def _gla_intra_bwd(p, gk, dqg, dkd, dvs, de, do):
    t = p.shape[0]
    n = t // CH

    def body(q_ref, k_ref, v_ref, g_ref, dqg_ref, dkd_ref, dvs_ref, de_ref, do_ref,
             dq_ref, dk_ref, dv_ref, dg_ref):
        d = pl.program_id(1)
        cs = _Consts(d == 1)

        @pl.when(d == 0)
        def _():
            dq_ref[...] = jnp.zeros_like(dq_ref)
            dk_ref[...] = jnp.zeros_like(dk_ref)
            dv_ref[...] = jnp.zeros_like(dv_ref)

        def step(i, carry):
            r = _srows(i)
            f = lambda q, k, v, g: _gla_intra(q, k, v, g, cs)
            _, vjp = jax.vjp(f, q_ref[r, :], k_ref[r, :], v_ref[r, :], g_ref[r, :])
            dq, dk, dv, dg = vjp((dqg_ref[r, :], dkd_ref[r, :], do_ref[r, :], _chunk_rows(de_ref, i)))
            dq_ref[r, :] += dq
            dk_ref[r, :] += dk
            dv_ref[r, :] += dv + dvs_ref[r, :]
            dg_ref[r, :] = dg
            return carry

        lax.fori_loop(0, t // SC, step, 0)

    q, k, v, dk, dv, e_s, _ = _gla_specs(t, "hd")
    hk = pl.BlockSpec((t, DKB), lambda h, d: (0, h))
    hv = pl.BlockSpec((t, DVB), lambda h, d: (0, h))
    return pl.pallas_call(
        body, name="gla_intra_bwd", grid=(NB, 2),
        in_specs=[q, k, v, dk, dk, dk, dv, e_s, hv],
        out_specs=[hk, hk, hv, dk],
        out_shape=[jax.ShapeDtypeStruct((t, NB * DKB), F32), jax.ShapeDtypeStruct((t, NB * DKB), F32),
                   jax.ShapeDtypeStruct((t, D), F32), jax.ShapeDtypeStruct((2, t, NB * DKB), F32)],
        compiler_params=_cparams(("arbitrary", "arbitrary")),
    )(p, p, p, gk, dqg, dkd, dvs, de, do)


def _seg_gate(o, z, w):
    return _rms(o, w) * _silu(z)


def _seg_merge(ya, yb, ga, gb):
    return _sigmoid(ga) * ya + _sigmoid(gb) * yb


def _seg_loss(out, x, tgt, w):
    err = x + _rms(out, w) - tgt
    return 0.5 * jnp.sum(jnp.mean(err * err, axis=-1, keepdims=True), axis=0, keepdims=True)


def _post(oa2, ob2, p, x, tgt, gdn_w, gla_w, lnpost, w3, tm=128):
    t = x.shape[0]

    def body(oa_ref, ob_ref, z_ref, gb_ref, ga_ref, gB_ref, x_ref, t_ref, aw_ref, bw_ref, lw_ref, w_ref,
             loss_ref, doa_ref, dob_ref, dz_ref, dgb_ref, dga_ref, dgB_ref, dy_ref,
             dw_ref, daw_ref, dbw_ref, dlw_ref):
        first = pl.program_id(0) == 0
        oa = oa_ref[0] + oa_ref[1]
        ob = ob_ref[0] + ob_ref[1]
        z, gb = z_ref[...], gb_ref[...]
        aw, bw = aw_ref[...], bw_ref[...]
        rs = D // NSHARD

        def mat(a, m):
            return sum(jnp.dot(a[:, s * rs:(s + 1) * rs], w_ref[s, m], preferred_element_type=F32)
                       for s in range(NSHARD))

        def mat_t(g, m):
            return jnp.concatenate([_bdot(g, w_ref[s, m], 1, 1) for s in range(NSHARD)], axis=1)

        def add_dw(a, g, m):
            for s in range(NSHARD):
                dw_ref[s, m] += _bdot(a[:, s * rs:(s + 1) * rs], g, 0, 0)

        pa = [jax.vjp(_seg_gate, oa[:, h * DA:(h + 1) * DA], z[:, h * DA:(h + 1) * DA], aw) for h in range(NA)]
        pb = [jax.vjp(_seg_gate, ob[:, h * DVB:(h + 1) * DVB], gb[:, h * DVB:(h + 1) * DVB], bw)
              for h in range(NB)]
        a1 = jnp.concatenate([v for v, _ in pa], axis=1).astype(BF16)
        a2 = jnp.concatenate([v for v, _ in pb], axis=1).astype(BF16)
        ya = mat(a1, 0)
        yb = mat(a2, 1)
        merged, vjp_m = jax.vjp(_seg_merge, ya, yb, ga_ref[...], gB_ref[...])
        mb = merged.astype(BF16)
        out = mat(mb, 2)
        loss, vjp_l = jax.vjp(_seg_loss, out, x_ref[...], t_ref[...], lw_ref[...])
        dout, dyres, _, dlw = vjp_l(jnp.ones((1, 1), F32))
        dy_ref[...] = dyres
        doutb = dout.astype(BF16)
        dmerged = mat_t(doutb, 2)
        dya, dyb, dga, dgB = vjp_m(dmerged)
        dga_ref[...] = dga.astype(BF16)
        dgB_ref[...] = dgB.astype(BF16)
        dyab, dybb = dya.astype(BF16), dyb.astype(BF16)
        da1 = mat_t(dyab, 0)
        da2 = mat_t(dybb, 1)

        daw = jnp.zeros_like(aw)
        for h in range(NA):
            sl = slice(h * DA, (h + 1) * DA)
            do, dz, dw = pa[h][1](da1[:, sl])
            doa_ref[:, sl] = do
            dz_ref[:, sl] = dz.astype(BF16)
            daw = daw + dw
        dbw = jnp.zeros_like(bw)
        for h in range(NB):
            sl = slice(h * DVB, (h + 1) * DVB)
            do, dg, dw = pb[h][1](da2[:, sl])
            dob_ref[:, sl] = do
            dgb_ref[:, sl] = dg.astype(BF16)
            dbw = dbw + dw

        @pl.when(first)
        def _():
            loss_ref[...] = jnp.zeros_like(loss_ref)
            dw_ref[...] = jnp.zeros_like(dw_ref)
            daw_ref[...] = jnp.zeros_like(daw_ref)
            dbw_ref[...] = jnp.zeros_like(dbw_ref)
            dlw_ref[...] = jnp.zeros_like(dlw_ref)

        loss_ref[...] += jnp.broadcast_to(loss, loss_ref.shape)
        add_dw(a1, dyab, 0)
        add_dw(a2, dybb, 1)
        add_dw(mb, doutb, 2)
        daw_ref[...] += jnp.broadcast_to(daw, daw_ref.shape)
        dbw_ref[...] += jnp.broadcast_to(dbw, dbw_ref.shape)
        dlw_ref[...] += jnp.broadcast_to(dlw, dlw_ref.shape)

    two = pl.BlockSpec((2, tm, D), lambda i: (0, i, 0))
    pcol = lambda c: pl.BlockSpec((tm, D), lambda i: (i, c))
    tok = pl.BlockSpec((tm, D), lambda i: (i, 0))
    row = lambda n: pl.BlockSpec((1, n), lambda i: (0, 0))
    row8 = lambda n: pl.BlockSpec((8, n), lambda i: (0, 0))
    once = pl.Buffered(1)
    tokf = jax.ShapeDtypeStruct((t, D), F32)
    tokb = jax.ShapeDtypeStruct((t, D), BF16)
    wspec = pl.BlockSpec((NSHARD, 3, D // NSHARD, D), lambda i: (0, 0, 0, 0), pipeline_mode=once)
    return pl.pallas_call(
        body, name="post", grid=(t // tm,),
        in_specs=[two, two, pcol(3), pcol(6), pcol(7), pcol(8), tok, tok, row(DA), row(DVB), row(D), wspec],
        out_specs=[row8(128), tok, tok, tok, tok, tok, tok, tok, wspec, row8(DA), row8(DVB), row8(D)],
        out_shape=[jax.ShapeDtypeStruct((8, 128), F32), tokf, tokf, tokb, tokb, tokb, tokb, tokf,
                   jax.ShapeDtypeStruct((NSHARD, 3, D // NSHARD, D), F32),
                   jax.ShapeDtypeStruct((8, DA), F32), jax.ShapeDtypeStruct((8, DVB), F32),
                   jax.ShapeDtypeStruct((8, D), F32)],
        compiler_params=_cparams(("arbitrary",), vmem_mb=56),
    )(oa2, ob2, p, p, p, p, x, tgt, gdn_w, gla_w, lnpost, w3)


def _adam_math(w, g, m, v):
    nm = B1 * m + (1.0 - B1) * g
    nv = B2 * v + (1.0 - B2) * (g * g)
    m_hat = nm / (1.0 - B1 ** STEP)
    v_hat = nv / (1.0 - B2 ** STEP)
    return -LR * (m_hat / (jnp.sqrt(v_hat) + ADAM_EPS) + WD * w), nm, nv


SMALL_SLOTS = (("ln_pre_w", 0, 1024, 0), ("a_log_fwd", 1024, 8, 0), ("a_log_bwd", 1024, 8, 8),
               ("dt_bias_fwd", 1152, 8, 0), ("dt_bias_bwd", 1152, 8, 8), ("gdn_norm_w", 1280, 128, 0),
               ("gk_b2_fwd", 1408, 512, 0), ("gk_b2_bwd", 1920, 512, 0), ("gla_norm_w", 2432, 256, 0),
               ("ln_post_w", 2688, 1024, 0))
SMALL_W = 3712


def _adam_small(gsum, ws, ms, vs):
    nw = len(SMALL_SLOTS)

    def body(g_ref, *refs):
        w_refs, m_refs, v_refs, outs = refs[0:nw], refs[nw:2 * nw], refs[2 * nw:3 * nw], refs[3 * nw:]
        for i, (_, off, n, shift) in enumerate(SMALL_SLOTS):
            slot = g_ref[0:1, off:off + max(n, 128)]
            if shift:
                slot = pltpu.roll(slot, 128 - shift, 1)
            g = slot[:, 0:n]
            d, nm, nv = _adam_math(w_refs[i][...], g, m_refs[i][...], v_refs[i][...])
            for k, val in enumerate((g, d, nm, nv)):
                outs[4 * i + k][...] = val

    vm = pl.BlockSpec(memory_space=pltpu.VMEM)
    res = pl.pallas_call(
        body, name="adam_small", in_specs=[vm] * (1 + 3 * nw), out_specs=[vm] * (4 * nw),
        out_shape=[jax.ShapeDtypeStruct((1, n), F32) for _, _, n, _ in SMALL_SLOTS for _ in range(4)],
    )(gsum, *ws, *ms, *vs)
    return {name: res[4 * i:4 * i + 4] for i, (name, _, _, _) in enumerate(SMALL_SLOTS)}


def _adam(w, mine, got, m, v, tr, tile0=0, name=""):
    rows, cols = w.shape
    nh = mine.shape[0] // tr

    def body(c_ref, w_ref, a_ref, b_ref, m_ref, v_ref, g_ref, d_ref, nm_ref, nv_ref):
        half = (tile0 + pl.program_id(0)) // nh
        g = jnp.where(half == c_ref[0], a_ref[...], b_ref[...])
        d, nm, nv = _adam_math(w_ref[...], g, m_ref[...], v_ref[...])
        g_ref[...] = g
        d_ref[...] = d
        nm_ref[...] = nm
        nv_ref[...] = nv

    blk = pl.BlockSpec((tr, cols), lambda i, cc: (i, 0))
    half = pl.BlockSpec((tr, cols), lambda i, cc: ((tile0 + i) % nh, 0))
    shp = jax.ShapeDtypeStruct((rows, cols), F32)
    return pl.pallas_call(
        body, name=f"adam_{name}{rows}x{cols}",
        grid_spec=pltpu.PrefetchScalarGridSpec(
            num_scalar_prefetch=1, grid=(rows // tr,),
            in_specs=[blk, half, half, blk, blk], out_specs=[blk] * 4),
        out_shape=[shp] * 4,
        compiler_params=_cparams(("parallel",)),
    )(lax.axis_index("c").reshape(1), w, mine, got, m, v)


def _sum_cast(own, got):
    ns, _, r, c = own.shape
    tr = r // 4 if r % 64 == 0 else r

    def body(c_ref, a_ref, b_ref, f_ref, h_ref):
        s = a_ref[...] + b_ref[...]
        f_ref[...] = s
        h_ref[...] = s.astype(BF16)

    return pl.pallas_call(
        body, name=f"sum_cast_{r}x{c}",
        grid_spec=pltpu.PrefetchScalarGridSpec(
            num_scalar_prefetch=1, grid=(ns, r // tr),
            in_specs=[pl.BlockSpec((None, None, tr, c), lambda s, i, cc: (s, cc[0], i, 0)),
                      pl.BlockSpec((None, tr, c), lambda s, i, cc: (s, i, 0))],
            out_specs=[pl.BlockSpec((None, tr, c), lambda s, i, cc: (s, i, 0)),
                       pl.BlockSpec((None, tr, c), lambda s, i, cc: (s, i, 0))]),
        out_shape=[jax.ShapeDtypeStruct((ns, r, c), F32), jax.ShapeDtypeStruct((ns, r, c), BF16)],
        compiler_params=_cparams(("parallel", "parallel")),
    )(lax.axis_index("c").reshape(1), own, got)


def _sum4(mine, got):
    _, r, c = mine.shape
    tr = r // 4 if r % 64 == 0 else r

    def body(s_ref, a_ref, g_ref, o_ref):
        acc = a_ref[...] + g_ref[0].astype(F32)
        acc = acc + g_ref[1].astype(F32)
        o_ref[...] = acc + g_ref[2].astype(F32)

    shard = (2 * lax.axis_index("x") + lax.axis_index("y")).reshape(1)
    return pl.pallas_call(
        body, name=f"sum4_{r}x{c}",
        grid_spec=pltpu.PrefetchScalarGridSpec(
            num_scalar_prefetch=1, grid=(r // tr,),
            in_specs=[pl.BlockSpec((None, tr, c), lambda i, ss: (ss[0], i, 0)),
                      pl.BlockSpec((3, tr, c), lambda i, ss: (0, i, 0))],
            out_specs=pl.BlockSpec((tr, c), lambda i, ss: (i, 0))),
        out_shape=jax.ShapeDtypeStruct((r, c), F32),
        compiler_params=_cparams(("parallel",)),
    )(shard, mine, got)


def _place():
    x, y, c = lax.axis_index("x"), lax.axis_index("y"), lax.axis_index("c")
    chips = [(1 - x, y), (x, 1 - y), (1 - x, 1 - y)]
    return x, y, c, chips


def _gather_weights(parts):
    npart = len(parts)

    def body(*refs):
        ins, outs = refs[:npart], refs[npart:2 * npart]
        send_sems, recv_sems = refs[2 * npart:]
        x, y, c, chips = _place()
        sibling = (x, y, 1 - c)
        mine = 2 * x + y

        def remote(k, p, shard, half, to, src=None):
            dst = outs[p].at[shard, half]
            return pltpu.make_async_remote_copy(
                src_ref=dst if src is None else src, dst_ref=dst,
                send_sem=send_sems.at[k], recv_sem=recv_sems.at[k], device_id=to, device_id_type=MESH)

        first = [remote(j * npart + p, p, mine, c, (*chip, c), src=ins[p].at[c])
                 for j, chip in enumerate(chips) for p in range(npart)]
        for cp in first:
            cp.start()
        passed = []
        for j, (cx, cy) in enumerate(chips):
            for p in range(npart):
                remote(j * npart + p, p, 2 * cx + cy, c, (x, y, c)).wait_recv()
                fw = remote((3 + j) * npart + p, p, 2 * cx + cy, c, sibling)
                fw.start()
                passed.append(fw)
        for j, (cx, cy) in enumerate(chips):
            for p in range(npart):
                remote((3 + j) * npart + p, p, 2 * cx + cy, 1 - c, (x, y, c)).wait_recv()
        for cp in first + passed:
            cp.wait_send()

    got = pl.pallas_call(
        body, name="gather_weights",
        in_specs=[ANY] * npart, out_specs=[ANY] * npart,
        out_shape=[jax.ShapeDtypeStruct((NSHARD,) + a.shape, a.dtype) for a in parts],
        scratch_shapes=[pltpu.SemaphoreType.DMA((6 * npart,)), pltpu.SemaphoreType.DMA((6 * npart,))],
    )(*parts)
    mine = 2 * lax.axis_index("x") + lax.axis_index("y")
    return [lax.dynamic_update_index_in_dim(g, a, mine, 0) for g, a in zip(got, parts)]


def _swap_halves(parts):
    npart = len(parts)

    def body(*refs):
        ins, outs = refs[:npart], refs[npart:2 * npart]
        send_sems, recv_sems = refs[2 * npart:]
        x, y, c, _ = _place()
        cps = [pltpu.make_async_remote_copy(
            src_ref=ins[p].at[s, 1 - c], dst_ref=outs[p].at[s],
            send_sem=send_sems.at[s * npart + p], recv_sem=recv_sems.at[s * npart + p],
            device_id=(x, y, 1 - c), device_id_type=MESH) for s in range(NSHARD) for p in range(npart)]
        for cp in cps:
            cp.start()
        for cp in cps:
            cp.wait()

    return pl.pallas_call(
        body, name="swap_halves", in_specs=[ANY] * npart, out_specs=[ANY] * npart,
        out_shape=[jax.ShapeDtypeStruct((NSHARD,) + a.shape[2:], a.dtype) for a in parts],
        scratch_shapes=[pltpu.SemaphoreType.DMA((NSHARD * npart,)), pltpu.SemaphoreType.DMA((NSHARD * npart,))],
    )(*parts)


def _scatter_shards(parts):
    npart = len(parts)

    def body(*refs):
        ins, outs = refs[:npart], refs[npart:2 * npart]
        send_sems, recv_sems = refs[2 * npart:]
        x, y, c, chips = _place()
        cps = [pltpu.make_async_remote_copy(
            src_ref=ins[p].at[2 * cx + cy], dst_ref=outs[p].at[j],
            send_sem=send_sems.at[j * npart + p], recv_sem=recv_sems.at[j * npart + p],
            device_id=(cx, cy, c), device_id_type=MESH)
            for j, (cx, cy) in enumerate(chips) for p in range(npart)]
        for cp in cps:
            cp.start()
        for cp in cps:
            cp.wait()

    return pl.pallas_call(
        body, name="scatter_shards", in_specs=[ANY] * npart, out_specs=[ANY] * npart,
        out_shape=[jax.ShapeDtypeStruct((3,) + a.shape[1:], a.dtype) for a in parts],
        scratch_shapes=[pltpu.SemaphoreType.DMA((3 * npart,)), pltpu.SemaphoreType.DMA((3 * npart,))],
    )(*parts)


HBM = pl.BlockSpec(memory_space=pltpu.HBM)
SEM = pl.BlockSpec(memory_space=pltpu.SEMAPHORE)
EFFECT = pltpu.SideEffectType.DATAFLOW_SIDE_EFFECTING


def _scatter_copies(srcs, lands, send_sems, recv_sems, waiting):
    x, y, c, chips = _place()
    n = len(srcs)
    return [pltpu.make_async_remote_copy(
        src_ref=srcs[p].at[2 * cx + cy], dst_ref=lands[p].at[j],
        send_sem=send_sems.at[j * n + p], recv_sem=recv_sems.at[j * n + p],
        device_id=(cx, cy, c), device_id_type=MESH)
        for j, (cx, cy) in enumerate(chips) for p in range(n)]


def _proj_copies(srcs, lands, send_sems, recv_sems, waiting):
    x, y, c, chips = _place()
    mine = 2 * x + y
    return [pltpu.make_async_remote_copy(
        src_ref=srcs[0].at[c], dst_ref=lands[0].at[mine, c],
        send_sem=send_sems.at[2 * j + to], recv_sem=recv_sems.at[2 * j + (to if waiting else c)],
        device_id=(cx, cy, to), device_id_type=MESH)
        for j, (cx, cy) in enumerate(chips) for to in range(2)]


def _start_copies(copies, nsem, parts, lands, name, after=None):
    n = len(parts)
    extra = [] if after is None else [after]

    def body(*refs):
        outs = refs[2 * n + len(extra):]
        for cp in copies(refs[:n], refs[n:2 * n], outs[0], outs[1], False):
            cp.start()
        outs[-1][...] = jnp.zeros_like(outs[-1])

    res = pl.pallas_call(
        body, name=name,
        out_shape=(pltpu.SemaphoreType.DMA((nsem,)), pltpu.SemaphoreType.DMA((nsem,)),
                   *[pltpu.HBM(a.shape, a.dtype) for a in parts], *[pltpu.HBM(a.shape, a.dtype) for a in lands],
                   jax.ShapeDtypeStruct((8, 128), F32)),
        in_specs=[HBM] * (2 * n) + [ANY] * len(extra),
        out_specs=(SEM, SEM, *[HBM] * (2 * n), pl.BlockSpec(memory_space=pltpu.VMEM)),
        input_output_aliases={i: 2 + i for i in range(2 * n)},
        compiler_params=pltpu.CompilerParams(has_side_effects=EFFECT),
    )(*[pltpu.with_memory_space_constraint(a, pltpu.HBM) for a in parts],
      *[pltpu.with_memory_space_constraint(a, pltpu.HBM) for a in lands], *extra)
    return res[0], res[1], res[2:2 + n], res[2 + n:2 + 2 * n], res[-1]


def _wait_copies(copies, started, after, name):
    send_sems, recv_sems, srcs, lands, _ = started
    n = len(srcs)

    def body(*refs):
        for cp in copies(refs[:n], refs[n:2 * n], refs[2 * n], refs[2 * n + 1], True):
            cp.wait_send()
            cp.wait_recv()

    res = pl.pallas_call(
        body, name=name,
        out_shape=tuple(pltpu.HBM(a.shape, a.dtype) for a in (*srcs, *lands)),
        in_specs=[HBM] * (2 * n) + [SEM, SEM, ANY], out_specs=tuple([HBM] * (2 * n)),
        input_output_aliases={i: i for i in range(2 * n)},
        compiler_params=pltpu.CompilerParams(has_side_effects=EFFECT),
    )(*srcs, *lands, send_sems, recv_sems, after)
    return res[n:]


def _join_halves(parts):
    npart = len(parts)

    def body(*refs):
        ins, outs = refs[:npart], refs[npart:2 * npart]
        send_sems, recv_sems = refs[2 * npart:]
        x, y, c, _ = _place()
        cps = [pltpu.make_async_remote_copy(
            src_ref=ins[p], dst_ref=outs[p], send_sem=send_sems.at[p], recv_sem=recv_sems.at[p],
            device_id=(x, y, 1 - c), device_id_type=MESH) for p in range(npart)]
        for cp in cps:
            cp.start()
        for cp in cps:
            cp.wait()

    return pl.pallas_call(
        body, name="join_halves", in_specs=[ANY] * npart, out_specs=[ANY] * npart,
        out_shape=[jax.ShapeDtypeStruct(a.shape, a.dtype) for a in parts],
        scratch_shapes=[pltpu.SemaphoreType.DMA((npart,)), pltpu.SemaphoreType.DMA((npart,))],
    )(*parts)


def _allreduce_small(v):
    r, ncol = v.shape

    def body(v_ref, o_ref, buf, send_sems, recv_sems):
        x, y, c, _ = _place()
        me = 4 * x + 2 * y + c
        buf[me] = v_ref[...]
        cps = []
        for k in range(1, 8):
            px, py, pc = x ^ (k >> 2), y ^ ((k >> 1) & 1), c ^ (k & 1)
            cps.append(pltpu.make_async_remote_copy(
                src_ref=v_ref, dst_ref=buf.at[me], send_sem=send_sems.at[k - 1], recv_sem=recv_sems.at[k - 1],
                device_id=(px, py, pc), device_id_type=MESH))
        for cp in cps:
            cp.start()
        for k in range(1, 8):
            px, py, pc = x ^ (k >> 2), y ^ ((k >> 1) & 1), c ^ (k & 1)
            pltpu.make_async_remote_copy(
                src_ref=v_ref, dst_ref=buf.at[4 * px + 2 * py + pc], send_sem=send_sems.at[k - 1],
                recv_sem=recv_sems.at[k - 1], device_id=(px, py, pc), device_id_type=MESH).wait_recv()
        for cp in cps:
            cp.wait_send()
        acc = buf[0]
        for d in range(1, 8):
            acc = acc + buf[d]
        o_ref[...] = acc

    return pl.pallas_call(
        body, name="allreduce_small",
        in_specs=[pl.BlockSpec(memory_space=pltpu.VMEM)], out_specs=pl.BlockSpec(memory_space=pltpu.VMEM),
        out_shape=jax.ShapeDtypeStruct((r, ncol), F32),
        scratch_shapes=[pltpu.VMEM((8, r, ncol), F32), pltpu.SemaphoreType.DMA((7,)), pltpu.SemaphoreType.DMA((7,))],
    )(v)


def _permute_rows(shards):
    w0, w1, w2, w3 = shards
    zeros = jnp.zeros((NPERM - 9280, w0.shape[1]), w0.dtype)
    return jnp.concatenate([w0, w1[0:1776], w1[1808:2320], w2, w3[0:240], w3[272:2320],
                            w1[1776:1808], w3[240:272], zeros], axis=0)


def _unpermute_rows(g):
    s1 = jnp.concatenate([g[2320:4096], g[9216:9248], g[4096:4608]], axis=0)
    s3 = jnp.concatenate([g[6928:7168], g[9248:9280], g[7168:9216]], axis=0)
    return jnp.stack([g[0:2320], s1, g[4608:6928], s3], axis=0)


def _pack_shard_small(conv, w2f, w2b):
    top = jnp.pad(conv, ((0, 8 - conv.shape[0]), (0, 0)))
    mid = jnp.pad(jnp.concatenate([w2f, w2b], axis=1), ((0, 0), (0, 768 - 256)))
    return jnp.concatenate([top, mid, jnp.zeros((8, 768), conv.dtype)], axis=0)


def _unpack_shard_small(a):
    return a[0:5], a[8:24, 0:128], a[8:24, 128:256]


def kernel(x, ln_pre_w, w_in, conv_w, a_log_fwd, a_log_bwd, dt_bias_fwd, dt_bias_bwd, gdn_norm_w, w_proj_gdn, gk_w2_fwd, gk_b2_fwd, gk_w2_bwd, gk_b2_bwd, gla_norm_w, w_proj_gla, w_out, ln_post_w, loss_target, m_ln_pre_w, m_w_in, m_conv_w, m_a_log_fwd, m_a_log_bwd, m_dt_bias_fwd, m_dt_bias_bwd, m_gdn_norm_w, m_w_proj_gdn, m_gk_w2_fwd, m_gk_b2_fwd, m_gk_w2_bwd, m_gk_b2_bwd, m_gla_norm_w, m_w_proj_gla, m_w_out, m_ln_post_w, v_ln_pre_w, v_w_in, v_conv_w, v_a_log_fwd, v_a_log_bwd, v_dt_bias_fwd, v_dt_bias_bwd, v_gdn_norm_w, v_w_proj_gdn, v_gk_w2_fwd, v_gk_b2_fwd, v_gk_w2_bwd, v_gk_b2_bwd, v_gla_norm_w, v_w_proj_gla, v_w_out, v_ln_post_w):
    t = x.shape[1]
    x2, tgt = x[0], loss_target[0]

    win_l = w_in[0].T.astype(BF16).reshape(2, SHW // 2, D)
    proj_l = jnp.concatenate([w_proj_gdn[0], w_proj_gla[0], w_out[0]], axis=0).astype(BF16).reshape(2, 384, D)
    small_l = _pack_shard_small(conv_w[0], gk_w2_fwd[0], gk_w2_bwd[0]).reshape(2, 16, 768)
    win_g, small_g = _gather_weights([win_l, small_l])
    proj_started = _start_copies(_proj_copies, 6, [proj_l], [lax.empty((NSHARD, 2, 384, D), BF16)],
                                 "gather_proj_start", after=small_g)
    wperm = _permute_rows(win_g.reshape(NSHARD, SHW, D))
    small_g = small_g.reshape(NSHARD, 32, 768)
    convw = small_g[:, 0:8, :].transpose(1, 0, 2).reshape(8, 3 * D)
    w2f = small_g[:, 8:24, 0:128].transpose(1, 0, 2).reshape(16, 512)
    w2b = small_g[:, 8:24, 128:256].transpose(1, 0, 2).reshape(16, 512)
    w2f_pad = jnp.pad(w2f, ((32, 80), (0, 0)))
    w2b_pad = jnp.pad(w2b, ((48, 64), (0, 0)))
    alog_row = jnp.pad(jnp.concatenate([a_log_fwd, a_log_bwd], axis=1), ((0, 0), (0, 112)))
    dt_row = jnp.pad(jnp.concatenate([dt_bias_fwd, dt_bias_bwd], axis=1), ((0, 0), (0, 112)))

    p, h = _inproj(x2, ln_pre_w + proj_started[4][0:1, 0:1], wperm)
    qn, kn, vc = (_qkv_fwd(p, convw, kind) for kind in range(3))
    gsm, gk = _gates_fwd(p, alog_row, dt_row, w2f_pad, gk_b2_fwd, w2b_pad, gk_b2_bwd)
    g2, b2 = _gcum_fwd(gsm)
    u, w, at, qd, kd, el, tinv = _gdn_intra_fwd(qn, kn, vc, g2, b2)
    oa2, sa = _gdn_scan_fwd(u, w, at, qd, kd, el)
    qg, kdb, intra, elb = _gla_intra_fwd(p, gk)
    ob2, sb = _gla_scan_fwd(p, qg, kdb, intra, elb)

    (proj_land,) = _wait_copies(_proj_copies, proj_started, ob2, "gather_proj_wait")
    mine = 2 * lax.axis_index("x") + lax.axis_index("y")
    w3 = lax.dynamic_update_index_in_dim(proj_land, proj_l, mine, 0).reshape(NSHARD, 3, D // NSHARD, D)
    (loss8, doa, dob, dz, dgb, dga, dgB, dyres, dw3, dgdn_w, dgla_w, dlnpost) = _post(
        oa2, ob2, p, x2, tgt, gdn_norm_w, gla_norm_w, ln_post_w, w3)

    du, dw, dat, dqd, dkd, del_ = _gdn_scan_bwd(u, w, at, qd, kd, el, sa, doa)
    dqn, dkn, dvc, dg2, db2 = _gdn_intra_bwd(qn, kn, vc, g2, b2, tinv, du, dw, dat, dqd, dkd, del_)
    dgsm = _gcum_bwd(gsm, dg2, db2)
    dqg, dkdb, dvs, delb = _gla_scan_bwd(p, qg, kdb, elb, sb, dob)
    dqb, dkb, dvb, dgk = _gla_intra_bwd(p, gk, dqg, dkdb, dvs, delb, dob)
    (dps, dalog8, ddt8, dw2f_pad, db2f8, dw2b_pad, db2b8) = _gates_bwd(
        p, alog_row, dt_row, w2f_pad, gk_b2_fwd, w2b_pad, gk_b2_bwd, dgsm, dgk)
    dpre, dconv = zip(*[_qkv_bwd(p, convw, g, kind) for kind, g in enumerate((dqn, dkn, dvc))])

    pieces = (*dpre, dz, dqb, dkb, dvb, dgb, dga, dgB, dps)
    dwperm = _inproj_dw(h, pieces)

    g_in = _unpermute_rows(dwperm).reshape(NSHARD, 2, SHW // 2, D)
    g_proj = dw3.reshape(NSHARD, 2, 384, D)
    dconv_full = jnp.concatenate(dconv, axis=1)
    dw2f, dw2b = dw2f_pad[32:48], dw2b_pad[48:64]
    g_small = jnp.stack([_pack_shard_small(dconv_full[0:5, 768 * s:768 * (s + 1)],
                                           dw2f[:, 128 * s:128 * (s + 1)], dw2b[:, 128 * s:128 * (s + 1)])
                         for s in range(NSHARD)])
    g_small = g_small.reshape(NSHARD, 2, 16, 768)
    parts = [g_in, g_proj, g_small]
    got = _swap_halves(parts)
    sums = [_sum_cast(a, b) for a, b in zip(parts, got)]
    hbs = [hb for _, hb in sums]
    started = _start_copies(_scatter_copies, 3 * len(hbs), hbs,
                            [lax.empty((3,) + a.shape[1:], a.dtype) for a in hbs], "scatter_start")
    dx, dlnpre8 = _inproj_dx(pieces, wperm, x2, ln_pre_w + started[4][0:1, 0:1], dyres)
    landed = _wait_copies(_scatter_copies, started, dx, "scatter_wait")
    halves = [_sum4(f, g) for (f, _), g in zip(sums, landed)]
    theirs = _join_halves(halves)

    gsmall = _allreduce_small(jnp.concatenate(
        [dlnpre8, dalog8, ddt8, dgdn_w, db2f8, db2b8, dgla_w, dlnpost], axis=1))

    a_in = [a.T for a in _adam(w_in[0].T, halves[0], theirs[0], m_w_in[0].T, v_w_in[0].T, 232, name="in")]
    a_pr = [_adam(w[0], halves[1], theirs[1], m[0], v[0], 128, tile0=2 * i, name=f"proj{i}")
            for i, (w, m, v) in enumerate(((w_proj_gdn, m_w_proj_gdn, v_w_proj_gdn),
                                           (w_proj_gla, m_w_proj_gla, v_w_proj_gla), (w_out, m_w_out, v_w_out)))]
    a_ss = _adam(_pack_shard_small(conv_w[0], gk_w2_fwd[0], gk_w2_bwd[0]), halves[2], theirs[2],
                 _pack_shard_small(m_conv_w[0], m_gk_w2_fwd[0], m_gk_w2_bwd[0]),
                 _pack_shard_small(v_conv_w[0], v_gk_w2_fwd[0], v_gk_w2_bwd[0]), 16, name="small")
    smalls = dict(ln_pre_w=(ln_pre_w, m_ln_pre_w, v_ln_pre_w), a_log_fwd=(a_log_fwd, m_a_log_fwd, v_a_log_fwd),
                  a_log_bwd=(a_log_bwd, m_a_log_bwd, v_a_log_bwd),
                  dt_bias_fwd=(dt_bias_fwd, m_dt_bias_fwd, v_dt_bias_fwd),
                  dt_bias_bwd=(dt_bias_bwd, m_dt_bias_bwd, v_dt_bias_bwd),
                  gdn_norm_w=(gdn_norm_w, m_gdn_norm_w, v_gdn_norm_w),
                  gk_b2_fwd=(gk_b2_fwd, m_gk_b2_fwd, v_gk_b2_fwd), gk_b2_bwd=(gk_b2_bwd, m_gk_b2_bwd, v_gk_b2_bwd),
                  gla_norm_w=(gla_norm_w, m_gla_norm_w, v_gla_norm_w), ln_post_w=(ln_post_w, m_ln_post_w, v_ln_post_w))
    names = [name for name, _, _, _ in SMALL_SLOTS]
    small = _adam_small(gsmall, *([smalls[n][i] for n in names] for i in range(3)))

    def family(k):
        conv, w2f_, w2b_ = _unpack_shard_small(a_ss[k])
        s = {n: small[n][k] for n in names}
        return [s["ln_pre_w"], a_in[k][None], conv[None], s["a_log_fwd"], s["a_log_bwd"], s["dt_bias_fwd"],
                s["dt_bias_bwd"], s["gdn_norm_w"], a_pr[0][k][None], w2f_[None], s["gk_b2_fwd"], w2b_[None],
                s["gk_b2_bwd"], s["gla_norm_w"], a_pr[1][k][None], a_pr[2][k][None], s["ln_post_w"]]

    loss = lax.psum(loss8[0, 0], ("x", "y", "c"))
    return (loss, dx[None], *family(0), *family(1), *family(2), *family(3))
```

```python
import functools

import jax
import jax.numpy as jnp
from jax import lax
from jax.experimental import pallas as pl
from jax.experimental.pallas import tpu as pltpu

F32 = jnp.float32
BF16 = jnp.bfloat16
HI = lax.Precision.HIGHEST
MESH = pl.DeviceIdType.MESH

D = 1024
CH = 64
EPS = 1e-6
NA, DA = 8, 128
NB, DKB, DVB = 4, 128, 256
NSHARD = 4
SHW = 2320
NPERM = 9728
PS_BLOCK = 72
LR, B1, B2, ADAM_EPS, WD, STEP = 0.001, 0.9, 0.999, 1e-08, 0.01, 10

ANY = pl.BlockSpec(memory_space=pl.ANY)


def _cparams(sem=None, vmem_mb=48):
    return pltpu.CompilerParams(dimension_semantics=sem, vmem_limit_bytes=vmem_mb << 20)


def _bdot(a, b, ca, cb):
    return lax.dot_general(a.astype(BF16), b.astype(BF16), (((ca,), (cb,)), ((), ())),
                           preferred_element_type=F32)


@jax.custom_vjp
def mm(a, b):
    return _bdot(a, b, 1, 0)


def _mm_fwd(a, b):
    return _bdot(a, b, 1, 0), (a, b)


def _mm_bwd(res, g):
    a, b = res
    return _bdot(g, b, 1, 1), _bdot(a, g, 0, 0)


mm.defvjp(_mm_fwd, _mm_bwd)


@jax.custom_vjp
def mm_nt(a, b):
    return _bdot(a, b, 1, 1)


def _mm_nt_fwd(a, b):
    return _bdot(a, b, 1, 1), (a, b)


def _mm_nt_bwd(res, g):
    a, b = res
    return _bdot(g, b, 1, 0), _bdot(g, a, 0, 0)


mm_nt.defvjp(_mm_nt_fwd, _mm_nt_bwd)


@jax.custom_vjp
def mm_tn(a, b):
    return _bdot(a, b, 0, 0)


def _mm_tn_fwd(a, b):
    return _bdot(a, b, 0, 0), (a, b)


def _mm_tn_bwd(res, g):
    a, b = res
    return _bdot(b, g, 1, 1), _bdot(a, g, 1, 0)


mm_tn.defvjp(_mm_tn_fwd, _mm_tn_bwd)


def dot_hi(a, b):
    return lax.dot_general(a, b, (((1,), (0,)), ((), ())), precision=HI, preferred_element_type=F32)


def _split3(x):
    x1 = x.astype(BF16)
    r = x - x1.astype(F32)
    x2 = r.astype(BF16)
    return x1, x2, (r - x2.astype(F32)).astype(BF16)


def _cdot(c, x, cc, cx, c_first=True):
    parts = _split3(x)
    if c_first:
        return _bdot(c, parts[0], cc, cx) + _bdot(c, parts[1], cc, cx) + _bdot(c, parts[2], cc, cx)
    return _bdot(parts[0], c, cx, cc) + _bdot(parts[1], c, cx, cc) + _bdot(parts[2], c, cx, cc)


@jax.custom_vjp
def cmm(c, x):
    return _cdot(c, x, 1, 0)


def _cmm_fwd(c, x):
    return _cdot(c, x, 1, 0), c


def _cmm_bwd(c, g):
    return jnp.zeros_like(c), _cdot(c, g, 0, 0)


cmm.defvjp(_cmm_fwd, _cmm_bwd)


@jax.custom_vjp
def mmc(x, c):
    return _cdot(c, x, 0, 1, c_first=False)


def _mmc_fwd(x, c):
    return _cdot(c, x, 0, 1, c_first=False), c


def _mmc_bwd(c, g):
    return _cdot(c, g, 1, 1, c_first=False), jnp.zeros_like(c)


mmc.defvjp(_mmc_fwd, _mmc_bwd)


def _sigmoid(x):
    return 1.0 / (1.0 + jnp.exp(-x))


def _silu(x):
    return x * _sigmoid(x)


def _softplus(x):
    return jnp.maximum(x, 0.0) + jnp.log(1.0 + jnp.exp(-jnp.abs(x)))


def _rms(x, w):
    return x * lax.rsqrt(jnp.mean(x * x, axis=-1, keepdims=True) + EPS) * w


SC = 256


class _Consts:
    def __init__(self, rev):
        r = lax.broadcasted_iota(jnp.int32, (SC, SC), 0)
        c = lax.broadcasted_iota(jnp.int32, (SC, SC), 1)
        same = (r >> 6) == (c >> 6)
        a = jnp.where(rev, c, r)
        b = jnp.where(rev, r, c)
        self.incl = same & (a >= b)
        self.strict = same & (a > b)
        self.incl_f = self.incl.astype(F32)
        self.eye = (r == c).astype(F32)
        rows = lax.broadcasted_iota(jnp.int32, (SC, 1), 0)
        self.last_col = ((rows & (CH - 1)) == jnp.where(rev, 0, CH - 1)).astype(F32)
        rr = lax.broadcasted_iota(jnp.int32, (SC, CH), 0)
        cc = lax.broadcasted_iota(jnp.int32, (SC, CH), 1)
        self.fold = ((rr & (CH - 1)) == cc).astype(F32)


def _dot3(a, b, ca=1, cb=0):
    ah, bh = a.astype(BF16), b.astype(BF16)
    al, bl = (a - ah.astype(F32)).astype(BF16), (b - bh.astype(F32)).astype(BF16)
    return _bdot(ah, bh, ca, cb) + (_bdot(ah, bl, ca, cb) + _bdot(al, bh, ca, cb))


TRI_SPLIT_LEVELS = 2


def _tri_inv(low, eye):
    n = -low
    acc = eye + n
    p = n
    for level in range(5):
        dot = _dot3 if level < TRI_SPLIT_LEVELS else (lambda a, b: _bdot(a, b, 1, 0))
        p = dot(p, p)
        acc = acc + dot(acc, p)
    return acc


@jax.custom_vjp
def _solve2(low, rv, rk, tinv):
    x = _dot3(tinv, jnp.concatenate([rv, rk], axis=1))
    return x[:, :DA], x[:, DA:]


def _solve2_fwd(low, rv, rk, tinv):
    x = _dot3(tinv, jnp.concatenate([rv, rk], axis=1))
    return (x[:, :DA], x[:, DA:]), (x, tinv)


def _solve2_bwd(res, g):
    x, tinv = res
    drhs = _dot3(tinv, jnp.concatenate(g, axis=1), 0, 0)
    return -_dot3(drhs, x, 1, 1), drhs[:, :DA], drhs[:, DA:], jnp.zeros_like(tinv)


_solve2.defvjp(_solve2_fwd, _solve2_bwd)


def _chunk_last(x, cs):
    xs = (x * cs.last_col).reshape(SC // CH, CH, x.shape[1])
    return jnp.broadcast_to(jnp.sum(xs, axis=1, keepdims=True), xs.shape).reshape(x.shape)


def _gdn_decay(g, cs):
    gw = jnp.concatenate([g] * (SC // DA), axis=1)
    grow = jnp.sum(cs.eye * gw, axis=0, keepdims=True)
    return jnp.where(cs.incl, jnp.exp(jnp.where(cs.incl, gw - grow, 0.0)), 0.0)


def _gdn_intra(q, k, v, g, bx, tinv, cs):
    decay = _gdn_decay(g, cs)
    kb = k * bx
    low = jnp.where(cs.strict, mm_nt(kb, k) * decay, 0.0)
    eg = jnp.exp(g)
    made = tinv is None
    if made:
        tinv = _tri_inv(low, cs.eye)
    u, w = _solve2(low, v * bx, kb * eg, tinv)
    attn = mmc(mm_nt(q, k) * decay, cs.fold)
    qd = q * eg
    glast = _chunk_last(g, cs)
    kd = k * jnp.exp(glast - g)
    outs = (u, w, attn, qd, kd, jnp.exp(glast))
    return outs + (tinv,) if made else outs


def _gdn_scan(u, w, attn, qd, kd, el, s):
    vn = u - mm(w, s)
    o = mm(qd, s) + mm(attn, vn)
    sn = s * el + mm_tn(kd, vn)
    return o, sn


def _gla_intra(q, k, v, gk, cs):
    gc = cmm(cs.incl_f, gk)
    qg = q * (DKB ** -0.5) * jnp.exp(gc)
    kg = k * jnp.exp(-gc)
    attn = jnp.where(cs.incl, mm_nt(qg, kg), 0.0)
    intra = mm(attn, v)
    glast = _chunk_last(gc, cs)
    kd = k * jnp.exp(glast - gc)
    return qg, kd, intra, jnp.exp(glast)


def _gla_scan(qg, kd, v, el, st):
    o = mm_nt(qg, st)
    stn = st * el + mm_tn(v, kd)
    return o, stn


def _shift_rows(x, s):
    if s == 0:
        return x
    t = x.shape[0]
    rolled = pltpu.roll(x, (-s) % t, 0)
    rows = lax.broadcasted_iota(jnp.int32, x.shape, 0)
    return jnp.where((rows + s >= 0) & (rows + s < t), rolled, 0.0)


@jax.custom_vjp
def _conv5(x, w):
    acc = w[0:1] * _shift_rows(x, -2)
    for j in range(1, 5):
        acc = acc + w[j:j + 1] * _shift_rows(x, j - 2)
    return acc


def _conv5_fwd(x, w):
    return _conv5(x, w), (x, w)


def _conv5_bwd(res, g):
    x, w = res
    dx = w[0:1] * _shift_rows(g, 2)
    for j in range(1, 5):
        dx = dx + w[j:j + 1] * _shift_rows(g, 2 - j)
    rows = lax.broadcasted_iota(jnp.int32, w.shape, 0)
    dw = jnp.zeros_like(w)
    for j in range(5):
        dwj = jnp.sum(g * _shift_rows(x, j - 2), axis=0, keepdims=True)
        dw = dw + jnp.where(rows == j, dwj, 0.0)
    return dx, dw


_conv5.defvjp(_conv5_fwd, _conv5_bwd)


def _qkv_act(kind):
    def f(x, w):
        c = _silu(_conv5(x, w))
        if kind == 2:
            return c
        c = c * lax.rsqrt(jnp.sum(c * c, axis=-1, keepdims=True) + EPS)
        return c * (DA ** -0.5) if kind == 0 else c
    return f


def _inproj(x, lnw, wperm, tm=512, tn=512):
    t = x.shape[0]

    def body(x_ref, lnw_ref, w_ref, p_ref, h_ref, hbuf):
        @pl.when(pl.program_id(1) == 0)
        def _():
            hb = _rms(x_ref[...], lnw_ref[...]).astype(BF16)
            hbuf[...] = hb
            h_ref[...] = hb
        p_ref[...] = _bdot(hbuf[...], w_ref[...], 1, 1)

    return pl.pallas_call(
        body, name="inproj", grid=(t // tm, NPERM // tn),
        in_specs=[pl.BlockSpec((tm, D), lambda i, j: (i, 0)),
                  pl.BlockSpec((1, D), lambda i, j: (0, 0)),
                  pl.BlockSpec((tn, D), lambda i, j: (j, 0))],
        out_specs=[pl.BlockSpec((tm, tn), lambda i, j: (i, j)),
                   pl.BlockSpec((tm, D), lambda i, j: (i, 0))],
        out_shape=[jax.ShapeDtypeStruct((t, NPERM), F32),
                   jax.ShapeDtypeStruct((t, D), BF16)],
        scratch_shapes=[pltpu.VMEM((tm, D), BF16)],
        compiler_params=_cparams(("parallel", "arbitrary")),
    )(x, lnw, wperm)


DP_TILE = 512
DP_PIECES = ((0, 19),)


def _piece_specs(tm, j_first):
    specs = []
    for j0, n in DP_PIECES:
        def imap(a, b, j0=j0, n=n):
            j, i = (a, b) if j_first else (b, a)
            inside = (j >= j0) & (j < j0 + n)
            return jnp.where(inside, i, 0), jnp.clip(j - j0, 0, n - 1)
        specs.append(pl.BlockSpec((tm, DP_TILE), imap))
    return specs


def _for_piece(j, refs, fn):
    for (j0, n), ref in zip(DP_PIECES, refs):
        @pl.when((j >= j0) & (j < j0 + n))
        def _(ref=ref):
            fn(ref[...])


def _inproj_dw(h, pieces, tm=512):
    t = h.shape[0]
    npc = len(pieces)

    def body(h_ref, *refs):
        dw_ref = refs[npc]

        @pl.when(pl.program_id(1) == 0)
        def _():
            dw_ref[...] = jnp.zeros_like(dw_ref)

        def add(dp):
            dw_ref[...] += _bdot(dp, h_ref[...], 0, 0)
        _for_piece(pl.program_id(0), refs[:npc], add)

    return pl.pallas_call(
        body, name="inproj_dw", grid=(NPERM // DP_TILE, t // tm),
        in_specs=[pl.BlockSpec((tm, D), lambda j, i: (i, 0))] + _piece_specs(tm, True),
        out_specs=pl.BlockSpec((DP_TILE, D), lambda j, i: (j, 0)),
        out_shape=jax.ShapeDtypeStruct((NPERM, D), F32),
        compiler_params=_cparams(("parallel", "arbitrary")),
    )(h, *pieces)


def _inproj_dx(pieces, wperm, x, lnw, dyres, tm=512):
    t = x.shape[0]
    tn = DP_TILE
    nj = NPERM // tn
    npc = len(pieces)

    def body(*refs):
        w_ref, x_ref, lnw_ref, dy_ref, dx_ref, dlnw_ref, acc = refs[npc:]
        j = pl.program_id(1)

        @pl.when(j == 0)
        def _():
            acc[...] = jnp.zeros_like(acc)

        def add(dp):
            acc[...] += _bdot(dp, w_ref[...], 1, 0)
        _for_piece(j, refs[:npc], add)

        @pl.when(j == nj - 1)
        def _():
            _, vjp = jax.vjp(_rms, x_ref[...], lnw_ref[...])
            dx, dlnw = vjp(acc[...])
            dx_ref[...] = dx + dy_ref[...]

            @pl.when(pl.program_id(0) == 0)
            def _():
                dlnw_ref[...] = jnp.zeros_like(dlnw_ref)
            dlnw_ref[...] += jnp.broadcast_to(dlnw, dlnw_ref.shape)

    return pl.pallas_call(
        body, name="inproj_dx", grid=(t // tm, nj),
        in_specs=_piece_specs(tm, False) + [
                  pl.BlockSpec((tn, D), lambda i, j: (j, 0)),
                  pl.BlockSpec((tm, D), lambda i, j: (i, 0)),
                  pl.BlockSpec((1, D), lambda i, j: (0, 0)),
                  pl.BlockSpec((tm, D), lambda i, j: (i, 0))],
        out_specs=[pl.BlockSpec((tm, D), lambda i, j: (i, 0)),
                   pl.BlockSpec((8, D), lambda i, j: (0, 0))],
        out_shape=[jax.ShapeDtypeStruct((t, D), F32), jax.ShapeDtypeStruct((8, D), F32)],
        scratch_shapes=[pltpu.VMEM((tm, D), F32)],
        compiler_params=_cparams(("arbitrary", "arbitrary")),
    )(*pieces, wperm, x, lnw, dyres)


def _qkv_fwd(p, convw, kind):
    t = p.shape[0]
    f = _qkv_act(kind)

    def body(p_ref, w_ref, o_ref):
        o_ref[...] = f(p_ref[...], w_ref[...])

    return pl.pallas_call(
        body, name=f"qkv_fwd{kind}", grid=(NA,),
        in_specs=[pl.BlockSpec((t, DA), lambda h: (0, kind * NA + h)),
                  pl.BlockSpec((8, DA), lambda h: (0, kind * NA + h))],
        out_specs=pl.BlockSpec((t, DA), lambda h: (0, h)),
        out_shape=jax.ShapeDtypeStruct((t, D), F32),
        compiler_params=_cparams(("parallel",)),
    )(p, convw)


def _qkv_bwd(p, convw, dout, kind):
    t = p.shape[0]
    f = _qkv_act(kind)

    def body(p_ref, w_ref, g_ref, dx_ref, dw_ref):
        _, vjp = jax.vjp(f, p_ref[...], w_ref[...])
        dx, dw = vjp(g_ref[...])
        dx_ref[...] = dx.astype(BF16)
        dw_ref[...] = dw

    return pl.pallas_call(
        body, name=f"qkv_bwd{kind}", grid=(NA,),
        in_specs=[pl.BlockSpec((t, DA), lambda h: (0, kind * NA + h)),
                  pl.BlockSpec((8, DA), lambda h: (0, kind * NA + h)),
                  pl.BlockSpec((t, DA), lambda h: (0, h))],
        out_specs=[pl.BlockSpec((t, DA), lambda h: (0, h)),
                   pl.BlockSpec((8, DA), lambda h: (0, h))],
        out_shape=[jax.ShapeDtypeStruct((t, D), BF16), jax.ShapeDtypeStruct((8, D), F32)],
        compiler_params=_cparams(("parallel",)),
    )(p, convw, dout)


def _gates_f(ps, alog_row, dt_row, w2f, b2f, w2b, b2b):
    lane = lax.broadcasted_iota(jnp.int32, ps.shape, 1)
    lg = -jnp.exp(alog_row) * _softplus(ps + dt_row)
    gsm = jnp.where(lane < 16, lg, jnp.where(lane < 32, _sigmoid(ps), 0.0))
    gkf = -_softplus(-(mm(ps, w2f) + b2f)) * (1.0 / 16.0)
    gkb = -_softplus(-(mm(ps, w2b) + b2b)) * (1.0 / 16.0)
    return gsm, gkf, gkb


def _gates_fwd(ps, alog_row, dt_row, w2f, b2f, w2b, b2b, tm=512):
    t = ps.shape[0]

    def body(ps_ref, a_ref, d_ref, wf_ref, bf_ref, wb_ref, bb_ref, gsm_ref, gk_ref):
        gsm, gkf, gkb = _gates_f(ps_ref[...], a_ref[...], d_ref[...], wf_ref[...], bf_ref[...],
                                 wb_ref[...], bb_ref[...])
        gsm_ref[...] = gsm
        gk_ref[0] = gkf
        gk_ref[1] = gkb

    row = lambda n: pl.BlockSpec((1, n), lambda i: (0, 0))
    mat = pl.BlockSpec((128, 512), lambda i: (0, 0))
    return pl.pallas_call(
        body, name="gates_fwd", grid=(t // tm,),
        in_specs=[pl.BlockSpec((tm, 128), lambda i: (i, PS_BLOCK)), row(128), row(128), mat, row(512), mat, row(512)],
        out_specs=[pl.BlockSpec((tm, 128), lambda i: (i, 0)),
                   pl.BlockSpec((2, tm, 512), lambda i: (0, i, 0))],
        out_shape=[jax.ShapeDtypeStruct((t, 128), F32), jax.ShapeDtypeStruct((2, t, 512), F32)],
        compiler_params=_cparams(("parallel",)),
    )(ps, alog_row, dt_row, w2f, b2f, w2b, b2b)


def _gates_bwd(ps, alog_row, dt_row, w2f, b2f, w2b, b2b, dgsm, dgk, tm=512):
    t = ps.shape[0]

    def body(ps_ref, a_ref, d_ref, wf_ref, bf_ref, wb_ref, bb_ref, dgsm_ref, dgk_ref,
             dps_ref, da_ref, dd_ref, dwf_ref, dbf_ref, dwb_ref, dbb_ref):
        _, vjp = jax.vjp(_gates_f, ps_ref[...], a_ref[...], d_ref[...], wf_ref[...], bf_ref[...],
                         wb_ref[...], bb_ref[...])
        dps, da, dd, dwf, dbf, dwb, dbb = vjp((dgsm_ref[...], dgk_ref[0], dgk_ref[1]))
        dps_ref[:, 0:128] = dps.astype(BF16)
        dps_ref[:, 128:DP_TILE] = jnp.zeros((tm, DP_TILE - 128), BF16)
        accs = ((da_ref, da), (dd_ref, dd), (dwf_ref, dwf), (dbf_ref, dbf), (dwb_ref, dwb), (dbb_ref, dbb))

        @pl.when(pl.program_id(0) == 0)
        def _():
            for ref, _ in accs:
                ref[...] = jnp.zeros_like(ref)
        for ref, val in accs:
            ref[...] += jnp.broadcast_to(val, ref.shape)

    row = lambda n: pl.BlockSpec((1, n), lambda i: (0, 0))
    row8 = lambda n: pl.BlockSpec((8, n), lambda i: (0, 0))
    mat = pl.BlockSpec((128, 512), lambda i: (0, 0))
    return pl.pallas_call(
        body, name="gates_bwd", grid=(t // tm,),
        in_specs=[pl.BlockSpec((tm, 128), lambda i: (i, PS_BLOCK)), row(128), row(128), mat, row(512), mat, row(512),
                  pl.BlockSpec((tm, 128), lambda i: (i, 0)),
                  pl.BlockSpec((2, tm, 512), lambda i: (0, i, 0))],
        out_specs=[pl.BlockSpec((tm, DP_TILE), lambda i: (i, 0)), row8(128), row8(128), mat, row8(512), mat,
                   row8(512)],
        out_shape=[jax.ShapeDtypeStruct((t, DP_TILE), BF16),
                   jax.ShapeDtypeStruct((8, 128), F32), jax.ShapeDtypeStruct((8, 128), F32),
                   jax.ShapeDtypeStruct((128, 512), F32), jax.ShapeDtypeStruct((8, 512), F32),
                   jax.ShapeDtypeStruct((128, 512), F32), jax.ShapeDtypeStruct((8, 512), F32)],
        compiler_params=_cparams(("arbitrary",)),
    )(ps, alog_row, dt_row, w2f, b2f, w2b, b2b, dgsm, dgk)


def _rows(i):
    return pl.ds(pl.multiple_of(i * CH, CH), CH)


def _srows(i):
    return pl.ds(pl.multiple_of(i * SC, SC), SC)


def _first_row(x):
    row = lax.broadcasted_iota(jnp.int32, (8, x.shape[1]), 0)
    return jnp.where(row == 0, jnp.broadcast_to(x, (8, x.shape[1])), 0.0)


def _chunk_rows(e_ref, i):
    pad = jnp.zeros((CH - 8, 128), F32)
    return jnp.concatenate([x for c in range(SC // CH) for x in (e_ref[(SC // CH) * i + c], pad)], axis=0)


def _gcum_f(gsm, tm):
    i = lax.broadcasted_iota(jnp.int32, (tm, tm), 0)
    j = lax.broadcasted_iota(jnp.int32, (tm, tm), 1)
    same = (i >> 6) == (j >> 6)
    lower = (same & (i >= j)).astype(F32)
    upper = (same & (i <= j)).astype(F32)
    r = lax.broadcasted_iota(jnp.int32, (128, D), 0)
    head = lax.broadcasted_iota(jnp.int32, (128, D), 1) >> 7
    pick = lambda off: (r == head + off).astype(F32)
    lane = lax.broadcasted_iota(jnp.int32, gsm.shape, 1)
    run = jnp.where(lane < 8, cmm(lower, gsm), cmm(upper, gsm))
    return mmc(run, pick(0)), mmc(run, pick(8)), mmc(gsm, pick(16)), mmc(gsm, pick(24))


def _gcum_fwd(gsm, tm=256):
    t = gsm.shape[0]

    def body(s_ref, g_ref, b_ref):
        gf, gb, bf, bb = _gcum_f(s_ref[...], tm)
        g_ref[0] = gf
        g_ref[1] = gb
        b_ref[0] = bf
        b_ref[1] = bb

    two = pl.BlockSpec((2, tm, D), lambda i: (0, i, 0))
    return pl.pallas_call(
        body, name="gcum_fwd", grid=(t // tm,),
        in_specs=[pl.BlockSpec((tm, 128), lambda i: (i, 0))], out_specs=[two, two],
        out_shape=[jax.ShapeDtypeStruct((2, t, D), F32)] * 2,
        compiler_params=_cparams(("parallel",)),
    )(gsm)


def _gcum_bwd(gsm, dg2, db2, tm=256):
    t = gsm.shape[0]

    def body(s_ref, dg_ref, db_ref, ds_ref):
        _, vjp = jax.vjp(lambda s: _gcum_f(s, tm), s_ref[...])
        ds_ref[...] = vjp((dg_ref[0], dg_ref[1], db_ref[0], db_ref[1]))[0]

    two = pl.BlockSpec((2, tm, D), lambda i: (0, i, 0))
    tile = pl.BlockSpec((tm, 128), lambda i: (i, 0))
    return pl.pallas_call(
        body, name="gcum_bwd", grid=(t // tm,),
        in_specs=[tile, two, two], out_specs=tile,
        out_shape=jax.ShapeDtypeStruct((t, 128), F32),
        compiler_params=_cparams(("parallel",)),
    )(gsm, dg2, db2)


def _gdn_intra_fwd(qn, kn, vc, g2, b2):
    t = qn.shape[0]
    n = t // CH

    def body(q_ref, k_ref, v_ref, g_ref, b_ref, u_ref, w_ref, a_ref, qd_ref, kd_ref, e_ref, t_ref):
        cs = _Consts(pl.program_id(0) == 1)

        def step(i, carry):
            r = _srows(i)
            q, k, v, g, bx = q_ref[r, :], k_ref[r, :], v_ref[r, :], g_ref[r, :], b_ref[r, :]
            u, w, a, qd, kd, el, tinv = _gdn_intra(q, k, v, g, bx, None, cs)
            u_ref[r, :] = u
            w_ref[r, :] = w
            a_ref[r, :] = a
            qd_ref[r, :] = qd
            kd_ref[r, :] = kd
            t_ref[r, :] = tinv
            for c in range(SC // CH):
                e_ref[(SC // CH) * i + c] = el[c * CH:c * CH + 8]
            return carry

        lax.fori_loop(0, t // SC, step, 0)

    head = pl.BlockSpec((t, DA), lambda d, h: (0, h))
    dh = pl.BlockSpec((None, t, DA), lambda d, h: (d, 0, h))
    sq = lambda w: pl.BlockSpec((None, None, t, w), lambda d, h: (d, h, 0, 0))
    big = jax.ShapeDtypeStruct((2, t, D), F32)
    return pl.pallas_call(
        body, name="gdn_intra_fwd", grid=(2, NA),
        in_specs=[head, head, head, dh, dh],
        out_specs=[dh, dh, sq(CH), dh, dh, pl.BlockSpec((None, None, n, 8, 128), lambda d, h: (d, h, 0, 0, 0)),
                   sq(SC)],
        out_shape=[big, big, jax.ShapeDtypeStruct((2, NA, t, CH), F32), big, big,
                   jax.ShapeDtypeStruct((2, NA, n, 8, 128), F32), jax.ShapeDtypeStruct((2, NA, t, SC), F32)],
        compiler_params=_cparams(("parallel", "parallel")),
    )(qn, kn, vc, g2, b2)


SCAN_TB = 256
SCAN_HB = 8


def _scan_specs(t, width, nheads, hb, along):
    nt = t // SCAN_TB
    nb = SCAN_TB // CH

    def tmap(d, tt):
        fwd = tt + d * (nt - 1 - 2 * tt)
        return fwd if along > 0 else nt - 1 - fwd

    tok = pl.BlockSpec((None, SCAN_TB, hb * width), lambda d, h, tt: (d, tmap(d, tt), h))
    per = lambda *tail: pl.BlockSpec((None, hb, nb) + tail, lambda d, h, tt: (d, h, tmap(d, tt)) + (0,) * len(tail))
    sq = pl.BlockSpec((None, hb, SCAN_TB, CH), lambda d, h, tt: (d, h, tmap(d, tt), 0))
    shared = lambda w: pl.BlockSpec((SCAN_TB, hb * w), lambda d, h, tt: (tmap(d, tt), h))
    return tok, per, sq, shared, (2, nheads // hb, nt), nb


def _gdn_scan_fwd(u, w, a, qd, kd, e):
    t = u.shape[1]
    tok, per, sq, _, grid, nb = _scan_specs(t, DA, NA, SCAN_HB, +1)

    def body(u_ref, w_ref, a_ref, qd_ref, kd_ref, e_ref, o_ref, s_ref, state):
        rev = pl.program_id(0) == 1

        @pl.when(pl.program_id(2) == 0)
        def _():
            state[...] = jnp.zeros_like(state)

        def step(i, ss):
            ci = jnp.where(rev, nb - 1 - i, i)
            r = _rows(ci)
            out = []
            for hh, s in enumerate(ss):
                c = slice(hh * DA, (hh + 1) * DA)
                s_ref[hh, ci] = s
                o, sn = _gdn_scan(u_ref[r, c], w_ref[r, c], a_ref[hh, r, :], qd_ref[r, c], kd_ref[r, c],
                                  e_ref[hh, ci][0:1], s)
                o_ref[r, c] = o
                out.append(sn)
            return tuple(out)

        ss = lax.fori_loop(0, nb, step, tuple(state[hh] for hh in range(SCAN_HB)))
        for hh, s in enumerate(ss):
            state[hh] = s

    return pl.pallas_call(
        body, name="gdn_scan_fwd", grid=grid,
        in_specs=[tok, tok, sq, tok, tok, per(8, 128)],
        out_specs=[tok, per(DA, DA)],
        out_shape=[jax.ShapeDtypeStruct((2, t, D), F32), jax.ShapeDtypeStruct((2, NA, t // CH, DA, DA), F32)],
        scratch_shapes=[pltpu.VMEM((SCAN_HB, DA, DA), F32)],
        compiler_params=_cparams(("parallel", "parallel", "arbitrary")),
    )(u, w, a, qd, kd, e)


def _gdn_scan_bwd(u, w, a, qd, kd, e, ssave, do):
    t = u.shape[1]
    tok, per, sq, shared, grid, nb = _scan_specs(t, DA, NA, SCAN_HB, -1)

    def body(u_ref, w_ref, a_ref, qd_ref, kd_ref, e_ref, s_ref, do_ref,
             du_ref, dw_ref, da_ref, dqd_ref, dkd_ref, de_ref, state):
        rev = pl.program_id(0) == 1

        @pl.when(pl.program_id(2) == 0)
        def _():
            state[...] = jnp.zeros_like(state)

        def step(i, dss):
            ci = jnp.where(rev, i, nb - 1 - i)
            r = _rows(ci)
            out = []
            for hh, ds in enumerate(dss):
                c = slice(hh * DA, (hh + 1) * DA)
                _, vjp = jax.vjp(_gdn_scan, u_ref[r, c], w_ref[r, c], a_ref[hh, r, :], qd_ref[r, c], kd_ref[r, c],
                                 e_ref[hh, ci][0:1], s_ref[hh, ci])
                du, dw, da, dqd, dkd, de, dsn = vjp((do_ref[r, c], ds))
                du_ref[r, c] = du
                dw_ref[r, c] = dw
                da_ref[hh, r, :] = da
                dqd_ref[r, c] = dqd
                dkd_ref[r, c] = dkd
                de_ref[hh, ci] = _first_row(de)
                out.append(dsn)
            return tuple(out)

        dss = lax.fori_loop(0, nb, step, tuple(state[hh] for hh in range(SCAN_HB)))
        for hh, ds in enumerate(dss):
            state[hh] = ds

    big = jax.ShapeDtypeStruct((2, t, D), F32)
    return pl.pallas_call(
        body, name="gdn_scan_bwd", grid=grid,
        in_specs=[tok, tok, sq, tok, tok, per(8, 128), per(DA, DA), shared(DA)],
        out_specs=[tok, tok, sq, tok, tok, per(8, 128)],
        out_shape=[big, big, jax.ShapeDtypeStruct((2, NA, t, CH), F32), big, big,
                   jax.ShapeDtypeStruct((2, NA, t // CH, 8, 128), F32)],
        scratch_shapes=[pltpu.VMEM((SCAN_HB, DA, DA), F32)],
        compiler_params=_cparams(("parallel", "parallel", "arbitrary")),
    )(u, w, a, qd, kd, e, ssave, do)


def _gdn_intra_bwd(qn, kn, vc, g2, b2, tinv, du, dw, da, dqd, dkd, de):
    t = qn.shape[0]
    n = t // CH

    def body(q_ref, k_ref, v_ref, g_ref, b_ref, t_ref, du_ref, dw_ref, da_ref, dqd_ref, dkd_ref, de_ref,
             dq_ref, dk_ref, dv_ref, dg_ref, db_ref):
        d = pl.program_id(1)
        cs = _Consts(d == 1)

        @pl.when(d == 0)
        def _():
            dq_ref[...] = jnp.zeros_like(dq_ref)
            dk_ref[...] = jnp.zeros_like(dk_ref)
            dv_ref[...] = jnp.zeros_like(dv_ref)

        def step(i, carry):
            r = _srows(i)
            tinv_c = t_ref[r, :]
            f = lambda q, k, v, g, bx: _gdn_intra(q, k, v, g, bx, tinv_c, cs)
            _, vjp = jax.vjp(f, q_ref[r, :], k_ref[r, :], v_ref[r, :], g_ref[r, :], b_ref[r, :])
            dq, dk, dv, dg, dbx = vjp((du_ref[r, :], dw_ref[r, :], da_ref[r, :], dqd_ref[r, :],
                                       dkd_ref[r, :], _chunk_rows(de_ref, i)))
            dq_ref[r, :] += dq
            dk_ref[r, :] += dk
            dv_ref[r, :] += dv
            dg_ref[r, :] = dg
            db_ref[r, :] = dbx
            return carry

        lax.fori_loop(0, t // SC, step, 0)

    head = pl.BlockSpec((t, DA), lambda h, d: (0, h))
    dh = pl.BlockSpec((None, t, DA), lambda h, d: (d, 0, h))
    sq = pl.BlockSpec((None, None, t, CH), lambda h, d: (d, h, 0, 0))
    tq = pl.BlockSpec((None, None, t, SC), lambda h, d: (d, h, 0, 0))
    full = jax.ShapeDtypeStruct((t, D), F32)
    big = jax.ShapeDtypeStruct((2, t, D), F32)
    return pl.pallas_call(
        body, name="gdn_intra_bwd", grid=(NA, 2),
        in_specs=[head, head, head, dh, dh, tq, dh, dh, sq, dh, dh,
                  pl.BlockSpec((None, None, n, 8, 128), lambda h, d: (d, h, 0, 0, 0))],
        out_specs=[head, head, head, dh, dh],
        out_shape=[full, full, full, big, big],
        compiler_params=_cparams(("arbitrary", "arbitrary")),
    )(qn, kn, vc, g2, b2, tinv, du, dw, da, dqd, dkd, de)


def _gla_specs(t, order):
    ix = (lambda d, h: (d, h)) if order == "dh" else (lambda h, d: (d, h))

    def mk(fn):
        return lambda a, b: fn(*ix(a, b))
    q = pl.BlockSpec((t, DKB), mk(lambda d, h: (0, 32 + h)))
    k = pl.BlockSpec((t, DKB), mk(lambda d, h: (0, 36 + h)))
    v = pl.BlockSpec((t, DVB), mk(lambda d, h: (0, 20 + h)))
    dk = pl.BlockSpec((None, t, DKB), mk(lambda d, h: (d, 0, h)))
    dv = pl.BlockSpec((None, t, DVB), mk(lambda d, h: (d, 0, h)))
    e = pl.BlockSpec((None, None, t // CH, 8, 128), mk(lambda d, h: (d, h, 0, 0, 0)))
    s = pl.BlockSpec((None, None, t // CH, DVB, DKB), mk(lambda d, h: (d, h, 0, 0, 0)))
    return q, k, v, dk, dv, e, s


def _gla_intra_fwd(p, gk):
    t = p.shape[0]
    n = t // CH

    def body(q_ref, k_ref, v_ref, g_ref, qg_ref, kd_ref, in_ref, e_ref):
        cs = _Consts(pl.program_id(0) == 1)

        def step(i, carry):
            r = _srows(i)
            qg, kd, intra, el = _gla_intra(q_ref[r, :], k_ref[r, :], v_ref[r, :], g_ref[r, :], cs)
            qg_ref[r, :] = qg
            kd_ref[r, :] = kd
            in_ref[r, :] = intra
            for c in range(SC // CH):
                e_ref[(SC // CH) * i + c] = el[c * CH:c * CH + 8]
            return carry

        lax.fori_loop(0, t // SC, step, 0)

    q, k, v, dk, dv, e, _ = _gla_specs(t, "dh")
    return pl.pallas_call(
        body, name="gla_intra_fwd", grid=(2, NB),
        in_specs=[q, k, v, dk], out_specs=[dk, dk, dv, e],
        out_shape=[jax.ShapeDtypeStruct((2, t, NB * DKB), F32), jax.ShapeDtypeStruct((2, t, NB * DKB), F32),
                   jax.ShapeDtypeStruct((2, t, D), F32), jax.ShapeDtypeStruct((2, NB, n, 8, 128), F32)],
        compiler_params=_cparams(("parallel", "parallel")),
    )(p, p, p, gk)


GLA_HB = 2


def _gla_v_spec(t, along):
    nt = t // SCAN_TB

    def tmap(d, tt):
        fwd = tt + d * (nt - 1 - 2 * tt)
        return fwd if along > 0 else nt - 1 - fwd

    return pl.BlockSpec((SCAN_TB, GLA_HB * DVB), lambda d, h, tt: (tmap(d, tt), 5120 // (GLA_HB * DVB) + h))


def _gla_scan_fwd(p, qg, kd, intra, e):
    t = p.shape[0]
    tokk, per, _, _, grid, nb = _scan_specs(t, DKB, NB, GLA_HB, +1)
    tokv = _scan_specs(t, DVB, NB, GLA_HB, +1)[0]

    def body(v_ref, qg_ref, kd_ref, in_ref, e_ref, o_ref, s_ref, state):
        rev = pl.program_id(0) == 1

        @pl.when(pl.program_id(2) == 0)
        def _():
            state[...] = jnp.zeros_like(state)

        def step(i, sts):
            ci = jnp.where(rev, nb - 1 - i, i)
            r = _rows(ci)
            out = []
            for hh, st in enumerate(sts):
                ck = slice(hh * DKB, (hh + 1) * DKB)
                cv = slice(hh * DVB, (hh + 1) * DVB)
                s_ref[hh, ci] = st
                o, stn = _gla_scan(qg_ref[r, ck], kd_ref[r, ck], v_ref[r, cv], e_ref[hh, ci][0:1], st)
                o_ref[r, cv] = o + in_ref[r, cv]
                out.append(stn)
            return tuple(out)

        sts = lax.fori_loop(0, nb, step, tuple(state[hh] for hh in range(GLA_HB)))
        for hh, st in enumerate(sts):
            state[hh] = st

    return pl.pallas_call(
        body, name="gla_scan_fwd", grid=grid,
        in_specs=[_gla_v_spec(t, +1), tokk, tokk, tokv, per(8, 128)], out_specs=[tokv, per(DVB, DKB)],
        out_shape=[jax.ShapeDtypeStruct((2, t, D), F32), jax.ShapeDtypeStruct((2, NB, t // CH, DVB, DKB), F32)],
        scratch_shapes=[pltpu.VMEM((GLA_HB, DVB, DKB), F32)],
        compiler_params=_cparams(("parallel", "parallel", "arbitrary")),
    )(p, qg, kd, intra, e)


def _gla_scan_bwd(p, qg, kd, e, ssave, do):
    t = p.shape[0]
    tokk, per, _, shared, grid, nb = _scan_specs(t, DKB, NB, GLA_HB, -1)
    tokv = _scan_specs(t, DVB, NB, GLA_HB, -1)[0]

    def body(v_ref, qg_ref, kd_ref, e_ref, s_ref, do_ref, dqg_ref, dkd_ref, dv_ref, de_ref, state):
        rev = pl.program_id(0) == 1

        @pl.when(pl.program_id(2) == 0)
        def _():
            state[...] = jnp.zeros_like(state)

        def step(i, dsts):
            ci = jnp.where(rev, i, nb - 1 - i)
            r = _rows(ci)
            out = []
            for hh, dst in enumerate(dsts):
                ck = slice(hh * DKB, (hh + 1) * DKB)
                cv = slice(hh * DVB, (hh + 1) * DVB)
                _, vjp = jax.vjp(_gla_scan, qg_ref[r, ck], kd_ref[r, ck], v_ref[r, cv], e_ref[hh, ci][0:1],
                                 s_ref[hh, ci])
                dqg, dkd, dv, de, dstn = vjp((do_ref[r, cv], dst))
                dqg_ref[r, ck] = dqg
                dkd_ref[r, ck] = dkd
                dv_ref[r, cv] = dv
                de_ref[hh, ci] = _first_row(de)
                out.append(dstn)
            return tuple(out)

        dsts = lax.fori_loop(0, nb, step, tuple(state[hh] for hh in range(GLA_HB)))
        for hh, dst in enumerate(dsts):
            state[hh] = dst

    return pl.pallas_call(
        body, name="gla_scan_bwd", grid=grid,
        in_specs=[_gla_v_spec(t, -1), tokk, tokk, per(8, 128), per(DVB, DKB), shared(DVB)],
        out_specs=[tokk, tokk, tokv, per(8, 128)],
        out_shape=[jax.ShapeDtypeStruct((2, t, NB * DKB), F32), jax.ShapeDtypeStruct((2, t, NB * DKB), F32),
                   jax.ShapeDtypeStruct((2, t, D), F32), jax.ShapeDtypeStruct((2, NB, t // CH, 8, 128), F32)],
        scratch_shapes=[pltpu.VMEM((GLA_HB, DVB, DKB), F32)],
        compiler_params=_cparams(("parallel", "parallel", "arbitrary")),
    )(p, qg, kd, e, ssave, do)


def _gla_intra_bwd(p, gk, dqg, dkd, dvs, de, do):
    t = p.shape[0]
    n = t // CH

    def body(q_ref, k_ref, v_ref, g_ref, dqg_ref, dkd_ref, dvs_ref, de_ref, do_ref,
             dq_ref, dk_ref, dv_ref, dg_ref):
        d = pl.program_id(1)
        cs = _Consts(d == 1)

        @pl.when(d == 0)
        def _():
            dq_ref[...] = jnp.zeros_like(dq_ref)
            dk_ref[...] = jnp.zeros_like(dk_ref)
            dv_ref[...] = jnp.zeros_like(dv_ref)

        def step(i, carry):
            r = _srows(i)
            f = lambda q, k, v, g: _gla_intra(q, k, v, g, cs)
            _, vjp = jax.vjp(f, q_ref[r, :], k_ref[r, :], v_ref[r, :], g_ref[r, :])
            dq, dk, dv, dg = vjp((dqg_ref[r, :], dkd_ref[r, :], do_ref[r, :], _chunk_rows(de_ref, i)))
            dq_ref[r, :] += dq
            dk_ref[r, :] += dk
            dv_ref[r, :] += dv + dvs_ref[r, :]
            dg_ref[r, :] = dg
            return carry

        lax.fori_loop(0, t // SC, step, 0)

    q, k, v, dk, dv, e_s, _ = _gla_specs(t, "hd")
    hk = pl.BlockSpec((t, DKB), lambda h, d: (0, h))
    hv = pl.BlockSpec((t, DVB), lambda h, d: (0, h))
    return pl.pallas_call(
        body, name="gla_intra_bwd", grid=(NB, 2),
        in_specs=[q, k, v, dk, dk, dk, dv, e_s, hv],
        out_specs=[hk, hk, hv, dk],
        out_shape=[jax.ShapeDtypeStruct((t, NB * DKB), F32), jax.ShapeDtypeStruct((t, NB * DKB), F32),
                   jax.ShapeDtypeStruct((t, D), F32), jax.ShapeDtypeStruct((2, t, NB * DKB), F32)],
        compiler_params=_cparams(("arbitrary", "arbitrary")),
    )(p, p, p, gk, dqg, dkd, dvs, de, do)


def _seg_gate(o, z, w):
    return _rms(o, w) * _silu(z)


def _seg_merge(ya, yb, ga, gb):
    return _sigmoid(ga) * ya + _sigmoid(gb) * yb


def _seg_loss(out, x, tgt, w):
    err = x + _rms(out, w) - tgt
    return 0.5 * jnp.sum(jnp.mean(err * err, axis=-1, keepdims=True), axis=0, keepdims=True)


def _post(oa2, ob2, p, x, tgt, gdn_w, gla_w, lnpost, w3, tm=128):
    t = x.shape[0]

    def body(oa_ref, ob_ref, z_ref, gb_ref, ga_ref, gB_ref, x_ref, t_ref, aw_ref, bw_ref, lw_ref, w_ref,
             loss_ref, doa_ref, dob_ref, dz_ref, dgb_ref, dga_ref, dgB_ref, dy_ref,
             dw_ref, daw_ref, dbw_ref, dlw_ref):
        first = pl.program_id(0) == 0
        oa = oa_ref[0] + oa_ref[1]
        ob = ob_ref[0] + ob_ref[1]
        z, gb = z_ref[...], gb_ref[...]
        aw, bw = aw_ref[...], bw_ref[...]
        rs = D // NSHARD

        def mat(a, m):
            return sum(jnp.dot(a[:, s * rs:(s + 1) * rs], w_ref[s, m], preferred_element_type=F32)
                       for s in range(NSHARD))

        def mat_t(g, m):
            return jnp.concatenate([_bdot(g, w_ref[s, m], 1, 1) for s in range(NSHARD)], axis=1)

        def add_dw(a, g, m):
            for s in range(NSHARD):
                dw_ref[s, m] += _bdot(a[:, s * rs:(s + 1) * rs], g, 0, 0)

        pa = [jax.vjp(_seg_gate, oa[:, h * DA:(h + 1) * DA], z[:, h * DA:(h + 1) * DA], aw) for h in range(NA)]
        pb = [jax.vjp(_seg_gate, ob[:, h * DVB:(h + 1) * DVB], gb[:, h * DVB:(h + 1) * DVB], bw)
              for h in range(NB)]
        a1 = jnp.concatenate([v for v, _ in pa], axis=1).astype(BF16)
        a2 = jnp.concatenate([v for v, _ in pb], axis=1).astype(BF16)
        ya = mat(a1, 0)
        yb = mat(a2, 1)
        merged, vjp_m = jax.vjp(_seg_merge, ya, yb, ga_ref[...], gB_ref[...])
        mb = merged.astype(BF16)
        out = mat(mb, 2)
        loss, vjp_l = jax.vjp(_seg_loss, out, x_ref[...], t_ref[...], lw_ref[...])
        dout, dyres, _, dlw = vjp_l(jnp.ones((1, 1), F32))
        dy_ref[...] = dyres
        doutb = dout.astype(BF16)
        dmerged = mat_t(doutb, 2)
        dya, dyb, dga, dgB = vjp_m(dmerged)
        dga_ref[...] = dga.astype(BF16)
        dgB_ref[...] = dgB.astype(BF16)
        dyab, dybb = dya.astype(BF16), dyb.astype(BF16)
        da1 = mat_t(dyab, 0)
        da2 = mat_t(dybb, 1)

        daw = jnp.zeros_like(aw)
        for h in range(NA):
            sl = slice(h * DA, (h + 1) * DA)
            do, dz, dw = pa[h][1](da1[:, sl])
            doa_ref[:, sl] = do
            dz_ref[:, sl] = dz.astype(BF16)
            daw = daw + dw
        dbw = jnp.zeros_like(bw)
        for h in range(NB):
            sl = slice(h * DVB, (h + 1) * DVB)
            do, dg, dw = pb[h][1](da2[:, sl])
            dob_ref[:, sl] = do
            dgb_ref[:, sl] = dg.astype(BF16)
            dbw = dbw + dw

        @pl.when(first)
        def _():
            loss_ref[...] = jnp.zeros_like(loss_ref)
            dw_ref[...] = jnp.zeros_like(dw_ref)
            daw_ref[...] = jnp.zeros_like(daw_ref)
            dbw_ref[...] = jnp.zeros_like(dbw_ref)
            dlw_ref[...] = jnp.zeros_like(dlw_ref)

        loss_ref[...] += jnp.broadcast_to(loss, loss_ref.shape)
        add_dw(a1, dyab, 0)
        add_dw(a2, dybb, 1)
        add_dw(mb, doutb, 2)
        daw_ref[...] += jnp.broadcast_to(daw, daw_ref.shape)
        dbw_ref[...] += jnp.broadcast_to(dbw, dbw_ref.shape)
        dlw_ref[...] += jnp.broadcast_to(dlw, dlw_ref.shape)

    two = pl.BlockSpec((2, tm, D), lambda i: (0, i, 0))
    pcol = lambda c: pl.BlockSpec((tm, D), lambda i: (i, c))
    tok = pl.BlockSpec((tm, D), lambda i: (i, 0))
    row = lambda n: pl.BlockSpec((1, n), lambda i: (0, 0))
    row8 = lambda n: pl.BlockSpec((8, n), lambda i: (0, 0))
    once = pl.Buffered(1)
    tokf = jax.ShapeDtypeStruct((t, D), F32)
    tokb = jax.ShapeDtypeStruct((t, D), BF16)
    wspec = pl.BlockSpec((NSHARD, 3, D // NSHARD, D), lambda i: (0, 0, 0, 0), pipeline_mode=once)
    return pl.pallas_call(
        body, name="post", grid=(t // tm,),
        in_specs=[two, two, pcol(3), pcol(6), pcol(7), pcol(8), tok, tok, row(DA), row(DVB), row(D), wspec],
        out_specs=[row8(128), tok, tok, tok, tok, tok, tok, tok, wspec, row8(DA), row8(DVB), row8(D)],
        out_shape=[jax.ShapeDtypeStruct((8, 128), F32), tokf, tokf, tokb, tokb, tokb, tokb, tokf,
                   jax.ShapeDtypeStruct((NSHARD, 3, D // NSHARD, D), F32),
                   jax.ShapeDtypeStruct((8, DA), F32), jax.ShapeDtypeStruct((8, DVB), F32),
                   jax.ShapeDtypeStruct((8, D), F32)],
        compiler_params=_cparams(("arbitrary",), vmem_mb=56),
    )(oa2, ob2, p, p, p, p, x, tgt, gdn_w, gla_w, lnpost, w3)


def _adam_math(w, g, m, v):
    nm = B1 * m + (1.0 - B1) * g
    nv = B2 * v + (1.0 - B2) * (g * g)
    m_hat = nm / (1.0 - B1 ** STEP)
    v_hat = nv / (1.0 - B2 ** STEP)
    return -LR * (m_hat / (jnp.sqrt(v_hat) + ADAM_EPS) + WD * w), nm, nv


SMALL_SLOTS = (("ln_pre_w", 0, 1024, 0), ("a_log_fwd", 1024, 8, 0), ("a_log_bwd", 1024, 8, 8),
               ("dt_bias_fwd", 1152, 8, 0), ("dt_bias_bwd", 1152, 8, 8), ("gdn_norm_w", 1280, 128, 0),
               ("gk_b2_fwd", 1408, 512, 0), ("gk_b2_bwd", 1920, 512, 0), ("gla_norm_w", 2432, 256, 0),
               ("ln_post_w", 2688, 1024, 0))
SMALL_W = 3712


def _adam_small(gsum, ws, ms, vs):
    nw = len(SMALL_SLOTS)

    def body(g_ref, *refs):
        w_refs, m_refs, v_refs, outs = refs[0:nw], refs[nw:2 * nw], refs[2 * nw:3 * nw], refs[3 * nw:]
        for i, (_, off, n, shift) in enumerate(SMALL_SLOTS):
            slot = g_ref[0:1, off:off + max(n, 128)]
            if shift:
                slot = pltpu.roll(slot, 128 - shift, 1)
            g = slot[:, 0:n]
            d, nm, nv = _adam_math(w_refs[i][...], g, m_refs[i][...], v_refs[i][...])
            for k, val in enumerate((g, d, nm, nv)):
                outs[4 * i + k][...] = val

    vm = pl.BlockSpec(memory_space=pltpu.VMEM)
    res = pl.pallas_call(
        body, name="adam_small", in_specs=[vm] * (1 + 3 * nw), out_specs=[vm] * (4 * nw),
        out_shape=[jax.ShapeDtypeStruct((1, n), F32) for _, _, n, _ in SMALL_SLOTS for _ in range(4)],
    )(gsum, *ws, *ms, *vs)
    return {name: res[4 * i:4 * i + 4] for i, (name, _, _, _) in enumerate(SMALL_SLOTS)}


def _adam(w, mine, got, m, v, tr, tile0=0, name=""):
    rows, cols = w.shape
    nh = mine.shape[0] // tr

    def body(c_ref, w_ref, a_ref, b_ref, m_ref, v_ref, g_ref, d_ref, nm_ref, nv_ref):
        half = (tile0 + pl.program_id(0)) // nh
        g = jnp.where(half == c_ref[0], a_ref[...], b_ref[...])
        d, nm, nv = _adam_math(w_ref[...], g, m_ref[...], v_ref[...])
        g_ref[...] = g
        d_ref[...] = d
        nm_ref[...] = nm
        nv_ref[...] = nv

    blk = pl.BlockSpec((tr, cols), lambda i, cc: (i, 0))
    half = pl.BlockSpec((tr, cols), lambda i, cc: ((tile0 + i) % nh, 0))
    shp = jax.ShapeDtypeStruct((rows, cols), F32)
    return pl.pallas_call(
        body, name=f"adam_{name}{rows}x{cols}",
        grid_spec=pltpu.PrefetchScalarGridSpec(
            num_scalar_prefetch=1, grid=(rows // tr,),
            in_specs=[blk, half, half, blk, blk], out_specs=[blk] * 4),
        out_shape=[shp] * 4,
        compiler_params=_cparams(("parallel",)),
    )(lax.axis_index("c").reshape(1), w, mine, got, m, v)


def _sum_cast(own, got):
    ns, _, r, c = own.shape
    tr = r // 4 if r % 64 == 0 else r

    def body(c_ref, a_ref, b_ref, f_ref, h_ref):
        s = a_ref[...] + b_ref[...]
        f_ref[...] = s
        h_ref[...] = s.astype(BF16)

    return pl.pallas_call(
        body, name=f"sum_cast_{r}x{c}",
        grid_spec=pltpu.PrefetchScalarGridSpec(
            num_scalar_prefetch=1, grid=(ns, r // tr),
            in_specs=[pl.BlockSpec((None, None, tr, c), lambda s, i, cc: (s, cc[0], i, 0)),
                      pl.BlockSpec((None, tr, c), lambda s, i, cc: (s, i, 0))],
            out_specs=[pl.BlockSpec((None, tr, c), lambda s, i, cc: (s, i, 0)),
                       pl.BlockSpec((None, tr, c), lambda s, i, cc: (s, i, 0))]),
        out_shape=[jax.ShapeDtypeStruct((ns, r, c), F32), jax.ShapeDtypeStruct((ns, r, c), BF16)],
        compiler_params=_cparams(("parallel", "parallel")),
    )(lax.axis_index("c").reshape(1), own, got)


def _sum4(mine, got):
    _, r, c = mine.shape
    tr = r // 4 if r % 64 == 0 else r

    def body(s_ref, a_ref, g_ref, o_ref):
        acc = a_ref[...] + g_ref[0].astype(F32)
        acc = acc + g_ref[1].astype(F32)
        o_ref[...] = acc + g_ref[2].astype(F32)

    shard = (2 * lax.axis_index("x") + lax.axis_index("y")).reshape(1)
    return pl.pallas_call(
        body, name=f"sum4_{r}x{c}",
        grid_spec=pltpu.PrefetchScalarGridSpec(
            num_scalar_prefetch=1, grid=(r // tr,),
            in_specs=[pl.BlockSpec((None, tr, c), lambda i, ss: (ss[0], i, 0)),
                      pl.BlockSpec((3, tr, c), lambda i, ss: (0, i, 0))],
            out_specs=pl.BlockSpec((tr, c), lambda i, ss: (i, 0))),
        out_shape=jax.ShapeDtypeStruct((r, c), F32),
        compiler_params=_cparams(("parallel",)),
    )(shard, mine, got)


def _place():
    x, y, c = lax.axis_index("x"), lax.axis_index("y"), lax.axis_index("c")
    chips = [(1 - x, y), (x, 1 - y), (1 - x, 1 - y)]
    return x, y, c, chips


def _gather_weights(parts):
    npart = len(parts)

    def body(*refs):
        ins, outs = refs[:npart], refs[npart:2 * npart]
        send_sems, recv_sems = refs[2 * npart:]
        x, y, c, chips = _place()
        sibling = (x, y, 1 - c)
        mine = 2 * x + y

        def remote(k, p, shard, half, to, src=None):
            dst = outs[p].at[shard, half]
            return pltpu.make_async_remote_copy(
                src_ref=dst if src is None else src, dst_ref=dst,
                send_sem=send_sems.at[k], recv_sem=recv_sems.at[k], device_id=to, device_id_type=MESH)

        first = [remote(j * npart + p, p, mine, c, (*chip, c), src=ins[p].at[c])
                 for j, chip in enumerate(chips) for p in range(npart)]
        for cp in first:
            cp.start()
        passed = []
        for j, (cx, cy) in enumerate(chips):
            for p in range(npart):
                remote(j * npart + p, p, 2 * cx + cy, c, (x, y, c)).wait_recv()
                fw = remote((3 + j) * npart + p, p, 2 * cx + cy, c, sibling)
                fw.start()
                passed.append(fw)
        for j, (cx, cy) in enumerate(chips):
            for p in range(npart):
                remote((3 + j) * npart + p, p, 2 * cx + cy, 1 - c, (x, y, c)).wait_recv()
        for cp in first + passed:
            cp.wait_send()

    got = pl.pallas_call(
        body, name="gather_weights",
        in_specs=[ANY] * npart, out_specs=[ANY] * npart,
        out_shape=[jax.ShapeDtypeStruct((NSHARD,) + a.shape, a.dtype) for a in parts],
        scratch_shapes=[pltpu.SemaphoreType.DMA((6 * npart,)), pltpu.SemaphoreType.DMA((6 * npart,))],
    )(*parts)
    mine = 2 * lax.axis_index("x") + lax.axis_index("y")
    return [lax.dynamic_update_index_in_dim(g, a, mine, 0) for g, a in zip(got, parts)]


def _swap_halves(parts):
    npart = len(parts)

    def body(*refs):
        ins, outs = refs[:npart], refs[npart:2 * npart]
        send_sems, recv_sems = refs[2 * npart:]
        x, y, c, _ = _place()
        cps = [pltpu.make_async_remote_copy(
            src_ref=ins[p].at[s, 1 - c], dst_ref=outs[p].at[s],
            send_sem=send_sems.at[s * npart + p], recv_sem=recv_sems.at[s * npart + p],
            device_id=(x, y, 1 - c), device_id_type=MESH) for s in range(NSHARD) for p in range(npart)]
        for cp in cps:
            cp.start()
        for cp in cps:
            cp.wait()

    return pl.pallas_call(
        body, name="swap_halves", in_specs=[ANY] * npart, out_specs=[ANY] * npart,
        out_shape=[jax.ShapeDtypeStruct((NSHARD,) + a.shape[2:], a.dtype) for a in parts],
        scratch_shapes=[pltpu.SemaphoreType.DMA((NSHARD * npart,)), pltpu.SemaphoreType.DMA((NSHARD * npart,))],
    )(*parts)


def _scatter_shards(parts):
    npart = len(parts)

    def body(*refs):
        ins, outs = refs[:npart], refs[npart:2 * npart]
        send_sems, recv_sems = refs[2 * npart:]
        x, y, c, chips = _place()
        cps = [pltpu.make_async_remote_copy(
            src_ref=ins[p].at[2 * cx + cy], dst_ref=outs[p].at[j],
            send_sem=send_sems.at[j * npart + p], recv_sem=recv_sems.at[j * npart + p],
            device_id=(cx, cy, c), device_id_type=MESH)
            for j, (cx, cy) in enumerate(chips) for p in range(npart)]
        for cp in cps:
            cp.start()
        for cp in cps:
            cp.wait()

    return pl.pallas_call(
        body, name="scatter_shards", in_specs=[ANY] * npart, out_specs=[ANY] * npart,
        out_shape=[jax.ShapeDtypeStruct((3,) + a.shape[1:], a.dtype) for a in parts],
        scratch_shapes=[pltpu.SemaphoreType.DMA((3 * npart,)), pltpu.SemaphoreType.DMA((3 * npart,))],
    )(*parts)


HBM = pl.BlockSpec(memory_space=pltpu.HBM)
SEM = pl.BlockSpec(memory_space=pltpu.SEMAPHORE)
EFFECT = pltpu.SideEffectType.DATAFLOW_SIDE_EFFECTING


def _scatter_copies(srcs, lands, send_sems, recv_sems, waiting):
    x, y, c, chips = _place()
    n = len(srcs)
    return [pltpu.make_async_remote_copy(
        src_ref=srcs[p].at[2 * cx + cy], dst_ref=lands[p].at[j],
        send_sem=send_sems.at[j * n + p], recv_sem=recv_sems.at[j * n + p],
        device_id=(cx, cy, c), device_id_type=MESH)
        for j, (cx, cy) in enumerate(chips) for p in range(n)]


def _proj_copies(srcs, lands, send_sems, recv_sems, waiting):
    x, y, c, chips = _place()
    mine = 2 * x + y
    return [pltpu.make_async_remote_copy(
        src_ref=srcs[0].at[c], dst_ref=lands[0].at[mine, c],
        send_sem=send_sems.at[2 * j + to], recv_sem=recv_sems.at[2 * j + (to if waiting else c)],
        device_id=(cx, cy, to), device_id_type=MESH)
        for j, (cx, cy) in enumerate(chips) for to in range(2)]


def _start_copies(copies, nsem, parts, lands, name, after=None):
    n = len(parts)
    extra = [] if after is None else [after]

    def body(*refs):
        outs = refs[2 * n + len(extra):]
        for cp in copies(refs[:n], refs[n:2 * n], outs[0], outs[1], False):
            cp.start()
        outs[-1][...] = jnp.zeros_like(outs[-1])

    res = pl.pallas_call(
        body, name=name,
        out_shape=(pltpu.SemaphoreType.DMA((nsem,)), pltpu.SemaphoreType.DMA((nsem,)),
                   *[pltpu.HBM(a.shape, a.dtype) for a in parts], *[pltpu.HBM(a.shape, a.dtype) for a in lands],
                   jax.ShapeDtypeStruct((8, 128), F32)),
        in_specs=[HBM] * (2 * n) + [ANY] * len(extra),
        out_specs=(SEM, SEM, *[HBM] * (2 * n), pl.BlockSpec(memory_space=pltpu.VMEM)),
        input_output_aliases={i: 2 + i for i in range(2 * n)},
        compiler_params=pltpu.CompilerParams(has_side_effects=EFFECT),
    )(*[pltpu.with_memory_space_constraint(a, pltpu.HBM) for a in parts],
      *[pltpu.with_memory_space_constraint(a, pltpu.HBM) for a in lands], *extra)
    return res[0], res[1], res[2:2 + n], res[2 + n:2 + 2 * n], res[-1]


def _wait_copies(copies, started, after, name):
    send_sems, recv_sems, srcs, lands, _ = started
    n = len(srcs)

    def body(*refs):
        for cp in copies(refs[:n], refs[n:2 * n], refs[2 * n], refs[2 * n + 1], True):
            cp.wait_send()
            cp.wait_recv()

    res = pl.pallas_call(
        body, name=name,
        out_shape=tuple(pltpu.HBM(a.shape, a.dtype) for a in (*srcs, *lands)),
        in_specs=[HBM] * (2 * n) + [SEM, SEM, ANY], out_specs=tuple([HBM] * (2 * n)),
        input_output_aliases={i: i for i in range(2 * n)},
        compiler_params=pltpu.CompilerParams(has_side_effects=EFFECT),
    )(*srcs, *lands, send_sems, recv_sems, after)
    return res[n:]


def _join_halves(parts):
    npart = len(parts)

    def body(*refs):
        ins, outs = refs[:npart], refs[npart:2 * npart]
        send_sems, recv_sems = refs[2 * npart:]
        x, y, c, _ = _place()
        cps = [pltpu.make_async_remote_copy(
            src_ref=ins[p], dst_ref=outs[p], send_sem=send_sems.at[p], recv_sem=recv_sems.at[p],
            device_id=(x, y, 1 - c), device_id_type=MESH) for p in range(npart)]
        for cp in cps:
            cp.start()
        for cp in cps:
            cp.wait()

    return pl.pallas_call(
        body, name="join_halves", in_specs=[ANY] * npart, out_specs=[ANY] * npart,
        out_shape=[jax.ShapeDtypeStruct(a.shape, a.dtype) for a in parts],
        scratch_shapes=[pltpu.SemaphoreType.DMA((npart,)), pltpu.SemaphoreType.DMA((npart,))],
    )(*parts)


def _allreduce_small(v):
    r, ncol = v.shape

    def body(v_ref, o_ref, buf, send_sems, recv_sems):
        x, y, c, _ = _place()
        me = 4 * x + 2 * y + c
        buf[me] = v_ref[...]
        cps = []
        for k in range(1, 8):
            px, py, pc = x ^ (k >> 2), y ^ ((k >> 1) & 1), c ^ (k & 1)
            cps.append(pltpu.make_async_remote_copy(
                src_ref=v_ref, dst_ref=buf.at[me], send_sem=send_sems.at[k - 1], recv_sem=recv_sems.at[k - 1],
                device_id=(px, py, pc), device_id_type=MESH))
        for cp in cps:
            cp.start()
        for k in range(1, 8):
            px, py, pc = x ^ (k >> 2), y ^ ((k >> 1) & 1), c ^ (k & 1)
            pltpu.make_async_remote_copy(
                src_ref=v_ref, dst_ref=buf.at[4 * px + 2 * py + pc], send_sem=send_sems.at[k - 1],
                recv_sem=recv_sems.at[k - 1], device_id=(px, py, pc), device_id_type=MESH).wait_recv()
        for cp in cps:
            cp.wait_send()
        acc = buf[0]
        for d in range(1, 8):
            acc = acc + buf[d]
        o_ref[...] = acc

    return pl.pallas_call(
        body, name="allreduce_small",
        in_specs=[pl.BlockSpec(memory_space=pltpu.VMEM)], out_specs=pl.BlockSpec(memory_space=pltpu.VMEM),
        out_shape=jax.ShapeDtypeStruct((r, ncol), F32),
        scratch_shapes=[pltpu.VMEM((8, r, ncol), F32), pltpu.SemaphoreType.DMA((7,)), pltpu.SemaphoreType.DMA((7,))],
    )(v)


def _permute_rows(shards):
    w0, w1, w2, w3 = shards
    zeros = jnp.zeros((NPERM - 9280, w0.shape[1]), w0.dtype)
    return jnp.concatenate([w0, w1[0:1776], w1[1808:2320], w2, w3[0:240], w3[272:2320],
                            w1[1776:1808], w3[240:272], zeros], axis=0)


def _unpermute_rows(g):
    s1 = jnp.concatenate([g[2320:4096], g[9216:9248], g[4096:4608]], axis=0)
    s3 = jnp.concatenate([g[6928:7168], g[9248:9280], g[7168:9216]], axis=0)
    return jnp.stack([g[0:2320], s1, g[4608:6928], s3], axis=0)


def _pack_shard_small(conv, w2f, w2b):
    top = jnp.pad(conv, ((0, 8 - conv.shape[0]), (0, 0)))
    mid = jnp.pad(jnp.concatenate([w2f, w2b], axis=1), ((0, 0), (0, 768 - 256)))
    return jnp.concatenate([top, mid, jnp.zeros((8, 768), conv.dtype)], axis=0)


def _unpack_shard_small(a):
    return a[0:5], a[8:24, 0:128], a[8:24, 128:256]


def kernel(x, ln_pre_w, w_in, conv_w, a_log_fwd, a_log_bwd, dt_bias_fwd, dt_bias_bwd, gdn_norm_w, w_proj_gdn, gk_w2_fwd, gk_b2_fwd, gk_w2_bwd, gk_b2_bwd, gla_norm_w, w_proj_gla, w_out, ln_post_w, loss_target, m_ln_pre_w, m_w_in, m_conv_w, m_a_log_fwd, m_a_log_bwd, m_dt_bias_fwd, m_dt_bias_bwd, m_gdn_norm_w, m_w_proj_gdn, m_gk_w2_fwd, m_gk_b2_fwd, m_gk_w2_bwd, m_gk_b2_bwd, m_gla_norm_w, m_w_proj_gla, m_w_out, m_ln_post_w, v_ln_pre_w, v_w_in, v_conv_w, v_a_log_fwd, v_a_log_bwd, v_dt_bias_fwd, v_dt_bias_bwd, v_gdn_norm_w, v_w_proj_gdn, v_gk_w2_fwd, v_gk_b2_fwd, v_gk_w2_bwd, v_gk_b2_bwd, v_gla_norm_w, v_w_proj_gla, v_w_out, v_ln_post_w):
    t = x.shape[1]
    x2, tgt = x[0], loss_target[0]

    win_l = w_in[0].T.astype(BF16).reshape(2, SHW // 2, D)
    proj_l = jnp.concatenate([w_proj_gdn[0], w_proj_gla[0], w_out[0]], axis=0).astype(BF16).reshape(2, 384, D)
    small_l = _pack_shard_small(conv_w[0], gk_w2_fwd[0], gk_w2_bwd[0]).reshape(2, 16, 768)
    win_g, small_g = _gather_weights([win_l, small_l])
    proj_started = _start_copies(_proj_copies, 6, [proj_l], [lax.empty((NSHARD, 2, 384, D), BF16)],
                                 "gather_proj_start", after=small_g)
    wperm = _permute_rows(win_g.reshape(NSHARD, SHW, D))
    small_g = small_g.reshape(NSHARD, 32, 768)
    convw = small_g[:, 0:8, :].transpose(1, 0, 2).reshape(8, 3 * D)
    w2f = small_g[:, 8:24, 0:128].transpose(1, 0, 2).reshape(16, 512)
    w2b = small_g[:, 8:24, 128:256].transpose(1, 0, 2).reshape(16, 512)
    w2f_pad = jnp.pad(w2f, ((32, 80), (0, 0)))
    w2b_pad = jnp.pad(w2b, ((48, 64), (0, 0)))
    alog_row = jnp.pad(jnp.concatenate([a_log_fwd, a_log_bwd], axis=1), ((0, 0), (0, 112)))
    dt_row = jnp.pad(jnp.concatenate([dt_bias_fwd, dt_bias_bwd], axis=1), ((0, 0), (0, 112)))

    p, h = _inproj(x2, ln_pre_w + proj_started[4][0:1, 0:1], wperm)
    qn, kn, vc = (_qkv_fwd(p, convw, kind) for kind in range(3))
    gsm, gk = _gates_fwd(p, alog_row, dt_row, w2f_pad, gk_b2_fwd, w2b_pad, gk_b2_bwd)
    g2, b2 = _gcum_fwd(gsm)
    u, w, at, qd, kd, el, tinv = _gdn_intra_fwd(qn, kn, vc, g2, b2)
    oa2, sa = _gdn_scan_fwd(u, w, at, qd, kd, el)
    qg, kdb, intra, elb = _gla_intra_fwd(p, gk)
    ob2, sb = _gla_scan_fwd(p, qg, kdb, intra, elb)

    (proj_land,) = _wait_copies(_proj_copies, proj_started, ob2, "gather_proj_wait")
    mine = 2 * lax.axis_index("x") + lax.axis_index("y")
    w3 = lax.dynamic_update_index_in_dim(proj_land, proj_l, mine, 0).reshape(NSHARD, 3, D // NSHARD, D)
    (loss8, doa, dob, dz, dgb, dga, dgB, dyres, dw3, dgdn_w, dgla_w, dlnpost) = _post(
        oa2, ob2, p, x2, tgt, gdn_norm_w, gla_norm_w, ln_post_w, w3)

    du, dw, dat, dqd, dkd, del_ = _gdn_scan_bwd(u, w, at, qd, kd, el, sa, doa)
    dqn, dkn, dvc, dg2, db2 = _gdn_intra_bwd(qn, kn, vc, g2, b2, tinv, du, dw, dat, dqd, dkd, del_)
    dgsm = _gcum_bwd(gsm, dg2, db2)
    dqg, dkdb, dvs, delb = _gla_scan_bwd(p, qg, kdb, elb, sb, dob)
    dqb, dkb, dvb, dgk = _gla_intra_bwd(p, gk, dqg, dkdb, dvs, delb, dob)
    (dps, dalog8, ddt8, dw2f_pad, db2f8, dw2b_pad, db2b8) = _gates_bwd(
        p, alog_row, dt_row, w2f_pad, gk_b2_fwd, w2b_pad, gk_b2_bwd, dgsm, dgk)
    dpre, dconv = zip(*[_qkv_bwd(p, convw, g, kind) for kind, g in enumerate((dqn, dkn, dvc))])

    pieces = (jnp.concatenate([a.astype(BF16) for a in (*dpre, dz, dqb, dkb, dvb, dgb, dga, dgB, dps)], axis=1),)
    dwperm = _inproj_dw(h, pieces)

    g_in = _unpermute_rows(dwperm).reshape(NSHARD, 2, SHW // 2, D)
    g_proj = dw3.reshape(NSHARD, 2, 384, D)
    dconv_full = jnp.concatenate(dconv, axis=1)
    dw2f, dw2b = dw2f_pad[32:48], dw2b_pad[48:64]
    g_small = jnp.stack([_pack_shard_small(dconv_full[0:5, 768 * s:768 * (s + 1)],
                                           dw2f[:, 128 * s:128 * (s + 1)], dw2b[:, 128 * s:128 * (s + 1)])
                         for s in range(NSHARD)])
    g_small = g_small.reshape(NSHARD, 2, 16, 768)
    parts = [g_in, g_proj, g_small]
    got = _swap_halves(parts)
    sums = [_sum_cast(a, b) for a, b in zip(parts, got)]
    hbs = [hb for _, hb in sums]
    started = _start_copies(_scatter_copies, 3 * len(hbs), hbs,
                            [lax.empty((3,) + a.shape[1:], a.dtype) for a in hbs], "scatter_start")
    dx, dlnpre8 = _inproj_dx(pieces, wperm, x2, ln_pre_w + started[4][0:1, 0:1], dyres)
    landed = _wait_copies(_scatter_copies, started, dx, "scatter_wait")
    halves = [_sum4(f, g) for (f, _), g in zip(sums, landed)]
    theirs = _join_halves(halves)

    gsmall = _allreduce_small(jnp.concatenate(
        [dlnpre8, dalog8, ddt8, dgdn_w, db2f8, db2b8, dgla_w, dlnpost], axis=1))

    a_in = [a.T for a in _adam(w_in[0].T, halves[0], theirs[0], m_w_in[0].T, v_w_in[0].T, 232, name="in")]
    a_pr = [_adam(w[0], halves[1], theirs[1], m[0], v[0], 128, tile0=2 * i, name=f"proj{i}")
            for i, (w, m, v) in enumerate(((w_proj_gdn, m_w_proj_gdn, v_w_proj_gdn),
                                           (w_proj_gla, m_w_proj_gla, v_w_proj_gla), (w_out, m_w_out, v_w_out)))]
    a_ss = _adam(_pack_shard_small(conv_w[0], gk_w2_fwd[0], gk_w2_bwd[0]), halves[2], theirs[2],
                 _pack_shard_small(m_conv_w[0], m_gk_w2_fwd[0], m_gk_w2_bwd[0]),
                 _pack_shard_small(v_conv_w[0], v_gk_w2_fwd[0], v_gk_w2_bwd[0]), 16, name="small")
    smalls = dict(ln_pre_w=(ln_pre_w, m_ln_pre_w, v_ln_pre_w), a_log_fwd=(a_log_fwd, m_a_log_fwd, v_a_log_fwd),
                  a_log_bwd=(a_log_bwd, m_a_log_bwd, v_a_log_bwd),
                  dt_bias_fwd=(dt_bias_fwd, m_dt_bias_fwd, v_dt_bias_fwd),
                  dt_bias_bwd=(dt_bias_bwd, m_dt_bias_bwd, v_dt_bias_bwd),
                  gdn_norm_w=(gdn_norm_w, m_gdn_norm_w, v_gdn_norm_w),
                  gk_b2_fwd=(gk_b2_fwd, m_gk_b2_fwd, v_gk_b2_fwd), gk_b2_bwd=(gk_b2_bwd, m_gk_b2_bwd, v_gk_b2_bwd),
                  gla_norm_w=(gla_norm_w, m_gla_norm_w, v_gla_norm_w), ln_post_w=(ln_post_w, m_ln_post_w, v_ln_post_w))
    names = [name for name, _, _, _ in SMALL_SLOTS]
    small = _adam_small(gsmall, *([smalls[n][i] for n in names] for i in range(3)))

    def family(k):
        conv, w2f_, w2b_ = _unpack_shard_small(a_ss[k])
        s = {n: small[n][k] for n in names}
        return [s["ln_pre_w"], a_in[k][None], conv[None], s["a_log_fwd"], s["a_log_bwd"], s["dt_bias_fwd"],
                s["dt_bias_bwd"], s["gdn_norm_w"], a_pr[0][k][None], w2f_[None], s["gk_b2_fwd"], w2b_[None],
                s["gk_b2_bwd"], s["gla_norm_w"], a_pr[1][k][None], a_pr[2][k][None], s["ln_post_w"]]

    loss = lax.psum(loss8[0, 0], ("x", "y", "c"))
    return (loss, dx[None], *family(0), *family(1), *family(2), *family(3))
```

```python
import functools

import jax
import jax.numpy as jnp
from jax import lax
from jax.experimental import pallas as pl
from jax.experimental.pallas import tpu as pltpu

F32 = jnp.float32
BF16 = jnp.bfloat16
HI = lax.Precision.HIGHEST
MESH = pl.DeviceIdType.MESH

D = 1024
CH = 64
EPS = 1e-6
NA, DA = 8, 128
NB, DKB, DVB = 4, 128, 256
NSHARD = 4
SHW = 2320
NPERM = 9728
PS_BLOCK = 72
LR, B1, B2, ADAM_EPS, WD, STEP = 0.001, 0.9, 0.999, 1e-08, 0.01, 10

ANY = pl.BlockSpec(memory_space=pl.ANY)


def _cparams(sem=None, vmem_mb=48):
    return pltpu.CompilerParams(dimension_semantics=sem, vmem_limit_bytes=vmem_mb << 20)


def _bdot(a, b, ca, cb):
    return lax.dot_general(a.astype(BF16), b.astype(BF16), (((ca,), (cb,)), ((), ())),
                           preferred_element_type=F32)


@jax.custom_vjp
def mm(a, b):
    return _bdot(a, b, 1, 0)


def _mm_fwd(a, b):
    return _bdot(a, b, 1, 0), (a, b)


def _mm_bwd(res, g):
    a, b = res
    return _bdot(g, b, 1, 1), _bdot(a, g, 0, 0)


mm.defvjp(_mm_fwd, _mm_bwd)


@jax.custom_vjp
def mm_nt(a, b):
    return _bdot(a, b, 1, 1)


def _mm_nt_fwd(a, b):
    return _bdot(a, b, 1, 1), (a, b)


def _mm_nt_bwd(res, g):
    a, b = res
    return _bdot(g, b, 1, 0), _bdot(g, a, 0, 0)


mm_nt.defvjp(_mm_nt_fwd, _mm_nt_bwd)


@jax.custom_vjp
def mm_tn(a, b):
    return _bdot(a, b, 0, 0)


def _mm_tn_fwd(a, b):
    return _bdot(a, b, 0, 0), (a, b)


def _mm_tn_bwd(res, g):
    a, b = res
    return _bdot(b, g, 1, 1), _bdot(a, g, 1, 0)


mm_tn.defvjp(_mm_tn_fwd, _mm_tn_bwd)


def dot_hi(a, b):
    return lax.dot_general(a, b, (((1,), (0,)), ((), ())), precision=HI, preferred_element_type=F32)


def _split3(x):
    x1 = x.astype(BF16)
    r = x - x1.astype(F32)
    x2 = r.astype(BF16)
    return x1, x2, (r - x2.astype(F32)).astype(BF16)


def _cdot(c, x, cc, cx, c_first=True):
    parts = _split3(x)
    if c_first:
        return _bdot(c, parts[0], cc, cx) + _bdot(c, parts[1], cc, cx) + _bdot(c, parts[2], cc, cx)
    return _bdot(parts[0], c, cx, cc) + _bdot(parts[1], c, cx, cc) + _bdot(parts[2], c, cx, cc)


@jax.custom_vjp
def cmm(c, x):
    return _cdot(c, x, 1, 0)


def _cmm_fwd(c, x):
    return _cdot(c, x, 1, 0), c


def _cmm_bwd(c, g):
    return jnp.zeros_like(c), _cdot(c, g, 0, 0)


cmm.defvjp(_cmm_fwd, _cmm_bwd)


@jax.custom_vjp
def mmc(x, c):
    return _cdot(c, x, 0, 1, c_first=False)


def _mmc_fwd(x, c):
    return _cdot(c, x, 0, 1, c_first=False), c


def _mmc_bwd(c, g):
    return _cdot(c, g, 1, 1, c_first=False), jnp.zeros_like(c)


mmc.defvjp(_mmc_fwd, _mmc_bwd)


def _sigmoid(x):
    return 1.0 / (1.0 + jnp.exp(-x))


def _silu(x):
    return x * _sigmoid(x)


def _softplus(x):
    return jnp.maximum(x, 0.0) + jnp.log(1.0 + jnp.exp(-jnp.abs(x)))


def _rms(x, w):
    return x * lax.rsqrt(jnp.mean(x * x, axis=-1, keepdims=True) + EPS) * w


SC = 256


class _Consts:
    def __init__(self, rev):
        r = lax.broadcasted_iota(jnp.int32, (SC, SC), 0)
        c = lax.broadcasted_iota(jnp.int32, (SC, SC), 1)
        same = (r >> 6) == (c >> 6)
        a = jnp.where(rev, c, r)
        b = jnp.where(rev, r, c)
        self.incl = same & (a >= b)
        self.strict = same & (a > b)
        self.incl_f = self.incl.astype(F32)
        self.eye = (r == c).astype(F32)
        rows = lax.broadcasted_iota(jnp.int32, (SC, 1), 0)
        self.last_col = ((rows & (CH - 1)) == jnp.where(rev, 0, CH - 1)).astype(F32)
        rr = lax.broadcasted_iota(jnp.int32, (SC, CH), 0)
        cc = lax.broadcasted_iota(jnp.int32, (SC, CH), 1)
        self.fold = ((rr & (CH - 1)) == cc).astype(F32)


def _dot3(a, b, ca=1, cb=0):
    ah, bh = a.astype(BF16), b.astype(BF16)
    al, bl = (a - ah.astype(F32)).astype(BF16), (b - bh.astype(F32)).astype(BF16)
    return _bdot(ah, bh, ca, cb) + (_bdot(ah, bl, ca, cb) + _bdot(al, bh, ca, cb))


TRI_SPLIT_LEVELS = 2


def _tri_inv(low, eye):
    n = -low
    acc = eye + n
    p = n
    for level in range(5):
        dot = _dot3 if level < TRI_SPLIT_LEVELS else (lambda a, b: _bdot(a, b, 1, 0))
        p = dot(p, p)
        acc = acc + dot(acc, p)
    return acc


@jax.custom_vjp
def _solve2(low, rv, rk, tinv):
    x = _dot3(tinv, jnp.concatenate([rv, rk], axis=1))
    return x[:, :DA], x[:, DA:]


def _solve2_fwd(low, rv, rk, tinv):
    x = _dot3(tinv, jnp.concatenate([rv, rk], axis=1))
    return (x[:, :DA], x[:, DA:]), (x, tinv)


def _solve2_bwd(res, g):
    x, tinv = res
    drhs = _dot3(tinv, jnp.concatenate(g, axis=1), 0, 0)
    return -_dot3(drhs, x, 1, 1), drhs[:, :DA], drhs[:, DA:], jnp.zeros_like(tinv)


_solve2.defvjp(_solve2_fwd, _solve2_bwd)


def _chunk_last(x, cs):
    xs = (x * cs.last_col).reshape(SC // CH, CH, x.shape[1])
    return jnp.broadcast_to(jnp.sum(xs, axis=1, keepdims=True), xs.shape).reshape(x.shape)


def _gdn_decay(g, cs):
    gw = jnp.concatenate([g] * (SC // DA), axis=1)
    grow = jnp.sum(cs.eye * gw, axis=0, keepdims=True)
    return jnp.where(cs.incl, jnp.exp(jnp.where(cs.incl, gw - grow, 0.0)), 0.0)


def _gdn_intra(q, k, v, g, bx, tinv, cs):
    decay = _gdn_decay(g, cs)
    kb = k * bx
    low = jnp.where(cs.strict, mm_nt(kb, k) * decay, 0.0)
    eg = jnp.exp(g)
    made = tinv is None
    if made:
        tinv = _tri_inv(low, cs.eye)
    u, w = _solve2(low, v * bx, kb * eg, tinv)
    attn = mmc(mm_nt(q, k) * decay, cs.fold)
    qd = q * eg
    glast = _chunk_last(g, cs)
    kd = k * jnp.exp(glast - g)
    outs = (u, w, attn, qd, kd, jnp.exp(glast))
    return outs + (tinv,) if made else outs


def _gdn_scan(u, w, attn, qd, kd, el, s):
    vn = u - mm(w, s)
    o = mm(qd, s) + mm(attn, vn)
    sn = s * el + mm_tn(kd, vn)
    return o, sn


def _gla_intra(q, k, v, gk, cs):
    gc = cmm(cs.incl_f, gk)
    qg = q * (DKB ** -0.5) * jnp.exp(gc)
    kg = k * jnp.exp(-gc)
    attn = jnp.where(cs.incl, mm_nt(qg, kg), 0.0)
    intra = mm(attn, v)
    glast = _chunk_last(gc, cs)
    kd = k * jnp.exp(glast - gc)
    return qg, kd, intra, jnp.exp(glast)


def _gla_scan(qg, kd, v, el, st):
    o = mm_nt(qg, st)
    stn = st * el + mm_tn(v, kd)
    return o, stn


def _shift_rows(x, s):
    if s == 0:
        return x
    t = x.shape[0]
    rolled = pltpu.roll(x, (-s) % t, 0)
    rows = lax.broadcasted_iota(jnp.int32, x.shape, 0)
    return jnp.where((rows + s >= 0) & (rows + s < t), rolled, 0.0)


@jax.custom_vjp
def _conv5(x, w):
    acc = w[0:1] * _shift_rows(x, -2)
    for j in range(1, 5):
        acc = acc + w[j:j + 1] * _shift_rows(x, j - 2)
    return acc


def _conv5_fwd(x, w):
    return _conv5(x, w), (x, w)


def _conv5_bwd(res, g):
    x, w = res
    dx = w[0:1] * _shift_rows(g, 2)
    for j in range(1, 5):
        dx = dx + w[j:j + 1] * _shift_rows(g, 2 - j)
    rows = lax.broadcasted_iota(jnp.int32, w.shape, 0)
    dw = jnp.zeros_like(w)
    for j in range(5):
        dwj = jnp.sum(g * _shift_rows(x, j - 2), axis=0, keepdims=True)
        dw = dw + jnp.where(rows == j, dwj, 0.0)
    return dx, dw


_conv5.defvjp(_conv5_fwd, _conv5_bwd)


def _qkv_act(kind):
    def f(x, w):
        c = _silu(_conv5(x, w))
        if kind == 2:
            return c
        c = c * lax.rsqrt(jnp.sum(c * c, axis=-1, keepdims=True) + EPS)
        return c * (DA ** -0.5) if kind == 0 else c
    return f


def _inproj(x, lnw, wperm, tm=512, tn=512):
    t = x.shape[0]

    def body(x_ref, lnw_ref, w_ref, p_ref, h_ref, hbuf):
        @pl.when(pl.program_id(1) == 0)
        def _():
            hb = _rms(x_ref[...], lnw_ref[...]).astype(BF16)
            hbuf[...] = hb
            h_ref[...] = hb
        p_ref[...] = _bdot(hbuf[...], w_ref[...], 1, 1)

    return pl.pallas_call(
        body, name="inproj", grid=(t // tm, NPERM // tn),
        in_specs=[pl.BlockSpec((tm, D), lambda i, j: (i, 0)),
                  pl.BlockSpec((1, D), lambda i, j: (0, 0)),
                  pl.BlockSpec((tn, D), lambda i, j: (j, 0))],
        out_specs=[pl.BlockSpec((tm, tn), lambda i, j: (i, j)),
                   pl.BlockSpec((tm, D), lambda i, j: (i, 0))],
        out_shape=[jax.ShapeDtypeStruct((t, NPERM), F32),
                   jax.ShapeDtypeStruct((t, D), BF16)],
        scratch_shapes=[pltpu.VMEM((tm, D), BF16)],
        compiler_params=_cparams(("parallel", "arbitrary")),
    )(x, lnw, wperm)


DP_TILE = 512
DP_PIECES = ((0, 19),)


def _piece_specs(tm, j_first):
    specs = []
    for j0, n in DP_PIECES:
        def imap(a, b, j0=j0, n=n):
            j, i = (a, b) if j_first else (b, a)
            inside = (j >= j0) & (j < j0 + n)
            return jnp.where(inside, i, 0), jnp.clip(j - j0, 0, n - 1)
        specs.append(pl.BlockSpec((tm, DP_TILE), imap))
    return specs


def _for_piece(j, refs, fn):
    for (j0, n), ref in zip(DP_PIECES, refs):
        @pl.when((j >= j0) & (j < j0 + n))
        def _(ref=ref):
            fn(ref[...])


def _inproj_dw(h, pieces, tm=512):
    t = h.shape[0]
    npc = len(pieces)

    def body(h_ref, *refs):
        dw_ref = refs[npc]

        @pl.when(pl.program_id(1) == 0)
        def _():
            dw_ref[...] = jnp.zeros_like(dw_ref)

        def add(dp):
            dw_ref[...] += _bdot(dp, h_ref[...], 0, 0)
        _for_piece(pl.program_id(0), refs[:npc], add)

    return pl.pallas_call(
        body, name="inproj_dw", grid=(NPERM // DP_TILE, t // tm),
        in_specs=[pl.BlockSpec((tm, D), lambda j, i: (i, 0))] + _piece_specs(tm, True),
        out_specs=pl.BlockSpec((DP_TILE, D), lambda j, i: (j, 0)),
        out_shape=jax.ShapeDtypeStruct((NPERM, D), F32),
        compiler_params=_cparams(("parallel", "arbitrary")),
    )(h, *pieces)


def _inproj_dx(pieces, wperm, x, lnw, dyres, tm=512):
    t = x.shape[0]
    tn = DP_TILE
    nj = NPERM // tn
    npc = len(pieces)

    def body(*refs):
        w_ref, x_ref, lnw_ref, dy_ref, dx_ref, dlnw_ref, acc = refs[npc:]
        j = pl.program_id(1)

        @pl.when(j == 0)
        def _():
            acc[...] = jnp.zeros_like(acc)

        def add(dp):
            acc[...] += _bdot(dp, w_ref[...], 1, 0)
        _for_piece(j, refs[:npc], add)

        @pl.when(j == nj - 1)
        def _():
            _, vjp = jax.vjp(_rms, x_ref[...], lnw_ref[...])
            dx, dlnw = vjp(acc[...])
            dx_ref[...] = dx + dy_ref[...]

            @pl.when(pl.program_id(0) == 0)
            def _():
                dlnw_ref[...] = jnp.zeros_like(dlnw_ref)
            dlnw_ref[...] += jnp.broadcast_to(dlnw, dlnw_ref.shape)

    return pl.pallas_call(
        body, name="inproj_dx", grid=(t // tm, nj),
        in_specs=_piece_specs(tm, False) + [
                  pl.BlockSpec((tn, D), lambda i, j: (j, 0)),
                  pl.BlockSpec((tm, D), lambda i, j: (i, 0)),
                  pl.BlockSpec((1, D), lambda i, j: (0, 0)),
                  pl.BlockSpec((tm, D), lambda i, j: (i, 0))],
        out_specs=[pl.BlockSpec((tm, D), lambda i, j: (i, 0)),
                   pl.BlockSpec((8, D), lambda i, j: (0, 0))],
        out_shape=[jax.ShapeDtypeStruct((t, D), F32), jax.ShapeDtypeStruct((8, D), F32)],
        scratch_shapes=[pltpu.VMEM((tm, D), F32)],
        compiler_params=_cparams(("arbitrary", "arbitrary")),
    )(*pieces, wperm, x, lnw, dyres)


def _qkv_fwd(p, convw, kind):
    t = p.shape[0]
    f = _qkv_act(kind)

    def body(p_ref, w_ref, o_ref):
        o_ref[...] = f(p_ref[...], w_ref[...])

    return pl.pallas_call(
        body, name=f"qkv_fwd{kind}", grid=(NA,),
        in_specs=[pl.BlockSpec((t, DA), lambda h: (0, kind * NA + h)),
                  pl.BlockSpec((8, DA), lambda h: (0, kind * NA + h))],
        out_specs=pl.BlockSpec((t, DA), lambda h: (0, h)),
        out_shape=jax.ShapeDtypeStruct((t, D), F32),
        compiler_params=_cparams(("parallel",)),
    )(p, convw)


def _qkv_bwd(p, convw, dout, kind):
    t = p.shape[0]
    f = _qkv_act(kind)

    def body(p_ref, w_ref, g_ref, dx_ref, dw_ref):
        _, vjp = jax.vjp(f, p_ref[...], w_ref[...])
        dx, dw = vjp(g_ref[...])
        dx_ref[...] = dx.astype(BF16)
        dw_ref[...] = dw

    return pl.pallas_call(
        body, name=f"qkv_bwd{kind}", grid=(NA,),
        in_specs=[pl.BlockSpec((t, DA), lambda h: (0, kind * NA + h)),
                  pl.BlockSpec((8, DA), lambda h: (0, kind * NA + h)),
                  pl.BlockSpec((t, DA), lambda h: (0, h))],
        out_specs=[pl.BlockSpec((t, DA), lambda h: (0, h)),
                   pl.BlockSpec((8, DA), lambda h: (0, h))],
        out_shape=[jax.ShapeDtypeStruct((t, D), BF16), jax.ShapeDtypeStruct((8, D), F32)],
        compiler_params=_cparams(("parallel",)),
    )(p, convw, dout)


def _gates_f(ps, alog_row, dt_row, w2f, b2f, w2b, b2b):
    lane = lax.broadcasted_iota(jnp.int32, ps.shape, 1)
    lg = -jnp.exp(alog_row) * _softplus(ps + dt_row)
    gsm = jnp.where(lane < 16, lg, jnp.where(lane < 32, _sigmoid(ps), 0.0))
    gkf = -_softplus(-(mm(ps, w2f) + b2f)) * (1.0 / 16.0)
    gkb = -_softplus(-(mm(ps, w2b) + b2b)) * (1.0 / 16.0)
    return gsm, gkf, gkb


def _gates_fwd(ps, alog_row, dt_row, w2f, b2f, w2b, b2b, tm=512):
    t = ps.shape[0]

    def body(ps_ref, a_ref, d_ref, wf_ref, bf_ref, wb_ref, bb_ref, gsm_ref, gk_ref):
        gsm, gkf, gkb = _gates_f(ps_ref[...], a_ref[...], d_ref[...], wf_ref[...], bf_ref[...],
                                 wb_ref[...], bb_ref[...])
        gsm_ref[...] = gsm
        gk_ref[0] = gkf
        gk_ref[1] = gkb

    row = lambda n: pl.BlockSpec((1, n), lambda i: (0, 0))
    mat = pl.BlockSpec((128, 512), lambda i: (0, 0))
    return pl.pallas_call(
        body, name="gates_fwd", grid=(t // tm,),
        in_specs=[pl.BlockSpec((tm, 128), lambda i: (i, PS_BLOCK)), row(128), row(128), mat, row(512), mat, row(512)],
        out_specs=[pl.BlockSpec((tm, 128), lambda i: (i, 0)),
                   pl.BlockSpec((2, tm, 512), lambda i: (0, i, 0))],
        out_shape=[jax.ShapeDtypeStruct((t, 128), F32), jax.ShapeDtypeStruct((2, t, 512), F32)],
        compiler_params=_cparams(("parallel",)),
    )(ps, alog_row, dt_row, w2f, b2f, w2b, b2b)


def _gates_bwd(ps, alog_row, dt_row, w2f, b2f, w2b, b2b, dgsm, dgk, tm=512):
    t = ps.shape[0]

    def body(ps_ref, a_ref, d_ref, wf_ref, bf_ref, wb_ref, bb_ref, dgsm_ref, dgk_ref,
             dps_ref, da_ref, dd_ref, dwf_ref, dbf_ref, dwb_ref, dbb_ref):
        _, vjp = jax.vjp(_gates_f, ps_ref[...], a_ref[...], d_ref[...], wf_ref[...], bf_ref[...],
                         wb_ref[...], bb_ref[...])
        dps, da, dd, dwf, dbf, dwb, dbb = vjp((dgsm_ref[...], dgk_ref[0], dgk_ref[1]))
        dps_ref[:, 0:128] = dps.astype(BF16)
        dps_ref[:, 128:DP_TILE] = jnp.zeros((tm, DP_TILE - 128), BF16)
        accs = ((da_ref, da), (dd_ref, dd), (dwf_ref, dwf), (dbf_ref, dbf), (dwb_ref, dwb), (dbb_ref, dbb))

        @pl.when(pl.program_id(0) == 0)
        def _():
            for ref, _ in accs:
                ref[...] = jnp.zeros_like(ref)
        for ref, val in accs:
            ref[...] += jnp.broadcast_to(val, ref.shape)

    row = lambda n: pl.BlockSpec((1, n), lambda i: (0, 0))
    row8 = lambda n: pl.BlockSpec((8, n), lambda i: (0, 0))
    mat = pl.BlockSpec((128, 512), lambda i: (0, 0))
    return pl.pallas_call(
        body, name="gates_bwd", grid=(t // tm,),
        in_specs=[pl.BlockSpec((tm, 128), lambda i: (i, PS_BLOCK)), row(128), row(128), mat, row(512), mat, row(512),
                  pl.BlockSpec((tm, 128), lambda i: (i, 0)),
                  pl.BlockSpec((2, tm, 512), lambda i: (0, i, 0))],
        out_specs=[pl.BlockSpec((tm, DP_TILE), lambda i: (i, 0)), row8(128), row8(128), mat, row8(512), mat,
                   row8(512)],
        out_shape=[jax.ShapeDtypeStruct((t, DP_TILE), BF16),
                   jax.ShapeDtypeStruct((8, 128), F32), jax.ShapeDtypeStruct((8, 128), F32),
                   jax.ShapeDtypeStruct((128, 512), F32), jax.ShapeDtypeStruct((8, 512), F32),
                   jax.ShapeDtypeStruct((128, 512), F32), jax.ShapeDtypeStruct((8, 512), F32)],
        compiler_params=_cparams(("arbitrary",)),
    )(ps, alog_row, dt_row, w2f, b2f, w2b, b2b, dgsm, dgk)


def _rows(i):
    return pl.ds(pl.multiple_of(i * CH, CH), CH)


def _srows(i):
    return pl.ds(pl.multiple_of(i * SC, SC), SC)


def _first_row(x):
    row = lax.broadcasted_iota(jnp.int32, (8, x.shape[1]), 0)
    return jnp.where(row == 0, jnp.broadcast_to(x, (8, x.shape[1])), 0.0)


def _chunk_rows(e_ref, i):
    pad = jnp.zeros((CH - 8, 128), F32)
    return jnp.concatenate([x for c in range(SC // CH) for x in (e_ref[(SC // CH) * i + c], pad)], axis=0)


def _gcum_f(gsm, tm):
    i = lax.broadcasted_iota(jnp.int32, (tm, tm), 0)
    j = lax.broadcasted_iota(jnp.int32, (tm, tm), 1)
    same = (i >> 6) == (j >> 6)
    lower = (same & (i >= j)).astype(F32)
    upper = (same & (i <= j)).astype(F32)
    r = lax.broadcasted_iota(jnp.int32, (128, D), 0)
    head = lax.broadcasted_iota(jnp.int32, (128, D), 1) >> 7
    pick = lambda off: (r == head + off).astype(F32)
    lane = lax.broadcasted_iota(jnp.int32, gsm.shape, 1)
    run = jnp.where(lane < 8, cmm(lower, gsm), cmm(upper, gsm))
    return mmc(run, pick(0)), mmc(run, pick(8)), mmc(gsm, pick(16)), mmc(gsm, pick(24))


def _gcum_fwd(gsm, tm=256):
    t = gsm.shape[0]

    def body(s_ref, g_ref, b_ref):
        gf, gb, bf, bb = _gcum_f(s_ref[...], tm)
        g_ref[0] = gf
        g_ref[1] = gb
        b_ref[0] = bf
        b_ref[1] = bb

    two = pl.BlockSpec((2, tm, D), lambda i: (0, i, 0))
    return pl.pallas_call(
        body, name="gcum_fwd", grid=(t // tm,),
        in_specs=[pl.BlockSpec((tm, 128), lambda i: (i, 0))], out_specs=[two, two],
        out_shape=[jax.ShapeDtypeStruct((2, t, D), F32)] * 2,
        compiler_params=_cparams(("parallel",)),
    )(gsm)


def _gcum_bwd(gsm, dg2, db2, tm=256):
    t = gsm.shape[0]

    def body(s_ref, dg_ref, db_ref, ds_ref):
        _, vjp = jax.vjp(lambda s: _gcum_f(s, tm), s_ref[...])
        ds_ref[...] = vjp((dg_ref[0], dg_ref[1], db_ref[0], db_ref[1]))[0]

    two = pl.BlockSpec((2, tm, D), lambda i: (0, i, 0))
    tile = pl.BlockSpec((tm, 128), lambda i: (i, 0))
    return pl.pallas_call(
        body, name="gcum_bwd", grid=(t // tm,),
        in_specs=[tile, two, two], out_specs=tile,
        out_shape=jax.ShapeDtypeStruct((t, 128), F32),
        compiler_params=_cparams(("parallel",)),
    )(gsm, dg2, db2)


def _gdn_intra_fwd(qn, kn, vc, g2, b2):
    t = qn.shape[0]
    n = t // CH

    def body(q_ref, k_ref, v_ref, g_ref, b_ref, u_ref, w_ref, a_ref, qd_ref, kd_ref, e_ref, t_ref):
        cs = _Consts(pl.program_id(0) == 1)

        def step(i, carry):
            r = _srows(i)
            q, k, v, g, bx = q_ref[r, :], k_ref[r, :], v_ref[r, :], g_ref[r, :], b_ref[r, :]
            u, w, a, qd, kd, el, tinv = _gdn_intra(q, k, v, g, bx, None, cs)
            u_ref[r, :] = u
            w_ref[r, :] = w
            a_ref[r, :] = a
            qd_ref[r, :] = qd
            kd_ref[r, :] = kd
            t_ref[r, :] = tinv
            for c in range(SC // CH):
                e_ref[(SC // CH) * i + c] = el[c * CH:c * CH + 8]
            return carry

        lax.fori_loop(0, t // SC, step, 0)

    head = pl.BlockSpec((t, DA), lambda d, h: (0, h))
    dh = pl.BlockSpec((None, t, DA), lambda d, h: (d, 0, h))
    sq = lambda w: pl.BlockSpec((None, None, t, w), lambda d, h: (d, h, 0, 0))
    big = jax.ShapeDtypeStruct((2, t, D), F32)
    return pl.pallas_call(
        body, name="gdn_intra_fwd", grid=(2, NA),
        in_specs=[head, head, head, dh, dh],
        out_specs=[dh, dh, sq(CH), dh, dh, pl.BlockSpec((None, None, n, 8, 128), lambda d, h: (d, h, 0, 0, 0)),
                   sq(SC)],
        out_shape=[big, big, jax.ShapeDtypeStruct((2, NA, t, CH), F32), big, big,
                   jax.ShapeDtypeStruct((2, NA, n, 8, 128), F32), jax.ShapeDtypeStruct((2, NA, t, SC), F32)],
        compiler_params=_cparams(("parallel", "parallel")),
    )(qn, kn, vc, g2, b2)


SCAN_TB = 256
SCAN_HB = 8


def _scan_specs(t, width, nheads, hb, along):
    nt = t // SCAN_TB
    nb = SCAN_TB // CH

    def tmap(d, tt):
        fwd = tt + d * (nt - 1 - 2 * tt)
        return fwd if along > 0 else nt - 1 - fwd

    tok = pl.BlockSpec((None, SCAN_TB, hb * width), lambda d, h, tt: (d, tmap(d, tt), h))
    per = lambda *tail: pl.BlockSpec((None, hb, nb) + tail, lambda d, h, tt: (d, h, tmap(d, tt)) + (0,) * len(tail))
    sq = pl.BlockSpec((None, hb, SCAN_TB, CH), lambda d, h, tt: (d, h, tmap(d, tt), 0))
    shared = lambda w: pl.BlockSpec((SCAN_TB, hb * w), lambda d, h, tt: (tmap(d, tt), h))
    return tok, per, sq, shared, (2, nheads // hb, nt), nb


def _gdn_scan_fwd(u, w, a, qd, kd, e):
    t = u.shape[1]
    tok, per, sq, _, grid, nb = _scan_specs(t, DA, NA, SCAN_HB, +1)

    def body(u_ref, w_ref, a_ref, qd_ref, kd_ref, e_ref, o_ref, s_ref, state):
        rev = pl.program_id(0) == 1

        @pl.when(pl.program_id(2) == 0)
        def _():
            state[...] = jnp.zeros_like(state)

        def step(i, ss):
            ci = jnp.where(rev, nb - 1 - i, i)
            r = _rows(ci)
            out = []
            for hh, s in enumerate(ss):
                c = slice(hh * DA, (hh + 1) * DA)
                s_ref[hh, ci] = s
                o, sn = _gdn_scan(u_ref[r, c], w_ref[r, c], a_ref[hh, r, :], qd_ref[r, c], kd_ref[r, c],
                                  e_ref[hh, ci][0:1], s)
                o_ref[r, c] = o
                out.append(sn)
            return tuple(out)

        ss = lax.fori_loop(0, nb, step, tuple(state[hh] for hh in range(SCAN_HB)))
        for hh, s in enumerate(ss):
            state[hh] = s

    return pl.pallas_call(
        body, name="gdn_scan_fwd", grid=grid,
        in_specs=[tok, tok, sq, tok, tok, per(8, 128)],
        out_specs=[tok, per(DA, DA)],
        out_shape=[jax.ShapeDtypeStruct((2, t, D), F32), jax.ShapeDtypeStruct((2, NA, t // CH, DA, DA), F32)],
        scratch_shapes=[pltpu.VMEM((SCAN_HB, DA, DA), F32)],
        compiler_params=_cparams(("parallel", "parallel", "arbitrary")),
    )(u, w, a, qd, kd, e)


def _gdn_scan_bwd(u, w, a, qd, kd, e, ssave, do):
    t = u.shape[1]
    tok, per, sq, shared, grid, nb = _scan_specs(t, DA, NA, SCAN_HB, -1)

    def body(u_ref, w_ref, a_ref, qd_ref, kd_ref, e_ref, s_ref, do_ref,
             du_ref, dw_ref, da_ref, dqd_ref, dkd_ref, de_ref, state):
        rev = pl.program_id(0) == 1

        @pl.when(pl.program_id(2) == 0)
        def _():
            state[...] = jnp.zeros_like(state)

        def step(i, dss):
            ci = jnp.where(rev, i, nb - 1 - i)
            r = _rows(ci)
            out = []
            for hh, ds in enumerate(dss):
                c = slice(hh * DA, (hh + 1) * DA)
                _, vjp = jax.vjp(_gdn_scan, u_ref[r, c], w_ref[r, c], a_ref[hh, r, :], qd_ref[r, c], kd_ref[r, c],
                                 e_ref[hh, ci][0:1], s_ref[hh, ci])
                du, dw, da, dqd, dkd, de, dsn = vjp((do_ref[r, c], ds))
                du_ref[r, c] = du
                dw_ref[r, c] = dw
                da_ref[hh, r, :] = da
                dqd_ref[r, c] = dqd
                dkd_ref[r, c] = dkd
                de_ref[hh, ci] = _first_row(de)
                out.append(dsn)
            return tuple(out)

        dss = lax.fori_loop(0, nb, step, tuple(state[hh] for hh in range(SCAN_HB)))
        for hh, ds in enumerate(dss):
            state[hh] = ds

    big = jax.ShapeDtypeStruct((2, t, D), F32)
    return pl.pallas_call(
        body, name="gdn_scan_bwd", grid=grid,
        in_specs=[tok, tok, sq, tok, tok, per(8, 128), per(DA, DA), shared(DA)],
        out_specs=[tok, tok, sq, tok, tok, per(8, 128)],
        out_shape=[big, big, jax.ShapeDtypeStruct((2, NA, t, CH), F32), big, big,
                   jax.ShapeDtypeStruct((2, NA, t // CH, 8, 128), F32)],
        scratch_shapes=[pltpu.VMEM((SCAN_HB, DA, DA), F32)],
        compiler_params=_cparams(("parallel", "parallel", "arbitrary")),
    )(u, w, a, qd, kd, e, ssave, do)


def _gdn_intra_bwd(qn, kn, vc, g2, b2, tinv, du, dw, da, dqd, dkd, de):
    t = qn.shape[0]
    n = t // CH

    def body(q_ref, k_ref, v_ref, g_ref, b_ref, t_ref, du_ref, dw_ref, da_ref, dqd_ref, dkd_ref, de_ref,
             dq_ref, dk_ref, dv_ref, dg_ref, db_ref):
        d = pl.program_id(1)
        cs = _Consts(d == 1)

        @pl.when(d == 0)
        def _():
            dq_ref[...] = jnp.zeros_like(dq_ref)
            dk_ref[...] = jnp.zeros_like(dk_ref)
            dv_ref[...] = jnp.zeros_like(dv_ref)

        def step(i, carry):
            r = _srows(i)
            tinv_c = t_ref[r, :]
            f = lambda q, k, v, g, bx: _gdn_intra(q, k, v, g, bx, tinv_c, cs)
            _, vjp = jax.vjp(f, q_ref[r, :], k_ref[r, :], v_ref[r, :], g_ref[r, :], b_ref[r, :])
            dq, dk, dv, dg, dbx = vjp((du_ref[r, :], dw_ref[r, :], da_ref[r, :], dqd_ref[r, :],
                                       dkd_ref[r, :], _chunk_rows(de_ref, i)))
            dq_ref[r, :] += dq
            dk_ref[r, :] += dk
            dv_ref[r, :] += dv
            dg_ref[r, :] = dg
            db_ref[r, :] = dbx
            return carry

        lax.fori_loop(0, t // SC, step, 0)

    head = pl.BlockSpec((t, DA), lambda h, d: (0, h))
    dh = pl.BlockSpec((None, t, DA), lambda h, d: (d, 0, h))
    sq = pl.BlockSpec((None, None, t, CH), lambda h, d: (d, h, 0, 0))
    tq = pl.BlockSpec((None, None, t, SC), lambda h, d: (d, h, 0, 0))
    full = jax.ShapeDtypeStruct((t, D), F32)
    big = jax.ShapeDtypeStruct((2, t, D), F32)
    return pl.pallas_call(
        body, name="gdn_intra_bwd", grid=(NA, 2),
        in_specs=[head, head, head, dh, dh, tq, dh, dh, sq, dh, dh,
                  pl.BlockSpec((None, None, n, 8, 128), lambda h, d: (d, h, 0, 0, 0))],
        out_specs=[head, head, head, dh, dh],
        out_shape=[full, full, full, big, big],
        compiler_params=_cparams(("arbitrary", "arbitrary")),
    )(qn, kn, vc, g2, b2, tinv, du, dw, da, dqd, dkd, de)


def _gla_specs(t, order):
    ix = (lambda d, h: (d, h)) if order == "dh" else (lambda h, d: (d, h))

    def mk(fn):
        return lambda a, b: fn(*ix(a, b))
    q = pl.BlockSpec((t, DKB), mk(lambda d, h: (0, 32 + h)))
    k = pl.BlockSpec((t, DKB), mk(lambda d, h: (0, 36 + h)))
    v = pl.BlockSpec((t, DVB), mk(lambda d, h: (0, 20 + h)))
    dk = pl.BlockSpec((None, t, DKB), mk(lambda d, h: (d, 0, h)))
    dv = pl.BlockSpec((None, t, DVB), mk(lambda d, h: (d, 0, h)))
    e = pl.BlockSpec((None, None, t // CH, 8, 128), mk(lambda d, h: (d, h, 0, 0, 0)))
    s = pl.BlockSpec((None, None, t // CH, DVB, DKB), mk(lambda d, h: (d, h, 0, 0, 0)))
    return q, k, v, dk, dv, e, s


def _gla_intra_fwd(p, gk):
    t = p.shape[0]
    n = t // CH

    def body(q_ref, k_ref, v_ref, g_ref, qg_ref, kd_ref, in_ref, e_ref):
        cs = _Consts(pl.program_id(0) == 1)

        def step(i, carry):
            r = _srows(i)
            qg, kd, intra, el = _gla_intra(q_ref[r, :], k_ref[r, :], v_ref[r, :], g_ref[r, :], cs)
            qg_ref[r, :] = qg
            kd_ref[r, :] = kd
            in_ref[r, :] = intra
            for c in range(SC // CH):
                e_ref[(SC // CH) * i + c] = el[c * CH:c * CH + 8]
            return carry

        lax.fori_loop(0, t // SC, step, 0)

    q, k, v, dk, dv, e, _ = _gla_specs(t, "dh")
    return pl.pallas_call(
        body, name="gla_intra_fwd", grid=(2, NB),
        in_specs=[q, k, v, dk], out_specs=[dk, dk, dv, e],
        out_shape=[jax.ShapeDtypeStruct((2, t, NB * DKB), F32), jax.ShapeDtypeStruct((2, t, NB * DKB), F32),
                   jax.ShapeDtypeStruct((2, t, D), F32), jax.ShapeDtypeStruct((2, NB, n, 8, 128), F32)],
        compiler_params=_cparams(("parallel", "parallel")),
    )(p, p, p, gk)


GLA_HB = 2


def _gla_v_spec(t, along):
    nt = t // SCAN_TB

    def tmap(d, tt):
        fwd = tt + d * (nt - 1 - 2 * tt)
        return fwd if along > 0 else nt - 1 - fwd

    return pl.BlockSpec((SCAN_TB, GLA_HB * DVB), lambda d, h, tt: (tmap(d, tt), 5120 // (GLA_HB * DVB) + h))


def _gla_scan_fwd(p, qg, kd, intra, e):
    t = p.shape[0]
    tokk, per, _, _, grid, nb = _scan_specs(t, DKB, NB, GLA_HB, +1)
    tokv = _scan_specs(t, DVB, NB, GLA_HB, +1)[0]

    def body(v_ref, qg_ref, kd_ref, in_ref, e_ref, o_ref, s_ref, state):
        rev = pl.program_id(0) == 1

        @pl.when(pl.program_id(2) == 0)
        def _():
            state[...] = jnp.zeros_like(state)

        def step(i, sts):
            ci = jnp.where(rev, nb - 1 - i, i)
            r = _rows(ci)
            out = []
            for hh, st in enumerate(sts):
                ck = slice(hh * DKB, (hh + 1) * DKB)
                cv = slice(hh * DVB, (hh + 1) * DVB)
                s_ref[hh, ci] = st
                o, stn = _gla_scan(qg_ref[r, ck], kd_ref[r, ck], v_ref[r, cv], e_ref[hh, ci][0:1], st)
                o_ref[r, cv] = o + in_ref[r, cv]
                out.append(stn)
            return tuple(out)

        sts = lax.fori_loop(0, nb, step, tuple(state[hh] for hh in range(GLA_HB)))
        for hh, st in enumerate(sts):
            state[hh] = st

    return pl.pallas_call(
        body, name="gla_scan_fwd", grid=grid,
        in_specs=[_gla_v_spec(t, +1), tokk, tokk, tokv, per(8, 128)], out_specs=[tokv, per(DVB, DKB)],
        out_shape=[jax.ShapeDtypeStruct((2, t, D), F32), jax.ShapeDtypeStruct((2, NB, t // CH, DVB, DKB), F32)],
        scratch_shapes=[pltpu.VMEM((GLA_HB, DVB, DKB), F32)],
        compiler_params=_cparams(("parallel", "parallel", "arbitrary")),
    )(p, qg, kd, intra, e)


def _gla_scan_bwd(p, qg, kd, e, ssave, do):
    t = p.shape[0]
    tokk, per, _, shared, grid, nb = _scan_specs(t, DKB, NB, GLA_HB, -1)
    tokv = _scan_specs(t, DVB, NB, GLA_HB, -1)[0]

    def body(v_ref, qg_ref, kd_ref, e_ref, s_ref, do_ref, dqg_ref, dkd_ref, dv_ref, de_ref, state):
        rev = pl.program_id(0) == 1

        @pl.when(pl.program_id(2) == 0)
        def _():
            state[...] = jnp.zeros_like(state)

        def step(i, dsts):
            ci = jnp.where(rev, i, nb - 1 - i)
            r = _rows(ci)
            out = []
            for hh, dst in enumerate(dsts):
                ck = slice(hh * DKB, (hh + 1) * DKB)
                cv = slice(hh * DVB, (hh + 1) * DVB)
                _, vjp = jax.vjp(_gla_scan, qg_ref[r, ck], kd_ref[r, ck], v_ref[r, cv], e_ref[hh, ci][0:1],
                                 s_ref[hh, ci])
                dqg, dkd, dv, de, dstn = vjp((do_ref[r, cv], dst))
                dqg_ref[r, ck] = dqg
                dkd_ref[r, ck] = dkd
                dv_ref[r, cv] = dv
                de_ref[hh, ci] = _first_row(de)
                out.append(dstn)
            return tuple(out)

        dsts = lax.fori_loop(0, nb, step, tuple(state[hh] for hh in range(GLA_HB)))
        for hh, dst in enumerate(dsts):
            state[hh] = dst

    return pl.pallas_call(
        body, name="gla_scan_bwd", grid=grid,
        in_specs=[_gla_v_spec(t, -1), tokk, tokk, per(8, 128), per(DVB, DKB), shared(DVB)],
        out_specs=[tokk, tokk, tokv, per(8, 128)],
        out_shape=[jax.ShapeDtypeStruct((2, t, NB * DKB), F32), jax.ShapeDtypeStruct((2, t, NB * DKB), F32),
                   jax.ShapeDtypeStruct((2, t, D), F32), jax.ShapeDtypeStruct((2, NB, t // CH, 8, 128), F32)],
        scratch_shapes=[pltpu.VMEM((GLA_HB, DVB, DKB), F32)],
        compiler_params=_cparams(("parallel", "parallel", "arbitrary")),
    )(p, qg, kd, e, ssave, do)


def _gla_intra_bwd(p, gk, dqg, dkd, dvs, de, do):
    t = p.shape[0]
    n = t // CH

    def body(q_ref, k_ref, v_ref, g_ref, dqg_ref, dkd_ref, dvs_ref, de_ref, do_ref,
             dq_ref, dk_ref, dv_ref, dg_ref):
        d = pl.program_id(1)
        cs = _Consts(d == 1)

        @pl.when(d == 0)
        def _():
            dq_ref[...] = jnp.zeros_like(dq_ref)
            dk_ref[...] = jnp.zeros_like(dk_ref)
            dv_ref[...] = jnp.zeros_like(dv_ref)

        def step(i, carry):
            r = _srows(i)
            f = lambda q, k, v, g: _gla_intra(q, k, v, g, cs)
            _, vjp = jax.vjp(f, q_ref[r, :], k_ref[r, :], v_ref[r, :], g_ref[r, :])
            dq, dk, dv, dg = vjp((dqg_ref[r, :], dkd_ref[r, :], do_ref[r, :], _chunk_rows(de_ref, i)))
            dq_ref[r, :] += dq
            dk_ref[r, :] += dk
            dv_ref[r, :] += dv + dvs_ref[r, :]
            dg_ref[r, :] = dg
            return carry

        lax.fori_loop(0, t // SC, step, 0)

    q, k, v, dk, dv, e_s, _ = _gla_specs(t, "hd")
    hk = pl.BlockSpec((t, DKB), lambda h, d: (0, h))
    hv = pl.BlockSpec((t, DVB), lambda h, d: (0, h))
    return pl.pallas_call(
        body, name="gla_intra_bwd", grid=(NB, 2),
        in_specs=[q, k, v, dk, dk, dk, dv, e_s, hv],
        out_specs=[hk, hk, hv, dk],
        out_shape=[jax.ShapeDtypeStruct((t, NB * DKB), F32), jax.ShapeDtypeStruct((t, NB * DKB), F32),
                   jax.ShapeDtypeStruct((t, D), F32), jax.ShapeDtypeStruct((2, t, NB * DKB), F32)],
        compiler_params=_cparams(("arbitrary", "arbitrary")),
    )(p, p, p, gk, dqg, dkd, dvs, de, do)


def _seg_gate(o, z, w):
    return _rms(o, w) * _silu(z)


def _seg_merge(ya, yb, ga, gb):
    return _sigmoid(ga) * ya + _sigmoid(gb) * yb


def _seg_loss(out, x, tgt, w):
    err = x + _rms(out, w) - tgt
    return 0.5 * jnp.sum(jnp.mean(err * err, axis=-1, keepdims=True), axis=0, keepdims=True)


def _post(oa2, ob2, p, x, tgt, gdn_w, gla_w, lnpost, w3, tm=128):
    t = x.shape[0]

    def body(oa_ref, ob_ref, z_ref, gb_ref, ga_ref, gB_ref, x_ref, t_ref, aw_ref, bw_ref, lw_ref, w_ref,
             loss_ref, doa_ref, dob_ref, dz_ref, dgb_ref, dga_ref, dgB_ref, dy_ref,
             dw_ref, daw_ref, dbw_ref, dlw_ref):
        first = pl.program_id(0) == 0
        oa = oa_ref[0] + oa_ref[1]
        ob = ob_ref[0] + ob_ref[1]
        z, gb = z_ref[...], gb_ref[...]
        aw, bw = aw_ref[...], bw_ref[...]
        rs = D // NSHARD

        def mat(a, m):
            return sum(jnp.dot(a[:, s * rs:(s + 1) * rs], w_ref[s, m], preferred_element_type=F32)
                       for s in range(NSHARD))

        def mat_t(g, m):
            return jnp.concatenate([_bdot(g, w_ref[s, m], 1, 1) for s in range(NSHARD)], axis=1)

        def add_dw(a, g, m):
            for s in range(NSHARD):
                dw_ref[s, m] += _bdot(a[:, s * rs:(s + 1) * rs], g, 0, 0)

        pa = [jax.vjp(_seg_gate, oa[:, h * DA:(h + 1) * DA], z[:, h * DA:(h + 1) * DA], aw) for h in range(NA)]
        pb = [jax.vjp(_seg_gate, ob[:, h * DVB:(h + 1) * DVB], gb[:, h * DVB:(h + 1) * DVB], bw)
              for h in range(NB)]
        a1 = jnp.concatenate([v for v, _ in pa], axis=1).astype(BF16)
        a2 = jnp.concatenate([v for v, _ in pb], axis=1).astype(BF16)
        ya = mat(a1, 0)
        yb = mat(a2, 1)
        merged, vjp_m = jax.vjp(_seg_merge, ya, yb, ga_ref[...], gB_ref[...])
        mb = merged.astype(BF16)
        out = mat(mb, 2)
        loss, vjp_l = jax.vjp(_seg_loss, out, x_ref[...], t_ref[...], lw_ref[...])
        dout, dyres, _, dlw = vjp_l(jnp.ones((1, 1), F32))
        dy_ref[...] = dyres
        doutb = dout.astype(BF16)
        dmerged = mat_t(doutb, 2)
        dya, dyb, dga, dgB = vjp_m(dmerged)
        dga_ref[...] = dga.astype(BF16)
        dgB_ref[...] = dgB.astype(BF16)
        dyab, dybb = dya.astype(BF16), dyb.astype(BF16)
        da1 = mat_t(dyab, 0)
        da2 = mat_t(dybb, 1)

        daw = jnp.zeros_like(aw)
        for h in range(NA):
            sl = slice(h * DA, (h + 1) * DA)
            do, dz, dw = pa[h][1](da1[:, sl])
            doa_ref[:, sl] = do
            dz_ref[:, sl] = dz.astype(BF16)
            daw = daw + dw
        dbw = jnp.zeros_like(bw)
        for h in range(NB):
            sl = slice(h * DVB, (h + 1) * DVB)
            do, dg, dw = pb[h][1](da2[:, sl])
            dob_ref[:, sl] = do
            dgb_ref[:, sl] = dg.astype(BF16)
            dbw = dbw + dw

        @pl.when(first)
        def _():
            loss_ref[...] = jnp.zeros_like(loss_ref)
            dw_ref[...] = jnp.zeros_like(dw_ref)
            daw_ref[...] = jnp.zeros_like(daw_ref)
            dbw_ref[...] = jnp.zeros_like(dbw_ref)
            dlw_ref[...] = jnp.zeros_like(dlw_ref)

        loss_ref[...] += jnp.broadcast_to(loss, loss_ref.shape)
        add_dw(a1, dyab, 0)
        add_dw(a2, dybb, 1)
        add_dw(mb, doutb, 2)
        daw_ref[...] += jnp.broadcast_to(daw, daw_ref.shape)
        dbw_ref[...] += jnp.broadcast_to(dbw, dbw_ref.shape)
        dlw_ref[...] += jnp.broadcast_to(dlw, dlw_ref.shape)

    two = pl.BlockSpec((2, tm, D), lambda i: (0, i, 0))
    pcol = lambda c: pl.BlockSpec((tm, D), lambda i: (i, c))
    tok = pl.BlockSpec((tm, D), lambda i: (i, 0))
    row = lambda n: pl.BlockSpec((1, n), lambda i: (0, 0))
    row8 = lambda n: pl.BlockSpec((8, n), lambda i: (0, 0))
    once = pl.Buffered(1)
    tokf = jax.ShapeDtypeStruct((t, D), F32)
    tokb = jax.ShapeDtypeStruct((t, D), BF16)
    wspec = pl.BlockSpec((NSHARD, 3, D // NSHARD, D), lambda i: (0, 0, 0, 0), pipeline_mode=once)
    return pl.pallas_call(
        body, name="post", grid=(t // tm,),
        in_specs=[two, two, pcol(3), pcol(6), pcol(7), pcol(8), tok, tok, row(DA), row(DVB), row(D), wspec],
        out_specs=[row8(128), tok, tok, tok, tok, tok, tok, tok, wspec, row8(DA), row8(DVB), row8(D)],
        out_shape=[jax.ShapeDtypeStruct((8, 128), F32), tokf, tokf, tokb, tokb, tokb, tokb, tokf,
                   jax.ShapeDtypeStruct((NSHARD, 3, D // NSHARD, D), F32),
                   jax.ShapeDtypeStruct((8, DA), F32), jax.ShapeDtypeStruct((8, DVB), F32),
                   jax.ShapeDtypeStruct((8, D), F32)],
        compiler_params=_cparams(("arbitrary",), vmem_mb=56),
    )(oa2, ob2, p, p, p, p, x, tgt, gdn_w, gla_w, lnpost, w3)


def _adam_math(w, g, m, v):
    nm = B1 * m + (1.0 - B1) * g
    nv = B2 * v + (1.0 - B2) * (g * g)
    m_hat = nm / (1.0 - B1 ** STEP)
    v_hat = nv / (1.0 - B2 ** STEP)
    return -LR * (m_hat / (jnp.sqrt(v_hat) + ADAM_EPS) + WD * w), nm, nv


SMALL_SLOTS = (("ln_pre_w", 0, 1024, 0), ("a_log_fwd", 1024, 8, 0), ("a_log_bwd", 1024, 8, 8),
               ("dt_bias_fwd", 1152, 8, 0), ("dt_bias_bwd", 1152, 8, 8), ("gdn_norm_w", 1280, 128, 0),
               ("gk_b2_fwd", 1408, 512, 0), ("gk_b2_bwd", 1920, 512, 0), ("gla_norm_w", 2432, 256, 0),
               ("ln_post_w", 2688, 1024, 0))
SMALL_W = 3712


def _adam_small(gsum, ws, ms, vs):
    nw = len(SMALL_SLOTS)

    def body(g_ref, *refs):
        w_refs, m_refs, v_refs, outs = refs[0:nw], refs[nw:2 * nw], refs[2 * nw:3 * nw], refs[3 * nw:]
        for i, (_, off, n, shift) in enumerate(SMALL_SLOTS):
            slot = g_ref[0:1, off:off + max(n, 128)]
            if shift:
                slot = pltpu.roll(slot, 128 - shift, 1)
            g = slot[:, 0:n]
            d, nm, nv = _adam_math(w_refs[i][...], g, m_refs[i][...], v_refs[i][...])
            for k, val in enumerate((g, d, nm, nv)):
                outs[4 * i + k][...] = val

    vm = pl.BlockSpec(memory_space=pltpu.VMEM)
    res = pl.pallas_call(
        body, name="adam_small", in_specs=[vm] * (1 + 3 * nw), out_specs=[vm] * (4 * nw),
        out_shape=[jax.ShapeDtypeStruct((1, n), F32) for _, _, n, _ in SMALL_SLOTS for _ in range(4)],
    )(gsum, *ws, *ms, *vs)
    return {name: res[4 * i:4 * i + 4] for i, (name, _, _, _) in enumerate(SMALL_SLOTS)}


def _adam(w, mine, got, m, v, tr, tile0=0, name=""):
    rows, cols = w.shape
    nh = mine.shape[0] // tr

    def body(c_ref, w_ref, a_ref, b_ref, m_ref, v_ref, g_ref, d_ref, nm_ref, nv_ref):
        half = (tile0 + pl.program_id(0)) // nh
        g = jnp.where(half == c_ref[0], a_ref[...], b_ref[...])
        d, nm, nv = _adam_math(w_ref[...], g, m_ref[...], v_ref[...])
        g_ref[...] = g
        d_ref[...] = d
        nm_ref[...] = nm
        nv_ref[...] = nv

    blk = pl.BlockSpec((tr, cols), lambda i, cc: (i, 0))
    half = pl.BlockSpec((tr, cols), lambda i, cc: ((tile0 + i) % nh, 0))
    shp = jax.ShapeDtypeStruct((rows, cols), F32)
    return pl.pallas_call(
        body, name=f"adam_{name}{rows}x{cols}",
        grid_spec=pltpu.PrefetchScalarGridSpec(
            num_scalar_prefetch=1, grid=(rows // tr,),
            in_specs=[blk, half, half, blk, blk], out_specs=[blk] * 4),
        out_shape=[shp] * 4,
        compiler_params=_cparams(("parallel",)),
    )(lax.axis_index("c").reshape(1), w, mine, got, m, v)


def _sum_cast(own, got):
    ns, _, r, c = own.shape
    tr = r // 4 if r % 64 == 0 else r

    def body(c_ref, a_ref, b_ref, f_ref, h_ref):
        s = a_ref[...] + b_ref[...]
        f_ref[...] = s
        h_ref[...] = s.astype(BF16)

    return pl.pallas_call(
        body, name=f"sum_cast_{r}x{c}",
        grid_spec=pltpu.PrefetchScalarGridSpec(
            num_scalar_prefetch=1, grid=(ns, r // tr),
            in_specs=[pl.BlockSpec((None, None, tr, c), lambda s, i, cc: (s, cc[0], i, 0)),
                      pl.BlockSpec((None, tr, c), lambda s, i, cc: (s, i, 0))],
            out_specs=[pl.BlockSpec((None, tr, c), lambda s, i, cc: (s, i, 0)),
                       pl.BlockSpec((None, tr, c), lambda s, i, cc: (s, i, 0))]),
        out_shape=[jax.ShapeDtypeStruct((ns, r, c), F32), jax.ShapeDtypeStruct((ns, r, c), BF16)],
        compiler_params=_cparams(("parallel", "parallel")),
    )(lax.axis_index("c").reshape(1), own, got)


def _sum4(mine, got):
    _, r, c = mine.shape
    tr = r // 4 if r % 64 == 0 else r

    def body(s_ref, a_ref, g_ref, o_ref):
        acc = a_ref[...] + g_ref[0].astype(F32)
        acc = acc + g_ref[1].astype(F32)
        o_ref[...] = acc + g_ref[2].astype(F32)

    shard = (2 * lax.axis_index("x") + lax.axis_index("y")).reshape(1)
    return pl.pallas_call(
        body, name=f"sum4_{r}x{c}",
        grid_spec=pltpu.PrefetchScalarGridSpec(
            num_scalar_prefetch=1, grid=(r // tr,),
            in_specs=[pl.BlockSpec((None, tr, c), lambda i, ss: (ss[0], i, 0)),
                      pl.BlockSpec((3, tr, c), lambda i, ss: (0, i, 0))],
            out_specs=pl.BlockSpec((tr, c), lambda i, ss: (i, 0))),
        out_shape=jax.ShapeDtypeStruct((r, c), F32),
        compiler_params=_cparams(("parallel",)),
    )(shard, mine, got)


def _place():
    x, y, c = lax.axis_index("x"), lax.axis_index("y"), lax.axis_index("c")
    chips = [(1 - x, y), (x, 1 - y), (1 - x, 1 - y)]
    return x, y, c, chips


def _gather_weights(parts):
    npart = len(parts)

    def body(*refs):
        ins, outs = refs[:npart], refs[npart:2 * npart]
        send_sems, recv_sems = refs[2 * npart:]
        x, y, c, chips = _place()
        sibling = (x, y, 1 - c)
        mine = 2 * x + y

        def remote(k, p, shard, half, to, src=None):
            dst = outs[p].at[shard, half]
            return pltpu.make_async_remote_copy(
                src_ref=dst if src is None else src, dst_ref=dst,
                send_sem=send_sems.at[k], recv_sem=recv_sems.at[k], device_id=to, device_id_type=MESH)

        first = [remote(j * npart + p, p, mine, c, (*chip, c), src=ins[p].at[c])
                 for j, chip in enumerate(chips) for p in range(npart)]
        for cp in first:
            cp.start()
        passed = []
        for j, (cx, cy) in enumerate(chips):
            for p in range(npart):
                remote(j * npart + p, p, 2 * cx + cy, c, (x, y, c)).wait_recv()
                fw = remote((3 + j) * npart + p, p, 2 * cx + cy, c, sibling)
                fw.start()
                passed.append(fw)
        for j, (cx, cy) in enumerate(chips):
            for p in range(npart):
                remote((3 + j) * npart + p, p, 2 * cx + cy, 1 - c, (x, y, c)).wait_recv()
        for cp in first + passed:
            cp.wait_send()

    got = pl.pallas_call(
        body, name="gather_weights",
        in_specs=[ANY] * npart, out_specs=[ANY] * npart,
        out_shape=[jax.ShapeDtypeStruct((NSHARD,) + a.shape, a.dtype) for a in parts],
        scratch_shapes=[pltpu.SemaphoreType.DMA((6 * npart,)), pltpu.SemaphoreType.DMA((6 * npart,))],
    )(*parts)
    mine = 2 * lax.axis_index("x") + lax.axis_index("y")
    return [lax.dynamic_update_index_in_dim(g, a, mine, 0) for g, a in zip(got, parts)]


def _swap_halves(parts):
    npart = len(parts)

    def body(*refs):
        ins, outs = refs[:npart], refs[npart:2 * npart]
        send_sems, recv_sems = refs[2 * npart:]
        x, y, c, _ = _place()
        cps = [pltpu.make_async_remote_copy(
            src_ref=ins[p].at[s, 1 - c], dst_ref=outs[p].at[s],
            send_sem=send_sems.at[s * npart + p], recv_sem=recv_sems.at[s * npart + p],
            device_id=(x, y, 1 - c), device_id_type=MESH) for s in range(NSHARD) for p in range(npart)]
        for cp in cps:
            cp.start()
        for cp in cps:
            cp.wait()

    return pl.pallas_call(
        body, name="swap_halves", in_specs=[ANY] * npart, out_specs=[ANY] * npart,
        out_shape=[jax.ShapeDtypeStruct((NSHARD,) + a.shape[2:], a.dtype) for a in parts],
        scratch_shapes=[pltpu.SemaphoreType.DMA((NSHARD * npart,)), pltpu.SemaphoreType.DMA((NSHARD * npart,))],
    )(*parts)


def _scatter_shards(parts):
    npart = len(parts)

    def body(*refs):
        ins, outs = refs[:npart], refs[npart:2 * npart]
        send_sems, recv_sems = refs[2 * npart:]
        x, y, c, chips = _place()
        cps = [pltpu.make_async_remote_copy(
            src_ref=ins[p].at[2 * cx + cy], dst_ref=outs[p].at[j],
            send_sem=send_sems.at[j * npart + p], recv_sem=recv_sems.at[j * npart + p],
            device_id=(cx, cy, c), device_id_type=MESH)
            for j, (cx, cy) in enumerate(chips) for p in range(npart)]
        for cp in cps:
            cp.start()
        for cp in cps:
            cp.wait()

    return pl.pallas_call(
        body, name="scatter_shards", in_specs=[ANY] * npart, out_specs=[ANY] * npart,
        out_shape=[jax.ShapeDtypeStruct((3,) + a.shape[1:], a.dtype) for a in parts],
        scratch_shapes=[pltpu.SemaphoreType.DMA((3 * npart,)), pltpu.SemaphoreType.DMA((3 * npart,))],
    )(*parts)


HBM = pl.BlockSpec(memory_space=pltpu.HBM)
SEM = pl.BlockSpec(memory_space=pltpu.SEMAPHORE)
EFFECT = pltpu.SideEffectType.DATAFLOW_SIDE_EFFECTING


def _scatter_copies(srcs, lands, send_sems, recv_sems, waiting):
    x, y, c, chips = _place()
    n = len(srcs)
    return [pltpu.make_async_remote_copy(
        src_ref=srcs[p].at[2 * cx + cy], dst_ref=lands[p].at[j],
        send_sem=send_sems.at[j * n + p], recv_sem=recv_sems.at[j * n + p],
        device_id=(cx, cy, c), device_id_type=MESH)
        for j, (cx, cy) in enumerate(chips) for p in range(n)]


def _proj_copies(srcs, lands, send_sems, recv_sems, waiting):
    x, y, c, chips = _place()
    mine = 2 * x + y
    return [pltpu.make_async_remote_copy(
        src_ref=srcs[0].at[c], dst_ref=lands[0].at[mine, c],
        send_sem=send_sems.at[2 * j + to], recv_sem=recv_sems.at[2 * j + (to if waiting else c)],
        device_id=(cx, cy, to), device_id_type=MESH)
        for j, (cx, cy) in enumerate(chips) for to in range(2)]


def _start_copies(copies, nsem, parts, lands, name, after=None):
    n = len(parts)
    extra = [] if after is None else [after]

    def body(*refs):
        outs = refs[2 * n + len(extra):]
        for cp in copies(refs[:n], refs[n:2 * n], outs[0], outs[1], False):
            cp.start()
        outs[-1][...] = jnp.zeros_like(outs[-1])

    res = pl.pallas_call(
        body, name=name,
        out_shape=(pltpu.SemaphoreType.DMA((nsem,)), pltpu.SemaphoreType.DMA((nsem,)),
                   *[pltpu.HBM(a.shape, a.dtype) for a in parts], *[pltpu.HBM(a.shape, a.dtype) for a in lands],
                   jax.ShapeDtypeStruct((8, 128), F32)),
        in_specs=[HBM] * (2 * n) + [ANY] * len(extra),
        out_specs=(SEM, SEM, *[HBM] * (2 * n), pl.BlockSpec(memory_space=pltpu.VMEM)),
        input_output_aliases={i: 2 + i for i in range(2 * n)},
        compiler_params=pltpu.CompilerParams(has_side_effects=EFFECT),
    )(*[pltpu.with_memory_space_constraint(a, pltpu.HBM) for a in parts],
      *[pltpu.with_memory_space_constraint(a, pltpu.HBM) for a in lands], *extra)
    return res[0], res[1], res[2:2 + n], res[2 + n:2 + 2 * n], res[-1]


def _wait_copies(copies, started, after, name):
    send_sems, recv_sems, srcs, lands, _ = started
    n = len(srcs)

    def body(*refs):
        for cp in copies(refs[:n], refs[n:2 * n], refs[2 * n], refs[2 * n + 1], True):
            cp.wait_send()
            cp.wait_recv()

    res = pl.pallas_call(
        body, name=name,
        out_shape=tuple(pltpu.HBM(a.shape, a.dtype) for a in (*srcs, *lands)),
        in_specs=[HBM] * (2 * n) + [SEM, SEM, ANY], out_specs=tuple([HBM] * (2 * n)),
        input_output_aliases={i: i for i in range(2 * n)},
        compiler_params=pltpu.CompilerParams(has_side_effects=EFFECT),
    )(*srcs, *lands, send_sems, recv_sems, after)
    return res[n:]


def _join_halves(parts):
    npart = len(parts)

    def body(*refs):
        ins, outs = refs[:npart], refs[npart:2 * npart]
        send_sems, recv_sems = refs[2 * npart:]
        x, y, c, _ = _place()
        cps = [pltpu.make_async_remote_copy(
            src_ref=ins[p], dst_ref=outs[p], send_sem=send_sems.at[p], recv_sem=recv_sems.at[p],
            device_id=(x, y, 1 - c), device_id_type=MESH) for p in range(npart)]
        for cp in cps:
            cp.start()
        for cp in cps:
            cp.wait()

    return pl.pallas_call(
        body, name="join_halves", in_specs=[ANY] * npart, out_specs=[ANY] * npart,
        out_shape=[jax.ShapeDtypeStruct(a.shape, a.dtype) for a in parts],
        scratch_shapes=[pltpu.SemaphoreType.DMA((npart,)), pltpu.SemaphoreType.DMA((npart,))],
    )(*parts)


def _allreduce_small(v):
    r, ncol = v.shape

    def body(v_ref, o_ref, buf, send_sems, recv_sems):
        x, y, c, _ = _place()
        me = 4 * x + 2 * y + c
        buf[me] = v_ref[...]
        cps = []
        for k in range(1, 8):
            px, py, pc = x ^ (k >> 2), y ^ ((k >> 1) & 1), c ^ (k & 1)
            cps.append(pltpu.make_async_remote_copy(
                src_ref=v_ref, dst_ref=buf.at[me], send_sem=send_sems.at[k - 1], recv_sem=recv_sems.at[k - 1],
                device_id=(px, py, pc), device_id_type=MESH))
        for cp in cps:
            cp.start()
        for k in range(1, 8):
            px, py, pc = x ^ (k >> 2), y ^ ((k >> 1) & 1), c ^ (k & 1)
            pltpu.make_async_remote_copy(
                src_ref=v_ref, dst_ref=buf.at[4 * px + 2 * py + pc], send_sem=send_sems.at[k - 1],
                recv_sem=recv_sems.at[k - 1], device_id=(px, py, pc), device_id_type=MESH).wait_recv()
        for cp in cps:
            cp.wait_send()
        acc = buf[0]
        for d in range(1, 8):
            acc = acc + buf[d]
        o_ref[...] = acc

    return pl.pallas_call(
        body, name="allreduce_small",
        in_specs=[pl.BlockSpec(memory_space=pltpu.VMEM)], out_specs=pl.BlockSpec(memory_space=pltpu.VMEM),
        out_shape=jax.ShapeDtypeStruct((r, ncol), F32),
        scratch_shapes=[pltpu.VMEM((8, r, ncol), F32), pltpu.SemaphoreType.DMA((7,)), pltpu.SemaphoreType.DMA((7,))],
    )(v)


def _permute_rows(shards):
    w0, w1, w2, w3 = shards
    zeros = jnp.zeros((NPERM - 9280, w0.shape[1]), w0.dtype)
    return jnp.concatenate([w0, w1[0:1776], w1[1808:2320], w2, w3[0:240], w3[272:2320],
                            w1[1776:1808], w3[240:272], zeros], axis=0)


def _unpermute_rows(g):
    s1 = jnp.concatenate([g[2320:4096], g[9216:9248], g[4096:4608]], axis=0)
    s3 = jnp.concatenate([g[6928:7168], g[9248:9280], g[7168:9216]], axis=0)
    return jnp.stack([g[0:2320], s1, g[4608:6928], s3], axis=0)


def _pack_shard_small(conv, w2f, w2b):
    top = jnp.pad(conv, ((0, 8 - conv.shape[0]), (0, 0)))
    mid = jnp.pad(jnp.concatenate([w2f, w2b], axis=1), ((0, 0), (0, 768 - 256)))
    return jnp.concatenate([top, mid, jnp.zeros((8, 768), conv.dtype)], axis=0)


def _unpack_shard_small(a):
    return a[0:5], a[8:24, 0:128], a[8:24, 128:256]


def kernel(x, ln_pre_w, w_in, conv_w, a_log_fwd, a_log_bwd, dt_bias_fwd, dt_bias_bwd, gdn_norm_w, w_proj_gdn, gk_w2_fwd, gk_b2_fwd, gk_w2_bwd, gk_b2_bwd, gla_norm_w, w_proj_gla, w_out, ln_post_w, loss_target, m_ln_pre_w, m_w_in, m_conv_w, m_a_log_fwd, m_a_log_bwd, m_dt_bias_fwd, m_dt_bias_bwd, m_gdn_norm_w, m_w_proj_gdn, m_gk_w2_fwd, m_gk_b2_fwd, m_gk_w2_bwd, m_gk_b2_bwd, m_gla_norm_w, m_w_proj_gla, m_w_out, m_ln_post_w, v_ln_pre_w, v_w_in, v_conv_w, v_a_log_fwd, v_a_log_bwd, v_dt_bias_fwd, v_dt_bias_bwd, v_gdn_norm_w, v_w_proj_gdn, v_gk_w2_fwd, v_gk_b2_fwd, v_gk_w2_bwd, v_gk_b2_bwd, v_gla_norm_w, v_w_proj_gla, v_w_out, v_ln_post_w):
    t = x.shape[1]
    x2, tgt = x[0], loss_target[0]

    win_l = w_in[0].T.astype(BF16).reshape(2, SHW // 2, D)
    proj_l = jnp.concatenate([w_proj_gdn[0], w_proj_gla[0], w_out[0]], axis=0).astype(BF16).reshape(2, 384, D)
    small_l = _pack_shard_small(conv_w[0], gk_w2_fwd[0], gk_w2_bwd[0]).reshape(2, 16, 768)
    win_g, small_g = _gather_weights([win_l, small_l])
    proj_started = _start_copies(_proj_copies, 6, [proj_l], [lax.empty((NSHARD, 2, 384, D), BF16)],
                                 "gather_proj_start", after=small_g)
    wperm = _permute_rows(win_g.reshape(NSHARD, SHW, D))
    small_g = small_g.reshape(NSHARD, 32, 768)
    convw = small_g[:, 0:8, :].transpose(1, 0, 2).reshape(8, 3 * D)
    w2f = small_g[:, 8:24, 0:128].transpose(1, 0, 2).reshape(16, 512)
    w2b = small_g[:, 8:24, 128:256].transpose(1, 0, 2).reshape(16, 512)
    w2f_pad = jnp.pad(w2f, ((32, 80), (0, 0)))
    w2b_pad = jnp.pad(w2b, ((48, 64), (0, 0)))
    alog_row = jnp.pad(jnp.concatenate([a_log_fwd, a_log_bwd], axis=1), ((0, 0), (0, 112)))
    dt_row = jnp.pad(jnp.concatenate([dt_bias_fwd, dt_bias_bwd], axis=1), ((0, 0), (0, 112)))

    p, h = _inproj(x2, ln_pre_w + proj_started[4][0:1, 0:1], wperm)
    qn, kn, vc = (_qkv_fwd(p, convw, kind) for kind in range(3))
    gsm, gk = _gates_fwd(p, alog_row, dt_row, w2f_pad, gk_b2_fwd, w2b_pad, gk_b2_bwd)
    g2, b2 = _gcum_fwd(gsm)
    u, w, at, qd, kd, el, tinv = _gdn_intra_fwd(qn, kn, vc, g2, b2)
    oa2, sa = _gdn_scan_fwd(u, w, at, qd, kd, el)
    qg, kdb, intra, elb = _gla_intra_fwd(p, gk)
    ob2, sb = _gla_scan_fwd(p, qg, kdb, intra, elb)

    (proj_land,) = _wait_copies(_proj_copies, proj_started, ob2, "gather_proj_wait")
    mine = 2 * lax.axis_index("x") + lax.axis_index("y")
    w3 = lax.dynamic_update_index_in_dim(proj_land, proj_l, mine, 0).reshape(NSHARD, 3, D // NSHARD, D)
    (loss8, doa, dob, dz, dgb, dga, dgB, dyres, dw3, dgdn_w, dgla_w, dlnpost) = _post(
        oa2, ob2, p, x2, tgt, gdn_norm_w, gla_norm_w, ln_post_w, w3)

    du, dw, dat, dqd, dkd, del_ = _gdn_scan_bwd(u, w, at, qd, kd, el, sa, doa)
    dqn, dkn, dvc, dg2, db2 = _gdn_intra_bwd(qn, kn, vc, g2, b2, tinv, du, dw, dat, dqd, dkd, del_)
    dgsm = _gcum_bwd(gsm, dg2, db2)
    dqg, dkdb, dvs, delb = _gla_scan_bwd(p, qg, kdb, elb, sb, dob)
    dqb, dkb, dvb, dgk = _gla_intra_bwd(p, gk, dqg, dkdb, dvs, delb, dob)
    (dps, dalog8, ddt8, dw2f_pad, db2f8, dw2b_pad, db2b8) = _gates_bwd(
        p, alog_row, dt_row, w2f_pad, gk_b2_fwd, w2b_pad, gk_b2_bwd, dgsm, dgk)
    dpre, dconv = zip(*[_qkv_bwd(p, convw, g, kind) for kind, g in enumerate((dqn, dkn, dvc))])

    pieces = (jnp.concatenate([a.astype(BF16) for a in (*dpre, dz, dqb, dkb, dvb, dgb, dga, dgB, dps)], axis=1),)
    dwperm = _inproj_dw(h, pieces)

    g_in = _unpermute_rows(dwperm).reshape(NSHARD, 2, SHW // 2, D)
    g_proj = dw3.reshape(NSHARD, 2, 384, D)
    dconv_full = jnp.concatenate(dconv, axis=1)
    dw2f, dw2b = dw2f_pad[32:48], dw2b_pad[48:64]
    g_small = jnp.stack([_pack_shard_small(dconv_full[0:5, 768 * s:768 * (s + 1)],
                                           dw2f[:, 128 * s:128 * (s + 1)], dw2b[:, 128 * s:128 * (s + 1)])
                         for s in range(NSHARD)])
    g_small = g_small.reshape(NSHARD, 2, 16, 768)
    parts = [g_in, g_proj, g_small]
    got = _swap_halves(parts)
    sums = [_sum_cast(a, b) for a, b in zip(parts, got)]
    hbs = [hb for _, hb in sums]
    started = _start_copies(_scatter_copies, 3 * len(hbs), hbs,
                            [lax.empty((3,) + a.shape[1:], a.dtype) for a in hbs], "scatter_start")
    dx, dlnpre8 = _inproj_dx(pieces, wperm, x2, ln_pre_w + started[4][0:1, 0:1], dyres)

    gsmall = _allreduce_small(jnp.concatenate(
        [dlnpre8, dalog8, ddt8, dgdn_w, db2f8, db2b8, dgla_w, dlnpost], axis=1))
    smalls = dict(ln_pre_w=(ln_pre_w, m_ln_pre_w, v_ln_pre_w), a_log_fwd=(a_log_fwd, m_a_log_fwd, v_a_log_fwd),
                  a_log_bwd=(a_log_bwd, m_a_log_bwd, v_a_log_bwd),
                  dt_bias_fwd=(dt_bias_fwd, m_dt_bias_fwd, v_dt_bias_fwd),
                  dt_bias_bwd=(dt_bias_bwd, m_dt_bias_bwd, v_dt_bias_bwd),
                  gdn_norm_w=(gdn_norm_w, m_gdn_norm_w, v_gdn_norm_w),
                  gk_b2_fwd=(gk_b2_fwd, m_gk_b2_fwd, v_gk_b2_fwd), gk_b2_bwd=(gk_b2_bwd, m_gk_b2_bwd, v_gk_b2_bwd),
                  gla_norm_w=(gla_norm_w, m_gla_norm_w, v_gla_norm_w), ln_post_w=(ln_post_w, m_ln_post_w, v_ln_post_w))
    names = [name for name, _, _, _ in SMALL_SLOTS]
    small = _adam_small(gsmall, *([smalls[n][i] for n in names] for i in range(3)))

    landed = _wait_copies(_scatter_copies, started, small["ln_pre_w"][1], "scatter_wait")
    halves = [_sum4(f, g) for (f, _), g in zip(sums, landed)]
    theirs = _join_halves(halves)

    a_in = [a.T for a in _adam(w_in[0].T, halves[0], theirs[0], m_w_in[0].T, v_w_in[0].T, 232, name="in")]
    a_pr = [_adam(w[0], halves[1], theirs[1], m[0], v[0], 128, tile0=2 * i, name=f"proj{i}")
            for i, (w, m, v) in enumerate(((w_proj_gdn, m_w_proj_gdn, v_w_proj_gdn),
                                           (w_proj_gla, m_w_proj_gla, v_w_proj_gla), (w_out, m_w_out, v_w_out)))]
    a_ss = _adam(_pack_shard_small(conv_w[0], gk_w2_fwd[0], gk_w2_bwd[0]), halves[2], theirs[2],
                 _pack_shard_small(m_conv_w[0], m_gk_w2_fwd[0], m_gk_w2_bwd[0]),
                 _pack_shard_small(v_conv_w[0], v_gk_w2_fwd[0], v_gk_w2_bwd[0]), 16, name="small")

    def family(k):
        conv, w2f_, w2b_ = _unpack_shard_small(a_ss[k])
        s = {n: small[n][k] for n in names}
        return [s["ln_pre_w"], a_in[k][None], conv[None], s["a_log_fwd"], s["a_log_bwd"], s["dt_bias_fwd"],
                s["dt_bias_bwd"], s["gdn_norm_w"], a_pr[0][k][None], w2f_[None], s["gk_b2_fwd"], w2b_[None],
                s["gk_b2_bwd"], s["gla_norm_w"], a_pr[1][k][None], a_pr[2][k][None], s["ln_post_w"]]

    loss = lax.psum(loss8[0, 0], ("x", "y", "c"))
    return (loss, dx[None], *family(0), *family(1), *family(2), *family(3))
```

```python
import functools

import jax
import jax.numpy as jnp
from jax import lax
from jax.experimental import pallas as pl
from jax.experimental.pallas import tpu as pltpu

F32 = jnp.float32
BF16 = jnp.bfloat16
HI = lax.Precision.HIGHEST
MESH = pl.DeviceIdType.MESH

D = 1024
CH = 64
EPS = 1e-6
NA, DA = 8, 128
NB, DKB, DVB = 4, 128, 256
NSHARD = 4
SHW = 2320
NPERM = 9728
PS_BLOCK = 72
LR, B1, B2, ADAM_EPS, WD, STEP = 0.001, 0.9, 0.999, 1e-08, 0.01, 10

ANY = pl.BlockSpec(memory_space=pl.ANY)


def _cparams(sem=None, vmem_mb=48):
    return pltpu.CompilerParams(dimension_semantics=sem, vmem_limit_bytes=vmem_mb << 20)


def _bdot(a, b, ca, cb):
    return lax.dot_general(a.astype(BF16), b.astype(BF16), (((ca,), (cb,)), ((), ())),
                           preferred_element_type=F32)


@jax.custom_vjp
def mm(a, b):
    return _bdot(a, b, 1, 0)


def _mm_fwd(a, b):
    return _bdot(a, b, 1, 0), (a, b)


def _mm_bwd(res, g):
    a, b = res
    return _bdot(g, b, 1, 1), _bdot(a, g, 0, 0)


mm.defvjp(_mm_fwd, _mm_bwd)


@jax.custom_vjp
def mm_nt(a, b):
    return _bdot(a, b, 1, 1)


def _mm_nt_fwd(a, b):
    return _bdot(a, b, 1, 1), (a, b)


def _mm_nt_bwd(res, g):
    a, b = res
    return _bdot(g, b, 1, 0), _bdot(g, a, 0, 0)


mm_nt.defvjp(_mm_nt_fwd, _mm_nt_bwd)


@jax.custom_vjp
def mm_tn(a, b):
    return _bdot(a, b, 0, 0)


def _mm_tn_fwd(a, b):
    return _bdot(a, b, 0, 0), (a, b)


def _mm_tn_bwd(res, g):
    a, b = res
    return _bdot(b, g, 1, 1), _bdot(a, g, 1, 0)


mm_tn.defvjp(_mm_tn_fwd, _mm_tn_bwd)


def dot_hi(a, b):
    return lax.dot_general(a, b, (((1,), (0,)), ((), ())), precision=HI, preferred_element_type=F32)


def _split3(x):
    x1 = x.astype(BF16)
    r = x - x1.astype(F32)
    x2 = r.astype(BF16)
    return x1, x2, (r - x2.astype(F32)).astype(BF16)


def _cdot(c, x, cc, cx, c_first=True):
    parts = _split3(x)
    if c_first:
        return _bdot(c, parts[0], cc, cx) + _bdot(c, parts[1], cc, cx) + _bdot(c, parts[2], cc, cx)
    return _bdot(parts[0], c, cx, cc) + _bdot(parts[1], c, cx, cc) + _bdot(parts[2], c, cx, cc)


@jax.custom_vjp
def cmm(c, x):
    return _cdot(c, x, 1, 0)


def _cmm_fwd(c, x):
    return _cdot(c, x, 1, 0), c


def _cmm_bwd(c, g):
    return jnp.zeros_like(c), _cdot(c, g, 0, 0)


cmm.defvjp(_cmm_fwd, _cmm_bwd)


@jax.custom_vjp
def mmc(x, c):
    return _cdot(c, x, 0, 1, c_first=False)


def _mmc_fwd(x, c):
    return _cdot(c, x, 0, 1, c_first=False), c


def _mmc_bwd(c, g):
    return _cdot(c, g, 1, 1, c_first=False), jnp.zeros_like(c)


mmc.defvjp(_mmc_fwd, _mmc_bwd)


def _sigmoid(x):
    return 1.0 / (1.0 + jnp.exp(-x))


def _silu(x):
    return x * _sigmoid(x)


def _softplus(x):
    return jnp.maximum(x, 0.0) + jnp.log(1.0 + jnp.exp(-jnp.abs(x)))


def _rms(x, w):
    return x * lax.rsqrt(jnp.mean(x * x, axis=-1, keepdims=True) + EPS) * w


SC = 256


class _Consts:
    def __init__(self, rev):
        r = lax.broadcasted_iota(jnp.int32, (SC, SC), 0)
        c = lax.broadcasted_iota(jnp.int32, (SC, SC), 1)
        same = (r >> 6) == (c >> 6)
        a = jnp.where(rev, c, r)
        b = jnp.where(rev, r, c)
        self.incl = same & (a >= b)
        self.strict = same & (a > b)
        self.incl_f = self.incl.astype(F32)
        self.eye = (r == c).astype(F32)
        rows = lax.broadcasted_iota(jnp.int32, (SC, 1), 0)
        self.last_col = ((rows & (CH - 1)) == jnp.where(rev, 0, CH - 1)).astype(F32)
        rr = lax.broadcasted_iota(jnp.int32, (SC, CH), 0)
        cc = lax.broadcasted_iota(jnp.int32, (SC, CH), 1)
        self.fold = ((rr & (CH - 1)) == cc).astype(F32)


def _dot3(a, b, ca=1, cb=0):
    ah, bh = a.astype(BF16), b.astype(BF16)
    al, bl = (a - ah.astype(F32)).astype(BF16), (b - bh.astype(F32)).astype(BF16)
    return _bdot(ah, bh, ca, cb) + (_bdot(ah, bl, ca, cb) + _bdot(al, bh, ca, cb))


TRI_SPLIT_LEVELS = 2


def _tri_inv(low, eye):
    n = -low
    acc = eye + n
    p = n
    for level in range(5):
        dot = _dot3 if level < TRI_SPLIT_LEVELS else (lambda a, b: _bdot(a, b, 1, 0))
        p = dot(p, p)
        acc = acc + dot(acc, p)
    return acc


@jax.custom_vjp
def _solve2(low, rv, rk, tinv):
    x = _dot3(tinv, jnp.concatenate([rv, rk], axis=1))
    return x[:, :DA], x[:, DA:]


def _solve2_fwd(low, rv, rk, tinv):
    x = _dot3(tinv, jnp.concatenate([rv, rk], axis=1))
    return (x[:, :DA], x[:, DA:]), (x, tinv)


def _solve2_bwd(res, g):
    x, tinv = res
    drhs = _dot3(tinv, jnp.concatenate(g, axis=1), 0, 0)
    return -_dot3(drhs, x, 1, 1), drhs[:, :DA], drhs[:, DA:], jnp.zeros_like(tinv)


_solve2.defvjp(_solve2_fwd, _solve2_bwd)


def _chunk_last(x, cs):
    xs = (x * cs.last_col).reshape(SC // CH, CH, x.shape[1])
    return jnp.broadcast_to(jnp.sum(xs, axis=1, keepdims=True), xs.shape).reshape(x.shape)


def _gdn_decay(g, cs):
    gw = jnp.concatenate([g] * (SC // DA), axis=1)
    grow = jnp.sum(cs.eye * gw, axis=0, keepdims=True)
    return jnp.where(cs.incl, jnp.exp(jnp.where(cs.incl, gw - grow, 0.0)), 0.0)


def _gdn_intra(q, k, v, g, bx, tinv, cs):
    decay = _gdn_decay(g, cs)
    kb = k * bx
    low = jnp.where(cs.strict, mm_nt(kb, k) * decay, 0.0)
    eg = jnp.exp(g)
    made = tinv is None
    if made:
        tinv = _tri_inv(low, cs.eye)
    u, w = _solve2(low, v * bx, kb * eg, tinv)
    attn = mmc(mm_nt(q, k) * decay, cs.fold)
    qd = q * eg
    glast = _chunk_last(g, cs)
    kd = k * jnp.exp(glast - g)
    outs = (u, w, attn, qd, kd, jnp.exp(glast))
    return outs + (tinv,) if made else outs


def _gdn_scan(u, w, attn, qd, kd, el, s):
    vn = u - mm(w, s)
    o = mm(qd, s) + mm(attn, vn)
    sn = s * el + mm_tn(kd, vn)
    return o, sn


def _gla_intra(q, k, v, gk, cs):
    gc = cmm(cs.incl_f, gk)
    qg = q * (DKB ** -0.5) * jnp.exp(gc)
    kg = k * jnp.exp(-gc)
    attn = jnp.where(cs.incl, mm_nt(qg, kg), 0.0)
    intra = mm(attn, v)
    glast = _chunk_last(gc, cs)
    kd = k * jnp.exp(glast - gc)
    return qg, kd, intra, jnp.exp(glast)


def _gla_scan(qg, kd, v, el, st):
    o = mm_nt(qg, st)
    stn = st * el + mm_tn(v, kd)
    return o, stn


def _shift_rows(x, s):
    if s == 0:
        return x
    t = x.shape[0]
    rolled = pltpu.roll(x, (-s) % t, 0)
    rows = lax.broadcasted_iota(jnp.int32, x.shape, 0)
    return jnp.where((rows + s >= 0) & (rows + s < t), rolled, 0.0)


@jax.custom_vjp
def _conv5(x, w):
    acc = w[0:1] * _shift_rows(x, -2)
    for j in range(1, 5):
        acc = acc + w[j:j + 1] * _shift_rows(x, j - 2)
    return acc


def _conv5_fwd(x, w):
    return _conv5(x, w), (x, w)


def _conv5_bwd(res, g):
    x, w = res
    dx = w[0:1] * _shift_rows(g, 2)
    for j in range(1, 5):
        dx = dx + w[j:j + 1] * _shift_rows(g, 2 - j)
    rows = lax.broadcasted_iota(jnp.int32, w.shape, 0)
    dw = jnp.zeros_like(w)
    for j in range(5):
        dwj = jnp.sum(g * _shift_rows(x, j - 2), axis=0, keepdims=True)
        dw = dw + jnp.where(rows == j, dwj, 0.0)
    return dx, dw


_conv5.defvjp(_conv5_fwd, _conv5_bwd)


def _qkv_act(kind):
    def f(x, w):
        c = _silu(_conv5(x, w))
        if kind == 2:
            return c
        c = c * lax.rsqrt(jnp.sum(c * c, axis=-1, keepdims=True) + EPS)
        return c * (DA ** -0.5) if kind == 0 else c
    return f


def _inproj(x, lnw, wperm, tn=512):
    t = x.shape[0]
    tm = min(t, 1024)

    def body(x_ref, lnw_ref, w_ref, p_ref, h_ref, hbuf):
        @pl.when(pl.program_id(1) == 0)
        def _():
            hb = _rms(x_ref[...], lnw_ref[...]).astype(BF16)
            hbuf[...] = hb
            h_ref[...] = hb
        p_ref[...] = _bdot(hbuf[...], w_ref[...], 1, 1)

    return pl.pallas_call(
        body, name="inproj", grid=(t // tm, NPERM // tn),
        in_specs=[pl.BlockSpec((tm, D), lambda i, j: (i, 0)),
                  pl.BlockSpec((1, D), lambda i, j: (0, 0)),
                  pl.BlockSpec((tn, D), lambda i, j: (j, 0))],
        out_specs=[pl.BlockSpec((tm, tn), lambda i, j: (i, j)),
                   pl.BlockSpec((tm, D), lambda i, j: (i, 0))],
        out_shape=[jax.ShapeDtypeStruct((t, NPERM), F32),
                   jax.ShapeDtypeStruct((t, D), BF16)],
        scratch_shapes=[pltpu.VMEM((tm, D), BF16)],
        compiler_params=_cparams(("parallel", "arbitrary")),
    )(x, lnw, wperm)


DP_TILE = 512
DP_PIECES = ((0, 19),)


def _piece_specs(tm, j_first):
    specs = []
    for j0, n in DP_PIECES:
        def imap(a, b, j0=j0, n=n):
            j, i = (a, b) if j_first else (b, a)
            inside = (j >= j0) & (j < j0 + n)
            return jnp.where(inside, i, 0), jnp.clip(j - j0, 0, n - 1)
        specs.append(pl.BlockSpec((tm, DP_TILE), imap))
    return specs


def _for_piece(j, refs, fn):
    for (j0, n), ref in zip(DP_PIECES, refs):
        @pl.when((j >= j0) & (j < j0 + n))
        def _(ref=ref):
            fn(ref[...])


def _inproj_dw(h, pieces):
    t = h.shape[0]
    tm = min(t, 1024)
    npc = len(pieces)

    def body(h_ref, *refs):
        dw_ref = refs[npc]

        @pl.when(pl.program_id(1) == 0)
        def _():
            dw_ref[...] = jnp.zeros_like(dw_ref)

        def add(dp):
            dw_ref[...] += _bdot(dp, h_ref[...], 0, 0)
        _for_piece(pl.program_id(0), refs[:npc], add)

    return pl.pallas_call(
        body, name="inproj_dw", grid=(NPERM // DP_TILE, t // tm),
        in_specs=[pl.BlockSpec((tm, D), lambda j, i: (i, 0))] + _piece_specs(tm, True),
        out_specs=pl.BlockSpec((DP_TILE, D), lambda j, i: (j, 0)),
        out_shape=jax.ShapeDtypeStruct((NPERM, D), F32),
        compiler_params=_cparams(("parallel", "arbitrary")),
    )(h, *pieces)


def _inproj_dx(pieces, wperm, x, lnw, dyres):
    t = x.shape[0]
    tm = min(t, 1024)
    tn = DP_TILE
    nj = NPERM // tn
    npc = len(pieces)

    def body(*refs):
        w_ref, x_ref, lnw_ref, dy_ref, dx_ref, dlnw_ref, acc = refs[npc:]
        j = pl.program_id(1)

        @pl.when(j == 0)
        def _():
            acc[...] = jnp.zeros_like(acc)

        def add(dp):
            acc[...] += _bdot(dp, w_ref[...], 1, 0)
        _for_piece(j, refs[:npc], add)

        @pl.when(j == nj - 1)
        def _():
            _, vjp = jax.vjp(_rms, x_ref[...], lnw_ref[...])
            dx, dlnw = vjp(acc[...])
            dx_ref[...] = dx + dy_ref[...]

            @pl.when(pl.program_id(0) == 0)
            def _():
                dlnw_ref[...] = jnp.zeros_like(dlnw_ref)
            dlnw_ref[...] += jnp.broadcast_to(dlnw, dlnw_ref.shape)

    return pl.pallas_call(
        body, name="inproj_dx", grid=(t // tm, nj),
        in_specs=_piece_specs(tm, False) + [
                  pl.BlockSpec((tn, D), lambda i, j: (j, 0)),
                  pl.BlockSpec((tm, D), lambda i, j: (i, 0)),
                  pl.BlockSpec((1, D), lambda i, j: (0, 0)),
                  pl.BlockSpec((tm, D), lambda i, j: (i, 0))],
        out_specs=[pl.BlockSpec((tm, D), lambda i, j: (i, 0)),
                   pl.BlockSpec((8, D), lambda i, j: (0, 0))],
        out_shape=[jax.ShapeDtypeStruct((t, D), F32), jax.ShapeDtypeStruct((8, D), F32)],
        scratch_shapes=[pltpu.VMEM((tm, D), F32)],
        compiler_params=_cparams(("arbitrary", "arbitrary")),
    )(*pieces, wperm, x, lnw, dyres)


def _qkv_fwd(p, convw, kind):
    t = p.shape[0]
    f = _qkv_act(kind)

    def body(p_ref, w_ref, o_ref):
        o_ref[...] = f(p_ref[...], w_ref[...])

    return pl.pallas_call(
        body, name=f"qkv_fwd{kind}", grid=(NA,),
        in_specs=[pl.BlockSpec((t, DA), lambda h: (0, kind * NA + h)),
                  pl.BlockSpec((8, DA), lambda h: (0, kind * NA + h))],
        out_specs=pl.BlockSpec((t, DA), lambda h: (0, h)),
        out_shape=jax.ShapeDtypeStruct((t, D), F32),
        compiler_params=_cparams(("parallel",)),
    )(p, convw)


def _qkv_bwd(p, convw, dout, kind):
    t = p.shape[0]
    f = _qkv_act(kind)

    def body(p_ref, w_ref, g_ref, dx_ref, dw_ref):
        _, vjp = jax.vjp(f, p_ref[...], w_ref[...])
        dx, dw = vjp(g_ref[...])
        dx_ref[...] = dx.astype(BF16)
        dw_ref[...] = dw

    return pl.pallas_call(
        body, name=f"qkv_bwd{kind}", grid=(NA,),
        in_specs=[pl.BlockSpec((t, DA), lambda h: (0, kind * NA + h)),
                  pl.BlockSpec((8, DA), lambda h: (0, kind * NA + h)),
                  pl.BlockSpec((t, DA), lambda h: (0, h))],
        out_specs=[pl.BlockSpec((t, DA), lambda h: (0, h)),
                   pl.BlockSpec((8, DA), lambda h: (0, h))],
        out_shape=[jax.ShapeDtypeStruct((t, D), BF16), jax.ShapeDtypeStruct((8, D), F32)],
        compiler_params=_cparams(("parallel",)),
    )(p, convw, dout)


def _gates_f(ps, alog_row, dt_row, w2f, b2f, w2b, b2b):
    lane = lax.broadcasted_iota(jnp.int32, ps.shape, 1)
    lg = -jnp.exp(alog_row) * _softplus(ps + dt_row)
    gsm = jnp.where(lane < 16, lg, jnp.where(lane < 32, _sigmoid(ps), 0.0))
    gkf = -_softplus(-(mm(ps, w2f) + b2f)) * (1.0 / 16.0)
    gkb = -_softplus(-(mm(ps, w2b) + b2b)) * (1.0 / 16.0)
    return gsm, gkf, gkb


def _gates_fwd(ps, alog_row, dt_row, w2f, b2f, w2b, b2b, tm=512):
    t = ps.shape[0]

    def body(ps_ref, a_ref, d_ref, wf_ref, bf_ref, wb_ref, bb_ref, gsm_ref, gk_ref):
        gsm, gkf, gkb = _gates_f(ps_ref[...], a_ref[...], d_ref[...], wf_ref[...], bf_ref[...],
                                 wb_ref[...], bb_ref[...])
        gsm_ref[...] = gsm
        gk_ref[0] = gkf
        gk_ref[1] = gkb

    row = lambda n: pl.BlockSpec((1, n), lambda i: (0, 0))
    mat = pl.BlockSpec((128, 512), lambda i: (0, 0))
    return pl.pallas_call(
        body, name="gates_fwd", grid=(t // tm,),
        in_specs=[pl.BlockSpec((tm, 128), lambda i: (i, PS_BLOCK)), row(128), row(128), mat, row(512), mat, row(512)],
        out_specs=[pl.BlockSpec((tm, 128), lambda i: (i, 0)),
                   pl.BlockSpec((2, tm, 512), lambda i: (0, i, 0))],
        out_shape=[jax.ShapeDtypeStruct((t, 128), F32), jax.ShapeDtypeStruct((2, t, 512), F32)],
        compiler_params=_cparams(("parallel",)),
    )(ps, alog_row, dt_row, w2f, b2f, w2b, b2b)


def _gates_bwd(ps, alog_row, dt_row, w2f, b2f, w2b, b2b, dgsm, dgk, tm=512):
    t = ps.shape[0]

    def body(ps_ref, a_ref, d_ref, wf_ref, bf_ref, wb_ref, bb_ref, dgsm_ref, dgk_ref,
             dps_ref, da_ref, dd_ref, dwf_ref, dbf_ref, dwb_ref, dbb_ref):
        _, vjp = jax.vjp(_gates_f, ps_ref[...], a_ref[...], d_ref[...], wf_ref[...], bf_ref[...],
                         wb_ref[...], bb_ref[...])
        dps, da, dd, dwf, dbf, dwb, dbb = vjp((dgsm_ref[...], dgk_ref[0], dgk_ref[1]))
        dps_ref[:, 0:128] = dps.astype(BF16)
        dps_ref[:, 128:DP_TILE] = jnp.zeros((tm, DP_TILE - 128), BF16)
        accs = ((da_ref, da), (dd_ref, dd), (dwf_ref, dwf), (dbf_ref, dbf), (dwb_ref, dwb), (dbb_ref, dbb))

        @pl.when(pl.program_id(0) == 0)
        def _():
            for ref, _ in accs:
                ref[...] = jnp.zeros_like(ref)
        for ref, val in accs:
            ref[...] += jnp.broadcast_to(val, ref.shape)

    row = lambda n: pl.BlockSpec((1, n), lambda i: (0, 0))
    row8 = lambda n: pl.BlockSpec((8, n), lambda i: (0, 0))
    mat = pl.BlockSpec((128, 512), lambda i: (0, 0))
    return pl.pallas_call(
        body, name="gates_bwd", grid=(t // tm,),
        in_specs=[pl.BlockSpec((tm, 128), lambda i: (i, PS_BLOCK)), row(128), row(128), mat, row(512), mat, row(512),
                  pl.BlockSpec((tm, 128), lambda i: (i, 0)),
                  pl.BlockSpec((2, tm, 512), lambda i: (0, i, 0))],
        out_specs=[pl.BlockSpec((tm, DP_TILE), lambda i: (i, 0)), row8(128), row8(128), mat, row8(512), mat,
                   row8(512)],
        out_shape=[jax.ShapeDtypeStruct((t, DP_TILE), BF16),
                   jax.ShapeDtypeStruct((8, 128), F32), jax.ShapeDtypeStruct((8, 128), F32),
                   jax.ShapeDtypeStruct((128, 512), F32), jax.ShapeDtypeStruct((8, 512), F32),
                   jax.ShapeDtypeStruct((128, 512), F32), jax.ShapeDtypeStruct((8, 512), F32)],
        compiler_params=_cparams(("arbitrary",)),
    )(ps, alog_row, dt_row, w2f, b2f, w2b, b2b, dgsm, dgk)


def _rows(i):
    return pl.ds(pl.multiple_of(i * CH, CH), CH)


def _srows(i):
    return pl.ds(pl.multiple_of(i * SC, SC), SC)


def _first_row(x):
    row = lax.broadcasted_iota(jnp.int32, (8, x.shape[1]), 0)
    return jnp.where(row == 0, jnp.broadcast_to(x, (8, x.shape[1])), 0.0)


def _chunk_rows(e_ref, i):
    pad = jnp.zeros((CH - 8, 128), F32)
    return jnp.concatenate([x for c in range(SC // CH) for x in (e_ref[(SC // CH) * i + c], pad)], axis=0)


def _gcum_f(gsm, tm):
    i = lax.broadcasted_iota(jnp.int32, (tm, tm), 0)
    j = lax.broadcasted_iota(jnp.int32, (tm, tm), 1)
    same = (i >> 6) == (j >> 6)
    lower = (same & (i >= j)).astype(F32)
    upper = (same & (i <= j)).astype(F32)
    r = lax.broadcasted_iota(jnp.int32, (128, D), 0)
    head = lax.broadcasted_iota(jnp.int32, (128, D), 1) >> 7
    pick = lambda off: (r == head + off).astype(F32)
    lane = lax.broadcasted_iota(jnp.int32, gsm.shape, 1)
    run = jnp.where(lane < 8, cmm(lower, gsm), cmm(upper, gsm))
    return mmc(run, pick(0)), mmc(run, pick(8)), mmc(gsm, pick(16)), mmc(gsm, pick(24))


def _gcum_fwd(gsm, tm=256):
    t = gsm.shape[0]

    def body(s_ref, g_ref, b_ref):
        gf, gb, bf, bb = _gcum_f(s_ref[...], tm)
        g_ref[0] = gf
        g_ref[1] = gb
        b_ref[0] = bf
        b_ref[1] = bb

    two = pl.BlockSpec((2, tm, D), lambda i: (0, i, 0))
    return pl.pallas_call(
        body, name="gcum_fwd", grid=(t // tm,),
        in_specs=[pl.BlockSpec((tm, 128), lambda i: (i, 0))], out_specs=[two, two],
        out_shape=[jax.ShapeDtypeStruct((2, t, D), F32)] * 2,
        compiler_params=_cparams(("parallel",)),
    )(gsm)


def _gcum_bwd(gsm, dg2, db2, tm=256):
    t = gsm.shape[0]

    def body(s_ref, dg_ref, db_ref, ds_ref):
        _, vjp = jax.vjp(lambda s: _gcum_f(s, tm), s_ref[...])
        ds_ref[...] = vjp((dg_ref[0], dg_ref[1], db_ref[0], db_ref[1]))[0]

    two = pl.BlockSpec((2, tm, D), lambda i: (0, i, 0))
    tile = pl.BlockSpec((tm, 128), lambda i: (i, 0))
    return pl.pallas_call(
        body, name="gcum_bwd", grid=(t // tm,),
        in_specs=[tile, two, two], out_specs=tile,
        out_shape=jax.ShapeDtypeStruct((t, 128), F32),
        compiler_params=_cparams(("parallel",)),
    )(gsm, dg2, db2)


def _gdn_intra_fwd(qn, kn, vc, g2, b2):
    t = qn.shape[0]
    n = t // CH

    def body(q_ref, k_ref, v_ref, g_ref, b_ref, u_ref, w_ref, a_ref, qd_ref, kd_ref, e_ref, t_ref):
        cs = _Consts(pl.program_id(0) == 1)

        def step(i, carry):
            r = _srows(i)
            q, k, v, g, bx = q_ref[r, :], k_ref[r, :], v_ref[r, :], g_ref[r, :], b_ref[r, :]
            u, w, a, qd, kd, el, tinv = _gdn_intra(q, k, v, g, bx, None, cs)
            u_ref[r, :] = u
            w_ref[r, :] = w
            a_ref[r, :] = a
            qd_ref[r, :] = qd
            kd_ref[r, :] = kd
            t_ref[r, :] = tinv
            for c in range(SC // CH):
                e_ref[(SC // CH) * i + c] = el[c * CH:c * CH + 8]
            return carry

        lax.fori_loop(0, t // SC, step, 0)

    head = pl.BlockSpec((t, DA), lambda d, h: (0, h))
    dh = pl.BlockSpec((None, t, DA), lambda d, h: (d, 0, h))
    sq = lambda w: pl.BlockSpec((None, None, t, w), lambda d, h: (d, h, 0, 0))
    big = jax.ShapeDtypeStruct((2, t, D), F32)
    return pl.pallas_call(
        body, name="gdn_intra_fwd", grid=(2, NA),
        in_specs=[head, head, head, dh, dh],
        out_specs=[dh, dh, sq(CH), dh, dh, pl.BlockSpec((None, None, n, 8, 128), lambda d, h: (d, h, 0, 0, 0)),
                   sq(SC)],
        out_shape=[big, big, jax.ShapeDtypeStruct((2, NA, t, CH), F32), big, big,
                   jax.ShapeDtypeStruct((2, NA, n, 8, 128), F32), jax.ShapeDtypeStruct((2, NA, t, SC), F32)],
        compiler_params=_cparams(("parallel", "parallel")),
    )(qn, kn, vc, g2, b2)


SCAN_TB = 256
SCAN_HB = 8


def _scan_specs(t, width, nheads, hb, along):
    nt = t // SCAN_TB
    nb = SCAN_TB // CH

    def tmap(d, tt):
        fwd = tt + d * (nt - 1 - 2 * tt)
        return fwd if along > 0 else nt - 1 - fwd

    tok = pl.BlockSpec((None, SCAN_TB, hb * width), lambda d, h, tt: (d, tmap(d, tt), h))
    per = lambda *tail: pl.BlockSpec((None, hb, nb) + tail, lambda d, h, tt: (d, h, tmap(d, tt)) + (0,) * len(tail))
    sq = pl.BlockSpec((None, hb, SCAN_TB, CH), lambda d, h, tt: (d, h, tmap(d, tt), 0))
    shared = lambda w: pl.BlockSpec((SCAN_TB, hb * w), lambda d, h, tt: (tmap(d, tt), h))
    return tok, per, sq, shared, (2, nheads // hb, nt), nb


def _gdn_scan_fwd(u, w, a, qd, kd, e):
    t = u.shape[1]
    tok, per, sq, _, grid, nb = _scan_specs(t, DA, NA, SCAN_HB, +1)

    def body(u_ref, w_ref, a_ref, qd_ref, kd_ref, e_ref, o_ref, s_ref, state):
        rev = pl.program_id(0) == 1

        @pl.when(pl.program_id(2) == 0)
        def _():
            state[...] = jnp.zeros_like(state)

        def step(i, ss):
            ci = jnp.where(rev, nb - 1 - i, i)
            r = _rows(ci)
            out = []
            for hh, s in enumerate(ss):
                c = slice(hh * DA, (hh + 1) * DA)
                s_ref[hh, ci] = s
                o, sn = _gdn_scan(u_ref[r, c], w_ref[r, c], a_ref[hh, r, :], qd_ref[r, c], kd_ref[r, c],
                                  e_ref[hh, ci][0:1], s)
                o_ref[r, c] = o
                out.append(sn)
            return tuple(out)

        ss = lax.fori_loop(0, nb, step, tuple(state[hh] for hh in range(SCAN_HB)))
        for hh, s in enumerate(ss):
            state[hh] = s

    return pl.pallas_call(
        body, name="gdn_scan_fwd", grid=grid,
        in_specs=[tok, tok, sq, tok, tok, per(8, 128)],
        out_specs=[tok, per(DA, DA)],
        out_shape=[jax.ShapeDtypeStruct((2, t, D), F32), jax.ShapeDtypeStruct((2, NA, t // CH, DA, DA), F32)],
        scratch_shapes=[pltpu.VMEM((SCAN_HB, DA, DA), F32)],
        compiler_params=_cparams(("parallel", "parallel", "arbitrary")),
    )(u, w, a, qd, kd, e)


def _gdn_scan_bwd(u, w, a, qd, kd, e, ssave, do):
    t = u.shape[1]
    tok, per, sq, shared, grid, nb = _scan_specs(t, DA, NA, SCAN_HB, -1)

    def body(u_ref, w_ref, a_ref, qd_ref, kd_ref, e_ref, s_ref, do_ref,
             du_ref, dw_ref, da_ref, dqd_ref, dkd_ref, de_ref, state):
        rev = pl.program_id(0) == 1

        @pl.when(pl.program_id(2) == 0)
        def _():
            state[...] = jnp.zeros_like(state)

        def step(i, dss):
            ci = jnp.where(rev, i, nb - 1 - i)
            r = _rows(ci)
            out = []
            for hh, ds in enumerate(dss):
                c = slice(hh * DA, (hh + 1) * DA)
                _, vjp = jax.vjp(_gdn_scan, u_ref[r, c], w_ref[r, c], a_ref[hh, r, :], qd_ref[r, c], kd_ref[r, c],
                                 e_ref[hh, ci][0:1], s_ref[hh, ci])
                du, dw, da, dqd, dkd, de, dsn = vjp((do_ref[r, c], ds))
                du_ref[r, c] = du
                dw_ref[r, c] = dw
                da_ref[hh, r, :] = da
                dqd_ref[r, c] = dqd
                dkd_ref[r, c] = dkd
                de_ref[hh, ci] = _first_row(de)
                out.append(dsn)
            return tuple(out)

        dss = lax.fori_loop(0, nb, step, tuple(state[hh] for hh in range(SCAN_HB)))
        for hh, ds in enumerate(dss):
            state[hh] = ds

    big = jax.ShapeDtypeStruct((2, t, D), F32)
    return pl.pallas_call(
        body, name="gdn_scan_bwd", grid=grid,
        in_specs=[tok, tok, sq, tok, tok, per(8, 128), per(DA, DA), shared(DA)],
        out_specs=[tok, tok, sq, tok, tok, per(8, 128)],
        out_shape=[big, big, jax.ShapeDtypeStruct((2, NA, t, CH), F32), big, big,
                   jax.ShapeDtypeStruct((2, NA, t // CH, 8, 128), F32)],
        scratch_shapes=[pltpu.VMEM((SCAN_HB, DA, DA), F32)],
        compiler_params=_cparams(("parallel", "parallel", "arbitrary")),
    )(u, w, a, qd, kd, e, ssave, do)


def _gdn_intra_bwd(qn, kn, vc, g2, b2, tinv, du, dw, da, dqd, dkd, de):
    t = qn.shape[0]
    n = t // CH

    def body(q_ref, k_ref, v_ref, g_ref, b_ref, t_ref, du_ref, dw_ref, da_ref, dqd_ref, dkd_ref, de_ref,
             dq_ref, dk_ref, dv_ref, dg_ref, db_ref):
        d = pl.program_id(1)
        cs = _Consts(d == 1)

        @pl.when(d == 0)
        def _():
            dq_ref[...] = jnp.zeros_like(dq_ref)
            dk_ref[...] = jnp.zeros_like(dk_ref)
            dv_ref[...] = jnp.zeros_like(dv_ref)

        def step(i, carry):
            r = _srows(i)
            tinv_c = t_ref[r, :]
            f = lambda q, k, v, g, bx: _gdn_intra(q, k, v, g, bx, tinv_c, cs)
            _, vjp = jax.vjp(f, q_ref[r, :], k_ref[r, :], v_ref[r, :], g_ref[r, :], b_ref[r, :])
            dq, dk, dv, dg, dbx = vjp((du_ref[r, :], dw_ref[r, :], da_ref[r, :], dqd_ref[r, :],
                                       dkd_ref[r, :], _chunk_rows(de_ref, i)))
            dq_ref[r, :] += dq
            dk_ref[r, :] += dk
            dv_ref[r, :] += dv
            dg_ref[r, :] = dg
            db_ref[r, :] = dbx
            return carry

        lax.fori_loop(0, t // SC, step, 0)

    head = pl.BlockSpec((t, DA), lambda h, d: (0, h))
    dh = pl.BlockSpec((None, t, DA), lambda h, d: (d, 0, h))
    sq = pl.BlockSpec((None, None, t, CH), lambda h, d: (d, h, 0, 0))
    tq = pl.BlockSpec((None, None, t, SC), lambda h, d: (d, h, 0, 0))
    full = jax.ShapeDtypeStruct((t, D), F32)
    big = jax.ShapeDtypeStruct((2, t, D), F32)
    return pl.pallas_call(
        body, name="gdn_intra_bwd", grid=(NA, 2),
        in_specs=[head, head, head, dh, dh, tq, dh, dh, sq, dh, dh,
                  pl.BlockSpec((None, None, n, 8, 128), lambda h, d: (d, h, 0, 0, 0))],
        out_specs=[head, head, head, dh, dh],
        out_shape=[full, full, full, big, big],
        compiler_params=_cparams(("arbitrary", "arbitrary")),
    )(qn, kn, vc, g2, b2, tinv, du, dw, da, dqd, dkd, de)


def _gla_specs(t, order):
    ix = (lambda d, h: (d, h)) if order == "dh" else (lambda h, d: (d, h))

    def mk(fn):
        return lambda a, b: fn(*ix(a, b))
    q = pl.BlockSpec((t, DKB), mk(lambda d, h: (0, 32 + h)))
    k = pl.BlockSpec((t, DKB), mk(lambda d, h: (0, 36 + h)))
    v = pl.BlockSpec((t, DVB), mk(lambda d, h: (0, 20 + h)))
    dk = pl.BlockSpec((None, t, DKB), mk(lambda d, h: (d, 0, h)))
    dv = pl.BlockSpec((None, t, DVB), mk(lambda d, h: (d, 0, h)))
    e = pl.BlockSpec((None, None, t // CH, 8, 128), mk(lambda d, h: (d, h, 0, 0, 0)))
    s = pl.BlockSpec((None, None, t // CH, DVB, DKB), mk(lambda d, h: (d, h, 0, 0, 0)))
    return q, k, v, dk, dv, e, s


def _gla_intra_fwd(p, gk):
    t = p.shape[0]
    n = t // CH

    def body(q_ref, k_ref, v_ref, g_ref, qg_ref, kd_ref, in_ref, e_ref):
        cs = _Consts(pl.program_id(0) == 1)

        def step(i, carry):
            r = _srows(i)
            qg, kd, intra, el = _gla_intra(q_ref[r, :], k_ref[r, :], v_ref[r, :], g_ref[r, :], cs)
            qg_ref[r, :] = qg
            kd_ref[r, :] = kd
            in_ref[r, :] = intra
            for c in range(SC // CH):
                e_ref[(SC // CH) * i + c] = el[c * CH:c * CH + 8]
            return carry

        lax.fori_loop(0, t // SC, step, 0)

    q, k, v, dk, dv, e, _ = _gla_specs(t, "dh")
    return pl.pallas_call(
        body, name="gla_intra_fwd", grid=(2, NB),
        in_specs=[q, k, v, dk], out_specs=[dk, dk, dv, e],
        out_shape=[jax.ShapeDtypeStruct((2, t, NB * DKB), F32), jax.ShapeDtypeStruct((2, t, NB * DKB), F32),
                   jax.ShapeDtypeStruct((2, t, D), F32), jax.ShapeDtypeStruct((2, NB, n, 8, 128), F32)],
        compiler_params=_cparams(("parallel", "parallel")),
    )(p, p, p, gk)


GLA_HB = 2


def _gla_v_spec(t, along):
    nt = t // SCAN_TB

    def tmap(d, tt):
        fwd = tt + d * (nt - 1 - 2 * tt)
        return fwd if along > 0 else nt - 1 - fwd

    return pl.BlockSpec((SCAN_TB, GLA_HB * DVB), lambda d, h, tt: (tmap(d, tt), 5120 // (GLA_HB * DVB) + h))


def _gla_scan_fwd(p, qg, kd, intra, e):
    t = p.shape[0]
    tokk, per, _, _, grid, nb = _scan_specs(t, DKB, NB, GLA_HB, +1)
    tokv = _scan_specs(t, DVB, NB, GLA_HB, +1)[0]

    def body(v_ref, qg_ref, kd_ref, in_ref, e_ref, o_ref, s_ref, state):
        rev = pl.program_id(0) == 1

        @pl.when(pl.program_id(2) == 0)
        def _():
            state[...] = jnp.zeros_like(state)

        def step(i, sts):
            ci = jnp.where(rev, nb - 1 - i, i)
            r = _rows(ci)
            out = []
            for hh, st in enumerate(sts):
                ck = slice(hh * DKB, (hh + 1) * DKB)
                cv = slice(hh * DVB, (hh + 1) * DVB)
                s_ref[hh, ci] = st
                o, stn = _gla_scan(qg_ref[r, ck], kd_ref[r, ck], v_ref[r, cv], e_ref[hh, ci][0:1], st)
                o_ref[r, cv] = o + in_ref[r, cv]
                out.append(stn)
            return tuple(out)

        sts = lax.fori_loop(0, nb, step, tuple(state[hh] for hh in range(GLA_HB)))
        for hh, st in enumerate(sts):
            state[hh] = st

    return pl.pallas_call(
        body, name="gla_scan_fwd", grid=grid,
        in_specs=[_gla_v_spec(t, +1), tokk, tokk, tokv, per(8, 128)], out_specs=[tokv, per(DVB, DKB)],
        out_shape=[jax.ShapeDtypeStruct((2, t, D), F32), jax.ShapeDtypeStruct((2, NB, t // CH, DVB, DKB), F32)],
        scratch_shapes=[pltpu.VMEM((GLA_HB, DVB, DKB), F32)],
        compiler_params=_cparams(("parallel", "parallel", "arbitrary")),
    )(p, qg, kd, intra, e)


def _gla_scan_bwd(p, qg, kd, e, ssave, do):
    t = p.shape[0]
    tokk, per, _, shared, grid, nb = _scan_specs(t, DKB, NB, GLA_HB, -1)
    tokv = _scan_specs(t, DVB, NB, GLA_HB, -1)[0]

    def body(v_ref, qg_ref, kd_ref, e_ref, s_ref, do_ref, dqg_ref, dkd_ref, dv_ref, de_ref, state):
        rev = pl.program_id(0) == 1

        @pl.when(pl.program_id(2) == 0)
        def _():
            state[...] = jnp.zeros_like(state)

        def step(i, dsts):
            ci = jnp.where(rev, i, nb - 1 - i)
            r = _rows(ci)
            out = []
            for hh, dst in enumerate(dsts):
                ck = slice(hh * DKB, (hh + 1) * DKB)
                cv = slice(hh * DVB, (hh + 1) * DVB)
                _, vjp = jax.vjp(_gla_scan, qg_ref[r, ck], kd_ref[r, ck], v_ref[r, cv], e_ref[hh, ci][0:1],
                                 s_ref[hh, ci])
                dqg, dkd, dv, de, dstn = vjp((do_ref[r, cv], dst))
                dqg_ref[r, ck] = dqg
                dkd_ref[r, ck] = dkd
                dv_ref[r, cv] = dv
                de_ref[hh, ci] = _first_row(de)
                out.append(dstn)
            return tuple(out)

        dsts = lax.fori_loop(0, nb, step, tuple(state[hh] for hh in range(GLA_HB)))
        for hh, dst in enumerate(dsts):
            state[hh] = dst

    return pl.pallas_call(
        body, name="gla_scan_bwd", grid=grid,
        in_specs=[_gla_v_spec(t, -1), tokk, tokk, per(8, 128), per(DVB, DKB), shared(DVB)],
        out_specs=[tokk, tokk, tokv, per(8, 128)],
        out_shape=[jax.ShapeDtypeStruct((2, t, NB * DKB), F32), jax.ShapeDtypeStruct((2, t, NB * DKB), F32),
                   jax.ShapeDtypeStruct((2, t, D), F32), jax.ShapeDtypeStruct((2, NB, t // CH, 8, 128), F32)],
        scratch_shapes=[pltpu.VMEM((GLA_HB, DVB, DKB), F32)],
        compiler_params=_cparams(("parallel", "parallel", "arbitrary")),
    )(p, qg, kd, e, ssave, do)


def _gla_intra_bwd(p, gk, dqg, dkd, dvs, de, do):
    t = p.shape[0]
    n = t // CH

    def body(q_ref, k_ref, v_ref, g_ref, dqg_ref, dkd_ref, dvs_ref, de_ref, do_ref,
             dq_ref, dk_ref, dv_ref, dg_ref):
        d = pl.program_id(1)
        cs = _Consts(d == 1)

        @pl.when(d == 0)
        def _():
            dq_ref[...] = jnp.zeros_like(dq_ref)
            dk_ref[...] = jnp.zeros_like(dk_ref)
            dv_ref[...] = jnp.zeros_like(dv_ref)

        def step(i, carry):
            r = _srows(i)
            f = lambda q, k, v, g: _gla_intra(q, k, v, g, cs)
            _, vjp = jax.vjp(f, q_ref[r, :], k_ref[r, :], v_ref[r, :], g_ref[r, :])
            dq, dk, dv, dg = vjp((dqg_ref[r, :], dkd_ref[r, :], do_ref[r, :], _chunk_rows(de_ref, i)))
            dq_ref[r, :] += dq
            dk_ref[r, :] += dk
            dv_ref[r, :] += dv + dvs_ref[r, :]
            dg_ref[r, :] = dg
            return carry

        lax.fori_loop(0, t // SC, step, 0)

    q, k, v, dk, dv, e_s, _ = _gla_specs(t, "hd")
    hk = pl.BlockSpec((t, DKB), lambda h, d: (0, h))
    hv = pl.BlockSpec((t, DVB), lambda h, d: (0, h))
    return pl.pallas_call(
        body, name="gla_intra_bwd", grid=(NB, 2),
        in_specs=[q, k, v, dk, dk, dk, dv, e_s, hv],
        out_specs=[hk, hk, hv, dk],
        out_shape=[jax.ShapeDtypeStruct((t, NB * DKB), F32), jax.ShapeDtypeStruct((t, NB * DKB), F32),
                   jax.ShapeDtypeStruct((t, D), F32), jax.ShapeDtypeStruct((2, t, NB * DKB), F32)],
        compiler_params=_cparams(("arbitrary", "arbitrary")),
    )(p, p, p, gk, dqg, dkd, dvs, de, do)


def _seg_gate(o, z, w):
    return _rms(o, w) * _silu(z)


def _seg_merge(ya, yb, ga, gb):
    return _sigmoid(ga) * ya + _sigmoid(gb) * yb


def _seg_loss(out, x, tgt, w):
    err = x + _rms(out, w) - tgt
    return 0.5 * jnp.sum(jnp.mean(err * err, axis=-1, keepdims=True), axis=0, keepdims=True)


def _post(oa2, ob2, p, x, tgt, gdn_w, gla_w, lnpost, w3, tm=128):
    t = x.shape[0]

    def body(oa_ref, ob_ref, z_ref, gb_ref, ga_ref, gB_ref, x_ref, t_ref, aw_ref, bw_ref, lw_ref, w_ref,
             loss_ref, doa_ref, dob_ref, dz_ref, dgb_ref, dga_ref, dgB_ref, dy_ref,
             dw_ref, daw_ref, dbw_ref, dlw_ref):
        first = pl.program_id(0) == 0
        oa = oa_ref[0] + oa_ref[1]
        ob = ob_ref[0] + ob_ref[1]
        z, gb = z_ref[...], gb_ref[...]
        aw, bw = aw_ref[...], bw_ref[...]
        rs = D // NSHARD

        def mat(a, m):
            return sum(jnp.dot(a[:, s * rs:(s + 1) * rs], w_ref[s, m], preferred_element_type=F32)
                       for s in range(NSHARD))

        def mat_t(g, m):
            return jnp.concatenate([_bdot(g, w_ref[s, m], 1, 1) for s in range(NSHARD)], axis=1)

        def add_dw(a, g, m):
            for s in range(NSHARD):
                dw_ref[s, m] += _bdot(a[:, s * rs:(s + 1) * rs], g, 0, 0)

        pa = [jax.vjp(_seg_gate, oa[:, h * DA:(h + 1) * DA], z[:, h * DA:(h + 1) * DA], aw) for h in range(NA)]
        pb = [jax.vjp(_seg_gate, ob[:, h * DVB:(h + 1) * DVB], gb[:, h * DVB:(h + 1) * DVB], bw)
              for h in range(NB)]
        a1 = jnp.concatenate([v for v, _ in pa], axis=1).astype(BF16)
        a2 = jnp.concatenate([v for v, _ in pb], axis=1).astype(BF16)
        ya = mat(a1, 0)
        yb = mat(a2, 1)
        merged, vjp_m = jax.vjp(_seg_merge, ya, yb, ga_ref[...], gB_ref[...])
        mb = merged.astype(BF16)
        out = mat(mb, 2)
        loss, vjp_l = jax.vjp(_seg_loss, out, x_ref[...], t_ref[...], lw_ref[...])
        dout, dyres, _, dlw = vjp_l(jnp.ones((1, 1), F32))
        dy_ref[...] = dyres
        doutb = dout.astype(BF16)
        dmerged = mat_t(doutb, 2)
        dya, dyb, dga, dgB = vjp_m(dmerged)
        dga_ref[...] = dga.astype(BF16)
        dgB_ref[...] = dgB.astype(BF16)
        dyab, dybb = dya.astype(BF16), dyb.astype(BF16)
        da1 = mat_t(dyab, 0)
        da2 = mat_t(dybb, 1)

        daw = jnp.zeros_like(aw)
        for h in range(NA):
            sl = slice(h * DA, (h + 1) * DA)
            do, dz, dw = pa[h][1](da1[:, sl])
            doa_ref[:, sl] = do
            dz_ref[:, sl] = dz.astype(BF16)
            daw = daw + dw
        dbw = jnp.zeros_like(bw)
        for h in range(NB):
            sl = slice(h * DVB, (h + 1) * DVB)
            do, dg, dw = pb[h][1](da2[:, sl])
            dob_ref[:, sl] = do
            dgb_ref[:, sl] = dg.astype(BF16)
            dbw = dbw + dw

        @pl.when(first)
        def _():
            loss_ref[...] = jnp.zeros_like(loss_ref)
            dw_ref[...] = jnp.zeros_like(dw_ref)
            daw_ref[...] = jnp.zeros_like(daw_ref)
            dbw_ref[...] = jnp.zeros_like(dbw_ref)
            dlw_ref[...] = jnp.zeros_like(dlw_ref)

        loss_ref[...] += jnp.broadcast_to(loss, loss_ref.shape)
        add_dw(a1, dyab, 0)
        add_dw(a2, dybb, 1)
        add_dw(mb, doutb, 2)
        daw_ref[...] += jnp.broadcast_to(daw, daw_ref.shape)
        dbw_ref[...] += jnp.broadcast_to(dbw, dbw_ref.shape)
        dlw_ref[...] += jnp.broadcast_to(dlw, dlw_ref.shape)

    two = pl.BlockSpec((2, tm, D), lambda i: (0, i, 0))
    pcol = lambda c: pl.BlockSpec((tm, D), lambda i: (i, c))
    tok = pl.BlockSpec((tm, D), lambda i: (i, 0))
    row = lambda n: pl.BlockSpec((1, n), lambda i: (0, 0))
    row8 = lambda n: pl.BlockSpec((8, n), lambda i: (0, 0))
    once = pl.Buffered(1)
    tokf = jax.ShapeDtypeStruct((t, D), F32)
    tokb = jax.ShapeDtypeStruct((t, D), BF16)
    wspec = pl.BlockSpec((NSHARD, 3, D // NSHARD, D), lambda i: (0, 0, 0, 0), pipeline_mode=once)
    return pl.pallas_call(
        body, name="post", grid=(t // tm,),
        in_specs=[two, two, pcol(3), pcol(6), pcol(7), pcol(8), tok, tok, row(DA), row(DVB), row(D), wspec],
        out_specs=[row8(128), tok, tok, tok, tok, tok, tok, tok, wspec, row8(DA), row8(DVB), row8(D)],
        out_shape=[jax.ShapeDtypeStruct((8, 128), F32), tokf, tokf, tokb, tokb, tokb, tokb, tokf,
                   jax.ShapeDtypeStruct((NSHARD, 3, D // NSHARD, D), F32),
                   jax.ShapeDtypeStruct((8, DA), F32), jax.ShapeDtypeStruct((8, DVB), F32),
                   jax.ShapeDtypeStruct((8, D), F32)],
        compiler_params=_cparams(("arbitrary",), vmem_mb=56),
    )(oa2, ob2, p, p, p, p, x, tgt, gdn_w, gla_w, lnpost, w3)


def _adam_math(w, g, m, v):
    nm = B1 * m + (1.0 - B1) * g
    nv = B2 * v + (1.0 - B2) * (g * g)
    m_hat = nm / (1.0 - B1 ** STEP)
    v_hat = nv / (1.0 - B2 ** STEP)
    return -LR * (m_hat / (jnp.sqrt(v_hat) + ADAM_EPS) + WD * w), nm, nv


SMALL_SLOTS = (("ln_pre_w", 0, 1024, 0), ("a_log_fwd", 1024, 8, 0), ("a_log_bwd", 1024, 8, 8),
               ("dt_bias_fwd", 1152, 8, 0), ("dt_bias_bwd", 1152, 8, 8), ("gdn_norm_w", 1280, 128, 0),
               ("gk_b2_fwd", 1408, 512, 0), ("gk_b2_bwd", 1920, 512, 0), ("gla_norm_w", 2432, 256, 0),
               ("ln_post_w", 2688, 1024, 0))
SMALL_W = 3840


def _adam_small(gsum, ws, ms, vs):
    nw = len(SMALL_SLOTS)

    def body(g_ref, *refs):
        w_refs, m_refs, v_refs, outs = refs[0:nw], refs[nw:2 * nw], refs[2 * nw:3 * nw], refs[3 * nw:]
        for i, (_, off, n, shift) in enumerate(SMALL_SLOTS):
            slot = g_ref[0:1, off:off + max(n, 128)]
            if shift:
                slot = pltpu.roll(slot, 128 - shift, 1)
            g = slot[:, 0:n]
            d, nm, nv = _adam_math(w_refs[i][...], g, m_refs[i][...], v_refs[i][...])
            for k, val in enumerate((g, d, nm, nv)):
                outs[4 * i + k][...] = val

    vm = pl.BlockSpec(memory_space=pltpu.VMEM)
    res = pl.pallas_call(
        body, name="adam_small", in_specs=[vm] * (1 + 3 * nw), out_specs=[vm] * (4 * nw),
        out_shape=[jax.ShapeDtypeStruct((1, n), F32) for _, _, n, _ in SMALL_SLOTS for _ in range(4)],
    )(gsum, *ws, *ms, *vs)
    return {name: res[4 * i:4 * i + 4] for i, (name, _, _, _) in enumerate(SMALL_SLOTS)}


def _adam(w, mine, got, m, v, tr, tile0=0, name=""):
    rows, cols = w.shape
    nh = mine.shape[0] // tr

    def body(c_ref, w_ref, a_ref, b_ref, m_ref, v_ref, g_ref, d_ref, nm_ref, nv_ref):
        half = (tile0 + pl.program_id(0)) // nh
        g = jnp.where(half == c_ref[0], a_ref[...], b_ref[...])
        d, nm, nv = _adam_math(w_ref[...], g, m_ref[...], v_ref[...])
        g_ref[...] = g
        d_ref[...] = d
        nm_ref[...] = nm
        nv_ref[...] = nv

    blk = pl.BlockSpec((tr, cols), lambda i, cc: (i, 0))
    half = pl.BlockSpec((tr, cols), lambda i, cc: ((tile0 + i) % nh, 0))
    shp = jax.ShapeDtypeStruct((rows, cols), F32)
    return pl.pallas_call(
        body, name=f"adam_{name}{rows}x{cols}",
        grid_spec=pltpu.PrefetchScalarGridSpec(
            num_scalar_prefetch=1, grid=(rows // tr,),
            in_specs=[blk, half, half, blk, blk], out_specs=[blk] * 4),
        out_shape=[shp] * 4,
        compiler_params=_cparams(("parallel",)),
    )(lax.axis_index("c").reshape(1), w, mine, got, m, v)


def _sum_cast(own, got):
    ns, _, r, c = own.shape
    tr = r // 4 if r % 64 == 0 else r

    def body(c_ref, a_ref, b_ref, f_ref, h_ref):
        s = a_ref[...] + b_ref[...]
        f_ref[...] = s
        h_ref[...] = s.astype(BF16)

    return pl.pallas_call(
        body, name=f"sum_cast_{r}x{c}",
        grid_spec=pltpu.PrefetchScalarGridSpec(
            num_scalar_prefetch=1, grid=(ns, r // tr),
            in_specs=[pl.BlockSpec((None, None, tr, c), lambda s, i, cc: (s, cc[0], i, 0)),
                      pl.BlockSpec((None, tr, c), lambda s, i, cc: (s, i, 0))],
            out_specs=[pl.BlockSpec((None, tr, c), lambda s, i, cc: (s, i, 0)),
                       pl.BlockSpec((None, tr, c), lambda s, i, cc: (s, i, 0))]),
        out_shape=[jax.ShapeDtypeStruct((ns, r, c), F32), jax.ShapeDtypeStruct((ns, r, c), BF16)],
        compiler_params=_cparams(("parallel", "parallel")),
    )(lax.axis_index("c").reshape(1), own, got)


def _sum4(mine, got):
    _, r, c = mine.shape
    tr = r // 4 if r % 64 == 0 else r

    def body(s_ref, a_ref, g_ref, o_ref):
        acc = a_ref[...] + g_ref[0].astype(F32)
        acc = acc + g_ref[1].astype(F32)
        o_ref[...] = acc + g_ref[2].astype(F32)

    shard = (2 * lax.axis_index("x") + lax.axis_index("y")).reshape(1)
    return pl.pallas_call(
        body, name=f"sum4_{r}x{c}",
        grid_spec=pltpu.PrefetchScalarGridSpec(
            num_scalar_prefetch=1, grid=(r // tr,),
            in_specs=[pl.BlockSpec((None, tr, c), lambda i, ss: (ss[0], i, 0)),
                      pl.BlockSpec((3, tr, c), lambda i, ss: (0, i, 0))],
            out_specs=pl.BlockSpec((tr, c), lambda i, ss: (i, 0))),
        out_shape=jax.ShapeDtypeStruct((r, c), F32),
        compiler_params=_cparams(("parallel",)),
    )(shard, mine, got)


def _place():
    x, y, c = lax.axis_index("x"), lax.axis_index("y"), lax.axis_index("c")
    chips = [(1 - x, y), (x, 1 - y), (1 - x, 1 - y)]
    return x, y, c, chips


def _gather_weights(parts):
    npart = len(parts)

    def body(*refs):
        ins, outs = refs[:npart], refs[npart:2 * npart]
        send_sems, recv_sems = refs[2 * npart:]
        x, y, c, chips = _place()
        sibling = (x, y, 1 - c)
        mine = 2 * x + y

        def remote(k, p, shard, half, to, src=None):
            dst = outs[p].at[shard, half]
            return pltpu.make_async_remote_copy(
                src_ref=dst if src is None else src, dst_ref=dst,
                send_sem=send_sems.at[k], recv_sem=recv_sems.at[k], device_id=to, device_id_type=MESH)

        first = [remote(j * npart + p, p, mine, c, (*chip, c), src=ins[p].at[c])
                 for j, chip in enumerate(chips) for p in range(npart)]
        for cp in first:
            cp.start()
        passed = []
        for j, (cx, cy) in enumerate(chips):
            for p in range(npart):
                remote(j * npart + p, p, 2 * cx + cy, c, (x, y, c)).wait_recv()
                fw = remote((3 + j) * npart + p, p, 2 * cx + cy, c, sibling)
                fw.start()
                passed.append(fw)
        for j, (cx, cy) in enumerate(chips):
            for p in range(npart):
                remote((3 + j) * npart + p, p, 2 * cx + cy, 1 - c, (x, y, c)).wait_recv()
        for cp in first + passed:
            cp.wait_send()

    got = pl.pallas_call(
        body, name="gather_weights",
        in_specs=[ANY] * npart, out_specs=[ANY] * npart,
        out_shape=[jax.ShapeDtypeStruct((NSHARD,) + a.shape, a.dtype) for a in parts],
        scratch_shapes=[pltpu.SemaphoreType.DMA((6 * npart,)), pltpu.SemaphoreType.DMA((6 * npart,))],
    )(*parts)
    mine = 2 * lax.axis_index("x") + lax.axis_index("y")
    return [lax.dynamic_update_index_in_dim(g, a, mine, 0) for g, a in zip(got, parts)]


def _swap_halves(parts):
    npart = len(parts)

    def body(*refs):
        ins, outs = refs[:npart], refs[npart:2 * npart]
        send_sems, recv_sems = refs[2 * npart:]
        x, y, c, _ = _place()
        cps = [pltpu.make_async_remote_copy(
            src_ref=ins[p].at[s, 1 - c], dst_ref=outs[p].at[s],
            send_sem=send_sems.at[s * npart + p], recv_sem=recv_sems.at[s * npart + p],
            device_id=(x, y, 1 - c), device_id_type=MESH) for s in range(NSHARD) for p in range(npart)]
        for cp in cps:
            cp.start()
        for cp in cps:
            cp.wait()

    return pl.pallas_call(
        body, name="swap_halves", in_specs=[ANY] * npart, out_specs=[ANY] * npart,
        out_shape=[jax.ShapeDtypeStruct((NSHARD,) + a.shape[2:], a.dtype) for a in parts],
        scratch_shapes=[pltpu.SemaphoreType.DMA((NSHARD * npart,)), pltpu.SemaphoreType.DMA((NSHARD * npart,))],
    )(*parts)


def _scatter_shards(parts):
    npart = len(parts)

    def body(*refs):
        ins, outs = refs[:npart], refs[npart:2 * npart]
        send_sems, recv_sems = refs[2 * npart:]
        x, y, c, chips = _place()
        cps = [pltpu.make_async_remote_copy(
            src_ref=ins[p].at[2 * cx + cy], dst_ref=outs[p].at[j],
            send_sem=send_sems.at[j * npart + p], recv_sem=recv_sems.at[j * npart + p],
            device_id=(cx, cy, c), device_id_type=MESH)
            for j, (cx, cy) in enumerate(chips) for p in range(npart)]
        for cp in cps:
            cp.start()
        for cp in cps:
            cp.wait()

    return pl.pallas_call(
        body, name="scatter_shards", in_specs=[ANY] * npart, out_specs=[ANY] * npart,
        out_shape=[jax.ShapeDtypeStruct((3,) + a.shape[1:], a.dtype) for a in parts],
        scratch_shapes=[pltpu.SemaphoreType.DMA((3 * npart,)), pltpu.SemaphoreType.DMA((3 * npart,))],
    )(*parts)


HBM = pl.BlockSpec(memory_space=pltpu.HBM)
SEM = pl.BlockSpec(memory_space=pltpu.SEMAPHORE)
EFFECT = pltpu.SideEffectType.DATAFLOW_SIDE_EFFECTING


def _scatter_copies(srcs, lands, send_sems, recv_sems, waiting):
    x, y, c, chips = _place()
    n = len(srcs)
    return [pltpu.make_async_remote_copy(
        src_ref=srcs[p].at[2 * cx + cy], dst_ref=lands[p].at[j],
        send_sem=send_sems.at[j * n + p], recv_sem=recv_sems.at[j * n + p],
        device_id=(cx, cy, c), device_id_type=MESH)
        for j, (cx, cy) in enumerate(chips) for p in range(n)]


def _proj_copies(srcs, lands, send_sems, recv_sems, waiting):
    x, y, c, chips = _place()
    mine = 2 * x + y
    return [pltpu.make_async_remote_copy(
        src_ref=srcs[0].at[c], dst_ref=lands[0].at[mine, c],
        send_sem=send_sems.at[2 * j + to], recv_sem=recv_sems.at[2 * j + (to if waiting else c)],
        device_id=(cx, cy, to), device_id_type=MESH)
        for j, (cx, cy) in enumerate(chips) for to in range(2)]


def _start_copies(copies, nsem, parts, lands, name, after=None):
    n = len(parts)
    extra = [] if after is None else [after]

    def body(*refs):
        outs = refs[2 * n + len(extra):]
        for cp in copies(refs[:n], refs[n:2 * n], outs[0], outs[1], False):
            cp.start()
        outs[-1][...] = jnp.zeros_like(outs[-1])

    res = pl.pallas_call(
        body, name=name,
        out_shape=(pltpu.SemaphoreType.DMA((nsem,)), pltpu.SemaphoreType.DMA((nsem,)),
                   *[pltpu.HBM(a.shape, a.dtype) for a in parts], *[pltpu.HBM(a.shape, a.dtype) for a in lands],
                   jax.ShapeDtypeStruct((8, 128), F32)),
        in_specs=[HBM] * (2 * n) + [ANY] * len(extra),
        out_specs=(SEM, SEM, *[HBM] * (2 * n), pl.BlockSpec(memory_space=pltpu.VMEM)),
        input_output_aliases={i: 2 + i for i in range(2 * n)},
        compiler_params=pltpu.CompilerParams(has_side_effects=EFFECT),
    )(*[pltpu.with_memory_space_constraint(a, pltpu.HBM) for a in parts],
      *[pltpu.with_memory_space_constraint(a, pltpu.HBM) for a in lands], *extra)
    return res[0], res[1], res[2:2 + n], res[2 + n:2 + 2 * n], res[-1]


def _wait_copies(copies, started, after, name):
    send_sems, recv_sems, srcs, lands, _ = started
    n = len(srcs)

    def body(*refs):
        for cp in copies(refs[:n], refs[n:2 * n], refs[2 * n], refs[2 * n + 1], True):
            cp.wait_send()
            cp.wait_recv()

    res = pl.pallas_call(
        body, name=name,
        out_shape=tuple(pltpu.HBM(a.shape, a.dtype) for a in (*srcs, *lands)),
        in_specs=[HBM] * (2 * n) + [SEM, SEM, ANY], out_specs=tuple([HBM] * (2 * n)),
        input_output_aliases={i: i for i in range(2 * n)},
        compiler_params=pltpu.CompilerParams(has_side_effects=EFFECT),
    )(*srcs, *lands, send_sems, recv_sems, after)
    return res[n:]


def _join_halves(parts):
    npart = len(parts)

    def body(*refs):
        ins, outs = refs[:npart], refs[npart:2 * npart]
        send_sems, recv_sems = refs[2 * npart:]
        x, y, c, _ = _place()
        cps = [pltpu.make_async_remote_copy(
            src_ref=ins[p], dst_ref=outs[p], send_sem=send_sems.at[p], recv_sem=recv_sems.at[p],
            device_id=(x, y, 1 - c), device_id_type=MESH) for p in range(npart)]
        for cp in cps:
            cp.start()
        for cp in cps:
            cp.wait()

    return pl.pallas_call(
        body, name="join_halves", in_specs=[ANY] * npart, out_specs=[ANY] * npart,
        out_shape=[jax.ShapeDtypeStruct(a.shape, a.dtype) for a in parts],
        scratch_shapes=[pltpu.SemaphoreType.DMA((npart,)), pltpu.SemaphoreType.DMA((npart,))],
    )(*parts)


def _allreduce_small(v):
    r, ncol = v.shape

    def body(v_ref, o_ref, buf, send_sems, recv_sems):
        x, y, c, _ = _place()
        me = 4 * x + 2 * y + c
        buf[me] = v_ref[...]
        cps = []
        for k in range(1, 8):
            px, py, pc = x ^ (k >> 2), y ^ ((k >> 1) & 1), c ^ (k & 1)
            cps.append(pltpu.make_async_remote_copy(
                src_ref=v_ref, dst_ref=buf.at[me], send_sem=send_sems.at[k - 1], recv_sem=recv_sems.at[k - 1],
                device_id=(px, py, pc), device_id_type=MESH))
        for cp in cps:
            cp.start()
        for k in range(1, 8):
            px, py, pc = x ^ (k >> 2), y ^ ((k >> 1) & 1), c ^ (k & 1)
            pltpu.make_async_remote_copy(
                src_ref=v_ref, dst_ref=buf.at[4 * px + 2 * py + pc], send_sem=send_sems.at[k - 1],
                recv_sem=recv_sems.at[k - 1], device_id=(px, py, pc), device_id_type=MESH).wait_recv()
        for cp in cps:
            cp.wait_send()
        acc = buf[0]
        for d in range(1, 8):
            acc = acc + buf[d]
        o_ref[...] = acc

    return pl.pallas_call(
        body, name="allreduce_small",
        in_specs=[pl.BlockSpec(memory_space=pltpu.VMEM)], out_specs=pl.BlockSpec(memory_space=pltpu.VMEM),
        out_shape=jax.ShapeDtypeStruct((r, ncol), F32),
        scratch_shapes=[pltpu.VMEM((8, r, ncol), F32), pltpu.SemaphoreType.DMA((7,)), pltpu.SemaphoreType.DMA((7,))],
    )(v)


def _permute_rows(shards):
    w0, w1, w2, w3 = shards
    zeros = jnp.zeros((NPERM - 9280, w0.shape[1]), w0.dtype)
    return jnp.concatenate([w0, w1[0:1776], w1[1808:2320], w2, w3[0:240], w3[272:2320],
                            w1[1776:1808], w3[240:272], zeros], axis=0)


def _unpermute_rows(g):
    s1 = jnp.concatenate([g[2320:4096], g[9216:9248], g[4096:4608]], axis=0)
    s3 = jnp.concatenate([g[6928:7168], g[9248:9280], g[7168:9216]], axis=0)
    return jnp.stack([g[0:2320], s1, g[4608:6928], s3], axis=0)


def _pack_shard_small(conv, w2f, w2b):
    top = jnp.pad(conv, ((0, 8 - conv.shape[0]), (0, 0)))
    mid = jnp.pad(jnp.concatenate([w2f, w2b], axis=1), ((0, 0), (0, 768 - 256)))
    return jnp.concatenate([top, mid, jnp.zeros((8, 768), conv.dtype)], axis=0)


def _unpack_shard_small(a):
    return a[0:5], a[8:24, 0:128], a[8:24, 128:256]


def kernel(x, ln_pre_w, w_in, conv_w, a_log_fwd, a_log_bwd, dt_bias_fwd, dt_bias_bwd, gdn_norm_w, w_proj_gdn, gk_w2_fwd, gk_b2_fwd, gk_w2_bwd, gk_b2_bwd, gla_norm_w, w_proj_gla, w_out, ln_post_w, loss_target, m_ln_pre_w, m_w_in, m_conv_w, m_a_log_fwd, m_a_log_bwd, m_dt_bias_fwd, m_dt_bias_bwd, m_gdn_norm_w, m_w_proj_gdn, m_gk_w2_fwd, m_gk_b2_fwd, m_gk_w2_bwd, m_gk_b2_bwd, m_gla_norm_w, m_w_proj_gla, m_w_out, m_ln_post_w, v_ln_pre_w, v_w_in, v_conv_w, v_a_log_fwd, v_a_log_bwd, v_dt_bias_fwd, v_dt_bias_bwd, v_gdn_norm_w, v_w_proj_gdn, v_gk_w2_fwd, v_gk_b2_fwd, v_gk_w2_bwd, v_gk_b2_bwd, v_gla_norm_w, v_w_proj_gla, v_w_out, v_ln_post_w):
    t = x.shape[1]
    x2, tgt = x[0], loss_target[0]

    win_l = w_in[0].T.astype(BF16).reshape(2, SHW // 2, D)
    proj_l = jnp.concatenate([w_proj_gdn[0], w_proj_gla[0], w_out[0]], axis=0).astype(BF16).reshape(2, 384, D)
    small_l = _pack_shard_small(conv_w[0], gk_w2_fwd[0], gk_w2_bwd[0]).reshape(2, 16, 768)
    win_g, small_g = _gather_weights([win_l, small_l])
    proj_started = _start_copies(_proj_copies, 6, [proj_l], [lax.empty((NSHARD, 2, 384, D), BF16)],
                                 "gather_proj_start", after=small_g)
    wperm = _permute_rows(win_g.reshape(NSHARD, SHW, D))
    small_g = small_g.reshape(NSHARD, 32, 768)
    convw = small_g[:, 0:8, :].transpose(1, 0, 2).reshape(8, 3 * D)
    w2f = small_g[:, 8:24, 0:128].transpose(1, 0, 2).reshape(16, 512)
    w2b = small_g[:, 8:24, 128:256].transpose(1, 0, 2).reshape(16, 512)
    w2f_pad = jnp.pad(w2f, ((32, 80), (0, 0)))
    w2b_pad = jnp.pad(w2b, ((48, 64), (0, 0)))
    alog_row = jnp.pad(jnp.concatenate([a_log_fwd, a_log_bwd], axis=1), ((0, 0), (0, 112)))
    dt_row = jnp.pad(jnp.concatenate([dt_bias_fwd, dt_bias_bwd], axis=1), ((0, 0), (0, 112)))

    p, h = _inproj(x2, ln_pre_w + proj_started[4][0:1, 0:1], wperm)
    qn, kn, vc = (_qkv_fwd(p, convw, kind) for kind in range(3))
    gsm, gk = _gates_fwd(p, alog_row, dt_row, w2f_pad, gk_b2_fwd, w2b_pad, gk_b2_bwd)
    g2, b2 = _gcum_fwd(gsm)
    u, w, at, qd, kd, el, tinv = _gdn_intra_fwd(qn, kn, vc, g2, b2)
    oa2, sa = _gdn_scan_fwd(u, w, at, qd, kd, el)
    qg, kdb, intra, elb = _gla_intra_fwd(p, gk)
    ob2, sb = _gla_scan_fwd(p, qg, kdb, intra, elb)

    (proj_land,) = _wait_copies(_proj_copies, proj_started, ob2, "gather_proj_wait")
    mine = 2 * lax.axis_index("x") + lax.axis_index("y")
    w3 = lax.dynamic_update_index_in_dim(proj_land, proj_l, mine, 0).reshape(NSHARD, 3, D // NSHARD, D)
    (loss8, doa, dob, dz, dgb, dga, dgB, dyres, dw3, dgdn_w, dgla_w, dlnpost) = _post(
        oa2, ob2, p, x2, tgt, gdn_norm_w, gla_norm_w, ln_post_w, w3)

    du, dw, dat, dqd, dkd, del_ = _gdn_scan_bwd(u, w, at, qd, kd, el, sa, doa)
    dqn, dkn, dvc, dg2, db2 = _gdn_intra_bwd(qn, kn, vc, g2, b2, tinv, du, dw, dat, dqd, dkd, del_)
    dgsm = _gcum_bwd(gsm, dg2, db2)
    dqg, dkdb, dvs, delb = _gla_scan_bwd(p, qg, kdb, elb, sb, dob)
    dqb, dkb, dvb, dgk = _gla_intra_bwd(p, gk, dqg, dkdb, dvs, delb, dob)
    (dps, dalog8, ddt8, dw2f_pad, db2f8, dw2b_pad, db2b8) = _gates_bwd(
        p, alog_row, dt_row, w2f_pad, gk_b2_fwd, w2b_pad, gk_b2_bwd, dgsm, dgk)
    dpre, dconv = zip(*[_qkv_bwd(p, convw, g, kind) for kind, g in enumerate((dqn, dkn, dvc))])

    pieces = (jnp.concatenate([a.astype(BF16) for a in (*dpre, dz, dqb, dkb, dvb, dgb, dga, dgB, dps)], axis=1),)
    dwperm = _inproj_dw(h, pieces)

    g_in = _unpermute_rows(dwperm).reshape(NSHARD, 2, SHW // 2, D)
    g_proj = dw3.reshape(NSHARD, 2, 384, D)
    dconv_full = jnp.concatenate(dconv, axis=1)
    dw2f, dw2b = dw2f_pad[32:48], dw2b_pad[48:64]
    g_small = jnp.stack([_pack_shard_small(dconv_full[0:5, 768 * s:768 * (s + 1)],
                                           dw2f[:, 128 * s:128 * (s + 1)], dw2b[:, 128 * s:128 * (s + 1)])
                         for s in range(NSHARD)])
    g_small = g_small.reshape(NSHARD, 2, 16, 768)
    parts = [g_in, g_proj, g_small]
    got = _swap_halves(parts)
    sums = [_sum_cast(a, b) for a, b in zip(parts, got)]
    hbs = [hb for _, hb in sums]
    started = _start_copies(_scatter_copies, 3 * len(hbs), hbs,
                            [lax.empty((3,) + a.shape[1:], a.dtype) for a in hbs], "scatter_start")
    dx, dlnpre8 = _inproj_dx(pieces, wperm, x2, ln_pre_w + started[4][0:1, 0:1], dyres)

    gsmall = _allreduce_small(jnp.concatenate(
        [dlnpre8, dalog8, ddt8, dgdn_w, db2f8, db2b8, dgla_w, dlnpost, loss8], axis=1))
    smalls = dict(ln_pre_w=(ln_pre_w, m_ln_pre_w, v_ln_pre_w), a_log_fwd=(a_log_fwd, m_a_log_fwd, v_a_log_fwd),
                  a_log_bwd=(a_log_bwd, m_a_log_bwd, v_a_log_bwd),
                  dt_bias_fwd=(dt_bias_fwd, m_dt_bias_fwd, v_dt_bias_fwd),
                  dt_bias_bwd=(dt_bias_bwd, m_dt_bias_bwd, v_dt_bias_bwd),
                  gdn_norm_w=(gdn_norm_w, m_gdn_norm_w, v_gdn_norm_w),
                  gk_b2_fwd=(gk_b2_fwd, m_gk_b2_fwd, v_gk_b2_fwd), gk_b2_bwd=(gk_b2_bwd, m_gk_b2_bwd, v_gk_b2_bwd),
                  gla_norm_w=(gla_norm_w, m_gla_norm_w, v_gla_norm_w), ln_post_w=(ln_post_w, m_ln_post_w, v_ln_post_w))
    names = [name for name, _, _, _ in SMALL_SLOTS]
    small = _adam_small(gsmall, *([smalls[n][i] for n in names] for i in range(3)))

    landed = _wait_copies(_scatter_copies, started, small["ln_pre_w"][1], "scatter_wait")
    halves = [_sum4(f, g) for (f, _), g in zip(sums, landed)]
    theirs = _join_halves(halves)

    a_in = [a.T for a in _adam(w_in[0].T, halves[0], theirs[0], m_w_in[0].T, v_w_in[0].T, 232, name="in")]
    a_pr = [_adam(w[0], halves[1], theirs[1], m[0], v[0], 128, tile0=2 * i, name=f"proj{i}")
            for i, (w, m, v) in enumerate(((w_proj_gdn, m_w_proj_gdn, v_w_proj_gdn),
                                           (w_proj_gla, m_w_proj_gla, v_w_proj_gla), (w_out, m_w_out, v_w_out)))]
    a_ss = _adam(_pack_shard_small(conv_w[0], gk_w2_fwd[0], gk_w2_bwd[0]), halves[2], theirs[2],
                 _pack_shard_small(m_conv_w[0], m_gk_w2_fwd[0], m_gk_w2_bwd[0]),
                 _pack_shard_small(v_conv_w[0], v_gk_w2_fwd[0], v_gk_w2_bwd[0]), 16, name="small")

    def family(k):
        conv, w2f_, w2b_ = _unpack_shard_small(a_ss[k])
        s = {n: small[n][k] for n in names}
        return [s["ln_pre_w"], a_in[k][None], conv[None], s["a_log_fwd"], s["a_log_bwd"], s["dt_bias_fwd"],
                s["dt_bias_bwd"], s["gdn_norm_w"], a_pr[0][k][None], w2f_[None], s["gk_b2_fwd"], w2b_[None],
                s["gk_b2_bwd"], s["gla_norm_w"], a_pr[1][k][None], a_pr[2][k][None], s["ln_post_w"]]

    return (gsmall[0, SMALL_W - 128], dx[None], *family(0), *family(1), *family(2), *family(3))
```

```python
import functools

import jax
import jax.numpy as jnp
from jax import lax
from jax.experimental import pallas as pl
from jax.experimental.pallas import tpu as pltpu

F32 = jnp.float32
BF16 = jnp.bfloat16
HI = lax.Precision.HIGHEST
MESH = pl.DeviceIdType.MESH

D = 1024
CH = 64
EPS = 1e-6
NA, DA = 8, 128
NB, DKB, DVB = 4, 128, 256
NSHARD = 4
SHW = 2320
NPERM = 9728
PS_BLOCK = 72
LR, B1, B2, ADAM_EPS, WD, STEP = 0.001, 0.9, 0.999, 1e-08, 0.01, 10

ANY = pl.BlockSpec(memory_space=pl.ANY)


def _cparams(sem=None, vmem_mb=48):
    return pltpu.CompilerParams(dimension_semantics=sem, vmem_limit_bytes=vmem_mb << 20)


def _bdot(a, b, ca, cb):
    return lax.dot_general(a.astype(BF16), b.astype(BF16), (((ca,), (cb,)), ((), ())),
                           preferred_element_type=F32)


@jax.custom_vjp
def mm(a, b):
    return _bdot(a, b, 1, 0)


def _mm_fwd(a, b):
    return _bdot(a, b, 1, 0), (a, b)


def _mm_bwd(res, g):
    a, b = res
    return _bdot(g, b, 1, 1), _bdot(a, g, 0, 0)


mm.defvjp(_mm_fwd, _mm_bwd)


@jax.custom_vjp
def mm_nt(a, b):
    return _bdot(a, b, 1, 1)


def _mm_nt_fwd(a, b):
    return _bdot(a, b, 1, 1), (a, b)


def _mm_nt_bwd(res, g):
    a, b = res
    return _bdot(g, b, 1, 0), _bdot(g, a, 0, 0)


mm_nt.defvjp(_mm_nt_fwd, _mm_nt_bwd)


@jax.custom_vjp
def mm_tn(a, b):
    return _bdot(a, b, 0, 0)


def _mm_tn_fwd(a, b):
    return _bdot(a, b, 0, 0), (a, b)


def _mm_tn_bwd(res, g):
    a, b = res
    return _bdot(b, g, 1, 1), _bdot(a, g, 1, 0)


mm_tn.defvjp(_mm_tn_fwd, _mm_tn_bwd)


def dot_hi(a, b):
    return lax.dot_general(a, b, (((1,), (0,)), ((), ())), precision=HI, preferred_element_type=F32)


def _split3(x):
    x1 = x.astype(BF16)
    r = x - x1.astype(F32)
    x2 = r.astype(BF16)
    return x1, x2, (r - x2.astype(F32)).astype(BF16)


def _cdot(c, x, cc, cx, c_first=True):
    parts = _split3(x)
    if c_first:
        return _bdot(c, parts[0], cc, cx) + _bdot(c, parts[1], cc, cx) + _bdot(c, parts[2], cc, cx)
    return _bdot(parts[0], c, cx, cc) + _bdot(parts[1], c, cx, cc) + _bdot(parts[2], c, cx, cc)


@jax.custom_vjp
def cmm(c, x):
    return _cdot(c, x, 1, 0)


def _cmm_fwd(c, x):
    return _cdot(c, x, 1, 0), c


def _cmm_bwd(c, g):
    return jnp.zeros_like(c), _cdot(c, g, 0, 0)


cmm.defvjp(_cmm_fwd, _cmm_bwd)


@jax.custom_vjp
def mmc(x, c):
    return _cdot(c, x, 0, 1, c_first=False)


def _mmc_fwd(x, c):
    return _cdot(c, x, 0, 1, c_first=False), c


def _mmc_bwd(c, g):
    return _cdot(c, g, 1, 1, c_first=False), jnp.zeros_like(c)


mmc.defvjp(_mmc_fwd, _mmc_bwd)


def _sigmoid(x):
    return 1.0 / (1.0 + jnp.exp(-x))


def _silu(x):
    return x * _sigmoid(x)


def _softplus(x):
    return jnp.maximum(x, 0.0) + jnp.log(1.0 + jnp.exp(-jnp.abs(x)))


def _rms(x, w):
    return x * lax.rsqrt(jnp.mean(x * x, axis=-1, keepdims=True) + EPS) * w


SC = 256


class _Consts:
    def __init__(self, rev):
        r = lax.broadcasted_iota(jnp.int32, (SC, SC), 0)
        c = lax.broadcasted_iota(jnp.int32, (SC, SC), 1)
        same = (r >> 6) == (c >> 6)
        a = jnp.where(rev, c, r)
        b = jnp.where(rev, r, c)
        self.incl = same & (a >= b)
        self.strict = same & (a > b)
        self.incl_f = self.incl.astype(F32)
        self.eye = (r == c).astype(F32)
        rows = lax.broadcasted_iota(jnp.int32, (SC, 1), 0)
        self.last_col = ((rows & (CH - 1)) == jnp.where(rev, 0, CH - 1)).astype(F32)
        rr = lax.broadcasted_iota(jnp.int32, (SC, CH), 0)
        cc = lax.broadcasted_iota(jnp.int32, (SC, CH), 1)
        self.fold = ((rr & (CH - 1)) == cc).astype(F32)


def _dot3(a, b, ca=1, cb=0):
    ah, bh = a.astype(BF16), b.astype(BF16)
    al, bl = (a - ah.astype(F32)).astype(BF16), (b - bh.astype(F32)).astype(BF16)
    return _bdot(ah, bh, ca, cb) + (_bdot(ah, bl, ca, cb) + _bdot(al, bh, ca, cb))


TRI_SPLIT_LEVELS = 2


def _tri_inv(low, eye):
    n = -low
    acc = eye + n
    p = n
    for level in range(5):
        dot = _dot3 if level < TRI_SPLIT_LEVELS else (lambda a, b: _bdot(a, b, 1, 0))
        p = dot(p, p)
        acc = acc + dot(acc, p)
    return acc


@jax.custom_vjp
def _solve2(low, rv, rk, tinv):
    x = _dot3(tinv, jnp.concatenate([rv, rk], axis=1))
    return x[:, :DA], x[:, DA:]


def _solve2_fwd(low, rv, rk, tinv):
    x = _dot3(tinv, jnp.concatenate([rv, rk], axis=1))
    return (x[:, :DA], x[:, DA:]), (x, tinv)


def _solve2_bwd(res, g):
    x, tinv = res
    drhs = _dot3(tinv, jnp.concatenate(g, axis=1), 0, 0)
    return -_dot3(drhs, x, 1, 1), drhs[:, :DA], drhs[:, DA:], jnp.zeros_like(tinv)


_solve2.defvjp(_solve2_fwd, _solve2_bwd)


def _chunk_last(x, cs):
    xs = (x * cs.last_col).reshape(SC // CH, CH, x.shape[1])
    return jnp.broadcast_to(jnp.sum(xs, axis=1, keepdims=True), xs.shape).reshape(x.shape)


def _gdn_decay(g, cs):
    gw = jnp.concatenate([g] * (SC // DA), axis=1)
    grow = jnp.sum(cs.eye * gw, axis=0, keepdims=True)
    return jnp.where(cs.incl, jnp.exp(jnp.where(cs.incl, gw - grow, 0.0)), 0.0)


def _gdn_intra(q, k, v, g, bx, tinv, cs):
    decay = _gdn_decay(g, cs)
    kb = k * bx
    low = jnp.where(cs.strict, mm_nt(kb, k) * decay, 0.0)
    eg = jnp.exp(g)
    made = tinv is None
    if made:
        tinv = _tri_inv(low, cs.eye)
    u, w = _solve2(low, v * bx, kb * eg, tinv)
    attn = mmc(mm_nt(q, k) * decay, cs.fold)
    qd = q * eg
    glast = _chunk_last(g, cs)
    kd = k * jnp.exp(glast - g)
    outs = (u, w, attn, qd, kd, jnp.exp(glast))
    return outs + (tinv,) if made else outs


def _gdn_scan(u, w, attn, qd, kd, el, s):
    vn = u - mm(w, s)
    o = mm(qd, s) + mm(attn, vn)
    sn = s * el + mm_tn(kd, vn)
    return o, sn


def _gla_intra(q, k, v, gk, cs):
    gc = cmm(cs.incl_f, gk)
    qg = q * (DKB ** -0.5) * jnp.exp(gc)
    kg = k * jnp.exp(-gc)
    attn = jnp.where(cs.incl, mm_nt(qg, kg), 0.0)
    intra = mm(attn, v)
    glast = _chunk_last(gc, cs)
    kd = k * jnp.exp(glast - gc)
    return qg, kd, intra, jnp.exp(glast)


def _gla_scan(qg, kd, v, el, st):
    o = mm_nt(qg, st)
    stn = st * el + mm_tn(v, kd)
    return o, stn


def _shift_rows(x, s):
    if s == 0:
        return x
    t = x.shape[0]
    rolled = pltpu.roll(x, (-s) % t, 0)
    rows = lax.broadcasted_iota(jnp.int32, x.shape, 0)
    return jnp.where((rows + s >= 0) & (rows + s < t), rolled, 0.0)


@jax.custom_vjp
def _conv5(x, w):
    acc = w[0:1] * _shift_rows(x, -2)
    for j in range(1, 5):
        acc = acc + w[j:j + 1] * _shift_rows(x, j - 2)
    return acc


def _conv5_fwd(x, w):
    return _conv5(x, w), (x, w)


def _conv5_bwd(res, g):
    x, w = res
    dx = w[0:1] * _shift_rows(g, 2)
    for j in range(1, 5):
        dx = dx + w[j:j + 1] * _shift_rows(g, 2 - j)
    rows = lax.broadcasted_iota(jnp.int32, w.shape, 0)
    dw = jnp.zeros_like(w)
    for j in range(5):
        dwj = jnp.sum(g * _shift_rows(x, j - 2), axis=0, keepdims=True)
        dw = dw + jnp.where(rows == j, dwj, 0.0)
    return dx, dw


_conv5.defvjp(_conv5_fwd, _conv5_bwd)


def _qkv_act(kind):
    def f(x, w):
        c = _silu(_conv5(x, w))
        if kind == 2:
            return c
        c = c * lax.rsqrt(jnp.sum(c * c, axis=-1, keepdims=True) + EPS)
        return c * (DA ** -0.5) if kind == 0 else c
    return f


def _inproj(x, lnw, wperm, tn=512):
    t = x.shape[0]
    tm = min(t, 1024)

    def body(x_ref, lnw_ref, w_ref, p_ref, h_ref, hbuf):
        @pl.when(pl.program_id(1) == 0)
        def _():
            hb = _rms(x_ref[...], lnw_ref[...]).astype(BF16)
            hbuf[...] = hb
            h_ref[...] = hb
        p_ref[...] = _bdot(hbuf[...], w_ref[...], 1, 1)

    return pl.pallas_call(
        body, name="inproj", grid=(t // tm, NPERM // tn),
        in_specs=[pl.BlockSpec((tm, D), lambda i, j: (i, 0)),
                  pl.BlockSpec((1, D), lambda i, j: (0, 0)),
                  pl.BlockSpec((tn, D), lambda i, j: (j, 0))],
        out_specs=[pl.BlockSpec((tm, tn), lambda i, j: (i, j)),
                   pl.BlockSpec((tm, D), lambda i, j: (i, 0))],
        out_shape=[jax.ShapeDtypeStruct((t, NPERM), F32),
                   jax.ShapeDtypeStruct((t, D), BF16)],
        scratch_shapes=[pltpu.VMEM((tm, D), BF16)],
        compiler_params=_cparams(("parallel", "arbitrary")),
    )(x, lnw, wperm)


DP_TILE = 512
DP_PIECES = ((0, 19),)


def _piece_specs(tm, j_first):
    specs = []
    for j0, n in DP_PIECES:
        def imap(a, b, j0=j0, n=n):
            j, i = (a, b) if j_first else (b, a)
            inside = (j >= j0) & (j < j0 + n)
            return jnp.where(inside, i, 0), jnp.clip(j - j0, 0, n - 1)
        specs.append(pl.BlockSpec((tm, DP_TILE), imap))
    return specs


def _for_piece(j, refs, fn):
    for (j0, n), ref in zip(DP_PIECES, refs):
        @pl.when((j >= j0) & (j < j0 + n))
        def _(ref=ref):
            fn(ref[...])


def _inproj_dw(h, pieces):
    t = h.shape[0]
    tm = min(t, 1024)
    npc = len(pieces)

    def body(h_ref, *refs):
        dw_ref = refs[npc]

        @pl.when(pl.program_id(1) == 0)
        def _():
            dw_ref[...] = jnp.zeros_like(dw_ref)

        def add(dp):
            dw_ref[...] += _bdot(dp, h_ref[...], 0, 0)
        _for_piece(pl.program_id(0), refs[:npc], add)

    return pl.pallas_call(
        body, name="inproj_dw", grid=(NPERM // DP_TILE, t // tm),
        in_specs=[pl.BlockSpec((tm, D), lambda j, i: (i, 0))] + _piece_specs(tm, True),
        out_specs=pl.BlockSpec((DP_TILE, D), lambda j, i: (j, 0)),
        out_shape=jax.ShapeDtypeStruct((NPERM, D), F32),
        compiler_params=_cparams(("parallel", "arbitrary")),
    )(h, *pieces)


def _inproj_dx(pieces, wperm, x, lnw, dyres):
    t = x.shape[0]
    tm = min(t, 1024)
    tn = DP_TILE
    nj = NPERM // tn
    npc = len(pieces)

    def body(*refs):
        w_ref, x_ref, lnw_ref, dy_ref, dx_ref, dlnw_ref, acc = refs[npc:]
        j = pl.program_id(1)

        @pl.when(j == 0)
        def _():
            acc[...] = jnp.zeros_like(acc)

        def add(dp):
            acc[...] += _bdot(dp, w_ref[...], 1, 0)
        _for_piece(j, refs[:npc], add)

        @pl.when(j == nj - 1)
        def _():
            _, vjp = jax.vjp(_rms, x_ref[...], lnw_ref[...])
            dx, dlnw = vjp(acc[...])
            dx_ref[...] = dx + dy_ref[...]

            @pl.when(pl.program_id(0) == 0)
            def _():
                dlnw_ref[...] = jnp.zeros_like(dlnw_ref)
            dlnw_ref[...] += jnp.broadcast_to(dlnw, dlnw_ref.shape)

    return pl.pallas_call(
        body, name="inproj_dx", grid=(t // tm, nj),
        in_specs=_piece_specs(tm, False) + [
                  pl.BlockSpec((tn, D), lambda i, j: (j, 0)),
                  pl.BlockSpec((tm, D), lambda i, j: (i, 0)),
                  pl.BlockSpec((1, D), lambda i, j: (0, 0)),
                  pl.BlockSpec((tm, D), lambda i, j: (i, 0))],
        out_specs=[pl.BlockSpec((tm, D), lambda i, j: (i, 0)),
                   pl.BlockSpec((8, D), lambda i, j: (0, 0))],
        out_shape=[jax.ShapeDtypeStruct((t, D), F32), jax.ShapeDtypeStruct((8, D), F32)],
        scratch_shapes=[pltpu.VMEM((tm, D), F32)],
        compiler_params=_cparams(("arbitrary", "arbitrary")),
    )(*pieces, wperm, x, lnw, dyres)


def _qkv_fwd(p, convw, kind):
    t = p.shape[0]
    f = _qkv_act(kind)

    def body(p_ref, w_ref, o_ref):
        o_ref[...] = f(p_ref[...], w_ref[...])

    return pl.pallas_call(
        body, name=f"qkv_fwd{kind}", grid=(NA,),
        in_specs=[pl.BlockSpec((t, DA), lambda h: (0, kind * NA + h)),
                  pl.BlockSpec((8, DA), lambda h: (0, kind * NA + h))],
        out_specs=pl.BlockSpec((t, DA), lambda h: (0, h)),
        out_shape=jax.ShapeDtypeStruct((t, D), F32),
        compiler_params=_cparams(("parallel",)),
    )(p, convw)


def _qkv_bwd(p, convw, dout, kind):
    t = p.shape[0]
    f = _qkv_act(kind)

    def body(p_ref, w_ref, g_ref, dx_ref, dw_ref):
        _, vjp = jax.vjp(f, p_ref[...], w_ref[...])
        dx, dw = vjp(g_ref[...])
        dx_ref[...] = dx.astype(BF16)
        dw_ref[...] = dw

    return pl.pallas_call(
        body, name=f"qkv_bwd{kind}", grid=(NA,),
        in_specs=[pl.BlockSpec((t, DA), lambda h: (0, kind * NA + h)),
                  pl.BlockSpec((8, DA), lambda h: (0, kind * NA + h)),
                  pl.BlockSpec((t, DA), lambda h: (0, h))],
        out_specs=[pl.BlockSpec((t, DA), lambda h: (0, h)),
                   pl.BlockSpec((8, DA), lambda h: (0, h))],
        out_shape=[jax.ShapeDtypeStruct((t, D), BF16), jax.ShapeDtypeStruct((8, D), F32)],
        compiler_params=_cparams(("parallel",)),
    )(p, convw, dout)


def _gates_f(ps, alog_row, dt_row, w2f, b2f, w2b, b2b):
    lane = lax.broadcasted_iota(jnp.int32, ps.shape, 1)
    lg = -jnp.exp(alog_row) * _softplus(ps + dt_row)
    gsm = jnp.where(lane < 16, lg, jnp.where(lane < 32, _sigmoid(ps), 0.0))
    gkf = -_softplus(-(mm(ps, w2f) + b2f)) * (1.0 / 16.0)
    gkb = -_softplus(-(mm(ps, w2b) + b2b)) * (1.0 / 16.0)
    return gsm, gkf, gkb


def _gates_fwd(ps, alog_row, dt_row, w2f, b2f, w2b, b2b, tm=512):
    t = ps.shape[0]

    def body(ps_ref, a_ref, d_ref, wf_ref, bf_ref, wb_ref, bb_ref, gsm_ref, gk_ref):
        gsm, gkf, gkb = _gates_f(ps_ref[...], a_ref[...], d_ref[...], wf_ref[...], bf_ref[...],
                                 wb_ref[...], bb_ref[...])
        gsm_ref[...] = gsm
        gk_ref[0] = gkf
        gk_ref[1] = gkb

    row = lambda n: pl.BlockSpec((1, n), lambda i: (0, 0))
    mat = pl.BlockSpec((128, 512), lambda i: (0, 0))
    return pl.pallas_call(
        body, name="gates_fwd", grid=(t // tm,),
        in_specs=[pl.BlockSpec((tm, 128), lambda i: (i, PS_BLOCK)), row(128), row(128), mat, row(512), mat, row(512)],
        out_specs=[pl.BlockSpec((tm, 128), lambda i: (i, 0)),
                   pl.BlockSpec((2, tm, 512), lambda i: (0, i, 0))],
        out_shape=[jax.ShapeDtypeStruct((t, 128), F32), jax.ShapeDtypeStruct((2, t, 512), F32)],
        compiler_params=_cparams(("parallel",)),
    )(ps, alog_row, dt_row, w2f, b2f, w2b, b2b)


def _gates_bwd(ps, alog_row, dt_row, w2f, b2f, w2b, b2b, dgsm, dgk, tm=512):
    t = ps.shape[0]

    def body(ps_ref, a_ref, d_ref, wf_ref, bf_ref, wb_ref, bb_ref, dgsm_ref, dgk_ref,
             dps_ref, da_ref, dd_ref, dwf_ref, dbf_ref, dwb_ref, dbb_ref):
        _, vjp = jax.vjp(_gates_f, ps_ref[...], a_ref[...], d_ref[...], wf_ref[...], bf_ref[...],
                         wb_ref[...], bb_ref[...])
        dps, da, dd, dwf, dbf, dwb, dbb = vjp((dgsm_ref[...], dgk_ref[0], dgk_ref[1]))
        dps_ref[:, 0:128] = dps.astype(BF16)
        dps_ref[:, 128:DP_TILE] = jnp.zeros((tm, DP_TILE - 128), BF16)
        accs = ((da_ref, da), (dd_ref, dd), (dwf_ref, dwf), (dbf_ref, dbf), (dwb_ref, dwb), (dbb_ref, dbb))

        @pl.when(pl.program_id(0) == 0)
        def _():
            for ref, _ in accs:
                ref[...] = jnp.zeros_like(ref)
        for ref, val in accs:
            ref[...] += jnp.broadcast_to(val, ref.shape)

    row = lambda n: pl.BlockSpec((1, n), lambda i: (0, 0))
    row8 = lambda n: pl.BlockSpec((8, n), lambda i: (0, 0))
    mat = pl.BlockSpec((128, 512), lambda i: (0, 0))
    return pl.pallas_call(
        body, name="gates_bwd", grid=(t // tm,),
        in_specs=[pl.BlockSpec((tm, 128), lambda i: (i, PS_BLOCK)), row(128), row(128), mat, row(512), mat, row(512),
                  pl.BlockSpec((tm, 128), lambda i: (i, 0)),
                  pl.BlockSpec((2, tm, 512), lambda i: (0, i, 0))],
        out_specs=[pl.BlockSpec((tm, DP_TILE), lambda i: (i, 0)), row8(128), row8(128), mat, row8(512), mat,
                   row8(512)],
        out_shape=[jax.ShapeDtypeStruct((t, DP_TILE), BF16),
                   jax.ShapeDtypeStruct((8, 128), F32), jax.ShapeDtypeStruct((8, 128), F32),
                   jax.ShapeDtypeStruct((128, 512), F32), jax.ShapeDtypeStruct((8, 512), F32),
                   jax.ShapeDtypeStruct((128, 512), F32), jax.ShapeDtypeStruct((8, 512), F32)],
        compiler_params=_cparams(("arbitrary",)),
    )(ps, alog_row, dt_row, w2f, b2f, w2b, b2b, dgsm, dgk)


def _rows(i):
    return pl.ds(pl.multiple_of(i * CH, CH), CH)


def _srows(i):
    return pl.ds(pl.multiple_of(i * SC, SC), SC)


def _first_row(x):
    row = lax.broadcasted_iota(jnp.int32, (8, x.shape[1]), 0)
    return jnp.where(row == 0, jnp.broadcast_to(x, (8, x.shape[1])), 0.0)


def _chunk_rows(e_ref, i):
    pad = jnp.zeros((CH - 8, 128), F32)
    return jnp.concatenate([x for c in range(SC // CH) for x in (e_ref[(SC // CH) * i + c], pad)], axis=0)


def _gcum_f(gsm, tm):
    i = lax.broadcasted_iota(jnp.int32, (tm, tm), 0)
    j = lax.broadcasted_iota(jnp.int32, (tm, tm), 1)
    same = (i >> 6) == (j >> 6)
    lower = (same & (i >= j)).astype(F32)
    upper = (same & (i <= j)).astype(F32)
    r = lax.broadcasted_iota(jnp.int32, (128, D), 0)
    head = lax.broadcasted_iota(jnp.int32, (128, D), 1) >> 7
    pick = lambda off: (r == head + off).astype(F32)
    lane = lax.broadcasted_iota(jnp.int32, gsm.shape, 1)
    run = jnp.where(lane < 8, cmm(lower, gsm), cmm(upper, gsm))
    return mmc(run, pick(0)), mmc(run, pick(8)), mmc(gsm, pick(16)), mmc(gsm, pick(24))


def _gcum_fwd(gsm, tm=256):
    t = gsm.shape[0]

    def body(s_ref, g_ref, b_ref):
        gf, gb, bf, bb = _gcum_f(s_ref[...], tm)
        g_ref[0] = gf
        g_ref[1] = gb
        b_ref[0] = bf
        b_ref[1] = bb

    two = pl.BlockSpec((2, tm, D), lambda i: (0, i, 0))
    return pl.pallas_call(
        body, name="gcum_fwd", grid=(t // tm,),
        in_specs=[pl.BlockSpec((tm, 128), lambda i: (i, 0))], out_specs=[two, two],
        out_shape=[jax.ShapeDtypeStruct((2, t, D), F32)] * 2,
        compiler_params=_cparams(("parallel",)),
    )(gsm)


def _gcum_bwd(gsm, dg2, db2, tm=256):
    t = gsm.shape[0]

    def body(s_ref, dg_ref, db_ref, ds_ref):
        _, vjp = jax.vjp(lambda s: _gcum_f(s, tm), s_ref[...])
        ds_ref[...] = vjp((dg_ref[0], dg_ref[1], db_ref[0], db_ref[1]))[0]

    two = pl.BlockSpec((2, tm, D), lambda i: (0, i, 0))
    tile = pl.BlockSpec((tm, 128), lambda i: (i, 0))
    return pl.pallas_call(
        body, name="gcum_bwd", grid=(t // tm,),
        in_specs=[tile, two, two], out_specs=tile,
        out_shape=jax.ShapeDtypeStruct((t, 128), F32),
        compiler_params=_cparams(("parallel",)),
    )(gsm, dg2, db2)


def _gdn_intra_fwd(qn, kn, vc, g2, b2):
    t = qn.shape[0]
    n = t // CH

    def body(q_ref, k_ref, v_ref, g_ref, b_ref, u_ref, w_ref, a_ref, qd_ref, kd_ref, e_ref, t_ref):
        cs = _Consts(pl.program_id(0) == 1)

        def step(i, carry):
            r = _srows(i)
            q, k, v, g, bx = q_ref[r, :], k_ref[r, :], v_ref[r, :], g_ref[r, :], b_ref[r, :]
            u, w, a, qd, kd, el, tinv = _gdn_intra(q, k, v, g, bx, None, cs)
            u_ref[r, :] = u
            w_ref[r, :] = w
            a_ref[r, :] = a
            qd_ref[r, :] = qd
            kd_ref[r, :] = kd
            t_ref[r, :] = tinv
            for c in range(SC // CH):
                e_ref[(SC // CH) * i + c] = el[c * CH:c * CH + 8]
            return carry

        lax.fori_loop(0, t // SC, step, 0)

    head = pl.BlockSpec((t, DA), lambda d, h: (0, h))
    dh = pl.BlockSpec((None, t, DA), lambda d, h: (d, 0, h))
    sq = lambda w: pl.BlockSpec((None, None, t, w), lambda d, h: (d, h, 0, 0))
    big = jax.ShapeDtypeStruct((2, t, D), F32)
    return pl.pallas_call(
        body, name="gdn_intra_fwd", grid=(2, NA),
        in_specs=[head, head, head, dh, dh],
        out_specs=[dh, dh, sq(CH), dh, dh, pl.BlockSpec((None, None, n, 8, 128), lambda d, h: (d, h, 0, 0, 0)),
                   sq(SC)],
        out_shape=[big, big, jax.ShapeDtypeStruct((2, NA, t, CH), F32), big, big,
                   jax.ShapeDtypeStruct((2, NA, n, 8, 128), F32), jax.ShapeDtypeStruct((2, NA, t, SC), F32)],
        compiler_params=_cparams(("parallel", "parallel")),
    )(qn, kn, vc, g2, b2)


SCAN_TB = 256
SCAN_HB = 8


def _scan_specs(t, width, nheads, hb, along):
    nt = t // SCAN_TB
    nb = SCAN_TB // CH

    def tmap(d, tt):
        fwd = tt + d * (nt - 1 - 2 * tt)
        return fwd if along > 0 else nt - 1 - fwd

    tok = pl.BlockSpec((None, SCAN_TB, hb * width), lambda d, h, tt: (d, tmap(d, tt), h))
    per = lambda *tail: pl.BlockSpec((None, hb, nb) + tail, lambda d, h, tt: (d, h, tmap(d, tt)) + (0,) * len(tail))
    sq = pl.BlockSpec((None, hb, SCAN_TB, CH), lambda d, h, tt: (d, h, tmap(d, tt), 0))
    shared = lambda w: pl.BlockSpec((SCAN_TB, hb * w), lambda d, h, tt: (tmap(d, tt), h))
    return tok, per, sq, shared, (2, nheads // hb, nt), nb


def _gdn_scan_fwd(u, w, a, qd, kd, e):
    t = u.shape[1]
    tok, per, sq, _, grid, nb = _scan_specs(t, DA, NA, SCAN_HB, +1)

    def body(u_ref, w_ref, a_ref, qd_ref, kd_ref, e_ref, o_ref, s_ref, state):
        rev = pl.program_id(0) == 1

        @pl.when(pl.program_id(2) == 0)
        def _():
            state[...] = jnp.zeros_like(state)

        def step(i, ss):
            ci = jnp.where(rev, nb - 1 - i, i)
            r = _rows(ci)
            out = []
            for hh, s in enumerate(ss):
                c = slice(hh * DA, (hh + 1) * DA)
                s_ref[hh, ci] = s
                o, sn = _gdn_scan(u_ref[r, c], w_ref[r, c], a_ref[hh, r, :], qd_ref[r, c], kd_ref[r, c],
                                  e_ref[hh, ci][0:1], s)
                o_ref[r, c] = o
                out.append(sn)
            return tuple(out)

        ss = lax.fori_loop(0, nb, step, tuple(state[hh] for hh in range(SCAN_HB)))
        for hh, s in enumerate(ss):
            state[hh] = s

    return pl.pallas_call(
        body, name="gdn_scan_fwd", grid=grid,
        in_specs=[tok, tok, sq, tok, tok, per(8, 128)],
        out_specs=[tok, per(DA, DA)],
        out_shape=[jax.ShapeDtypeStruct((2, t, D), F32), jax.ShapeDtypeStruct((2, NA, t // CH, DA, DA), F32)],
        scratch_shapes=[pltpu.VMEM((SCAN_HB, DA, DA), F32)],
        compiler_params=_cparams(("parallel", "parallel", "arbitrary")),
    )(u, w, a, qd, kd, e)


def _gdn_scan_bwd(u, w, a, qd, kd, e, ssave, do):
    t = u.shape[1]
    tok, per, sq, shared, grid, nb = _scan_specs(t, DA, NA, SCAN_HB, -1)

    def body(u_ref, w_ref, a_ref, qd_ref, kd_ref, e_ref, s_ref, do_ref,
             du_ref, dw_ref, da_ref, dqd_ref, dkd_ref, de_ref, state):
        rev = pl.program_id(0) == 1

        @pl.when(pl.program_id(2) == 0)
        def _():
            state[...] = jnp.zeros_like(state)

        def step(i, dss):
            ci = jnp.where(rev, i, nb - 1 - i)
            r = _rows(ci)
            out = []
            for hh, ds in enumerate(dss):
                c = slice(hh * DA, (hh + 1) * DA)
                _, vjp = jax.vjp(_gdn_scan, u_ref[r, c], w_ref[r, c], a_ref[hh, r, :], qd_ref[r, c], kd_ref[r, c],
                                 e_ref[hh, ci][0:1], s_ref[hh, ci])
                du, dw, da, dqd, dkd, de, dsn = vjp((do_ref[r, c], ds))
                du_ref[r, c] = du
                dw_ref[r, c] = dw
                da_ref[hh, r, :] = da
                dqd_ref[r, c] = dqd
                dkd_ref[r, c] = dkd
                de_ref[hh, ci] = _first_row(de)
                out.append(dsn)
            return tuple(out)

        dss = lax.fori_loop(0, nb, step, tuple(state[hh] for hh in range(SCAN_HB)))
        for hh, ds in enumerate(dss):
            state[hh] = ds

    big = jax.ShapeDtypeStruct((2, t, D), F32)
    return pl.pallas_call(
        body, name="gdn_scan_bwd", grid=grid,
        in_specs=[tok, tok, sq, tok, tok, per(8, 128), per(DA, DA), shared(DA)],
        out_specs=[tok, tok, sq, tok, tok, per(8, 128)],
        out_shape=[big, big, jax.ShapeDtypeStruct((2, NA, t, CH), F32), big, big,
                   jax.ShapeDtypeStruct((2, NA, t // CH, 8, 128), F32)],
        scratch_shapes=[pltpu.VMEM((SCAN_HB, DA, DA), F32)],
        compiler_params=_cparams(("parallel", "parallel", "arbitrary")),
    )(u, w, a, qd, kd, e, ssave, do)


def _gdn_intra_bwd(qn, kn, vc, g2, b2, tinv, du, dw, da, dqd, dkd, de):
    t = qn.shape[0]
    n = t // CH

    def body(q_ref, k_ref, v_ref, g_ref, b_ref, t_ref, du_ref, dw_ref, da_ref, dqd_ref, dkd_ref, de_ref,
             dq_ref, dk_ref, dv_ref, dg_ref, db_ref):
        d = pl.program_id(1)
        cs = _Consts(d == 1)

        @pl.when(d == 0)
        def _():
            dq_ref[...] = jnp.zeros_like(dq_ref)
            dk_ref[...] = jnp.zeros_like(dk_ref)
            dv_ref[...] = jnp.zeros_like(dv_ref)

        def step(i, carry):
            r = _srows(i)
            tinv_c = t_ref[r, :]
            f = lambda q, k, v, g, bx: _gdn_intra(q, k, v, g, bx, tinv_c, cs)
            _, vjp = jax.vjp(f, q_ref[r, :], k_ref[r, :], v_ref[r, :], g_ref[r, :], b_ref[r, :])
            dq, dk, dv, dg, dbx = vjp((du_ref[r, :], dw_ref[r, :], da_ref[r, :], dqd_ref[r, :],
                                       dkd_ref[r, :], _chunk_rows(de_ref, i)))
            dq_ref[r, :] += dq
            dk_ref[r, :] += dk
            dv_ref[r, :] += dv
            dg_ref[r, :] = dg
            db_ref[r, :] = dbx
            return carry

        lax.fori_loop(0, t // SC, step, 0)

    head = pl.BlockSpec((t, DA), lambda h, d: (0, h))
    dh = pl.BlockSpec((None, t, DA), lambda h, d: (d, 0, h))
    sq = pl.BlockSpec((None, None, t, CH), lambda h, d: (d, h, 0, 0))
    tq = pl.BlockSpec((None, None, t, SC), lambda h, d: (d, h, 0, 0))
    full = jax.ShapeDtypeStruct((t, D), F32)
    big = jax.ShapeDtypeStruct((2, t, D), F32)
    return pl.pallas_call(
        body, name="gdn_intra_bwd", grid=(NA, 2),
        in_specs=[head, head, head, dh, dh, tq, dh, dh, sq, dh, dh,
                  pl.BlockSpec((None, None, n, 8, 128), lambda h, d: (d, h, 0, 0, 0))],
        out_specs=[head, head, head, dh, dh],
        out_shape=[full, full, full, big, big],
        compiler_params=_cparams(("arbitrary", "arbitrary")),
    )(qn, kn, vc, g2, b2, tinv, du, dw, da, dqd, dkd, de)


def _gla_specs(t, order):
    ix = (lambda d, h: (d, h)) if order == "dh" else (lambda h, d: (d, h))

    def mk(fn):
        return lambda a, b: fn(*ix(a, b))
    q = pl.BlockSpec((t, DKB), mk(lambda d, h: (0, 32 + h)))
    k = pl.BlockSpec((t, DKB), mk(lambda d, h: (0, 36 + h)))
    v = pl.BlockSpec((t, DVB), mk(lambda d, h: (0, 20 + h)))
    dk = pl.BlockSpec((None, t, DKB), mk(lambda d, h: (d, 0, h)))
    dv = pl.BlockSpec((None, t, DVB), mk(lambda d, h: (d, 0, h)))
    e = pl.BlockSpec((None, None, t // CH, 8, 128), mk(lambda d, h: (d, h, 0, 0, 0)))
    s = pl.BlockSpec((None, None, t // CH, DVB, DKB), mk(lambda d, h: (d, h, 0, 0, 0)))
    return q, k, v, dk, dv, e, s


def _gla_intra_fwd(p, gk):
    t = p.shape[0]
    n = t // CH

    def body(q_ref, k_ref, v_ref, g_ref, qg_ref, kd_ref, in_ref, e_ref):
        cs = _Consts(pl.program_id(0) == 1)

        def step(i, carry):
            r = _srows(i)
            qg, kd, intra, el = _gla_intra(q_ref[r, :], k_ref[r, :], v_ref[r, :], g_ref[r, :], cs)
            qg_ref[r, :] = qg
            kd_ref[r, :] = kd
            in_ref[r, :] = intra
            for c in range(SC // CH):
                e_ref[(SC // CH) * i + c] = el[c * CH:c * CH + 8]
            return carry

        lax.fori_loop(0, t // SC, step, 0)

    q, k, v, dk, dv, e, _ = _gla_specs(t, "dh")
    return pl.pallas_call(
        body, name="gla_intra_fwd", grid=(2, NB),
        in_specs=[q, k, v, dk], out_specs=[dk, dk, dv, e],
        out_shape=[jax.ShapeDtypeStruct((2, t, NB * DKB), F32), jax.ShapeDtypeStruct((2, t, NB * DKB), F32),
                   jax.ShapeDtypeStruct((2, t, D), F32), jax.ShapeDtypeStruct((2, NB, n, 8, 128), F32)],
        compiler_params=_cparams(("parallel", "parallel")),
    )(p, p, p, gk)


GLA_HB = 2


def _gla_v_spec(t, along):
    nt = t // SCAN_TB

    def tmap(d, tt):
        fwd = tt + d * (nt - 1 - 2 * tt)
        return fwd if along > 0 else nt - 1 - fwd

    return pl.BlockSpec((SCAN_TB, GLA_HB * DVB), lambda d, h, tt: (tmap(d, tt), 5120 // (GLA_HB * DVB) + h))


def _gla_scan_fwd(p, qg, kd, intra, e):
    t = p.shape[0]
    tokk, per, _, _, grid, nb = _scan_specs(t, DKB, NB, GLA_HB, +1)
    tokv = _scan_specs(t, DVB, NB, GLA_HB, +1)[0]

    def body(v_ref, qg_ref, kd_ref, in_ref, e_ref, o_ref, s_ref, state):
        rev = pl.program_id(0) == 1

        @pl.when(pl.program_id(2) == 0)
        def _():
            state[...] = jnp.zeros_like(state)

        def step(i, sts):
            ci = jnp.where(rev, nb - 1 - i, i)
            r = _rows(ci)
            out = []
            for hh, st in enumerate(sts):
                ck = slice(hh * DKB, (hh + 1) * DKB)
                cv = slice(hh * DVB, (hh + 1) * DVB)
                s_ref[hh, ci] = st
                o, stn = _gla_scan(qg_ref[r, ck], kd_ref[r, ck], v_ref[r, cv], e_ref[hh, ci][0:1], st)
                o_ref[r, cv] = o + in_ref[r, cv]
                out.append(stn)
            return tuple(out)

        sts = lax.fori_loop(0, nb, step, tuple(state[hh] for hh in range(GLA_HB)))
        for hh, st in enumerate(sts):
            state[hh] = st

    return pl.pallas_call(
        body, name="gla_scan_fwd", grid=grid,
        in_specs=[_gla_v_spec(t, +1), tokk, tokk, tokv, per(8, 128)], out_specs=[tokv, per(DVB, DKB)],
        out_shape=[jax.ShapeDtypeStruct((2, t, D), F32), jax.ShapeDtypeStruct((2, NB, t // CH, DVB, DKB), F32)],
        scratch_shapes=[pltpu.VMEM((GLA_HB, DVB, DKB), F32)],
        compiler_params=_cparams(("parallel", "parallel", "arbitrary")),
    )(p, qg, kd, intra, e)


def _gla_scan_bwd(p, qg, kd, e, ssave, do):
    t = p.shape[0]
    tokk, per, _, shared, grid, nb = _scan_specs(t, DKB, NB, GLA_HB, -1)
    tokv = _scan_specs(t, DVB, NB, GLA_HB, -1)[0]

    def body(v_ref, qg_ref, kd_ref, e_ref, s_ref, do_ref, dqg_ref, dkd_ref, dv_ref, de_ref, state):
        rev = pl.program_id(0) == 1

        @pl.when(pl.program_id(2) == 0)
        def _():
            state[...] = jnp.zeros_like(state)

        def step(i, dsts):
            ci = jnp.where(rev, i, nb - 1 - i)
            r = _rows(ci)
            out = []
            for hh, dst in enumerate(dsts):
                ck = slice(hh * DKB, (hh + 1) * DKB)
                cv = slice(hh * DVB, (hh + 1) * DVB)
                _, vjp = jax.vjp(_gla_scan, qg_ref[r, ck], kd_ref[r, ck], v_ref[r, cv], e_ref[hh, ci][0:1],
                                 s_ref[hh, ci])
                dqg, dkd, dv, de, dstn = vjp((do_ref[r, cv], dst))
                dqg_ref[r, ck] = dqg
                dkd_ref[r, ck] = dkd
                dv_ref[r, cv] = dv
                de_ref[hh, ci] = _first_row(de)
                out.append(dstn)
            return tuple(out)

        dsts = lax.fori_loop(0, nb, step, tuple(state[hh] for hh in range(GLA_HB)))
        for hh, dst in enumerate(dsts):
            state[hh] = dst

    return pl.pallas_call(
        body, name="gla_scan_bwd", grid=grid,
        in_specs=[_gla_v_spec(t, -1), tokk, tokk, per(8, 128), per(DVB, DKB), shared(DVB)],
        out_specs=[tokk, tokk, tokv, per(8, 128)],
        out_shape=[jax.ShapeDtypeStruct((2, t, NB * DKB), F32), jax.ShapeDtypeStruct((2, t, NB * DKB), F32),
                   jax.ShapeDtypeStruct((2, t, D), F32), jax.ShapeDtypeStruct((2, NB, t // CH, 8, 128), F32)],
        scratch_shapes=[pltpu.VMEM((GLA_HB, DVB, DKB), F32)],
        compiler_params=_cparams(("parallel", "parallel", "arbitrary")),
    )(p, qg, kd, e, ssave, do)


def _gla_intra_bwd(p, gk, dqg, dkd, dvs, de, do):
    t = p.shape[0]
    n = t // CH

    def body(q_ref, k_ref, v_ref, g_ref, dqg_ref, dkd_ref, dvs_ref, de_ref, do_ref,
             dq_ref, dk_ref, dv_ref, dg_ref):
        d = pl.program_id(1)
        cs = _Consts(d == 1)

        @pl.when(d == 0)
        def _():
            dq_ref[...] = jnp.zeros_like(dq_ref)
            dk_ref[...] = jnp.zeros_like(dk_ref)
            dv_ref[...] = jnp.zeros_like(dv_ref)

        def step(i, carry):
            r = _srows(i)
            f = lambda q, k, v, g: _gla_intra(q, k, v, g, cs)
            _, vjp = jax.vjp(f, q_ref[r, :], k_ref[r, :], v_ref[r, :], g_ref[r, :])
            dq, dk, dv, dg = vjp((dqg_ref[r, :], dkd_ref[r, :], do_ref[r, :], _chunk_rows(de_ref, i)))
            dq_ref[r, :] += dq
            dk_ref[r, :] += dk
            dv_ref[r, :] += dv + dvs_ref[r, :]
            dg_ref[r, :] = dg
            return carry

        lax.fori_loop(0, t // SC, step, 0)

    q, k, v, dk, dv, e_s, _ = _gla_specs(t, "hd")
    hk = pl.BlockSpec((t, DKB), lambda h, d: (0, h))
    hv = pl.BlockSpec((t, DVB), lambda h, d: (0, h))
    return pl.pallas_call(
        body, name="gla_intra_bwd", grid=(NB, 2),
        in_specs=[q, k, v, dk, dk, dk, dv, e_s, hv],
        out_specs=[hk, hk, hv, dk],
        out_shape=[jax.ShapeDtypeStruct((t, NB * DKB), F32), jax.ShapeDtypeStruct((t, NB * DKB), F32),
                   jax.ShapeDtypeStruct((t, D), F32), jax.ShapeDtypeStruct((2, t, NB * DKB), F32)],
        compiler_params=_cparams(("arbitrary", "arbitrary")),
    )(p, p, p, gk, dqg, dkd, dvs, de, do)


def _seg_gate(o, z, w):
    return _rms(o, w) * _silu(z)


def _seg_merge(ya, yb, ga, gb):
    return _sigmoid(ga) * ya + _sigmoid(gb) * yb


def _seg_loss(out, x, tgt, w):
    err = x + _rms(out, w) - tgt
    return 0.5 * jnp.sum(jnp.mean(err * err, axis=-1, keepdims=True), axis=0, keepdims=True)


def _post(oa2, ob2, p, x, tgt, gdn_w, gla_w, lnpost, w3, tm=128):
    t = x.shape[0]

    def body(oa_ref, ob_ref, z_ref, gb_ref, ga_ref, gB_ref, x_ref, t_ref, aw_ref, bw_ref, lw_ref, w_ref,
             loss_ref, doa_ref, dob_ref, dz_ref, dgb_ref, dga_ref, dgB_ref, dy_ref,
             dw_ref, daw_ref, dbw_ref, dlw_ref):
        first = pl.program_id(0) == 0
        oa = oa_ref[0] + oa_ref[1]
        ob = ob_ref[0] + ob_ref[1]
        z, gb = z_ref[...], gb_ref[...]
        aw, bw = aw_ref[...], bw_ref[...]
        rs = D // NSHARD

        def mat(a, m):
            return sum(jnp.dot(a[:, s * rs:(s + 1) * rs], w_ref[s, m], preferred_element_type=F32)
                       for s in range(NSHARD))

        def mat_t(g, m):
            return jnp.concatenate([_bdot(g, w_ref[s, m], 1, 1) for s in range(NSHARD)], axis=1)

        def add_dw(a, g, m):
            for s in range(NSHARD):
                dw_ref[s, m] += _bdot(a[:, s * rs:(s + 1) * rs], g, 0, 0)

        pa = [jax.vjp(_seg_gate, oa[:, h * DA:(h + 1) * DA], z[:, h * DA:(h + 1) * DA], aw) for h in range(NA)]
        pb = [jax.vjp(_seg_gate, ob[:, h * DVB:(h + 1) * DVB], gb[:, h * DVB:(h + 1) * DVB], bw)
              for h in range(NB)]
        a1 = jnp.concatenate([v for v, _ in pa], axis=1).astype(BF16)
        a2 = jnp.concatenate([v for v, _ in pb], axis=1).astype(BF16)
        ya = mat(a1, 0)
        yb = mat(a2, 1)
        merged, vjp_m = jax.vjp(_seg_merge, ya, yb, ga_ref[...], gB_ref[...])
        mb = merged.astype(BF16)
        out = mat(mb, 2)
        loss, vjp_l = jax.vjp(_seg_loss, out, x_ref[...], t_ref[...], lw_ref[...])
        dout, dyres, _, dlw = vjp_l(jnp.ones((1, 1), F32))
        dy_ref[...] = dyres
        doutb = dout.astype(BF16)
        dmerged = mat_t(doutb, 2)
        dya, dyb, dga, dgB = vjp_m(dmerged)
        dga_ref[...] = dga.astype(BF16)
        dgB_ref[...] = dgB.astype(BF16)
        dyab, dybb = dya.astype(BF16), dyb.astype(BF16)
        da1 = mat_t(dyab, 0)
        da2 = mat_t(dybb, 1)

        daw = jnp.zeros_like(aw)
        for h in range(NA):
            sl = slice(h * DA, (h + 1) * DA)
            do, dz, dw = pa[h][1](da1[:, sl])
            doa_ref[:, sl] = do
            dz_ref[:, sl] = dz.astype(BF16)
            daw = daw + dw
        dbw = jnp.zeros_like(bw)
        for h in range(NB):
            sl = slice(h * DVB, (h + 1) * DVB)
            do, dg, dw = pb[h][1](da2[:, sl])
            dob_ref[:, sl] = do
            dgb_ref[:, sl] = dg.astype(BF16)
            dbw = dbw + dw

        @pl.when(first)
        def _():
            loss_ref[...] = jnp.zeros_like(loss_ref)
            dw_ref[...] = jnp.zeros_like(dw_ref)
            daw_ref[...] = jnp.zeros_like(daw_ref)
            dbw_ref[...] = jnp.zeros_like(dbw_ref)
            dlw_ref[...] = jnp.zeros_like(dlw_ref)

        loss_ref[...] += jnp.broadcast_to(loss, loss_ref.shape)
        add_dw(a1, dyab, 0)
        add_dw(a2, dybb, 1)
        add_dw(mb, doutb, 2)
        daw_ref[...] += jnp.broadcast_to(daw, daw_ref.shape)
        dbw_ref[...] += jnp.broadcast_to(dbw, dbw_ref.shape)
        dlw_ref[...] += jnp.broadcast_to(dlw, dlw_ref.shape)

    two = pl.BlockSpec((2, tm, D), lambda i: (0, i, 0))
    pcol = lambda c: pl.BlockSpec((tm, D), lambda i: (i, c))
    tok = pl.BlockSpec((tm, D), lambda i: (i, 0))
    row = lambda n: pl.BlockSpec((1, n), lambda i: (0, 0))
    row8 = lambda n: pl.BlockSpec((8, n), lambda i: (0, 0))
    once = pl.Buffered(1)
    tokf = jax.ShapeDtypeStruct((t, D), F32)
    tokb = jax.ShapeDtypeStruct((t, D), BF16)
    wspec = pl.BlockSpec((NSHARD, 3, D // NSHARD, D), lambda i: (0, 0, 0, 0), pipeline_mode=once)
    return pl.pallas_call(
        body, name="post", grid=(t // tm,),
        in_specs=[two, two, pcol(3), pcol(6), pcol(7), pcol(8), tok, tok, row(DA), row(DVB), row(D), wspec],
        out_specs=[row8(128), tok, tok, tok, tok, tok, tok, tok, wspec, row8(DA), row8(DVB), row8(D)],
        out_shape=[jax.ShapeDtypeStruct((8, 128), F32), tokf, tokf, tokb, tokb, tokb, tokb, tokf,
                   jax.ShapeDtypeStruct((NSHARD, 3, D // NSHARD, D), F32),
                   jax.ShapeDtypeStruct((8, DA), F32), jax.ShapeDtypeStruct((8, DVB), F32),
                   jax.ShapeDtypeStruct((8, D), F32)],
        compiler_params=_cparams(("arbitrary",), vmem_mb=56),
    )(oa2, ob2, p, p, p, p, x, tgt, gdn_w, gla_w, lnpost, w3)


def _adam_math(w, g, m, v):
    nm = B1 * m + (1.0 - B1) * g
    nv = B2 * v + (1.0 - B2) * (g * g)
    m_hat = nm / (1.0 - B1 ** STEP)
    v_hat = nv / (1.0 - B2 ** STEP)
    return -LR * (m_hat / (jnp.sqrt(v_hat) + ADAM_EPS) + WD * w), nm, nv


SMALL_SLOTS = (("ln_pre_w", 0, 1024, 0), ("a_log_fwd", 1024, 8, 0), ("a_log_bwd", 1024, 8, 8),
               ("dt_bias_fwd", 1152, 8, 0), ("dt_bias_bwd", 1152, 8, 8), ("gdn_norm_w", 1280, 128, 0),
               ("gk_b2_fwd", 1408, 512, 0), ("gk_b2_bwd", 1920, 512, 0), ("gla_norm_w", 2432, 256, 0),
               ("ln_post_w", 2688, 1024, 0))
SMALL_W = 3840


def _adam_small(gsum, ws, ms, vs):
    nw = len(SMALL_SLOTS)

    def body(g_ref, *refs):
        w_refs, m_refs, v_refs, outs = refs[0:nw], refs[nw:2 * nw], refs[2 * nw:3 * nw], refs[3 * nw:]
        for i, (_, off, n, shift) in enumerate(SMALL_SLOTS):
            slot = g_ref[0:1, off:off + max(n, 128)]
            if shift:
                slot = pltpu.roll(slot, 128 - shift, 1)
            g = slot[:, 0:n]
            d, nm, nv = _adam_math(w_refs[i][...], g, m_refs[i][...], v_refs[i][...])
            for k, val in enumerate((g, d, nm, nv)):
                outs[4 * i + k][...] = val

    vm = pl.BlockSpec(memory_space=pltpu.VMEM)
    res = pl.pallas_call(
        body, name="adam_small", in_specs=[vm] * (1 + 3 * nw), out_specs=[vm] * (4 * nw),
        out_shape=[jax.ShapeDtypeStruct((1, n), F32) for _, _, n, _ in SMALL_SLOTS for _ in range(4)],
    )(gsum, *ws, *ms, *vs)
    return {name: res[4 * i:4 * i + 4] for i, (name, _, _, _) in enumerate(SMALL_SLOTS)}


def _adam(w, mine, got, m, v, tr, tile0=0, name=""):
    rows, cols = w.shape
    nh = mine.shape[0] // tr

    def body(c_ref, w_ref, a_ref, b_ref, m_ref, v_ref, g_ref, d_ref, nm_ref, nv_ref):
        half = (tile0 + pl.program_id(0)) // nh
        g = jnp.where(half == c_ref[0], a_ref[...], b_ref[...])
        d, nm, nv = _adam_math(w_ref[...], g, m_ref[...], v_ref[...])
        g_ref[...] = g
        d_ref[...] = d
        nm_ref[...] = nm
        nv_ref[...] = nv

    blk = pl.BlockSpec((tr, cols), lambda i, cc: (i, 0))
    half = pl.BlockSpec((tr, cols), lambda i, cc: ((tile0 + i) % nh, 0))
    shp = jax.ShapeDtypeStruct((rows, cols), F32)
    return pl.pallas_call(
        body, name=f"adam_{name}{rows}x{cols}",
        grid_spec=pltpu.PrefetchScalarGridSpec(
            num_scalar_prefetch=1, grid=(rows // tr,),
            in_specs=[blk, half, half, blk, blk], out_specs=[blk] * 4),
        out_shape=[shp] * 4,
        compiler_params=_cparams(("parallel",)),
    )(lax.axis_index("c").reshape(1), w, mine, got, m, v)


def _sum_cast(own, got):
    ns, _, r, c = own.shape
    tr = r // 4 if r % 64 == 0 else r

    def body(c_ref, a_ref, b_ref, f_ref, h_ref):
        s = a_ref[...] + b_ref[...]
        f_ref[...] = s
        h_ref[...] = s.astype(BF16)

    return pl.pallas_call(
        body, name=f"sum_cast_{r}x{c}",
        grid_spec=pltpu.PrefetchScalarGridSpec(
            num_scalar_prefetch=1, grid=(ns, r // tr),
            in_specs=[pl.BlockSpec((None, None, tr, c), lambda s, i, cc: (s, cc[0], i, 0)),
                      pl.BlockSpec((None, tr, c), lambda s, i, cc: (s, i, 0))],
            out_specs=[pl.BlockSpec((None, tr, c), lambda s, i, cc: (s, i, 0)),
                       pl.BlockSpec((None, tr, c), lambda s, i, cc: (s, i, 0))]),
        out_shape=[jax.ShapeDtypeStruct((ns, r, c), F32), jax.ShapeDtypeStruct((ns, r, c), BF16)],
        compiler_params=_cparams(("parallel", "parallel")),
    )(lax.axis_index("c").reshape(1), own, got)


def _sum4(mine, got):
    _, r, c = mine.shape
    tr = r // 4 if r % 64 == 0 else r

    def body(s_ref, a_ref, g_ref, o_ref):
        acc = a_ref[...] + g_ref[0].astype(F32)
        acc = acc + g_ref[1].astype(F32)
        o_ref[...] = acc + g_ref[2].astype(F32)

    shard = (2 * lax.axis_index("x") + lax.axis_index("y")).reshape(1)
    return pl.pallas_call(
        body, name=f"sum4_{r}x{c}",
        grid_spec=pltpu.PrefetchScalarGridSpec(
            num_scalar_prefetch=1, grid=(r // tr,),
            in_specs=[pl.BlockSpec((None, tr, c), lambda i, ss: (ss[0], i, 0)),
                      pl.BlockSpec((3, tr, c), lambda i, ss: (0, i, 0))],
            out_specs=pl.BlockSpec((tr, c), lambda i, ss: (i, 0))),
        out_shape=jax.ShapeDtypeStruct((r, c), F32),
        compiler_params=_cparams(("parallel",)),
    )(shard, mine, got)


def _place():
    x, y, c = lax.axis_index("x"), lax.axis_index("y"), lax.axis_index("c")
    chips = [(1 - x, y), (x, 1 - y), (1 - x, 1 - y)]
    return x, y, c, chips


def _gather_weights(parts):
    npart = len(parts)

    def body(*refs):
        ins, outs = refs[:npart], refs[npart:2 * npart]
        send_sems, recv_sems = refs[2 * npart:]
        x, y, c, chips = _place()
        sibling = (x, y, 1 - c)
        mine = 2 * x + y

        def remote(k, p, shard, half, to, src=None):
            dst = outs[p].at[shard, half]
            return pltpu.make_async_remote_copy(
                src_ref=dst if src is None else src, dst_ref=dst,
                send_sem=send_sems.at[k], recv_sem=recv_sems.at[k], device_id=to, device_id_type=MESH)

        first = [remote(j * npart + p, p, mine, c, (*chip, c), src=ins[p].at[c])
                 for j, chip in enumerate(chips) for p in range(npart)]
        for cp in first:
            cp.start()
        passed = []
        for j, (cx, cy) in enumerate(chips):
            for p in range(npart):
                remote(j * npart + p, p, 2 * cx + cy, c, (x, y, c)).wait_recv()
                fw = remote((3 + j) * npart + p, p, 2 * cx + cy, c, sibling)
                fw.start()
                passed.append(fw)
        for j, (cx, cy) in enumerate(chips):
            for p in range(npart):
                remote((3 + j) * npart + p, p, 2 * cx + cy, 1 - c, (x, y, c)).wait_recv()
        for cp in first + passed:
            cp.wait_send()

    got = pl.pallas_call(
        body, name="gather_weights",
        in_specs=[ANY] * npart, out_specs=[ANY] * npart,
        out_shape=[jax.ShapeDtypeStruct((NSHARD,) + a.shape, a.dtype) for a in parts],
        scratch_shapes=[pltpu.SemaphoreType.DMA((6 * npart,)), pltpu.SemaphoreType.DMA((6 * npart,))],
    )(*parts)
    mine = 2 * lax.axis_index("x") + lax.axis_index("y")
    return [lax.dynamic_update_index_in_dim(g, a, mine, 0) for g, a in zip(got, parts)]


def _swap_halves(parts, tag=""):
    npart = len(parts)

    def body(*refs):
        ins, outs = refs[:npart], refs[npart:2 * npart]
        send_sems, recv_sems = refs[2 * npart:]
        x, y, c, _ = _place()
        cps = [pltpu.make_async_remote_copy(
            src_ref=ins[p].at[s, 1 - c], dst_ref=outs[p].at[s],
            send_sem=send_sems.at[s * npart + p], recv_sem=recv_sems.at[s * npart + p],
            device_id=(x, y, 1 - c), device_id_type=MESH) for s in range(NSHARD) for p in range(npart)]
        for cp in cps:
            cp.start()
        for cp in cps:
            cp.wait()

    return pl.pallas_call(
        body, name="swap_halves" + tag, in_specs=[ANY] * npart, out_specs=[ANY] * npart,
        out_shape=[jax.ShapeDtypeStruct((NSHARD,) + a.shape[2:], a.dtype) for a in parts],
        scratch_shapes=[pltpu.SemaphoreType.DMA((NSHARD * npart,)), pltpu.SemaphoreType.DMA((NSHARD * npart,))],
    )(*parts)


def _scatter_shards(parts):
    npart = len(parts)

    def body(*refs):
        ins, outs = refs[:npart], refs[npart:2 * npart]
        send_sems, recv_sems = refs[2 * npart:]
        x, y, c, chips = _place()
        cps = [pltpu.make_async_remote_copy(
            src_ref=ins[p].at[2 * cx + cy], dst_ref=outs[p].at[j],
            send_sem=send_sems.at[j * npart + p], recv_sem=recv_sems.at[j * npart + p],
            device_id=(cx, cy, c), device_id_type=MESH)
            for j, (cx, cy) in enumerate(chips) for p in range(npart)]
        for cp in cps:
            cp.start()
        for cp in cps:
            cp.wait()

    return pl.pallas_call(
        body, name="scatter_shards", in_specs=[ANY] * npart, out_specs=[ANY] * npart,
        out_shape=[jax.ShapeDtypeStruct((3,) + a.shape[1:], a.dtype) for a in parts],
        scratch_shapes=[pltpu.SemaphoreType.DMA((3 * npart,)), pltpu.SemaphoreType.DMA((3 * npart,))],
    )(*parts)


HBM = pl.BlockSpec(memory_space=pltpu.HBM)
SEM = pl.BlockSpec(memory_space=pltpu.SEMAPHORE)
EFFECT = pltpu.SideEffectType.DATAFLOW_SIDE_EFFECTING


def _scatter_copies(srcs, lands, send_sems, recv_sems, waiting):
    x, y, c, chips = _place()
    n = len(srcs)
    return [pltpu.make_async_remote_copy(
        src_ref=srcs[p].at[2 * cx + cy], dst_ref=lands[p].at[j],
        send_sem=send_sems.at[j * n + p], recv_sem=recv_sems.at[j * n + p],
        device_id=(cx, cy, c), device_id_type=MESH)
        for j, (cx, cy) in enumerate(chips) for p in range(n)]


def _proj_copies(srcs, lands, send_sems, recv_sems, waiting):
    x, y, c, chips = _place()
    mine = 2 * x + y
    return [pltpu.make_async_remote_copy(
        src_ref=srcs[0].at[c], dst_ref=lands[0].at[mine, c],
        send_sem=send_sems.at[2 * j + to], recv_sem=recv_sems.at[2 * j + (to if waiting else c)],
        device_id=(cx, cy, to), device_id_type=MESH)
        for j, (cx, cy) in enumerate(chips) for to in range(2)]


def _start_copies(copies, nsem, parts, lands, name, after=None):
    n = len(parts)
    extra = [] if after is None else [after]

    def body(*refs):
        outs = refs[2 * n + len(extra):]
        for cp in copies(refs[:n], refs[n:2 * n], outs[0], outs[1], False):
            cp.start()
        outs[-1][...] = jnp.zeros_like(outs[-1])

    res = pl.pallas_call(
        body, name=name,
        out_shape=(pltpu.SemaphoreType.DMA((nsem,)), pltpu.SemaphoreType.DMA((nsem,)),
                   *[pltpu.HBM(a.shape, a.dtype) for a in parts], *[pltpu.HBM(a.shape, a.dtype) for a in lands],
                   jax.ShapeDtypeStruct((8, 128), F32)),
        in_specs=[HBM] * (2 * n) + [ANY] * len(extra),
        out_specs=(SEM, SEM, *[HBM] * (2 * n), pl.BlockSpec(memory_space=pltpu.VMEM)),
        input_output_aliases={i: 2 + i for i in range(2 * n)},
        compiler_params=pltpu.CompilerParams(has_side_effects=EFFECT),
    )(*[pltpu.with_memory_space_constraint(a, pltpu.HBM) for a in parts],
      *[pltpu.with_memory_space_constraint(a, pltpu.HBM) for a in lands], *extra)
    return res[0], res[1], res[2:2 + n], res[2 + n:2 + 2 * n], res[-1]


def _wait_copies(copies, started, after, name):
    send_sems, recv_sems, srcs, lands, _ = started
    n = len(srcs)

    def body(*refs):
        for cp in copies(refs[:n], refs[n:2 * n], refs[2 * n], refs[2 * n + 1], True):
            cp.wait_send()
            cp.wait_recv()

    res = pl.pallas_call(
        body, name=name,
        out_shape=tuple(pltpu.HBM(a.shape, a.dtype) for a in (*srcs, *lands)),
        in_specs=[HBM] * (2 * n) + [SEM, SEM, ANY], out_specs=tuple([HBM] * (2 * n)),
        input_output_aliases={i: i for i in range(2 * n)},
        compiler_params=pltpu.CompilerParams(has_side_effects=EFFECT),
    )(*srcs, *lands, send_sems, recv_sems, after)
    return res[n:]


def _join_halves(parts):
    npart = len(parts)

    def body(*refs):
        ins, outs = refs[:npart], refs[npart:2 * npart]
        send_sems, recv_sems = refs[2 * npart:]
        x, y, c, _ = _place()
        cps = [pltpu.make_async_remote_copy(
            src_ref=ins[p], dst_ref=outs[p], send_sem=send_sems.at[p], recv_sem=recv_sems.at[p],
            device_id=(x, y, 1 - c), device_id_type=MESH) for p in range(npart)]
        for cp in cps:
            cp.start()
        for cp in cps:
            cp.wait()

    return pl.pallas_call(
        body, name="join_halves", in_specs=[ANY] * npart, out_specs=[ANY] * npart,
        out_shape=[jax.ShapeDtypeStruct(a.shape, a.dtype) for a in parts],
        scratch_shapes=[pltpu.SemaphoreType.DMA((npart,)), pltpu.SemaphoreType.DMA((npart,))],
    )(*parts)


def _allreduce_small(v):
    r, ncol = v.shape

    def body(v_ref, o_ref, buf, send_sems, recv_sems):
        x, y, c, _ = _place()
        me = 4 * x + 2 * y + c
        buf[me] = v_ref[...]
        cps = []
        for k in range(1, 8):
            px, py, pc = x ^ (k >> 2), y ^ ((k >> 1) & 1), c ^ (k & 1)
            cps.append(pltpu.make_async_remote_copy(
                src_ref=v_ref, dst_ref=buf.at[me], send_sem=send_sems.at[k - 1], recv_sem=recv_sems.at[k - 1],
                device_id=(px, py, pc), device_id_type=MESH))
        for cp in cps:
            cp.start()
        for k in range(1, 8):
            px, py, pc = x ^ (k >> 2), y ^ ((k >> 1) & 1), c ^ (k & 1)
            pltpu.make_async_remote_copy(
                src_ref=v_ref, dst_ref=buf.at[4 * px + 2 * py + pc], send_sem=send_sems.at[k - 1],
                recv_sem=recv_sems.at[k - 1], device_id=(px, py, pc), device_id_type=MESH).wait_recv()
        for cp in cps:
            cp.wait_send()
        acc = buf[0]
        for d in range(1, 8):
            acc = acc + buf[d]
        o_ref[...] = acc

    return pl.pallas_call(
        body, name="allreduce_small",
        in_specs=[pl.BlockSpec(memory_space=pltpu.VMEM)], out_specs=pl.BlockSpec(memory_space=pltpu.VMEM),
        out_shape=jax.ShapeDtypeStruct((r, ncol), F32),
        scratch_shapes=[pltpu.VMEM((8, r, ncol), F32), pltpu.SemaphoreType.DMA((7,)), pltpu.SemaphoreType.DMA((7,))],
    )(v)


def _permute_rows(shards):
    w0, w1, w2, w3 = shards
    zeros = jnp.zeros((NPERM - 9280, w0.shape[1]), w0.dtype)
    return jnp.concatenate([w0, w1[0:1776], w1[1808:2320], w2, w3[0:240], w3[272:2320],
                            w1[1776:1808], w3[240:272], zeros], axis=0)


def _unpermute_rows(g):
    s1 = jnp.concatenate([g[2320:4096], g[9216:9248], g[4096:4608]], axis=0)
    s3 = jnp.concatenate([g[6928:7168], g[9248:9280], g[7168:9216]], axis=0)
    return jnp.stack([g[0:2320], s1, g[4608:6928], s3], axis=0)


def _pack_shard_small(conv, w2f, w2b):
    top = jnp.pad(conv, ((0, 8 - conv.shape[0]), (0, 0)))
    mid = jnp.pad(jnp.concatenate([w2f, w2b], axis=1), ((0, 0), (0, 768 - 256)))
    return jnp.concatenate([top, mid, jnp.zeros((8, 768), conv.dtype)], axis=0)


def _unpack_shard_small(a):
    return a[0:5], a[8:24, 0:128], a[8:24, 128:256]


def kernel(x, ln_pre_w, w_in, conv_w, a_log_fwd, a_log_bwd, dt_bias_fwd, dt_bias_bwd, gdn_norm_w, w_proj_gdn, gk_w2_fwd, gk_b2_fwd, gk_w2_bwd, gk_b2_bwd, gla_norm_w, w_proj_gla, w_out, ln_post_w, loss_target, m_ln_pre_w, m_w_in, m_conv_w, m_a_log_fwd, m_a_log_bwd, m_dt_bias_fwd, m_dt_bias_bwd, m_gdn_norm_w, m_w_proj_gdn, m_gk_w2_fwd, m_gk_b2_fwd, m_gk_w2_bwd, m_gk_b2_bwd, m_gla_norm_w, m_w_proj_gla, m_w_out, m_ln_post_w, v_ln_pre_w, v_w_in, v_conv_w, v_a_log_fwd, v_a_log_bwd, v_dt_bias_fwd, v_dt_bias_bwd, v_gdn_norm_w, v_w_proj_gdn, v_gk_w2_fwd, v_gk_b2_fwd, v_gk_w2_bwd, v_gk_b2_bwd, v_gla_norm_w, v_w_proj_gla, v_w_out, v_ln_post_w):
    t = x.shape[1]
    x2, tgt = x[0], loss_target[0]

    win_l = w_in[0].T.astype(BF16).reshape(2, SHW // 2, D)
    proj_l = jnp.concatenate([w_proj_gdn[0], w_proj_gla[0], w_out[0]], axis=0).astype(BF16).reshape(2, 384, D)
    small_l = _pack_shard_small(conv_w[0], gk_w2_fwd[0], gk_w2_bwd[0]).reshape(2, 16, 768)
    win_g, small_g = _gather_weights([win_l, small_l])
    proj_started = _start_copies(_proj_copies, 6, [proj_l], [lax.empty((NSHARD, 2, 384, D), BF16)],
                                 "gather_proj_start", after=small_g)
    wperm = _permute_rows(win_g.reshape(NSHARD, SHW, D))
    small_g = small_g.reshape(NSHARD, 32, 768)
    convw = small_g[:, 0:8, :].transpose(1, 0, 2).reshape(8, 3 * D)
    w2f = small_g[:, 8:24, 0:128].transpose(1, 0, 2).reshape(16, 512)
    w2b = small_g[:, 8:24, 128:256].transpose(1, 0, 2).reshape(16, 512)
    w2f_pad = jnp.pad(w2f, ((32, 80), (0, 0)))
    w2b_pad = jnp.pad(w2b, ((48, 64), (0, 0)))
    alog_row = jnp.pad(jnp.concatenate([a_log_fwd, a_log_bwd], axis=1), ((0, 0), (0, 112)))
    dt_row = jnp.pad(jnp.concatenate([dt_bias_fwd, dt_bias_bwd], axis=1), ((0, 0), (0, 112)))

    p, h = _inproj(x2, ln_pre_w + proj_started[4][0:1, 0:1], wperm)
    qn, kn, vc = (_qkv_fwd(p, convw, kind) for kind in range(3))
    gsm, gk = _gates_fwd(p, alog_row, dt_row, w2f_pad, gk_b2_fwd, w2b_pad, gk_b2_bwd)
    g2, b2 = _gcum_fwd(gsm)
    u, w, at, qd, kd, el, tinv = _gdn_intra_fwd(qn, kn, vc, g2, b2)
    oa2, sa = _gdn_scan_fwd(u, w, at, qd, kd, el)
    qg, kdb, intra, elb = _gla_intra_fwd(p, gk)
    ob2, sb = _gla_scan_fwd(p, qg, kdb, intra, elb)

    (proj_land,) = _wait_copies(_proj_copies, proj_started, ob2, "gather_proj_wait")
    mine = 2 * lax.axis_index("x") + lax.axis_index("y")
    w3 = lax.dynamic_update_index_in_dim(proj_land, proj_l, mine, 0).reshape(NSHARD, 3, D // NSHARD, D)
    (loss8, doa, dob, dz, dgb, dga, dgB, dyres, dw3, dgdn_w, dgla_w, dlnpost) = _post(
        oa2, ob2, p, x2, tgt, gdn_norm_w, gla_norm_w, ln_post_w, w3)

    g_proj = dw3.reshape(NSHARD, 2, 384, D)
    sum_proj = _sum_cast(g_proj, _swap_halves([g_proj], "_proj")[0])
    started_proj = _start_copies(_scatter_copies, 3, [sum_proj[1]], [lax.empty((3, 384, D), BF16)],
                                 "scatter_proj_start")
    du, dw, dat, dqd, dkd, del_ = _gdn_scan_bwd(u, w, at, qd, kd, el + started_proj[4][0, 0], sa, doa)
    dqn, dkn, dvc, dg2, db2 = _gdn_intra_bwd(qn, kn, vc, g2, b2, tinv, du, dw, dat, dqd, dkd, del_)
    dgsm = _gcum_bwd(gsm, dg2, db2)
    dqg, dkdb, dvs, delb = _gla_scan_bwd(p, qg, kdb, elb, sb, dob)
    dqb, dkb, dvb, dgk = _gla_intra_bwd(p, gk, dqg, dkdb, dvs, delb, dob)
    (dps, dalog8, ddt8, dw2f_pad, db2f8, dw2b_pad, db2b8) = _gates_bwd(
        p, alog_row, dt_row, w2f_pad, gk_b2_fwd, w2b_pad, gk_b2_bwd, dgsm, dgk)
    dpre, dconv = zip(*[_qkv_bwd(p, convw, g, kind) for kind, g in enumerate((dqn, dkn, dvc))])

    pieces = (jnp.concatenate([a.astype(BF16) for a in (*dpre, dz, dqb, dkb, dvb, dgb, dga, dgB, dps)], axis=1),)
    dwperm = _inproj_dw(h, pieces)

    g_in = _unpermute_rows(dwperm).reshape(NSHARD, 2, SHW // 2, D)
    dconv_full = jnp.concatenate(dconv, axis=1)
    dw2f, dw2b = dw2f_pad[32:48], dw2b_pad[48:64]
    g_small = jnp.stack([_pack_shard_small(dconv_full[0:5, 768 * s:768 * (s + 1)],
                                           dw2f[:, 128 * s:128 * (s + 1)], dw2b[:, 128 * s:128 * (s + 1)])
                         for s in range(NSHARD)])
    g_small = g_small.reshape(NSHARD, 2, 16, 768)
    parts = [g_in, g_small]
    got = _swap_halves(parts)
    sums = [_sum_cast(a, b) for a, b in zip(parts, got)]
    hbs = [hb for _, hb in sums]
    started = _start_copies(_scatter_copies, 3 * len(hbs), hbs,
                            [lax.empty((3,) + a.shape[1:], a.dtype) for a in hbs], "scatter_start")
    dx, dlnpre8 = _inproj_dx(pieces, wperm, x2, ln_pre_w + started[4][0:1, 0:1], dyres)

    gsmall = _allreduce_small(jnp.concatenate(
        [dlnpre8, dalog8, ddt8, dgdn_w, db2f8, db2b8, dgla_w, dlnpost, loss8], axis=1))
    smalls = dict(ln_pre_w=(ln_pre_w, m_ln_pre_w, v_ln_pre_w), a_log_fwd=(a_log_fwd, m_a_log_fwd, v_a_log_fwd),
                  a_log_bwd=(a_log_bwd, m_a_log_bwd, v_a_log_bwd),
                  dt_bias_fwd=(dt_bias_fwd, m_dt_bias_fwd, v_dt_bias_fwd),
                  dt_bias_bwd=(dt_bias_bwd, m_dt_bias_bwd, v_dt_bias_bwd),
                  gdn_norm_w=(gdn_norm_w, m_gdn_norm_w, v_gdn_norm_w),
                  gk_b2_fwd=(gk_b2_fwd, m_gk_b2_fwd, v_gk_b2_fwd), gk_b2_bwd=(gk_b2_bwd, m_gk_b2_bwd, v_gk_b2_bwd),
                  gla_norm_w=(gla_norm_w, m_gla_norm_w, v_gla_norm_w), ln_post_w=(ln_post_w, m_ln_post_w, v_ln_post_w))
    names = [name for name, _, _, _ in SMALL_SLOTS]
    small = _adam_small(gsmall, *([smalls[n][i] for n in names] for i in range(3)))

    landed_proj = _wait_copies(_scatter_copies, started_proj, small["ln_pre_w"][1], "scatter_proj_wait")
    landed = _wait_copies(_scatter_copies, started, small["ln_pre_w"][1], "scatter_wait")
    sums = [sums[0], sum_proj, sums[1]]
    halves = [_sum4(f, g) for (f, _), g in zip(sums, [landed[0], landed_proj[0], landed[1]])]
    theirs = _join_halves(halves)

    a_in = [a.T for a in _adam(w_in[0].T, halves[0], theirs[0], m_w_in[0].T, v_w_in[0].T, 232, name="in")]
    a_pr = [_adam(w[0], halves[1], theirs[1], m[0], v[0], 128, tile0=2 * i, name=f"proj{i}")
            for i, (w, m, v) in enumerate(((w_proj_gdn, m_w_proj_gdn, v_w_proj_gdn),
                                           (w_proj_gla, m_w_proj_gla, v_w_proj_gla), (w_out, m_w_out, v_w_out)))]
    a_ss = _adam(_pack_shard_small(conv_w[0], gk_w2_fwd[0], gk_w2_bwd[0]), halves[2], theirs[2],
                 _pack_shard_small(m_conv_w[0], m_gk_w2_fwd[0], m_gk_w2_bwd[0]),
                 _pack_shard_small(v_conv_w[0], v_gk_w2_fwd[0], v_gk_w2_bwd[0]), 16, name="small")

    def family(k):
        conv, w2f_, w2b_ = _unpack_shard_small(a_ss[k])
        s = {n: small[n][k] for n in names}
        return [s["ln_pre_w"], a_in[k][None], conv[None], s["a_log_fwd"], s["a_log_bwd"], s["dt_bias_fwd"],
                s["dt_bias_bwd"], s["gdn_norm_w"], a_pr[0][k][None], w2f_[None], s["gk_b2_fwd"], w2b_[None],
                s["gk_b2_bwd"], s["gla_norm_w"], a_pr[1][k][None], a_pr[2][k][None], s["ln_post_w"]]

    return (gsmall[0, SMALL_W - 128], dx[None], *family(0), *family(1), *family(2), *family(3))
```

```python
import functools

import jax
import jax.numpy as jnp
from jax import lax
from jax.experimental import pallas as pl
from jax.experimental.pallas import tpu as pltpu

F32 = jnp.float32
BF16 = jnp.bfloat16
HI = lax.Precision.HIGHEST
MESH = pl.DeviceIdType.MESH

D = 1024
CH = 64
EPS = 1e-6
NA, DA = 8, 128
NB, DKB, DVB = 4, 128, 256
NSHARD = 4
SHW = 2320
NPERM = 9728
PS_BLOCK = 72
LR, B1, B2, ADAM_EPS, WD, STEP = 0.001, 0.9, 0.999, 1e-08, 0.01, 10

ANY = pl.BlockSpec(memory_space=pl.ANY)


def _cparams(sem=None, vmem_mb=48):
    return pltpu.CompilerParams(dimension_semantics=sem, vmem_limit_bytes=vmem_mb << 20)


def _bdot(a, b, ca, cb):
    return lax.dot_general(a.astype(BF16), b.astype(BF16), (((ca,), (cb,)), ((), ())),
                           preferred_element_type=F32)


@jax.custom_vjp
def mm(a, b):
    return _bdot(a, b, 1, 0)


def _mm_fwd(a, b):
    return _bdot(a, b, 1, 0), (a, b)


def _mm_bwd(res, g):
    a, b = res
    return _bdot(g, b, 1, 1), _bdot(a, g, 0, 0)


mm.defvjp(_mm_fwd, _mm_bwd)


@jax.custom_vjp
def mm_nt(a, b):
    return _bdot(a, b, 1, 1)


def _mm_nt_fwd(a, b):
    return _bdot(a, b, 1, 1), (a, b)


def _mm_nt_bwd(res, g):
    a, b = res
    return _bdot(g, b, 1, 0), _bdot(g, a, 0, 0)


mm_nt.defvjp(_mm_nt_fwd, _mm_nt_bwd)


@jax.custom_vjp
def mm_tn(a, b):
    return _bdot(a, b, 0, 0)


def _mm_tn_fwd(a, b):
    return _bdot(a, b, 0, 0), (a, b)


def _mm_tn_bwd(res, g):
    a, b = res
    return _bdot(b, g, 1, 1), _bdot(a, g, 1, 0)


mm_tn.defvjp(_mm_tn_fwd, _mm_tn_bwd)


def dot_hi(a, b):
    return lax.dot_general(a, b, (((1,), (0,)), ((), ())), precision=HI, preferred_element_type=F32)


def _split3(x):
    x1 = x.astype(BF16)
    r = x - x1.astype(F32)
    x2 = r.astype(BF16)
    return x1, x2, (r - x2.astype(F32)).astype(BF16)


def _cdot(c, x, cc, cx, c_first=True):
    parts = _split3(x)
    if c_first:
        return _bdot(c, parts[0], cc, cx) + _bdot(c, parts[1], cc, cx) + _bdot(c, parts[2], cc, cx)
    return _bdot(parts[0], c, cx, cc) + _bdot(parts[1], c, cx, cc) + _bdot(parts[2], c, cx, cc)


@jax.custom_vjp
def cmm(c, x):
    return _cdot(c, x, 1, 0)


def _cmm_fwd(c, x):
    return _cdot(c, x, 1, 0), c


def _cmm_bwd(c, g):
    return jnp.zeros_like(c), _cdot(c, g, 0, 0)


cmm.defvjp(_cmm_fwd, _cmm_bwd)


@jax.custom_vjp
def mmc(x, c):
    return _cdot(c, x, 0, 1, c_first=False)


def _mmc_fwd(x, c):
    return _cdot(c, x, 0, 1, c_first=False), c


def _mmc_bwd(c, g):
    return _cdot(c, g, 1, 1, c_first=False), jnp.zeros_like(c)


mmc.defvjp(_mmc_fwd, _mmc_bwd)


def _sigmoid(x):
    return 1.0 / (1.0 + jnp.exp(-x))


def _silu(x):
    return x * _sigmoid(x)


def _softplus(x):
    return jnp.maximum(x, 0.0) + jnp.log(1.0 + jnp.exp(-jnp.abs(x)))


def _rms(x, w):
    return x * lax.rsqrt(jnp.mean(x * x, axis=-1, keepdims=True) + EPS) * w


SC = 256


class _Consts:
    def __init__(self, rev):
        r = lax.broadcasted_iota(jnp.int32, (SC, SC), 0)
        c = lax.broadcasted_iota(jnp.int32, (SC, SC), 1)
        same = (r >> 6) == (c >> 6)
        a = jnp.where(rev, c, r)
        b = jnp.where(rev, r, c)
        self.incl = same & (a >= b)
        self.strict = same & (a > b)
        self.incl_f = self.incl.astype(F32)
        self.eye = (r == c).astype(F32)
        rows = lax.broadcasted_iota(jnp.int32, (SC, 1), 0)
        self.last_col = ((rows & (CH - 1)) == jnp.where(rev, 0, CH - 1)).astype(F32)
        rr = lax.broadcasted_iota(jnp.int32, (SC, CH), 0)
        cc = lax.broadcasted_iota(jnp.int32, (SC, CH), 1)
        self.fold = ((rr & (CH - 1)) == cc).astype(F32)


def _dot3(a, b, ca=1, cb=0):
    ah, bh = a.astype(BF16), b.astype(BF16)
    al, bl = (a - ah.astype(F32)).astype(BF16), (b - bh.astype(F32)).astype(BF16)
    return _bdot(ah, bh, ca, cb) + (_bdot(ah, bl, ca, cb) + _bdot(al, bh, ca, cb))


TRI_SPLIT_LEVELS = 2


def _tri_inv(low, eye):
    n = -low
    acc = eye + n
    p = n
    for level in range(5):
        dot = _dot3 if level < TRI_SPLIT_LEVELS else (lambda a, b: _bdot(a, b, 1, 0))
        p = dot(p, p)
        acc = acc + dot(acc, p)
    return acc


@jax.custom_vjp
def _solve2(low, rv, rk, tinv):
    x = _dot3(tinv, jnp.concatenate([rv, rk], axis=1))
    return x[:, :DA], x[:, DA:]


def _solve2_fwd(low, rv, rk, tinv):
    x = _dot3(tinv, jnp.concatenate([rv, rk], axis=1))
    return (x[:, :DA], x[:, DA:]), (x, tinv)


def _solve2_bwd(res, g):
    x, tinv = res
    drhs = _dot3(tinv, jnp.concatenate(g, axis=1), 0, 0)
    return -_dot3(drhs, x, 1, 1), drhs[:, :DA], drhs[:, DA:], jnp.zeros_like(tinv)


_solve2.defvjp(_solve2_fwd, _solve2_bwd)


def _chunk_last(x, cs):
    xs = (x * cs.last_col).reshape(SC // CH, CH, x.shape[1])
    return jnp.broadcast_to(jnp.sum(xs, axis=1, keepdims=True), xs.shape).reshape(x.shape)


def _gdn_decay(g, cs):
    gw = jnp.concatenate([g] * (SC // DA), axis=1)
    grow = jnp.sum(cs.eye * gw, axis=0, keepdims=True)
    return jnp.where(cs.incl, jnp.exp(jnp.where(cs.incl, gw - grow, 0.0)), 0.0)


def _gdn_intra(q, k, v, g, bx, tinv, cs):
    decay = _gdn_decay(g, cs)
    kb = k * bx
    low = jnp.where(cs.strict, mm_nt(kb, k) * decay, 0.0)
    eg = jnp.exp(g)
    made = tinv is None
    if made:
        tinv = _tri_inv(low, cs.eye)
    u, w = _solve2(low, v * bx, kb * eg, tinv)
    attn = mmc(mm_nt(q, k) * decay, cs.fold)
    qd = q * eg
    glast = _chunk_last(g, cs)
    kd = k * jnp.exp(glast - g)
    outs = (u, w, attn, qd, kd, jnp.exp(glast))
    return outs + (tinv,) if made else outs


def _gdn_scan(u, w, attn, qd, kd, el, s):
    vn = u - mm(w, s)
    o = mm(qd, s) + mm(attn, vn)
    sn = s * el + mm_tn(kd, vn)
    return o, sn


def _gla_intra(q, k, v, gk, cs):
    gc = cmm(cs.incl_f, gk)
    qg = q * (DKB ** -0.5) * jnp.exp(gc)
    kg = k * jnp.exp(-gc)
    attn = jnp.where(cs.incl, mm_nt(qg, kg), 0.0)
    intra = mm(attn, v)
    glast = _chunk_last(gc, cs)
    kd = k * jnp.exp(glast - gc)
    return qg, kd, intra, jnp.exp(glast)


def _gla_scan(qg, kd, v, el, st):
    o = mm_nt(qg, st)
    stn = st * el + mm_tn(v, kd)
    return o, stn


def _shift_rows(x, s):
    if s == 0:
        return x
    t = x.shape[0]
    rolled = pltpu.roll(x, (-s) % t, 0)
    rows = lax.broadcasted_iota(jnp.int32, x.shape, 0)
    return jnp.where((rows + s >= 0) & (rows + s < t), rolled, 0.0)


@jax.custom_vjp
def _conv5(x, w):
    acc = w[0:1] * _shift_rows(x, -2)
    for j in range(1, 5):
        acc = acc + w[j:j + 1] * _shift_rows(x, j - 2)
    return acc


def _conv5_fwd(x, w):
    return _conv5(x, w), (x, w)


def _conv5_bwd(res, g):
    x, w = res
    dx = w[0:1] * _shift_rows(g, 2)
    for j in range(1, 5):
        dx = dx + w[j:j + 1] * _shift_rows(g, 2 - j)
    rows = lax.broadcasted_iota(jnp.int32, w.shape, 0)
    dw = jnp.zeros_like(w)
    for j in range(5):
        dwj = jnp.sum(g * _shift_rows(x, j - 2), axis=0, keepdims=True)
        dw = dw + jnp.where(rows == j, dwj, 0.0)
    return dx, dw


_conv5.defvjp(_conv5_fwd, _conv5_bwd)


def _qkv_act(kind):
    def f(x, w):
        c = _silu(_conv5(x, w))
        if kind == 2:
            return c
        c = c * lax.rsqrt(jnp.sum(c * c, axis=-1, keepdims=True) + EPS)
        return c * (DA ** -0.5) if kind == 0 else c
    return f


def _inproj(x, lnw, wperm, tn=512):
    t = x.shape[0]
    tm = min(t, 1024)

    def body(x_ref, lnw_ref, w_ref, p_ref, h_ref, hbuf):
        @pl.when(pl.program_id(1) == 0)
        def _():
            hb = _rms(x_ref[...], lnw_ref[...]).astype(BF16)
            hbuf[...] = hb
            h_ref[...] = hb
        p_ref[...] = _bdot(hbuf[...], w_ref[...], 1, 1)

    return pl.pallas_call(
        body, name="inproj", grid=(t // tm, NPERM // tn),
        in_specs=[pl.BlockSpec((tm, D), lambda i, j: (i, 0)),
                  pl.BlockSpec((1, D), lambda i, j: (0, 0)),
                  pl.BlockSpec((tn, D), lambda i, j: (j, 0))],
        out_specs=[pl.BlockSpec((tm, tn), lambda i, j: (i, j)),
                   pl.BlockSpec((tm, D), lambda i, j: (i, 0))],
        out_shape=[jax.ShapeDtypeStruct((t, NPERM), F32),
                   jax.ShapeDtypeStruct((t, D), BF16)],
        scratch_shapes=[pltpu.VMEM((tm, D), BF16)],
        compiler_params=_cparams(("parallel", "arbitrary")),
    )(x, lnw, wperm)


DP_TILE = 512
DP_PIECES = ((0, 19),)


def _piece_specs(tm, j_first):
    specs = []
    for j0, n in DP_PIECES:
        def imap(a, b, j0=j0, n=n):
            j, i = (a, b) if j_first else (b, a)
            inside = (j >= j0) & (j < j0 + n)
            return jnp.where(inside, i, 0), jnp.clip(j - j0, 0, n - 1)
        specs.append(pl.BlockSpec((tm, DP_TILE), imap))
    return specs


def _for_piece(j, refs, fn):
    for (j0, n), ref in zip(DP_PIECES, refs):
        @pl.when((j >= j0) & (j < j0 + n))
        def _(ref=ref):
            fn(ref[...])


def _inproj_dw(h, pieces):
    t = h.shape[0]
    tm = min(t, 2048)
    npc = len(pieces)

    def body(h_ref, *refs):
        dw_ref = refs[npc]

        @pl.when(pl.program_id(1) == 0)
        def _():
            dw_ref[...] = jnp.zeros_like(dw_ref)

        def add(dp):
            dw_ref[...] += _bdot(dp, h_ref[...], 0, 0)
        _for_piece(pl.program_id(0), refs[:npc], add)

    return pl.pallas_call(
        body, name="inproj_dw", grid=(NPERM // DP_TILE, t // tm),
        in_specs=[pl.BlockSpec((tm, D), lambda j, i: (i, 0))] + _piece_specs(tm, True),
        out_specs=pl.BlockSpec((DP_TILE, D), lambda j, i: (j, 0)),
        out_shape=jax.ShapeDtypeStruct((NPERM, D), F32),
        compiler_params=_cparams(("parallel", "arbitrary")),
    )(h, *pieces)


def _inproj_dx(pieces, wperm, x, lnw, dyres):
    t = x.shape[0]
    tm = min(t, 1024)
    tn = DP_TILE
    nj = NPERM // tn
    npc = len(pieces)

    def body(*refs):
        w_ref, x_ref, lnw_ref, dy_ref, dx_ref, dlnw_ref, acc = refs[npc:]
        j = pl.program_id(1)

        @pl.when(j == 0)
        def _():
            acc[...] = jnp.zeros_like(acc)

        def add(dp):
            acc[...] += _bdot(dp, w_ref[...], 1, 0)
        _for_piece(j, refs[:npc], add)

        @pl.when(j == nj - 1)
        def _():
            _, vjp = jax.vjp(_rms, x_ref[...], lnw_ref[...])
            dx, dlnw = vjp(acc[...])
            dx_ref[...] = dx + dy_ref[...]

            @pl.when(pl.program_id(0) == 0)
            def _():
                dlnw_ref[...] = jnp.zeros_like(dlnw_ref)
            dlnw_ref[...] += jnp.broadcast_to(dlnw, dlnw_ref.shape)

    return pl.pallas_call(
        body, name="inproj_dx", grid=(t // tm, nj),
        in_specs=_piece_specs(tm, False) + [
                  pl.BlockSpec((tn, D), lambda i, j: (j, 0)),
                  pl.BlockSpec((tm, D), lambda i, j: (i, 0)),
                  pl.BlockSpec((1, D), lambda i, j: (0, 0)),
                  pl.BlockSpec((tm, D), lambda i, j: (i, 0))],
        out_specs=[pl.BlockSpec((tm, D), lambda i, j: (i, 0)),
                   pl.BlockSpec((8, D), lambda i, j: (0, 0))],
        out_shape=[jax.ShapeDtypeStruct((t, D), F32), jax.ShapeDtypeStruct((8, D), F32)],
        scratch_shapes=[pltpu.VMEM((tm, D), F32)],
        compiler_params=_cparams(("arbitrary", "arbitrary")),
    )(*pieces, wperm, x, lnw, dyres)


def _qkv_fwd(p, convw, kind):
    t = p.shape[0]
    f = _qkv_act(kind)

    def body(p_ref, w_ref, o_ref):
        o_ref[...] = f(p_ref[...], w_ref[...])

    return pl.pallas_call(
        body, name=f"qkv_fwd{kind}", grid=(NA,),
        in_specs=[pl.BlockSpec((t, DA), lambda h: (0, kind * NA + h)),
                  pl.BlockSpec((8, DA), lambda h: (0, kind * NA + h))],
        out_specs=pl.BlockSpec((t, DA), lambda h: (0, h)),
        out_shape=jax.ShapeDtypeStruct((t, D), F32),
        compiler_params=_cparams(("parallel",)),
    )(p, convw)


def _qkv_bwd(p, convw, dout, kind):
    t = p.shape[0]
    f = _qkv_act(kind)

    def body(p_ref, w_ref, g_ref, dx_ref, dw_ref):
        _, vjp = jax.vjp(f, p_ref[...], w_ref[...])
        dx, dw = vjp(g_ref[...])
        dx_ref[...] = dx.astype(BF16)
        dw_ref[...] = dw

    return pl.pallas_call(
        body, name=f"qkv_bwd{kind}", grid=(NA,),
        in_specs=[pl.BlockSpec((t, DA), lambda h: (0, kind * NA + h)),
                  pl.BlockSpec((8, DA), lambda h: (0, kind * NA + h)),
                  pl.BlockSpec((t, DA), lambda h: (0, h))],
        out_specs=[pl.BlockSpec((t, DA), lambda h: (0, h)),
                   pl.BlockSpec((8, DA), lambda h: (0, h))],
        out_shape=[jax.ShapeDtypeStruct((t, D), BF16), jax.ShapeDtypeStruct((8, D), F32)],
        compiler_params=_cparams(("parallel",)),
    )(p, convw, dout)


def _gates_f(ps, alog_row, dt_row, w2f, b2f, w2b, b2b):
    lane = lax.broadcasted_iota(jnp.int32, ps.shape, 1)
    lg = -jnp.exp(alog_row) * _softplus(ps + dt_row)
    gsm = jnp.where(lane < 16, lg, jnp.where(lane < 32, _sigmoid(ps), 0.0))
    gkf = -_softplus(-(mm(ps, w2f) + b2f)) * (1.0 / 16.0)
    gkb = -_softplus(-(mm(ps, w2b) + b2b)) * (1.0 / 16.0)
    return gsm, gkf, gkb


def _gates_fwd(ps, alog_row, dt_row, w2f, b2f, w2b, b2b, tm=512):
    t = ps.shape[0]

    def body(ps_ref, a_ref, d_ref, wf_ref, bf_ref, wb_ref, bb_ref, gsm_ref, gk_ref):
        gsm, gkf, gkb = _gates_f(ps_ref[...], a_ref[...], d_ref[...], wf_ref[...], bf_ref[...],
                                 wb_ref[...], bb_ref[...])
        gsm_ref[...] = gsm
        gk_ref[0] = gkf
        gk_ref[1] = gkb

    row = lambda n: pl.BlockSpec((1, n), lambda i: (0, 0))
    mat = pl.BlockSpec((128, 512), lambda i: (0, 0))
    return pl.pallas_call(
        body, name="gates_fwd", grid=(t // tm,),
        in_specs=[pl.BlockSpec((tm, 128), lambda i: (i, PS_BLOCK)), row(128), row(128), mat, row(512), mat, row(512)],
        out_specs=[pl.BlockSpec((tm, 128), lambda i: (i, 0)),
                   pl.BlockSpec((2, tm, 512), lambda i: (0, i, 0))],
        out_shape=[jax.ShapeDtypeStruct((t, 128), F32), jax.ShapeDtypeStruct((2, t, 512), F32)],
        compiler_params=_cparams(("parallel",)),
    )(ps, alog_row, dt_row, w2f, b2f, w2b, b2b)


def _gates_bwd(ps, alog_row, dt_row, w2f, b2f, w2b, b2b, dgsm, dgk, tm=512):
    t = ps.shape[0]

    def body(ps_ref, a_ref, d_ref, wf_ref, bf_ref, wb_ref, bb_ref, dgsm_ref, dgk_ref,
             dps_ref, da_ref, dd_ref, dwf_ref, dbf_ref, dwb_ref, dbb_ref):
        _, vjp = jax.vjp(_gates_f, ps_ref[...], a_ref[...], d_ref[...], wf_ref[...], bf_ref[...],
                         wb_ref[...], bb_ref[...])
        dps, da, dd, dwf, dbf, dwb, dbb = vjp((dgsm_ref[...], dgk_ref[0], dgk_ref[1]))
        dps_ref[:, 0:128] = dps.astype(BF16)
        dps_ref[:, 128:DP_TILE] = jnp.zeros((tm, DP_TILE - 128), BF16)
        accs = ((da_ref, da), (dd_ref, dd), (dwf_ref, dwf), (dbf_ref, dbf), (dwb_ref, dwb), (dbb_ref, dbb))

        @pl.when(pl.program_id(0) == 0)
        def _():
            for ref, _ in accs:
                ref[...] = jnp.zeros_like(ref)
        for ref, val in accs:
            ref[...] += jnp.broadcast_to(val, ref.shape)

    row = lambda n: pl.BlockSpec((1, n), lambda i: (0, 0))
    row8 = lambda n: pl.BlockSpec((8, n), lambda i: (0, 0))
    mat = pl.BlockSpec((128, 512), lambda i: (0, 0))
    return pl.pallas_call(
        body, name="gates_bwd", grid=(t // tm,),
        in_specs=[pl.BlockSpec((tm, 128), lambda i: (i, PS_BLOCK)), row(128), row(128), mat, row(512), mat, row(512),
                  pl.BlockSpec((tm, 128), lambda i: (i, 0)),
                  pl.BlockSpec((2, tm, 512), lambda i: (0, i, 0))],
        out_specs=[pl.BlockSpec((tm, DP_TILE), lambda i: (i, 0)), row8(128), row8(128), mat, row8(512), mat,
                   row8(512)],
        out_shape=[jax.ShapeDtypeStruct((t, DP_TILE), BF16),
                   jax.ShapeDtypeStruct((8, 128), F32), jax.ShapeDtypeStruct((8, 128), F32),
                   jax.ShapeDtypeStruct((128, 512), F32), jax.ShapeDtypeStruct((8, 512), F32),
                   jax.ShapeDtypeStruct((128, 512), F32), jax.ShapeDtypeStruct((8, 512), F32)],
        compiler_params=_cparams(("arbitrary",)),
    )(ps, alog_row, dt_row, w2f, b2f, w2b, b2b, dgsm, dgk)


def _rows(i):
    return pl.ds(pl.multiple_of(i * CH, CH), CH)


def _srows(i):
    return pl.ds(pl.multiple_of(i * SC, SC), SC)


def _first_row(x):
    row = lax.broadcasted_iota(jnp.int32, (8, x.shape[1]), 0)
    return jnp.where(row == 0, jnp.broadcast_to(x, (8, x.shape[1])), 0.0)


def _chunk_rows(e_ref, i):
    pad = jnp.zeros((CH - 8, 128), F32)
    return jnp.concatenate([x for c in range(SC // CH) for x in (e_ref[(SC // CH) * i + c], pad)], axis=0)


def _gcum_f(gsm, tm):
    i = lax.broadcasted_iota(jnp.int32, (tm, tm), 0)
    j = lax.broadcasted_iota(jnp.int32, (tm, tm), 1)
    same = (i >> 6) == (j >> 6)
    lower = (same & (i >= j)).astype(F32)
    upper = (same & (i <= j)).astype(F32)
    r = lax.broadcasted_iota(jnp.int32, (128, D), 0)
    head = lax.broadcasted_iota(jnp.int32, (128, D), 1) >> 7
    pick = lambda off: (r == head + off).astype(F32)
    lane = lax.broadcasted_iota(jnp.int32, gsm.shape, 1)
    run = jnp.where(lane < 8, cmm(lower, gsm), cmm(upper, gsm))
    return mmc(run, pick(0)), mmc(run, pick(8)), mmc(gsm, pick(16)), mmc(gsm, pick(24))


def _gcum_fwd(gsm, tm=256):
    t = gsm.shape[0]

    def body(s_ref, g_ref, b_ref):
        gf, gb, bf, bb = _gcum_f(s_ref[...], tm)
        g_ref[0] = gf
        g_ref[1] = gb
        b_ref[0] = bf
        b_ref[1] = bb

    two = pl.BlockSpec((2, tm, D), lambda i: (0, i, 0))
    return pl.pallas_call(
        body, name="gcum_fwd", grid=(t // tm,),
        in_specs=[pl.BlockSpec((tm, 128), lambda i: (i, 0))], out_specs=[two, two],
        out_shape=[jax.ShapeDtypeStruct((2, t, D), F32)] * 2,
        compiler_params=_cparams(("parallel",)),
    )(gsm)


def _gcum_bwd(gsm, dg2, db2, tm=256):
    t = gsm.shape[0]

    def body(s_ref, dg_ref, db_ref, ds_ref):
        _, vjp = jax.vjp(lambda s: _gcum_f(s, tm), s_ref[...])
        ds_ref[...] = vjp((dg_ref[0], dg_ref[1], db_ref[0], db_ref[1]))[0]

    two = pl.BlockSpec((2, tm, D), lambda i: (0, i, 0))
    tile = pl.BlockSpec((tm, 128), lambda i: (i, 0))
    return pl.pallas_call(
        body, name="gcum_bwd", grid=(t // tm,),
        in_specs=[tile, two, two], out_specs=tile,
        out_shape=jax.ShapeDtypeStruct((t, 128), F32),
        compiler_params=_cparams(("parallel",)),
    )(gsm, dg2, db2)


def _gdn_intra_fwd(qn, kn, vc, g2, b2):
    t = qn.shape[0]
    n = t // CH

    def body(q_ref, k_ref, v_ref, g_ref, b_ref, u_ref, w_ref, a_ref, qd_ref, kd_ref, e_ref, t_ref):
        cs = _Consts(pl.program_id(0) == 1)

        def step(i, carry):
            r = _srows(i)
            q, k, v, g, bx = q_ref[r, :], k_ref[r, :], v_ref[r, :], g_ref[r, :], b_ref[r, :]
            u, w, a, qd, kd, el, tinv = _gdn_intra(q, k, v, g, bx, None, cs)
            u_ref[r, :] = u
            w_ref[r, :] = w
            a_ref[r, :] = a
            qd_ref[r, :] = qd
            kd_ref[r, :] = kd
            t_ref[r, :] = tinv
            for c in range(SC // CH):
                e_ref[(SC // CH) * i + c] = el[c * CH:c * CH + 8]
            return carry

        lax.fori_loop(0, t // SC, step, 0)

    head = pl.BlockSpec((t, DA), lambda d, h: (0, h))
    dh = pl.BlockSpec((None, t, DA), lambda d, h: (d, 0, h))
    sq = lambda w: pl.BlockSpec((None, None, t, w), lambda d, h: (d, h, 0, 0))
    big = jax.ShapeDtypeStruct((2, t, D), F32)
    return pl.pallas_call(
        body, name="gdn_intra_fwd", grid=(2, NA),
        in_specs=[head, head, head, dh, dh],
        out_specs=[dh, dh, sq(CH), dh, dh, pl.BlockSpec((None, None, n, 8, 128), lambda d, h: (d, h, 0, 0, 0)),
                   sq(SC)],
        out_shape=[big, big, jax.ShapeDtypeStruct((2, NA, t, CH), F32), big, big,
                   jax.ShapeDtypeStruct((2, NA, n, 8, 128), F32), jax.ShapeDtypeStruct((2, NA, t, SC), F32)],
        compiler_params=_cparams(("parallel", "parallel")),
    )(qn, kn, vc, g2, b2)


SCAN_TB = 256
SCAN_HB = 8


def _scan_specs(t, width, nheads, hb, along):
    nt = t // SCAN_TB
    nb = SCAN_TB // CH

    def tmap(d, tt):
        fwd = tt + d * (nt - 1 - 2 * tt)
        return fwd if along > 0 else nt - 1 - fwd

    tok = pl.BlockSpec((None, SCAN_TB, hb * width), lambda d, h, tt: (d, tmap(d, tt), h))
    per = lambda *tail: pl.BlockSpec((None, hb, nb) + tail, lambda d, h, tt: (d, h, tmap(d, tt)) + (0,) * len(tail))
    sq = pl.BlockSpec((None, hb, SCAN_TB, CH), lambda d, h, tt: (d, h, tmap(d, tt), 0))
    shared = lambda w: pl.BlockSpec((SCAN_TB, hb * w), lambda d, h, tt: (tmap(d, tt), h))
    return tok, per, sq, shared, (2, nheads // hb, nt), nb


def _gdn_scan_fwd(u, w, a, qd, kd, e):
    t = u.shape[1]
    tok, per, sq, _, grid, nb = _scan_specs(t, DA, NA, SCAN_HB, +1)

    def body(u_ref, w_ref, a_ref, qd_ref, kd_ref, e_ref, o_ref, s_ref, state):
        rev = pl.program_id(0) == 1

        @pl.when(pl.program_id(2) == 0)
        def _():
            state[...] = jnp.zeros_like(state)

        def step(i, ss):
            ci = jnp.where(rev, nb - 1 - i, i)
            r = _rows(ci)
            out = []
            for hh, s in enumerate(ss):
                c = slice(hh * DA, (hh + 1) * DA)
                s_ref[hh, ci] = s
                o, sn = _gdn_scan(u_ref[r, c], w_ref[r, c], a_ref[hh, r, :], qd_ref[r, c], kd_ref[r, c],
                                  e_ref[hh, ci][0:1], s)
                o_ref[r, c] = o
                out.append(sn)
            return tuple(out)

        ss = lax.fori_loop(0, nb, step, tuple(state[hh] for hh in range(SCAN_HB)))
        for hh, s in enumerate(ss):
            state[hh] = s

    return pl.pallas_call(
        body, name="gdn_scan_fwd", grid=grid,
        in_specs=[tok, tok, sq, tok, tok, per(8, 128)],
        out_specs=[tok, per(DA, DA)],
        out_shape=[jax.ShapeDtypeStruct((2, t, D), F32), jax.ShapeDtypeStruct((2, NA, t // CH, DA, DA), F32)],
        scratch_shapes=[pltpu.VMEM((SCAN_HB, DA, DA), F32)],
        compiler_params=_cparams(("parallel", "parallel", "arbitrary")),
    )(u, w, a, qd, kd, e)


def _gdn_scan_bwd(u, w, a, qd, kd, e, ssave, do):
    t = u.shape[1]
    tok, per, sq, shared, grid, nb = _scan_specs(t, DA, NA, SCAN_HB, -1)

    def body(u_ref, w_ref, a_ref, qd_ref, kd_ref, e_ref, s_ref, do_ref,
             du_ref, dw_ref, da_ref, dqd_ref, dkd_ref, de_ref, state):
        rev = pl.program_id(0) == 1

        @pl.when(pl.program_id(2) == 0)
        def _():
            state[...] = jnp.zeros_like(state)

        def step(i, dss):
            ci = jnp.where(rev, i, nb - 1 - i)
            r = _rows(ci)
            out = []
            for hh, ds in enumerate(dss):
                c = slice(hh * DA, (hh + 1) * DA)
                _, vjp = jax.vjp(_gdn_scan, u_ref[r, c], w_ref[r, c], a_ref[hh, r, :], qd_ref[r, c], kd_ref[r, c],
                                 e_ref[hh, ci][0:1], s_ref[hh, ci])
                du, dw, da, dqd, dkd, de, dsn = vjp((do_ref[r, c], ds))
                du_ref[r, c] = du
                dw_ref[r, c] = dw
                da_ref[hh, r, :] = da
                dqd_ref[r, c] = dqd
                dkd_ref[r, c] = dkd
                de_ref[hh, ci] = _first_row(de)
                out.append(dsn)
            return tuple(out)

        dss = lax.fori_loop(0, nb, step, tuple(state[hh] for hh in range(SCAN_HB)))
        for hh, ds in enumerate(dss):
            state[hh] = ds

    big = jax.ShapeDtypeStruct((2, t, D), F32)
    return pl.pallas_call(
        body, name="gdn_scan_bwd", grid=grid,
        in_specs=[tok, tok, sq, tok, tok, per(8, 128), per(DA, DA), shared(DA)],
        out_specs=[tok, tok, sq, tok, tok, per(8, 128)],
        out_shape=[big, big, jax.ShapeDtypeStruct((2, NA, t, CH), F32), big, big,
                   jax.ShapeDtypeStruct((2, NA, t // CH, 8, 128), F32)],
        scratch_shapes=[pltpu.VMEM((SCAN_HB, DA, DA), F32)],
        compiler_params=_cparams(("parallel", "parallel", "arbitrary")),
    )(u, w, a, qd, kd, e, ssave, do)


def _gdn_intra_bwd(qn, kn, vc, g2, b2, tinv, du, dw, da, dqd, dkd, de):
    t = qn.shape[0]
    n = t // CH

    def body(q_ref, k_ref, v_ref, g_ref, b_ref, t_ref, du_ref, dw_ref, da_ref, dqd_ref, dkd_ref, de_ref,
             dq_ref, dk_ref, dv_ref, dg_ref, db_ref):
        d = pl.program_id(1)
        cs = _Consts(d == 1)

        @pl.when(d == 0)
        def _():
            dq_ref[...] = jnp.zeros_like(dq_ref)
            dk_ref[...] = jnp.zeros_like(dk_ref)
            dv_ref[...] = jnp.zeros_like(dv_ref)

        def step(i, carry):
            r = _srows(i)
            tinv_c = t_ref[r, :]
            f = lambda q, k, v, g, bx: _gdn_intra(q, k, v, g, bx, tinv_c, cs)
            _, vjp = jax.vjp(f, q_ref[r, :], k_ref[r, :], v_ref[r, :], g_ref[r, :], b_ref[r, :])
            dq, dk, dv, dg, dbx = vjp((du_ref[r, :], dw_ref[r, :], da_ref[r, :], dqd_ref[r, :],
                                       dkd_ref[r, :], _chunk_rows(de_ref, i)))
            dq_ref[r, :] += dq
            dk_ref[r, :] += dk
            dv_ref[r, :] += dv
            dg_ref[r, :] = dg
            db_ref[r, :] = dbx
            return carry

        lax.fori_loop(0, t // SC, step, 0)

    head = pl.BlockSpec((t, DA), lambda h, d: (0, h))
    dh = pl.BlockSpec((None, t, DA), lambda h, d: (d, 0, h))
    sq = pl.BlockSpec((None, None, t, CH), lambda h, d: (d, h, 0, 0))
    tq = pl.BlockSpec((None, None, t, SC), lambda h, d: (d, h, 0, 0))
    full = jax.ShapeDtypeStruct((t, D), F32)
    big = jax.ShapeDtypeStruct((2, t, D), F32)
    return pl.pallas_call(
        body, name="gdn_intra_bwd", grid=(NA, 2),
        in_specs=[head, head, head, dh, dh, tq, dh, dh, sq, dh, dh,
                  pl.BlockSpec((None, None, n, 8, 128), lambda h, d: (d, h, 0, 0, 0))],
        out_specs=[head, head, head, dh, dh],
        out_shape=[full, full, full, big, big],
        compiler_params=_cparams(("arbitrary", "arbitrary")),
    )(qn, kn, vc, g2, b2, tinv, du, dw, da, dqd, dkd, de)


def _gla_specs(t, order):
    ix = (lambda d, h: (d, h)) if order == "dh" else (lambda h, d: (d, h))

    def mk(fn):
        return lambda a, b: fn(*ix(a, b))
    q = pl.BlockSpec((t, DKB), mk(lambda d, h: (0, 32 + h)))
    k = pl.BlockSpec((t, DKB), mk(lambda d, h: (0, 36 + h)))
    v = pl.BlockSpec((t, DVB), mk(lambda d, h: (0, 20 + h)))
    dk = pl.BlockSpec((None, t, DKB), mk(lambda d, h: (d, 0, h)))
    dv = pl.BlockSpec((None, t, DVB), mk(lambda d, h: (d, 0, h)))
    e = pl.BlockSpec((None, None, t // CH, 8, 128), mk(lambda d, h: (d, h, 0, 0, 0)))
    s = pl.BlockSpec((None, None, t // CH, DVB, DKB), mk(lambda d, h: (d, h, 0, 0, 0)))
    return q, k, v, dk, dv, e, s


def _gla_intra_fwd(p, gk):
    t = p.shape[0]
    n = t // CH

    def body(q_ref, k_ref, v_ref, g_ref, qg_ref, kd_ref, in_ref, e_ref):
        cs = _Consts(pl.program_id(0) == 1)

        def step(i, carry):
            r = _srows(i)
            qg, kd, intra, el = _gla_intra(q_ref[r, :], k_ref[r, :], v_ref[r, :], g_ref[r, :], cs)
            qg_ref[r, :] = qg
            kd_ref[r, :] = kd
            in_ref[r, :] = intra
            for c in range(SC // CH):
                e_ref[(SC // CH) * i + c] = el[c * CH:c * CH + 8]
            return carry

        lax.fori_loop(0, t // SC, step, 0)

    q, k, v, dk, dv, e, _ = _gla_specs(t, "dh")
    return pl.pallas_call(
        body, name="gla_intra_fwd", grid=(2, NB),
        in_specs=[q, k, v, dk], out_specs=[dk, dk, dv, e],
        out_shape=[jax.ShapeDtypeStruct((2, t, NB * DKB), F32), jax.ShapeDtypeStruct((2, t, NB * DKB), F32),
                   jax.ShapeDtypeStruct((2, t, D), F32), jax.ShapeDtypeStruct((2, NB, n, 8, 128), F32)],
        compiler_params=_cparams(("parallel", "parallel")),
    )(p, p, p, gk)


GLA_HB = 4


def _gla_v_spec(t, along):
    nt = t // SCAN_TB

    def tmap(d, tt):
        fwd = tt + d * (nt - 1 - 2 * tt)
        return fwd if along > 0 else nt - 1 - fwd

    return pl.BlockSpec((SCAN_TB, GLA_HB * DVB), lambda d, h, tt: (tmap(d, tt), 5120 // (GLA_HB * DVB) + h))


def _gla_scan_fwd(p, qg, kd, intra, e):
    t = p.shape[0]
    tokk, per, _, _, grid, nb = _scan_specs(t, DKB, NB, GLA_HB, +1)
    tokv = _scan_specs(t, DVB, NB, GLA_HB, +1)[0]

    def body(v_ref, qg_ref, kd_ref, in_ref, e_ref, o_ref, s_ref, state):
        rev = pl.program_id(0) == 1

        @pl.when(pl.program_id(2) == 0)
        def _():
            state[...] = jnp.zeros_like(state)

        def step(i, sts):
            ci = jnp.where(rev, nb - 1 - i, i)
            r = _rows(ci)
            out = []
            for hh, st in enumerate(sts):
                ck = slice(hh * DKB, (hh + 1) * DKB)
                cv = slice(hh * DVB, (hh + 1) * DVB)
                s_ref[hh, ci] = st
                o, stn = _gla_scan(qg_ref[r, ck], kd_ref[r, ck], v_ref[r, cv], e_ref[hh, ci][0:1], st)
                o_ref[r, cv] = o + in_ref[r, cv]
                out.append(stn)
            return tuple(out)

        sts = lax.fori_loop(0, nb, step, tuple(state[hh] for hh in range(GLA_HB)))
        for hh, st in enumerate(sts):
            state[hh] = st

    return pl.pallas_call(
        body, name="gla_scan_fwd", grid=grid,
        in_specs=[_gla_v_spec(t, +1), tokk, tokk, tokv, per(8, 128)], out_specs=[tokv, per(DVB, DKB)],
        out_shape=[jax.ShapeDtypeStruct((2, t, D), F32), jax.ShapeDtypeStruct((2, NB, t // CH, DVB, DKB), F32)],
        scratch_shapes=[pltpu.VMEM((GLA_HB, DVB, DKB), F32)],
        compiler_params=_cparams(("parallel", "parallel", "arbitrary")),
    )(p, qg, kd, intra, e)


def _gla_scan_bwd(p, qg, kd, e, ssave, do):
    t = p.shape[0]
    tokk, per, _, shared, grid, nb = _scan_specs(t, DKB, NB, GLA_HB, -1)
    tokv = _scan_specs(t, DVB, NB, GLA_HB, -1)[0]

    def body(v_ref, qg_ref, kd_ref, e_ref, s_ref, do_ref, dqg_ref, dkd_ref, dv_ref, de_ref, state):
        rev = pl.program_id(0) == 1

        @pl.when(pl.program_id(2) == 0)
        def _():
            state[...] = jnp.zeros_like(state)

        def step(i, dsts):
            ci = jnp.where(rev, i, nb - 1 - i)
            r = _rows(ci)
            out = []
            for hh, dst in enumerate(dsts):
                ck = slice(hh * DKB, (hh + 1) * DKB)
                cv = slice(hh * DVB, (hh + 1) * DVB)
                _, vjp = jax.vjp(_gla_scan, qg_ref[r, ck], kd_ref[r, ck], v_ref[r, cv], e_ref[hh, ci][0:1],
                                 s_ref[hh, ci])
                dqg, dkd, dv, de, dstn = vjp((do_ref[r, cv], dst))
                dqg_ref[r, ck] = dqg
                dkd_ref[r, ck] = dkd
                dv_ref[r, cv] = dv
                de_ref[hh, ci] = _first_row(de)
                out.append(dstn)
            return tuple(out)

        dsts = lax.fori_loop(0, nb, step, tuple(state[hh] for hh in range(GLA_HB)))
        for hh, dst in enumerate(dsts):
            state[hh] = dst

    return pl.pallas_call(
        body, name="gla_scan_bwd", grid=grid,
        in_specs=[_gla_v_spec(t, -1), tokk, tokk, per(8, 128), per(DVB, DKB), shared(DVB)],
        out_specs=[tokk, tokk, tokv, per(8, 128)],
        out_shape=[jax.ShapeDtypeStruct((2, t, NB * DKB), F32), jax.ShapeDtypeStruct((2, t, NB * DKB), F32),
                   jax.ShapeDtypeStruct((2, t, D), F32), jax.ShapeDtypeStruct((2, NB, t // CH, 8, 128), F32)],
        scratch_shapes=[pltpu.VMEM((GLA_HB, DVB, DKB), F32)],
        compiler_params=_cparams(("parallel", "parallel", "arbitrary")),
    )(p, qg, kd, e, ssave, do)


def _gla_intra_bwd(p, gk, dqg, dkd, dvs, de, do):
    t = p.shape[0]
    n = t // CH

    def body(q_ref, k_ref, v_ref, g_ref, dqg_ref, dkd_ref, dvs_ref, de_ref, do_ref,
             dq_ref, dk_ref, dv_ref, dg_ref):
        d = pl.program_id(1)
        cs = _Consts(d == 1)

        @pl.when(d == 0)
        def _():
            dq_ref[...] = jnp.zeros_like(dq_ref)
            dk_ref[...] = jnp.zeros_like(dk_ref)
            dv_ref[...] = jnp.zeros_like(dv_ref)

        def step(i, carry):
            r = _srows(i)
            f = lambda q, k, v, g: _gla_intra(q, k, v, g, cs)
            _, vjp = jax.vjp(f, q_ref[r, :], k_ref[r, :], v_ref[r, :], g_ref[r, :])
            dq, dk, dv, dg = vjp((dqg_ref[r, :], dkd_ref[r, :], do_ref[r, :], _chunk_rows(de_ref, i)))
            dq_ref[r, :] += dq
            dk_ref[r, :] += dk
            dv_ref[r, :] += dv + dvs_ref[r, :]
            dg_ref[r, :] = dg
            return carry

        lax.fori_loop(0, t // SC, step, 0)

    q, k, v, dk, dv, e_s, _ = _gla_specs(t, "hd")
    hk = pl.BlockSpec((t, DKB), lambda h, d: (0, h))
    hv = pl.BlockSpec((t, DVB), lambda h, d: (0, h))
    return pl.pallas_call(
        body, name="gla_intra_bwd", grid=(NB, 2),
        in_specs=[q, k, v, dk, dk, dk, dv, e_s, hv],
        out_specs=[hk, hk, hv, dk],
        out_shape=[jax.ShapeDtypeStruct((t, NB * DKB), F32), jax.ShapeDtypeStruct((t, NB * DKB), F32),
                   jax.ShapeDtypeStruct((t, D), F32), jax.ShapeDtypeStruct((2, t, NB * DKB), F32)],
        compiler_params=_cparams(("arbitrary", "arbitrary")),
    )(p, p, p, gk, dqg, dkd, dvs, de, do)


def _seg_gate(o, z, w):
    return _rms(o, w) * _silu(z)


def _seg_merge(ya, yb, ga, gb):
    return _sigmoid(ga) * ya + _sigmoid(gb) * yb


def _seg_loss(out, x, tgt, w):
    err = x + _rms(out, w) - tgt
    return 0.5 * jnp.sum(jnp.mean(err * err, axis=-1, keepdims=True), axis=0, keepdims=True)


def _post(oa2, ob2, p, x, tgt, gdn_w, gla_w, lnpost, w3, tm=128):
    t = x.shape[0]

    def body(oa_ref, ob_ref, z_ref, gb_ref, ga_ref, gB_ref, x_ref, t_ref, aw_ref, bw_ref, lw_ref, w_ref,
             loss_ref, doa_ref, dob_ref, dz_ref, dgb_ref, dga_ref, dgB_ref, dy_ref,
             dw_ref, daw_ref, dbw_ref, dlw_ref):
        first = pl.program_id(0) == 0
        oa = oa_ref[0] + oa_ref[1]
        ob = ob_ref[0] + ob_ref[1]
        z, gb = z_ref[...], gb_ref[...]
        aw, bw = aw_ref[...], bw_ref[...]
        rs = D // NSHARD

        def mat(a, m):
            return sum(jnp.dot(a[:, s * rs:(s + 1) * rs], w_ref[s, m], preferred_element_type=F32)
                       for s in range(NSHARD))

        def mat_t(g, m):
            return jnp.concatenate([_bdot(g, w_ref[s, m], 1, 1) for s in range(NSHARD)], axis=1)

        def add_dw(a, g, m):
            for s in range(NSHARD):
                dw_ref[s, m] += _bdot(a[:, s * rs:(s + 1) * rs], g, 0, 0)

        pa = [jax.vjp(_seg_gate, oa[:, h * DA:(h + 1) * DA], z[:, h * DA:(h + 1) * DA], aw) for h in range(NA)]
        pb = [jax.vjp(_seg_gate, ob[:, h * DVB:(h + 1) * DVB], gb[:, h * DVB:(h + 1) * DVB], bw)
              for h in range(NB)]
        a1 = jnp.concatenate([v for v, _ in pa], axis=1).astype(BF16)
        a2 = jnp.concatenate([v for v, _ in pb], axis=1).astype(BF16)
        ya = mat(a1, 0)
        yb = mat(a2, 1)
        merged, vjp_m = jax.vjp(_seg_merge, ya, yb, ga_ref[...], gB_ref[...])
        mb = merged.astype(BF16)
        out = mat(mb, 2)
        loss, vjp_l = jax.vjp(_seg_loss, out, x_ref[...], t_ref[...], lw_ref[...])
        dout, dyres, _, dlw = vjp_l(jnp.ones((1, 1), F32))
        dy_ref[...] = dyres
        doutb = dout.astype(BF16)
        dmerged = mat_t(doutb, 2)
        dya, dyb, dga, dgB = vjp_m(dmerged)
        dga_ref[...] = dga.astype(BF16)
        dgB_ref[...] = dgB.astype(BF16)
        dyab, dybb = dya.astype(BF16), dyb.astype(BF16)
        da1 = mat_t(dyab, 0)
        da2 = mat_t(dybb, 1)

        daw = jnp.zeros_like(aw)
        for h in range(NA):
            sl = slice(h * DA, (h + 1) * DA)
            do, dz, dw = pa[h][1](da1[:, sl])
            doa_ref[:, sl] = do
            dz_ref[:, sl] = dz.astype(BF16)
            daw = daw + dw
        dbw = jnp.zeros_like(bw)
        for h in range(NB):
            sl = slice(h * DVB, (h + 1) * DVB)
            do, dg, dw = pb[h][1](da2[:, sl])
            dob_ref[:, sl] = do
            dgb_ref[:, sl] = dg.astype(BF16)
            dbw = dbw + dw

        @pl.when(first)
        def _():
            loss_ref[...] = jnp.zeros_like(loss_ref)
            dw_ref[...] = jnp.zeros_like(dw_ref)
            daw_ref[...] = jnp.zeros_like(daw_ref)
            dbw_ref[...] = jnp.zeros_like(dbw_ref)
            dlw_ref[...] = jnp.zeros_like(dlw_ref)

        loss_ref[...] += jnp.broadcast_to(loss, loss_ref.shape)
        add_dw(a1, dyab, 0)
        add_dw(a2, dybb, 1)
        add_dw(mb, doutb, 2)
        daw_ref[...] += jnp.broadcast_to(daw, daw_ref.shape)
        dbw_ref[...] += jnp.broadcast_to(dbw, dbw_ref.shape)
        dlw_ref[...] += jnp.broadcast_to(dlw, dlw_ref.shape)

    two = pl.BlockSpec((2, tm, D), lambda i: (0, i, 0))
    pcol = lambda c: pl.BlockSpec((tm, D), lambda i: (i, c))
    tok = pl.BlockSpec((tm, D), lambda i: (i, 0))
    row = lambda n: pl.BlockSpec((1, n), lambda i: (0, 0))
    row8 = lambda n: pl.BlockSpec((8, n), lambda i: (0, 0))
    once = pl.Buffered(1)
    tokf = jax.ShapeDtypeStruct((t, D), F32)
    tokb = jax.ShapeDtypeStruct((t, D), BF16)
    wspec = pl.BlockSpec((NSHARD, 3, D // NSHARD, D), lambda i: (0, 0, 0, 0), pipeline_mode=once)
    return pl.pallas_call(
        body, name="post", grid=(t // tm,),
        in_specs=[two, two, pcol(3), pcol(6), pcol(7), pcol(8), tok, tok, row(DA), row(DVB), row(D), wspec],
        out_specs=[row8(128), tok, tok, tok, tok, tok, tok, tok, wspec, row8(DA), row8(DVB), row8(D)],
        out_shape=[jax.ShapeDtypeStruct((8, 128), F32), tokf, tokf, tokb, tokb, tokb, tokb, tokf,
                   jax.ShapeDtypeStruct((NSHARD, 3, D // NSHARD, D), F32),
                   jax.ShapeDtypeStruct((8, DA), F32), jax.ShapeDtypeStruct((8, DVB), F32),
                   jax.ShapeDtypeStruct((8, D), F32)],
        compiler_params=_cparams(("arbitrary",), vmem_mb=56),
    )(oa2, ob2, p, p, p, p, x, tgt, gdn_w, gla_w, lnpost, w3)


def _adam_math(w, g, m, v):
    nm = B1 * m + (1.0 - B1) * g
    nv = B2 * v + (1.0 - B2) * (g * g)
    m_hat = nm / (1.0 - B1 ** STEP)
    v_hat = nv / (1.0 - B2 ** STEP)
    return -LR * (m_hat / (jnp.sqrt(v_hat) + ADAM_EPS) + WD * w), nm, nv


SMALL_SLOTS = (("ln_pre_w", 0, 1024, 0), ("a_log_fwd", 1024, 8, 0), ("a_log_bwd", 1024, 8, 8),
               ("dt_bias_fwd", 1152, 8, 0), ("dt_bias_bwd", 1152, 8, 8), ("gdn_norm_w", 1280, 128, 0),
               ("gk_b2_fwd", 1408, 512, 0), ("gk_b2_bwd", 1920, 512, 0), ("gla_norm_w", 2432, 256, 0),
               ("ln_post_w", 2688, 1024, 0))
SMALL_W = 3840


def _adam_small(gsum, ws, ms, vs):
    nw = len(SMALL_SLOTS)

    def body(g_ref, *refs):
        w_refs, m_refs, v_refs, outs = refs[0:nw], refs[nw:2 * nw], refs[2 * nw:3 * nw], refs[3 * nw:]
        for i, (_, off, n, shift) in enumerate(SMALL_SLOTS):
            slot = g_ref[0:1, off:off + max(n, 128)]
            if shift:
                slot = pltpu.roll(slot, 128 - shift, 1)
            g = slot[:, 0:n]
            d, nm, nv = _adam_math(w_refs[i][...], g, m_refs[i][...], v_refs[i][...])
            for k, val in enumerate((g, d, nm, nv)):
                outs[4 * i + k][...] = val

    vm = pl.BlockSpec(memory_space=pltpu.VMEM)
    res = pl.pallas_call(
        body, name="adam_small", in_specs=[vm] * (1 + 3 * nw), out_specs=[vm] * (4 * nw),
        out_shape=[jax.ShapeDtypeStruct((1, n), F32) for _, _, n, _ in SMALL_SLOTS for _ in range(4)],
    )(gsum, *ws, *ms, *vs)
    return {name: res[4 * i:4 * i + 4] for i, (name, _, _, _) in enumerate(SMALL_SLOTS)}


def _adam(w, mine, got, m, v, tr, tile0=0, name=""):
    rows, cols = w.shape
    nh = mine.shape[0] // tr

    def body(c_ref, w_ref, a_ref, b_ref, m_ref, v_ref, g_ref, d_ref, nm_ref, nv_ref):
        half = (tile0 + pl.program_id(0)) // nh
        g = jnp.where(half == c_ref[0], a_ref[...], b_ref[...])
        d, nm, nv = _adam_math(w_ref[...], g, m_ref[...], v_ref[...])
        g_ref[...] = g
        d_ref[...] = d
        nm_ref[...] = nm
        nv_ref[...] = nv

    blk = pl.BlockSpec((tr, cols), lambda i, cc: (i, 0))
    half = pl.BlockSpec((tr, cols), lambda i, cc: ((tile0 + i) % nh, 0))
    shp = jax.ShapeDtypeStruct((rows, cols), F32)
    return pl.pallas_call(
        body, name=f"adam_{name}{rows}x{cols}",
        grid_spec=pltpu.PrefetchScalarGridSpec(
            num_scalar_prefetch=1, grid=(rows // tr,),
            in_specs=[blk, half, half, blk, blk], out_specs=[blk] * 4),
        out_shape=[shp] * 4,
        compiler_params=_cparams(("parallel",)),
    )(lax.axis_index("c").reshape(1), w, mine, got, m, v)


def _sum_cast(own, got):
    ns, _, r, c = own.shape
    tr = r // 4 if r % 64 == 0 else r

    def body(c_ref, a_ref, b_ref, f_ref, h_ref):
        s = a_ref[...] + b_ref[...]
        f_ref[...] = s
        h_ref[...] = s.astype(BF16)

    return pl.pallas_call(
        body, name=f"sum_cast_{r}x{c}",
        grid_spec=pltpu.PrefetchScalarGridSpec(
            num_scalar_prefetch=1, grid=(ns, r // tr),
            in_specs=[pl.BlockSpec((None, None, tr, c), lambda s, i, cc: (s, cc[0], i, 0)),
                      pl.BlockSpec((None, tr, c), lambda s, i, cc: (s, i, 0))],
            out_specs=[pl.BlockSpec((None, tr, c), lambda s, i, cc: (s, i, 0)),
                       pl.BlockSpec((None, tr, c), lambda s, i, cc: (s, i, 0))]),
        out_shape=[jax.ShapeDtypeStruct((ns, r, c), F32), jax.ShapeDtypeStruct((ns, r, c), BF16)],
        compiler_params=_cparams(("parallel", "parallel")),
    )(lax.axis_index("c").reshape(1), own, got)


def _sum4(mine, got):
    _, r, c = mine.shape
    tr = r // 4 if r % 64 == 0 else r

    def body(s_ref, a_ref, g_ref, o_ref):
        acc = a_ref[...] + g_ref[0].astype(F32)
        acc = acc + g_ref[1].astype(F32)
        o_ref[...] = acc + g_ref[2].astype(F32)

    shard = (2 * lax.axis_index("x") + lax.axis_index("y")).reshape(1)
    return pl.pallas_call(
        body, name=f"sum4_{r}x{c}",
        grid_spec=pltpu.PrefetchScalarGridSpec(
            num_scalar_prefetch=1, grid=(r // tr,),
            in_specs=[pl.BlockSpec((None, tr, c), lambda i, ss: (ss[0], i, 0)),
                      pl.BlockSpec((3, tr, c), lambda i, ss: (0, i, 0))],
            out_specs=pl.BlockSpec((tr, c), lambda i, ss: (i, 0))),
        out_shape=jax.ShapeDtypeStruct((r, c), F32),
        compiler_params=_cparams(("parallel",)),
    )(shard, mine, got)


def _place():
    x, y, c = lax.axis_index("x"), lax.axis_index("y"), lax.axis_index("c")
    chips = [(1 - x, y), (x, 1 - y), (1 - x, 1 - y)]
    return x, y, c, chips


def _gather_weights(parts):
    npart = len(parts)

    def body(*refs):
        ins, outs = refs[:npart], refs[npart:2 * npart]
        send_sems, recv_sems = refs[2 * npart:]
        x, y, c, chips = _place()
        sibling = (x, y, 1 - c)
        mine = 2 * x + y

        def remote(k, p, shard, half, to, src=None):
            dst = outs[p].at[shard, half]
            return pltpu.make_async_remote_copy(
                src_ref=dst if src is None else src, dst_ref=dst,
                send_sem=send_sems.at[k], recv_sem=recv_sems.at[k], device_id=to, device_id_type=MESH)

        first = [remote(j * npart + p, p, mine, c, (*chip, c), src=ins[p].at[c])
                 for j, chip in enumerate(chips) for p in range(npart)]
        for cp in first:
            cp.start()
        passed = []
        for j, (cx, cy) in enumerate(chips):
            for p in range(npart):
                remote(j * npart + p, p, 2 * cx + cy, c, (x, y, c)).wait_recv()
                fw = remote((3 + j) * npart + p, p, 2 * cx + cy, c, sibling)
                fw.start()
                passed.append(fw)
        for j, (cx, cy) in enumerate(chips):
            for p in range(npart):
                remote((3 + j) * npart + p, p, 2 * cx + cy, 1 - c, (x, y, c)).wait_recv()
        for cp in first + passed:
            cp.wait_send()

    got = pl.pallas_call(
        body, name="gather_weights",
        in_specs=[ANY] * npart, out_specs=[ANY] * npart,
        out_shape=[jax.ShapeDtypeStruct((NSHARD,) + a.shape, a.dtype) for a in parts],
        scratch_shapes=[pltpu.SemaphoreType.DMA((6 * npart,)), pltpu.SemaphoreType.DMA((6 * npart,))],
    )(*parts)
    mine = 2 * lax.axis_index("x") + lax.axis_index("y")
    return [lax.dynamic_update_index_in_dim(g, a, mine, 0) for g, a in zip(got, parts)]


def _swap_halves(parts, tag=""):
    npart = len(parts)

    def body(*refs):
        ins, outs = refs[:npart], refs[npart:2 * npart]
        send_sems, recv_sems = refs[2 * npart:]
        x, y, c, _ = _place()
        cps = [pltpu.make_async_remote_copy(
            src_ref=ins[p].at[s, 1 - c], dst_ref=outs[p].at[s],
            send_sem=send_sems.at[s * npart + p], recv_sem=recv_sems.at[s * npart + p],
            device_id=(x, y, 1 - c), device_id_type=MESH) for s in range(NSHARD) for p in range(npart)]
        for cp in cps:
            cp.start()
        for cp in cps:
            cp.wait()

    return pl.pallas_call(
        body, name="swap_halves" + tag, in_specs=[ANY] * npart, out_specs=[ANY] * npart,
        out_shape=[jax.ShapeDtypeStruct((NSHARD,) + a.shape[2:], a.dtype) for a in parts],
        scratch_shapes=[pltpu.SemaphoreType.DMA((NSHARD * npart,)), pltpu.SemaphoreType.DMA((NSHARD * npart,))],
    )(*parts)


def _scatter_shards(parts):
    npart = len(parts)

    def body(*refs):
        ins, outs = refs[:npart], refs[npart:2 * npart]
        send_sems, recv_sems = refs[2 * npart:]
        x, y, c, chips = _place()
        cps = [pltpu.make_async_remote_copy(
            src_ref=ins[p].at[2 * cx + cy], dst_ref=outs[p].at[j],
            send_sem=send_sems.at[j * npart + p], recv_sem=recv_sems.at[j * npart + p],
            device_id=(cx, cy, c), device_id_type=MESH)
            for j, (cx, cy) in enumerate(chips) for p in range(npart)]
        for cp in cps:
            cp.start()
        for cp in cps:
            cp.wait()

    return pl.pallas_call(
        body, name="scatter_shards", in_specs=[ANY] * npart, out_specs=[ANY] * npart,
        out_shape=[jax.ShapeDtypeStruct((3,) + a.shape[1:], a.dtype) for a in parts],
        scratch_shapes=[pltpu.SemaphoreType.DMA((3 * npart,)), pltpu.SemaphoreType.DMA((3 * npart,))],
    )(*parts)


HBM = pl.BlockSpec(memory_space=pltpu.HBM)
SEM = pl.BlockSpec(memory_space=pltpu.SEMAPHORE)
EFFECT = pltpu.SideEffectType.DATAFLOW_SIDE_EFFECTING


def _scatter_copies(srcs, lands, send_sems, recv_sems, waiting):
    x, y, c, chips = _place()
    n = len(srcs)
    return [pltpu.make_async_remote_copy(
        src_ref=srcs[p].at[2 * cx + cy], dst_ref=lands[p].at[j],
        send_sem=send_sems.at[j * n + p], recv_sem=recv_sems.at[j * n + p],
        device_id=(cx, cy, c), device_id_type=MESH)
        for j, (cx, cy) in enumerate(chips) for p in range(n)]


def _proj_copies(srcs, lands, send_sems, recv_sems, waiting):
    x, y, c, chips = _place()
    mine = 2 * x + y
    return [pltpu.make_async_remote_copy(
        src_ref=srcs[0].at[c], dst_ref=lands[0].at[mine, c],
        send_sem=send_sems.at[2 * j + to], recv_sem=recv_sems.at[2 * j + (to if waiting else c)],
        device_id=(cx, cy, to), device_id_type=MESH)
        for j, (cx, cy) in enumerate(chips) for to in range(2)]


def _start_copies(copies, nsem, parts, lands, name, after=None):
    n = len(parts)
    extra = [] if after is None else [after]

    def body(*refs):
        outs = refs[2 * n + len(extra):]
        for cp in copies(refs[:n], refs[n:2 * n], outs[0], outs[1], False):
            cp.start()
        outs[-1][...] = jnp.zeros_like(outs[-1])

    res = pl.pallas_call(
        body, name=name,
        out_shape=(pltpu.SemaphoreType.DMA((nsem,)), pltpu.SemaphoreType.DMA((nsem,)),
                   *[pltpu.HBM(a.shape, a.dtype) for a in parts], *[pltpu.HBM(a.shape, a.dtype) for a in lands],
                   jax.ShapeDtypeStruct((8, 128), F32)),
        in_specs=[HBM] * (2 * n) + [ANY] * len(extra),
        out_specs=(SEM, SEM, *[HBM] * (2 * n), pl.BlockSpec(memory_space=pltpu.VMEM)),
        input_output_aliases={i: 2 + i for i in range(2 * n)},
        compiler_params=pltpu.CompilerParams(has_side_effects=EFFECT),
    )(*[pltpu.with_memory_space_constraint(a, pltpu.HBM) for a in parts],
      *[pltpu.with_memory_space_constraint(a, pltpu.HBM) for a in lands], *extra)
    return res[0], res[1], res[2:2 + n], res[2 + n:2 + 2 * n], res[-1]


def _wait_copies(copies, started, after, name):
    send_sems, recv_sems, srcs, lands, _ = started
    n = len(srcs)

    def body(*refs):
        for cp in copies(refs[:n], refs[n:2 * n], refs[2 * n], refs[2 * n + 1], True):
            cp.wait_send()
            cp.wait_recv()

    res = pl.pallas_call(
        body, name=name,
        out_shape=tuple(pltpu.HBM(a.shape, a.dtype) for a in (*srcs, *lands)),
        in_specs=[HBM] * (2 * n) + [SEM, SEM, ANY], out_specs=tuple([HBM] * (2 * n)),
        input_output_aliases={i: i for i in range(2 * n)},
        compiler_params=pltpu.CompilerParams(has_side_effects=EFFECT),
    )(*srcs, *lands, send_sems, recv_sems, after)
    return res[n:]


def _join_halves(parts):
    npart = len(parts)

    def body(*refs):
        ins, outs = refs[:npart], refs[npart:2 * npart]
        send_sems, recv_sems = refs[2 * npart:]
        x, y, c, _ = _place()
        cps = [pltpu.make_async_remote_copy(
            src_ref=ins[p], dst_ref=outs[p], send_sem=send_sems.at[p], recv_sem=recv_sems.at[p],
            device_id=(x, y, 1 - c), device_id_type=MESH) for p in range(npart)]
        for cp in cps:
            cp.start()
        for cp in cps:
            cp.wait()

    return pl.pallas_call(
        body, name="join_halves", in_specs=[ANY] * npart, out_specs=[ANY] * npart,
        out_shape=[jax.ShapeDtypeStruct(a.shape, a.dtype) for a in parts],
        scratch_shapes=[pltpu.SemaphoreType.DMA((npart,)), pltpu.SemaphoreType.DMA((npart,))],
    )(*parts)


def _allreduce_small(v):
    r, ncol = v.shape

    def body(v_ref, o_ref, buf, send_sems, recv_sems):
        x, y, c, _ = _place()
        me = 4 * x + 2 * y + c
        buf[me] = v_ref[...]
        cps = []
        for k in range(1, 8):
            px, py, pc = x ^ (k >> 2), y ^ ((k >> 1) & 1), c ^ (k & 1)
            cps.append(pltpu.make_async_remote_copy(
                src_ref=v_ref, dst_ref=buf.at[me], send_sem=send_sems.at[k - 1], recv_sem=recv_sems.at[k - 1],
                device_id=(px, py, pc), device_id_type=MESH))
        for cp in cps:
            cp.start()
        for k in range(1, 8):
            px, py, pc = x ^ (k >> 2), y ^ ((k >> 1) & 1), c ^ (k & 1)
            pltpu.make_async_remote_copy(
                src_ref=v_ref, dst_ref=buf.at[4 * px + 2 * py + pc], send_sem=send_sems.at[k - 1],
                recv_sem=recv_sems.at[k - 1], device_id=(px, py, pc), device_id_type=MESH).wait_recv()
        for cp in cps:
            cp.wait_send()
        acc = buf[0]
        for d in range(1, 8):
            acc = acc + buf[d]
        o_ref[...] = acc

    return pl.pallas_call(
        body, name="allreduce_small",
        in_specs=[pl.BlockSpec(memory_space=pltpu.VMEM)], out_specs=pl.BlockSpec(memory_space=pltpu.VMEM),
        out_shape=jax.ShapeDtypeStruct((r, ncol), F32),
        scratch_shapes=[pltpu.VMEM((8, r, ncol), F32), pltpu.SemaphoreType.DMA((7,)), pltpu.SemaphoreType.DMA((7,))],
    )(v)


def _permute_rows(shards):
    w0, w1, w2, w3 = shards
    zeros = jnp.zeros((NPERM - 9280, w0.shape[1]), w0.dtype)
    return jnp.concatenate([w0, w1[0:1776], w1[1808:2320], w2, w3[0:240], w3[272:2320],
                            w1[1776:1808], w3[240:272], zeros], axis=0)


def _unpermute_rows(g):
    s1 = jnp.concatenate([g[2320:4096], g[9216:9248], g[4096:4608]], axis=0)
    s3 = jnp.concatenate([g[6928:7168], g[9248:9280], g[7168:9216]], axis=0)
    return jnp.stack([g[0:2320], s1, g[4608:6928], s3], axis=0)


def _pack_shard_small(conv, w2f, w2b):
    top = jnp.pad(conv, ((0, 8 - conv.shape[0]), (0, 0)))
    mid = jnp.pad(jnp.concatenate([w2f, w2b], axis=1), ((0, 0), (0, 768 - 256)))
    return jnp.concatenate([top, mid, jnp.zeros((8, 768), conv.dtype)], axis=0)


def _unpack_shard_small(a):
    return a[0:5], a[8:24, 0:128], a[8:24, 128:256]


def kernel(x, ln_pre_w, w_in, conv_w, a_log_fwd, a_log_bwd, dt_bias_fwd, dt_bias_bwd, gdn_norm_w, w_proj_gdn, gk_w2_fwd, gk_b2_fwd, gk_w2_bwd, gk_b2_bwd, gla_norm_w, w_proj_gla, w_out, ln_post_w, loss_target, m_ln_pre_w, m_w_in, m_conv_w, m_a_log_fwd, m_a_log_bwd, m_dt_bias_fwd, m_dt_bias_bwd, m_gdn_norm_w, m_w_proj_gdn, m_gk_w2_fwd, m_gk_b2_fwd, m_gk_w2_bwd, m_gk_b2_bwd, m_gla_norm_w, m_w_proj_gla, m_w_out, m_ln_post_w, v_ln_pre_w, v_w_in, v_conv_w, v_a_log_fwd, v_a_log_bwd, v_dt_bias_fwd, v_dt_bias_bwd, v_gdn_norm_w, v_w_proj_gdn, v_gk_w2_fwd, v_gk_b2_fwd, v_gk_w2_bwd, v_gk_b2_bwd, v_gla_norm_w, v_w_proj_gla, v_w_out, v_ln_post_w):
    t = x.shape[1]
    x2, tgt = x[0], loss_target[0]

    win_l = w_in[0].T.astype(BF16).reshape(2, SHW // 2, D)
    proj_l = jnp.concatenate([w_proj_gdn[0], w_proj_gla[0], w_out[0]], axis=0).astype(BF16).reshape(2, 384, D)
    small_l = _pack_shard_small(conv_w[0], gk_w2_fwd[0], gk_w2_bwd[0]).reshape(2, 16, 768)
    win_g, small_g = _gather_weights([win_l, small_l])
    proj_started = _start_copies(_proj_copies, 6, [proj_l], [lax.empty((NSHARD, 2, 384, D), BF16)],
                                 "gather_proj_start", after=small_g)
    wperm = _permute_rows(win_g.reshape(NSHARD, SHW, D))
    small_g = small_g.reshape(NSHARD, 32, 768)
    convw = small_g[:, 0:8, :].transpose(1, 0, 2).reshape(8, 3 * D)
    w2f = small_g[:, 8:24, 0:128].transpose(1, 0, 2).reshape(16, 512)
    w2b = small_g[:, 8:24, 128:256].transpose(1, 0, 2).reshape(16, 512)
    w2f_pad = jnp.pad(w2f, ((32, 80), (0, 0)))
    w2b_pad = jnp.pad(w2b, ((48, 64), (0, 0)))
    alog_row = jnp.pad(jnp.concatenate([a_log_fwd, a_log_bwd], axis=1), ((0, 0), (0, 112)))
    dt_row = jnp.pad(jnp.concatenate([dt_bias_fwd, dt_bias_bwd], axis=1), ((0, 0), (0, 112)))

    p, h = _inproj(x2, ln_pre_w + proj_started[4][0:1, 0:1], wperm)
    qn, kn, vc = (_qkv_fwd(p, convw, kind) for kind in range(3))
    gsm, gk = _gates_fwd(p, alog_row, dt_row, w2f_pad, gk_b2_fwd, w2b_pad, gk_b2_bwd)
    g2, b2 = _gcum_fwd(gsm)
    u, w, at, qd, kd, el, tinv = _gdn_intra_fwd(qn, kn, vc, g2, b2)
    oa2, sa = _gdn_scan_fwd(u, w, at, qd, kd, el)
    qg, kdb, intra, elb = _gla_intra_fwd(p, gk)
    ob2, sb = _gla_scan_fwd(p, qg, kdb, intra, elb)

    (proj_land,) = _wait_copies(_proj_copies, proj_started, ob2, "gather_proj_wait")
    mine = 2 * lax.axis_index("x") + lax.axis_index("y")
    w3 = lax.dynamic_update_index_in_dim(proj_land, proj_l, mine, 0).reshape(NSHARD, 3, D // NSHARD, D)
    (loss8, doa, dob, dz, dgb, dga, dgB, dyres, dw3, dgdn_w, dgla_w, dlnpost) = _post(
        oa2, ob2, p, x2, tgt, gdn_norm_w, gla_norm_w, ln_post_w, w3)

    g_proj = dw3.reshape(NSHARD, 2, 384, D)
    sum_proj = _sum_cast(g_proj, _swap_halves([g_proj], "_proj")[0])
    started_proj = _start_copies(_scatter_copies, 3, [sum_proj[1]], [lax.empty((3, 384, D), BF16)],
                                 "scatter_proj_start")
    du, dw, dat, dqd, dkd, del_ = _gdn_scan_bwd(u, w, at, qd, kd, el + started_proj[4][0, 0], sa, doa)
    dqn, dkn, dvc, dg2, db2 = _gdn_intra_bwd(qn, kn, vc, g2, b2, tinv, du, dw, dat, dqd, dkd, del_)
    dgsm = _gcum_bwd(gsm, dg2, db2)
    dqg, dkdb, dvs, delb = _gla_scan_bwd(p, qg, kdb, elb, sb, dob)
    dqb, dkb, dvb, dgk = _gla_intra_bwd(p, gk, dqg, dkdb, dvs, delb, dob)
    (dps, dalog8, ddt8, dw2f_pad, db2f8, dw2b_pad, db2b8) = _gates_bwd(
        p, alog_row, dt_row, w2f_pad, gk_b2_fwd, w2b_pad, gk_b2_bwd, dgsm, dgk)
    dpre, dconv = zip(*[_qkv_bwd(p, convw, g, kind) for kind, g in enumerate((dqn, dkn, dvc))])

    pieces = (jnp.concatenate([a.astype(BF16) for a in (*dpre, dz, dqb, dkb, dvb, dgb, dga, dgB, dps)], axis=1),)
    dwperm = _inproj_dw(h, pieces)

    g_in = _unpermute_rows(dwperm).reshape(NSHARD, 2, SHW // 2, D)
    dconv_full = jnp.concatenate(dconv, axis=1)
    dw2f, dw2b = dw2f_pad[32:48], dw2b_pad[48:64]
    g_small = jnp.stack([_pack_shard_small(dconv_full[0:5, 768 * s:768 * (s + 1)],
                                           dw2f[:, 128 * s:128 * (s + 1)], dw2b[:, 128 * s:128 * (s + 1)])
                         for s in range(NSHARD)])
    g_small = g_small.reshape(NSHARD, 2, 16, 768)
    parts = [g_in, g_small]
    got = _swap_halves(parts)
    sums = [_sum_cast(a, b) for a, b in zip(parts, got)]
    hbs = [hb for _, hb in sums]
    started = _start_copies(_scatter_copies, 3 * len(hbs), hbs,
                            [lax.empty((3,) + a.shape[1:], a.dtype) for a in hbs], "scatter_start")
    dx, dlnpre8 = _inproj_dx(pieces, wperm, x2, ln_pre_w + started[4][0:1, 0:1], dyres)

    gsmall = _allreduce_small(jnp.concatenate(
        [dlnpre8, dalog8, ddt8, dgdn_w, db2f8, db2b8, dgla_w, dlnpost, loss8], axis=1))
    smalls = dict(ln_pre_w=(ln_pre_w, m_ln_pre_w, v_ln_pre_w), a_log_fwd=(a_log_fwd, m_a_log_fwd, v_a_log_fwd),
                  a_log_bwd=(a_log_bwd, m_a_log_bwd, v_a_log_bwd),
                  dt_bias_fwd=(dt_bias_fwd, m_dt_bias_fwd, v_dt_bias_fwd),
                  dt_bias_bwd=(dt_bias_bwd, m_dt_bias_bwd, v_dt_bias_bwd),
                  gdn_norm_w=(gdn_norm_w, m_gdn_norm_w, v_gdn_norm_w),
                  gk_b2_fwd=(gk_b2_fwd, m_gk_b2_fwd, v_gk_b2_fwd), gk_b2_bwd=(gk_b2_bwd, m_gk_b2_bwd, v_gk_b2_bwd),
                  gla_norm_w=(gla_norm_w, m_gla_norm_w, v_gla_norm_w), ln_post_w=(ln_post_w, m_ln_post_w, v_ln_post_w))
    names = [name for name, _, _, _ in SMALL_SLOTS]
    small = _adam_small(gsmall, *([smalls[n][i] for n in names] for i in range(3)))

    landed_proj = _wait_copies(_scatter_copies, started_proj, small["ln_pre_w"][1], "scatter_proj_wait")
    landed = _wait_copies(_scatter_copies, started, small["ln_pre_w"][1], "scatter_wait")
    sums = [sums[0], sum_proj, sums[1]]
    halves = [_sum4(f, g) for (f, _), g in zip(sums, [landed[0], landed_proj[0], landed[1]])]
    theirs = _join_halves(halves)

    a_in = [a.T for a in _adam(w_in[0].T, halves[0], theirs[0], m_w_in[0].T, v_w_in[0].T, 232, name="in")]
    a_pr = [_adam(w[0], halves[1], theirs[1], m[0], v[0], 128, tile0=2 * i, name=f"proj{i}")
            for i, (w, m, v) in enumerate(((w_proj_gdn, m_w_proj_gdn, v_w_proj_gdn),
                                           (w_proj_gla, m_w_proj_gla, v_w_proj_gla), (w_out, m_w_out, v_w_out)))]
    a_ss = _adam(_pack_shard_small(conv_w[0], gk_w2_fwd[0], gk_w2_bwd[0]), halves[2], theirs[2],
                 _pack_shard_small(m_conv_w[0], m_gk_w2_fwd[0], m_gk_w2_bwd[0]),
                 _pack_shard_small(v_conv_w[0], v_gk_w2_fwd[0], v_gk_w2_bwd[0]), 16, name="small")

    def family(k):
        conv, w2f_, w2b_ = _unpack_shard_small(a_ss[k])
        s = {n: small[n][k] for n in names}
        return [s["ln_pre_w"], a_in[k][None], conv[None], s["a_log_fwd"], s["a_log_bwd"], s["dt_bias_fwd"],
                s["dt_bias_bwd"], s["gdn_norm_w"], a_pr[0][k][None], w2f_[None], s["gk_b2_fwd"], w2b_[None],
                s["gk_b2_bwd"], s["gla_norm_w"], a_pr[1][k][None], a_pr[2][k][None], s["ln_post_w"]]

    return (gsmall[0, SMALL_W - 128], dx[None], *family(0), *family(1), *family(2), *family(3))
```

```python
import functools

import jax
import jax.numpy as jnp
from jax import lax
from jax.experimental import pallas as pl
from jax.experimental.pallas import tpu as pltpu

F32 = jnp.float32
BF16 = jnp.bfloat16
HI = lax.Precision.HIGHEST
MESH = pl.DeviceIdType.MESH

D = 1024
CH = 64
EPS = 1e-6
NA, DA = 8, 128
NB, DKB, DVB = 4, 128, 256
NSHARD = 4
SHW = 2320
NPERM = 9728
PS_BLOCK = 72
LR, B1, B2, ADAM_EPS, WD, STEP = 0.001, 0.9, 0.999, 1e-08, 0.01, 10

ANY = pl.BlockSpec(memory_space=pl.ANY)


def _cparams(sem=None, vmem_mb=48):
    return pltpu.CompilerParams(dimension_semantics=sem, vmem_limit_bytes=vmem_mb << 20)


def _bdot(a, b, ca, cb):
    return lax.dot_general(a.astype(BF16), b.astype(BF16), (((ca,), (cb,)), ((), ())),
                           preferred_element_type=F32)


@jax.custom_vjp
def mm(a, b):
    return _bdot(a, b, 1, 0)


def _mm_fwd(a, b):
    return _bdot(a, b, 1, 0), (a, b)


def _mm_bwd(res, g):
    a, b = res
    return _bdot(g, b, 1, 1), _bdot(a, g, 0, 0)


mm.defvjp(_mm_fwd, _mm_bwd)


@jax.custom_vjp
def mm_nt(a, b):
    return _bdot(a, b, 1, 1)


def _mm_nt_fwd(a, b):
    return _bdot(a, b, 1, 1), (a, b)


def _mm_nt_bwd(res, g):
    a, b = res
    return _bdot(g, b, 1, 0), _bdot(g, a, 0, 0)


mm_nt.defvjp(_mm_nt_fwd, _mm_nt_bwd)


@jax.custom_vjp
def mm_tn(a, b):
    return _bdot(a, b, 0, 0)


def _mm_tn_fwd(a, b):
    return _bdot(a, b, 0, 0), (a, b)


def _mm_tn_bwd(res, g):
    a, b = res
    return _bdot(b, g, 1, 1), _bdot(a, g, 1, 0)


mm_tn.defvjp(_mm_tn_fwd, _mm_tn_bwd)


def dot_hi(a, b):
    return lax.dot_general(a, b, (((1,), (0,)), ((), ())), precision=HI, preferred_element_type=F32)


def _split3(x):
    x1 = x.astype(BF16)
    r = x - x1.astype(F32)
    x2 = r.astype(BF16)
    return x1, x2, (r - x2.astype(F32)).astype(BF16)


def _cdot(c, x, cc, cx, c_first=True):
    parts = _split3(x)
    if c_first:
        return _bdot(c, parts[0], cc, cx) + _bdot(c, parts[1], cc, cx) + _bdot(c, parts[2], cc, cx)
    return _bdot(parts[0], c, cx, cc) + _bdot(parts[1], c, cx, cc) + _bdot(parts[2], c, cx, cc)


@jax.custom_vjp
def cmm(c, x):
    return _cdot(c, x, 1, 0)


def _cmm_fwd(c, x):
    return _cdot(c, x, 1, 0), c


def _cmm_bwd(c, g):
    return jnp.zeros_like(c), _cdot(c, g, 0, 0)


cmm.defvjp(_cmm_fwd, _cmm_bwd)


@jax.custom_vjp
def mmc(x, c):
    return _cdot(c, x, 0, 1, c_first=False)


def _mmc_fwd(x, c):
    return _cdot(c, x, 0, 1, c_first=False), c


def _mmc_bwd(c, g):
    return _cdot(c, g, 1, 1, c_first=False), jnp.zeros_like(c)


mmc.defvjp(_mmc_fwd, _mmc_bwd)


def _sigmoid(x):
    return 1.0 / (1.0 + jnp.exp(-x))


def _silu(x):
    return x * _sigmoid(x)


def _softplus(x):
    return jnp.maximum(x, 0.0) + jnp.log(1.0 + jnp.exp(-jnp.abs(x)))


def _rms(x, w):
    return x * lax.rsqrt(jnp.mean(x * x, axis=-1, keepdims=True) + EPS) * w


SC = 256


class _Consts:
    def __init__(self, rev):
        r = lax.broadcasted_iota(jnp.int32, (SC, SC), 0)
        c = lax.broadcasted_iota(jnp.int32, (SC, SC), 1)
        same = (r >> 6) == (c >> 6)
        a = jnp.where(rev, c, r)
        b = jnp.where(rev, r, c)
        self.incl = same & (a >= b)
        self.strict = same & (a > b)
        self.incl_f = self.incl.astype(F32)
        self.eye = (r == c).astype(F32)
        rows = lax.broadcasted_iota(jnp.int32, (SC, 1), 0)
        self.last_col = ((rows & (CH - 1)) == jnp.where(rev, 0, CH - 1)).astype(F32)
        rr = lax.broadcasted_iota(jnp.int32, (SC, CH), 0)
        cc = lax.broadcasted_iota(jnp.int32, (SC, CH), 1)
        self.fold = ((rr & (CH - 1)) == cc).astype(F32)


def _dot3(a, b, ca=1, cb=0):
    ah, bh = a.astype(BF16), b.astype(BF16)
    al, bl = (a - ah.astype(F32)).astype(BF16), (b - bh.astype(F32)).astype(BF16)
    return _bdot(ah, bh, ca, cb) + (_bdot(ah, bl, ca, cb) + _bdot(al, bh, ca, cb))


TRI_SPLIT_LEVELS = 2


def _tri_inv(low, eye):
    n = -low
    acc = eye + n
    p = n
    for level in range(5):
        dot = _dot3 if level < TRI_SPLIT_LEVELS else (lambda a, b: _bdot(a, b, 1, 0))
        p = dot(p, p)
        acc = acc + dot(acc, p)
    return acc


@jax.custom_vjp
def _solve2(low, rv, rk, tinv):
    x = _dot3(tinv, jnp.concatenate([rv, rk], axis=1))
    return x[:, :DA], x[:, DA:]


def _solve2_fwd(low, rv, rk, tinv):
    x = _dot3(tinv, jnp.concatenate([rv, rk], axis=1))
    return (x[:, :DA], x[:, DA:]), (x, tinv)


def _solve2_bwd(res, g):
    x, tinv = res
    drhs = _dot3(tinv, jnp.concatenate(g, axis=1), 0, 0)
    return -_dot3(drhs, x, 1, 1), drhs[:, :DA], drhs[:, DA:], jnp.zeros_like(tinv)


_solve2.defvjp(_solve2_fwd, _solve2_bwd)


def _chunk_last(x, cs):
    xs = (x * cs.last_col).reshape(SC // CH, CH, x.shape[1])
    return jnp.broadcast_to(jnp.sum(xs, axis=1, keepdims=True), xs.shape).reshape(x.shape)


def _gdn_decay(g, cs):
    gw = jnp.concatenate([g] * (SC // DA), axis=1)
    grow = jnp.sum(cs.eye * gw, axis=0, keepdims=True)
    return jnp.where(cs.incl, jnp.exp(jnp.where(cs.incl, gw - grow, 0.0)), 0.0)


def _gdn_intra(q, k, v, g, bx, tinv, cs):
    decay = _gdn_decay(g, cs)
    kb = k * bx
    low = jnp.where(cs.strict, mm_nt(kb, k) * decay, 0.0)
    eg = jnp.exp(g)
    made = tinv is None
    if made:
        tinv = _tri_inv(low, cs.eye)
    u, w = _solve2(low, v * bx, kb * eg, tinv)
    attn = mmc(mm_nt(q, k) * decay, cs.fold)
    qd = q * eg
    glast = _chunk_last(g, cs)
    kd = k * jnp.exp(glast - g)
    outs = (u, w, attn, qd, kd, jnp.exp(glast))
    return outs + (tinv,) if made else outs


def _gdn_scan(u, w, attn, qd, kd, el, s):
    vn = u - mm(w, s)
    o = mm(qd, s) + mm(attn, vn)
    sn = s * el + mm_tn(kd, vn)
    return o, sn


def _gla_intra(q, k, v, gk, cs):
    gc = cmm(cs.incl_f, gk)
    qg = q * (DKB ** -0.5) * jnp.exp(gc)
    kg = k * jnp.exp(-gc)
    attn = jnp.where(cs.incl, mm_nt(qg, kg), 0.0)
    intra = mm(attn, v)
    glast = _chunk_last(gc, cs)
    kd = k * jnp.exp(glast - gc)
    return qg, kd, intra, jnp.exp(glast)


def _gla_scan(qg, kd, v, el, st):
    o = mm_nt(qg, st)
    stn = st * el + mm_tn(v, kd)
    return o, stn


def _shift_rows(x, s):
    if s == 0:
        return x
    t = x.shape[0]
    rolled = pltpu.roll(x, (-s) % t, 0)
    rows = lax.broadcasted_iota(jnp.int32, x.shape, 0)
    return jnp.where((rows + s >= 0) & (rows + s < t), rolled, 0.0)


@jax.custom_vjp
def _conv5(x, w):
    acc = w[0:1] * _shift_rows(x, -2)
    for j in range(1, 5):
        acc = acc + w[j:j + 1] * _shift_rows(x, j - 2)
    return acc


def _conv5_fwd(x, w):
    return _conv5(x, w), (x, w)


def _conv5_bwd(res, g):
    x, w = res
    dx = w[0:1] * _shift_rows(g, 2)
    for j in range(1, 5):
        dx = dx + w[j:j + 1] * _shift_rows(g, 2 - j)
    rows = lax.broadcasted_iota(jnp.int32, w.shape, 0)
    dw = jnp.zeros_like(w)
    for j in range(5):
        dwj = jnp.sum(g * _shift_rows(x, j - 2), axis=0, keepdims=True)
        dw = dw + jnp.where(rows == j, dwj, 0.0)
    return dx, dw


_conv5.defvjp(_conv5_fwd, _conv5_bwd)


def _qkv_act(kind):
    def f(x, w):
        c = _silu(_conv5(x, w))
        if kind == 2:
            return c
        c = c * lax.rsqrt(jnp.sum(c * c, axis=-1, keepdims=True) + EPS)
        return c * (DA ** -0.5) if kind == 0 else c
    return f


def _inproj(x, lnw, wperm, tn=512):
    t = x.shape[0]
    tm = min(t, 2048)

    def body(x_ref, lnw_ref, w_ref, p_ref, h_ref, hbuf):
        @pl.when(pl.program_id(1) == 0)
        def _():
            hb = _rms(x_ref[...], lnw_ref[...]).astype(BF16)
            hbuf[...] = hb
            h_ref[...] = hb
        p_ref[...] = _bdot(hbuf[...], w_ref[...], 1, 1)

    return pl.pallas_call(
        body, name="inproj", grid=(t // tm, NPERM // tn),
        in_specs=[pl.BlockSpec((tm, D), lambda i, j: (i, 0)),
                  pl.BlockSpec((1, D), lambda i, j: (0, 0)),
                  pl.BlockSpec((tn, D), lambda i, j: (j, 0))],
        out_specs=[pl.BlockSpec((tm, tn), lambda i, j: (i, j)),
                   pl.BlockSpec((tm, D), lambda i, j: (i, 0))],
        out_shape=[jax.ShapeDtypeStruct((t, NPERM), F32),
                   jax.ShapeDtypeStruct((t, D), BF16)],
        scratch_shapes=[pltpu.VMEM((tm, D), BF16)],
        compiler_params=_cparams(("parallel", "arbitrary")),
    )(x, lnw, wperm)


DP_TILE = 512
DP_PIECES = ((0, 19),)


def _piece_specs(tm, j_first):
    specs = []
    for j0, n in DP_PIECES:
        def imap(a, b, j0=j0, n=n):
            j, i = (a, b) if j_first else (b, a)
            inside = (j >= j0) & (j < j0 + n)
            return jnp.where(inside, i, 0), jnp.clip(j - j0, 0, n - 1)
        specs.append(pl.BlockSpec((tm, DP_TILE), imap))
    return specs


def _for_piece(j, refs, fn):
    for (j0, n), ref in zip(DP_PIECES, refs):
        @pl.when((j >= j0) & (j < j0 + n))
        def _(ref=ref):
            fn(ref[...])


def _inproj_dw(h, pieces):
    t = h.shape[0]
    tm = min(t, 2048)
    npc = len(pieces)

    def body(h_ref, *refs):
        dw_ref = refs[npc]

        @pl.when(pl.program_id(1) == 0)
        def _():
            dw_ref[...] = jnp.zeros_like(dw_ref)

        def add(dp):
            dw_ref[...] += _bdot(dp, h_ref[...], 0, 0)
        _for_piece(pl.program_id(0), refs[:npc], add)

    return pl.pallas_call(
        body, name="inproj_dw", grid=(NPERM // DP_TILE, t // tm),
        in_specs=[pl.BlockSpec((tm, D), lambda j, i: (i, 0))] + _piece_specs(tm, True),
        out_specs=pl.BlockSpec((DP_TILE, D), lambda j, i: (j, 0)),
        out_shape=jax.ShapeDtypeStruct((NPERM, D), F32),
        compiler_params=_cparams(("parallel", "arbitrary")),
    )(h, *pieces)


def _inproj_dx(pieces, wperm, x, lnw, dyres):
    t = x.shape[0]
    tm = min(t, 1024)
    tn = DP_TILE
    nj = NPERM // tn
    npc = len(pieces)

    def body(*refs):
        w_ref, x_ref, lnw_ref, dy_ref, dx_ref, dlnw_ref, acc = refs[npc:]
        j = pl.program_id(1)

        @pl.when(j == 0)
        def _():
            acc[...] = jnp.zeros_like(acc)

        def add(dp):
            acc[...] += _bdot(dp, w_ref[...], 1, 0)
        _for_piece(j, refs[:npc], add)

        @pl.when(j == nj - 1)
        def _():
            _, vjp = jax.vjp(_rms, x_ref[...], lnw_ref[...])
            dx, dlnw = vjp(acc[...])
            dx_ref[...] = dx + dy_ref[...]

            @pl.when(pl.program_id(0) == 0)
            def _():
                dlnw_ref[...] = jnp.zeros_like(dlnw_ref)
            dlnw_ref[...] += jnp.broadcast_to(dlnw, dlnw_ref.shape)

    return pl.pallas_call(
        body, name="inproj_dx", grid=(t // tm, nj),
        in_specs=_piece_specs(tm, False) + [
                  pl.BlockSpec((tn, D), lambda i, j: (j, 0)),
                  pl.BlockSpec((tm, D), lambda i, j: (i, 0)),
                  pl.BlockSpec((1, D), lambda i, j: (0, 0)),
                  pl.BlockSpec((tm, D), lambda i, j: (i, 0))],
        out_specs=[pl.BlockSpec((tm, D), lambda i, j: (i, 0)),
                   pl.BlockSpec((8, D), lambda i, j: (0, 0))],
        out_shape=[jax.ShapeDtypeStruct((t, D), F32), jax.ShapeDtypeStruct((8, D), F32)],
        scratch_shapes=[pltpu.VMEM((tm, D), F32)],
        compiler_params=_cparams(("arbitrary", "arbitrary")),
    )(*pieces, wperm, x, lnw, dyres)


def _qkv_fwd(p, convw, kind):
    t = p.shape[0]
    f = _qkv_act(kind)

    def body(p_ref, w_ref, o_ref):
        o_ref[...] = f(p_ref[...], w_ref[...])

    return pl.pallas_call(
        body, name=f"qkv_fwd{kind}", grid=(NA,),
        in_specs=[pl.BlockSpec((t, DA), lambda h: (0, kind * NA + h)),
                  pl.BlockSpec((8, DA), lambda h: (0, kind * NA + h))],
        out_specs=pl.BlockSpec((t, DA), lambda h: (0, h)),
        out_shape=jax.ShapeDtypeStruct((t, D), F32),
        compiler_params=_cparams(("parallel",)),
    )(p, convw)


def _qkv_bwd(p, convw, dout, kind):
    t = p.shape[0]
    f = _qkv_act(kind)

    def body(p_ref, w_ref, g_ref, dx_ref, dw_ref):
        _, vjp = jax.vjp(f, p_ref[...], w_ref[...])
        dx, dw = vjp(g_ref[...])
        dx_ref[...] = dx.astype(BF16)
        dw_ref[...] = dw

    return pl.pallas_call(
        body, name=f"qkv_bwd{kind}", grid=(NA,),
        in_specs=[pl.BlockSpec((t, DA), lambda h: (0, kind * NA + h)),
                  pl.BlockSpec((8, DA), lambda h: (0, kind * NA + h)),
                  pl.BlockSpec((t, DA), lambda h: (0, h))],
        out_specs=[pl.BlockSpec((t, DA), lambda h: (0, h)),
                   pl.BlockSpec((8, DA), lambda h: (0, h))],
        out_shape=[jax.ShapeDtypeStruct((t, D), BF16), jax.ShapeDtypeStruct((8, D), F32)],
        compiler_params=_cparams(("parallel",)),
    )(p, convw, dout)


def _gates_f(ps, alog_row, dt_row, w2f, b2f, w2b, b2b):
    lane = lax.broadcasted_iota(jnp.int32, ps.shape, 1)
    lg = -jnp.exp(alog_row) * _softplus(ps + dt_row)
    gsm = jnp.where(lane < 16, lg, jnp.where(lane < 32, _sigmoid(ps), 0.0))
    gkf = -_softplus(-(mm(ps, w2f) + b2f)) * (1.0 / 16.0)
    gkb = -_softplus(-(mm(ps, w2b) + b2b)) * (1.0 / 16.0)
    return gsm, gkf, gkb


def _gates_fwd(ps, alog_row, dt_row, w2f, b2f, w2b, b2b, tm=512):
    t = ps.shape[0]

    def body(ps_ref, a_ref, d_ref, wf_ref, bf_ref, wb_ref, bb_ref, gsm_ref, gk_ref):
        gsm, gkf, gkb = _gates_f(ps_ref[...], a_ref[...], d_ref[...], wf_ref[...], bf_ref[...],
                                 wb_ref[...], bb_ref[...])
        gsm_ref[...] = gsm
        gk_ref[0] = gkf
        gk_ref[1] = gkb

    row = lambda n: pl.BlockSpec((1, n), lambda i: (0, 0))
    mat = pl.BlockSpec((128, 512), lambda i: (0, 0))
    return pl.pallas_call(
        body, name="gates_fwd", grid=(t // tm,),
        in_specs=[pl.BlockSpec((tm, 128), lambda i: (i, PS_BLOCK)), row(128), row(128), mat, row(512), mat, row(512)],
        out_specs=[pl.BlockSpec((tm, 128), lambda i: (i, 0)),
                   pl.BlockSpec((2, tm, 512), lambda i: (0, i, 0))],
        out_shape=[jax.ShapeDtypeStruct((t, 128), F32), jax.ShapeDtypeStruct((2, t, 512), F32)],
        compiler_params=_cparams(("parallel",)),
    )(ps, alog_row, dt_row, w2f, b2f, w2b, b2b)


def _gates_bwd(ps, alog_row, dt_row, w2f, b2f, w2b, b2b, dgsm, dgk, tm=512):
    t = ps.shape[0]

    def body(ps_ref, a_ref, d_ref, wf_ref, bf_ref, wb_ref, bb_ref, dgsm_ref, dgk_ref,
             dps_ref, da_ref, dd_ref, dwf_ref, dbf_ref, dwb_ref, dbb_ref):
        _, vjp = jax.vjp(_gates_f, ps_ref[...], a_ref[...], d_ref[...], wf_ref[...], bf_ref[...],
                         wb_ref[...], bb_ref[...])
        dps, da, dd, dwf, dbf, dwb, dbb = vjp((dgsm_ref[...], dgk_ref[0], dgk_ref[1]))
        dps_ref[:, 0:128] = dps.astype(BF16)
        dps_ref[:, 128:DP_TILE] = jnp.zeros((tm, DP_TILE - 128), BF16)
        accs = ((da_ref, da), (dd_ref, dd), (dwf_ref, dwf), (dbf_ref, dbf), (dwb_ref, dwb), (dbb_ref, dbb))

        @pl.when(pl.program_id(0) == 0)
        def _():
            for ref, _ in accs:
                ref[...] = jnp.zeros_like(ref)
        for ref, val in accs:
            ref[...] += jnp.broadcast_to(val, ref.shape)

    row = lambda n: pl.BlockSpec((1, n), lambda i: (0, 0))
    row8 = lambda n: pl.BlockSpec((8, n), lambda i: (0, 0))
    mat = pl.BlockSpec((128, 512), lambda i: (0, 0))
    return pl.pallas_call(
        body, name="gates_bwd", grid=(t // tm,),
        in_specs=[pl.BlockSpec((tm, 128), lambda i: (i, PS_BLOCK)), row(128), row(128), mat, row(512), mat, row(512),
                  pl.BlockSpec((tm, 128), lambda i: (i, 0)),
                  pl.BlockSpec((2, tm, 512), lambda i: (0, i, 0))],
        out_specs=[pl.BlockSpec((tm, DP_TILE), lambda i: (i, 0)), row8(128), row8(128), mat, row8(512), mat,
                   row8(512)],
        out_shape=[jax.ShapeDtypeStruct((t, DP_TILE), BF16),
                   jax.ShapeDtypeStruct((8, 128), F32), jax.ShapeDtypeStruct((8, 128), F32),
                   jax.ShapeDtypeStruct((128, 512), F32), jax.ShapeDtypeStruct((8, 512), F32),
                   jax.ShapeDtypeStruct((128, 512), F32), jax.ShapeDtypeStruct((8, 512), F32)],
        compiler_params=_cparams(("arbitrary",)),
    )(ps, alog_row, dt_row, w2f, b2f, w2b, b2b, dgsm, dgk)


def _rows(i):
    return pl.ds(pl.multiple_of(i * CH, CH), CH)


def _srows(i):
    return pl.ds(pl.multiple_of(i * SC, SC), SC)


def _first_row(x):
    row = lax.broadcasted_iota(jnp.int32, (8, x.shape[1]), 0)
    return jnp.where(row == 0, jnp.broadcast_to(x, (8, x.shape[1])), 0.0)


def _chunk_rows(e_ref, i):
    pad = jnp.zeros((CH - 8, 128), F32)
    return jnp.concatenate([x for c in range(SC // CH) for x in (e_ref[(SC // CH) * i + c], pad)], axis=0)


def _gcum_f(gsm, tm):
    i = lax.broadcasted_iota(jnp.int32, (tm, tm), 0)
    j = lax.broadcasted_iota(jnp.int32, (tm, tm), 1)
    same = (i >> 6) == (j >> 6)
    lower = (same & (i >= j)).astype(F32)
    upper = (same & (i <= j)).astype(F32)
    r = lax.broadcasted_iota(jnp.int32, (128, D), 0)
    head = lax.broadcasted_iota(jnp.int32, (128, D), 1) >> 7
    pick = lambda off: (r == head + off).astype(F32)
    lane = lax.broadcasted_iota(jnp.int32, gsm.shape, 1)
    run = jnp.where(lane < 8, cmm(lower, gsm), cmm(upper, gsm))
    return mmc(run, pick(0)), mmc(run, pick(8)), mmc(gsm, pick(16)), mmc(gsm, pick(24))


def _gcum_fwd(gsm, tm=256):
    t = gsm.shape[0]

    def body(s_ref, g_ref, b_ref):
        gf, gb, bf, bb = _gcum_f(s_ref[...], tm)
        g_ref[0] = gf
        g_ref[1] = gb
        b_ref[0] = bf
        b_ref[1] = bb

    two = pl.BlockSpec((2, tm, D), lambda i: (0, i, 0))
    return pl.pallas_call(
        body, name="gcum_fwd", grid=(t // tm,),
        in_specs=[pl.BlockSpec((tm, 128), lambda i: (i, 0))], out_specs=[two, two],
        out_shape=[jax.ShapeDtypeStruct((2, t, D), F32)] * 2,
        compiler_params=_cparams(("parallel",)),
    )(gsm)


def _gcum_bwd(gsm, dg2, db2, tm=256):
    t = gsm.shape[0]

    def body(s_ref, dg_ref, db_ref, ds_ref):
        _, vjp = jax.vjp(lambda s: _gcum_f(s, tm), s_ref[...])
        ds_ref[...] = vjp((dg_ref[0], dg_ref[1], db_ref[0], db_ref[1]))[0]

    two = pl.BlockSpec((2, tm, D), lambda i: (0, i, 0))
    tile = pl.BlockSpec((tm, 128), lambda i: (i, 0))
    return pl.pallas_call(
        body, name="gcum_bwd", grid=(t // tm,),
        in_specs=[tile, two, two], out_specs=tile,
        out_shape=jax.ShapeDtypeStruct((t, 128), F32),
        compiler_params=_cparams(("parallel",)),
    )(gsm, dg2, db2)


def _gdn_intra_fwd(qn, kn, vc, g2, b2):
    t = qn.shape[0]
    n = t // CH

    def body(q_ref, k_ref, v_ref, g_ref, b_ref, u_ref, w_ref, a_ref, qd_ref, kd_ref, e_ref, t_ref):
        cs = _Consts(pl.program_id(0) == 1)

        def step(i, carry):
            r = _srows(i)
            q, k, v, g, bx = q_ref[r, :], k_ref[r, :], v_ref[r, :], g_ref[r, :], b_ref[r, :]
            u, w, a, qd, kd, el, tinv = _gdn_intra(q, k, v, g, bx, None, cs)
            u_ref[r, :] = u
            w_ref[r, :] = w
            a_ref[r, :] = a
            qd_ref[r, :] = qd
            kd_ref[r, :] = kd
            t_ref[r, :] = tinv
            for c in range(SC // CH):
                e_ref[(SC // CH) * i + c] = el[c * CH:c * CH + 8]
            return carry

        lax.fori_loop(0, t // SC, step, 0)

    head = pl.BlockSpec((t, DA), lambda d, h: (0, h))
    dh = pl.BlockSpec((None, t, DA), lambda d, h: (d, 0, h))
    sq = lambda w: pl.BlockSpec((None, None, t, w), lambda d, h: (d, h, 0, 0))
    big = jax.ShapeDtypeStruct((2, t, D), F32)
    return pl.pallas_call(
        body, name="gdn_intra_fwd", grid=(2, NA),
        in_specs=[head, head, head, dh, dh],
        out_specs=[dh, dh, sq(CH), dh, dh, pl.BlockSpec((None, None, n, 8, 128), lambda d, h: (d, h, 0, 0, 0)),
                   sq(SC)],
        out_shape=[big, big, jax.ShapeDtypeStruct((2, NA, t, CH), F32), big, big,
                   jax.ShapeDtypeStruct((2, NA, n, 8, 128), F32), jax.ShapeDtypeStruct((2, NA, t, SC), F32)],
        compiler_params=_cparams(("parallel", "parallel")),
    )(qn, kn, vc, g2, b2)


SCAN_TB = 256
SCAN_HB = 8


def _scan_specs(t, width, nheads, hb, along):
    nt = t // SCAN_TB
    nb = SCAN_TB // CH

    def tmap(d, tt):
        fwd = tt + d * (nt - 1 - 2 * tt)
        return fwd if along > 0 else nt - 1 - fwd

    tok = pl.BlockSpec((None, SCAN_TB, hb * width), lambda d, h, tt: (d, tmap(d, tt), h))
    per = lambda *tail: pl.BlockSpec((None, hb, nb) + tail, lambda d, h, tt: (d, h, tmap(d, tt)) + (0,) * len(tail))
    sq = pl.BlockSpec((None, hb, SCAN_TB, CH), lambda d, h, tt: (d, h, tmap(d, tt), 0))
    shared = lambda w: pl.BlockSpec((SCAN_TB, hb * w), lambda d, h, tt: (tmap(d, tt), h))
    return tok, per, sq, shared, (2, nheads // hb, nt), nb


def _gdn_scan_fwd(u, w, a, qd, kd, e):
    t = u.shape[1]
    tok, per, sq, _, grid, nb = _scan_specs(t, DA, NA, SCAN_HB, +1)

    def body(u_ref, w_ref, a_ref, qd_ref, kd_ref, e_ref, o_ref, s_ref, state):
        rev = pl.program_id(0) == 1

        @pl.when(pl.program_id(2) == 0)
        def _():
            state[...] = jnp.zeros_like(state)

        def step(i, ss):
            ci = jnp.where(rev, nb - 1 - i, i)
            r = _rows(ci)
            out = []
            for hh, s in enumerate(ss):
                c = slice(hh * DA, (hh + 1) * DA)
                s_ref[hh, ci] = s
                o, sn = _gdn_scan(u_ref[r, c], w_ref[r, c], a_ref[hh, r, :], qd_ref[r, c], kd_ref[r, c],
                                  e_ref[hh, ci][0:1], s)
                o_ref[r, c] = o
                out.append(sn)
            return tuple(out)

        ss = lax.fori_loop(0, nb, step, tuple(state[hh] for hh in range(SCAN_HB)))
        for hh, s in enumerate(ss):
            state[hh] = s

    return pl.pallas_call(
        body, name="gdn_scan_fwd", grid=grid,
        in_specs=[tok, tok, sq, tok, tok, per(8, 128)],
        out_specs=[tok, per(DA, DA)],
        out_shape=[jax.ShapeDtypeStruct((2, t, D), F32), jax.ShapeDtypeStruct((2, NA, t // CH, DA, DA), F32)],
        scratch_shapes=[pltpu.VMEM((SCAN_HB, DA, DA), F32)],
        compiler_params=_cparams(("parallel", "parallel", "arbitrary")),
    )(u, w, a, qd, kd, e)


def _gdn_scan_bwd(u, w, a, qd, kd, e, ssave, do):
    t = u.shape[1]
    tok, per, sq, shared, grid, nb = _scan_specs(t, DA, NA, SCAN_HB, -1)

    def body(u_ref, w_ref, a_ref, qd_ref, kd_ref, e_ref, s_ref, do_ref,
             du_ref, dw_ref, da_ref, dqd_ref, dkd_ref, de_ref, state):
        rev = pl.program_id(0) == 1

        @pl.when(pl.program_id(2) == 0)
        def _():
            state[...] = jnp.zeros_like(state)

        def step(i, dss):
            ci = jnp.where(rev, i, nb - 1 - i)
            r = _rows(ci)
            out = []
            for hh, ds in enumerate(dss):
                c = slice(hh * DA, (hh + 1) * DA)
                _, vjp = jax.vjp(_gdn_scan, u_ref[r, c], w_ref[r, c], a_ref[hh, r, :], qd_ref[r, c], kd_ref[r, c],
                                 e_ref[hh, ci][0:1], s_ref[hh, ci])
                du, dw, da, dqd, dkd, de, dsn = vjp((do_ref[r, c], ds))
                du_ref[r, c] = du
                dw_ref[r, c] = dw
                da_ref[hh, r, :] = da
                dqd_ref[r, c] = dqd
                dkd_ref[r, c] = dkd
                de_ref[hh, ci] = _first_row(de)
                out.append(dsn)
            return tuple(out)

        dss = lax.fori_loop(0, nb, step, tuple(state[hh] for hh in range(SCAN_HB)))
        for hh, ds in enumerate(dss):
            state[hh] = ds

    big = jax.ShapeDtypeStruct((2, t, D), F32)
    return pl.pallas_call(
        body, name="gdn_scan_bwd", grid=grid,
        in_specs=[tok, tok, sq, tok, tok, per(8, 128), per(DA, DA), shared(DA)],
        out_specs=[tok, tok, sq, tok, tok, per(8, 128)],
        out_shape=[big, big, jax.ShapeDtypeStruct((2, NA, t, CH), F32), big, big,
                   jax.ShapeDtypeStruct((2, NA, t // CH, 8, 128), F32)],
        scratch_shapes=[pltpu.VMEM((SCAN_HB, DA, DA), F32)],
        compiler_params=_cparams(("parallel", "parallel", "arbitrary")),
    )(u, w, a, qd, kd, e, ssave, do)


def _gdn_intra_bwd(qn, kn, vc, g2, b2, tinv, du, dw, da, dqd, dkd, de):
    t = qn.shape[0]
    n = t // CH

    def body(q_ref, k_ref, v_ref, g_ref, b_ref, t_ref, du_ref, dw_ref, da_ref, dqd_ref, dkd_ref, de_ref,
             dq_ref, dk_ref, dv_ref, dg_ref, db_ref):
        d = pl.program_id(1)
        cs = _Consts(d == 1)

        @pl.when(d == 0)
        def _():
            dq_ref[...] = jnp.zeros_like(dq_ref)
            dk_ref[...] = jnp.zeros_like(dk_ref)
            dv_ref[...] = jnp.zeros_like(dv_ref)

        def step(i, carry):
            r = _srows(i)
            tinv_c = t_ref[r, :]
            f = lambda q, k, v, g, bx: _gdn_intra(q, k, v, g, bx, tinv_c, cs)
            _, vjp = jax.vjp(f, q_ref[r, :], k_ref[r, :], v_ref[r, :], g_ref[r, :], b_ref[r, :])
            dq, dk, dv, dg, dbx = vjp((du_ref[r, :], dw_ref[r, :], da_ref[r, :], dqd_ref[r, :],
                                       dkd_ref[r, :], _chunk_rows(de_ref, i)))
            dq_ref[r, :] += dq
            dk_ref[r, :] += dk
            dv_ref[r, :] += dv
            dg_ref[r, :] = dg
            db_ref[r, :] = dbx
            return carry

        lax.fori_loop(0, t // SC, step, 0)

    head = pl.BlockSpec((t, DA), lambda h, d: (0, h))
    dh = pl.BlockSpec((None, t, DA), lambda h, d: (d, 0, h))
    sq = pl.BlockSpec((None, None, t, CH), lambda h, d: (d, h, 0, 0))
    tq = pl.BlockSpec((None, None, t, SC), lambda h, d: (d, h, 0, 0))
    full = jax.ShapeDtypeStruct((t, D), F32)
    big = jax.ShapeDtypeStruct((2, t, D), F32)
    return pl.pallas_call(
        body, name="gdn_intra_bwd", grid=(NA, 2),
        in_specs=[head, head, head, dh, dh, tq, dh, dh, sq, dh, dh,
                  pl.BlockSpec((None, None, n, 8, 128), lambda h, d: (d, h, 0, 0, 0))],
        out_specs=[head, head, head, dh, dh],
        out_shape=[full, full, full, big, big],
        compiler_params=_cparams(("arbitrary", "arbitrary")),
    )(qn, kn, vc, g2, b2, tinv, du, dw, da, dqd, dkd, de)


def _gla_specs(t, order):
    ix = (lambda d, h: (d, h)) if order == "dh" else (lambda h, d: (d, h))

    def mk(fn):
        return lambda a, b: fn(*ix(a, b))
    q = pl.BlockSpec((t, DKB), mk(lambda d, h: (0, 32 + h)))
    k = pl.BlockSpec((t, DKB), mk(lambda d, h: (0, 36 + h)))
    v = pl.BlockSpec((t, DVB), mk(lambda d, h: (0, 20 + h)))
    dk = pl.BlockSpec((None, t, DKB), mk(lambda d, h: (d, 0, h)))
    dv = pl.BlockSpec((None, t, DVB), mk(lambda d, h: (d, 0, h)))
    e = pl.BlockSpec((None, None, t // CH, 8, 128), mk(lambda d, h: (d, h, 0, 0, 0)))
    s = pl.BlockSpec((None, None, t // CH, DVB, DKB), mk(lambda d, h: (d, h, 0, 0, 0)))
    return q, k, v, dk, dv, e, s


def _gla_intra_fwd(p, gk):
    t = p.shape[0]
    n = t // CH

    def body(q_ref, k_ref, v_ref, g_ref, qg_ref, kd_ref, in_ref, e_ref):
        cs = _Consts(pl.program_id(0) == 1)

        def step(i, carry):
            r = _srows(i)
            qg, kd, intra, el = _gla_intra(q_ref[r, :], k_ref[r, :], v_ref[r, :], g_ref[r, :], cs)
            qg_ref[r, :] = qg
            kd_ref[r, :] = kd
            in_ref[r, :] = intra
            for c in range(SC // CH):
                e_ref[(SC // CH) * i + c] = el[c * CH:c * CH + 8]
            return carry

        lax.fori_loop(0, t // SC, step, 0)

    q, k, v, dk, dv, e, _ = _gla_specs(t, "dh")
    return pl.pallas_call(
        body, name="gla_intra_fwd", grid=(2, NB),
        in_specs=[q, k, v, dk], out_specs=[dk, dk, dv, e],
        out_shape=[jax.ShapeDtypeStruct((2, t, NB * DKB), F32), jax.ShapeDtypeStruct((2, t, NB * DKB), F32),
                   jax.ShapeDtypeStruct((2, t, D), F32), jax.ShapeDtypeStruct((2, NB, n, 8, 128), F32)],
        compiler_params=_cparams(("parallel", "parallel")),
    )(p, p, p, gk)


GLA_HB = 4


def _gla_v_spec(t, along):
    nt = t // SCAN_TB

    def tmap(d, tt):
        fwd = tt + d * (nt - 1 - 2 * tt)
        return fwd if along > 0 else nt - 1 - fwd

    return pl.BlockSpec((SCAN_TB, GLA_HB * DVB), lambda d, h, tt: (tmap(d, tt), 5120 // (GLA_HB * DVB) + h))


def _gla_scan_fwd(p, qg, kd, intra, e):
    t = p.shape[0]
    tokk, per, _, _, grid, nb = _scan_specs(t, DKB, NB, GLA_HB, +1)
    tokv = _scan_specs(t, DVB, NB, GLA_HB, +1)[0]

    def body(v_ref, qg_ref, kd_ref, in_ref, e_ref, o_ref, s_ref, state):
        rev = pl.program_id(0) == 1

        @pl.when(pl.program_id(2) == 0)
        def _():
            state[...] = jnp.zeros_like(state)

        def step(i, sts):
            ci = jnp.where(rev, nb - 1 - i, i)
            r = _rows(ci)
            out = []
            for hh, st in enumerate(sts):
                ck = slice(hh * DKB, (hh + 1) * DKB)
                cv = slice(hh * DVB, (hh + 1) * DVB)
                s_ref[hh, ci] = st
                o, stn = _gla_scan(qg_ref[r, ck], kd_ref[r, ck], v_ref[r, cv], e_ref[hh, ci][0:1], st)
                o_ref[r, cv] = o + in_ref[r, cv]
                out.append(stn)
            return tuple(out)

        sts = lax.fori_loop(0, nb, step, tuple(state[hh] for hh in range(GLA_HB)))
        for hh, st in enumerate(sts):
            state[hh] = st

    return pl.pallas_call(
        body, name="gla_scan_fwd", grid=grid,
        in_specs=[_gla_v_spec(t, +1), tokk, tokk, tokv, per(8, 128)], out_specs=[tokv, per(DVB, DKB)],
        out_shape=[jax.ShapeDtypeStruct((2, t, D), F32), jax.ShapeDtypeStruct((2, NB, t // CH, DVB, DKB), F32)],
        scratch_shapes=[pltpu.VMEM((GLA_HB, DVB, DKB), F32)],
        compiler_params=_cparams(("parallel", "parallel", "arbitrary")),
    )(p, qg, kd, intra, e)


def _gla_scan_bwd(p, qg, kd, e, ssave, do):
    t = p.shape[0]
    tokk, per, _, shared, grid, nb = _scan_specs(t, DKB, NB, GLA_HB, -1)
    tokv = _scan_specs(t, DVB, NB, GLA_HB, -1)[0]

    def body(v_ref, qg_ref, kd_ref, e_ref, s_ref, do_ref, dqg_ref, dkd_ref, dv_ref, de_ref, state):
        rev = pl.program_id(0) == 1

        @pl.when(pl.program_id(2) == 0)
        def _():
            state[...] = jnp.zeros_like(state)

        def step(i, dsts):
            ci = jnp.where(rev, i, nb - 1 - i)
            r = _rows(ci)
            out = []
            for hh, dst in enumerate(dsts):
                ck = slice(hh * DKB, (hh + 1) * DKB)
                cv = slice(hh * DVB, (hh + 1) * DVB)
                _, vjp = jax.vjp(_gla_scan, qg_ref[r, ck], kd_ref[r, ck], v_ref[r, cv], e_ref[hh, ci][0:1],
                                 s_ref[hh, ci])
                dqg, dkd, dv, de, dstn = vjp((do_ref[r, cv], dst))
                dqg_ref[r, ck] = dqg
                dkd_ref[r, ck] = dkd
                dv_ref[r, cv] = dv
                de_ref[hh, ci] = _first_row(de)
                out.append(dstn)
            return tuple(out)

        dsts = lax.fori_loop(0, nb, step, tuple(state[hh] for hh in range(GLA_HB)))
        for hh, dst in enumerate(dsts):
            state[hh] = dst

    return pl.pallas_call(
        body, name="gla_scan_bwd", grid=grid,
        in_specs=[_gla_v_spec(t, -1), tokk, tokk, per(8, 128), per(DVB, DKB), shared(DVB)],
        out_specs=[tokk, tokk, tokv, per(8, 128)],
        out_shape=[jax.ShapeDtypeStruct((2, t, NB * DKB), F32), jax.ShapeDtypeStruct((2, t, NB * DKB), F32),
                   jax.ShapeDtypeStruct((2, t, D), F32), jax.ShapeDtypeStruct((2, NB, t // CH, 8, 128), F32)],
        scratch_shapes=[pltpu.VMEM((GLA_HB, DVB, DKB), F32)],
        compiler_params=_cparams(("parallel", "parallel", "arbitrary")),
    )(p, qg, kd, e, ssave, do)


def _gla_intra_bwd(p, gk, dqg, dkd, dvs, de, do):
    t = p.shape[0]
    n = t // CH

    def body(q_ref, k_ref, v_ref, g_ref, dqg_ref, dkd_ref, dvs_ref, de_ref, do_ref,
             dq_ref, dk_ref, dv_ref, dg_ref):
        d = pl.program_id(1)
        cs = _Consts(d == 1)

        @pl.when(d == 0)
        def _():
            dq_ref[...] = jnp.zeros_like(dq_ref)
            dk_ref[...] = jnp.zeros_like(dk_ref)
            dv_ref[...] = jnp.zeros_like(dv_ref)

        def step(i, carry):
            r = _srows(i)
            f = lambda q, k, v, g: _gla_intra(q, k, v, g, cs)
            _, vjp = jax.vjp(f, q_ref[r, :], k_ref[r, :], v_ref[r, :], g_ref[r, :])
            dq, dk, dv, dg = vjp((dqg_ref[r, :], dkd_ref[r, :], do_ref[r, :], _chunk_rows(de_ref, i)))
            dq_ref[r, :] += dq
            dk_ref[r, :] += dk
            dv_ref[r, :] += dv + dvs_ref[r, :]
            dg_ref[r, :] = dg
            return carry

        lax.fori_loop(0, t // SC, step, 0)

    q, k, v, dk, dv, e_s, _ = _gla_specs(t, "hd")
    hk = pl.BlockSpec((t, DKB), lambda h, d: (0, h))
    hv = pl.BlockSpec((t, DVB), lambda h, d: (0, h))
    return pl.pallas_call(
        body, name="gla_intra_bwd", grid=(NB, 2),
        in_specs=[q, k, v, dk, dk, dk, dv, e_s, hv],
        out_specs=[hk, hk, hv, dk],
        out_shape=[jax.ShapeDtypeStruct((t, NB * DKB), F32), jax.ShapeDtypeStruct((t, NB * DKB), F32),
                   jax.ShapeDtypeStruct((t, D), F32), jax.ShapeDtypeStruct((2, t, NB * DKB), F32)],
        compiler_params=_cparams(("arbitrary", "arbitrary")),
    )(p, p, p, gk, dqg, dkd, dvs, de, do)


def _seg_gate(o, z, w):
    return _rms(o, w) * _silu(z)


def _seg_merge(ya, yb, ga, gb):
    return _sigmoid(ga) * ya + _sigmoid(gb) * yb


def _seg_loss(out, x, tgt, w):
    err = x + _rms(out, w) - tgt
    return 0.5 * jnp.sum(jnp.mean(err * err, axis=-1, keepdims=True), axis=0, keepdims=True)


def _post(oa2, ob2, p, x, tgt, gdn_w, gla_w, lnpost, w3, tm=128):
    t = x.shape[0]

    def body(oa_ref, ob_ref, z_ref, gb_ref, ga_ref, gB_ref, x_ref, t_ref, aw_ref, bw_ref, lw_ref, w_ref,
             loss_ref, doa_ref, dob_ref, dz_ref, dgb_ref, dga_ref, dgB_ref, dy_ref,
             dw_ref, daw_ref, dbw_ref, dlw_ref):
        first = pl.program_id(0) == 0
        oa = oa_ref[0] + oa_ref[1]
        ob = ob_ref[0] + ob_ref[1]
        z, gb = z_ref[...], gb_ref[...]
        aw, bw = aw_ref[...], bw_ref[...]
        rs = D // NSHARD

        def mat(a, m):
            return sum(jnp.dot(a[:, s * rs:(s + 1) * rs], w_ref[s, m], preferred_element_type=F32)
                       for s in range(NSHARD))

        def mat_t(g, m):
            return jnp.concatenate([_bdot(g, w_ref[s, m], 1, 1) for s in range(NSHARD)], axis=1)

        def add_dw(a, g, m):
            for s in range(NSHARD):
                dw_ref[s, m] += _bdot(a[:, s * rs:(s + 1) * rs], g, 0, 0)

        pa = [jax.vjp(_seg_gate, oa[:, h * DA:(h + 1) * DA], z[:, h * DA:(h + 1) * DA], aw) for h in range(NA)]
        pb = [jax.vjp(_seg_gate, ob[:, h * DVB:(h + 1) * DVB], gb[:, h * DVB:(h + 1) * DVB], bw)
              for h in range(NB)]
        a1 = jnp.concatenate([v for v, _ in pa], axis=1).astype(BF16)
        a2 = jnp.concatenate([v for v, _ in pb], axis=1).astype(BF16)
        ya = mat(a1, 0)
        yb = mat(a2, 1)
        merged, vjp_m = jax.vjp(_seg_merge, ya, yb, ga_ref[...], gB_ref[...])
        mb = merged.astype(BF16)
        out = mat(mb, 2)
        loss, vjp_l = jax.vjp(_seg_loss, out, x_ref[...], t_ref[...], lw_ref[...])
        dout, dyres, _, dlw = vjp_l(jnp.ones((1, 1), F32))
        dy_ref[...] = dyres
        doutb = dout.astype(BF16)
        dmerged = mat_t(doutb, 2)
        dya, dyb, dga, dgB = vjp_m(dmerged)
        dga_ref[...] = dga.astype(BF16)
        dgB_ref[...] = dgB.astype(BF16)
        dyab, dybb = dya.astype(BF16), dyb.astype(BF16)
        da1 = mat_t(dyab, 0)
        da2 = mat_t(dybb, 1)

        daw = jnp.zeros_like(aw)
        for h in range(NA):
            sl = slice(h * DA, (h + 1) * DA)
            do, dz, dw = pa[h][1](da1[:, sl])
            doa_ref[:, sl] = do
            dz_ref[:, sl] = dz.astype(BF16)
            daw = daw + dw
        dbw = jnp.zeros_like(bw)
        for h in range(NB):
            sl = slice(h * DVB, (h + 1) * DVB)
            do, dg, dw = pb[h][1](da2[:, sl])
            dob_ref[:, sl] = do
            dgb_ref[:, sl] = dg.astype(BF16)
            dbw = dbw + dw

        @pl.when(first)
        def _():
            loss_ref[...] = jnp.zeros_like(loss_ref)
            dw_ref[...] = jnp.zeros_like(dw_ref)
            daw_ref[...] = jnp.zeros_like(daw_ref)
            dbw_ref[...] = jnp.zeros_like(dbw_ref)
            dlw_ref[...] = jnp.zeros_like(dlw_ref)

        loss_ref[...] += jnp.broadcast_to(loss, loss_ref.shape)
        add_dw(a1, dyab, 0)
        add_dw(a2, dybb, 1)
        add_dw(mb, doutb, 2)
        daw_ref[...] += jnp.broadcast_to(daw, daw_ref.shape)
        dbw_ref[...] += jnp.broadcast_to(dbw, dbw_ref.shape)
        dlw_ref[...] += jnp.broadcast_to(dlw, dlw_ref.shape)

    two = pl.BlockSpec((2, tm, D), lambda i: (0, i, 0))
    pcol = lambda c: pl.BlockSpec((tm, D), lambda i: (i, c))
    tok = pl.BlockSpec((tm, D), lambda i: (i, 0))
    row = lambda n: pl.BlockSpec((1, n), lambda i: (0, 0))
    row8 = lambda n: pl.BlockSpec((8, n), lambda i: (0, 0))
    once = pl.Buffered(1)
    tokf = jax.ShapeDtypeStruct((t, D), F32)
    tokb = jax.ShapeDtypeStruct((t, D), BF16)
    wspec = pl.BlockSpec((NSHARD, 3, D // NSHARD, D), lambda i: (0, 0, 0, 0), pipeline_mode=once)
    return pl.pallas_call(
        body, name="post", grid=(t // tm,),
        in_specs=[two, two, pcol(3), pcol(6), pcol(7), pcol(8), tok, tok, row(DA), row(DVB), row(D), wspec],
        out_specs=[row8(128), tok, tok, tok, tok, tok, tok, tok, wspec, row8(DA), row8(DVB), row8(D)],
        out_shape=[jax.ShapeDtypeStruct((8, 128), F32), tokf, tokf, tokb, tokb, tokb, tokb, tokf,
                   jax.ShapeDtypeStruct((NSHARD, 3, D // NSHARD, D), F32),
                   jax.ShapeDtypeStruct((8, DA), F32), jax.ShapeDtypeStruct((8, DVB), F32),
                   jax.ShapeDtypeStruct((8, D), F32)],
        compiler_params=_cparams(("arbitrary",), vmem_mb=56),
    )(oa2, ob2, p, p, p, p, x, tgt, gdn_w, gla_w, lnpost, w3)


def _adam_math(w, g, m, v):
    nm = B1 * m + (1.0 - B1) * g
    nv = B2 * v + (1.0 - B2) * (g * g)
    m_hat = nm / (1.0 - B1 ** STEP)
    v_hat = nv / (1.0 - B2 ** STEP)
    return -LR * (m_hat / (jnp.sqrt(v_hat) + ADAM_EPS) + WD * w), nm, nv


SMALL_SLOTS = (("ln_pre_w", 0, 1024, 0), ("a_log_fwd", 1024, 8, 0), ("a_log_bwd", 1024, 8, 8),
               ("dt_bias_fwd", 1152, 8, 0), ("dt_bias_bwd", 1152, 8, 8), ("gdn_norm_w", 1280, 128, 0),
               ("gk_b2_fwd", 1408, 512, 0), ("gk_b2_bwd", 1920, 512, 0), ("gla_norm_w", 2432, 256, 0),
               ("ln_post_w", 2688, 1024, 0))
SMALL_W = 3840


def _adam_small(gsum, ws, ms, vs):
    nw = len(SMALL_SLOTS)

    def body(g_ref, *refs):
        w_refs, m_refs, v_refs, outs = refs[0:nw], refs[nw:2 * nw], refs[2 * nw:3 * nw], refs[3 * nw:]
        for i, (_, off, n, shift) in enumerate(SMALL_SLOTS):
            slot = g_ref[0:1, off:off + max(n, 128)]
            if shift:
                slot = pltpu.roll(slot, 128 - shift, 1)
            g = slot[:, 0:n]
            d, nm, nv = _adam_math(w_refs[i][...], g, m_refs[i][...], v_refs[i][...])
            for k, val in enumerate((g, d, nm, nv)):
                outs[4 * i + k][...] = val

    vm = pl.BlockSpec(memory_space=pltpu.VMEM)
    res = pl.pallas_call(
        body, name="adam_small", in_specs=[vm] * (1 + 3 * nw), out_specs=[vm] * (4 * nw),
        out_shape=[jax.ShapeDtypeStruct((1, n), F32) for _, _, n, _ in SMALL_SLOTS for _ in range(4)],
    )(gsum, *ws, *ms, *vs)
    return {name: res[4 * i:4 * i + 4] for i, (name, _, _, _) in enumerate(SMALL_SLOTS)}


def _adam(w, mine, got, m, v, tr, tile0=0, name=""):
    rows, cols = w.shape
    nh = mine.shape[0] // tr

    def body(c_ref, w_ref, a_ref, b_ref, m_ref, v_ref, g_ref, d_ref, nm_ref, nv_ref):
        half = (tile0 + pl.program_id(0)) // nh
        g = jnp.where(half == c_ref[0], a_ref[...], b_ref[...])
        d, nm, nv = _adam_math(w_ref[...], g, m_ref[...], v_ref[...])
        g_ref[...] = g
        d_ref[...] = d
        nm_ref[...] = nm
        nv_ref[...] = nv

    blk = pl.BlockSpec((tr, cols), lambda i, cc: (i, 0))
    half = pl.BlockSpec((tr, cols), lambda i, cc: ((tile0 + i) % nh, 0))
    shp = jax.ShapeDtypeStruct((rows, cols), F32)
    return pl.pallas_call(
        body, name=f"adam_{name}{rows}x{cols}",
        grid_spec=pltpu.PrefetchScalarGridSpec(
            num_scalar_prefetch=1, grid=(rows // tr,),
            in_specs=[blk, half, half, blk, blk], out_specs=[blk] * 4),
        out_shape=[shp] * 4,
        compiler_params=_cparams(("parallel",)),
    )(lax.axis_index("c").reshape(1), w, mine, got, m, v)


def _sum_cast(own, got):
    ns, _, r, c = own.shape
    tr = r // 4 if r % 64 == 0 else r

    def body(c_ref, a_ref, b_ref, f_ref, h_ref):
        s = a_ref[...] + b_ref[...]
        f_ref[...] = s
        h_ref[...] = s.astype(BF16)

    return pl.pallas_call(
        body, name=f"sum_cast_{r}x{c}",
        grid_spec=pltpu.PrefetchScalarGridSpec(
            num_scalar_prefetch=1, grid=(ns, r // tr),
            in_specs=[pl.BlockSpec((None, None, tr, c), lambda s, i, cc: (s, cc[0], i, 0)),
                      pl.BlockSpec((None, tr, c), lambda s, i, cc: (s, i, 0))],
            out_specs=[pl.BlockSpec((None, tr, c), lambda s, i, cc: (s, i, 0)),
                       pl.BlockSpec((None, tr, c), lambda s, i, cc: (s, i, 0))]),
        out_shape=[jax.ShapeDtypeStruct((ns, r, c), F32), jax.ShapeDtypeStruct((ns, r, c), BF16)],
        compiler_params=_cparams(("parallel", "parallel")),
    )(lax.axis_index("c").reshape(1), own, got)


def _sum4(mine, got):
    _, r, c = mine.shape
    tr = r // 4 if r % 64 == 0 else r

    def body(s_ref, a_ref, g_ref, o_ref):
        acc = a_ref[...] + g_ref[0].astype(F32)
        acc = acc + g_ref[1].astype(F32)
        o_ref[...] = acc + g_ref[2].astype(F32)

    shard = (2 * lax.axis_index("x") + lax.axis_index("y")).reshape(1)
    return pl.pallas_call(
        body, name=f"sum4_{r}x{c}",
        grid_spec=pltpu.PrefetchScalarGridSpec(
            num_scalar_prefetch=1, grid=(r // tr,),
            in_specs=[pl.BlockSpec((None, tr, c), lambda i, ss: (ss[0], i, 0)),
                      pl.BlockSpec((3, tr, c), lambda i, ss: (0, i, 0))],
            out_specs=pl.BlockSpec((tr, c), lambda i, ss: (i, 0))),
        out_shape=jax.ShapeDtypeStruct((r, c), F32),
        compiler_params=_cparams(("parallel",)),
    )(shard, mine, got)


def _place():
    x, y, c = lax.axis_index("x"), lax.axis_index("y"), lax.axis_index("c")
    chips = [(1 - x, y), (x, 1 - y), (1 - x, 1 - y)]
    return x, y, c, chips


def _gather_weights(parts):
    npart = len(parts)

    def body(*refs):
        ins, outs = refs[:npart], refs[npart:2 * npart]
        send_sems, recv_sems = refs[2 * npart:]
        x, y, c, chips = _place()
        sibling = (x, y, 1 - c)
        mine = 2 * x + y

        def remote(k, p, shard, half, to, src=None):
            dst = outs[p].at[shard, half]
            return pltpu.make_async_remote_copy(
                src_ref=dst if src is None else src, dst_ref=dst,
                send_sem=send_sems.at[k], recv_sem=recv_sems.at[k], device_id=to, device_id_type=MESH)

        first = [remote(j * npart + p, p, mine, c, (*chip, c), src=ins[p].at[c])
                 for j, chip in enumerate(chips) for p in range(npart)]
        for cp in first:
            cp.start()
        passed = []
        for j, (cx, cy) in enumerate(chips):
            for p in range(npart):
                remote(j * npart + p, p, 2 * cx + cy, c, (x, y, c)).wait_recv()
                fw = remote((3 + j) * npart + p, p, 2 * cx + cy, c, sibling)
                fw.start()
                passed.append(fw)
        for j, (cx, cy) in enumerate(chips):
            for p in range(npart):
                remote((3 + j) * npart + p, p, 2 * cx + cy, 1 - c, (x, y, c)).wait_recv()
        for cp in first + passed:
            cp.wait_send()

    got = pl.pallas_call(
        body, name="gather_weights",
        in_specs=[ANY] * npart, out_specs=[ANY] * npart,
        out_shape=[jax.ShapeDtypeStruct((NSHARD,) + a.shape, a.dtype) for a in parts],
        scratch_shapes=[pltpu.SemaphoreType.DMA((6 * npart,)), pltpu.SemaphoreType.DMA((6 * npart,))],
    )(*parts)
    mine = 2 * lax.axis_index("x") + lax.axis_index("y")
    return [lax.dynamic_update_index_in_dim(g, a, mine, 0) for g, a in zip(got, parts)]


def _swap_halves(parts, tag=""):
    npart = len(parts)

    def body(*refs):
        ins, outs = refs[:npart], refs[npart:2 * npart]
        send_sems, recv_sems = refs[2 * npart:]
        x, y, c, _ = _place()
        cps = [pltpu.make_async_remote_copy(
            src_ref=ins[p].at[s, 1 - c], dst_ref=outs[p].at[s],
            send_sem=send_sems.at[s * npart + p], recv_sem=recv_sems.at[s * npart + p],
            device_id=(x, y, 1 - c), device_id_type=MESH) for s in range(NSHARD) for p in range(npart)]
        for cp in cps:
            cp.start()
        for cp in cps:
            cp.wait()

    return pl.pallas_call(
        body, name="swap_halves" + tag, in_specs=[ANY] * npart, out_specs=[ANY] * npart,
        out_shape=[jax.ShapeDtypeStruct((NSHARD,) + a.shape[2:], a.dtype) for a in parts],
        scratch_shapes=[pltpu.SemaphoreType.DMA((NSHARD * npart,)), pltpu.SemaphoreType.DMA((NSHARD * npart,))],
    )(*parts)


def _scatter_shards(parts):
    npart = len(parts)

    def body(*refs):
        ins, outs = refs[:npart], refs[npart:2 * npart]
        send_sems, recv_sems = refs[2 * npart:]
        x, y, c, chips = _place()
        cps = [pltpu.make_async_remote_copy(
            src_ref=ins[p].at[2 * cx + cy], dst_ref=outs[p].at[j],
            send_sem=send_sems.at[j * npart + p], recv_sem=recv_sems.at[j * npart + p],
            device_id=(cx, cy, c), device_id_type=MESH)
            for j, (cx, cy) in enumerate(chips) for p in range(npart)]
        for cp in cps:
            cp.start()
        for cp in cps:
            cp.wait()

    return pl.pallas_call(
        body, name="scatter_shards", in_specs=[ANY] * npart, out_specs=[ANY] * npart,
        out_shape=[jax.ShapeDtypeStruct((3,) + a.shape[1:], a.dtype) for a in parts],
        scratch_shapes=[pltpu.SemaphoreType.DMA((3 * npart,)), pltpu.SemaphoreType.DMA((3 * npart,))],
    )(*parts)


HBM = pl.BlockSpec(memory_space=pltpu.HBM)
SEM = pl.BlockSpec(memory_space=pltpu.SEMAPHORE)
EFFECT = pltpu.SideEffectType.DATAFLOW_SIDE_EFFECTING


def _scatter_copies(srcs, lands, send_sems, recv_sems, waiting):
    x, y, c, chips = _place()
    n = len(srcs)
    return [pltpu.make_async_remote_copy(
        src_ref=srcs[p].at[2 * cx + cy], dst_ref=lands[p].at[j],
        send_sem=send_sems.at[j * n + p], recv_sem=recv_sems.at[j * n + p],
        device_id=(cx, cy, c), device_id_type=MESH)
        for j, (cx, cy) in enumerate(chips) for p in range(n)]


def _proj_copies(srcs, lands, send_sems, recv_sems, waiting):
    x, y, c, chips = _place()
    mine = 2 * x + y
    return [pltpu.make_async_remote_copy(
        src_ref=srcs[0].at[c], dst_ref=lands[0].at[mine, c],
        send_sem=send_sems.at[2 * j + to], recv_sem=recv_sems.at[2 * j + (to if waiting else c)],
        device_id=(cx, cy, to), device_id_type=MESH)
        for j, (cx, cy) in enumerate(chips) for to in range(2)]


def _start_copies(copies, nsem, parts, lands, name, after=None):
    n = len(parts)
    extra = [] if after is None else [after]

    def body(*refs):
        outs = refs[2 * n + len(extra):]
        for cp in copies(refs[:n], refs[n:2 * n], outs[0], outs[1], False):
            cp.start()
        outs[-1][...] = jnp.zeros_like(outs[-1])

    res = pl.pallas_call(
        body, name=name,
        out_shape=(pltpu.SemaphoreType.DMA((nsem,)), pltpu.SemaphoreType.DMA((nsem,)),
                   *[pltpu.HBM(a.shape, a.dtype) for a in parts], *[pltpu.HBM(a.shape, a.dtype) for a in lands],
                   jax.ShapeDtypeStruct((8, 128), F32)),
        in_specs=[HBM] * (2 * n) + [ANY] * len(extra),
        out_specs=(SEM, SEM, *[HBM] * (2 * n), pl.BlockSpec(memory_space=pltpu.VMEM)),
        input_output_aliases={i: 2 + i for i in range(2 * n)},
        compiler_params=pltpu.CompilerParams(has_side_effects=EFFECT),
    )(*[pltpu.with_memory_space_constraint(a, pltpu.HBM) for a in parts],
      *[pltpu.with_memory_space_constraint(a, pltpu.HBM) for a in lands], *extra)
    return res[0], res[1], res[2:2 + n], res[2 + n:2 + 2 * n], res[-1]


def _wait_copies(copies, started, after, name):
    send_sems, recv_sems, srcs, lands, _ = started
    n = len(srcs)

    def body(*refs):
        for cp in copies(refs[:n], refs[n:2 * n], refs[2 * n], refs[2 * n + 1], True):
            cp.wait_send()
            cp.wait_recv()

    res = pl.pallas_call(
        body, name=name,
        out_shape=tuple(pltpu.HBM(a.shape, a.dtype) for a in (*srcs, *lands)),
        in_specs=[HBM] * (2 * n) + [SEM, SEM, ANY], out_specs=tuple([HBM] * (2 * n)),
        input_output_aliases={i: i for i in range(2 * n)},
        compiler_params=pltpu.CompilerParams(has_side_effects=EFFECT),
    )(*srcs, *lands, send_sems, recv_sems, after)
    return res[n:]


def _join_halves(parts):
    npart = len(parts)

    def body(*refs):
        ins, outs = refs[:npart], refs[npart:2 * npart]
        send_sems, recv_sems = refs[2 * npart:]
        x, y, c, _ = _place()
        cps = [pltpu.make_async_remote_copy(
            src_ref=ins[p], dst_ref=outs[p], send_sem=send_sems.at[p], recv_sem=recv_sems.at[p],
            device_id=(x, y, 1 - c), device_id_type=MESH) for p in range(npart)]
        for cp in cps:
            cp.start()
        for cp in cps:
            cp.wait()

    return pl.pallas_call(
        body, name="join_halves", in_specs=[ANY] * npart, out_specs=[ANY] * npart,
        out_shape=[jax.ShapeDtypeStruct(a.shape, a.dtype) for a in parts],
        scratch_shapes=[pltpu.SemaphoreType.DMA((npart,)), pltpu.SemaphoreType.DMA((npart,))],
    )(*parts)


def _allreduce_small(v):
    r, ncol = v.shape

    def body(v_ref, o_ref, buf, send_sems, recv_sems):
        x, y, c, _ = _place()
        me = 4 * x + 2 * y + c
        buf[me] = v_ref[...]
        cps = []
        for k in range(1, 8):
            px, py, pc = x ^ (k >> 2), y ^ ((k >> 1) & 1), c ^ (k & 1)
            cps.append(pltpu.make_async_remote_copy(
                src_ref=v_ref, dst_ref=buf.at[me], send_sem=send_sems.at[k - 1], recv_sem=recv_sems.at[k - 1],
                device_id=(px, py, pc), device_id_type=MESH))
        for cp in cps:
            cp.start()
        for k in range(1, 8):
            px, py, pc = x ^ (k >> 2), y ^ ((k >> 1) & 1), c ^ (k & 1)
            pltpu.make_async_remote_copy(
                src_ref=v_ref, dst_ref=buf.at[4 * px + 2 * py + pc], send_sem=send_sems.at[k - 1],
                recv_sem=recv_sems.at[k - 1], device_id=(px, py, pc), device_id_type=MESH).wait_recv()
        for cp in cps:
            cp.wait_send()
        acc = buf[0]
        for d in range(1, 8):
            acc = acc + buf[d]
        o_ref[...] = acc

    return pl.pallas_call(
        body, name="allreduce_small",
        in_specs=[pl.BlockSpec(memory_space=pltpu.VMEM)], out_specs=pl.BlockSpec(memory_space=pltpu.VMEM),
        out_shape=jax.ShapeDtypeStruct((r, ncol), F32),
        scratch_shapes=[pltpu.VMEM((8, r, ncol), F32), pltpu.SemaphoreType.DMA((7,)), pltpu.SemaphoreType.DMA((7,))],
    )(v)


def _permute_rows(shards):
    w0, w1, w2, w3 = shards
    zeros = jnp.zeros((NPERM - 9280, w0.shape[1]), w0.dtype)
    return jnp.concatenate([w0, w1[0:1776], w1[1808:2320], w2, w3[0:240], w3[272:2320],
                            w1[1776:1808], w3[240:272], zeros], axis=0)


def _unpermute_rows(g):
    s1 = jnp.concatenate([g[2320:4096], g[9216:9248], g[4096:4608]], axis=0)
    s3 = jnp.concatenate([g[6928:7168], g[9248:9280], g[7168:9216]], axis=0)
    return jnp.stack([g[0:2320], s1, g[4608:6928], s3], axis=0)


def _pack_shard_small(conv, w2f, w2b):
    top = jnp.pad(conv, ((0, 8 - conv.shape[0]), (0, 0)))
    mid = jnp.pad(jnp.concatenate([w2f, w2b], axis=1), ((0, 0), (0, 768 - 256)))
    return jnp.concatenate([top, mid, jnp.zeros((8, 768), conv.dtype)], axis=0)


def _unpack_shard_small(a):
    return a[0:5], a[8:24, 0:128], a[8:24, 128:256]


def kernel(x, ln_pre_w, w_in, conv_w, a_log_fwd, a_log_bwd, dt_bias_fwd, dt_bias_bwd, gdn_norm_w, w_proj_gdn, gk_w2_fwd, gk_b2_fwd, gk_w2_bwd, gk_b2_bwd, gla_norm_w, w_proj_gla, w_out, ln_post_w, loss_target, m_ln_pre_w, m_w_in, m_conv_w, m_a_log_fwd, m_a_log_bwd, m_dt_bias_fwd, m_dt_bias_bwd, m_gdn_norm_w, m_w_proj_gdn, m_gk_w2_fwd, m_gk_b2_fwd, m_gk_w2_bwd, m_gk_b2_bwd, m_gla_norm_w, m_w_proj_gla, m_w_out, m_ln_post_w, v_ln_pre_w, v_w_in, v_conv_w, v_a_log_fwd, v_a_log_bwd, v_dt_bias_fwd, v_dt_bias_bwd, v_gdn_norm_w, v_w_proj_gdn, v_gk_w2_fwd, v_gk_b2_fwd, v_gk_w2_bwd, v_gk_b2_bwd, v_gla_norm_w, v_w_proj_gla, v_w_out, v_ln_post_w):
    t = x.shape[1]
    x2, tgt = x[0], loss_target[0]

    win_l = w_in[0].T.astype(BF16).reshape(2, SHW // 2, D)
    proj_l = jnp.concatenate([w_proj_gdn[0], w_proj_gla[0], w_out[0]], axis=0).astype(BF16).reshape(2, 384, D)
    small_l = _pack_shard_small(conv_w[0], gk_w2_fwd[0], gk_w2_bwd[0]).reshape(2, 16, 768)
    win_g, small_g = _gather_weights([win_l, small_l])
    proj_started = _start_copies(_proj_copies, 6, [proj_l], [lax.empty((NSHARD, 2, 384, D), BF16)],
                                 "gather_proj_start", after=small_g)
    wperm = _permute_rows(win_g.reshape(NSHARD, SHW, D))
    small_g = small_g.reshape(NSHARD, 32, 768)
    convw = small_g[:, 0:8, :].transpose(1, 0, 2).reshape(8, 3 * D)
    w2f = small_g[:, 8:24, 0:128].transpose(1, 0, 2).reshape(16, 512)
    w2b = small_g[:, 8:24, 128:256].transpose(1, 0, 2).reshape(16, 512)
    w2f_pad = jnp.pad(w2f, ((32, 80), (0, 0)))
    w2b_pad = jnp.pad(w2b, ((48, 64), (0, 0)))
    alog_row = jnp.pad(jnp.concatenate([a_log_fwd, a_log_bwd], axis=1), ((0, 0), (0, 112)))
    dt_row = jnp.pad(jnp.concatenate([dt_bias_fwd, dt_bias_bwd], axis=1), ((0, 0), (0, 112)))

    p, h = _inproj(x2, ln_pre_w + proj_started[4][0:1, 0:1], wperm)
    qn, kn, vc = (_qkv_fwd(p, convw, kind) for kind in range(3))
    gsm, gk = _gates_fwd(p, alog_row, dt_row, w2f_pad, gk_b2_fwd, w2b_pad, gk_b2_bwd)
    g2, b2 = _gcum_fwd(gsm)
    u, w, at, qd, kd, el, tinv = _gdn_intra_fwd(qn, kn, vc, g2, b2)
    oa2, sa = _gdn_scan_fwd(u, w, at, qd, kd, el)
    qg, kdb, intra, elb = _gla_intra_fwd(p, gk)
    ob2, sb = _gla_scan_fwd(p, qg, kdb, intra, elb)

    (proj_land,) = _wait_copies(_proj_copies, proj_started, ob2, "gather_proj_wait")
    mine = 2 * lax.axis_index("x") + lax.axis_index("y")
    w3 = lax.dynamic_update_index_in_dim(proj_land, proj_l, mine, 0).reshape(NSHARD, 3, D // NSHARD, D)
    (loss8, doa, dob, dz, dgb, dga, dgB, dyres, dw3, dgdn_w, dgla_w, dlnpost) = _post(
        oa2, ob2, p, x2, tgt, gdn_norm_w, gla_norm_w, ln_post_w, w3)

    g_proj = dw3.reshape(NSHARD, 2, 384, D)
    sum_proj = _sum_cast(g_proj, _swap_halves([g_proj], "_proj")[0])
    started_proj = _start_copies(_scatter_copies, 3, [sum_proj[1]], [lax.empty((3, 384, D), BF16)],
                                 "scatter_proj_start")
    du, dw, dat, dqd, dkd, del_ = _gdn_scan_bwd(u, w, at, qd, kd, el + started_proj[4][0, 0], sa, doa)
    dqn, dkn, dvc, dg2, db2 = _gdn_intra_bwd(qn, kn, vc, g2, b2, tinv, du, dw, dat, dqd, dkd, del_)
    dgsm = _gcum_bwd(gsm, dg2, db2)
    dqg, dkdb, dvs, delb = _gla_scan_bwd(p, qg, kdb, elb, sb, dob)
    dqb, dkb, dvb, dgk = _gla_intra_bwd(p, gk, dqg, dkdb, dvs, delb, dob)
    (dps, dalog8, ddt8, dw2f_pad, db2f8, dw2b_pad, db2b8) = _gates_bwd(
        p, alog_row, dt_row, w2f_pad, gk_b2_fwd, w2b_pad, gk_b2_bwd, dgsm, dgk)
    dpre, dconv = zip(*[_qkv_bwd(p, convw, g, kind) for kind, g in enumerate((dqn, dkn, dvc))])

    pieces = (jnp.concatenate([a.astype(BF16) for a in (*dpre, dz, dqb, dkb, dvb, dgb, dga, dgB, dps)], axis=1),)
    dwperm = _inproj_dw(h, pieces)

    g_in = _unpermute_rows(dwperm).reshape(NSHARD, 2, SHW // 2, D)
    dconv_full = jnp.concatenate(dconv, axis=1)
    dw2f, dw2b = dw2f_pad[32:48], dw2b_pad[48:64]
    g_small = jnp.stack([_pack_shard_small(dconv_full[0:5, 768 * s:768 * (s + 1)],
                                           dw2f[:, 128 * s:128 * (s + 1)], dw2b[:, 128 * s:128 * (s + 1)])
                         for s in range(NSHARD)])
    g_small = g_small.reshape(NSHARD, 2, 16, 768)
    parts = [g_in, g_small]
    got = _swap_halves(parts)
    sums = [_sum_cast(a, b) for a, b in zip(parts, got)]
    hbs = [hb for _, hb in sums]
    started = _start_copies(_scatter_copies, 3 * len(hbs), hbs,
                            [lax.empty((3,) + a.shape[1:], a.dtype) for a in hbs], "scatter_start")
    dx, dlnpre8 = _inproj_dx(pieces, wperm, x2, ln_pre_w + started[4][0:1, 0:1], dyres)

    gsmall = _allreduce_small(jnp.concatenate(
        [dlnpre8, dalog8, ddt8, dgdn_w, db2f8, db2b8, dgla_w, dlnpost, loss8], axis=1))
    smalls = dict(ln_pre_w=(ln_pre_w, m_ln_pre_w, v_ln_pre_w), a_log_fwd=(a_log_fwd, m_a_log_fwd, v_a_log_fwd),
                  a_log_bwd=(a_log_bwd, m_a_log_bwd, v_a_log_bwd),
                  dt_bias_fwd=(dt_bias_fwd, m_dt_bias_fwd, v_dt_bias_fwd),
                  dt_bias_bwd=(dt_bias_bwd, m_dt_bias_bwd, v_dt_bias_bwd),
                  gdn_norm_w=(gdn_norm_w, m_gdn_norm_w, v_gdn_norm_w),
                  gk_b2_fwd=(gk_b2_fwd, m_gk_b2_fwd, v_gk_b2_fwd), gk_b2_bwd=(gk_b2_bwd, m_gk_b2_bwd, v_gk_b2_bwd),
                  gla_norm_w=(gla_norm_w, m_gla_norm_w, v_gla_norm_w), ln_post_w=(ln_post_w, m_ln_post_w, v_ln_post_w))
    names = [name for name, _, _, _ in SMALL_SLOTS]
    small = _adam_small(gsmall, *([smalls[n][i] for n in names] for i in range(3)))

    landed_proj = _wait_copies(_scatter_copies, started_proj, small["ln_pre_w"][1], "scatter_proj_wait")
    landed = _wait_copies(_scatter_copies, started, small["ln_pre_w"][1], "scatter_wait")
    sums = [sums[0], sum_proj, sums[1]]
    halves = [_sum4(f, g) for (f, _), g in zip(sums, [landed[0], landed_proj[0], landed[1]])]
    theirs = _join_halves(halves)

    a_in = [a.T for a in _adam(w_in[0].T, halves[0], theirs[0], m_w_in[0].T, v_w_in[0].T, 232, name="in")]
    a_pr = [_adam(w[0], halves[1], theirs[1], m[0], v[0], 128, tile0=2 * i, name=f"proj{i}")
            for i, (w, m, v) in enumerate(((w_proj_gdn, m_w_proj_gdn, v_w_proj_gdn),
                                           (w_proj_gla, m_w_proj_gla, v_w_proj_gla), (w_out, m_w_out, v_w_out)))]
    a_ss = _adam(_pack_shard_small(conv_w[0], gk_w2_fwd[0], gk_w2_bwd[0]), halves[2], theirs[2],
                 _pack_shard_small(m_conv_w[0], m_gk_w2_fwd[0], m_gk_w2_bwd[0]),
                 _pack_shard_small(v_conv_w[0], v_gk_w2_fwd[0], v_gk_w2_bwd[0]), 16, name="small")

    def family(k):
        conv, w2f_, w2b_ = _unpack_shard_small(a_ss[k])
        s = {n: small[n][k] for n in names}
        return [s["ln_pre_w"], a_in[k][None], conv[None], s["a_log_fwd"], s["a_log_bwd"], s["dt_bias_fwd"],
                s["dt_bias_bwd"], s["gdn_norm_w"], a_pr[0][k][None], w2f_[None], s["gk_b2_fwd"], w2b_[None],
                s["gk_b2_bwd"], s["gla_norm_w"], a_pr[1][k][None], a_pr[2][k][None], s["ln_post_w"]]

    return (gsmall[0, SMALL_W - 128], dx[None], *family(0), *family(1), *family(2), *family(3))
```

```python
import functools

import jax
import jax.numpy as jnp
from jax import lax
from jax.experimental import pallas as pl
from jax.experimental.pallas import tpu as pltpu

F32 = jnp.float32
BF16 = jnp.bfloat16
HI = lax.Precision.HIGHEST
MESH = pl.DeviceIdType.MESH

D = 1024
CH = 64
EPS = 1e-6
NA, DA = 8, 128
NB, DKB, DVB = 4, 128, 256
NSHARD = 4
SHW = 2320
NPERM = 9728
PS_BLOCK = 72
LR, B1, B2, ADAM_EPS, WD, STEP = 0.001, 0.9, 0.999, 1e-08, 0.01, 10

ANY = pl.BlockSpec(memory_space=pl.ANY)


def _cparams(sem=None, vmem_mb=48):
    return pltpu.CompilerParams(dimension_semantics=sem, vmem_limit_bytes=vmem_mb << 20)


def _bdot(a, b, ca, cb):
    return lax.dot_general(a.astype(BF16), b.astype(BF16), (((ca,), (cb,)), ((), ())),
                           preferred_element_type=F32)


@jax.custom_vjp
def mm(a, b):
    return _bdot(a, b, 1, 0)


def _mm_fwd(a, b):
    return _bdot(a, b, 1, 0), (a, b)


def _mm_bwd(res, g):
    a, b = res
    return _bdot(g, b, 1, 1), _bdot(a, g, 0, 0)


mm.defvjp(_mm_fwd, _mm_bwd)


@jax.custom_vjp
def mm_nt(a, b):
    return _bdot(a, b, 1, 1)


def _mm_nt_fwd(a, b):
    return _bdot(a, b, 1, 1), (a, b)


def _mm_nt_bwd(res, g):
    a, b = res
    return _bdot(g, b, 1, 0), _bdot(g, a, 0, 0)


mm_nt.defvjp(_mm_nt_fwd, _mm_nt_bwd)


@jax.custom_vjp
def mm_tn(a, b):
    return _bdot(a, b, 0, 0)


def _mm_tn_fwd(a, b):
    return _bdot(a, b, 0, 0), (a, b)


def _mm_tn_bwd(res, g):
    a, b = res
    return _bdot(b, g, 1, 1), _bdot(a, g, 1, 0)


mm_tn.defvjp(_mm_tn_fwd, _mm_tn_bwd)


def dot_hi(a, b):
    return lax.dot_general(a, b, (((1,), (0,)), ((), ())), precision=HI, preferred_element_type=F32)


def _split3(x):
    x1 = x.astype(BF16)
    r = x - x1.astype(F32)
    x2 = r.astype(BF16)
    return x1, x2, (r - x2.astype(F32)).astype(BF16)


def _cdot(c, x, cc, cx, c_first=True):
    parts = _split3(x)
    if c_first:
        return _bdot(c, parts[0], cc, cx) + _bdot(c, parts[1], cc, cx) + _bdot(c, parts[2], cc, cx)
    return _bdot(parts[0], c, cx, cc) + _bdot(parts[1], c, cx, cc) + _bdot(parts[2], c, cx, cc)


@jax.custom_vjp
def cmm(c, x):
    return _cdot(c, x, 1, 0)


def _cmm_fwd(c, x):
    return _cdot(c, x, 1, 0), c


def _cmm_bwd(c, g):
    return jnp.zeros_like(c), _cdot(c, g, 0, 0)


cmm.defvjp(_cmm_fwd, _cmm_bwd)


@jax.custom_vjp
def mmc(x, c):
    return _cdot(c, x, 0, 1, c_first=False)


def _mmc_fwd(x, c):
    return _cdot(c, x, 0, 1, c_first=False), c


def _mmc_bwd(c, g):
    return _cdot(c, g, 1, 1, c_first=False), jnp.zeros_like(c)


mmc.defvjp(_mmc_fwd, _mmc_bwd)


def _sigmoid(x):
    return 1.0 / (1.0 + jnp.exp(-x))


def _silu(x):
    return x * _sigmoid(x)


def _softplus(x):
    return jnp.maximum(x, 0.0) + jnp.log(1.0 + jnp.exp(-jnp.abs(x)))


def _rms(x, w):
    return x * lax.rsqrt(jnp.mean(x * x, axis=-1, keepdims=True) + EPS) * w


SC = 256


class _Consts:
    def __init__(self, rev):
        r = lax.broadcasted_iota(jnp.int32, (SC, SC), 0)
        c = lax.broadcasted_iota(jnp.int32, (SC, SC), 1)
        same = (r >> 6) == (c >> 6)
        a = jnp.where(rev, c, r)
        b = jnp.where(rev, r, c)
        self.incl = same & (a >= b)
        self.strict = same & (a > b)
        self.incl_f = self.incl.astype(F32)
        self.eye = (r == c).astype(F32)
        rows = lax.broadcasted_iota(jnp.int32, (SC, 1), 0)
        self.last_col = ((rows & (CH - 1)) == jnp.where(rev, 0, CH - 1)).astype(F32)
        rr = lax.broadcasted_iota(jnp.int32, (SC, CH), 0)
        cc = lax.broadcasted_iota(jnp.int32, (SC, CH), 1)
        self.fold = ((rr & (CH - 1)) == cc).astype(F32)


def _dot3(a, b, ca=1, cb=0):
    ah, bh = a.astype(BF16), b.astype(BF16)
    al, bl = (a - ah.astype(F32)).astype(BF16), (b - bh.astype(F32)).astype(BF16)
    return _bdot(ah, bh, ca, cb) + (_bdot(ah, bl, ca, cb) + _bdot(al, bh, ca, cb))


TRI_SPLIT_LEVELS = 2


def _tri_inv(low, eye):
    n = -low
    acc = eye + n
    p = n
    for level in range(5):
        dot = _dot3 if level < TRI_SPLIT_LEVELS else (lambda a, b: _bdot(a, b, 1, 0))
        p = dot(p, p)
        acc = acc + dot(acc, p)
    return acc


@jax.custom_vjp
def _solve2(low, rv, rk, tinv):
    x = _dot3(tinv, jnp.concatenate([rv, rk], axis=1))
    return x[:, :DA], x[:, DA:]


def _solve2_fwd(low, rv, rk, tinv):
    x = _dot3(tinv, jnp.concatenate([rv, rk], axis=1))
    return (x[:, :DA], x[:, DA:]), (x, tinv)


def _solve2_bwd(res, g):
    x, tinv = res
    drhs = _dot3(tinv, jnp.concatenate(g, axis=1), 0, 0)
    return -_dot3(drhs, x, 1, 1), drhs[:, :DA], drhs[:, DA:], jnp.zeros_like(tinv)


_solve2.defvjp(_solve2_fwd, _solve2_bwd)


def _chunk_last(x, cs):
    xs = (x * cs.last_col).reshape(SC // CH, CH, x.shape[1])
    return jnp.broadcast_to(jnp.sum(xs, axis=1, keepdims=True), xs.shape).reshape(x.shape)


def _gdn_decay(g, cs):
    gw = jnp.concatenate([g] * (SC // DA), axis=1)
    grow = jnp.sum(cs.eye * gw, axis=0, keepdims=True)
    return jnp.where(cs.incl, jnp.exp(jnp.where(cs.incl, gw - grow, 0.0)), 0.0)


def _gdn_intra(q, k, v, g, bx, tinv, cs):
    decay = _gdn_decay(g, cs)
    kb = k * bx
    low = jnp.where(cs.strict, mm_nt(kb, k) * decay, 0.0)
    eg = jnp.exp(g)
    made = tinv is None
    if made:
        tinv = _tri_inv(low, cs.eye)
    u, w = _solve2(low, v * bx, kb * eg, tinv)
    attn = mmc(mm_nt(q, k) * decay, cs.fold)
    qd = q * eg
    glast = _chunk_last(g, cs)
    kd = k * jnp.exp(glast - g)
    outs = (u, w, attn, qd, kd, jnp.exp(glast))
    return outs + (tinv,) if made else outs


def _gdn_scan(u, w, attn, qd, kd, el, s):
    vn = u - mm(w, s)
    o = mm(qd, s) + mm(attn, vn)
    sn = s * el + mm_tn(kd, vn)
    return o, sn


def _gla_intra(q, k, v, gk, cs):
    gc = cmm(cs.incl_f, gk)
    qg = q * (DKB ** -0.5) * jnp.exp(gc)
    kg = k * jnp.exp(-gc)
    attn = jnp.where(cs.incl, mm_nt(qg, kg), 0.0)
    intra = mm(attn, v)
    glast = _chunk_last(gc, cs)
    kd = k * jnp.exp(glast - gc)
    return qg, kd, intra, jnp.exp(glast)


def _gla_scan(qg, kd, v, el, st):
    o = mm_nt(qg, st)
    stn = st * el + mm_tn(v, kd)
    return o, stn


def _shift_rows(x, s):
    if s == 0:
        return x
    t = x.shape[0]
    rolled = pltpu.roll(x, (-s) % t, 0)
    rows = lax.broadcasted_iota(jnp.int32, x.shape, 0)
    return jnp.where((rows + s >= 0) & (rows + s < t), rolled, 0.0)


@jax.custom_vjp
def _conv5(x, w):
    acc = w[0:1] * _shift_rows(x, -2)
    for j in range(1, 5):
        acc = acc + w[j:j + 1] * _shift_rows(x, j - 2)
    return acc


def _conv5_fwd(x, w):
    return _conv5(x, w), (x, w)


def _conv5_bwd(res, g):
    x, w = res
    dx = w[0:1] * _shift_rows(g, 2)
    for j in range(1, 5):
        dx = dx + w[j:j + 1] * _shift_rows(g, 2 - j)
    rows = lax.broadcasted_iota(jnp.int32, w.shape, 0)
    dw = jnp.zeros_like(w)
    for j in range(5):
        dwj = jnp.sum(g * _shift_rows(x, j - 2), axis=0, keepdims=True)
        dw = dw + jnp.where(rows == j, dwj, 0.0)
    return dx, dw


_conv5.defvjp(_conv5_fwd, _conv5_bwd)


def _qkv_act(kind):
    def f(x, w):
        c = _silu(_conv5(x, w))
        if kind == 2:
            return c
        c = c * lax.rsqrt(jnp.sum(c * c, axis=-1, keepdims=True) + EPS)
        return c * (DA ** -0.5) if kind == 0 else c
    return f


def _inproj(x, lnw, wperm, tn=512):
    t = x.shape[0]
    tm = min(t, 2048)

    def body(x_ref, lnw_ref, w_ref, p_ref, h_ref, hbuf):
        @pl.when(pl.program_id(1) == 0)
        def _():
            hb = _rms(x_ref[...], lnw_ref[...]).astype(BF16)
            hbuf[...] = hb
            h_ref[...] = hb
        p_ref[...] = _bdot(hbuf[...], w_ref[...], 1, 1)

    return pl.pallas_call(
        body, name="inproj", grid=(t // tm, NPERM // tn),
        in_specs=[pl.BlockSpec((tm, D), lambda i, j: (i, 0)),
                  pl.BlockSpec((1, D), lambda i, j: (0, 0)),
                  pl.BlockSpec((tn, D), lambda i, j: (j, 0))],
        out_specs=[pl.BlockSpec((tm, tn), lambda i, j: (i, j)),
                   pl.BlockSpec((tm, D), lambda i, j: (i, 0))],
        out_shape=[jax.ShapeDtypeStruct((t, NPERM), F32),
                   jax.ShapeDtypeStruct((t, D), BF16)],
        scratch_shapes=[pltpu.VMEM((tm, D), BF16)],
        compiler_params=_cparams(("parallel", "arbitrary")),
    )(x, lnw, wperm)


DP_TILE = 512
DP_PIECES = ((0, 19),)


def _piece_specs(tm, j_first):
    specs = []
    for j0, n in DP_PIECES:
        def imap(a, b, j0=j0, n=n):
            j, i = (a, b) if j_first else (b, a)
            inside = (j >= j0) & (j < j0 + n)
            return jnp.where(inside, i, 0), jnp.clip(j - j0, 0, n - 1)
        specs.append(pl.BlockSpec((tm, DP_TILE), imap))
    return specs


def _for_piece(j, refs, fn):
    for (j0, n), ref in zip(DP_PIECES, refs):
        @pl.when((j >= j0) & (j < j0 + n))
        def _(ref=ref):
            fn(ref[...])


def _inproj_dw(h, pieces):
    t = h.shape[0]
    tm = min(t, 2048)
    npc = len(pieces)

    def body(h_ref, *refs):
        dw_ref = refs[npc]

        @pl.when(pl.program_id(1) == 0)
        def _():
            dw_ref[...] = jnp.zeros_like(dw_ref)

        def add(dp):
            dw_ref[...] += _bdot(dp, h_ref[...], 0, 0)
        _for_piece(pl.program_id(0), refs[:npc], add)

    return pl.pallas_call(
        body, name="inproj_dw", grid=(NPERM // DP_TILE, t // tm),
        in_specs=[pl.BlockSpec((tm, D), lambda j, i: (i, 0))] + _piece_specs(tm, True),
        out_specs=pl.BlockSpec((DP_TILE, D), lambda j, i: (j, 0)),
        out_shape=jax.ShapeDtypeStruct((NPERM, D), F32),
        compiler_params=_cparams(("parallel", "arbitrary")),
    )(h, *pieces)


def _inproj_dx(pieces, wperm, x, lnw, dyres):
    t = x.shape[0]
    tm = min(t, 2048)
    tn = DP_TILE
    nj = NPERM // tn
    npc = len(pieces)

    def body(*refs):
        w_ref, x_ref, lnw_ref, dy_ref, dx_ref, dlnw_ref, acc = refs[npc:]
        j = pl.program_id(1)

        @pl.when(j == 0)
        def _():
            acc[...] = jnp.zeros_like(acc)

        def add(dp):
            acc[...] += _bdot(dp, w_ref[...], 1, 0)
        _for_piece(j, refs[:npc], add)

        @pl.when(j == nj - 1)
        def _():
            _, vjp = jax.vjp(_rms, x_ref[...], lnw_ref[...])
            dx, dlnw = vjp(acc[...])
            dx_ref[...] = dx + dy_ref[...]

            @pl.when(pl.program_id(0) == 0)
            def _():
                dlnw_ref[...] = jnp.zeros_like(dlnw_ref)
            dlnw_ref[...] += jnp.broadcast_to(dlnw, dlnw_ref.shape)

    rows = functools.partial(pl.BlockSpec, (tm, D), lambda i, j: (i, 0))
    if t == tm:
        rows = functools.partial(rows, pipeline_mode=pl.Buffered(1))
    return pl.pallas_call(
        body, name="inproj_dx", grid=(t // tm, nj),
        in_specs=_piece_specs(tm, False) + [
                  pl.BlockSpec((tn, D), lambda i, j: (j, 0)),
                  rows(),
                  pl.BlockSpec((1, D), lambda i, j: (0, 0)),
                  rows()],
        out_specs=[rows(),
                   pl.BlockSpec((8, D), lambda i, j: (0, 0))],
        out_shape=[jax.ShapeDtypeStruct((t, D), F32), jax.ShapeDtypeStruct((8, D), F32)],
        scratch_shapes=[pltpu.VMEM((tm, D), F32)],
        compiler_params=_cparams(("arbitrary", "arbitrary")),
    )(*pieces, wperm, x, lnw, dyres)


def _qkv_fwd(p, convw, kind):
    t = p.shape[0]
    f = _qkv_act(kind)

    def body(p_ref, w_ref, o_ref):
        o_ref[...] = f(p_ref[...], w_ref[...])

    return pl.pallas_call(
        body, name=f"qkv_fwd{kind}", grid=(NA,),
        in_specs=[pl.BlockSpec((t, DA), lambda h: (0, kind * NA + h)),
                  pl.BlockSpec((8, DA), lambda h: (0, kind * NA + h))],
        out_specs=pl.BlockSpec((t, DA), lambda h: (0, h)),
        out_shape=jax.ShapeDtypeStruct((t, D), F32),
        compiler_params=_cparams(("parallel",)),
    )(p, convw)


def _qkv_bwd(p, convw, dout, kind):
    t = p.shape[0]
    f = _qkv_act(kind)

    def body(p_ref, w_ref, g_ref, dx_ref, dw_ref):
        _, vjp = jax.vjp(f, p_ref[...], w_ref[...])
        dx, dw = vjp(g_ref[...])
        dx_ref[...] = dx.astype(BF16)
        dw_ref[...] = dw

    return pl.pallas_call(
        body, name=f"qkv_bwd{kind}", grid=(NA,),
        in_specs=[pl.BlockSpec((t, DA), lambda h: (0, kind * NA + h)),
                  pl.BlockSpec((8, DA), lambda h: (0, kind * NA + h)),
                  pl.BlockSpec((t, DA), lambda h: (0, h))],
        out_specs=[pl.BlockSpec((t, DA), lambda h: (0, h)),
                   pl.BlockSpec((8, DA), lambda h: (0, h))],
        out_shape=[jax.ShapeDtypeStruct((t, D), BF16), jax.ShapeDtypeStruct((8, D), F32)],
        compiler_params=_cparams(("parallel",)),
    )(p, convw, dout)


def _gates_f(ps, alog_row, dt_row, w2f, b2f, w2b, b2b):
    lane = lax.broadcasted_iota(jnp.int32, ps.shape, 1)
    lg = -jnp.exp(alog_row) * _softplus(ps + dt_row)
    gsm = jnp.where(lane < 16, lg, jnp.where(lane < 32, _sigmoid(ps), 0.0))
    gkf = -_softplus(-(mm(ps, w2f) + b2f)) * (1.0 / 16.0)
    gkb = -_softplus(-(mm(ps, w2b) + b2b)) * (1.0 / 16.0)
    return gsm, gkf, gkb


def _gates_fwd(ps, alog_row, dt_row, w2f, b2f, w2b, b2b, tm=512):
    t = ps.shape[0]

    def body(ps_ref, a_ref, d_ref, wf_ref, bf_ref, wb_ref, bb_ref, gsm_ref, gk_ref):
        gsm, gkf, gkb = _gates_f(ps_ref[...], a_ref[...], d_ref[...], wf_ref[...], bf_ref[...],
                                 wb_ref[...], bb_ref[...])
        gsm_ref[...] = gsm
        gk_ref[0] = gkf
        gk_ref[1] = gkb

    row = lambda n: pl.BlockSpec((1, n), lambda i: (0, 0))
    mat = pl.BlockSpec((128, 512), lambda i: (0, 0))
    return pl.pallas_call(
        body, name="gates_fwd", grid=(t // tm,),
        in_specs=[pl.BlockSpec((tm, 128), lambda i: (i, PS_BLOCK)), row(128), row(128), mat, row(512), mat, row(512)],
        out_specs=[pl.BlockSpec((tm, 128), lambda i: (i, 0)),
                   pl.BlockSpec((2, tm, 512), lambda i: (0, i, 0))],
        out_shape=[jax.ShapeDtypeStruct((t, 128), F32), jax.ShapeDtypeStruct((2, t, 512), F32)],
        compiler_params=_cparams(("parallel",)),
    )(ps, alog_row, dt_row, w2f, b2f, w2b, b2b)


def _gates_bwd(ps, alog_row, dt_row, w2f, b2f, w2b, b2b, dgsm, dgk, tm=512):
    t = ps.shape[0]

    def body(ps_ref, a_ref, d_ref, wf_ref, bf_ref, wb_ref, bb_ref, dgsm_ref, dgk_ref,
             dps_ref, da_ref, dd_ref, dwf_ref, dbf_ref, dwb_ref, dbb_ref):
        _, vjp = jax.vjp(_gates_f, ps_ref[...], a_ref[...], d_ref[...], wf_ref[...], bf_ref[...],
                         wb_ref[...], bb_ref[...])
        dps, da, dd, dwf, dbf, dwb, dbb = vjp((dgsm_ref[...], dgk_ref[0], dgk_ref[1]))
        dps_ref[:, 0:128] = dps.astype(BF16)
        dps_ref[:, 128:DP_TILE] = jnp.zeros((tm, DP_TILE - 128), BF16)
        accs = ((da_ref, da), (dd_ref, dd), (dwf_ref, dwf), (dbf_ref, dbf), (dwb_ref, dwb), (dbb_ref, dbb))

        @pl.when(pl.program_id(0) == 0)
        def _():
            for ref, _ in accs:
                ref[...] = jnp.zeros_like(ref)
        for ref, val in accs:
            ref[...] += jnp.broadcast_to(val, ref.shape)

    row = lambda n: pl.BlockSpec((1, n), lambda i: (0, 0))
    row8 = lambda n: pl.BlockSpec((8, n), lambda i: (0, 0))
    mat = pl.BlockSpec((128, 512), lambda i: (0, 0))
    return pl.pallas_call(
        body, name="gates_bwd", grid=(t // tm,),
        in_specs=[pl.BlockSpec((tm, 128), lambda i: (i, PS_BLOCK)), row(128), row(128), mat, row(512), mat, row(512),
                  pl.BlockSpec((tm, 128), lambda i: (i, 0)),
                  pl.BlockSpec((2, tm, 512), lambda i: (0, i, 0))],
        out_specs=[pl.BlockSpec((tm, DP_TILE), lambda i: (i, 0)), row8(128), row8(128), mat, row8(512), mat,
                   row8(512)],
        out_shape=[jax.ShapeDtypeStruct((t, DP_TILE), BF16),
                   jax.ShapeDtypeStruct((8, 128), F32), jax.ShapeDtypeStruct((8, 128), F32),
                   jax.ShapeDtypeStruct((128, 512), F32), jax.ShapeDtypeStruct((8, 512), F32),
                   jax.ShapeDtypeStruct((128, 512), F32), jax.ShapeDtypeStruct((8, 512), F32)],
        compiler_params=_cparams(("arbitrary",)),
    )(ps, alog_row, dt_row, w2f, b2f, w2b, b2b, dgsm, dgk)


def _rows(i):
    return pl.ds(pl.multiple_of(i * CH, CH), CH)


def _srows(i):
    return pl.ds(pl.multiple_of(i * SC, SC), SC)


def _first_row(x):
    row = lax.broadcasted_iota(jnp.int32, (8, x.shape[1]), 0)
    return jnp.where(row == 0, jnp.broadcast_to(x, (8, x.shape[1])), 0.0)


def _chunk_rows(e_ref, i):
    pad = jnp.zeros((CH - 8, 128), F32)
    return jnp.concatenate([x for c in range(SC // CH) for x in (e_ref[(SC // CH) * i + c], pad)], axis=0)


def _gcum_f(gsm, tm):
    i = lax.broadcasted_iota(jnp.int32, (tm, tm), 0)
    j = lax.broadcasted_iota(jnp.int32, (tm, tm), 1)
    same = (i >> 6) == (j >> 6)
    lower = (same & (i >= j)).astype(F32)
    upper = (same & (i <= j)).astype(F32)
    r = lax.broadcasted_iota(jnp.int32, (128, D), 0)
    head = lax.broadcasted_iota(jnp.int32, (128, D), 1) >> 7
    pick = lambda off: (r == head + off).astype(F32)
    lane = lax.broadcasted_iota(jnp.int32, gsm.shape, 1)
    run = jnp.where(lane < 8, cmm(lower, gsm), cmm(upper, gsm))
    return mmc(run, pick(0)), mmc(run, pick(8)), mmc(gsm, pick(16)), mmc(gsm, pick(24))


def _gcum_fwd(gsm, tm=256):
    t = gsm.shape[0]

    def body(s_ref, g_ref, b_ref):
        gf, gb, bf, bb = _gcum_f(s_ref[...], tm)
        g_ref[0] = gf
        g_ref[1] = gb
        b_ref[0] = bf
        b_ref[1] = bb

    two = pl.BlockSpec((2, tm, D), lambda i: (0, i, 0))
    return pl.pallas_call(
        body, name="gcum_fwd", grid=(t // tm,),
        in_specs=[pl.BlockSpec((tm, 128), lambda i: (i, 0))], out_specs=[two, two],
        out_shape=[jax.ShapeDtypeStruct((2, t, D), F32)] * 2,
        compiler_params=_cparams(("parallel",)),
    )(gsm)


def _gcum_bwd(gsm, dg2, db2, tm=256):
    t = gsm.shape[0]

    def body(s_ref, dg_ref, db_ref, ds_ref):
        _, vjp = jax.vjp(lambda s: _gcum_f(s, tm), s_ref[...])
        ds_ref[...] = vjp((dg_ref[0], dg_ref[1], db_ref[0], db_ref[1]))[0]

    two = pl.BlockSpec((2, tm, D), lambda i: (0, i, 0))
    tile = pl.BlockSpec((tm, 128), lambda i: (i, 0))
    return pl.pallas_call(
        body, name="gcum_bwd", grid=(t // tm,),
        in_specs=[tile, two, two], out_specs=tile,
        out_shape=jax.ShapeDtypeStruct((t, 128), F32),
        compiler_params=_cparams(("parallel",)),
    )(gsm, dg2, db2)


def _gdn_intra_fwd(qn, kn, vc, g2, b2):
    t = qn.shape[0]
    n = t // CH

    def body(q_ref, k_ref, v_ref, g_ref, b_ref, u_ref, w_ref, a_ref, qd_ref, kd_ref, e_ref, t_ref):
        cs = _Consts(pl.program_id(0) == 1)

        def step(i, carry):
            r = _srows(i)
            q, k, v, g, bx = q_ref[r, :], k_ref[r, :], v_ref[r, :], g_ref[r, :], b_ref[r, :]
            u, w, a, qd, kd, el, tinv = _gdn_intra(q, k, v, g, bx, None, cs)
            u_ref[r, :] = u
            w_ref[r, :] = w
            a_ref[r, :] = a
            qd_ref[r, :] = qd
            kd_ref[r, :] = kd
            t_ref[r, :] = tinv
            for c in range(SC // CH):
                e_ref[(SC // CH) * i + c] = el[c * CH:c * CH + 8]
            return carry

        lax.fori_loop(0, t // SC, step, 0)

    head = pl.BlockSpec((t, DA), lambda d, h: (0, h))
    dh = pl.BlockSpec((None, t, DA), lambda d, h: (d, 0, h))
    sq = lambda w: pl.BlockSpec((None, None, t, w), lambda d, h: (d, h, 0, 0))
    big = jax.ShapeDtypeStruct((2, t, D), F32)
    return pl.pallas_call(
        body, name="gdn_intra_fwd", grid=(2, NA),
        in_specs=[head, head, head, dh, dh],
        out_specs=[dh, dh, sq(CH), dh, dh, pl.BlockSpec((None, None, n, 8, 128), lambda d, h: (d, h, 0, 0, 0)),
                   sq(SC)],
        out_shape=[big, big, jax.ShapeDtypeStruct((2, NA, t, CH), F32), big, big,
                   jax.ShapeDtypeStruct((2, NA, n, 8, 128), F32), jax.ShapeDtypeStruct((2, NA, t, SC), F32)],
        compiler_params=_cparams(("parallel", "parallel")),
    )(qn, kn, vc, g2, b2)


SCAN_TB = 256
SCAN_HB = 8


def _scan_specs(t, width, nheads, hb, along):
    nt = t // SCAN_TB
    nb = SCAN_TB // CH

    def tmap(d, tt):
        fwd = tt + d * (nt - 1 - 2 * tt)
        return fwd if along > 0 else nt - 1 - fwd

    tok = pl.BlockSpec((None, SCAN_TB, hb * width), lambda d, h, tt: (d, tmap(d, tt), h))
    per = lambda *tail: pl.BlockSpec((None, hb, nb) + tail, lambda d, h, tt: (d, h, tmap(d, tt)) + (0,) * len(tail))
    sq = pl.BlockSpec((None, hb, SCAN_TB, CH), lambda d, h, tt: (d, h, tmap(d, tt), 0))
    shared = lambda w: pl.BlockSpec((SCAN_TB, hb * w), lambda d, h, tt: (tmap(d, tt), h))
    return tok, per, sq, shared, (2, nheads // hb, nt), nb


def _gdn_scan_fwd(u, w, a, qd, kd, e):
    t = u.shape[1]
    tok, per, sq, _, grid, nb = _scan_specs(t, DA, NA, SCAN_HB, +1)

    def body(u_ref, w_ref, a_ref, qd_ref, kd_ref, e_ref, o_ref, s_ref, state):
        rev = pl.program_id(0) == 1

        @pl.when(pl.program_id(2) == 0)
        def _():
            state[...] = jnp.zeros_like(state)

        def step(i, ss):
            ci = jnp.where(rev, nb - 1 - i, i)
            r = _rows(ci)
            out = []
            for hh, s in enumerate(ss):
                c = slice(hh * DA, (hh + 1) * DA)
                s_ref[hh, ci] = s
                o, sn = _gdn_scan(u_ref[r, c], w_ref[r, c], a_ref[hh, r, :], qd_ref[r, c], kd_ref[r, c],
                                  e_ref[hh, ci][0:1], s)
                o_ref[r, c] = o
                out.append(sn)
            return tuple(out)

        ss = lax.fori_loop(0, nb, step, tuple(state[hh] for hh in range(SCAN_HB)))
        for hh, s in enumerate(ss):
            state[hh] = s

    return pl.pallas_call(
        body, name="gdn_scan_fwd", grid=grid,
        in_specs=[tok, tok, sq, tok, tok, per(8, 128)],
        out_specs=[tok, per(DA, DA)],
        out_shape=[jax.ShapeDtypeStruct((2, t, D), F32), jax.ShapeDtypeStruct((2, NA, t // CH, DA, DA), F32)],
        scratch_shapes=[pltpu.VMEM((SCAN_HB, DA, DA), F32)],
        compiler_params=_cparams(("parallel", "parallel", "arbitrary")),
    )(u, w, a, qd, kd, e)


def _gdn_scan_bwd(u, w, a, qd, kd, e, ssave, do):
    t = u.shape[1]
    tok, per, sq, shared, grid, nb = _scan_specs(t, DA, NA, SCAN_HB, -1)

    def body(u_ref, w_ref, a_ref, qd_ref, kd_ref, e_ref, s_ref, do_ref,
             du_ref, dw_ref, da_ref, dqd_ref, dkd_ref, de_ref, state):
        rev = pl.program_id(0) == 1

        @pl.when(pl.program_id(2) == 0)
        def _():
            state[...] = jnp.zeros_like(state)

        def step(i, dss):
            ci = jnp.where(rev, i, nb - 1 - i)
            r = _rows(ci)
            out = []
            for hh, ds in enumerate(dss):
                c = slice(hh * DA, (hh + 1) * DA)
                _, vjp = jax.vjp(_gdn_scan, u_ref[r, c], w_ref[r, c], a_ref[hh, r, :], qd_ref[r, c], kd_ref[r, c],
                                 e_ref[hh, ci][0:1], s_ref[hh, ci])
                du, dw, da, dqd, dkd, de, dsn = vjp((do_ref[r, c], ds))
                du_ref[r, c] = du
                dw_ref[r, c] = dw
                da_ref[hh, r, :] = da
                dqd_ref[r, c] = dqd
                dkd_ref[r, c] = dkd
                de_ref[hh, ci] = _first_row(de)
                out.append(dsn)
            return tuple(out)

        dss = lax.fori_loop(0, nb, step, tuple(state[hh] for hh in range(SCAN_HB)))
        for hh, ds in enumerate(dss):
            state[hh] = ds

    big = jax.ShapeDtypeStruct((2, t, D), F32)
    return pl.pallas_call(
        body, name="gdn_scan_bwd", grid=grid,
        in_specs=[tok, tok, sq, tok, tok, per(8, 128), per(DA, DA), shared(DA)],
        out_specs=[tok, tok, sq, tok, tok, per(8, 128)],
        out_shape=[big, big, jax.ShapeDtypeStruct((2, NA, t, CH), F32), big, big,
                   jax.ShapeDtypeStruct((2, NA, t // CH, 8, 128), F32)],
        scratch_shapes=[pltpu.VMEM((SCAN_HB, DA, DA), F32)],
        compiler_params=_cparams(("parallel", "parallel", "arbitrary")),
    )(u, w, a, qd, kd, e, ssave, do)


def _gdn_intra_bwd(qn, kn, vc, g2, b2, tinv, du, dw, da, dqd, dkd, de):
    t = qn.shape[0]
    n = t // CH

    def body(q_ref, k_ref, v_ref, g_ref, b_ref, t_ref, du_ref, dw_ref, da_ref, dqd_ref, dkd_ref, de_ref,
             dq_ref, dk_ref, dv_ref, dg_ref, db_ref):
        d = pl.program_id(1)
        cs = _Consts(d == 1)

        @pl.when(d == 0)
        def _():
            dq_ref[...] = jnp.zeros_like(dq_ref)
            dk_ref[...] = jnp.zeros_like(dk_ref)
            dv_ref[...] = jnp.zeros_like(dv_ref)

        def step(i, carry):
            r = _srows(i)
            tinv_c = t_ref[r, :]
            f = lambda q, k, v, g, bx: _gdn_intra(q, k, v, g, bx, tinv_c, cs)
            _, vjp = jax.vjp(f, q_ref[r, :], k_ref[r, :], v_ref[r, :], g_ref[r, :], b_ref[r, :])
            dq, dk, dv, dg, dbx = vjp((du_ref[r, :], dw_ref[r, :], da_ref[r, :], dqd_ref[r, :],
                                       dkd_ref[r, :], _chunk_rows(de_ref, i)))
            dq_ref[r, :] += dq
            dk_ref[r, :] += dk
            dv_ref[r, :] += dv
            dg_ref[r, :] = dg
            db_ref[r, :] = dbx
            return carry

        lax.fori_loop(0, t // SC, step, 0)

    head = pl.BlockSpec((t, DA), lambda h, d: (0, h))
    dh = pl.BlockSpec((None, t, DA), lambda h, d: (d, 0, h))
    sq = pl.BlockSpec((None, None, t, CH), lambda h, d: (d, h, 0, 0))
    tq = pl.BlockSpec((None, None, t, SC), lambda h, d: (d, h, 0, 0))
    full = jax.ShapeDtypeStruct((t, D), F32)
    big = jax.ShapeDtypeStruct((2, t, D), F32)
    return pl.pallas_call(
        body, name="gdn_intra_bwd", grid=(NA, 2),
        in_specs=[head, head, head, dh, dh, tq, dh, dh, sq, dh, dh,
                  pl.BlockSpec((None, None, n, 8, 128), lambda h, d: (d, h, 0, 0, 0))],
        out_specs=[head, head, head, dh, dh],
        out_shape=[full, full, full, big, big],
        compiler_params=_cparams(("arbitrary", "arbitrary")),
    )(qn, kn, vc, g2, b2, tinv, du, dw, da, dqd, dkd, de)


def _gla_specs(t, order):
    ix = (lambda d, h: (d, h)) if order == "dh" else (lambda h, d: (d, h))

    def mk(fn):
        return lambda a, b: fn(*ix(a, b))
    q = pl.BlockSpec((t, DKB), mk(lambda d, h: (0, 32 + h)))
    k = pl.BlockSpec((t, DKB), mk(lambda d, h: (0, 36 + h)))
    v = pl.BlockSpec((t, DVB), mk(lambda d, h: (0, 20 + h)))
    dk = pl.BlockSpec((None, t, DKB), mk(lambda d, h: (d, 0, h)))
    dv = pl.BlockSpec((None, t, DVB), mk(lambda d, h: (d, 0, h)))
    e = pl.BlockSpec((None, None, t // CH, 8, 128), mk(lambda d, h: (d, h, 0, 0, 0)))
    s = pl.BlockSpec((None, None, t // CH, DVB, DKB), mk(lambda d, h: (d, h, 0, 0, 0)))
    return q, k, v, dk, dv, e, s


def _gla_intra_fwd(p, gk):
    t = p.shape[0]
    n = t // CH

    def body(q_ref, k_ref, v_ref, g_ref, qg_ref, kd_ref, in_ref, e_ref):
        cs = _Consts(pl.program_id(0) == 1)

        def step(i, carry):
            r = _srows(i)
            qg, kd, intra, el = _gla_intra(q_ref[r, :], k_ref[r, :], v_ref[r, :], g_ref[r, :], cs)
            qg_ref[r, :] = qg
            kd_ref[r, :] = kd
            in_ref[r, :] = intra
            for c in range(SC // CH):
                e_ref[(SC // CH) * i + c] = el[c * CH:c * CH + 8]
            return carry

        lax.fori_loop(0, t // SC, step, 0)

    q, k, v, dk, dv, e, _ = _gla_specs(t, "dh")
    return pl.pallas_call(
        body, name="gla_intra_fwd", grid=(2, NB),
        in_specs=[q, k, v, dk], out_specs=[dk, dk, dv, e],
        out_shape=[jax.ShapeDtypeStruct((2, t, NB * DKB), F32), jax.ShapeDtypeStruct((2, t, NB * DKB), F32),
                   jax.ShapeDtypeStruct((2, t, D), F32), jax.ShapeDtypeStruct((2, NB, n, 8, 128), F32)],
        compiler_params=_cparams(("parallel", "parallel")),
    )(p, p, p, gk)


GLA_HB = 4


def _gla_v_spec(t, along):
    nt = t // SCAN_TB

    def tmap(d, tt):
        fwd = tt + d * (nt - 1 - 2 * tt)
        return fwd if along > 0 else nt - 1 - fwd

    return pl.BlockSpec((SCAN_TB, GLA_HB * DVB), lambda d, h, tt: (tmap(d, tt), 5120 // (GLA_HB * DVB) + h))


def _gla_scan_fwd(p, qg, kd, intra, e):
    t = p.shape[0]
    tokk, per, _, _, grid, nb = _scan_specs(t, DKB, NB, GLA_HB, +1)
    tokv = _scan_specs(t, DVB, NB, GLA_HB, +1)[0]

    def body(v_ref, qg_ref, kd_ref, in_ref, e_ref, o_ref, s_ref, state):
        rev = pl.program_id(0) == 1

        @pl.when(pl.program_id(2) == 0)
        def _():
            state[...] = jnp.zeros_like(state)

        def step(i, sts):
            ci = jnp.where(rev, nb - 1 - i, i)
            r = _rows(ci)
            out = []
            for hh, st in enumerate(sts):
                ck = slice(hh * DKB, (hh + 1) * DKB)
                cv = slice(hh * DVB, (hh + 1) * DVB)
                s_ref[hh, ci] = st
                o, stn = _gla_scan(qg_ref[r, ck], kd_ref[r, ck], v_ref[r, cv], e_ref[hh, ci][0:1], st)
                o_ref[r, cv] = o + in_ref[r, cv]
                out.append(stn)
            return tuple(out)

        sts = lax.fori_loop(0, nb, step, tuple(state[hh] for hh in range(GLA_HB)))
        for hh, st in enumerate(sts):
            state[hh] = st

    return pl.pallas_call(
        body, name="gla_scan_fwd", grid=grid,
        in_specs=[_gla_v_spec(t, +1), tokk, tokk, tokv, per(8, 128)], out_specs=[tokv, per(DVB, DKB)],
        out_shape=[jax.ShapeDtypeStruct((2, t, D), F32), jax.ShapeDtypeStruct((2, NB, t // CH, DVB, DKB), F32)],
        scratch_shapes=[pltpu.VMEM((GLA_HB, DVB, DKB), F32)],
        compiler_params=_cparams(("parallel", "parallel", "arbitrary")),
    )(p, qg, kd, intra, e)


def _gla_scan_bwd(p, qg, kd, e, ssave, do):
    t = p.shape[0]
    tokk, per, _, shared, grid, nb = _scan_specs(t, DKB, NB, GLA_HB, -1)
    tokv = _scan_specs(t, DVB, NB, GLA_HB, -1)[0]

    def body(v_ref, qg_ref, kd_ref, e_ref, s_ref, do_ref, dqg_ref, dkd_ref, dv_ref, de_ref, state):
        rev = pl.program_id(0) == 1

        @pl.when(pl.program_id(2) == 0)
        def _():
            state[...] = jnp.zeros_like(state)

        def step(i, dsts):
            ci = jnp.where(rev, i, nb - 1 - i)
            r = _rows(ci)
            out = []
            for hh, dst in enumerate(dsts):
                ck = slice(hh * DKB, (hh + 1) * DKB)
                cv = slice(hh * DVB, (hh + 1) * DVB)
                _, vjp = jax.vjp(_gla_scan, qg_ref[r, ck], kd_ref[r, ck], v_ref[r, cv], e_ref[hh, ci][0:1],
                                 s_ref[hh, ci])
                dqg, dkd, dv, de, dstn = vjp((do_ref[r, cv], dst))
                dqg_ref[r, ck] = dqg
                dkd_ref[r, ck] = dkd
                dv_ref[r, cv] = dv
                de_ref[hh, ci] = _first_row(de)
                out.append(dstn)
            return tuple(out)

        dsts = lax.fori_loop(0, nb, step, tuple(state[hh] for hh in range(GLA_HB)))
        for hh, dst in enumerate(dsts):
            state[hh] = dst

    return pl.pallas_call(
        body, name="gla_scan_bwd", grid=grid,
        in_specs=[_gla_v_spec(t, -1), tokk, tokk, per(8, 128), per(DVB, DKB), shared(DVB)],
        out_specs=[tokk, tokk, tokv, per(8, 128)],
        out_shape=[jax.ShapeDtypeStruct((2, t, NB * DKB), F32), jax.ShapeDtypeStruct((2, t, NB * DKB), F32),
                   jax.ShapeDtypeStruct((2, t, D), F32), jax.ShapeDtypeStruct((2, NB, t // CH, 8, 128), F32)],
        scratch_shapes=[pltpu.VMEM((GLA_HB, DVB, DKB), F32)],
        compiler_params=_cparams(("parallel", "parallel", "arbitrary")),
    )(p, qg, kd, e, ssave, do)


def _gla_intra_bwd(p, gk, dqg, dkd, dvs, de, do):
    t = p.shape[0]
    n = t // CH

    def body(q_ref, k_ref, v_ref, g_ref, dqg_ref, dkd_ref, dvs_ref, de_ref, do_ref,
             dq_ref, dk_ref, dv_ref, dg_ref):
        d = pl.program_id(1)
        cs = _Consts(d == 1)

        @pl.when(d == 0)
        def _():
            dq_ref[...] = jnp.zeros_like(dq_ref)
            dk_ref[...] = jnp.zeros_like(dk_ref)
            dv_ref[...] = jnp.zeros_like(dv_ref)

        def step(i, carry):
            r = _srows(i)
            f = lambda q, k, v, g: _gla_intra(q, k, v, g, cs)
            _, vjp = jax.vjp(f, q_ref[r, :], k_ref[r, :], v_ref[r, :], g_ref[r, :])
            dq, dk, dv, dg = vjp((dqg_ref[r, :], dkd_ref[r, :], do_ref[r, :], _chunk_rows(de_ref, i)))
            dq_ref[r, :] += dq
            dk_ref[r, :] += dk
            dv_ref[r, :] += dv + dvs_ref[r, :]
            dg_ref[r, :] = dg
            return carry

        lax.fori_loop(0, t // SC, step, 0)

    q, k, v, dk, dv, e_s, _ = _gla_specs(t, "hd")
    hk = pl.BlockSpec((t, DKB), lambda h, d: (0, h))
    hv = pl.BlockSpec((t, DVB), lambda h, d: (0, h))
    return pl.pallas_call(
        body, name="gla_intra_bwd", grid=(NB, 2),
        in_specs=[q, k, v, dk, dk, dk, dv, e_s, hv],
        out_specs=[hk, hk, hv, dk],
        out_shape=[jax.ShapeDtypeStruct((t, NB * DKB), F32), jax.ShapeDtypeStruct((t, NB * DKB), F32),
                   jax.ShapeDtypeStruct((t, D), F32), jax.ShapeDtypeStruct((2, t, NB * DKB), F32)],
        compiler_params=_cparams(("arbitrary", "arbitrary")),
    )(p, p, p, gk, dqg, dkd, dvs, de, do)


def _seg_gate(o, z, w):
    return _rms(o, w) * _silu(z)


def _seg_merge(ya, yb, ga, gb):
    return _sigmoid(ga) * ya + _sigmoid(gb) * yb


def _seg_loss(out, x, tgt, w):
    err = x + _rms(out, w) - tgt
    return 0.5 * jnp.sum(jnp.mean(err * err, axis=-1, keepdims=True), axis=0, keepdims=True)


def _post(oa2, ob2, p, x, tgt, gdn_w, gla_w, lnpost, w3, tm=128):
    t = x.shape[0]

    def body(oa_ref, ob_ref, z_ref, gb_ref, ga_ref, gB_ref, x_ref, t_ref, aw_ref, bw_ref, lw_ref, w_ref,
             loss_ref, doa_ref, dob_ref, dz_ref, dgb_ref, dga_ref, dgB_ref, dy_ref,
             dw_ref, daw_ref, dbw_ref, dlw_ref):
        first = pl.program_id(0) == 0
        oa = oa_ref[0] + oa_ref[1]
        ob = ob_ref[0] + ob_ref[1]
        z, gb = z_ref[...], gb_ref[...]
        aw, bw = aw_ref[...], bw_ref[...]
        rs = D // NSHARD

        def mat(a, m):
            return sum(jnp.dot(a[:, s * rs:(s + 1) * rs], w_ref[s, m], preferred_element_type=F32)
                       for s in range(NSHARD))

        def mat_t(g, m):
            return jnp.concatenate([_bdot(g, w_ref[s, m], 1, 1) for s in range(NSHARD)], axis=1)

        def add_dw(a, g, m):
            for s in range(NSHARD):
                dw_ref[s, m] += _bdot(a[:, s * rs:(s + 1) * rs], g, 0, 0)

        pa = [jax.vjp(_seg_gate, oa[:, h * DA:(h + 1) * DA], z[:, h * DA:(h + 1) * DA], aw) for h in range(NA)]
        pb = [jax.vjp(_seg_gate, ob[:, h * DVB:(h + 1) * DVB], gb[:, h * DVB:(h + 1) * DVB], bw)
              for h in range(NB)]
        a1 = jnp.concatenate([v for v, _ in pa], axis=1).astype(BF16)
        a2 = jnp.concatenate([v for v, _ in pb], axis=1).astype(BF16)
        ya = mat(a1, 0)
        yb = mat(a2, 1)
        merged, vjp_m = jax.vjp(_seg_merge, ya, yb, ga_ref[...], gB_ref[...])
        mb = merged.astype(BF16)
        out = mat(mb, 2)
        loss, vjp_l = jax.vjp(_seg_loss, out, x_ref[...], t_ref[...], lw_ref[...])
        dout, dyres, _, dlw = vjp_l(jnp.ones((1, 1), F32))
        dy_ref[...] = dyres
        doutb = dout.astype(BF16)
        dmerged = mat_t(doutb, 2)
        dya, dyb, dga, dgB = vjp_m(dmerged)
        dga_ref[...] = dga.astype(BF16)
        dgB_ref[...] = dgB.astype(BF16)
        dyab, dybb = dya.astype(BF16), dyb.astype(BF16)
        da1 = mat_t(dyab, 0)
        da2 = mat_t(dybb, 1)

        daw = jnp.zeros_like(aw)
        for h in range(NA):
            sl = slice(h * DA, (h + 1) * DA)
            do, dz, dw = pa[h][1](da1[:, sl])
            doa_ref[:, sl] = do
            dz_ref[:, sl] = dz.astype(BF16)
            daw = daw + dw
        dbw = jnp.zeros_like(bw)
        for h in range(NB):
            sl = slice(h * DVB, (h + 1) * DVB)
            do, dg, dw = pb[h][1](da2[:, sl])
            dob_ref[:, sl] = do
            dgb_ref[:, sl] = dg.astype(BF16)
            dbw = dbw + dw

        @pl.when(first)
        def _():
            loss_ref[...] = jnp.zeros_like(loss_ref)
            dw_ref[...] = jnp.zeros_like(dw_ref)
            daw_ref[...] = jnp.zeros_like(daw_ref)
            dbw_ref[...] = jnp.zeros_like(dbw_ref)
            dlw_ref[...] = jnp.zeros_like(dlw_ref)

        loss_ref[...] += jnp.broadcast_to(loss, loss_ref.shape)
        add_dw(a1, dyab, 0)
        add_dw(a2, dybb, 1)
        add_dw(mb, doutb, 2)
        daw_ref[...] += jnp.broadcast_to(daw, daw_ref.shape)
        dbw_ref[...] += jnp.broadcast_to(dbw, dbw_ref.shape)
        dlw_ref[...] += jnp.broadcast_to(dlw, dlw_ref.shape)

    two = pl.BlockSpec((2, tm, D), lambda i: (0, i, 0))
    pcol = lambda c: pl.BlockSpec((tm, D), lambda i: (i, c))
    tok = pl.BlockSpec((tm, D), lambda i: (i, 0))
    row = lambda n: pl.BlockSpec((1, n), lambda i: (0, 0))
    row8 = lambda n: pl.BlockSpec((8, n), lambda i: (0, 0))
    once = pl.Buffered(1)
    tokf = jax.ShapeDtypeStruct((t, D), F32)
    tokb = jax.ShapeDtypeStruct((t, D), BF16)
    wspec = pl.BlockSpec((NSHARD, 3, D // NSHARD, D), lambda i: (0, 0, 0, 0), pipeline_mode=once)
    return pl.pallas_call(
        body, name="post", grid=(t // tm,),
        in_specs=[two, two, pcol(3), pcol(6), pcol(7), pcol(8), tok, tok, row(DA), row(DVB), row(D), wspec],
        out_specs=[row8(128), tok, tok, tok, tok, tok, tok, tok, wspec, row8(DA), row8(DVB), row8(D)],
        out_shape=[jax.ShapeDtypeStruct((8, 128), F32), tokf, tokf, tokb, tokb, tokb, tokb, tokf,
                   jax.ShapeDtypeStruct((NSHARD, 3, D // NSHARD, D), F32),
                   jax.ShapeDtypeStruct((8, DA), F32), jax.ShapeDtypeStruct((8, DVB), F32),
                   jax.ShapeDtypeStruct((8, D), F32)],
        compiler_params=_cparams(("arbitrary",), vmem_mb=56),
    )(oa2, ob2, p, p, p, p, x, tgt, gdn_w, gla_w, lnpost, w3)


def _adam_math(w, g, m, v):
    nm = B1 * m + (1.0 - B1) * g
    nv = B2 * v + (1.0 - B2) * (g * g)
    m_hat = nm / (1.0 - B1 ** STEP)
    v_hat = nv / (1.0 - B2 ** STEP)
    return -LR * (m_hat / (jnp.sqrt(v_hat) + ADAM_EPS) + WD * w), nm, nv


SMALL_SLOTS = (("ln_pre_w", 0, 1024, 0), ("a_log_fwd", 1024, 8, 0), ("a_log_bwd", 1024, 8, 8),
               ("dt_bias_fwd", 1152, 8, 0), ("dt_bias_bwd", 1152, 8, 8), ("gdn_norm_w", 1280, 128, 0),
               ("gk_b2_fwd", 1408, 512, 0), ("gk_b2_bwd", 1920, 512, 0), ("gla_norm_w", 2432, 256, 0),
               ("ln_post_w", 2688, 1024, 0))
SMALL_W = 3840


def _adam_small(gsum, ws, ms, vs):
    nw = len(SMALL_SLOTS)

    def body(g_ref, *refs):
        w_refs, m_refs, v_refs, outs = refs[0:nw], refs[nw:2 * nw], refs[2 * nw:3 * nw], refs[3 * nw:]
        for i, (_, off, n, shift) in enumerate(SMALL_SLOTS):
            slot = g_ref[0:1, off:off + max(n, 128)]
            if shift:
                slot = pltpu.roll(slot, 128 - shift, 1)
            g = slot[:, 0:n]
            d, nm, nv = _adam_math(w_refs[i][...], g, m_refs[i][...], v_refs[i][...])
            for k, val in enumerate((g, d, nm, nv)):
                outs[4 * i + k][...] = val

    vm = pl.BlockSpec(memory_space=pltpu.VMEM)
    res = pl.pallas_call(
        body, name="adam_small", in_specs=[vm] * (1 + 3 * nw), out_specs=[vm] * (4 * nw),
        out_shape=[jax.ShapeDtypeStruct((1, n), F32) for _, _, n, _ in SMALL_SLOTS for _ in range(4)],
    )(gsum, *ws, *ms, *vs)
    return {name: res[4 * i:4 * i + 4] for i, (name, _, _, _) in enumerate(SMALL_SLOTS)}


def _adam(w, mine, got, m, v, tr, tile0=0, name=""):
    rows, cols = w.shape
    nh = mine.shape[0] // tr

    def body(c_ref, w_ref, a_ref, b_ref, m_ref, v_ref, g_ref, d_ref, nm_ref, nv_ref):
        half = (tile0 + pl.program_id(0)) // nh
        g = jnp.where(half == c_ref[0], a_ref[...], b_ref[...])
        d, nm, nv = _adam_math(w_ref[...], g, m_ref[...], v_ref[...])
        g_ref[...] = g
        d_ref[...] = d
        nm_ref[...] = nm
        nv_ref[...] = nv

    blk = pl.BlockSpec((tr, cols), lambda i, cc: (i, 0))
    half = pl.BlockSpec((tr, cols), lambda i, cc: ((tile0 + i) % nh, 0))
    shp = jax.ShapeDtypeStruct((rows, cols), F32)
    return pl.pallas_call(
        body, name=f"adam_{name}{rows}x{cols}",
        grid_spec=pltpu.PrefetchScalarGridSpec(
            num_scalar_prefetch=1, grid=(rows // tr,),
            in_specs=[blk, half, half, blk, blk], out_specs=[blk] * 4),
        out_shape=[shp] * 4,
        compiler_params=_cparams(("parallel",)),
    )(lax.axis_index("c").reshape(1), w, mine, got, m, v)


def _sum_cast(own, got):
    ns, _, r, c = own.shape
    tr = r // 4 if r % 64 == 0 else r

    def body(c_ref, a_ref, b_ref, f_ref, h_ref):
        s = a_ref[...] + b_ref[...]
        f_ref[...] = s
        h_ref[...] = s.astype(BF16)

    return pl.pallas_call(
        body, name=f"sum_cast_{r}x{c}",
        grid_spec=pltpu.PrefetchScalarGridSpec(
            num_scalar_prefetch=1, grid=(ns, r // tr),
            in_specs=[pl.BlockSpec((None, None, tr, c), lambda s, i, cc: (s, cc[0], i, 0)),
                      pl.BlockSpec((None, tr, c), lambda s, i, cc: (s, i, 0))],
            out_specs=[pl.BlockSpec((None, tr, c), lambda s, i, cc: (s, i, 0)),
                       pl.BlockSpec((None, tr, c), lambda s, i, cc: (s, i, 0))]),
        out_shape=[jax.ShapeDtypeStruct((ns, r, c), F32), jax.ShapeDtypeStruct((ns, r, c), BF16)],
        compiler_params=_cparams(("parallel", "parallel")),
    )(lax.axis_index("c").reshape(1), own, got)


def _sum4(mine, got):
    _, r, c = mine.shape
    tr = r // 4 if r % 64 == 0 else r

    def body(s_ref, a_ref, g_ref, o_ref):
        acc = a_ref[...] + g_ref[0].astype(F32)
        acc = acc + g_ref[1].astype(F32)
        o_ref[...] = acc + g_ref[2].astype(F32)

    shard = (2 * lax.axis_index("x") + lax.axis_index("y")).reshape(1)
    return pl.pallas_call(
        body, name=f"sum4_{r}x{c}",
        grid_spec=pltpu.PrefetchScalarGridSpec(
            num_scalar_prefetch=1, grid=(r // tr,),
            in_specs=[pl.BlockSpec((None, tr, c), lambda i, ss: (ss[0], i, 0)),
                      pl.BlockSpec((3, tr, c), lambda i, ss: (0, i, 0))],
            out_specs=pl.BlockSpec((tr, c), lambda i, ss: (i, 0))),
        out_shape=jax.ShapeDtypeStruct((r, c), F32),
        compiler_params=_cparams(("parallel",)),
    )(shard, mine, got)


def _place():
    x, y, c = lax.axis_index("x"), lax.axis_index("y"), lax.axis_index("c")
    chips = [(1 - x, y), (x, 1 - y), (1 - x, 1 - y)]
    return x, y, c, chips


def _gather_weights(parts):
    npart = len(parts)

    def body(*refs):
        ins, outs = refs[:npart], refs[npart:2 * npart]
        send_sems, recv_sems = refs[2 * npart:]
        x, y, c, chips = _place()
        sibling = (x, y, 1 - c)
        mine = 2 * x + y

        def remote(k, p, shard, half, to, src=None):
            dst = outs[p].at[shard, half]
            return pltpu.make_async_remote_copy(
                src_ref=dst if src is None else src, dst_ref=dst,
                send_sem=send_sems.at[k], recv_sem=recv_sems.at[k], device_id=to, device_id_type=MESH)

        first = [remote(j * npart + p, p, mine, c, (*chip, c), src=ins[p].at[c])
                 for j, chip in enumerate(chips) for p in range(npart)]
        for cp in first:
            cp.start()
        passed = []
        for j, (cx, cy) in enumerate(chips):
            for p in range(npart):
                remote(j * npart + p, p, 2 * cx + cy, c, (x, y, c)).wait_recv()
                fw = remote((3 + j) * npart + p, p, 2 * cx + cy, c, sibling)
                fw.start()
                passed.append(fw)
        for j, (cx, cy) in enumerate(chips):
            for p in range(npart):
                remote((3 + j) * npart + p, p, 2 * cx + cy, 1 - c, (x, y, c)).wait_recv()
        for cp in first + passed:
            cp.wait_send()

    got = pl.pallas_call(
        body, name="gather_weights",
        in_specs=[ANY] * npart, out_specs=[ANY] * npart,
        out_shape=[jax.ShapeDtypeStruct((NSHARD,) + a.shape, a.dtype) for a in parts],
        scratch_shapes=[pltpu.SemaphoreType.DMA((6 * npart,)), pltpu.SemaphoreType.DMA((6 * npart,))],
    )(*parts)
    mine = 2 * lax.axis_index("x") + lax.axis_index("y")
    return [lax.dynamic_update_index_in_dim(g, a, mine, 0) for g, a in zip(got, parts)]


def _swap_halves(parts, tag=""):
    npart = len(parts)

    def body(*refs):
        ins, outs = refs[:npart], refs[npart:2 * npart]
        send_sems, recv_sems = refs[2 * npart:]
        x, y, c, _ = _place()
        cps = [pltpu.make_async_remote_copy(
            src_ref=ins[p].at[s, 1 - c], dst_ref=outs[p].at[s],
            send_sem=send_sems.at[s * npart + p], recv_sem=recv_sems.at[s * npart + p],
            device_id=(x, y, 1 - c), device_id_type=MESH) for s in range(NSHARD) for p in range(npart)]
        for cp in cps:
            cp.start()
        for cp in cps:
            cp.wait()

    return pl.pallas_call(
        body, name="swap_halves" + tag, in_specs=[ANY] * npart, out_specs=[ANY] * npart,
        out_shape=[jax.ShapeDtypeStruct((NSHARD,) + a.shape[2:], a.dtype) for a in parts],
        scratch_shapes=[pltpu.SemaphoreType.DMA((NSHARD * npart,)), pltpu.SemaphoreType.DMA((NSHARD * npart,))],
    )(*parts)


def _scatter_shards(parts):
    npart = len(parts)

    def body(*refs):
        ins, outs = refs[:npart], refs[npart:2 * npart]
        send_sems, recv_sems = refs[2 * npart:]
        x, y, c, chips = _place()
        cps = [pltpu.make_async_remote_copy(
            src_ref=ins[p].at[2 * cx + cy], dst_ref=outs[p].at[j],
            send_sem=send_sems.at[j * npart + p], recv_sem=recv_sems.at[j * npart + p],
            device_id=(cx, cy, c), device_id_type=MESH)
            for j, (cx, cy) in enumerate(chips) for p in range(npart)]
        for cp in cps:
            cp.start()
        for cp in cps:
            cp.wait()

    return pl.pallas_call(
        body, name="scatter_shards", in_specs=[ANY] * npart, out_specs=[ANY] * npart,
        out_shape=[jax.ShapeDtypeStruct((3,) + a.shape[1:], a.dtype) for a in parts],
        scratch_shapes=[pltpu.SemaphoreType.DMA((3 * npart,)), pltpu.SemaphoreType.DMA((3 * npart,))],
    )(*parts)


HBM = pl.BlockSpec(memory_space=pltpu.HBM)
SEM = pl.BlockSpec(memory_space=pltpu.SEMAPHORE)
EFFECT = pltpu.SideEffectType.DATAFLOW_SIDE_EFFECTING


def _scatter_copies(srcs, lands, send_sems, recv_sems, waiting):
    x, y, c, chips = _place()
    n = len(srcs)
    return [pltpu.make_async_remote_copy(
        src_ref=srcs[p].at[2 * cx + cy], dst_ref=lands[p].at[j],
        send_sem=send_sems.at[j * n + p], recv_sem=recv_sems.at[j * n + p],
        device_id=(cx, cy, c), device_id_type=MESH)
        for j, (cx, cy) in enumerate(chips) for p in range(n)]


def _proj_copies(srcs, lands, send_sems, recv_sems, waiting):
    x, y, c, chips = _place()
    mine = 2 * x + y
    return [pltpu.make_async_remote_copy(
        src_ref=srcs[0].at[c], dst_ref=lands[0].at[mine, c],
        send_sem=send_sems.at[2 * j + to], recv_sem=recv_sems.at[2 * j + (to if waiting else c)],
        device_id=(cx, cy, to), device_id_type=MESH)
        for j, (cx, cy) in enumerate(chips) for to in range(2)]


def _start_copies(copies, nsem, parts, lands, name, after=None):
    n = len(parts)
    extra = [] if after is None else [after]

    def body(*refs):
        outs = refs[2 * n + len(extra):]
        for cp in copies(refs[:n], refs[n:2 * n], outs[0], outs[1], False):
            cp.start()
        outs[-1][...] = jnp.zeros_like(outs[-1])

    res = pl.pallas_call(
        body, name=name,
        out_shape=(pltpu.SemaphoreType.DMA((nsem,)), pltpu.SemaphoreType.DMA((nsem,)),
                   *[pltpu.HBM(a.shape, a.dtype) for a in parts], *[pltpu.HBM(a.shape, a.dtype) for a in lands],
                   jax.ShapeDtypeStruct((8, 128), F32)),
        in_specs=[HBM] * (2 * n) + [ANY] * len(extra),
        out_specs=(SEM, SEM, *[HBM] * (2 * n), pl.BlockSpec(memory_space=pltpu.VMEM)),
        input_output_aliases={i: 2 + i for i in range(2 * n)},
        compiler_params=pltpu.CompilerParams(has_side_effects=EFFECT),
    )(*[pltpu.with_memory_space_constraint(a, pltpu.HBM) for a in parts],
      *[pltpu.with_memory_space_constraint(a, pltpu.HBM) for a in lands], *extra)
    return res[0], res[1], res[2:2 + n], res[2 + n:2 + 2 * n], res[-1]


def _wait_copies(copies, started, after, name):
    send_sems, recv_sems, srcs, lands, _ = started
    n = len(srcs)

    def body(*refs):
        for cp in copies(refs[:n], refs[n:2 * n], refs[2 * n], refs[2 * n + 1], True):
            cp.wait_send()
            cp.wait_recv()

    res = pl.pallas_call(
        body, name=name,
        out_shape=tuple(pltpu.HBM(a.shape, a.dtype) for a in (*srcs, *lands)),
        in_specs=[HBM] * (2 * n) + [SEM, SEM, ANY], out_specs=tuple([HBM] * (2 * n)),
        input_output_aliases={i: i for i in range(2 * n)},
        compiler_params=pltpu.CompilerParams(has_side_effects=EFFECT),
    )(*srcs, *lands, send_sems, recv_sems, after)
    return res[n:]


def _join_halves(parts):
    npart = len(parts)

    def body(*refs):
        ins, outs = refs[:npart], refs[npart:2 * npart]
        send_sems, recv_sems = refs[2 * npart:]
        x, y, c, _ = _place()
        cps = [pltpu.make_async_remote_copy(
            src_ref=ins[p], dst_ref=outs[p], send_sem=send_sems.at[p], recv_sem=recv_sems.at[p],
            device_id=(x, y, 1 - c), device_id_type=MESH) for p in range(npart)]
        for cp in cps:
            cp.start()
        for cp in cps:
            cp.wait()

    return pl.pallas_call(
        body, name="join_halves", in_specs=[ANY] * npart, out_specs=[ANY] * npart,
        out_shape=[jax.ShapeDtypeStruct(a.shape, a.dtype) for a in parts],
        scratch_shapes=[pltpu.SemaphoreType.DMA((npart,)), pltpu.SemaphoreType.DMA((npart,))],
    )(*parts)


def _allreduce_small(v):
    r, ncol = v.shape

    def body(v_ref, o_ref, buf, send_sems, recv_sems):
        x, y, c, _ = _place()
        me = 4 * x + 2 * y + c
        buf[me] = v_ref[...]
        cps = []
        for k in range(1, 8):
            px, py, pc = x ^ (k >> 2), y ^ ((k >> 1) & 1), c ^ (k & 1)
            cps.append(pltpu.make_async_remote_copy(
                src_ref=v_ref, dst_ref=buf.at[me], send_sem=send_sems.at[k - 1], recv_sem=recv_sems.at[k - 1],
                device_id=(px, py, pc), device_id_type=MESH))
        for cp in cps:
            cp.start()
        for k in range(1, 8):
            px, py, pc = x ^ (k >> 2), y ^ ((k >> 1) & 1), c ^ (k & 1)
            pltpu.make_async_remote_copy(
                src_ref=v_ref, dst_ref=buf.at[4 * px + 2 * py + pc], send_sem=send_sems.at[k - 1],
                recv_sem=recv_sems.at[k - 1], device_id=(px, py, pc), device_id_type=MESH).wait_recv()
        for cp in cps:
            cp.wait_send()
        acc = buf[0]
        for d in range(1, 8):
            acc = acc + buf[d]
        o_ref[...] = acc

    return pl.pallas_call(
        body, name="allreduce_small",
        in_specs=[pl.BlockSpec(memory_space=pltpu.VMEM)], out_specs=pl.BlockSpec(memory_space=pltpu.VMEM),
        out_shape=jax.ShapeDtypeStruct((r, ncol), F32),
        scratch_shapes=[pltpu.VMEM((8, r, ncol), F32), pltpu.SemaphoreType.DMA((7,)), pltpu.SemaphoreType.DMA((7,))],
    )(v)


def _permute_rows(shards):
    w0, w1, w2, w3 = shards
    zeros = jnp.zeros((NPERM - 9280, w0.shape[1]), w0.dtype)
    return jnp.concatenate([w0, w1[0:1776], w1[1808:2320], w2, w3[0:240], w3[272:2320],
                            w1[1776:1808], w3[240:272], zeros], axis=0)


def _unpermute_rows(g):
    s1 = jnp.concatenate([g[2320:4096], g[9216:9248], g[4096:4608]], axis=0)
    s3 = jnp.concatenate([g[6928:7168], g[9248:9280], g[7168:9216]], axis=0)
    return jnp.stack([g[0:2320], s1, g[4608:6928], s3], axis=0)


def _pack_shard_small(conv, w2f, w2b):
    top = jnp.pad(conv, ((0, 8 - conv.shape[0]), (0, 0)))
    mid = jnp.pad(jnp.concatenate([w2f, w2b], axis=1), ((0, 0), (0, 768 - 256)))
    return jnp.concatenate([top, mid, jnp.zeros((8, 768), conv.dtype)], axis=0)


def _unpack_shard_small(a):
    return a[0:5], a[8:24, 0:128], a[8:24, 128:256]


def kernel(x, ln_pre_w, w_in, conv_w, a_log_fwd, a_log_bwd, dt_bias_fwd, dt_bias_bwd, gdn_norm_w, w_proj_gdn, gk_w2_fwd, gk_b2_fwd, gk_w2_bwd, gk_b2_bwd, gla_norm_w, w_proj_gla, w_out, ln_post_w, loss_target, m_ln_pre_w, m_w_in, m_conv_w, m_a_log_fwd, m_a_log_bwd, m_dt_bias_fwd, m_dt_bias_bwd, m_gdn_norm_w, m_w_proj_gdn, m_gk_w2_fwd, m_gk_b2_fwd, m_gk_w2_bwd, m_gk_b2_bwd, m_gla_norm_w, m_w_proj_gla, m_w_out, m_ln_post_w, v_ln_pre_w, v_w_in, v_conv_w, v_a_log_fwd, v_a_log_bwd, v_dt_bias_fwd, v_dt_bias_bwd, v_gdn_norm_w, v_w_proj_gdn, v_gk_w2_fwd, v_gk_b2_fwd, v_gk_w2_bwd, v_gk_b2_bwd, v_gla_norm_w, v_w_proj_gla, v_w_out, v_ln_post_w):
    t = x.shape[1]
    x2, tgt = x[0], loss_target[0]

    win_l = w_in[0].T.astype(BF16).reshape(2, SHW // 2, D)
    proj_l = jnp.concatenate([w_proj_gdn[0], w_proj_gla[0], w_out[0]], axis=0).astype(BF16).reshape(2, 384, D)
    small_l = _pack_shard_small(conv_w[0], gk_w2_fwd[0], gk_w2_bwd[0]).reshape(2, 16, 768)
    win_g, small_g = _gather_weights([win_l, small_l])
    proj_started = _start_copies(_proj_copies, 6, [proj_l], [lax.empty((NSHARD, 2, 384, D), BF16)],
                                 "gather_proj_start", after=small_g)
    wperm = _permute_rows(win_g.reshape(NSHARD, SHW, D))
    small_g = small_g.reshape(NSHARD, 32, 768)
    convw = small_g[:, 0:8, :].transpose(1, 0, 2).reshape(8, 3 * D)
    w2f = small_g[:, 8:24, 0:128].transpose(1, 0, 2).reshape(16, 512)
    w2b = small_g[:, 8:24, 128:256].transpose(1, 0, 2).reshape(16, 512)
    w2f_pad = jnp.pad(w2f, ((32, 80), (0, 0)))
    w2b_pad = jnp.pad(w2b, ((48, 64), (0, 0)))
    alog_row = jnp.pad(jnp.concatenate([a_log_fwd, a_log_bwd], axis=1), ((0, 0), (0, 112)))
    dt_row = jnp.pad(jnp.concatenate([dt_bias_fwd, dt_bias_bwd], axis=1), ((0, 0), (0, 112)))

    p, h = _inproj(x2, ln_pre_w + proj_started[4][0:1, 0:1], wperm)
    qn, kn, vc = (_qkv_fwd(p, convw, kind) for kind in range(3))
    gsm, gk = _gates_fwd(p, alog_row, dt_row, w2f_pad, gk_b2_fwd, w2b_pad, gk_b2_bwd)
    g2, b2 = _gcum_fwd(gsm)
    u, w, at, qd, kd, el, tinv = _gdn_intra_fwd(qn, kn, vc, g2, b2)
    oa2, sa = _gdn_scan_fwd(u, w, at, qd, kd, el)
    qg, kdb, intra, elb = _gla_intra_fwd(p, gk)
    ob2, sb = _gla_scan_fwd(p, qg, kdb, intra, elb)

    (proj_land,) = _wait_copies(_proj_copies, proj_started, ob2, "gather_proj_wait")
    mine = 2 * lax.axis_index("x") + lax.axis_index("y")
    w3 = lax.dynamic_update_index_in_dim(proj_land, proj_l, mine, 0).reshape(NSHARD, 3, D // NSHARD, D)
    (loss8, doa, dob, dz, dgb, dga, dgB, dyres, dw3, dgdn_w, dgla_w, dlnpost) = _post(
        oa2, ob2, p, x2, tgt, gdn_norm_w, gla_norm_w, ln_post_w, w3)

    g_proj = dw3.reshape(NSHARD, 2, 384, D)
    sum_proj = _sum_cast(g_proj, _swap_halves([g_proj], "_proj")[0])
    started_proj = _start_copies(_scatter_copies, 3, [sum_proj[1]], [lax.empty((3, 384, D), BF16)],
                                 "scatter_proj_start")
    du, dw, dat, dqd, dkd, del_ = _gdn_scan_bwd(u, w, at, qd, kd, el + started_proj[4][0, 0], sa, doa)
    dqn, dkn, dvc, dg2, db2 = _gdn_intra_bwd(qn, kn, vc, g2, b2, tinv, du, dw, dat, dqd, dkd, del_)
    dgsm = _gcum_bwd(gsm, dg2, db2)
    dqg, dkdb, dvs, delb = _gla_scan_bwd(p, qg, kdb, elb, sb, dob)
    dqb, dkb, dvb, dgk = _gla_intra_bwd(p, gk, dqg, dkdb, dvs, delb, dob)
    (dps, dalog8, ddt8, dw2f_pad, db2f8, dw2b_pad, db2b8) = _gates_bwd(
        p, alog_row, dt_row, w2f_pad, gk_b2_fwd, w2b_pad, gk_b2_bwd, dgsm, dgk)
    dpre, dconv = zip(*[_qkv_bwd(p, convw, g, kind) for kind, g in enumerate((dqn, dkn, dvc))])

    pieces = (jnp.concatenate([a.astype(BF16) for a in (*dpre, dz, dqb, dkb, dvb, dgb, dga, dgB, dps)], axis=1),)
    dwperm = _inproj_dw(h, pieces)

    g_in = _unpermute_rows(dwperm).reshape(NSHARD, 2, SHW // 2, D)
    dconv_full = jnp.concatenate(dconv, axis=1)
    dw2f, dw2b = dw2f_pad[32:48], dw2b_pad[48:64]
    g_small = jnp.stack([_pack_shard_small(dconv_full[0:5, 768 * s:768 * (s + 1)],
                                           dw2f[:, 128 * s:128 * (s + 1)], dw2b[:, 128 * s:128 * (s + 1)])
                         for s in range(NSHARD)])
    g_small = g_small.reshape(NSHARD, 2, 16, 768)
    parts = [g_in, g_small]
    got = _swap_halves(parts)
    sums = [_sum_cast(a, b) for a, b in zip(parts, got)]
    hbs = [hb for _, hb in sums]
    started = _start_copies(_scatter_copies, 3 * len(hbs), hbs,
                            [lax.empty((3,) + a.shape[1:], a.dtype) for a in hbs], "scatter_start")
    dx, dlnpre8 = _inproj_dx(pieces, wperm, x2, ln_pre_w + started[4][0:1, 0:1], dyres)

    gsmall = _allreduce_small(jnp.concatenate(
        [dlnpre8, dalog8, ddt8, dgdn_w, db2f8, db2b8, dgla_w, dlnpost, loss8], axis=1))
    smalls = dict(ln_pre_w=(ln_pre_w, m_ln_pre_w, v_ln_pre_w), a_log_fwd=(a_log_fwd, m_a_log_fwd, v_a_log_fwd),
                  a_log_bwd=(a_log_bwd, m_a_log_bwd, v_a_log_bwd),
                  dt_bias_fwd=(dt_bias_fwd, m_dt_bias_fwd, v_dt_bias_fwd),
                  dt_bias_bwd=(dt_bias_bwd, m_dt_bias_bwd, v_dt_bias_bwd),
                  gdn_norm_w=(gdn_norm_w, m_gdn_norm_w, v_gdn_norm_w),
                  gk_b2_fwd=(gk_b2_fwd, m_gk_b2_fwd, v_gk_b2_fwd), gk_b2_bwd=(gk_b2_bwd, m_gk_b2_bwd, v_gk_b2_bwd),
                  gla_norm_w=(gla_norm_w, m_gla_norm_w, v_gla_norm_w), ln_post_w=(ln_post_w, m_ln_post_w, v_ln_post_w))
    names = [name for name, _, _, _ in SMALL_SLOTS]
    small = _adam_small(gsmall, *([smalls[n][i] for n in names] for i in range(3)))

    landed_proj = _wait_copies(_scatter_copies, started_proj, small["ln_pre_w"][1], "scatter_proj_wait")
    landed = _wait_copies(_scatter_copies, started, small["ln_pre_w"][1], "scatter_wait")
    sums = [sums[0], sum_proj, sums[1]]
    halves = [_sum4(f, g) for (f, _), g in zip(sums, [landed[0], landed_proj[0], landed[1]])]
    theirs = _join_halves(halves)

    a_in = [a.T for a in _adam(w_in[0].T, halves[0], theirs[0], m_w_in[0].T, v_w_in[0].T, 232, name="in")]
    a_pr = [_adam(w[0], halves[1], theirs[1], m[0], v[0], 128, tile0=2 * i, name=f"proj{i}")
            for i, (w, m, v) in enumerate(((w_proj_gdn, m_w_proj_gdn, v_w_proj_gdn),
                                           (w_proj_gla, m_w_proj_gla, v_w_proj_gla), (w_out, m_w_out, v_w_out)))]
    a_ss = _adam(_pack_shard_small(conv_w[0], gk_w2_fwd[0], gk_w2_bwd[0]), halves[2], theirs[2],
                 _pack_shard_small(m_conv_w[0], m_gk_w2_fwd[0], m_gk_w2_bwd[0]),
                 _pack_shard_small(v_conv_w[0], v_gk_w2_fwd[0], v_gk_w2_bwd[0]), 16, name="small")

    def family(k):
        conv, w2f_, w2b_ = _unpack_shard_small(a_ss[k])
        s = {n: small[n][k] for n in names}
        return [s["ln_pre_w"], a_in[k][None], conv[None], s["a_log_fwd"], s["a_log_bwd"], s["dt_bias_fwd"],
                s["dt_bias_bwd"], s["gdn_norm_w"], a_pr[0][k][None], w2f_[None], s["gk_b2_fwd"], w2b_[None],
                s["gk_b2_bwd"], s["gla_norm_w"], a_pr[1][k][None], a_pr[2][k][None], s["ln_post_w"]]

    return (gsmall[0, SMALL_W - 128], dx[None], *family(0), *family(1), *family(2), *family(3))
```

```python
import functools

import jax
import jax.numpy as jnp
from jax import lax
from jax.experimental import pallas as pl
from jax.experimental.pallas import tpu as pltpu

F32 = jnp.float32
BF16 = jnp.bfloat16
HI = lax.Precision.HIGHEST
MESH = pl.DeviceIdType.MESH

D = 1024
CH = 64
EPS = 1e-6
NA, DA = 8, 128
NB, DKB, DVB = 4, 128, 256
NSHARD = 4
SHW = 2320
NPERM = 9728
PS_BLOCK = 72
LR, B1, B2, ADAM_EPS, WD, STEP = 0.001, 0.9, 0.999, 1e-08, 0.01, 10

ANY = pl.BlockSpec(memory_space=pl.ANY)


def _cparams(sem=None, vmem_mb=48):
    return pltpu.CompilerParams(dimension_semantics=sem, vmem_limit_bytes=vmem_mb << 20)


def _bdot(a, b, ca, cb):
    return lax.dot_general(a.astype(BF16), b.astype(BF16), (((ca,), (cb,)), ((), ())),
                           preferred_element_type=F32)


@jax.custom_vjp
def mm(a, b):
    return _bdot(a, b, 1, 0)


def _mm_fwd(a, b):
    return _bdot(a, b, 1, 0), (a, b)


def _mm_bwd(res, g):
    a, b = res
    return _bdot(g, b, 1, 1), _bdot(a, g, 0, 0)


mm.defvjp(_mm_fwd, _mm_bwd)


@jax.custom_vjp
def mm_nt(a, b):
    return _bdot(a, b, 1, 1)


def _mm_nt_fwd(a, b):
    return _bdot(a, b, 1, 1), (a, b)


def _mm_nt_bwd(res, g):
    a, b = res
    return _bdot(g, b, 1, 0), _bdot(g, a, 0, 0)


mm_nt.defvjp(_mm_nt_fwd, _mm_nt_bwd)


@jax.custom_vjp
def mm_tn(a, b):
    return _bdot(a, b, 0, 0)


def _mm_tn_fwd(a, b):
    return _bdot(a, b, 0, 0), (a, b)


def _mm_tn_bwd(res, g):
    a, b = res
    return _bdot(b, g, 1, 1), _bdot(a, g, 1, 0)


mm_tn.defvjp(_mm_tn_fwd, _mm_tn_bwd)


def dot_hi(a, b):
    return lax.dot_general(a, b, (((1,), (0,)), ((), ())), precision=HI, preferred_element_type=F32)


def _split3(x):
    x1 = x.astype(BF16)
    r = x - x1.astype(F32)
    x2 = r.astype(BF16)
    return x1, x2, (r - x2.astype(F32)).astype(BF16)


def _cdot(c, x, cc, cx, c_first=True):
    parts = _split3(x)
    if c_first:
        return _bdot(c, parts[0], cc, cx) + _bdot(c, parts[1], cc, cx) + _bdot(c, parts[2], cc, cx)
    return _bdot(parts[0], c, cx, cc) + _bdot(parts[1], c, cx, cc) + _bdot(parts[2], c, cx, cc)


@jax.custom_vjp
def cmm(c, x):
    return _cdot(c, x, 1, 0)


def _cmm_fwd(c, x):
    return _cdot(c, x, 1, 0), c


def _cmm_bwd(c, g):
    return jnp.zeros_like(c), _cdot(c, g, 0, 0)


cmm.defvjp(_cmm_fwd, _cmm_bwd)


@jax.custom_vjp
def mmc(x, c):
    return _cdot(c, x, 0, 1, c_first=False)


def _mmc_fwd(x, c):
    return _cdot(c, x, 0, 1, c_first=False), c


def _mmc_bwd(c, g):
    return _cdot(c, g, 1, 1, c_first=False), jnp.zeros_like(c)


mmc.defvjp(_mmc_fwd, _mmc_bwd)


def _sigmoid(x):
    return 1.0 / (1.0 + jnp.exp(-x))


def _silu(x):
    return x * _sigmoid(x)


def _softplus(x):
    return jnp.maximum(x, 0.0) + jnp.log(1.0 + jnp.exp(-jnp.abs(x)))


def _rms(x, w):
    return x * lax.rsqrt(jnp.mean(x * x, axis=-1, keepdims=True) + EPS) * w


SC = 256


class _Consts:
    def __init__(self, rev):
        r = lax.broadcasted_iota(jnp.int32, (SC, SC), 0)
        c = lax.broadcasted_iota(jnp.int32, (SC, SC), 1)
        same = (r >> 6) == (c >> 6)
        a = jnp.where(rev, c, r)
        b = jnp.where(rev, r, c)
        self.incl = same & (a >= b)
        self.strict = same & (a > b)
        self.incl_f = self.incl.astype(F32)
        self.eye = (r == c).astype(F32)
        rows = lax.broadcasted_iota(jnp.int32, (SC, 1), 0)
        self.last_col = ((rows & (CH - 1)) == jnp.where(rev, 0, CH - 1)).astype(F32)
        rr = lax.broadcasted_iota(jnp.int32, (SC, CH), 0)
        cc = lax.broadcasted_iota(jnp.int32, (SC, CH), 1)
        self.fold = ((rr & (CH - 1)) == cc).astype(F32)


def _dot3(a, b, ca=1, cb=0):
    ah, bh = a.astype(BF16), b.astype(BF16)
    al, bl = (a - ah.astype(F32)).astype(BF16), (b - bh.astype(F32)).astype(BF16)
    return _bdot(ah, bh, ca, cb) + (_bdot(ah, bl, ca, cb) + _bdot(al, bh, ca, cb))


TRI_SPLIT_LEVELS = 2


def _tri_inv(low, eye):
    n = -low
    acc = eye + n
    p = n
    for level in range(5):
        dot = _dot3 if level < TRI_SPLIT_LEVELS else (lambda a, b: _bdot(a, b, 1, 0))
        p = dot(p, p)
        acc = acc + dot(acc, p)
    return acc


@jax.custom_vjp
def _solve2(low, rv, rk, tinv):
    x = _dot3(tinv, jnp.concatenate([rv, rk], axis=1))
    return x[:, :DA], x[:, DA:]


def _solve2_fwd(low, rv, rk, tinv):
    x = _dot3(tinv, jnp.concatenate([rv, rk], axis=1))
    return (x[:, :DA], x[:, DA:]), (x, tinv)


def _solve2_bwd(res, g):
    x, tinv = res
    drhs = _dot3(tinv, jnp.concatenate(g, axis=1), 0, 0)
    return -_dot3(drhs, x, 1, 1), drhs[:, :DA], drhs[:, DA:], jnp.zeros_like(tinv)


_solve2.defvjp(_solve2_fwd, _solve2_bwd)


def _chunk_last(x, cs):
    xs = (x * cs.last_col).reshape(SC // CH, CH, x.shape[1])
    return jnp.broadcast_to(jnp.sum(xs, axis=1, keepdims=True), xs.shape).reshape(x.shape)


def _gdn_decay(g, cs):
    gw = jnp.concatenate([g] * (SC // DA), axis=1)
    grow = jnp.sum(cs.eye * gw, axis=0, keepdims=True)
    return jnp.where(cs.incl, jnp.exp(jnp.where(cs.incl, gw - grow, 0.0)), 0.0)


def _gdn_intra(q, k, v, g, bx, tinv, cs):
    decay = _gdn_decay(g, cs)
    kb = k * bx
    low = jnp.where(cs.strict, mm_nt(kb, k) * decay, 0.0)
    eg = jnp.exp(g)
    made = tinv is None
    if made:
        tinv = _tri_inv(low, cs.eye)
    u, w = _solve2(low, v * bx, kb * eg, tinv)
    attn = mmc(mm_nt(q, k) * decay, cs.fold)
    qd = q * eg
    glast = _chunk_last(g, cs)
    kd = k * jnp.exp(glast - g)
    outs = (u, w, attn, qd, kd, jnp.exp(glast))
    return outs + (tinv,) if made else outs


def _gdn_scan(u, w, attn, qd, kd, el, s):
    vn = u - mm(w, s)
    o = mm(qd, s) + mm(attn, vn)
    sn = s * el + mm_tn(kd, vn)
    return o, sn


def _gla_intra(q, k, v, gk, cs):
    gc = cmm(cs.incl_f, gk)
    qg = q * (DKB ** -0.5) * jnp.exp(gc)
    kg = k * jnp.exp(-gc)
    attn = jnp.where(cs.incl, mm_nt(qg, kg), 0.0)
    intra = mm(attn, v)
    glast = _chunk_last(gc, cs)
    kd = k * jnp.exp(glast - gc)
    return qg, kd, intra, jnp.exp(glast)


def _gla_scan(qg, kd, v, el, st):
    o = mm_nt(qg, st)
    stn = st * el + mm_tn(v, kd)
    return o, stn


def _shift_rows(x, s):
    if s == 0:
        return x
    t = x.shape[0]
    rolled = pltpu.roll(x, (-s) % t, 0)
    rows = lax.broadcasted_iota(jnp.int32, x.shape, 0)
    return jnp.where((rows + s >= 0) & (rows + s < t), rolled, 0.0)


@jax.custom_vjp
def _conv5(x, w):
    acc = w[0:1] * _shift_rows(x, -2)
    for j in range(1, 5):
        acc = acc + w[j:j + 1] * _shift_rows(x, j - 2)
    return acc


def _conv5_fwd(x, w):
    return _conv5(x, w), (x, w)


def _conv5_bwd(res, g):
    x, w = res
    dx = w[0:1] * _shift_rows(g, 2)
    for j in range(1, 5):
        dx = dx + w[j:j + 1] * _shift_rows(g, 2 - j)
    rows = lax.broadcasted_iota(jnp.int32, w.shape, 0)
    dw = jnp.zeros_like(w)
    for j in range(5):
        dwj = jnp.sum(g * _shift_rows(x, j - 2), axis=0, keepdims=True)
        dw = dw + jnp.where(rows == j, dwj, 0.0)
    return dx, dw


_conv5.defvjp(_conv5_fwd, _conv5_bwd)


def _qkv_act(kind):
    def f(x, w):
        c = _silu(_conv5(x, w))
        if kind == 2:
            return c
        c = c * lax.rsqrt(jnp.sum(c * c, axis=-1, keepdims=True) + EPS)
        return c * (DA ** -0.5) if kind == 0 else c
    return f


def _inproj(x, lnw, wperm, tn=512):
    t = x.shape[0]
    tm = min(t, 2048)

    def body(x_ref, lnw_ref, w_ref, p_ref, h_ref, hbuf):
        @pl.when(pl.program_id(1) == 0)
        def _():
            hb = _rms(x_ref[...], lnw_ref[...]).astype(BF16)
            hbuf[...] = hb
            h_ref[...] = hb
        p_ref[...] = _bdot(hbuf[...], w_ref[...], 1, 1)

    return pl.pallas_call(
        body, name="inproj", grid=(t // tm, NPERM // tn),
        in_specs=[pl.BlockSpec((tm, D), lambda i, j: (i, 0)),
                  pl.BlockSpec((1, D), lambda i, j: (0, 0)),
                  pl.BlockSpec((tn, D), lambda i, j: (j, 0))],
        out_specs=[pl.BlockSpec((tm, tn), lambda i, j: (i, j)),
                   pl.BlockSpec((tm, D), lambda i, j: (i, 0))],
        out_shape=[jax.ShapeDtypeStruct((t, NPERM), F32),
                   jax.ShapeDtypeStruct((t, D), BF16)],
        scratch_shapes=[pltpu.VMEM((tm, D), BF16)],
        compiler_params=_cparams(("parallel", "arbitrary")),
    )(x, lnw, wperm)


DP_TILE = 512
DP_PIECES = ((0, 19),)


def _piece_specs(tm, j_first):
    specs = []
    for j0, n in DP_PIECES:
        def imap(a, b, j0=j0, n=n):
            j, i = (a, b) if j_first else (b, a)
            inside = (j >= j0) & (j < j0 + n)
            return jnp.where(inside, i, 0), jnp.clip(j - j0, 0, n - 1)
        specs.append(pl.BlockSpec((tm, DP_TILE), imap))
    return specs


def _for_piece(j, refs, fn):
    for (j0, n), ref in zip(DP_PIECES, refs):
        @pl.when((j >= j0) & (j < j0 + n))
        def _(ref=ref):
            fn(ref[...])


def _inproj_dw(h, pieces):
    t = h.shape[0]
    tm = min(t, 2048)
    npc = len(pieces)

    def body(h_ref, *refs):
        dw_ref = refs[npc]

        @pl.when(pl.program_id(1) == 0)
        def _():
            dw_ref[...] = jnp.zeros_like(dw_ref)

        def add(dp):
            dw_ref[...] += _bdot(dp, h_ref[...], 0, 0)
        _for_piece(pl.program_id(0), refs[:npc], add)

    return pl.pallas_call(
        body, name="inproj_dw", grid=(NPERM // DP_TILE, t // tm),
        in_specs=[pl.BlockSpec((tm, D), lambda j, i: (i, 0))] + _piece_specs(tm, True),
        out_specs=pl.BlockSpec((DP_TILE, D), lambda j, i: (j, 0)),
        out_shape=jax.ShapeDtypeStruct((NPERM, D), F32),
        compiler_params=_cparams(("parallel", "arbitrary")),
    )(h, *pieces)


def _inproj_dx(pieces, wperm, x, lnw, dyres):
    t = x.shape[0]
    tm = min(t, 1024)
    tn = DP_TILE
    nj = NPERM // tn
    npc = len(pieces)

    def body(*refs):
        w_ref, x_ref, lnw_ref, dy_ref, dx_ref, dlnw_ref, acc = refs[npc:]
        j = pl.program_id(1)

        @pl.when(j == 0)
        def _():
            acc[...] = jnp.zeros_like(acc)

        def add(dp):
            acc[...] += _bdot(dp, w_ref[...], 1, 0)
        _for_piece(j, refs[:npc], add)

        @pl.when(j == nj - 1)
        def _():
            _, vjp = jax.vjp(_rms, x_ref[...], lnw_ref[...])
            dx, dlnw = vjp(acc[...])
            dx_ref[...] = dx + dy_ref[...]

            @pl.when(pl.program_id(0) == 0)
            def _():
                dlnw_ref[...] = jnp.zeros_like(dlnw_ref)
            dlnw_ref[...] += jnp.broadcast_to(dlnw, dlnw_ref.shape)

    return pl.pallas_call(
        body, name="inproj_dx", grid=(t // tm, nj),
        in_specs=_piece_specs(tm, False) + [
                  pl.BlockSpec((tn, D), lambda i, j: (j, 0)),
                  pl.BlockSpec((tm, D), lambda i, j: (i, 0)),
                  pl.BlockSpec((1, D), lambda i, j: (0, 0)),
                  pl.BlockSpec((tm, D), lambda i, j: (i, 0))],
        out_specs=[pl.BlockSpec((tm, D), lambda i, j: (i, 0)),
                   pl.BlockSpec((8, D), lambda i, j: (0, 0))],
        out_shape=[jax.ShapeDtypeStruct((t, D), F32), jax.ShapeDtypeStruct((8, D), F32)],
        scratch_shapes=[pltpu.VMEM((tm, D), F32)],
        compiler_params=_cparams(("arbitrary", "arbitrary")),
    )(*pieces, wperm, x, lnw, dyres)


def _qkv_fwd(p, convw, kind):
    t = p.shape[0]
    f = _qkv_act(kind)

    def body(p_ref, w_ref, o_ref):
        o_ref[...] = f(p_ref[...], w_ref[...])

    return pl.pallas_call(
        body, name=f"qkv_fwd{kind}", grid=(NA,),
        in_specs=[pl.BlockSpec((t, DA), lambda h: (0, kind * NA + h)),
                  pl.BlockSpec((8, DA), lambda h: (0, kind * NA + h))],
        out_specs=pl.BlockSpec((t, DA), lambda h: (0, h)),
        out_shape=jax.ShapeDtypeStruct((t, D), F32),
        compiler_params=_cparams(("parallel",)),
    )(p, convw)


def _qkv_bwd(p, convw, dout, kind):
    t = p.shape[0]
    f = _qkv_act(kind)

    def body(p_ref, w_ref, g_ref, dx_ref, dw_ref):
        _, vjp = jax.vjp(f, p_ref[...], w_ref[...])
        dx, dw = vjp(g_ref[...])
        dx_ref[...] = dx.astype(BF16)
        dw_ref[...] = dw

    return pl.pallas_call(
        body, name=f"qkv_bwd{kind}", grid=(NA,),
        in_specs=[pl.BlockSpec((t, DA), lambda h: (0, kind * NA + h)),
                  pl.BlockSpec((8, DA), lambda h: (0, kind * NA + h)),
                  pl.BlockSpec((t, DA), lambda h: (0, h))],
        out_specs=[pl.BlockSpec((t, DA), lambda h: (0, h)),
                   pl.BlockSpec((8, DA), lambda h: (0, h))],
        out_shape=[jax.ShapeDtypeStruct((t, D), BF16), jax.ShapeDtypeStruct((8, D), F32)],
        compiler_params=_cparams(("parallel",)),
    )(p, convw, dout)


def _gates_f(ps, alog_row, dt_row, w2f, b2f, w2b, b2b):
    lane = lax.broadcasted_iota(jnp.int32, ps.shape, 1)
    lg = -jnp.exp(alog_row) * _softplus(ps + dt_row)
    gsm = jnp.where(lane < 16, lg, jnp.where(lane < 32, _sigmoid(ps), 0.0))
    gkf = -_softplus(-(mm(ps, w2f) + b2f)) * (1.0 / 16.0)
    gkb = -_softplus(-(mm(ps, w2b) + b2b)) * (1.0 / 16.0)
    return gsm, gkf, gkb


def _gates_fwd(ps, alog_row, dt_row, w2f, b2f, w2b, b2b, tm=512):
    t = ps.shape[0]

    def body(ps_ref, a_ref, d_ref, wf_ref, bf_ref, wb_ref, bb_ref, gsm_ref, gk_ref):
        gsm, gkf, gkb = _gates_f(ps_ref[...], a_ref[...], d_ref[...], wf_ref[...], bf_ref[...],
                                 wb_ref[...], bb_ref[...])
        gsm_ref[...] = gsm
        gk_ref[0] = gkf
        gk_ref[1] = gkb

    row = lambda n: pl.BlockSpec((1, n), lambda i: (0, 0))
    mat = pl.BlockSpec((128, 512), lambda i: (0, 0))
    return pl.pallas_call(
        body, name="gates_fwd", grid=(t // tm,),
        in_specs=[pl.BlockSpec((tm, 128), lambda i: (i, PS_BLOCK)), row(128), row(128), mat, row(512), mat, row(512)],
        out_specs=[pl.BlockSpec((tm, 128), lambda i: (i, 0)),
                   pl.BlockSpec((2, tm, 512), lambda i: (0, i, 0))],
        out_shape=[jax.ShapeDtypeStruct((t, 128), F32), jax.ShapeDtypeStruct((2, t, 512), F32)],
        compiler_params=_cparams(("parallel",)),
    )(ps, alog_row, dt_row, w2f, b2f, w2b, b2b)


def _gates_bwd(ps, alog_row, dt_row, w2f, b2f, w2b, b2b, dgsm, dgk, tm=512):
    t = ps.shape[0]

    def body(ps_ref, a_ref, d_ref, wf_ref, bf_ref, wb_ref, bb_ref, dgsm_ref, dgk_ref,
             dps_ref, da_ref, dd_ref, dwf_ref, dbf_ref, dwb_ref, dbb_ref):
        _, vjp = jax.vjp(_gates_f, ps_ref[...], a_ref[...], d_ref[...], wf_ref[...], bf_ref[...],
                         wb_ref[...], bb_ref[...])
        dps, da, dd, dwf, dbf, dwb, dbb = vjp((dgsm_ref[...], dgk_ref[0], dgk_ref[1]))
        dps_ref[:, 0:128] = dps.astype(BF16)
        dps_ref[:, 128:DP_TILE] = jnp.zeros((tm, DP_TILE - 128), BF16)
        accs = ((da_ref, da), (dd_ref, dd), (dwf_ref, dwf), (dbf_ref, dbf), (dwb_ref, dwb), (dbb_ref, dbb))

        @pl.when(pl.program_id(0) == 0)
        def _():
            for ref, _ in accs:
                ref[...] = jnp.zeros_like(ref)
        for ref, val in accs:
            ref[...] += jnp.broadcast_to(val, ref.shape)

    row = lambda n: pl.BlockSpec((1, n), lambda i: (0, 0))
    row8 = lambda n: pl.BlockSpec((8, n), lambda i: (0, 0))
    mat = pl.BlockSpec((128, 512), lambda i: (0, 0))
    return pl.pallas_call(
        body, name="gates_bwd", grid=(t // tm,),
        in_specs=[pl.BlockSpec((tm, 128), lambda i: (i, PS_BLOCK)), row(128), row(128), mat, row(512), mat, row(512),
                  pl.BlockSpec((tm, 128), lambda i: (i, 0)),
                  pl.BlockSpec((2, tm, 512), lambda i: (0, i, 0))],
        out_specs=[pl.BlockSpec((tm, DP_TILE), lambda i: (i, 0)), row8(128), row8(128), mat, row8(512), mat,
                   row8(512)],
        out_shape=[jax.ShapeDtypeStruct((t, DP_TILE), BF16),
                   jax.ShapeDtypeStruct((8, 128), F32), jax.ShapeDtypeStruct((8, 128), F32),
                   jax.ShapeDtypeStruct((128, 512), F32), jax.ShapeDtypeStruct((8, 512), F32),
                   jax.ShapeDtypeStruct((128, 512), F32), jax.ShapeDtypeStruct((8, 512), F32)],
        compiler_params=_cparams(("arbitrary",)),
    )(ps, alog_row, dt_row, w2f, b2f, w2b, b2b, dgsm, dgk)


def _rows(i):
    return pl.ds(pl.multiple_of(i * CH, CH), CH)


def _srows(i):
    return pl.ds(pl.multiple_of(i * SC, SC), SC)


def _first_row(x):
    row = lax.broadcasted_iota(jnp.int32, (8, x.shape[1]), 0)
    return jnp.where(row == 0, jnp.broadcast_to(x, (8, x.shape[1])), 0.0)


def _chunk_rows(e_ref, i):
    pad = jnp.zeros((CH - 8, 128), F32)
    return jnp.concatenate([x for c in range(SC // CH) for x in (e_ref[(SC // CH) * i + c], pad)], axis=0)


def _gcum_f(gsm, tm):
    i = lax.broadcasted_iota(jnp.int32, (tm, tm), 0)
    j = lax.broadcasted_iota(jnp.int32, (tm, tm), 1)
    same = (i >> 6) == (j >> 6)
    lower = (same & (i >= j)).astype(F32)
    upper = (same & (i <= j)).astype(F32)
    r = lax.broadcasted_iota(jnp.int32, (128, D), 0)
    head = lax.broadcasted_iota(jnp.int32, (128, D), 1) >> 7
    pick = lambda off: (r == head + off).astype(F32)
    lane = lax.broadcasted_iota(jnp.int32, gsm.shape, 1)
    run = jnp.where(lane < 8, cmm(lower, gsm), cmm(upper, gsm))
    return mmc(run, pick(0)), mmc(run, pick(8)), mmc(gsm, pick(16)), mmc(gsm, pick(24))


def _gcum_fwd(gsm, tm=256):
    t = gsm.shape[0]

    def body(s_ref, g_ref, b_ref):
        gf, gb, bf, bb = _gcum_f(s_ref[...], tm)
        g_ref[0] = gf
        g_ref[1] = gb
        b_ref[0] = bf
        b_ref[1] = bb

    two = pl.BlockSpec((2, tm, D), lambda i: (0, i, 0))
    return pl.pallas_call(
        body, name="gcum_fwd", grid=(t // tm,),
        in_specs=[pl.BlockSpec((tm, 128), lambda i: (i, 0))], out_specs=[two, two],
        out_shape=[jax.ShapeDtypeStruct((2, t, D), F32)] * 2,
        compiler_params=_cparams(("parallel",)),
    )(gsm)


def _gcum_bwd(gsm, dg2, db2, tm=256):
    t = gsm.shape[0]

    def body(s_ref, dg_ref, db_ref, ds_ref):
        _, vjp = jax.vjp(lambda s: _gcum_f(s, tm), s_ref[...])
        ds_ref[...] = vjp((dg_ref[0], dg_ref[1], db_ref[0], db_ref[1]))[0]

    two = pl.BlockSpec((2, tm, D), lambda i: (0, i, 0))
    tile = pl.BlockSpec((tm, 128), lambda i: (i, 0))
    return pl.pallas_call(
        body, name="gcum_bwd", grid=(t // tm,),
        in_specs=[tile, two, two], out_specs=tile,
        out_shape=jax.ShapeDtypeStruct((t, 128), F32),
        compiler_params=_cparams(("parallel",)),
    )(gsm, dg2, db2)


def _gdn_intra_fwd(qn, kn, vc, g2, b2):
    t = qn.shape[0]
    n = t // CH

    def body(q_ref, k_ref, v_ref, g_ref, b_ref, u_ref, w_ref, a_ref, qd_ref, kd_ref, e_ref, t_ref):
        cs = _Consts(pl.program_id(0) == 1)

        def step(i, carry):
            r = _srows(i)
            q, k, v, g, bx = q_ref[r, :], k_ref[r, :], v_ref[r, :], g_ref[r, :], b_ref[r, :]
            u, w, a, qd, kd, el, tinv = _gdn_intra(q, k, v, g, bx, None, cs)
            u_ref[r, :] = u
            w_ref[r, :] = w
            a_ref[r, :] = a
            qd_ref[r, :] = qd
            kd_ref[r, :] = kd
            t_ref[r, :] = tinv
            for c in range(SC // CH):
                e_ref[(SC // CH) * i + c] = el[c * CH:c * CH + 8]
            return carry

        lax.fori_loop(0, t // SC, step, 0)

    head = pl.BlockSpec((t, DA), lambda d, h: (0, h))
    dh = pl.BlockSpec((None, t, DA), lambda d, h: (d, 0, h))
    sq = lambda w: pl.BlockSpec((None, None, t, w), lambda d, h: (d, h, 0, 0))
    big = jax.ShapeDtypeStruct((2, t, D), F32)
    return pl.pallas_call(
        body, name="gdn_intra_fwd", grid=(2, NA),
        in_specs=[head, head, head, dh, dh],
        out_specs=[dh, dh, sq(CH), dh, dh, pl.BlockSpec((None, None, n, 8, 128), lambda d, h: (d, h, 0, 0, 0)),
                   sq(SC)],
        out_shape=[big, big, jax.ShapeDtypeStruct((2, NA, t, CH), F32), big, big,
                   jax.ShapeDtypeStruct((2, NA, n, 8, 128), F32), jax.ShapeDtypeStruct((2, NA, t, SC), F32)],
        compiler_params=_cparams(("parallel", "parallel")),
    )(qn, kn, vc, g2, b2)


SCAN_TB = 256
SCAN_HB = 8


def _scan_specs(t, width, nheads, hb, along):
    nt = t // SCAN_TB
    nb = SCAN_TB // CH

    def tmap(d, tt):
        fwd = tt + d * (nt - 1 - 2 * tt)
        return fwd if along > 0 else nt - 1 - fwd

    tok = pl.BlockSpec((None, SCAN_TB, hb * width), lambda d, h, tt: (d, tmap(d, tt), h))
    per = lambda *tail: pl.BlockSpec((None, hb, nb) + tail, lambda d, h, tt: (d, h, tmap(d, tt)) + (0,) * len(tail))
    sq = pl.BlockSpec((None, hb, SCAN_TB, CH), lambda d, h, tt: (d, h, tmap(d, tt), 0))
    shared = lambda w: pl.BlockSpec((SCAN_TB, hb * w), lambda d, h, tt: (tmap(d, tt), h))
    return tok, per, sq, shared, (2, nheads // hb, nt), nb


def _gdn_scan_fwd(u, w, a, qd, kd, e):
    t = u.shape[1]
    tok, per, sq, _, grid, nb = _scan_specs(t, DA, NA, SCAN_HB, +1)

    def body(u_ref, w_ref, a_ref, qd_ref, kd_ref, e_ref, o_ref, s_ref, state):
        rev = pl.program_id(0) == 1

        @pl.when(pl.program_id(2) == 0)
        def _():
            state[...] = jnp.zeros_like(state)

        def step(i, ss):
            ci = jnp.where(rev, nb - 1 - i, i)
            r = _rows(ci)
            out = []
            for hh, s in enumerate(ss):
                c = slice(hh * DA, (hh + 1) * DA)
                s_ref[hh, ci] = s
                o, sn = _gdn_scan(u_ref[r, c], w_ref[r, c], a_ref[hh, r, :], qd_ref[r, c], kd_ref[r, c],
                                  e_ref[hh, ci][0:1], s)
                o_ref[r, c] = o
                out.append(sn)
            return tuple(out)

        ss = lax.fori_loop(0, nb, step, tuple(state[hh] for hh in range(SCAN_HB)))
        for hh, s in enumerate(ss):
            state[hh] = s

    return pl.pallas_call(
        body, name="gdn_scan_fwd", grid=grid,
        in_specs=[tok, tok, sq, tok, tok, per(8, 128)],
        out_specs=[tok, per(DA, DA)],
        out_shape=[jax.ShapeDtypeStruct((2, t, D), F32), jax.ShapeDtypeStruct((2, NA, t // CH, DA, DA), F32)],
        scratch_shapes=[pltpu.VMEM((SCAN_HB, DA, DA), F32)],
        compiler_params=_cparams(("parallel", "parallel", "arbitrary")),
    )(u, w, a, qd, kd, e)


def _gdn_scan_bwd(u, w, a, qd, kd, e, ssave, do):
    t = u.shape[1]
    tok, per, sq, shared, grid, nb = _scan_specs(t, DA, NA, SCAN_HB, -1)

    def body(u_ref, w_ref, a_ref, qd_ref, kd_ref, e_ref, s_ref, do_ref,
             du_ref, dw_ref, da_ref, dqd_ref, dkd_ref, de_ref, state):
        rev = pl.program_id(0) == 1

        @pl.when(pl.program_id(2) == 0)
        def _():
            state[...] = jnp.zeros_like(state)

        def step(i, dss):
            ci = jnp.where(rev, i, nb - 1 - i)
            r = _rows(ci)
            out = []
            for hh, ds in enumerate(dss):
                c = slice(hh * DA, (hh + 1) * DA)
                _, vjp = jax.vjp(_gdn_scan, u_ref[r, c], w_ref[r, c], a_ref[hh, r, :], qd_ref[r, c], kd_ref[r, c],
                                 e_ref[hh, ci][0:1], s_ref[hh, ci])
                du, dw, da, dqd, dkd, de, dsn = vjp((do_ref[r, c], ds))
                du_ref[r, c] = du
                dw_ref[r, c] = dw
                da_ref[hh, r, :] = da
                dqd_ref[r, c] = dqd
                dkd_ref[r, c] = dkd
                de_ref[hh, ci] = _first_row(de)
                out.append(dsn)
            return tuple(out)

        dss = lax.fori_loop(0, nb, step, tuple(state[hh] for hh in range(SCAN_HB)))
        for hh, ds in enumerate(dss):
            state[hh] = ds

    big = jax.ShapeDtypeStruct((2, t, D), F32)
    return pl.pallas_call(
        body, name="gdn_scan_bwd", grid=grid,
        in_specs=[tok, tok, sq, tok, tok, per(8, 128), per(DA, DA), shared(DA)],
        out_specs=[tok, tok, sq, tok, tok, per(8, 128)],
        out_shape=[big, big, jax.ShapeDtypeStruct((2, NA, t, CH), F32), big, big,
                   jax.ShapeDtypeStruct((2, NA, t // CH, 8, 128), F32)],
        scratch_shapes=[pltpu.VMEM((SCAN_HB, DA, DA), F32)],
        compiler_params=_cparams(("parallel", "parallel", "arbitrary")),
    )(u, w, a, qd, kd, e, ssave, do)


def _gdn_intra_bwd(qn, kn, vc, g2, b2, tinv, du, dw, da, dqd, dkd, de):
    t = qn.shape[0]
    n = t // CH

    def body(q_ref, k_ref, v_ref, g_ref, b_ref, t_ref, du_ref, dw_ref, da_ref, dqd_ref, dkd_ref, de_ref,
             dq_ref, dk_ref, dv_ref, dg_ref, db_ref):
        d = pl.program_id(1)
        cs = _Consts(d == 1)

        @pl.when(d == 0)
        def _():
            dq_ref[...] = jnp.zeros_like(dq_ref)
            dk_ref[...] = jnp.zeros_like(dk_ref)
            dv_ref[...] = jnp.zeros_like(dv_ref)

        def step(i, carry):
            r = _srows(i)
            tinv_c = t_ref[r, :]
            f = lambda q, k, v, g, bx: _gdn_intra(q, k, v, g, bx, tinv_c, cs)
            _, vjp = jax.vjp(f, q_ref[r, :], k_ref[r, :], v_ref[r, :], g_ref[r, :], b_ref[r, :])
            dq, dk, dv, dg, dbx = vjp((du_ref[r, :], dw_ref[r, :], da_ref[r, :], dqd_ref[r, :],
                                       dkd_ref[r, :], _chunk_rows(de_ref, i)))
            dq_ref[r, :] += dq
            dk_ref[r, :] += dk
            dv_ref[r, :] += dv
            dg_ref[r, :] = dg
            db_ref[r, :] = dbx
            return carry

        lax.fori_loop(0, t // SC, step, 0)

    head = pl.BlockSpec((t, DA), lambda h, d: (0, h))
    dh = pl.BlockSpec((None, t, DA), lambda h, d: (d, 0, h))
    sq = pl.BlockSpec((None, None, t, CH), lambda h, d: (d, h, 0, 0))
    tq = pl.BlockSpec((None, None, t, SC), lambda h, d: (d, h, 0, 0))
    full = jax.ShapeDtypeStruct((t, D), F32)
    big = jax.ShapeDtypeStruct((2, t, D), F32)
    return pl.pallas_call(
        body, name="gdn_intra_bwd", grid=(NA, 2),
        in_specs=[head, head, head, dh, dh, tq, dh, dh, sq, dh, dh,
                  pl.BlockSpec((None, None, n, 8, 128), lambda h, d: (d, h, 0, 0, 0))],
        out_specs=[head, head, head, dh, dh],
        out_shape=[full, full, full, big, big],
        compiler_params=_cparams(("arbitrary", "arbitrary")),
    )(qn, kn, vc, g2, b2, tinv, du, dw, da, dqd, dkd, de)


def _gla_specs(t, order):
    ix = (lambda d, h: (d, h)) if order == "dh" else (lambda h, d: (d, h))

    def mk(fn):
        return lambda a, b: fn(*ix(a, b))
    q = pl.BlockSpec((t, DKB), mk(lambda d, h: (0, 32 + h)))
    k = pl.BlockSpec((t, DKB), mk(lambda d, h: (0, 36 + h)))
    v = pl.BlockSpec((t, DVB), mk(lambda d, h: (0, 20 + h)))
    dk = pl.BlockSpec((None, t, DKB), mk(lambda d, h: (d, 0, h)))
    dv = pl.BlockSpec((None, t, DVB), mk(lambda d, h: (d, 0, h)))
    e = pl.BlockSpec((None, None, t // CH, 8, 128), mk(lambda d, h: (d, h, 0, 0, 0)))
    s = pl.BlockSpec((None, None, t // CH, DVB, DKB), mk(lambda d, h: (d, h, 0, 0, 0)))
    return q, k, v, dk, dv, e, s


def _gla_intra_fwd(p, gk):
    t = p.shape[0]
    n = t // CH

    def body(q_ref, k_ref, v_ref, g_ref, qg_ref, kd_ref, in_ref, e_ref):
        cs = _Consts(pl.program_id(0) == 1)

        def step(i, carry):
            r = _srows(i)
            qg, kd, intra, el = _gla_intra(q_ref[r, :], k_ref[r, :], v_ref[r, :], g_ref[r, :], cs)
            qg_ref[r, :] = qg
            kd_ref[r, :] = kd
            in_ref[r, :] = intra
            for c in range(SC // CH):
                e_ref[(SC // CH) * i + c] = el[c * CH:c * CH + 8]
            return carry

        lax.fori_loop(0, t // SC, step, 0)

    q, k, v, dk, dv, e, _ = _gla_specs(t, "dh")
    return pl.pallas_call(
        body, name="gla_intra_fwd", grid=(2, NB),
        in_specs=[q, k, v, dk], out_specs=[dk, dk, dv, e],
        out_shape=[jax.ShapeDtypeStruct((2, t, NB * DKB), F32), jax.ShapeDtypeStruct((2, t, NB * DKB), F32),
                   jax.ShapeDtypeStruct((2, t, D), F32), jax.ShapeDtypeStruct((2, NB, n, 8, 128), F32)],
        compiler_params=_cparams(("parallel", "parallel")),
    )(p, p, p, gk)


GLA_HB = 4


def _gla_v_spec(t, along):
    nt = t // SCAN_TB

    def tmap(d, tt):
        fwd = tt + d * (nt - 1 - 2 * tt)
        return fwd if along > 0 else nt - 1 - fwd

    return pl.BlockSpec((SCAN_TB, GLA_HB * DVB), lambda d, h, tt: (tmap(d, tt), 5120 // (GLA_HB * DVB) + h))


def _gla_scan_fwd(p, qg, kd, intra, e):
    t = p.shape[0]
    tokk, per, _, _, grid, nb = _scan_specs(t, DKB, NB, GLA_HB, +1)
    tokv = _scan_specs(t, DVB, NB, GLA_HB, +1)[0]

    def body(v_ref, qg_ref, kd_ref, in_ref, e_ref, o_ref, s_ref, state):
        rev = pl.program_id(0) == 1

        @pl.when(pl.program_id(2) == 0)
        def _():
            state[...] = jnp.zeros_like(state)

        def step(i, sts):
            ci = jnp.where(rev, nb - 1 - i, i)
            r = _rows(ci)
            out = []
            for hh, st in enumerate(sts):
                ck = slice(hh * DKB, (hh + 1) * DKB)
                cv = slice(hh * DVB, (hh + 1) * DVB)
                s_ref[hh, ci] = st
                o, stn = _gla_scan(qg_ref[r, ck], kd_ref[r, ck], v_ref[r, cv], e_ref[hh, ci][0:1], st)
                o_ref[r, cv] = o + in_ref[r, cv]
                out.append(stn)
            return tuple(out)

        sts = lax.fori_loop(0, nb, step, tuple(state[hh] for hh in range(GLA_HB)))
        for hh, st in enumerate(sts):
            state[hh] = st

    return pl.pallas_call(
        body, name="gla_scan_fwd", grid=grid,
        in_specs=[_gla_v_spec(t, +1), tokk, tokk, tokv, per(8, 128)], out_specs=[tokv, per(DVB, DKB)],
        out_shape=[jax.ShapeDtypeStruct((2, t, D), F32), jax.ShapeDtypeStruct((2, NB, t // CH, DVB, DKB), F32)],
        scratch_shapes=[pltpu.VMEM((GLA_HB, DVB, DKB), F32)],
        compiler_params=_cparams(("parallel", "parallel", "arbitrary")),
    )(p, qg, kd, intra, e)


def _gla_scan_bwd(p, qg, kd, e, ssave, do):
    t = p.shape[0]
    tokk, per, _, shared, grid, nb = _scan_specs(t, DKB, NB, GLA_HB, -1)
    tokv = _scan_specs(t, DVB, NB, GLA_HB, -1)[0]

    def body(v_ref, qg_ref, kd_ref, e_ref, s_ref, do_ref, dqg_ref, dkd_ref, dv_ref, de_ref, state):
        rev = pl.program_id(0) == 1

        @pl.when(pl.program_id(2) == 0)
        def _():
            state[...] = jnp.zeros_like(state)

        def step(i, dsts):
            ci = jnp.where(rev, i, nb - 1 - i)
            r = _rows(ci)
            out = []
            for hh, dst in enumerate(dsts):
                ck = slice(hh * DKB, (hh + 1) * DKB)
                cv = slice(hh * DVB, (hh + 1) * DVB)
                _, vjp = jax.vjp(_gla_scan, qg_ref[r, ck], kd_ref[r, ck], v_ref[r, cv], e_ref[hh, ci][0:1],
                                 s_ref[hh, ci])
                dqg, dkd, dv, de, dstn = vjp((do_ref[r, cv], dst))
                dqg_ref[r, ck] = dqg
                dkd_ref[r, ck] = dkd
                dv_ref[r, cv] = dv
                de_ref[hh, ci] = _first_row(de)
                out.append(dstn)
            return tuple(out)

        dsts = lax.fori_loop(0, nb, step, tuple(state[hh] for hh in range(GLA_HB)))
        for hh, dst in enumerate(dsts):
            state[hh] = dst

    return pl.pallas_call(
        body, name="gla_scan_bwd", grid=grid,
        in_specs=[_gla_v_spec(t, -1), tokk, tokk, per(8, 128), per(DVB, DKB), shared(DVB)],
        out_specs=[tokk, tokk, tokv, per(8, 128)],
        out_shape=[jax.ShapeDtypeStruct((2, t, NB * DKB), F32), jax.ShapeDtypeStruct((2, t, NB * DKB), F32),
                   jax.ShapeDtypeStruct((2, t, D), F32), jax.ShapeDtypeStruct((2, NB, t // CH, 8, 128), F32)],
        scratch_shapes=[pltpu.VMEM((GLA_HB, DVB, DKB), F32)],
        compiler_params=_cparams(("parallel", "parallel", "arbitrary")),
    )(p, qg, kd, e, ssave, do)


def _gla_intra_bwd(p, gk, dqg, dkd, dvs, de, do):
    t = p.shape[0]
    n = t // CH

    def body(q_ref, k_ref, v_ref, g_ref, dqg_ref, dkd_ref, dvs_ref, de_ref, do_ref,
             dq_ref, dk_ref, dv_ref, dg_ref):
        d = pl.program_id(1)
        cs = _Consts(d == 1)

        @pl.when(d == 0)
        def _():
            dq_ref[...] = jnp.zeros_like(dq_ref)
            dk_ref[...] = jnp.zeros_like(dk_ref)
            dv_ref[...] = jnp.zeros_like(dv_ref)

        def step(i, carry):
            r = _srows(i)
            f = lambda q, k, v, g: _gla_intra(q, k, v, g, cs)
            _, vjp = jax.vjp(f, q_ref[r, :], k_ref[r, :], v_ref[r, :], g_ref[r, :])
            dq, dk, dv, dg = vjp((dqg_ref[r, :], dkd_ref[r, :], do_ref[r, :], _chunk_rows(de_ref, i)))
            dq_ref[r, :] += dq
            dk_ref[r, :] += dk
            dv_ref[r, :] += dv + dvs_ref[r, :]
            dg_ref[r, :] = dg
            return carry

        lax.fori_loop(0, t // SC, step, 0)

    q, k, v, dk, dv, e_s, _ = _gla_specs(t, "hd")
    hk = pl.BlockSpec((t, DKB), lambda h, d: (0, h))
    hv = pl.BlockSpec((t, DVB), lambda h, d: (0, h))
    return pl.pallas_call(
        body, name="gla_intra_bwd", grid=(NB, 2),
        in_specs=[q, k, v, dk, dk, dk, dv, e_s, hv],
        out_specs=[hk, hk, hv, dk],
        out_shape=[jax.ShapeDtypeStruct((t, NB * DKB), F32), jax.ShapeDtypeStruct((t, NB * DKB), F32),
                   jax.ShapeDtypeStruct((t, D), F32), jax.ShapeDtypeStruct((2, t, NB * DKB), F32)],
        compiler_params=_cparams(("arbitrary", "arbitrary")),
    )(p, p, p, gk, dqg, dkd, dvs, de, do)


def _seg_gate(o, z, w):
    return _rms(o, w) * _silu(z)


def _seg_merge(ya, yb, ga, gb):
    return _sigmoid(ga) * ya + _sigmoid(gb) * yb


def _seg_loss(out, x, tgt, w):
    err = x + _rms(out, w) - tgt
    return 0.5 * jnp.sum(jnp.mean(err * err, axis=-1, keepdims=True), axis=0, keepdims=True)


def _post(oa2, ob2, p, x, tgt, gdn_w, gla_w, lnpost, w3, tm=128):
    t = x.shape[0]

    def body(oa_ref, ob_ref, z_ref, gb_ref, ga_ref, gB_ref, x_ref, t_ref, aw_ref, bw_ref, lw_ref, w_ref,
             loss_ref, doa_ref, dob_ref, dz_ref, dgb_ref, dga_ref, dgB_ref, dy_ref,
             dw_ref, daw_ref, dbw_ref, dlw_ref):
        first = pl.program_id(0) == 0
        oa = oa_ref[0] + oa_ref[1]
        ob = ob_ref[0] + ob_ref[1]
        z, gb = z_ref[...], gb_ref[...]
        aw, bw = aw_ref[...], bw_ref[...]
        rs = D // NSHARD

        def mat(a, m):
            return sum(jnp.dot(a[:, s * rs:(s + 1) * rs], w_ref[s, m], preferred_element_type=F32)
                       for s in range(NSHARD))

        def mat_t(g, m):
            return jnp.concatenate([_bdot(g, w_ref[s, m], 1, 1) for s in range(NSHARD)], axis=1)

        def add_dw(a, g, m):
            for s in range(NSHARD):
                dw_ref[s, m] += _bdot(a[:, s * rs:(s + 1) * rs], g, 0, 0)

        pa = [jax.vjp(_seg_gate, oa[:, h * DA:(h + 1) * DA], z[:, h * DA:(h + 1) * DA], aw) for h in range(NA)]
        pb = [jax.vjp(_seg_gate, ob[:, h * DVB:(h + 1) * DVB], gb[:, h * DVB:(h + 1) * DVB], bw)
              for h in range(NB)]
        a1 = jnp.concatenate([v for v, _ in pa], axis=1).astype(BF16)
        a2 = jnp.concatenate([v for v, _ in pb], axis=1).astype(BF16)
        ya = mat(a1, 0)
        yb = mat(a2, 1)
        merged, vjp_m = jax.vjp(_seg_merge, ya, yb, ga_ref[...], gB_ref[...])
        mb = merged.astype(BF16)
        out = mat(mb, 2)
        loss, vjp_l = jax.vjp(_seg_loss, out, x_ref[...], t_ref[...], lw_ref[...])
        dout, dyres, _, dlw = vjp_l(jnp.ones((1, 1), F32))
        dy_ref[...] = dyres
        doutb = dout.astype(BF16)
        dmerged = mat_t(doutb, 2)
        dya, dyb, dga, dgB = vjp_m(dmerged)
        dga_ref[...] = dga.astype(BF16)
        dgB_ref[...] = dgB.astype(BF16)
        dyab, dybb = dya.astype(BF16), dyb.astype(BF16)
        da1 = mat_t(dyab, 0)
        da2 = mat_t(dybb, 1)

        daw = jnp.zeros_like(aw)
        for h in range(NA):
            sl = slice(h * DA, (h + 1) * DA)
            do, dz, dw = pa[h][1](da1[:, sl])
            doa_ref[:, sl] = do
            dz_ref[:, sl] = dz.astype(BF16)
            daw = daw + dw
        dbw = jnp.zeros_like(bw)
        for h in range(NB):
            sl = slice(h * DVB, (h + 1) * DVB)
            do, dg, dw = pb[h][1](da2[:, sl])
            dob_ref[:, sl] = do
            dgb_ref[:, sl] = dg.astype(BF16)
            dbw = dbw + dw

        @pl.when(first)
        def _():
            loss_ref[...] = jnp.zeros_like(loss_ref)
            dw_ref[...] = jnp.zeros_like(dw_ref)
            daw_ref[...] = jnp.zeros_like(daw_ref)
            dbw_ref[...] = jnp.zeros_like(dbw_ref)
            dlw_ref[...] = jnp.zeros_like(dlw_ref)

        loss_ref[...] += jnp.broadcast_to(loss, loss_ref.shape)
        add_dw(a1, dyab, 0)
        add_dw(a2, dybb, 1)
        add_dw(mb, doutb, 2)
        daw_ref[...] += jnp.broadcast_to(daw, daw_ref.shape)
        dbw_ref[...] += jnp.broadcast_to(dbw, dbw_ref.shape)
        dlw_ref[...] += jnp.broadcast_to(dlw, dlw_ref.shape)

    two = pl.BlockSpec((2, tm, D), lambda i: (0, i, 0))
    pcol = lambda c: pl.BlockSpec((tm, D), lambda i: (i, c))
    tok = pl.BlockSpec((tm, D), lambda i: (i, 0))
    row = lambda n: pl.BlockSpec((1, n), lambda i: (0, 0))
    row8 = lambda n: pl.BlockSpec((8, n), lambda i: (0, 0))
    once = pl.Buffered(1)
    tokf = jax.ShapeDtypeStruct((t, D), F32)
    tokb = jax.ShapeDtypeStruct((t, D), BF16)
    wspec = pl.BlockSpec((NSHARD, 3, D // NSHARD, D), lambda i: (0, 0, 0, 0), pipeline_mode=once)
    return pl.pallas_call(
        body, name="post", grid=(t // tm,),
        in_specs=[two, two, pcol(3), pcol(6), pcol(7), pcol(8), tok, tok, row(DA), row(DVB), row(D), wspec],
        out_specs=[row8(128), tok, tok, tok, tok, tok, tok, tok, wspec, row8(DA), row8(DVB), row8(D)],
        out_shape=[jax.ShapeDtypeStruct((8, 128), F32), tokf, tokf, tokb, tokb, tokb, tokb, tokf,
                   jax.ShapeDtypeStruct((NSHARD, 3, D // NSHARD, D), F32),
                   jax.ShapeDtypeStruct((8, DA), F32), jax.ShapeDtypeStruct((8, DVB), F32),
                   jax.ShapeDtypeStruct((8, D), F32)],
        compiler_params=_cparams(("arbitrary",), vmem_mb=56),
    )(oa2, ob2, p, p, p, p, x, tgt, gdn_w, gla_w, lnpost, w3)


def _adam_math(w, g, m, v):
    nm = B1 * m + (1.0 - B1) * g
    nv = B2 * v + (1.0 - B2) * (g * g)
    m_hat = nm / (1.0 - B1 ** STEP)
    v_hat = nv / (1.0 - B2 ** STEP)
    return -LR * (m_hat / (jnp.sqrt(v_hat) + ADAM_EPS) + WD * w), nm, nv


SMALL_SLOTS = (("ln_pre_w", 0, 1024, 0), ("a_log_fwd", 1024, 8, 0), ("a_log_bwd", 1024, 8, 8),
               ("dt_bias_fwd", 1152, 8, 0), ("dt_bias_bwd", 1152, 8, 8), ("gdn_norm_w", 1280, 128, 0),
               ("gk_b2_fwd", 1408, 512, 0), ("gk_b2_bwd", 1920, 512, 0), ("gla_norm_w", 2432, 256, 0),
               ("ln_post_w", 2688, 1024, 0))
SMALL_W = 3840


def _adam_small(gsum, ws, ms, vs):
    nw = len(SMALL_SLOTS)

    def body(g_ref, *refs):
        w_refs, m_refs, v_refs, outs = refs[0:nw], refs[nw:2 * nw], refs[2 * nw:3 * nw], refs[3 * nw:]
        for i, (_, off, n, shift) in enumerate(SMALL_SLOTS):
            slot = g_ref[0:1, off:off + max(n, 128)]
            if shift:
                slot = pltpu.roll(slot, 128 - shift, 1)
            g = slot[:, 0:n]
            d, nm, nv = _adam_math(w_refs[i][...], g, m_refs[i][...], v_refs[i][...])
            for k, val in enumerate((g, d, nm, nv)):
                outs[4 * i + k][...] = val

    vm = pl.BlockSpec(memory_space=pltpu.VMEM)
    res = pl.pallas_call(
        body, name="adam_small", in_specs=[vm] * (1 + 3 * nw), out_specs=[vm] * (4 * nw),
        out_shape=[jax.ShapeDtypeStruct((1, n), F32) for _, _, n, _ in SMALL_SLOTS for _ in range(4)],
    )(gsum, *ws, *ms, *vs)
    return {name: res[4 * i:4 * i + 4] for i, (name, _, _, _) in enumerate(SMALL_SLOTS)}


def _adam(w, mine, got, m, v, tr, tile0=0, name=""):
    rows, cols = w.shape
    nh = mine.shape[0] // tr

    def body(c_ref, w_ref, a_ref, b_ref, m_ref, v_ref, g_ref, d_ref, nm_ref, nv_ref):
        half = (tile0 + pl.program_id(0)) // nh
        g = jnp.where(half == c_ref[0], a_ref[...], b_ref[...])
        d, nm, nv = _adam_math(w_ref[...], g, m_ref[...], v_ref[...])
        g_ref[...] = g
        d_ref[...] = d
        nm_ref[...] = nm
        nv_ref[...] = nv

    blk = pl.BlockSpec((tr, cols), lambda i, cc: (i, 0))
    half = pl.BlockSpec((tr, cols), lambda i, cc: ((tile0 + i) % nh, 0))
    shp = jax.ShapeDtypeStruct((rows, cols), F32)
    return pl.pallas_call(
        body, name=f"adam_{name}{rows}x{cols}",
        grid_spec=pltpu.PrefetchScalarGridSpec(
            num_scalar_prefetch=1, grid=(rows // tr,),
            in_specs=[blk, half, half, blk, blk], out_specs=[blk] * 4),
        out_shape=[shp] * 4,
        compiler_params=_cparams(("parallel",)),
    )(lax.axis_index("c").reshape(1), w, mine, got, m, v)


def _sum_cast(own, got):
    ns, _, r, c = own.shape
    tr = r // 4 if r % 64 == 0 else r

    def body(c_ref, a_ref, b_ref, f_ref, h_ref):
        s = a_ref[...] + b_ref[...]
        f_ref[...] = s
        h_ref[...] = s.astype(BF16)

    return pl.pallas_call(
        body, name=f"sum_cast_{r}x{c}",
        grid_spec=pltpu.PrefetchScalarGridSpec(
            num_scalar_prefetch=1, grid=(ns, r // tr),
            in_specs=[pl.BlockSpec((None, None, tr, c), lambda s, i, cc: (s, cc[0], i, 0)),
                      pl.BlockSpec((None, tr, c), lambda s, i, cc: (s, i, 0))],
            out_specs=[pl.BlockSpec((None, tr, c), lambda s, i, cc: (s, i, 0)),
                       pl.BlockSpec((None, tr, c), lambda s, i, cc: (s, i, 0))]),
        out_shape=[jax.ShapeDtypeStruct((ns, r, c), F32), jax.ShapeDtypeStruct((ns, r, c), BF16)],
        compiler_params=_cparams(("parallel", "parallel")),
    )(lax.axis_index("c").reshape(1), own, got)


def _sum4(mine, got):
    _, r, c = mine.shape
    tr = r // 4 if r % 64 == 0 else r

    def body(s_ref, a_ref, g_ref, o_ref):
        acc = a_ref[...] + g_ref[0].astype(F32)
        acc = acc + g_ref[1].astype(F32)
        o_ref[...] = acc + g_ref[2].astype(F32)

    shard = (2 * lax.axis_index("x") + lax.axis_index("y")).reshape(1)
    return pl.pallas_call(
        body, name=f"sum4_{r}x{c}",
        grid_spec=pltpu.PrefetchScalarGridSpec(
            num_scalar_prefetch=1, grid=(r // tr,),
            in_specs=[pl.BlockSpec((None, tr, c), lambda i, ss: (ss[0], i, 0)),
                      pl.BlockSpec((3, tr, c), lambda i, ss: (0, i, 0))],
            out_specs=pl.BlockSpec((tr, c), lambda i, ss: (i, 0))),
        out_shape=jax.ShapeDtypeStruct((r, c), F32),
        compiler_params=_cparams(("parallel",)),
    )(shard, mine, got)


def _place():
    x, y, c = lax.axis_index("x"), lax.axis_index("y"), lax.axis_index("c")
    chips = [(1 - x, y), (x, 1 - y), (1 - x, 1 - y)]
    return x, y, c, chips


def _gather_weights(parts):
    npart = len(parts)

    def body(*refs):
        ins, outs = refs[:npart], refs[npart:2 * npart]
        send_sems, recv_sems = refs[2 * npart:]
        x, y, c, chips = _place()
        sibling = (x, y, 1 - c)
        mine = 2 * x + y

        def remote(k, p, shard, half, to, src=None):
            dst = outs[p].at[shard, half]
            return pltpu.make_async_remote_copy(
                src_ref=dst if src is None else src, dst_ref=dst,
                send_sem=send_sems.at[k], recv_sem=recv_sems.at[k], device_id=to, device_id_type=MESH)

        first = [remote(j * npart + p, p, mine, c, (*chip, c), src=ins[p].at[c])
                 for j, chip in enumerate(chips) for p in range(npart)]
        for cp in first:
            cp.start()
        passed = []
        for j, (cx, cy) in enumerate(chips):
            for p in range(npart):
                remote(j * npart + p, p, 2 * cx + cy, c, (x, y, c)).wait_recv()
                fw = remote((3 + j) * npart + p, p, 2 * cx + cy, c, sibling)
                fw.start()
                passed.append(fw)
        for j, (cx, cy) in enumerate(chips):
            for p in range(npart):
                remote((3 + j) * npart + p, p, 2 * cx + cy, 1 - c, (x, y, c)).wait_recv()
        for cp in first + passed:
            cp.wait_send()

    got = pl.pallas_call(
        body, name="gather_weights",
        in_specs=[ANY] * npart, out_specs=[ANY] * npart,
        out_shape=[jax.ShapeDtypeStruct((NSHARD,) + a.shape, a.dtype) for a in parts],
        scratch_shapes=[pltpu.SemaphoreType.DMA((6 * npart,)), pltpu.SemaphoreType.DMA((6 * npart,))],
    )(*parts)
    mine = 2 * lax.axis_index("x") + lax.axis_index("y")
    return [lax.dynamic_update_index_in_dim(g, a, mine, 0) for g, a in zip(got, parts)]


def _swap_halves(parts, tag=""):
    npart = len(parts)

    def body(*refs):
        ins, outs = refs[:npart], refs[npart:2 * npart]
        send_sems, recv_sems = refs[2 * npart:]
        x, y, c, _ = _place()
        cps = [pltpu.make_async_remote_copy(
            src_ref=ins[p].at[s, 1 - c], dst_ref=outs[p].at[s],
            send_sem=send_sems.at[s * npart + p], recv_sem=recv_sems.at[s * npart + p],
            device_id=(x, y, 1 - c), device_id_type=MESH) for s in range(NSHARD) for p in range(npart)]
        for cp in cps:
            cp.start()
        for cp in cps:
            cp.wait()

    return pl.pallas_call(
        body, name="swap_halves" + tag, in_specs=[ANY] * npart, out_specs=[ANY] * npart,
        out_shape=[jax.ShapeDtypeStruct((NSHARD,) + a.shape[2:], a.dtype) for a in parts],
        scratch_shapes=[pltpu.SemaphoreType.DMA((NSHARD * npart,)), pltpu.SemaphoreType.DMA((NSHARD * npart,))],
    )(*parts)


def _scatter_shards(parts):
    npart = len(parts)

    def body(*refs):
        ins, outs = refs[:npart], refs[npart:2 * npart]
        send_sems, recv_sems = refs[2 * npart:]
        x, y, c, chips = _place()
        cps = [pltpu.make_async_remote_copy(
            src_ref=ins[p].at[2 * cx + cy], dst_ref=outs[p].at[j],
            send_sem=send_sems.at[j * npart + p], recv_sem=recv_sems.at[j * npart + p],
            device_id=(cx, cy, c), device_id_type=MESH)
            for j, (cx, cy) in enumerate(chips) for p in range(npart)]
        for cp in cps:
            cp.start()
        for cp in cps:
            cp.wait()

    return pl.pallas_call(
        body, name="scatter_shards", in_specs=[ANY] * npart, out_specs=[ANY] * npart,
        out_shape=[jax.ShapeDtypeStruct((3,) + a.shape[1:], a.dtype) for a in parts],
        scratch_shapes=[pltpu.SemaphoreType.DMA((3 * npart,)), pltpu.SemaphoreType.DMA((3 * npart,))],
    )(*parts)


HBM = pl.BlockSpec(memory_space=pltpu.HBM)
SEM = pl.BlockSpec(memory_space=pltpu.SEMAPHORE)
EFFECT = pltpu.SideEffectType.DATAFLOW_SIDE_EFFECTING


def _scatter_copies(srcs, lands, send_sems, recv_sems, waiting):
    x, y, c, chips = _place()
    n = len(srcs)
    return [pltpu.make_async_remote_copy(
        src_ref=srcs[p].at[2 * cx + cy], dst_ref=lands[p].at[j],
        send_sem=send_sems.at[j * n + p], recv_sem=recv_sems.at[j * n + p],
        device_id=(cx, cy, c), device_id_type=MESH)
        for j, (cx, cy) in enumerate(chips) for p in range(n)]


def _proj_copies(srcs, lands, send_sems, recv_sems, waiting):
    x, y, c, chips = _place()
    mine = 2 * x + y
    return [pltpu.make_async_remote_copy(
        src_ref=srcs[0].at[c], dst_ref=lands[0].at[mine, c],
        send_sem=send_sems.at[2 * j + to], recv_sem=recv_sems.at[2 * j + (to if waiting else c)],
        device_id=(cx, cy, to), device_id_type=MESH)
        for j, (cx, cy) in enumerate(chips) for to in range(2)]


def _start_copies(copies, nsem, parts, lands, name, after=None):
    n = len(parts)
    extra = [] if after is None else [after]

    def body(*refs):
        outs = refs[2 * n + len(extra):]
        for cp in copies(refs[:n], refs[n:2 * n], outs[0], outs[1], False):
            cp.start()
        outs[-1][...] = jnp.zeros_like(outs[-1])

    res = pl.pallas_call(
        body, name=name,
        out_shape=(pltpu.SemaphoreType.DMA((nsem,)), pltpu.SemaphoreType.DMA((nsem,)),
                   *[pltpu.HBM(a.shape, a.dtype) for a in parts], *[pltpu.HBM(a.shape, a.dtype) for a in lands],
                   jax.ShapeDtypeStruct((8, 128), F32)),
        in_specs=[HBM] * (2 * n) + [ANY] * len(extra),
        out_specs=(SEM, SEM, *[HBM] * (2 * n), pl.BlockSpec(memory_space=pltpu.VMEM)),
        input_output_aliases={i: 2 + i for i in range(2 * n)},
        compiler_params=pltpu.CompilerParams(has_side_effects=EFFECT),
    )(*[pltpu.with_memory_space_constraint(a, pltpu.HBM) for a in parts],
      *[pltpu.with_memory_space_constraint(a, pltpu.HBM) for a in lands], *extra)
    return res[0], res[1], res[2:2 + n], res[2 + n:2 + 2 * n], res[-1]


def _wait_copies(copies, started, after, name):
    send_sems, recv_sems, srcs, lands, _ = started
    n = len(srcs)

    def body(*refs):
        for cp in copies(refs[:n], refs[n:2 * n], refs[2 * n], refs[2 * n + 1], True):
            cp.wait_send()
            cp.wait_recv()

    res = pl.pallas_call(
        body, name=name,
        out_shape=tuple(pltpu.HBM(a.shape, a.dtype) for a in (*srcs, *lands)),
        in_specs=[HBM] * (2 * n) + [SEM, SEM, ANY], out_specs=tuple([HBM] * (2 * n)),
        input_output_aliases={i: i for i in range(2 * n)},
        compiler_params=pltpu.CompilerParams(has_side_effects=EFFECT),
    )(*srcs, *lands, send_sems, recv_sems, after)
    return res[n:]


def _join_halves(parts, tag=""):
    npart = len(parts)

    def body(*refs):
        ins, outs = refs[:npart], refs[npart:2 * npart]
        send_sems, recv_sems = refs[2 * npart:]
        x, y, c, _ = _place()
        cps = [pltpu.make_async_remote_copy(
            src_ref=ins[p], dst_ref=outs[p], send_sem=send_sems.at[p], recv_sem=recv_sems.at[p],
            device_id=(x, y, 1 - c), device_id_type=MESH) for p in range(npart)]
        for cp in cps:
            cp.start()
        for cp in cps:
            cp.wait()

    return pl.pallas_call(
        body, name="join_halves" + tag, in_specs=[ANY] * npart, out_specs=[ANY] * npart,
        out_shape=[jax.ShapeDtypeStruct(a.shape, a.dtype) for a in parts],
        scratch_shapes=[pltpu.SemaphoreType.DMA((npart,)), pltpu.SemaphoreType.DMA((npart,))],
    )(*parts)


def _allreduce_small(v):
    r, ncol = v.shape

    def body(v_ref, o_ref, buf, send_sems, recv_sems):
        x, y, c, _ = _place()
        me = 4 * x + 2 * y + c
        buf[me] = v_ref[...]
        cps = []
        for k in range(1, 8):
            px, py, pc = x ^ (k >> 2), y ^ ((k >> 1) & 1), c ^ (k & 1)
            cps.append(pltpu.make_async_remote_copy(
                src_ref=v_ref, dst_ref=buf.at[me], send_sem=send_sems.at[k - 1], recv_sem=recv_sems.at[k - 1],
                device_id=(px, py, pc), device_id_type=MESH))
        for cp in cps:
            cp.start()
        for k in range(1, 8):
            px, py, pc = x ^ (k >> 2), y ^ ((k >> 1) & 1), c ^ (k & 1)
            pltpu.make_async_remote_copy(
                src_ref=v_ref, dst_ref=buf.at[4 * px + 2 * py + pc], send_sem=send_sems.at[k - 1],
                recv_sem=recv_sems.at[k - 1], device_id=(px, py, pc), device_id_type=MESH).wait_recv()
        for cp in cps:
            cp.wait_send()
        acc = buf[0]
        for d in range(1, 8):
            acc = acc + buf[d]
        o_ref[...] = acc

    return pl.pallas_call(
        body, name="allreduce_small",
        in_specs=[pl.BlockSpec(memory_space=pltpu.VMEM)], out_specs=pl.BlockSpec(memory_space=pltpu.VMEM),
        out_shape=jax.ShapeDtypeStruct((r, ncol), F32),
        scratch_shapes=[pltpu.VMEM((8, r, ncol), F32), pltpu.SemaphoreType.DMA((7,)), pltpu.SemaphoreType.DMA((7,))],
    )(v)


def _permute_rows(shards):
    w0, w1, w2, w3 = shards
    zeros = jnp.zeros((NPERM - 9280, w0.shape[1]), w0.dtype)
    return jnp.concatenate([w0, w1[0:1776], w1[1808:2320], w2, w3[0:240], w3[272:2320],
                            w1[1776:1808], w3[240:272], zeros], axis=0)


def _unpermute_rows(g):
    s1 = jnp.concatenate([g[2320:4096], g[9216:9248], g[4096:4608]], axis=0)
    s3 = jnp.concatenate([g[6928:7168], g[9248:9280], g[7168:9216]], axis=0)
    return jnp.stack([g[0:2320], s1, g[4608:6928], s3], axis=0)


def _pack_shard_small(conv, w2f, w2b):
    top = jnp.pad(conv, ((0, 8 - conv.shape[0]), (0, 0)))
    mid = jnp.pad(jnp.concatenate([w2f, w2b], axis=1), ((0, 0), (0, 768 - 256)))
    return jnp.concatenate([top, mid, jnp.zeros((8, 768), conv.dtype)], axis=0)


def _unpack_shard_small(a):
    return a[0:5], a[8:24, 0:128], a[8:24, 128:256]


def kernel(x, ln_pre_w, w_in, conv_w, a_log_fwd, a_log_bwd, dt_bias_fwd, dt_bias_bwd, gdn_norm_w, w_proj_gdn, gk_w2_fwd, gk_b2_fwd, gk_w2_bwd, gk_b2_bwd, gla_norm_w, w_proj_gla, w_out, ln_post_w, loss_target, m_ln_pre_w, m_w_in, m_conv_w, m_a_log_fwd, m_a_log_bwd, m_dt_bias_fwd, m_dt_bias_bwd, m_gdn_norm_w, m_w_proj_gdn, m_gk_w2_fwd, m_gk_b2_fwd, m_gk_w2_bwd, m_gk_b2_bwd, m_gla_norm_w, m_w_proj_gla, m_w_out, m_ln_post_w, v_ln_pre_w, v_w_in, v_conv_w, v_a_log_fwd, v_a_log_bwd, v_dt_bias_fwd, v_dt_bias_bwd, v_gdn_norm_w, v_w_proj_gdn, v_gk_w2_fwd, v_gk_b2_fwd, v_gk_w2_bwd, v_gk_b2_bwd, v_gla_norm_w, v_w_proj_gla, v_w_out, v_ln_post_w):
    t = x.shape[1]
    x2, tgt = x[0], loss_target[0]

    win_l = w_in[0].T.astype(BF16).reshape(2, SHW // 2, D)
    proj_l = jnp.concatenate([w_proj_gdn[0], w_proj_gla[0], w_out[0]], axis=0).astype(BF16).reshape(2, 384, D)
    small_l = _pack_shard_small(conv_w[0], gk_w2_fwd[0], gk_w2_bwd[0]).reshape(2, 16, 768)
    win_g, small_g = _gather_weights([win_l, small_l])
    proj_started = _start_copies(_proj_copies, 6, [proj_l], [lax.empty((NSHARD, 2, 384, D), BF16)],
                                 "gather_proj_start", after=small_g)
    wperm = _permute_rows(win_g.reshape(NSHARD, SHW, D))
    small_g = small_g.reshape(NSHARD, 32, 768)
    convw = small_g[:, 0:8, :].transpose(1, 0, 2).reshape(8, 3 * D)
    w2f = small_g[:, 8:24, 0:128].transpose(1, 0, 2).reshape(16, 512)
    w2b = small_g[:, 8:24, 128:256].transpose(1, 0, 2).reshape(16, 512)
    w2f_pad = jnp.pad(w2f, ((32, 80), (0, 0)))
    w2b_pad = jnp.pad(w2b, ((48, 64), (0, 0)))
    alog_row = jnp.pad(jnp.concatenate([a_log_fwd, a_log_bwd], axis=1), ((0, 0), (0, 112)))
    dt_row = jnp.pad(jnp.concatenate([dt_bias_fwd, dt_bias_bwd], axis=1), ((0, 0), (0, 112)))

    p, h = _inproj(x2, ln_pre_w + proj_started[4][0:1, 0:1], wperm)
    qn, kn, vc = (_qkv_fwd(p, convw, kind) for kind in range(3))
    gsm, gk = _gates_fwd(p, alog_row, dt_row, w2f_pad, gk_b2_fwd, w2b_pad, gk_b2_bwd)
    g2, b2 = _gcum_fwd(gsm)
    u, w, at, qd, kd, el, tinv = _gdn_intra_fwd(qn, kn, vc, g2, b2)
    oa2, sa = _gdn_scan_fwd(u, w, at, qd, kd, el)
    qg, kdb, intra, elb = _gla_intra_fwd(p, gk)
    ob2, sb = _gla_scan_fwd(p, qg, kdb, intra, elb)

    (proj_land,) = _wait_copies(_proj_copies, proj_started, ob2, "gather_proj_wait")
    mine = 2 * lax.axis_index("x") + lax.axis_index("y")
    w3 = lax.dynamic_update_index_in_dim(proj_land, proj_l, mine, 0).reshape(NSHARD, 3, D // NSHARD, D)
    (loss8, doa, dob, dz, dgb, dga, dgB, dyres, dw3, dgdn_w, dgla_w, dlnpost) = _post(
        oa2, ob2, p, x2, tgt, gdn_norm_w, gla_norm_w, ln_post_w, w3)

    g_proj = dw3.reshape(NSHARD, 2, 384, D)
    sum_proj = _sum_cast(g_proj, _swap_halves([g_proj], "_proj")[0])
    started_proj = _start_copies(_scatter_copies, 3, [sum_proj[1]], [lax.empty((3, 384, D), BF16)],
                                 "scatter_proj_start")
    du, dw, dat, dqd, dkd, del_ = _gdn_scan_bwd(u, w, at, qd, kd, el + started_proj[4][0, 0], sa, doa)
    dqn, dkn, dvc, dg2, db2 = _gdn_intra_bwd(qn, kn, vc, g2, b2, tinv, du, dw, dat, dqd, dkd, del_)
    dgsm = _gcum_bwd(gsm, dg2, db2)
    dqg, dkdb, dvs, delb = _gla_scan_bwd(p, qg, kdb, elb, sb, dob)
    dqb, dkb, dvb, dgk = _gla_intra_bwd(p, gk, dqg, dkdb, dvs, delb, dob)
    (dps, dalog8, ddt8, dw2f_pad, db2f8, dw2b_pad, db2b8) = _gates_bwd(
        p, alog_row, dt_row, w2f_pad, gk_b2_fwd, w2b_pad, gk_b2_bwd, dgsm, dgk)
    dpre, dconv = zip(*[_qkv_bwd(p, convw, g, kind) for kind, g in enumerate((dqn, dkn, dvc))])

    pieces = (jnp.concatenate([a.astype(BF16) for a in (*dpre, dz, dqb, dkb, dvb, dgb, dga, dgB, dps)], axis=1),)
    dwperm = _inproj_dw(h, pieces)

    g_in = _unpermute_rows(dwperm).reshape(NSHARD, 2, SHW // 2, D)
    dconv_full = jnp.concatenate(dconv, axis=1)
    dw2f, dw2b = dw2f_pad[32:48], dw2b_pad[48:64]
    g_small = jnp.stack([_pack_shard_small(dconv_full[0:5, 768 * s:768 * (s + 1)],
                                           dw2f[:, 128 * s:128 * (s + 1)], dw2b[:, 128 * s:128 * (s + 1)])
                         for s in range(NSHARD)])
    g_small = g_small.reshape(NSHARD, 2, 16, 768)
    parts = [g_in, g_small]
    got = _swap_halves(parts)
    sums = [_sum_cast(a, b) for a, b in zip(parts, got)]
    hbs = [hb for _, hb in sums]
    started = _start_copies(_scatter_copies, 3 * len(hbs), hbs,
                            [lax.empty((3,) + a.shape[1:], a.dtype) for a in hbs], "scatter_start")
    dx, dlnpre8 = _inproj_dx(pieces, wperm, x2, ln_pre_w + started[4][0:1, 0:1], dyres)

    gsmall = _allreduce_small(jnp.concatenate(
        [dlnpre8, dalog8, ddt8, dgdn_w, db2f8, db2b8, dgla_w, dlnpost, loss8], axis=1))
    smalls = dict(ln_pre_w=(ln_pre_w, m_ln_pre_w, v_ln_pre_w), a_log_fwd=(a_log_fwd, m_a_log_fwd, v_a_log_fwd),
                  a_log_bwd=(a_log_bwd, m_a_log_bwd, v_a_log_bwd),
                  dt_bias_fwd=(dt_bias_fwd, m_dt_bias_fwd, v_dt_bias_fwd),
                  dt_bias_bwd=(dt_bias_bwd, m_dt_bias_bwd, v_dt_bias_bwd),
                  gdn_norm_w=(gdn_norm_w, m_gdn_norm_w, v_gdn_norm_w),
                  gk_b2_fwd=(gk_b2_fwd, m_gk_b2_fwd, v_gk_b2_fwd), gk_b2_bwd=(gk_b2_bwd, m_gk_b2_bwd, v_gk_b2_bwd),
                  gla_norm_w=(gla_norm_w, m_gla_norm_w, v_gla_norm_w), ln_post_w=(ln_post_w, m_ln_post_w, v_ln_post_w))
    names = [name for name, _, _, _ in SMALL_SLOTS]
    small = _adam_small(gsmall, *([smalls[n][i] for n in names] for i in range(3)))

    landed_proj = _wait_copies(_scatter_copies, started_proj, small["ln_pre_w"][1], "scatter_proj_wait")
    half_proj = _sum4(sum_proj[0], landed_proj[0])
    their_proj = _join_halves([half_proj], "_proj")[0]
    a_pr = [_adam(w[0], half_proj, their_proj, m[0], v[0], 128, tile0=2 * i, name=f"proj{i}")
            for i, (w, m, v) in enumerate(((w_proj_gdn, m_w_proj_gdn, v_w_proj_gdn),
                                           (w_proj_gla, m_w_proj_gla, v_w_proj_gla), (w_out, m_w_out, v_w_out)))]

    landed = _wait_copies(_scatter_copies, started, a_pr[2][1], "scatter_wait")
    halves = [_sum4(f, g) for (f, _), g in zip(sums, landed)]
    theirs = _join_halves(halves)

    a_in = [a.T for a in _adam(w_in[0].T, halves[0], theirs[0], m_w_in[0].T, v_w_in[0].T, 232, name="in")]
    a_ss = _adam(_pack_shard_small(conv_w[0], gk_w2_fwd[0], gk_w2_bwd[0]), halves[1], theirs[1],
                 _pack_shard_small(m_conv_w[0], m_gk_w2_fwd[0], m_gk_w2_bwd[0]),
                 _pack_shard_small(v_conv_w[0], v_gk_w2_fwd[0], v_gk_w2_bwd[0]), 16, name="small")

    def family(k):
        conv, w2f_, w2b_ = _unpack_shard_small(a_ss[k])
        s = {n: small[n][k] for n in names}
        return [s["ln_pre_w"], a_in[k][None], conv[None], s["a_log_fwd"], s["a_log_bwd"], s["dt_bias_fwd"],
                s["dt_bias_bwd"], s["gdn_norm_w"], a_pr[0][k][None], w2f_[None], s["gk_b2_fwd"], w2b_[None],
                s["gk_b2_bwd"], s["gla_norm_w"], a_pr[1][k][None], a_pr[2][k][None], s["ln_post_w"]]

    return (gsmall[0, SMALL_W - 128], dx[None], *family(0), *family(1), *family(2), *family(3))
```

```python
import functools

import jax
import jax.numpy as jnp
from jax import lax
from jax.experimental import pallas as pl
from jax.experimental.pallas import tpu as pltpu

F32 = jnp.float32
BF16 = jnp.bfloat16
HI = lax.Precision.HIGHEST
MESH = pl.DeviceIdType.MESH

D = 1024
CH = 64
EPS = 1e-6
NA, DA = 8, 128
NB, DKB, DVB = 4, 128, 256
NSHARD = 4
SHW = 2320
NPERM = 9728
PS_BLOCK = 72
LR, B1, B2, ADAM_EPS, WD, STEP = 0.001, 0.9, 0.999, 1e-08, 0.01, 10

ANY = pl.BlockSpec(memory_space=pl.ANY)


def _cparams(sem=None, vmem_mb=48):
    return pltpu.CompilerParams(dimension_semantics=sem, vmem_limit_bytes=vmem_mb << 20)


def _bdot(a, b, ca, cb):
    return lax.dot_general(a.astype(BF16), b.astype(BF16), (((ca,), (cb,)), ((), ())),
                           preferred_element_type=F32)


@jax.custom_vjp
def mm(a, b):
    return _bdot(a, b, 1, 0)


def _mm_fwd(a, b):
    return _bdot(a, b, 1, 0), (a, b)


def _mm_bwd(res, g):
    a, b = res
    return _bdot(g, b, 1, 1), _bdot(a, g, 0, 0)


mm.defvjp(_mm_fwd, _mm_bwd)


@jax.custom_vjp
def mm_nt(a, b):
    return _bdot(a, b, 1, 1)


def _mm_nt_fwd(a, b):
    return _bdot(a, b, 1, 1), (a, b)


def _mm_nt_bwd(res, g):
    a, b = res
    return _bdot(g, b, 1, 0), _bdot(g, a, 0, 0)


mm_nt.defvjp(_mm_nt_fwd, _mm_nt_bwd)


@jax.custom_vjp
def mm_tn(a, b):
    return _bdot(a, b, 0, 0)


def _mm_tn_fwd(a, b):
    return _bdot(a, b, 0, 0), (a, b)


def _mm_tn_bwd(res, g):
    a, b = res
    return _bdot(b, g, 1, 1), _bdot(a, g, 1, 0)


mm_tn.defvjp(_mm_tn_fwd, _mm_tn_bwd)


def dot_hi(a, b):
    return lax.dot_general(a, b, (((1,), (0,)), ((), ())), precision=HI, preferred_element_type=F32)


def _split3(x):
    x1 = x.astype(BF16)
    r = x - x1.astype(F32)
    x2 = r.astype(BF16)
    return x1, x2, (r - x2.astype(F32)).astype(BF16)


def _cdot(c, x, cc, cx, c_first=True):
    parts = _split3(x)
    if c_first:
        return _bdot(c, parts[0], cc, cx) + _bdot(c, parts[1], cc, cx) + _bdot(c, parts[2], cc, cx)
    return _bdot(parts[0], c, cx, cc) + _bdot(parts[1], c, cx, cc) + _bdot(parts[2], c, cx, cc)


@jax.custom_vjp
def cmm(c, x):
    return _cdot(c, x, 1, 0)


def _cmm_fwd(c, x):
    return _cdot(c, x, 1, 0), c


def _cmm_bwd(c, g):
    return jnp.zeros_like(c), _cdot(c, g, 0, 0)


cmm.defvjp(_cmm_fwd, _cmm_bwd)


@jax.custom_vjp
def mmc(x, c):
    return _cdot(c, x, 0, 1, c_first=False)


def _mmc_fwd(x, c):
    return _cdot(c, x, 0, 1, c_first=False), c


def _mmc_bwd(c, g):
    return _cdot(c, g, 1, 1, c_first=False), jnp.zeros_like(c)


mmc.defvjp(_mmc_fwd, _mmc_bwd)


def _sigmoid(x):
    return 1.0 / (1.0 + jnp.exp(-x))


def _silu(x):
    return x * _sigmoid(x)


def _softplus(x):
    return jnp.maximum(x, 0.0) + jnp.log(1.0 + jnp.exp(-jnp.abs(x)))


def _rms(x, w):
    return x * lax.rsqrt(jnp.mean(x * x, axis=-1, keepdims=True) + EPS) * w


SC = 256


class _Consts:
    def __init__(self, rev):
        r = lax.broadcasted_iota(jnp.int32, (SC, SC), 0)
        c = lax.broadcasted_iota(jnp.int32, (SC, SC), 1)
        same = (r >> 6) == (c >> 6)
        a = jnp.where(rev, c, r)
        b = jnp.where(rev, r, c)
        self.incl = same & (a >= b)
        self.strict = same & (a > b)
        self.incl_f = self.incl.astype(F32)
        self.eye = (r == c).astype(F32)
        rows = lax.broadcasted_iota(jnp.int32, (SC, 1), 0)
        self.last_col = ((rows & (CH - 1)) == jnp.where(rev, 0, CH - 1)).astype(F32)
        rr = lax.broadcasted_iota(jnp.int32, (SC, CH), 0)
        cc = lax.broadcasted_iota(jnp.int32, (SC, CH), 1)
        self.fold = ((rr & (CH - 1)) == cc).astype(F32)


def _dot3(a, b, ca=1, cb=0):
    ah, bh = a.astype(BF16), b.astype(BF16)
    al, bl = (a - ah.astype(F32)).astype(BF16), (b - bh.astype(F32)).astype(BF16)
    return _bdot(ah, bh, ca, cb) + (_bdot(ah, bl, ca, cb) + _bdot(al, bh, ca, cb))


TRI_SPLIT_LEVELS = 2


def _tri_inv(low, eye):
    n = -low
    acc = eye + n
    p = n
    for level in range(5):
        dot = _dot3 if level < TRI_SPLIT_LEVELS else (lambda a, b: _bdot(a, b, 1, 0))
        p = dot(p, p)
        acc = acc + dot(acc, p)
    return acc


@jax.custom_vjp
def _solve2(low, rv, rk, tinv):
    x = _dot3(tinv, jnp.concatenate([rv, rk], axis=1))
    return x[:, :DA], x[:, DA:]


def _solve2_fwd(low, rv, rk, tinv):
    x = _dot3(tinv, jnp.concatenate([rv, rk], axis=1))
    return (x[:, :DA], x[:, DA:]), (x, tinv)


def _solve2_bwd(res, g):
    x, tinv = res
    drhs = _dot3(tinv, jnp.concatenate(g, axis=1), 0, 0)
    return -_dot3(drhs, x, 1, 1), drhs[:, :DA], drhs[:, DA:], jnp.zeros_like(tinv)


_solve2.defvjp(_solve2_fwd, _solve2_bwd)


def _chunk_last(x, cs):
    xs = (x * cs.last_col).reshape(SC // CH, CH, x.shape[1])
    return jnp.broadcast_to(jnp.sum(xs, axis=1, keepdims=True), xs.shape).reshape(x.shape)


def _gdn_decay(g, cs):
    gw = jnp.concatenate([g] * (SC // DA), axis=1)
    grow = jnp.sum(cs.eye * gw, axis=0, keepdims=True)
    return jnp.where(cs.incl, jnp.exp(jnp.where(cs.incl, gw - grow, 0.0)), 0.0)


def _gdn_intra(q, k, v, g, bx, tinv, cs):
    decay = _gdn_decay(g, cs)
    kb = k * bx
    low = jnp.where(cs.strict, mm_nt(kb, k) * decay, 0.0)
    eg = jnp.exp(g)
    made = tinv is None
    if made:
        tinv = _tri_inv(low, cs.eye)
    u, w = _solve2(low, v * bx, kb * eg, tinv)
    attn = mmc(mm_nt(q, k) * decay, cs.fold)
    qd = q * eg
    glast = _chunk_last(g, cs)
    kd = k * jnp.exp(glast - g)
    outs = (u, w, attn, qd, kd, jnp.exp(glast))
    return outs + (tinv,) if made else outs


def _gdn_scan(u, w, attn, qd, kd, el, s):
    vn = u - mm(w, s)
    o = mm(qd, s) + mm(attn, vn)
    sn = s * el + mm_tn(kd, vn)
    return o, sn


def _gla_intra(q, k, v, gk, cs):
    gc = cmm(cs.incl_f, gk)
    qg = q * (DKB ** -0.5) * jnp.exp(gc)
    kg = k * jnp.exp(-gc)
    attn = jnp.where(cs.incl, mm_nt(qg, kg), 0.0)
    intra = mm(attn, v)
    glast = _chunk_last(gc, cs)
    kd = k * jnp.exp(glast - gc)
    return qg, kd, intra, jnp.exp(glast)


def _gla_scan(qg, kd, v, el, st):
    o = mm_nt(qg, st)
    stn = st * el + mm_tn(v, kd)
    return o, stn


def _shift_rows(x, s):
    if s == 0:
        return x
    t = x.shape[0]
    rolled = pltpu.roll(x, (-s) % t, 0)
    rows = lax.broadcasted_iota(jnp.int32, x.shape, 0)
    return jnp.where((rows + s >= 0) & (rows + s < t), rolled, 0.0)


@jax.custom_vjp
def _conv5(x, w):
    acc = w[0:1] * _shift_rows(x, -2)
    for j in range(1, 5):
        acc = acc + w[j:j + 1] * _shift_rows(x, j - 2)
    return acc


def _conv5_fwd(x, w):
    return _conv5(x, w), (x, w)


def _conv5_bwd(res, g):
    x, w = res
    dx = w[0:1] * _shift_rows(g, 2)
    for j in range(1, 5):
        dx = dx + w[j:j + 1] * _shift_rows(g, 2 - j)
    rows = lax.broadcasted_iota(jnp.int32, w.shape, 0)
    dw = jnp.zeros_like(w)
    for j in range(5):
        dwj = jnp.sum(g * _shift_rows(x, j - 2), axis=0, keepdims=True)
        dw = dw + jnp.where(rows == j, dwj, 0.0)
    return dx, dw


_conv5.defvjp(_conv5_fwd, _conv5_bwd)


def _qkv_act(kind):
    def f(x, w):
        c = _silu(_conv5(x, w))
        if kind == 2:
            return c
        c = c * lax.rsqrt(jnp.sum(c * c, axis=-1, keepdims=True) + EPS)
        return c * (DA ** -0.5) if kind == 0 else c
    return f


def _inproj(x, lnw, wperm, tn=512):
    t = x.shape[0]
    tm = min(t, 2048)

    def body(x_ref, lnw_ref, w_ref, p_ref, h_ref, hbuf):
        @pl.when(pl.program_id(1) == 0)
        def _():
            hb = _rms(x_ref[...], lnw_ref[...]).astype(BF16)
            hbuf[...] = hb
            h_ref[...] = hb
        p_ref[...] = _bdot(hbuf[...], w_ref[...], 1, 1)

    return pl.pallas_call(
        body, name="inproj", grid=(t // tm, NPERM // tn),
        in_specs=[pl.BlockSpec((tm, D), lambda i, j: (i, 0)),
                  pl.BlockSpec((1, D), lambda i, j: (0, 0)),
                  pl.BlockSpec((tn, D), lambda i, j: (j, 0))],
        out_specs=[pl.BlockSpec((tm, tn), lambda i, j: (i, j)),
                   pl.BlockSpec((tm, D), lambda i, j: (i, 0))],
        out_shape=[jax.ShapeDtypeStruct((t, NPERM), F32),
                   jax.ShapeDtypeStruct((t, D), BF16)],
        scratch_shapes=[pltpu.VMEM((tm, D), BF16)],
        compiler_params=_cparams(("parallel", "arbitrary")),
    )(x, lnw, wperm)


DP_TILE = 512
DP_PIECES = ((0, 19),)


def _piece_specs(tm, j_first):
    specs = []
    for j0, n in DP_PIECES:
        def imap(a, b, j0=j0, n=n):
            j, i = (a, b) if j_first else (b, a)
            inside = (j >= j0) & (j < j0 + n)
            return jnp.where(inside, i, 0), jnp.clip(j - j0, 0, n - 1)
        specs.append(pl.BlockSpec((tm, DP_TILE), imap))
    return specs


def _for_piece(j, refs, fn):
    for (j0, n), ref in zip(DP_PIECES, refs):
        @pl.when((j >= j0) & (j < j0 + n))
        def _(ref=ref):
            fn(ref[...])


def _inproj_dw(h, pieces):
    t = h.shape[0]
    tm = min(t, 2048)
    npc = len(pieces)

    def body(h_ref, *refs):
        dw_ref = refs[npc]

        @pl.when(pl.program_id(1) == 0)
        def _():
            dw_ref[...] = jnp.zeros_like(dw_ref)

        def add(dp):
            dw_ref[...] += _bdot(dp, h_ref[...], 0, 0)
        _for_piece(pl.program_id(0), refs[:npc], add)

    return pl.pallas_call(
        body, name="inproj_dw", grid=(NPERM // DP_TILE, t // tm),
        in_specs=[pl.BlockSpec((tm, D), lambda j, i: (i, 0))] + _piece_specs(tm, True),
        out_specs=pl.BlockSpec((DP_TILE, D), lambda j, i: (j, 0)),
        out_shape=jax.ShapeDtypeStruct((NPERM, D), F32),
        compiler_params=_cparams(("parallel", "arbitrary")),
    )(h, *pieces)


def _inproj_dx(pieces, wperm, x, lnw, dyres):
    t = x.shape[0]
    tm = min(t, 1024)
    tn = DP_TILE
    nj = NPERM // tn
    npc = len(pieces)

    def body(*refs):
        w_ref, x_ref, lnw_ref, dy_ref, dx_ref, dlnw_ref, acc = refs[npc:]
        j = pl.program_id(1)

        @pl.when(j == 0)
        def _():
            acc[...] = jnp.zeros_like(acc)

        def add(dp):
            acc[...] += _bdot(dp, w_ref[...], 1, 0)
        _for_piece(j, refs[:npc], add)

        @pl.when(j == nj - 1)
        def _():
            _, vjp = jax.vjp(_rms, x_ref[...], lnw_ref[...])
            dx, dlnw = vjp(acc[...])
            dx_ref[...] = dx + dy_ref[...]

            @pl.when(pl.program_id(0) == 0)
            def _():
                dlnw_ref[...] = jnp.zeros_like(dlnw_ref)
            dlnw_ref[...] += jnp.broadcast_to(dlnw, dlnw_ref.shape)

    return pl.pallas_call(
        body, name="inproj_dx", grid=(t // tm, nj),
        in_specs=_piece_specs(tm, False) + [
                  pl.BlockSpec((tn, D), lambda i, j: (j, 0)),
                  pl.BlockSpec((tm, D), lambda i, j: (i, 0)),
                  pl.BlockSpec((1, D), lambda i, j: (0, 0)),
                  pl.BlockSpec((tm, D), lambda i, j: (i, 0))],
        out_specs=[pl.BlockSpec((tm, D), lambda i, j: (i, 0)),
                   pl.BlockSpec((8, D), lambda i, j: (0, 0))],
        out_shape=[jax.ShapeDtypeStruct((t, D), F32), jax.ShapeDtypeStruct((8, D), F32)],
        scratch_shapes=[pltpu.VMEM((tm, D), F32)],
        compiler_params=_cparams(("arbitrary", "arbitrary")),
    )(*pieces, wperm, x, lnw, dyres)


def _qkv_fwd(p, convw, kind):
    t = p.shape[0]
    f = _qkv_act(kind)

    def body(p_ref, w_ref, o_ref):
        o_ref[...] = f(p_ref[...], w_ref[...])

    return pl.pallas_call(
        body, name=f"qkv_fwd{kind}", grid=(NA,),
        in_specs=[pl.BlockSpec((t, DA), lambda h: (0, kind * NA + h)),
                  pl.BlockSpec((8, DA), lambda h: (0, kind * NA + h))],
        out_specs=pl.BlockSpec((t, DA), lambda h: (0, h)),
        out_shape=jax.ShapeDtypeStruct((t, D), F32),
        compiler_params=_cparams(("parallel",)),
    )(p, convw)


def _qkv_bwd(p, convw, dout, kind):
    t = p.shape[0]
    f = _qkv_act(kind)

    def body(p_ref, w_ref, g_ref, dx_ref, dw_ref):
        _, vjp = jax.vjp(f, p_ref[...], w_ref[...])
        dx, dw = vjp(g_ref[...])
        dx_ref[...] = dx.astype(BF16)
        dw_ref[...] = dw

    return pl.pallas_call(
        body, name=f"qkv_bwd{kind}", grid=(NA,),
        in_specs=[pl.BlockSpec((t, DA), lambda h: (0, kind * NA + h)),
                  pl.BlockSpec((8, DA), lambda h: (0, kind * NA + h)),
                  pl.BlockSpec((t, DA), lambda h: (0, h))],
        out_specs=[pl.BlockSpec((t, DA), lambda h: (0, h)),
                   pl.BlockSpec((8, DA), lambda h: (0, h))],
        out_shape=[jax.ShapeDtypeStruct((t, D), BF16), jax.ShapeDtypeStruct((8, D), F32)],
        compiler_params=_cparams(("parallel",)),
    )(p, convw, dout)


def _gates_f(ps, alog_row, dt_row, w2f, b2f, w2b, b2b):
    lane = lax.broadcasted_iota(jnp.int32, ps.shape, 1)
    lg = -jnp.exp(alog_row) * _softplus(ps + dt_row)
    gsm = jnp.where(lane < 16, lg, jnp.where(lane < 32, _sigmoid(ps), 0.0))
    gkf = -_softplus(-(mm(ps, w2f) + b2f)) * (1.0 / 16.0)
    gkb = -_softplus(-(mm(ps, w2b) + b2b)) * (1.0 / 16.0)
    return gsm, gkf, gkb


def _gates_fwd(ps, alog_row, dt_row, w2f, b2f, w2b, b2b, tm=512):
    t = ps.shape[0]

    def body(ps_ref, a_ref, d_ref, wf_ref, bf_ref, wb_ref, bb_ref, gsm_ref, gk_ref):
        gsm, gkf, gkb = _gates_f(ps_ref[...], a_ref[...], d_ref[...], wf_ref[...], bf_ref[...],
                                 wb_ref[...], bb_ref[...])
        gsm_ref[...] = gsm
        gk_ref[0] = gkf
        gk_ref[1] = gkb

    row = lambda n: pl.BlockSpec((1, n), lambda i: (0, 0))
    mat = pl.BlockSpec((128, 512), lambda i: (0, 0))
    return pl.pallas_call(
        body, name="gates_fwd", grid=(t // tm,),
        in_specs=[pl.BlockSpec((tm, 128), lambda i: (i, PS_BLOCK)), row(128), row(128), mat, row(512), mat, row(512)],
        out_specs=[pl.BlockSpec((tm, 128), lambda i: (i, 0)),
                   pl.BlockSpec((2, tm, 512), lambda i: (0, i, 0))],
        out_shape=[jax.ShapeDtypeStruct((t, 128), F32), jax.ShapeDtypeStruct((2, t, 512), F32)],
        compiler_params=_cparams(("parallel",)),
    )(ps, alog_row, dt_row, w2f, b2f, w2b, b2b)


def _gates_bwd(ps, alog_row, dt_row, w2f, b2f, w2b, b2b, dgsm, dgk, tm=512):
    t = ps.shape[0]

    def body(ps_ref, a_ref, d_ref, wf_ref, bf_ref, wb_ref, bb_ref, dgsm_ref, dgk_ref,
             dps_ref, da_ref, dd_ref, dwf_ref, dbf_ref, dwb_ref, dbb_ref):
        _, vjp = jax.vjp(_gates_f, ps_ref[...], a_ref[...], d_ref[...], wf_ref[...], bf_ref[...],
                         wb_ref[...], bb_ref[...])
        dps, da, dd, dwf, dbf, dwb, dbb = vjp((dgsm_ref[...], dgk_ref[0], dgk_ref[1]))
        dps_ref[:, 0:128] = dps.astype(BF16)
        dps_ref[:, 128:DP_TILE] = jnp.zeros((tm, DP_TILE - 128), BF16)
        accs = ((da_ref, da), (dd_ref, dd), (dwf_ref, dwf), (dbf_ref, dbf), (dwb_ref, dwb), (dbb_ref, dbb))

        @pl.when(pl.program_id(0) == 0)
        def _():
            for ref, _ in accs:
                ref[...] = jnp.zeros_like(ref)
        for ref, val in accs:
            ref[...] += jnp.broadcast_to(val, ref.shape)

    row = lambda n: pl.BlockSpec((1, n), lambda i: (0, 0))
    row8 = lambda n: pl.BlockSpec((8, n), lambda i: (0, 0))
    mat = pl.BlockSpec((128, 512), lambda i: (0, 0))
    return pl.pallas_call(
        body, name="gates_bwd", grid=(t // tm,),
        in_specs=[pl.BlockSpec((tm, 128), lambda i: (i, PS_BLOCK)), row(128), row(128), mat, row(512), mat, row(512),
                  pl.BlockSpec((tm, 128), lambda i: (i, 0)),
                  pl.BlockSpec((2, tm, 512), lambda i: (0, i, 0))],
        out_specs=[pl.BlockSpec((tm, DP_TILE), lambda i: (i, 0)), row8(128), row8(128), mat, row8(512), mat,
                   row8(512)],
        out_shape=[jax.ShapeDtypeStruct((t, DP_TILE), BF16),
                   jax.ShapeDtypeStruct((8, 128), F32), jax.ShapeDtypeStruct((8, 128), F32),
                   jax.ShapeDtypeStruct((128, 512), F32), jax.ShapeDtypeStruct((8, 512), F32),
                   jax.ShapeDtypeStruct((128, 512), F32), jax.ShapeDtypeStruct((8, 512), F32)],
        compiler_params=_cparams(("arbitrary",)),
    )(ps, alog_row, dt_row, w2f, b2f, w2b, b2b, dgsm, dgk)


def _rows(i):
    return pl.ds(pl.multiple_of(i * CH, CH), CH)


def _srows(i):
    return pl.ds(pl.multiple_of(i * SC, SC), SC)


def _first_row(x):
    row = lax.broadcasted_iota(jnp.int32, (8, x.shape[1]), 0)
    return jnp.where(row == 0, jnp.broadcast_to(x, (8, x.shape[1])), 0.0)


def _chunk_rows(e_ref, i):
    pad = jnp.zeros((CH - 8, 128), F32)
    return jnp.concatenate([x for c in range(SC // CH) for x in (e_ref[(SC // CH) * i + c], pad)], axis=0)


def _gcum_f(gsm, tm):
    i = lax.broadcasted_iota(jnp.int32, (tm, tm), 0)
    j = lax.broadcasted_iota(jnp.int32, (tm, tm), 1)
    same = (i >> 6) == (j >> 6)
    lower = (same & (i >= j)).astype(F32)
    upper = (same & (i <= j)).astype(F32)
    r = lax.broadcasted_iota(jnp.int32, (128, D), 0)
    head = lax.broadcasted_iota(jnp.int32, (128, D), 1) >> 7
    pick = lambda off: (r == head + off).astype(F32)
    lane = lax.broadcasted_iota(jnp.int32, gsm.shape, 1)
    run = jnp.where(lane < 8, cmm(lower, gsm), cmm(upper, gsm))
    return mmc(run, pick(0)), mmc(run, pick(8)), mmc(gsm, pick(16)), mmc(gsm, pick(24))


def _gcum_fwd(gsm, tm=256):
    t = gsm.shape[0]

    def body(s_ref, g_ref, b_ref):
        gf, gb, bf, bb = _gcum_f(s_ref[...], tm)
        g_ref[0] = gf
        g_ref[1] = gb
        b_ref[0] = bf
        b_ref[1] = bb

    two = pl.BlockSpec((2, tm, D), lambda i: (0, i, 0))
    return pl.pallas_call(
        body, name="gcum_fwd", grid=(t // tm,),
        in_specs=[pl.BlockSpec((tm, 128), lambda i: (i, 0))], out_specs=[two, two],
        out_shape=[jax.ShapeDtypeStruct((2, t, D), F32)] * 2,
        compiler_params=_cparams(("parallel",)),
    )(gsm)


def _gcum_bwd(gsm, dg2, db2, tm=256):
    t = gsm.shape[0]

    def body(s_ref, dg_ref, db_ref, ds_ref):
        _, vjp = jax.vjp(lambda s: _gcum_f(s, tm), s_ref[...])
        ds_ref[...] = vjp((dg_ref[0], dg_ref[1], db_ref[0], db_ref[1]))[0]

    two = pl.BlockSpec((2, tm, D), lambda i: (0, i, 0))
    tile = pl.BlockSpec((tm, 128), lambda i: (i, 0))
    return pl.pallas_call(
        body, name="gcum_bwd", grid=(t // tm,),
        in_specs=[tile, two, two], out_specs=tile,
        out_shape=jax.ShapeDtypeStruct((t, 128), F32),
        compiler_params=_cparams(("parallel",)),
    )(gsm, dg2, db2)


def _gdn_intra_fwd(qn, kn, vc, g2, b2):
    t = qn.shape[0]
    n = t // CH

    def body(q_ref, k_ref, v_ref, g_ref, b_ref, u_ref, w_ref, a_ref, qd_ref, kd_ref, e_ref, t_ref):
        cs = _Consts(pl.program_id(0) == 1)

        def step(i, carry):
            r = _srows(i)
            q, k, v, g, bx = q_ref[r, :], k_ref[r, :], v_ref[r, :], g_ref[r, :], b_ref[r, :]
            u, w, a, qd, kd, el, tinv = _gdn_intra(q, k, v, g, bx, None, cs)
            u_ref[r, :] = u
            w_ref[r, :] = w
            a_ref[r, :] = a
            qd_ref[r, :] = qd
            kd_ref[r, :] = kd
            t_ref[r, :] = tinv
            for c in range(SC // CH):
                e_ref[(SC // CH) * i + c] = el[c * CH:c * CH + 8]
            return carry

        lax.fori_loop(0, t // SC, step, 0)

    head = pl.BlockSpec((t, DA), lambda d, h: (0, h))
    dh = pl.BlockSpec((None, t, DA), lambda d, h: (d, 0, h))
    sq = lambda w: pl.BlockSpec((None, None, t, w), lambda d, h: (d, h, 0, 0))
    big = jax.ShapeDtypeStruct((2, t, D), F32)
    return pl.pallas_call(
        body, name="gdn_intra_fwd", grid=(2, NA),
        in_specs=[head, head, head, dh, dh],
        out_specs=[dh, dh, sq(CH), dh, dh, pl.BlockSpec((None, None, n, 8, 128), lambda d, h: (d, h, 0, 0, 0)),
                   sq(SC)],
        out_shape=[big, big, jax.ShapeDtypeStruct((2, NA, t, CH), F32), big, big,
                   jax.ShapeDtypeStruct((2, NA, n, 8, 128), F32), jax.ShapeDtypeStruct((2, NA, t, SC), F32)],
        compiler_params=_cparams(("parallel", "parallel")),
    )(qn, kn, vc, g2, b2)


SCAN_TB = 256
SCAN_HB = 8


def _scan_specs(t, width, nheads, hb, along):
    nt = t // SCAN_TB
    nb = SCAN_TB // CH

    def tmap(d, tt):
        fwd = tt + d * (nt - 1 - 2 * tt)
        return fwd if along > 0 else nt - 1 - fwd

    tok = pl.BlockSpec((None, SCAN_TB, hb * width), lambda d, h, tt: (d, tmap(d, tt), h))
    per = lambda *tail: pl.BlockSpec((None, hb, nb) + tail, lambda d, h, tt: (d, h, tmap(d, tt)) + (0,) * len(tail))
    sq = pl.BlockSpec((None, hb, SCAN_TB, CH), lambda d, h, tt: (d, h, tmap(d, tt), 0))
    shared = lambda w: pl.BlockSpec((SCAN_TB, hb * w), lambda d, h, tt: (tmap(d, tt), h))
    return tok, per, sq, shared, (2, nheads // hb, nt), nb


def _gdn_scan_fwd(u, w, a, qd, kd, e):
    t = u.shape[1]
    tok, per, sq, _, grid, nb = _scan_specs(t, DA, NA, SCAN_HB, +1)

    def body(u_ref, w_ref, a_ref, qd_ref, kd_ref, e_ref, o_ref, s_ref, state):
        rev = pl.program_id(0) == 1

        @pl.when(pl.program_id(2) == 0)
        def _():
            state[...] = jnp.zeros_like(state)

        def step(i, ss):
            ci = jnp.where(rev, nb - 1 - i, i)
            r = _rows(ci)
            out = []
            for hh, s in enumerate(ss):
                c = slice(hh * DA, (hh + 1) * DA)
                s_ref[hh, ci] = s
                o, sn = _gdn_scan(u_ref[r, c], w_ref[r, c], a_ref[hh, r, :], qd_ref[r, c], kd_ref[r, c],
                                  e_ref[hh, ci][0:1], s)
                o_ref[r, c] = o
                out.append(sn)
            return tuple(out)

        ss = lax.fori_loop(0, nb, step, tuple(state[hh] for hh in range(SCAN_HB)))
        for hh, s in enumerate(ss):
            state[hh] = s

    return pl.pallas_call(
        body, name="gdn_scan_fwd", grid=grid,
        in_specs=[tok, tok, sq, tok, tok, per(8, 128)],
        out_specs=[tok, per(DA, DA)],
        out_shape=[jax.ShapeDtypeStruct((2, t, D), F32), jax.ShapeDtypeStruct((2, NA, t // CH, DA, DA), F32)],
        scratch_shapes=[pltpu.VMEM((SCAN_HB, DA, DA), F32)],
        compiler_params=_cparams(("parallel", "parallel", "arbitrary")),
    )(u, w, a, qd, kd, e)


def _gdn_scan_bwd(u, w, a, qd, kd, e, ssave, do):
    t = u.shape[1]
    tok, per, sq, shared, grid, nb = _scan_specs(t, DA, NA, SCAN_HB, -1)

    def body(u_ref, w_ref, a_ref, qd_ref, kd_ref, e_ref, s_ref, do_ref,
             du_ref, dw_ref, da_ref, dqd_ref, dkd_ref, de_ref, state):
        rev = pl.program_id(0) == 1

        @pl.when(pl.program_id(2) == 0)
        def _():
            state[...] = jnp.zeros_like(state)

        def step(i, dss):
            ci = jnp.where(rev, i, nb - 1 - i)
            r = _rows(ci)
            out = []
            for hh, ds in enumerate(dss):
                c = slice(hh * DA, (hh + 1) * DA)
                _, vjp = jax.vjp(_gdn_scan, u_ref[r, c], w_ref[r, c], a_ref[hh, r, :], qd_ref[r, c], kd_ref[r, c],
                                 e_ref[hh, ci][0:1], s_ref[hh, ci])
                du, dw, da, dqd, dkd, de, dsn = vjp((do_ref[r, c], ds))
                du_ref[r, c] = du
                dw_ref[r, c] = dw
                da_ref[hh, r, :] = da
                dqd_ref[r, c] = dqd
                dkd_ref[r, c] = dkd
                de_ref[hh, ci] = _first_row(de)
                out.append(dsn)
            return tuple(out)

        dss = lax.fori_loop(0, nb, step, tuple(state[hh] for hh in range(SCAN_HB)))
        for hh, ds in enumerate(dss):
            state[hh] = ds

    big = jax.ShapeDtypeStruct((2, t, D), F32)
    return pl.pallas_call(
        body, name="gdn_scan_bwd", grid=grid,
        in_specs=[tok, tok, sq, tok, tok, per(8, 128), per(DA, DA), shared(DA)],
        out_specs=[tok, tok, sq, tok, tok, per(8, 128)],
        out_shape=[big, big, jax.ShapeDtypeStruct((2, NA, t, CH), F32), big, big,
                   jax.ShapeDtypeStruct((2, NA, t // CH, 8, 128), F32)],
        scratch_shapes=[pltpu.VMEM((SCAN_HB, DA, DA), F32)],
        compiler_params=_cparams(("parallel", "parallel", "arbitrary")),
    )(u, w, a, qd, kd, e, ssave, do)


def _gdn_intra_bwd(qn, kn, vc, g2, b2, tinv, du, dw, da, dqd, dkd, de):
    t = qn.shape[0]
    n = t // CH

    def body(q_ref, k_ref, v_ref, g_ref, b_ref, t_ref, du_ref, dw_ref, da_ref, dqd_ref, dkd_ref, de_ref,
             dq_ref, dk_ref, dv_ref, dg_ref, db_ref):
        d = pl.program_id(1)
        cs = _Consts(d == 1)

        @pl.when(d == 0)
        def _():
            dq_ref[...] = jnp.zeros_like(dq_ref)
            dk_ref[...] = jnp.zeros_like(dk_ref)
            dv_ref[...] = jnp.zeros_like(dv_ref)

        def step(i, carry):
            r = _srows(i)
            tinv_c = t_ref[r, :]
            f = lambda q, k, v, g, bx: _gdn_intra(q, k, v, g, bx, tinv_c, cs)
            _, vjp = jax.vjp(f, q_ref[r, :], k_ref[r, :], v_ref[r, :], g_ref[r, :], b_ref[r, :])
            dq, dk, dv, dg, dbx = vjp((du_ref[r, :], dw_ref[r, :], da_ref[r, :], dqd_ref[r, :],
                                       dkd_ref[r, :], _chunk_rows(de_ref, i)))
            dq_ref[r, :] += dq
            dk_ref[r, :] += dk
            dv_ref[r, :] += dv
            dg_ref[r, :] = dg
            db_ref[r, :] = dbx
            return carry

        lax.fori_loop(0, t // SC, step, 0)

    head = pl.BlockSpec((t, DA), lambda h, d: (0, h))
    dh = pl.BlockSpec((None, t, DA), lambda h, d: (d, 0, h))
    sq = pl.BlockSpec((None, None, t, CH), lambda h, d: (d, h, 0, 0))
    tq = pl.BlockSpec((None, None, t, SC), lambda h, d: (d, h, 0, 0))
    full = jax.ShapeDtypeStruct((t, D), F32)
    big = jax.ShapeDtypeStruct((2, t, D), F32)
    return pl.pallas_call(
        body, name="gdn_intra_bwd", grid=(NA, 2),
        in_specs=[head, head, head, dh, dh, tq, dh, dh, sq, dh, dh,
                  pl.BlockSpec((None, None, n, 8, 128), lambda h, d: (d, h, 0, 0, 0))],
        out_specs=[head, head, head, dh, dh],
        out_shape=[full, full, full, big, big],
        compiler_params=_cparams(("arbitrary", "arbitrary")),
    )(qn, kn, vc, g2, b2, tinv, du, dw, da, dqd, dkd, de)


def _gla_specs(t, order):
    ix = (lambda d, h: (d, h)) if order == "dh" else (lambda h, d: (d, h))

    def mk(fn):
        return lambda a, b: fn(*ix(a, b))
    q = pl.BlockSpec((t, DKB), mk(lambda d, h: (0, 32 + h)))
    k = pl.BlockSpec((t, DKB), mk(lambda d, h: (0, 36 + h)))
    v = pl.BlockSpec((t, DVB), mk(lambda d, h: (0, 20 + h)))
    dk = pl.BlockSpec((None, t, DKB), mk(lambda d, h: (d, 0, h)))
    dv = pl.BlockSpec((None, t, DVB), mk(lambda d, h: (d, 0, h)))
    e = pl.BlockSpec((None, None, t // CH, 8, 128), mk(lambda d, h: (d, h, 0, 0, 0)))
    s = pl.BlockSpec((None, None, t // CH, DVB, DKB), mk(lambda d, h: (d, h, 0, 0, 0)))
    return q, k, v, dk, dv, e, s


def _gla_intra_fwd(p, gk):
    t = p.shape[0]
    n = t // CH

    def body(q_ref, k_ref, v_ref, g_ref, qg_ref, kd_ref, in_ref, e_ref):
        cs = _Consts(pl.program_id(0) == 1)

        def step(i, carry):
            r = _srows(i)
            qg, kd, intra, el = _gla_intra(q_ref[r, :], k_ref[r, :], v_ref[r, :], g_ref[r, :], cs)
            qg_ref[r, :] = qg
            kd_ref[r, :] = kd
            in_ref[r, :] = intra
            for c in range(SC // CH):
                e_ref[(SC // CH) * i + c] = el[c * CH:c * CH + 8]
            return carry

        lax.fori_loop(0, t // SC, step, 0)

    q, k, v, dk, dv, e, _ = _gla_specs(t, "dh")
    return pl.pallas_call(
        body, name="gla_intra_fwd", grid=(2, NB),
        in_specs=[q, k, v, dk], out_specs=[dk, dk, dv, e],
        out_shape=[jax.ShapeDtypeStruct((2, t, NB * DKB), F32), jax.ShapeDtypeStruct((2, t, NB * DKB), F32),
                   jax.ShapeDtypeStruct((2, t, D), F32), jax.ShapeDtypeStruct((2, NB, n, 8, 128), F32)],
        compiler_params=_cparams(("parallel", "parallel")),
    )(p, p, p, gk)


GLA_HB = 4


def _gla_v_spec(t, along):
    nt = t // SCAN_TB

    def tmap(d, tt):
        fwd = tt + d * (nt - 1 - 2 * tt)
        return fwd if along > 0 else nt - 1 - fwd

    return pl.BlockSpec((SCAN_TB, GLA_HB * DVB), lambda d, h, tt: (tmap(d, tt), 5120 // (GLA_HB * DVB) + h))


def _gla_scan_fwd(p, qg, kd, intra, e):
    t = p.shape[0]
    tokk, per, _, _, grid, nb = _scan_specs(t, DKB, NB, GLA_HB, +1)
    tokv = _scan_specs(t, DVB, NB, GLA_HB, +1)[0]

    def body(v_ref, qg_ref, kd_ref, in_ref, e_ref, o_ref, s_ref, state):
        rev = pl.program_id(0) == 1

        @pl.when(pl.program_id(2) == 0)
        def _():
            state[...] = jnp.zeros_like(state)

        def step(i, sts):
            ci = jnp.where(rev, nb - 1 - i, i)
            r = _rows(ci)
            out = []
            for hh, st in enumerate(sts):
                ck = slice(hh * DKB, (hh + 1) * DKB)
                cv = slice(hh * DVB, (hh + 1) * DVB)
                s_ref[hh, ci] = st
                o, stn = _gla_scan(qg_ref[r, ck], kd_ref[r, ck], v_ref[r, cv], e_ref[hh, ci][0:1], st)
                o_ref[r, cv] = o + in_ref[r, cv]
                out.append(stn)
            return tuple(out)

        sts = lax.fori_loop(0, nb, step, tuple(state[hh] for hh in range(GLA_HB)))
        for hh, st in enumerate(sts):
            state[hh] = st

    return pl.pallas_call(
        body, name="gla_scan_fwd", grid=grid,
        in_specs=[_gla_v_spec(t, +1), tokk, tokk, tokv, per(8, 128)], out_specs=[tokv, per(DVB, DKB)],
        out_shape=[jax.ShapeDtypeStruct((2, t, D), F32), jax.ShapeDtypeStruct((2, NB, t // CH, DVB, DKB), F32)],
        scratch_shapes=[pltpu.VMEM((GLA_HB, DVB, DKB), F32)],
        compiler_params=_cparams(("parallel", "parallel", "arbitrary")),
    )(p, qg, kd, intra, e)


def _gla_scan_bwd(p, qg, kd, e, ssave, do):
    t = p.shape[0]
    tokk, per, _, shared, grid, nb = _scan_specs(t, DKB, NB, GLA_HB, -1)
    tokv = _scan_specs(t, DVB, NB, GLA_HB, -1)[0]

    def body(v_ref, qg_ref, kd_ref, e_ref, s_ref, do_ref, dqg_ref, dkd_ref, dv_ref, de_ref, state):
        rev = pl.program_id(0) == 1

        @pl.when(pl.program_id(2) == 0)
        def _():
            state[...] = jnp.zeros_like(state)

        def step(i, dsts):
            ci = jnp.where(rev, i, nb - 1 - i)
            r = _rows(ci)
            out = []
            for hh, dst in enumerate(dsts):
                ck = slice(hh * DKB, (hh + 1) * DKB)
                cv = slice(hh * DVB, (hh + 1) * DVB)
                _, vjp = jax.vjp(_gla_scan, qg_ref[r, ck], kd_ref[r, ck], v_ref[r, cv], e_ref[hh, ci][0:1],
                                 s_ref[hh, ci])
                dqg, dkd, dv, de, dstn = vjp((do_ref[r, cv], dst))
                dqg_ref[r, ck] = dqg
                dkd_ref[r, ck] = dkd
                dv_ref[r, cv] = dv
                de_ref[hh, ci] = _first_row(de)
                out.append(dstn)
            return tuple(out)

        dsts = lax.fori_loop(0, nb, step, tuple(state[hh] for hh in range(GLA_HB)))
        for hh, dst in enumerate(dsts):
            state[hh] = dst

    return pl.pallas_call(
        body, name="gla_scan_bwd", grid=grid,
        in_specs=[_gla_v_spec(t, -1), tokk, tokk, per(8, 128), per(DVB, DKB), shared(DVB)],
        out_specs=[tokk, tokk, tokv, per(8, 128)],
        out_shape=[jax.ShapeDtypeStruct((2, t, NB * DKB), F32), jax.ShapeDtypeStruct((2, t, NB * DKB), F32),
                   jax.ShapeDtypeStruct((2, t, D), F32), jax.ShapeDtypeStruct((2, NB, t // CH, 8, 128), F32)],
        scratch_shapes=[pltpu.VMEM((GLA_HB, DVB, DKB), F32)],
        compiler_params=_cparams(("parallel", "parallel", "arbitrary")),
    )(p, qg, kd, e, ssave, do)


def _gla_intra_bwd(p, gk, dqg, dkd, dvs, de, do):
    t = p.shape[0]
    n = t // CH

    def body(q_ref, k_ref, v_ref, g_ref, dqg_ref, dkd_ref, dvs_ref, de_ref, do_ref,
             dq_ref, dk_ref, dv_ref, dg_ref):
        d = pl.program_id(1)
        cs = _Consts(d == 1)

        @pl.when(d == 0)
        def _():
            dq_ref[...] = jnp.zeros_like(dq_ref)
            dk_ref[...] = jnp.zeros_like(dk_ref)
            dv_ref[...] = jnp.zeros_like(dv_ref)

        def step(i, carry):
            r = _srows(i)
            f = lambda q, k, v, g: _gla_intra(q, k, v, g, cs)
            _, vjp = jax.vjp(f, q_ref[r, :], k_ref[r, :], v_ref[r, :], g_ref[r, :])
            dq, dk, dv, dg = vjp((dqg_ref[r, :], dkd_ref[r, :], do_ref[r, :], _chunk_rows(de_ref, i)))
            dq_ref[r, :] += dq
            dk_ref[r, :] += dk
            dv_ref[r, :] += dv + dvs_ref[r, :]
            dg_ref[r, :] = dg
            return carry

        lax.fori_loop(0, t // SC, step, 0)

    q, k, v, dk, dv, e_s, _ = _gla_specs(t, "hd")
    hk = pl.BlockSpec((t, DKB), lambda h, d: (0, h))
    hv = pl.BlockSpec((t, DVB), lambda h, d: (0, h))
    return pl.pallas_call(
        body, name="gla_intra_bwd", grid=(NB, 2),
        in_specs=[q, k, v, dk, dk, dk, dv, e_s, hv],
        out_specs=[hk, hk, hv, dk],
        out_shape=[jax.ShapeDtypeStruct((t, NB * DKB), F32), jax.ShapeDtypeStruct((t, NB * DKB), F32),
                   jax.ShapeDtypeStruct((t, D), F32), jax.ShapeDtypeStruct((2, t, NB * DKB), F32)],
        compiler_params=_cparams(("arbitrary", "arbitrary")),
    )(p, p, p, gk, dqg, dkd, dvs, de, do)


def _seg_gate(o, z, w):
    return _rms(o, w) * _silu(z)


def _seg_merge(ya, yb, ga, gb):
    return _sigmoid(ga) * ya + _sigmoid(gb) * yb


def _seg_loss(out, x, tgt, w):
    err = x + _rms(out, w) - tgt
    return 0.5 * jnp.sum(jnp.mean(err * err, axis=-1, keepdims=True), axis=0, keepdims=True)


def _post(oa2, ob2, p, x, tgt, gdn_w, gla_w, lnpost, w3, tm=128):
    t = x.shape[0]

    def body(oa_ref, ob_ref, z_ref, gb_ref, ga_ref, gB_ref, x_ref, t_ref, aw_ref, bw_ref, lw_ref, w_ref,
             loss_ref, doa_ref, dob_ref, dz_ref, dgb_ref, dga_ref, dgB_ref, dy_ref,
             dw_ref, daw_ref, dbw_ref, dlw_ref):
        first = pl.program_id(0) == 0
        oa = oa_ref[0] + oa_ref[1]
        ob = ob_ref[0] + ob_ref[1]
        z, gb = z_ref[...], gb_ref[...]
        aw, bw = aw_ref[...], bw_ref[...]
        rs = D // NSHARD

        def mat(a, m):
            return sum(jnp.dot(a[:, s * rs:(s + 1) * rs], w_ref[s, m], preferred_element_type=F32)
                       for s in range(NSHARD))

        def mat_t(g, m):
            return jnp.concatenate([_bdot(g, w_ref[s, m], 1, 1) for s in range(NSHARD)], axis=1)

        def add_dw(a, g, m):
            for s in range(NSHARD):
                dw_ref[s, m] += _bdot(a[:, s * rs:(s + 1) * rs], g, 0, 0)

        pa = [jax.vjp(_seg_gate, oa[:, h * DA:(h + 1) * DA], z[:, h * DA:(h + 1) * DA], aw) for h in range(NA)]
        pb = [jax.vjp(_seg_gate, ob[:, h * DVB:(h + 1) * DVB], gb[:, h * DVB:(h + 1) * DVB], bw)
              for h in range(NB)]
        a1 = jnp.concatenate([v for v, _ in pa], axis=1).astype(BF16)
        a2 = jnp.concatenate([v for v, _ in pb], axis=1).astype(BF16)
        ya = mat(a1, 0)
        yb = mat(a2, 1)
        merged, vjp_m = jax.vjp(_seg_merge, ya, yb, ga_ref[...], gB_ref[...])
        mb = merged.astype(BF16)
        out = mat(mb, 2)
        loss, vjp_l = jax.vjp(_seg_loss, out, x_ref[...], t_ref[...], lw_ref[...])
        dout, dyres, _, dlw = vjp_l(jnp.ones((1, 1), F32))
        dy_ref[...] = dyres
        doutb = dout.astype(BF16)
        dmerged = mat_t(doutb, 2)
        dya, dyb, dga, dgB = vjp_m(dmerged)
        dga_ref[...] = dga.astype(BF16)
        dgB_ref[...] = dgB.astype(BF16)
        dyab, dybb = dya.astype(BF16), dyb.astype(BF16)
        da1 = mat_t(dyab, 0)
        da2 = mat_t(dybb, 1)

        daw = jnp.zeros_like(aw)
        for h in range(NA):
            sl = slice(h * DA, (h + 1) * DA)
            do, dz, dw = pa[h][1](da1[:, sl])
            doa_ref[:, sl] = do
            dz_ref[:, sl] = dz.astype(BF16)
            daw = daw + dw
        dbw = jnp.zeros_like(bw)
        for h in range(NB):
            sl = slice(h * DVB, (h + 1) * DVB)
            do, dg, dw = pb[h][1](da2[:, sl])
            dob_ref[:, sl] = do
            dgb_ref[:, sl] = dg.astype(BF16)
            dbw = dbw + dw

        @pl.when(first)
        def _():
            loss_ref[...] = jnp.zeros_like(loss_ref)
            dw_ref[...] = jnp.zeros_like(dw_ref)
            daw_ref[...] = jnp.zeros_like(daw_ref)
            dbw_ref[...] = jnp.zeros_like(dbw_ref)
            dlw_ref[...] = jnp.zeros_like(dlw_ref)

        loss_ref[...] += jnp.broadcast_to(loss, loss_ref.shape)
        add_dw(a1, dyab, 0)
        add_dw(a2, dybb, 1)
        add_dw(mb, doutb, 2)
        daw_ref[...] += jnp.broadcast_to(daw, daw_ref.shape)
        dbw_ref[...] += jnp.broadcast_to(dbw, dbw_ref.shape)
        dlw_ref[...] += jnp.broadcast_to(dlw, dlw_ref.shape)

    two = pl.BlockSpec((2, tm, D), lambda i: (0, i, 0))
    pcol = lambda c: pl.BlockSpec((tm, D), lambda i: (i, c))
    tok = pl.BlockSpec((tm, D), lambda i: (i, 0))
    row = lambda n: pl.BlockSpec((1, n), lambda i: (0, 0))
    row8 = lambda n: pl.BlockSpec((8, n), lambda i: (0, 0))
    once = pl.Buffered(1)
    tokf = jax.ShapeDtypeStruct((t, D), F32)
    tokb = jax.ShapeDtypeStruct((t, D), BF16)
    wspec = pl.BlockSpec((NSHARD, 3, D // NSHARD, D), lambda i: (0, 0, 0, 0), pipeline_mode=once)
    return pl.pallas_call(
        body, name="post", grid=(t // tm,),
        in_specs=[two, two, pcol(3), pcol(6), pcol(7), pcol(8), tok, tok, row(DA), row(DVB), row(D), wspec],
        out_specs=[row8(128), tok, tok, tok, tok, tok, tok, tok, wspec, row8(DA), row8(DVB), row8(D)],
        out_shape=[jax.ShapeDtypeStruct((8, 128), F32), tokf, tokf, tokb, tokb, tokb, tokb, tokf,
                   jax.ShapeDtypeStruct((NSHARD, 3, D // NSHARD, D), F32),
                   jax.ShapeDtypeStruct((8, DA), F32), jax.ShapeDtypeStruct((8, DVB), F32),
                   jax.ShapeDtypeStruct((8, D), F32)],
        compiler_params=_cparams(("arbitrary",), vmem_mb=56),
    )(oa2, ob2, p, p, p, p, x, tgt, gdn_w, gla_w, lnpost, w3)


def _adam_math(w, g, m, v):
    nm = B1 * m + (1.0 - B1) * g
    nv = B2 * v + (1.0 - B2) * (g * g)
    m_hat = nm / (1.0 - B1 ** STEP)
    v_hat = nv / (1.0 - B2 ** STEP)
    return -LR * (m_hat / (jnp.sqrt(v_hat) + ADAM_EPS) + WD * w), nm, nv


SMALL_SLOTS = (("ln_pre_w", 0, 1024, 0), ("a_log_fwd", 1024, 8, 0), ("a_log_bwd", 1024, 8, 8),
               ("dt_bias_fwd", 1152, 8, 0), ("dt_bias_bwd", 1152, 8, 8), ("gdn_norm_w", 1280, 128, 0),
               ("gk_b2_fwd", 1408, 512, 0), ("gk_b2_bwd", 1920, 512, 0), ("gla_norm_w", 2432, 256, 0),
               ("ln_post_w", 2688, 1024, 0))
SMALL_W = 3840


def _adam_small(gsum, ws, ms, vs):
    nw = len(SMALL_SLOTS)

    def body(g_ref, *refs):
        w_refs, m_refs, v_refs, outs = refs[0:nw], refs[nw:2 * nw], refs[2 * nw:3 * nw], refs[3 * nw:]
        for i, (_, off, n, shift) in enumerate(SMALL_SLOTS):
            slot = g_ref[0:1, off:off + max(n, 128)]
            if shift:
                slot = pltpu.roll(slot, 128 - shift, 1)
            g = slot[:, 0:n]
            d, nm, nv = _adam_math(w_refs[i][...], g, m_refs[i][...], v_refs[i][...])
            for k, val in enumerate((g, d, nm, nv)):
                outs[4 * i + k][...] = val

    vm = pl.BlockSpec(memory_space=pltpu.VMEM)
    res = pl.pallas_call(
        body, name="adam_small", in_specs=[vm] * (1 + 3 * nw), out_specs=[vm] * (4 * nw),
        out_shape=[jax.ShapeDtypeStruct((1, n), F32) for _, _, n, _ in SMALL_SLOTS for _ in range(4)],
    )(gsum, *ws, *ms, *vs)
    return {name: res[4 * i:4 * i + 4] for i, (name, _, _, _) in enumerate(SMALL_SLOTS)}


def _adam(w, mine, got, m, v, tr, tile0=0, name=""):
    rows, cols = w.shape
    nh = mine.shape[0] // tr

    def body(c_ref, w_ref, a_ref, b_ref, m_ref, v_ref, g_ref, d_ref, nm_ref, nv_ref):
        half = (tile0 + pl.program_id(0)) // nh
        g = jnp.where(half == c_ref[0], a_ref[...], b_ref[...])
        d, nm, nv = _adam_math(w_ref[...], g, m_ref[...], v_ref[...])
        g_ref[...] = g
        d_ref[...] = d
        nm_ref[...] = nm
        nv_ref[...] = nv

    blk = pl.BlockSpec((tr, cols), lambda i, cc: (i, 0))
    half = pl.BlockSpec((tr, cols), lambda i, cc: ((tile0 + i) % nh, 0))
    shp = jax.ShapeDtypeStruct((rows, cols), F32)
    return pl.pallas_call(
        body, name=f"adam_{name}{rows}x{cols}",
        grid_spec=pltpu.PrefetchScalarGridSpec(
            num_scalar_prefetch=1, grid=(rows // tr,),
            in_specs=[blk, half, half, blk, blk], out_specs=[blk] * 4),
        out_shape=[shp] * 4,
        compiler_params=_cparams(("parallel",)),
    )(lax.axis_index("c").reshape(1), w, mine, got, m, v)


def _adam_shard_small(mine, got, ws, ms, vs):
    def body(c_ref, a_ref, b_ref, *refs):
        w_refs, m_refs, v_refs, outs = refs[0:3], refs[3:6], refs[6:9], refs[9:]
        south = c_ref[0] == 0
        g = jnp.concatenate([jnp.where(south, a_ref[...], b_ref[...]),
                             jnp.where(south, b_ref[...], a_ref[...])], axis=0)
        grads = (g[0:5], g[8:24, 0:128], g[8:24, 128:256])
        for i, gi in enumerate(grads):
            d, nm, nv = _adam_math(w_refs[i][...], gi, m_refs[i][...], v_refs[i][...])
            for k, val in enumerate((gi, d, nm, nv)):
                outs[4 * i + k][...] = val

    vm = pl.BlockSpec(memory_space=pltpu.VMEM)
    res = pl.pallas_call(
        body, name="adam_shard_small",
        in_specs=[pl.BlockSpec(memory_space=pltpu.SMEM)] + [vm] * 11, out_specs=[vm] * 12,
        out_shape=[jax.ShapeDtypeStruct(w.shape, F32) for w in ws for _ in range(4)],
    )(lax.axis_index("c").reshape(1), mine, got, *ws, *ms, *vs)
    return [res[4 * i:4 * i + 4] for i in range(3)]


def _sum_cast(own, got):
    ns, _, r, c = own.shape
    tr = r // 4 if r % 64 == 0 else r

    def body(c_ref, a_ref, b_ref, f_ref, h_ref):
        s = a_ref[...] + b_ref[...]
        f_ref[...] = s
        h_ref[...] = s.astype(BF16)

    return pl.pallas_call(
        body, name=f"sum_cast_{r}x{c}",
        grid_spec=pltpu.PrefetchScalarGridSpec(
            num_scalar_prefetch=1, grid=(ns, r // tr),
            in_specs=[pl.BlockSpec((None, None, tr, c), lambda s, i, cc: (s, cc[0], i, 0)),
                      pl.BlockSpec((None, tr, c), lambda s, i, cc: (s, i, 0))],
            out_specs=[pl.BlockSpec((None, tr, c), lambda s, i, cc: (s, i, 0)),
                       pl.BlockSpec((None, tr, c), lambda s, i, cc: (s, i, 0))]),
        out_shape=[jax.ShapeDtypeStruct((ns, r, c), F32), jax.ShapeDtypeStruct((ns, r, c), BF16)],
        compiler_params=_cparams(("parallel", "parallel")),
    )(lax.axis_index("c").reshape(1), own, got)


def _sum4(mine, got):
    _, r, c = mine.shape
    tr = r // 4 if r % 64 == 0 else r

    def body(s_ref, a_ref, g_ref, o_ref):
        acc = a_ref[...] + g_ref[0].astype(F32)
        acc = acc + g_ref[1].astype(F32)
        o_ref[...] = acc + g_ref[2].astype(F32)

    shard = (2 * lax.axis_index("x") + lax.axis_index("y")).reshape(1)
    return pl.pallas_call(
        body, name=f"sum4_{r}x{c}",
        grid_spec=pltpu.PrefetchScalarGridSpec(
            num_scalar_prefetch=1, grid=(r // tr,),
            in_specs=[pl.BlockSpec((None, tr, c), lambda i, ss: (ss[0], i, 0)),
                      pl.BlockSpec((3, tr, c), lambda i, ss: (0, i, 0))],
            out_specs=pl.BlockSpec((tr, c), lambda i, ss: (i, 0))),
        out_shape=jax.ShapeDtypeStruct((r, c), F32),
        compiler_params=_cparams(("parallel",)),
    )(shard, mine, got)


def _place():
    x, y, c = lax.axis_index("x"), lax.axis_index("y"), lax.axis_index("c")
    chips = [(1 - x, y), (x, 1 - y), (1 - x, 1 - y)]
    return x, y, c, chips


def _gather_weights(parts):
    npart = len(parts)

    def body(*refs):
        ins, outs = refs[:npart], refs[npart:2 * npart]
        send_sems, recv_sems = refs[2 * npart:]
        x, y, c, chips = _place()
        sibling = (x, y, 1 - c)
        mine = 2 * x + y

        def remote(k, p, shard, half, to, src=None):
            dst = outs[p].at[shard, half]
            return pltpu.make_async_remote_copy(
                src_ref=dst if src is None else src, dst_ref=dst,
                send_sem=send_sems.at[k], recv_sem=recv_sems.at[k], device_id=to, device_id_type=MESH)

        first = [remote(j * npart + p, p, mine, c, (*chip, c), src=ins[p].at[c])
                 for j, chip in enumerate(chips) for p in range(npart)]
        for cp in first:
            cp.start()
        passed = []
        for j, (cx, cy) in enumerate(chips):
            for p in range(npart):
                remote(j * npart + p, p, 2 * cx + cy, c, (x, y, c)).wait_recv()
                fw = remote((3 + j) * npart + p, p, 2 * cx + cy, c, sibling)
                fw.start()
                passed.append(fw)
        for j, (cx, cy) in enumerate(chips):
            for p in range(npart):
                remote((3 + j) * npart + p, p, 2 * cx + cy, 1 - c, (x, y, c)).wait_recv()
        for cp in first + passed:
            cp.wait_send()

    got = pl.pallas_call(
        body, name="gather_weights",
        in_specs=[ANY] * npart, out_specs=[ANY] * npart,
        out_shape=[jax.ShapeDtypeStruct((NSHARD,) + a.shape, a.dtype) for a in parts],
        scratch_shapes=[pltpu.SemaphoreType.DMA((6 * npart,)), pltpu.SemaphoreType.DMA((6 * npart,))],
    )(*parts)
    mine = 2 * lax.axis_index("x") + lax.axis_index("y")
    return [lax.dynamic_update_index_in_dim(g, a, mine, 0) for g, a in zip(got, parts)]


def _swap_halves(parts, tag=""):
    npart = len(parts)

    def body(*refs):
        ins, outs = refs[:npart], refs[npart:2 * npart]
        send_sems, recv_sems = refs[2 * npart:]
        x, y, c, _ = _place()
        cps = [pltpu.make_async_remote_copy(
            src_ref=ins[p].at[s, 1 - c], dst_ref=outs[p].at[s],
            send_sem=send_sems.at[s * npart + p], recv_sem=recv_sems.at[s * npart + p],
            device_id=(x, y, 1 - c), device_id_type=MESH) for s in range(NSHARD) for p in range(npart)]
        for cp in cps:
            cp.start()
        for cp in cps:
            cp.wait()

    return pl.pallas_call(
        body, name="swap_halves" + tag, in_specs=[ANY] * npart, out_specs=[ANY] * npart,
        out_shape=[jax.ShapeDtypeStruct((NSHARD,) + a.shape[2:], a.dtype) for a in parts],
        scratch_shapes=[pltpu.SemaphoreType.DMA((NSHARD * npart,)), pltpu.SemaphoreType.DMA((NSHARD * npart,))],
    )(*parts)


def _scatter_shards(parts):
    npart = len(parts)

    def body(*refs):
        ins, outs = refs[:npart], refs[npart:2 * npart]
        send_sems, recv_sems = refs[2 * npart:]
        x, y, c, chips = _place()
        cps = [pltpu.make_async_remote_copy(
            src_ref=ins[p].at[2 * cx + cy], dst_ref=outs[p].at[j],
            send_sem=send_sems.at[j * npart + p], recv_sem=recv_sems.at[j * npart + p],
            device_id=(cx, cy, c), device_id_type=MESH)
            for j, (cx, cy) in enumerate(chips) for p in range(npart)]
        for cp in cps:
            cp.start()
        for cp in cps:
            cp.wait()

    return pl.pallas_call(
        body, name="scatter_shards", in_specs=[ANY] * npart, out_specs=[ANY] * npart,
        out_shape=[jax.ShapeDtypeStruct((3,) + a.shape[1:], a.dtype) for a in parts],
        scratch_shapes=[pltpu.SemaphoreType.DMA((3 * npart,)), pltpu.SemaphoreType.DMA((3 * npart,))],
    )(*parts)


HBM = pl.BlockSpec(memory_space=pltpu.HBM)
SEM = pl.BlockSpec(memory_space=pltpu.SEMAPHORE)
EFFECT = pltpu.SideEffectType.DATAFLOW_SIDE_EFFECTING


def _scatter_copies(srcs, lands, send_sems, recv_sems, waiting):
    x, y, c, chips = _place()
    n = len(srcs)
    return [pltpu.make_async_remote_copy(
        src_ref=srcs[p].at[2 * cx + cy], dst_ref=lands[p].at[j],
        send_sem=send_sems.at[j * n + p], recv_sem=recv_sems.at[j * n + p],
        device_id=(cx, cy, c), device_id_type=MESH)
        for j, (cx, cy) in enumerate(chips) for p in range(n)]


def _proj_copies(srcs, lands, send_sems, recv_sems, waiting):
    x, y, c, chips = _place()
    mine = 2 * x + y
    return [pltpu.make_async_remote_copy(
        src_ref=srcs[0].at[c], dst_ref=lands[0].at[mine, c],
        send_sem=send_sems.at[2 * j + to], recv_sem=recv_sems.at[2 * j + (to if waiting else c)],
        device_id=(cx, cy, to), device_id_type=MESH)
        for j, (cx, cy) in enumerate(chips) for to in range(2)]


def _start_copies(copies, nsem, parts, lands, name, after=None):
    n = len(parts)
    extra = [] if after is None else [after]

    def body(*refs):
        outs = refs[2 * n + len(extra):]
        for cp in copies(refs[:n], refs[n:2 * n], outs[0], outs[1], False):
            cp.start()
        outs[-1][...] = jnp.zeros_like(outs[-1])

    res = pl.pallas_call(
        body, name=name,
        out_shape=(pltpu.SemaphoreType.DMA((nsem,)), pltpu.SemaphoreType.DMA((nsem,)),
                   *[pltpu.HBM(a.shape, a.dtype) for a in parts], *[pltpu.HBM(a.shape, a.dtype) for a in lands],
                   jax.ShapeDtypeStruct((8, 128), F32)),
        in_specs=[HBM] * (2 * n) + [ANY] * len(extra),
        out_specs=(SEM, SEM, *[HBM] * (2 * n), pl.BlockSpec(memory_space=pltpu.VMEM)),
        input_output_aliases={i: 2 + i for i in range(2 * n)},
        compiler_params=pltpu.CompilerParams(has_side_effects=EFFECT),
    )(*[pltpu.with_memory_space_constraint(a, pltpu.HBM) for a in parts],
      *[pltpu.with_memory_space_constraint(a, pltpu.HBM) for a in lands], *extra)
    return res[0], res[1], res[2:2 + n], res[2 + n:2 + 2 * n], res[-1]


def _wait_copies(copies, started, after, name):
    send_sems, recv_sems, srcs, lands, _ = started
    n = len(srcs)

    def body(*refs):
        for cp in copies(refs[:n], refs[n:2 * n], refs[2 * n], refs[2 * n + 1], True):
            cp.wait_send()
            cp.wait_recv()

    res = pl.pallas_call(
        body, name=name,
        out_shape=tuple(pltpu.HBM(a.shape, a.dtype) for a in (*srcs, *lands)),
        in_specs=[HBM] * (2 * n) + [SEM, SEM, ANY], out_specs=tuple([HBM] * (2 * n)),
        input_output_aliases={i: i for i in range(2 * n)},
        compiler_params=pltpu.CompilerParams(has_side_effects=EFFECT),
    )(*srcs, *lands, send_sems, recv_sems, after)
    return res[n:]


def _join_halves(parts):
    npart = len(parts)

    def body(*refs):
        ins, outs = refs[:npart], refs[npart:2 * npart]
        send_sems, recv_sems = refs[2 * npart:]
        x, y, c, _ = _place()
        cps = [pltpu.make_async_remote_copy(
            src_ref=ins[p], dst_ref=outs[p], send_sem=send_sems.at[p], recv_sem=recv_sems.at[p],
            device_id=(x, y, 1 - c), device_id_type=MESH) for p in range(npart)]
        for cp in cps:
            cp.start()
        for cp in cps:
            cp.wait()

    return pl.pallas_call(
        body, name="join_halves", in_specs=[ANY] * npart, out_specs=[ANY] * npart,
        out_shape=[jax.ShapeDtypeStruct(a.shape, a.dtype) for a in parts],
        scratch_shapes=[pltpu.SemaphoreType.DMA((npart,)), pltpu.SemaphoreType.DMA((npart,))],
    )(*parts)


def _allreduce_small(v):
    r, ncol = v.shape

    def body(v_ref, o_ref, buf, send_sems, recv_sems):
        x, y, c, _ = _place()
        me = 4 * x + 2 * y + c
        buf[me] = v_ref[...]
        cps = []
        for k in range(1, 8):
            px, py, pc = x ^ (k >> 2), y ^ ((k >> 1) & 1), c ^ (k & 1)
            cps.append(pltpu.make_async_remote_copy(
                src_ref=v_ref, dst_ref=buf.at[me], send_sem=send_sems.at[k - 1], recv_sem=recv_sems.at[k - 1],
                device_id=(px, py, pc), device_id_type=MESH))
        for cp in cps:
            cp.start()
        for k in range(1, 8):
            px, py, pc = x ^ (k >> 2), y ^ ((k >> 1) & 1), c ^ (k & 1)
            pltpu.make_async_remote_copy(
                src_ref=v_ref, dst_ref=buf.at[4 * px + 2 * py + pc], send_sem=send_sems.at[k - 1],
                recv_sem=recv_sems.at[k - 1], device_id=(px, py, pc), device_id_type=MESH).wait_recv()
        for cp in cps:
            cp.wait_send()
        acc = buf[0]
        for d in range(1, 8):
            acc = acc + buf[d]
        o_ref[...] = acc

    return pl.pallas_call(
        body, name="allreduce_small",
        in_specs=[pl.BlockSpec(memory_space=pltpu.VMEM)], out_specs=pl.BlockSpec(memory_space=pltpu.VMEM),
        out_shape=jax.ShapeDtypeStruct((r, ncol), F32),
        scratch_shapes=[pltpu.VMEM((8, r, ncol), F32), pltpu.SemaphoreType.DMA((7,)), pltpu.SemaphoreType.DMA((7,))],
    )(v)


def _permute_rows(shards):
    w0, w1, w2, w3 = shards
    zeros = jnp.zeros((NPERM - 9280, w0.shape[1]), w0.dtype)
    return jnp.concatenate([w0, w1[0:1776], w1[1808:2320], w2, w3[0:240], w3[272:2320],
                            w1[1776:1808], w3[240:272], zeros], axis=0)


def _unpermute_rows(g):
    s1 = jnp.concatenate([g[2320:4096], g[9216:9248], g[4096:4608]], axis=0)
    s3 = jnp.concatenate([g[6928:7168], g[9248:9280], g[7168:9216]], axis=0)
    return jnp.stack([g[0:2320], s1, g[4608:6928], s3], axis=0)


def _pack_shard_small(conv, w2f, w2b):
    top = jnp.pad(conv, ((0, 8 - conv.shape[0]), (0, 0)))
    mid = jnp.pad(jnp.concatenate([w2f, w2b], axis=1), ((0, 0), (0, 768 - 256)))
    return jnp.concatenate([top, mid, jnp.zeros((8, 768), conv.dtype)], axis=0)


def _unpack_shard_small(a):
    return a[0:5], a[8:24, 0:128], a[8:24, 128:256]


def kernel(x, ln_pre_w, w_in, conv_w, a_log_fwd, a_log_bwd, dt_bias_fwd, dt_bias_bwd, gdn_norm_w, w_proj_gdn, gk_w2_fwd, gk_b2_fwd, gk_w2_bwd, gk_b2_bwd, gla_norm_w, w_proj_gla, w_out, ln_post_w, loss_target, m_ln_pre_w, m_w_in, m_conv_w, m_a_log_fwd, m_a_log_bwd, m_dt_bias_fwd, m_dt_bias_bwd, m_gdn_norm_w, m_w_proj_gdn, m_gk_w2_fwd, m_gk_b2_fwd, m_gk_w2_bwd, m_gk_b2_bwd, m_gla_norm_w, m_w_proj_gla, m_w_out, m_ln_post_w, v_ln_pre_w, v_w_in, v_conv_w, v_a_log_fwd, v_a_log_bwd, v_dt_bias_fwd, v_dt_bias_bwd, v_gdn_norm_w, v_w_proj_gdn, v_gk_w2_fwd, v_gk_b2_fwd, v_gk_w2_bwd, v_gk_b2_bwd, v_gla_norm_w, v_w_proj_gla, v_w_out, v_ln_post_w):
    t = x.shape[1]
    x2, tgt = x[0], loss_target[0]

    win_l = w_in[0].T.astype(BF16).reshape(2, SHW // 2, D)
    proj_l = jnp.concatenate([w_proj_gdn[0], w_proj_gla[0], w_out[0]], axis=0).astype(BF16).reshape(2, 384, D)
    small_l = _pack_shard_small(conv_w[0], gk_w2_fwd[0], gk_w2_bwd[0]).reshape(2, 16, 768)
    win_g, small_g = _gather_weights([win_l, small_l])
    proj_started = _start_copies(_proj_copies, 6, [proj_l], [lax.empty((NSHARD, 2, 384, D), BF16)],
                                 "gather_proj_start", after=small_g)
    wperm = _permute_rows(win_g.reshape(NSHARD, SHW, D))
    small_g = small_g.reshape(NSHARD, 32, 768)
    convw = small_g[:, 0:8, :].transpose(1, 0, 2).reshape(8, 3 * D)
    w2f = small_g[:, 8:24, 0:128].transpose(1, 0, 2).reshape(16, 512)
    w2b = small_g[:, 8:24, 128:256].transpose(1, 0, 2).reshape(16, 512)
    w2f_pad = jnp.pad(w2f, ((32, 80), (0, 0)))
    w2b_pad = jnp.pad(w2b, ((48, 64), (0, 0)))
    alog_row = jnp.pad(jnp.concatenate([a_log_fwd, a_log_bwd], axis=1), ((0, 0), (0, 112)))
    dt_row = jnp.pad(jnp.concatenate([dt_bias_fwd, dt_bias_bwd], axis=1), ((0, 0), (0, 112)))

    p, h = _inproj(x2, ln_pre_w + proj_started[4][0:1, 0:1], wperm)
    qn, kn, vc = (_qkv_fwd(p, convw, kind) for kind in range(3))
    gsm, gk = _gates_fwd(p, alog_row, dt_row, w2f_pad, gk_b2_fwd, w2b_pad, gk_b2_bwd)
    g2, b2 = _gcum_fwd(gsm)
    u, w, at, qd, kd, el, tinv = _gdn_intra_fwd(qn, kn, vc, g2, b2)
    oa2, sa = _gdn_scan_fwd(u, w, at, qd, kd, el)
    qg, kdb, intra, elb = _gla_intra_fwd(p, gk)
    ob2, sb = _gla_scan_fwd(p, qg, kdb, intra, elb)

    (proj_land,) = _wait_copies(_proj_copies, proj_started, ob2, "gather_proj_wait")
    mine = 2 * lax.axis_index("x") + lax.axis_index("y")
    w3 = lax.dynamic_update_index_in_dim(proj_land, proj_l, mine, 0).reshape(NSHARD, 3, D // NSHARD, D)
    (loss8, doa, dob, dz, dgb, dga, dgB, dyres, dw3, dgdn_w, dgla_w, dlnpost) = _post(
        oa2, ob2, p, x2, tgt, gdn_norm_w, gla_norm_w, ln_post_w, w3)

    g_proj = dw3.reshape(NSHARD, 2, 384, D)
    sum_proj = _sum_cast(g_proj, _swap_halves([g_proj], "_proj")[0])
    started_proj = _start_copies(_scatter_copies, 3, [sum_proj[1]], [lax.empty((3, 384, D), BF16)],
                                 "scatter_proj_start")
    du, dw, dat, dqd, dkd, del_ = _gdn_scan_bwd(u, w, at, qd, kd, el + started_proj[4][0, 0], sa, doa)
    dqn, dkn, dvc, dg2, db2 = _gdn_intra_bwd(qn, kn, vc, g2, b2, tinv, du, dw, dat, dqd, dkd, del_)
    dgsm = _gcum_bwd(gsm, dg2, db2)
    dqg, dkdb, dvs, delb = _gla_scan_bwd(p, qg, kdb, elb, sb, dob)
    dqb, dkb, dvb, dgk = _gla_intra_bwd(p, gk, dqg, dkdb, dvs, delb, dob)
    (dps, dalog8, ddt8, dw2f_pad, db2f8, dw2b_pad, db2b8) = _gates_bwd(
        p, alog_row, dt_row, w2f_pad, gk_b2_fwd, w2b_pad, gk_b2_bwd, dgsm, dgk)
    dpre, dconv = zip(*[_qkv_bwd(p, convw, g, kind) for kind, g in enumerate((dqn, dkn, dvc))])

    pieces = (jnp.concatenate([a.astype(BF16) for a in (*dpre, dz, dqb, dkb, dvb, dgb, dga, dgB, dps)], axis=1),)
    dwperm = _inproj_dw(h, pieces)

    g_in = _unpermute_rows(dwperm).reshape(NSHARD, 2, SHW // 2, D)
    dconv_full = jnp.concatenate(dconv, axis=1)
    dw2f, dw2b = dw2f_pad[32:48], dw2b_pad[48:64]
    g_small = jnp.stack([_pack_shard_small(dconv_full[0:5, 768 * s:768 * (s + 1)],
                                           dw2f[:, 128 * s:128 * (s + 1)], dw2b[:, 128 * s:128 * (s + 1)])
                         for s in range(NSHARD)])
    g_small = g_small.reshape(NSHARD, 2, 16, 768)
    parts = [g_in, g_small]
    got = _swap_halves(parts)
    sums = [_sum_cast(a, b) for a, b in zip(parts, got)]
    hbs = [hb for _, hb in sums]
    started = _start_copies(_scatter_copies, 3 * len(hbs), hbs,
                            [lax.empty((3,) + a.shape[1:], a.dtype) for a in hbs], "scatter_start")
    dx, dlnpre8 = _inproj_dx(pieces, wperm, x2, ln_pre_w + started[4][0:1, 0:1], dyres)

    gsmall = _allreduce_small(jnp.concatenate(
        [dlnpre8, dalog8, ddt8, dgdn_w, db2f8, db2b8, dgla_w, dlnpost, loss8], axis=1))
    smalls = dict(ln_pre_w=(ln_pre_w, m_ln_pre_w, v_ln_pre_w), a_log_fwd=(a_log_fwd, m_a_log_fwd, v_a_log_fwd),
                  a_log_bwd=(a_log_bwd, m_a_log_bwd, v_a_log_bwd),
                  dt_bias_fwd=(dt_bias_fwd, m_dt_bias_fwd, v_dt_bias_fwd),
                  dt_bias_bwd=(dt_bias_bwd, m_dt_bias_bwd, v_dt_bias_bwd),
                  gdn_norm_w=(gdn_norm_w, m_gdn_norm_w, v_gdn_norm_w),
                  gk_b2_fwd=(gk_b2_fwd, m_gk_b2_fwd, v_gk_b2_fwd), gk_b2_bwd=(gk_b2_bwd, m_gk_b2_bwd, v_gk_b2_bwd),
                  gla_norm_w=(gla_norm_w, m_gla_norm_w, v_gla_norm_w), ln_post_w=(ln_post_w, m_ln_post_w, v_ln_post_w))
    names = [name for name, _, _, _ in SMALL_SLOTS]
    small = _adam_small(gsmall, *([smalls[n][i] for n in names] for i in range(3)))

    landed_proj = _wait_copies(_scatter_copies, started_proj, small["ln_pre_w"][1], "scatter_proj_wait")
    landed = _wait_copies(_scatter_copies, started, small["ln_pre_w"][1], "scatter_wait")
    sums = [sums[0], sum_proj, sums[1]]
    halves = [_sum4(f, g) for (f, _), g in zip(sums, [landed[0], landed_proj[0], landed[1]])]
    theirs = _join_halves(halves)

    a_in = [a.T for a in _adam(w_in[0].T, halves[0], theirs[0], m_w_in[0].T, v_w_in[0].T, 232, name="in")]
    a_pr = [_adam(w[0], halves[1], theirs[1], m[0], v[0], 128, tile0=2 * i, name=f"proj{i}")
            for i, (w, m, v) in enumerate(((w_proj_gdn, m_w_proj_gdn, v_w_proj_gdn),
                                           (w_proj_gla, m_w_proj_gla, v_w_proj_gla), (w_out, m_w_out, v_w_out)))]
    a_ss = _adam_shard_small(halves[2], theirs[2], (conv_w[0], gk_w2_fwd[0], gk_w2_bwd[0]),
                             (m_conv_w[0], m_gk_w2_fwd[0], m_gk_w2_bwd[0]),
                             (v_conv_w[0], v_gk_w2_fwd[0], v_gk_w2_bwd[0]))

    def family(k):
        conv, w2f_, w2b_ = a_ss[0][k], a_ss[1][k], a_ss[2][k]
        s = {n: small[n][k] for n in names}
        return [s["ln_pre_w"], a_in[k][None], conv[None], s["a_log_fwd"], s["a_log_bwd"], s["dt_bias_fwd"],
                s["dt_bias_bwd"], s["gdn_norm_w"], a_pr[0][k][None], w2f_[None], s["gk_b2_fwd"], w2b_[None],
                s["gk_b2_bwd"], s["gla_norm_w"], a_pr[1][k][None], a_pr[2][k][None], s["ln_post_w"]]

    return (gsmall[0, SMALL_W - 128], dx[None], *family(0), *family(1), *family(2), *family(3))
```

```python
import functools

import jax
import jax.numpy as jnp
from jax import lax
from jax.experimental import pallas as pl
from jax.experimental.pallas import tpu as pltpu

F32 = jnp.float32
BF16 = jnp.bfloat16
HI = lax.Precision.HIGHEST
MESH = pl.DeviceIdType.MESH

D = 1024
CH = 64
EPS = 1e-6
NA, DA = 8, 128
NB, DKB, DVB = 4, 128, 256
NSHARD = 4
SHW = 2320
NPERM = 9728
PS_BLOCK = 72
LR, B1, B2, ADAM_EPS, WD, STEP = 0.001, 0.9, 0.999, 1e-08, 0.01, 10

ANY = pl.BlockSpec(memory_space=pl.ANY)


def _cparams(sem=None, vmem_mb=48):
    return pltpu.CompilerParams(dimension_semantics=sem, vmem_limit_bytes=vmem_mb << 20)


def _bdot(a, b, ca, cb):
    return lax.dot_general(a.astype(BF16), b.astype(BF16), (((ca,), (cb,)), ((), ())),
                           preferred_element_type=F32)


@jax.custom_vjp
def mm(a, b):
    return _bdot(a, b, 1, 0)


def _mm_fwd(a, b):
    return _bdot(a, b, 1, 0), (a, b)


def _mm_bwd(res, g):
    a, b = res
    return _bdot(g, b, 1, 1), _bdot(a, g, 0, 0)


mm.defvjp(_mm_fwd, _mm_bwd)


@jax.custom_vjp
def mm_nt(a, b):
    return _bdot(a, b, 1, 1)


def _mm_nt_fwd(a, b):
    return _bdot(a, b, 1, 1), (a, b)


def _mm_nt_bwd(res, g):
    a, b = res
    return _bdot(g, b, 1, 0), _bdot(g, a, 0, 0)


mm_nt.defvjp(_mm_nt_fwd, _mm_nt_bwd)


@jax.custom_vjp
def mm_tn(a, b):
    return _bdot(a, b, 0, 0)


def _mm_tn_fwd(a, b):
    return _bdot(a, b, 0, 0), (a, b)


def _mm_tn_bwd(res, g):
    a, b = res
    return _bdot(b, g, 1, 1), _bdot(a, g, 1, 0)


mm_tn.defvjp(_mm_tn_fwd, _mm_tn_bwd)


def dot_hi(a, b):
    return lax.dot_general(a, b, (((1,), (0,)), ((), ())), precision=HI, preferred_element_type=F32)


def _split3(x):
    x1 = x.astype(BF16)
    r = x - x1.astype(F32)
    x2 = r.astype(BF16)
    return x1, x2, (r - x2.astype(F32)).astype(BF16)


def _cdot(c, x, cc, cx, c_first=True):
    parts = _split3(x)
    if c_first:
        return _bdot(c, parts[0], cc, cx) + _bdot(c, parts[1], cc, cx) + _bdot(c, parts[2], cc, cx)
    return _bdot(parts[0], c, cx, cc) + _bdot(parts[1], c, cx, cc) + _bdot(parts[2], c, cx, cc)


@jax.custom_vjp
def cmm(c, x):
    return _cdot(c, x, 1, 0)


def _cmm_fwd(c, x):
    return _cdot(c, x, 1, 0), c


def _cmm_bwd(c, g):
    return jnp.zeros_like(c), _cdot(c, g, 0, 0)


cmm.defvjp(_cmm_fwd, _cmm_bwd)


@jax.custom_vjp
def mmc(x, c):
    return _cdot(c, x, 0, 1, c_first=False)


def _mmc_fwd(x, c):
    return _cdot(c, x, 0, 1, c_first=False), c


def _mmc_bwd(c, g):
    return _cdot(c, g, 1, 1, c_first=False), jnp.zeros_like(c)


mmc.defvjp(_mmc_fwd, _mmc_bwd)


def _sigmoid(x):
    return 1.0 / (1.0 + jnp.exp(-x))


def _silu(x):
    return x * _sigmoid(x)


def _softplus(x):
    return jnp.maximum(x, 0.0) + jnp.log(1.0 + jnp.exp(-jnp.abs(x)))


def _rms(x, w):
    return x * lax.rsqrt(jnp.mean(x * x, axis=-1, keepdims=True) + EPS) * w


SC = 256


class _Consts:
    def __init__(self, rev):
        r = lax.broadcasted_iota(jnp.int32, (SC, SC), 0)
        c = lax.broadcasted_iota(jnp.int32, (SC, SC), 1)
        same = (r >> 6) == (c >> 6)
        a = jnp.where(rev, c, r)
        b = jnp.where(rev, r, c)
        self.incl = same & (a >= b)
        self.strict = same & (a > b)
        self.incl_f = self.incl.astype(F32)
        self.eye = (r == c).astype(F32)
        rows = lax.broadcasted_iota(jnp.int32, (SC, 1), 0)
        self.last_col = ((rows & (CH - 1)) == jnp.where(rev, 0, CH - 1)).astype(F32)
        rr = lax.broadcasted_iota(jnp.int32, (SC, CH), 0)
        cc = lax.broadcasted_iota(jnp.int32, (SC, CH), 1)
        self.fold = ((rr & (CH - 1)) == cc).astype(F32)


def _dot3(a, b, ca=1, cb=0):
    ah, bh = a.astype(BF16), b.astype(BF16)
    al, bl = (a - ah.astype(F32)).astype(BF16), (b - bh.astype(F32)).astype(BF16)
    return _bdot(ah, bh, ca, cb) + (_bdot(ah, bl, ca, cb) + _bdot(al, bh, ca, cb))


TRI_SPLIT_LEVELS = 2


def _tri_inv(low, eye):
    n = -low
    acc = eye + n
    p = n
    for level in range(5):
        dot = _dot3 if level < TRI_SPLIT_LEVELS else (lambda a, b: _bdot(a, b, 1, 0))
        p = dot(p, p)
        acc = acc + dot(acc, p)
    return acc


@jax.custom_vjp
def _solve2(low, rv, rk, tinv):
    x = _dot3(tinv, jnp.concatenate([rv, rk], axis=1))
    return x[:, :DA], x[:, DA:]


def _solve2_fwd(low, rv, rk, tinv):
    x = _dot3(tinv, jnp.concatenate([rv, rk], axis=1))
    return (x[:, :DA], x[:, DA:]), (x, tinv)


def _solve2_bwd(res, g):
    x, tinv = res
    drhs = _dot3(tinv, jnp.concatenate(g, axis=1), 0, 0)
    return -_dot3(drhs, x, 1, 1), drhs[:, :DA], drhs[:, DA:], jnp.zeros_like(tinv)


_solve2.defvjp(_solve2_fwd, _solve2_bwd)


def _chunk_last(x, cs):
    xs = (x * cs.last_col).reshape(SC // CH, CH, x.shape[1])
    return jnp.broadcast_to(jnp.sum(xs, axis=1, keepdims=True), xs.shape).reshape(x.shape)


def _gdn_decay(g, cs):
    gw = jnp.concatenate([g] * (SC // DA), axis=1)
    grow = jnp.sum(cs.eye * gw, axis=0, keepdims=True)
    return jnp.where(cs.incl, jnp.exp(jnp.where(cs.incl, gw - grow, 0.0)), 0.0)


def _gdn_intra(q, k, v, g, bx, tinv, cs):
    decay = _gdn_decay(g, cs)
    kb = k * bx
    low = jnp.where(cs.strict, mm_nt(kb, k) * decay, 0.0)
    eg = jnp.exp(g)
    made = tinv is None
    if made:
        tinv = _tri_inv(low, cs.eye)
    u, w = _solve2(low, v * bx, kb * eg, tinv)
    attn = mmc(mm_nt(q, k) * decay, cs.fold)
    qd = q * eg
    glast = _chunk_last(g, cs)
    kd = k * jnp.exp(glast - g)
    outs = (u, w, attn, qd, kd, jnp.exp(glast))
    return outs + (tinv,) if made else outs


def _gdn_scan(u, w, attn, qd, kd, el, s):
    vn = u - mm(w, s)
    o = mm(qd, s) + mm(attn, vn)
    sn = s * el + mm_tn(kd, vn)
    return o, sn


def _gla_intra(q, k, v, gk, cs):
    gc = cmm(cs.incl_f, gk)
    qg = q * (DKB ** -0.5) * jnp.exp(gc)
    kg = k * jnp.exp(-gc)
    attn = jnp.where(cs.incl, mm_nt(qg, kg), 0.0)
    intra = mm(attn, v)
    glast = _chunk_last(gc, cs)
    kd = k * jnp.exp(glast - gc)
    return qg, kd, intra, jnp.exp(glast)


def _gla_scan(qg, kd, v, el, st):
    o = mm_nt(qg, st)
    stn = st * el + mm_tn(v, kd)
    return o, stn


def _shift_rows(x, s):
    if s == 0:
        return x
    t = x.shape[0]
    rolled = pltpu.roll(x, (-s) % t, 0)
    rows = lax.broadcasted_iota(jnp.int32, x.shape, 0)
    return jnp.where((rows + s >= 0) & (rows + s < t), rolled, 0.0)


@jax.custom_vjp
def _conv5(x, w):
    acc = w[0:1] * _shift_rows(x, -2)
    for j in range(1, 5):
        acc = acc + w[j:j + 1] * _shift_rows(x, j - 2)
    return acc


def _conv5_fwd(x, w):
    return _conv5(x, w), (x, w)


def _conv5_bwd(res, g):
    x, w = res
    dx = w[0:1] * _shift_rows(g, 2)
    for j in range(1, 5):
        dx = dx + w[j:j + 1] * _shift_rows(g, 2 - j)
    rows = lax.broadcasted_iota(jnp.int32, w.shape, 0)
    dw = jnp.zeros_like(w)
    for j in range(5):
        dwj = jnp.sum(g * _shift_rows(x, j - 2), axis=0, keepdims=True)
        dw = dw + jnp.where(rows == j, dwj, 0.0)
    return dx, dw


_conv5.defvjp(_conv5_fwd, _conv5_bwd)


def _qkv_act(kind):
    def f(x, w):
        c = _silu(_conv5(x, w))
        if kind == 2:
            return c
        c = c * lax.rsqrt(jnp.sum(c * c, axis=-1, keepdims=True) + EPS)
        return c * (DA ** -0.5) if kind == 0 else c
    return f


def _inproj(x, lnw, wperm, tn=512):
    t = x.shape[0]
    tm = min(t, 2048)

    def body(x_ref, lnw_ref, w_ref, p_ref, h_ref, hbuf):
        @pl.when(pl.program_id(1) == 0)
        def _():
            hb = _rms(x_ref[...], lnw_ref[...]).astype(BF16)
            hbuf[...] = hb
            h_ref[...] = hb
        p_ref[...] = _bdot(hbuf[...], w_ref[...], 1, 1)

    return pl.pallas_call(
        body, name="inproj", grid=(t // tm, NPERM // tn),
        in_specs=[pl.BlockSpec((tm, D), lambda i, j: (i, 0)),
                  pl.BlockSpec((1, D), lambda i, j: (0, 0)),
                  pl.BlockSpec((tn, D), lambda i, j: (j, 0))],
        out_specs=[pl.BlockSpec((tm, tn), lambda i, j: (i, j)),
                   pl.BlockSpec((tm, D), lambda i, j: (i, 0))],
        out_shape=[jax.ShapeDtypeStruct((t, NPERM), F32),
                   jax.ShapeDtypeStruct((t, D), BF16)],
        scratch_shapes=[pltpu.VMEM((tm, D), BF16)],
        compiler_params=_cparams(("parallel", "arbitrary")),
    )(x, lnw, wperm)


DP_TILE = 512
DP_PIECES = ((0, 19),)


def _piece_specs(tm, j_first):
    specs = []
    for j0, n in DP_PIECES:
        def imap(a, b, j0=j0, n=n):
            j, i = (a, b) if j_first else (b, a)
            inside = (j >= j0) & (j < j0 + n)
            return jnp.where(inside, i, 0), jnp.clip(j - j0, 0, n - 1)
        specs.append(pl.BlockSpec((tm, DP_TILE), imap))
    return specs


def _for_piece(j, refs, fn):
    for (j0, n), ref in zip(DP_PIECES, refs):
        @pl.when((j >= j0) & (j < j0 + n))
        def _(ref=ref):
            fn(ref[...])


def _inproj_dw(h, pieces):
    t = h.shape[0]
    tm = min(t, 2048)
    npc = len(pieces)

    def body(h_ref, *refs):
        dw_ref = refs[npc]

        @pl.when(pl.program_id(1) == 0)
        def _():
            dw_ref[...] = jnp.zeros_like(dw_ref)

        def add(dp):
            dw_ref[...] += _bdot(dp, h_ref[...], 0, 0)
        _for_piece(pl.program_id(0), refs[:npc], add)

    return pl.pallas_call(
        body, name="inproj_dw", grid=(NPERM // DP_TILE, t // tm),
        in_specs=[pl.BlockSpec((tm, D), lambda j, i: (i, 0))] + _piece_specs(tm, True),
        out_specs=pl.BlockSpec((DP_TILE, D), lambda j, i: (j, 0)),
        out_shape=jax.ShapeDtypeStruct((NPERM, D), F32),
        compiler_params=_cparams(("parallel", "arbitrary")),
    )(h, *pieces)


def _inproj_dx(pieces, wperm, x, lnw, dyres):
    t = x.shape[0]
    tm = min(t, 1024)
    tn = DP_TILE
    nj = NPERM // tn
    npc = len(pieces)

    def body(*refs):
        w_ref, x_ref, lnw_ref, dy_ref, dx_ref, dlnw_ref, acc = refs[npc:]
        j = pl.program_id(1)

        @pl.when(j == 0)
        def _():
            acc[...] = jnp.zeros_like(acc)

        def add(dp):
            acc[...] += _bdot(dp, w_ref[...], 1, 0)
        _for_piece(j, refs[:npc], add)

        @pl.when(j == nj - 1)
        def _():
            _, vjp = jax.vjp(_rms, x_ref[...], lnw_ref[...])
            dx, dlnw = vjp(acc[...])
            dx_ref[...] = dx + dy_ref[...]

            @pl.when(pl.program_id(0) == 0)
            def _():
                dlnw_ref[...] = jnp.zeros_like(dlnw_ref)
            dlnw_ref[...] += jnp.broadcast_to(dlnw, dlnw_ref.shape)

    return pl.pallas_call(
        body, name="inproj_dx", grid=(t // tm, nj),
        in_specs=_piece_specs(tm, False) + [
                  pl.BlockSpec((tn, D), lambda i, j: (j, 0)),
                  pl.BlockSpec((tm, D), lambda i, j: (i, 0)),
                  pl.BlockSpec((1, D), lambda i, j: (0, 0)),
                  pl.BlockSpec((tm, D), lambda i, j: (i, 0))],
        out_specs=[pl.BlockSpec((tm, D), lambda i, j: (i, 0)),
                   pl.BlockSpec((8, D), lambda i, j: (0, 0))],
        out_shape=[jax.ShapeDtypeStruct((t, D), F32), jax.ShapeDtypeStruct((8, D), F32)],
        scratch_shapes=[pltpu.VMEM((tm, D), F32)],
        compiler_params=_cparams(("arbitrary", "arbitrary")),
    )(*pieces, wperm, x, lnw, dyres)


def _qkv_fwd(p, convw, kind):
    t = p.shape[0]
    f = _qkv_act(kind)

    def body(p_ref, w_ref, o_ref):
        o_ref[...] = f(p_ref[...], w_ref[...])

    return pl.pallas_call(
        body, name=f"qkv_fwd{kind}", grid=(NA,),
        in_specs=[pl.BlockSpec((t, DA), lambda h: (0, kind * NA + h)),
                  pl.BlockSpec((8, DA), lambda h: (0, kind * NA + h))],
        out_specs=pl.BlockSpec((t, DA), lambda h: (0, h)),
        out_shape=jax.ShapeDtypeStruct((t, D), F32),
        compiler_params=_cparams(("parallel",)),
    )(p, convw)


def _qkv_bwd(p, convw, dout, kind):
    t = p.shape[0]
    f = _qkv_act(kind)

    def body(p_ref, w_ref, g_ref, dx_ref, dw_ref):
        _, vjp = jax.vjp(f, p_ref[...], w_ref[...])
        dx, dw = vjp(g_ref[...])
        dx_ref[...] = dx.astype(BF16)
        dw_ref[...] = dw

    return pl.pallas_call(
        body, name=f"qkv_bwd{kind}", grid=(NA,),
        in_specs=[pl.BlockSpec((t, DA), lambda h: (0, kind * NA + h)),
                  pl.BlockSpec((8, DA), lambda h: (0, kind * NA + h)),
                  pl.BlockSpec((t, DA), lambda h: (0, h))],
        out_specs=[pl.BlockSpec((t, DA), lambda h: (0, h)),
                   pl.BlockSpec((8, DA), lambda h: (0, h))],
        out_shape=[jax.ShapeDtypeStruct((t, D), BF16), jax.ShapeDtypeStruct((8, D), F32)],
        compiler_params=_cparams(("parallel",)),
    )(p, convw, dout)


def _gates_f(ps, alog_row, dt_row, w2f, b2f, w2b, b2b):
    lane = lax.broadcasted_iota(jnp.int32, ps.shape, 1)
    lg = -jnp.exp(alog_row) * _softplus(ps + dt_row)
    gsm = jnp.where(lane < 16, lg, jnp.where(lane < 32, _sigmoid(ps), 0.0))
    gkf = -_softplus(-(mm(ps, w2f) + b2f)) * (1.0 / 16.0)
    gkb = -_softplus(-(mm(ps, w2b) + b2b)) * (1.0 / 16.0)
    return gsm, gkf, gkb


def _gates_fwd(ps, alog_row, dt_row, w2f, b2f, w2b, b2b, tm=512):
    t = ps.shape[0]

    def body(ps_ref, a_ref, d_ref, wf_ref, bf_ref, wb_ref, bb_ref, gsm_ref, gk_ref):
        gsm, gkf, gkb = _gates_f(ps_ref[...], a_ref[...], d_ref[...], wf_ref[...], bf_ref[...],
                                 wb_ref[...], bb_ref[...])
        gsm_ref[...] = gsm
        gk_ref[0] = gkf
        gk_ref[1] = gkb

    row = lambda n: pl.BlockSpec((1, n), lambda i: (0, 0))
    mat = pl.BlockSpec((128, 512), lambda i: (0, 0))
    return pl.pallas_call(
        body, name="gates_fwd", grid=(t // tm,),
        in_specs=[pl.BlockSpec((tm, 128), lambda i: (i, PS_BLOCK)), row(128), row(128), mat, row(512), mat, row(512)],
        out_specs=[pl.BlockSpec((tm, 128), lambda i: (i, 0)),
                   pl.BlockSpec((2, tm, 512), lambda i: (0, i, 0))],
        out_shape=[jax.ShapeDtypeStruct((t, 128), F32), jax.ShapeDtypeStruct((2, t, 512), F32)],
        compiler_params=_cparams(("parallel",)),
    )(ps, alog_row, dt_row, w2f, b2f, w2b, b2b)


def _gates_bwd(ps, alog_row, dt_row, w2f, b2f, w2b, b2b, dgsm, dgk, tm=512):
    t = ps.shape[0]

    def body(ps_ref, a_ref, d_ref, wf_ref, bf_ref, wb_ref, bb_ref, dgsm_ref, dgk_ref,
             dps_ref, da_ref, dd_ref, dwf_ref, dbf_ref, dwb_ref, dbb_ref):
        _, vjp = jax.vjp(_gates_f, ps_ref[...], a_ref[...], d_ref[...], wf_ref[...], bf_ref[...],
                         wb_ref[...], bb_ref[...])
        dps, da, dd, dwf, dbf, dwb, dbb = vjp((dgsm_ref[...], dgk_ref[0], dgk_ref[1]))
        dps_ref[:, 0:128] = dps.astype(BF16)
        dps_ref[:, 128:DP_TILE] = jnp.zeros((tm, DP_TILE - 128), BF16)
        accs = ((da_ref, da), (dd_ref, dd), (dwf_ref, dwf), (dbf_ref, dbf), (dwb_ref, dwb), (dbb_ref, dbb))

        @pl.when(pl.program_id(0) == 0)
        def _():
            for ref, _ in accs:
                ref[...] = jnp.zeros_like(ref)
        for ref, val in accs:
            ref[...] += jnp.broadcast_to(val, ref.shape)

    row = lambda n: pl.BlockSpec((1, n), lambda i: (0, 0))
    row8 = lambda n: pl.BlockSpec((8, n), lambda i: (0, 0))
    mat = pl.BlockSpec((128, 512), lambda i: (0, 0))
    return pl.pallas_call(
        body, name="gates_bwd", grid=(t // tm,),
        in_specs=[pl.BlockSpec((tm, 128), lambda i: (i, PS_BLOCK)), row(128), row(128), mat, row(512), mat, row(512),
                  pl.BlockSpec((tm, 128), lambda i: (i, 0)),
                  pl.BlockSpec((2, tm, 512), lambda i: (0, i, 0))],
        out_specs=[pl.BlockSpec((tm, DP_TILE), lambda i: (i, 0)), row8(128), row8(128), mat, row8(512), mat,
                   row8(512)],
        out_shape=[jax.ShapeDtypeStruct((t, DP_TILE), BF16),
                   jax.ShapeDtypeStruct((8, 128), F32), jax.ShapeDtypeStruct((8, 128), F32),
                   jax.ShapeDtypeStruct((128, 512), F32), jax.ShapeDtypeStruct((8, 512), F32),
                   jax.ShapeDtypeStruct((128, 512), F32), jax.ShapeDtypeStruct((8, 512), F32)],
        compiler_params=_cparams(("arbitrary",)),
    )(ps, alog_row, dt_row, w2f, b2f, w2b, b2b, dgsm, dgk)


def _rows(i):
    return pl.ds(pl.multiple_of(i * CH, CH), CH)


def _srows(i):
    return pl.ds(pl.multiple_of(i * SC, SC), SC)


def _first_row(x):
    row = lax.broadcasted_iota(jnp.int32, (8, x.shape[1]), 0)
    return jnp.where(row == 0, jnp.broadcast_to(x, (8, x.shape[1])), 0.0)


def _chunk_rows(e_ref, i):
    pad = jnp.zeros((CH - 8, 128), F32)
    return jnp.concatenate([x for c in range(SC // CH) for x in (e_ref[(SC // CH) * i + c], pad)], axis=0)


def _gcum_f(gsm, tm):
    i = lax.broadcasted_iota(jnp.int32, (tm, tm), 0)
    j = lax.broadcasted_iota(jnp.int32, (tm, tm), 1)
    same = (i >> 6) == (j >> 6)
    lower = (same & (i >= j)).astype(F32)
    upper = (same & (i <= j)).astype(F32)
    r = lax.broadcasted_iota(jnp.int32, (128, D), 0)
    head = lax.broadcasted_iota(jnp.int32, (128, D), 1) >> 7
    pick = lambda off: (r == head + off).astype(F32)
    lane = lax.broadcasted_iota(jnp.int32, gsm.shape, 1)
    run = jnp.where(lane < 8, cmm(lower, gsm), cmm(upper, gsm))
    return mmc(run, pick(0)), mmc(run, pick(8)), mmc(gsm, pick(16)), mmc(gsm, pick(24))


def _gcum_fwd(gsm, tm=256):
    t = gsm.shape[0]

    def body(s_ref, g_ref, b_ref):
        gf, gb, bf, bb = _gcum_f(s_ref[...], tm)
        g_ref[0] = gf
        g_ref[1] = gb
        b_ref[0] = bf
        b_ref[1] = bb

    two = pl.BlockSpec((2, tm, D), lambda i: (0, i, 0))
    return pl.pallas_call(
        body, name="gcum_fwd", grid=(t // tm,),
        in_specs=[pl.BlockSpec((tm, 128), lambda i: (i, 0))], out_specs=[two, two],
        out_shape=[jax.ShapeDtypeStruct((2, t, D), F32)] * 2,
        compiler_params=_cparams(("parallel",)),
    )(gsm)


def _gcum_bwd(gsm, dg2, db2, tm=256):
    t = gsm.shape[0]

    def body(s_ref, dg_ref, db_ref, ds_ref):
        _, vjp = jax.vjp(lambda s: _gcum_f(s, tm), s_ref[...])
        ds_ref[...] = vjp((dg_ref[0], dg_ref[1], db_ref[0], db_ref[1]))[0]

    two = pl.BlockSpec((2, tm, D), lambda i: (0, i, 0))
    tile = pl.BlockSpec((tm, 128), lambda i: (i, 0))
    return pl.pallas_call(
        body, name="gcum_bwd", grid=(t // tm,),
        in_specs=[tile, two, two], out_specs=tile,
        out_shape=jax.ShapeDtypeStruct((t, 128), F32),
        compiler_params=_cparams(("parallel",)),
    )(gsm, dg2, db2)


def _gdn_intra_fwd(qn, kn, vc, g2, b2):
    t = qn.shape[0]
    n = t // CH

    def body(q_ref, k_ref, v_ref, g_ref, b_ref, u_ref, w_ref, a_ref, qd_ref, kd_ref, e_ref, t_ref):
        cs = _Consts(pl.program_id(0) == 1)

        def step(i, carry):
            r = _srows(i)
            q, k, v, g, bx = q_ref[r, :], k_ref[r, :], v_ref[r, :], g_ref[r, :], b_ref[r, :]
            u, w, a, qd, kd, el, tinv = _gdn_intra(q, k, v, g, bx, None, cs)
            u_ref[r, :] = u
            w_ref[r, :] = w
            a_ref[r, :] = a
            qd_ref[r, :] = qd
            kd_ref[r, :] = kd
            t_ref[r, :] = tinv
            for c in range(SC // CH):
                e_ref[(SC // CH) * i + c] = el[c * CH:c * CH + 8]
            return carry

        lax.fori_loop(0, t // SC, step, 0)

    head = pl.BlockSpec((t, DA), lambda d, h: (0, h))
    dh = pl.BlockSpec((None, t, DA), lambda d, h: (d, 0, h))
    sq = lambda w: pl.BlockSpec((None, None, t, w), lambda d, h: (d, h, 0, 0))
    big = jax.ShapeDtypeStruct((2, t, D), F32)
    return pl.pallas_call(
        body, name="gdn_intra_fwd", grid=(2, NA),
        in_specs=[head, head, head, dh, dh],
        out_specs=[dh, dh, sq(CH), dh, dh, pl.BlockSpec((None, None, n, 8, 128), lambda d, h: (d, h, 0, 0, 0)),
                   sq(SC)],
        out_shape=[big, big, jax.ShapeDtypeStruct((2, NA, t, CH), F32), big, big,
                   jax.ShapeDtypeStruct((2, NA, n, 8, 128), F32), jax.ShapeDtypeStruct((2, NA, t, SC), F32)],
        compiler_params=_cparams(("parallel", "parallel")),
    )(qn, kn, vc, g2, b2)


SCAN_TB = 256
SCAN_HB = 8


def _scan_specs(t, width, nheads, hb, along, tb=SCAN_TB):
    nt = t // tb
    nb = tb // CH

    def tmap(d, tt):
        fwd = tt + d * (nt - 1 - 2 * tt)
        return fwd if along > 0 else nt - 1 - fwd

    tok = pl.BlockSpec((None, tb, hb * width), lambda d, h, tt: (d, tmap(d, tt), h))
    per = lambda *tail: pl.BlockSpec((None, hb, nb) + tail, lambda d, h, tt: (d, h, tmap(d, tt)) + (0,) * len(tail))
    sq = pl.BlockSpec((None, hb, tb, CH), lambda d, h, tt: (d, h, tmap(d, tt), 0))
    shared = lambda w: pl.BlockSpec((tb, hb * w), lambda d, h, tt: (tmap(d, tt), h))
    return tok, per, sq, shared, (2, nheads // hb, nt), nb


def _gdn_scan_fwd(u, w, a, qd, kd, e):
    t = u.shape[1]
    tok, per, sq, _, grid, nb = _scan_specs(t, DA, NA, SCAN_HB, +1, tb=2 * SCAN_TB)

    def body(u_ref, w_ref, a_ref, qd_ref, kd_ref, e_ref, o_ref, s_ref, state):
        rev = pl.program_id(0) == 1

        @pl.when(pl.program_id(2) == 0)
        def _():
            state[...] = jnp.zeros_like(state)

        def step(i, ss):
            ci = jnp.where(rev, nb - 1 - i, i)
            r = _rows(ci)
            out = []
            for hh, s in enumerate(ss):
                c = slice(hh * DA, (hh + 1) * DA)
                s_ref[hh, ci] = s
                o, sn = _gdn_scan(u_ref[r, c], w_ref[r, c], a_ref[hh, r, :], qd_ref[r, c], kd_ref[r, c],
                                  e_ref[hh, ci][0:1], s)
                o_ref[r, c] = o
                out.append(sn)
            return tuple(out)

        ss = lax.fori_loop(0, nb, step, tuple(state[hh] for hh in range(SCAN_HB)))
        for hh, s in enumerate(ss):
            state[hh] = s

    return pl.pallas_call(
        body, name="gdn_scan_fwd", grid=grid,
        in_specs=[tok, tok, sq, tok, tok, per(8, 128)],
        out_specs=[tok, per(DA, DA)],
        out_shape=[jax.ShapeDtypeStruct((2, t, D), F32), jax.ShapeDtypeStruct((2, NA, t // CH, DA, DA), F32)],
        scratch_shapes=[pltpu.VMEM((SCAN_HB, DA, DA), F32)],
        compiler_params=_cparams(("parallel", "parallel", "arbitrary")),
    )(u, w, a, qd, kd, e)


def _gdn_scan_bwd(u, w, a, qd, kd, e, ssave, do):
    t = u.shape[1]
    tok, per, sq, shared, grid, nb = _scan_specs(t, DA, NA, SCAN_HB, -1)

    def body(u_ref, w_ref, a_ref, qd_ref, kd_ref, e_ref, s_ref, do_ref,
             du_ref, dw_ref, da_ref, dqd_ref, dkd_ref, de_ref, state):
        rev = pl.program_id(0) == 1

        @pl.when(pl.program_id(2) == 0)
        def _():
            state[...] = jnp.zeros_like(state)

        def step(i, dss):
            ci = jnp.where(rev, i, nb - 1 - i)
            r = _rows(ci)
            out = []
            for hh, ds in enumerate(dss):
                c = slice(hh * DA, (hh + 1) * DA)
                _, vjp = jax.vjp(_gdn_scan, u_ref[r, c], w_ref[r, c], a_ref[hh, r, :], qd_ref[r, c], kd_ref[r, c],
                                 e_ref[hh, ci][0:1], s_ref[hh, ci])
                du, dw, da, dqd, dkd, de, dsn = vjp((do_ref[r, c], ds))
                du_ref[r, c] = du
                dw_ref[r, c] = dw
                da_ref[hh, r, :] = da
                dqd_ref[r, c] = dqd
                dkd_ref[r, c] = dkd
                de_ref[hh, ci] = _first_row(de)
                out.append(dsn)
            return tuple(out)

        dss = lax.fori_loop(0, nb, step, tuple(state[hh] for hh in range(SCAN_HB)))
        for hh, ds in enumerate(dss):
            state[hh] = ds

    big = jax.ShapeDtypeStruct((2, t, D), F32)
    return pl.pallas_call(
        body, name="gdn_scan_bwd", grid=grid,
        in_specs=[tok, tok, sq, tok, tok, per(8, 128), per(DA, DA), shared(DA)],
        out_specs=[tok, tok, sq, tok, tok, per(8, 128)],
        out_shape=[big, big, jax.ShapeDtypeStruct((2, NA, t, CH), F32), big, big,
                   jax.ShapeDtypeStruct((2, NA, t // CH, 8, 128), F32)],
        scratch_shapes=[pltpu.VMEM((SCAN_HB, DA, DA), F32)],
        compiler_params=_cparams(("parallel", "parallel", "arbitrary")),
    )(u, w, a, qd, kd, e, ssave, do)


def _gdn_intra_bwd(qn, kn, vc, g2, b2, tinv, du, dw, da, dqd, dkd, de):
    t = qn.shape[0]
    n = t // CH

    def body(q_ref, k_ref, v_ref, g_ref, b_ref, t_ref, du_ref, dw_ref, da_ref, dqd_ref, dkd_ref, de_ref,
             dq_ref, dk_ref, dv_ref, dg_ref, db_ref):
        d = pl.program_id(1)
        cs = _Consts(d == 1)

        @pl.when(d == 0)
        def _():
            dq_ref[...] = jnp.zeros_like(dq_ref)
            dk_ref[...] = jnp.zeros_like(dk_ref)
            dv_ref[...] = jnp.zeros_like(dv_ref)

        def step(i, carry):
            r = _srows(i)
            tinv_c = t_ref[r, :]
            f = lambda q, k, v, g, bx: _gdn_intra(q, k, v, g, bx, tinv_c, cs)
            _, vjp = jax.vjp(f, q_ref[r, :], k_ref[r, :], v_ref[r, :], g_ref[r, :], b_ref[r, :])
            dq, dk, dv, dg, dbx = vjp((du_ref[r, :], dw_ref[r, :], da_ref[r, :], dqd_ref[r, :],
                                       dkd_ref[r, :], _chunk_rows(de_ref, i)))
            dq_ref[r, :] += dq
            dk_ref[r, :] += dk
            dv_ref[r, :] += dv
            dg_ref[r, :] = dg
            db_ref[r, :] = dbx
            return carry

        lax.fori_loop(0, t // SC, step, 0)

    head = pl.BlockSpec((t, DA), lambda h, d: (0, h))
    dh = pl.BlockSpec((None, t, DA), lambda h, d: (d, 0, h))
    sq = pl.BlockSpec((None, None, t, CH), lambda h, d: (d, h, 0, 0))
    tq = pl.BlockSpec((None, None, t, SC), lambda h, d: (d, h, 0, 0))
    full = jax.ShapeDtypeStruct((t, D), F32)
    big = jax.ShapeDtypeStruct((2, t, D), F32)
    return pl.pallas_call(
        body, name="gdn_intra_bwd", grid=(NA, 2),
        in_specs=[head, head, head, dh, dh, tq, dh, dh, sq, dh, dh,
                  pl.BlockSpec((None, None, n, 8, 128), lambda h, d: (d, h, 0, 0, 0))],
        out_specs=[head, head, head, dh, dh],
        out_shape=[full, full, full, big, big],
        compiler_params=_cparams(("arbitrary", "arbitrary")),
    )(qn, kn, vc, g2, b2, tinv, du, dw, da, dqd, dkd, de)


def _gla_specs(t, order):
    ix = (lambda d, h: (d, h)) if order == "dh" else (lambda h, d: (d, h))

    def mk(fn):
        return lambda a, b: fn(*ix(a, b))
    q = pl.BlockSpec((t, DKB), mk(lambda d, h: (0, 32 + h)))
    k = pl.BlockSpec((t, DKB), mk(lambda d, h: (0, 36 + h)))
    v = pl.BlockSpec((t, DVB), mk(lambda d, h: (0, 20 + h)))
    dk = pl.BlockSpec((None, t, DKB), mk(lambda d, h: (d, 0, h)))
    dv = pl.BlockSpec((None, t, DVB), mk(lambda d, h: (d, 0, h)))
    e = pl.BlockSpec((None, None, t // CH, 8, 128), mk(lambda d, h: (d, h, 0, 0, 0)))
    s = pl.BlockSpec((None, None, t // CH, DVB, DKB), mk(lambda d, h: (d, h, 0, 0, 0)))
    return q, k, v, dk, dv, e, s


def _gla_intra_fwd(p, gk):
    t = p.shape[0]
    n = t // CH

    def body(q_ref, k_ref, v_ref, g_ref, qg_ref, kd_ref, in_ref, e_ref):
        cs = _Consts(pl.program_id(0) == 1)

        def step(i, carry):
            r = _srows(i)
            qg, kd, intra, el = _gla_intra(q_ref[r, :], k_ref[r, :], v_ref[r, :], g_ref[r, :], cs)
            qg_ref[r, :] = qg
            kd_ref[r, :] = kd
            in_ref[r, :] = intra
            for c in range(SC // CH):
                e_ref[(SC // CH) * i + c] = el[c * CH:c * CH + 8]
            return carry

        lax.fori_loop(0, t // SC, step, 0)

    q, k, v, dk, dv, e, _ = _gla_specs(t, "dh")
    return pl.pallas_call(
        body, name="gla_intra_fwd", grid=(2, NB),
        in_specs=[q, k, v, dk], out_specs=[dk, dk, dv, e],
        out_shape=[jax.ShapeDtypeStruct((2, t, NB * DKB), F32), jax.ShapeDtypeStruct((2, t, NB * DKB), F32),
                   jax.ShapeDtypeStruct((2, t, D), F32), jax.ShapeDtypeStruct((2, NB, n, 8, 128), F32)],
        compiler_params=_cparams(("parallel", "parallel")),
    )(p, p, p, gk)


GLA_HB = 4


def _gla_v_spec(t, along):
    nt = t // SCAN_TB

    def tmap(d, tt):
        fwd = tt + d * (nt - 1 - 2 * tt)
        return fwd if along > 0 else nt - 1 - fwd

    return pl.BlockSpec((SCAN_TB, GLA_HB * DVB), lambda d, h, tt: (tmap(d, tt), 5120 // (GLA_HB * DVB) + h))


def _gla_scan_fwd(p, qg, kd, intra, e):
    t = p.shape[0]
    tokk, per, _, _, grid, nb = _scan_specs(t, DKB, NB, GLA_HB, +1)
    tokv = _scan_specs(t, DVB, NB, GLA_HB, +1)[0]

    def body(v_ref, qg_ref, kd_ref, in_ref, e_ref, o_ref, s_ref, state):
        rev = pl.program_id(0) == 1

        @pl.when(pl.program_id(2) == 0)
        def _():
            state[...] = jnp.zeros_like(state)

        def step(i, sts):
            ci = jnp.where(rev, nb - 1 - i, i)
            r = _rows(ci)
            out = []
            for hh, st in enumerate(sts):
                ck = slice(hh * DKB, (hh + 1) * DKB)
                cv = slice(hh * DVB, (hh + 1) * DVB)
                s_ref[hh, ci] = st
                o, stn = _gla_scan(qg_ref[r, ck], kd_ref[r, ck], v_ref[r, cv], e_ref[hh, ci][0:1], st)
                o_ref[r, cv] = o + in_ref[r, cv]
                out.append(stn)
            return tuple(out)

        sts = lax.fori_loop(0, nb, step, tuple(state[hh] for hh in range(GLA_HB)))
        for hh, st in enumerate(sts):
            state[hh] = st

    return pl.pallas_call(
        body, name="gla_scan_fwd", grid=grid,
        in_specs=[_gla_v_spec(t, +1), tokk, tokk, tokv, per(8, 128)], out_specs=[tokv, per(DVB, DKB)],
        out_shape=[jax.ShapeDtypeStruct((2, t, D), F32), jax.ShapeDtypeStruct((2, NB, t // CH, DVB, DKB), F32)],
        scratch_shapes=[pltpu.VMEM((GLA_HB, DVB, DKB), F32)],
        compiler_params=_cparams(("parallel", "parallel", "arbitrary")),
    )(p, qg, kd, intra, e)


def _gla_scan_bwd(p, qg, kd, e, ssave, do):
    t = p.shape[0]
    tokk, per, _, shared, grid, nb = _scan_specs(t, DKB, NB, GLA_HB, -1)
    tokv = _scan_specs(t, DVB, NB, GLA_HB, -1)[0]

    def body(v_ref, qg_ref, kd_ref, e_ref, s_ref, do_ref, dqg_ref, dkd_ref, dv_ref, de_ref, state):
        rev = pl.program_id(0) == 1

        @pl.when(pl.program_id(2) == 0)
        def _():
            state[...] = jnp.zeros_like(state)

        def step(i, dsts):
            ci = jnp.where(rev, i, nb - 1 - i)
            r = _rows(ci)
            out = []
            for hh, dst in enumerate(dsts):
                ck = slice(hh * DKB, (hh + 1) * DKB)
                cv = slice(hh * DVB, (hh + 1) * DVB)
                _, vjp = jax.vjp(_gla_scan, qg_ref[r, ck], kd_ref[r, ck], v_ref[r, cv], e_ref[hh, ci][0:1],
                                 s_ref[hh, ci])
                dqg, dkd, dv, de, dstn = vjp((do_ref[r, cv], dst))
                dqg_ref[r, ck] = dqg
                dkd_ref[r, ck] = dkd
                dv_ref[r, cv] = dv
                de_ref[hh, ci] = _first_row(de)
                out.append(dstn)
            return tuple(out)

        dsts = lax.fori_loop(0, nb, step, tuple(state[hh] for hh in range(GLA_HB)))
        for hh, dst in enumerate(dsts):
            state[hh] = dst

    return pl.pallas_call(
        body, name="gla_scan_bwd", grid=grid,
        in_specs=[_gla_v_spec(t, -1), tokk, tokk, per(8, 128), per(DVB, DKB), shared(DVB)],
        out_specs=[tokk, tokk, tokv, per(8, 128)],
        out_shape=[jax.ShapeDtypeStruct((2, t, NB * DKB), F32), jax.ShapeDtypeStruct((2, t, NB * DKB), F32),
                   jax.ShapeDtypeStruct((2, t, D), F32), jax.ShapeDtypeStruct((2, NB, t // CH, 8, 128), F32)],
        scratch_shapes=[pltpu.VMEM((GLA_HB, DVB, DKB), F32)],
        compiler_params=_cparams(("parallel", "parallel", "arbitrary")),
    )(p, qg, kd, e, ssave, do)


def _gla_intra_bwd(p, gk, dqg, dkd, dvs, de, do):
    t = p.shape[0]
    n = t // CH

    def body(q_ref, k_ref, v_ref, g_ref, dqg_ref, dkd_ref, dvs_ref, de_ref, do_ref,
             dq_ref, dk_ref, dv_ref, dg_ref):
        d = pl.program_id(1)
        cs = _Consts(d == 1)

        @pl.when(d == 0)
        def _():
            dq_ref[...] = jnp.zeros_like(dq_ref)
            dk_ref[...] = jnp.zeros_like(dk_ref)
            dv_ref[...] = jnp.zeros_like(dv_ref)

        def step(i, carry):
            r = _srows(i)
            f = lambda q, k, v, g: _gla_intra(q, k, v, g, cs)
            _, vjp = jax.vjp(f, q_ref[r, :], k_ref[r, :], v_ref[r, :], g_ref[r, :])
            dq, dk, dv, dg = vjp((dqg_ref[r, :], dkd_ref[r, :], do_ref[r, :], _chunk_rows(de_ref, i)))
            dq_ref[r, :] += dq
            dk_ref[r, :] += dk
            dv_ref[r, :] += dv + dvs_ref[r, :]
            dg_ref[r, :] = dg
            return carry

        lax.fori_loop(0, t // SC, step, 0)

    q, k, v, dk, dv, e_s, _ = _gla_specs(t, "hd")
    hk = pl.BlockSpec((t, DKB), lambda h, d: (0, h))
    hv = pl.BlockSpec((t, DVB), lambda h, d: (0, h))
    return pl.pallas_call(
        body, name="gla_intra_bwd", grid=(NB, 2),
        in_specs=[q, k, v, dk, dk, dk, dv, e_s, hv],
        out_specs=[hk, hk, hv, dk],
        out_shape=[jax.ShapeDtypeStruct((t, NB * DKB), F32), jax.ShapeDtypeStruct((t, NB * DKB), F32),
                   jax.ShapeDtypeStruct((t, D), F32), jax.ShapeDtypeStruct((2, t, NB * DKB), F32)],
        compiler_params=_cparams(("arbitrary", "arbitrary")),
    )(p, p, p, gk, dqg, dkd, dvs, de, do)


def _seg_gate(o, z, w):
    return _rms(o, w) * _silu(z)


def _seg_merge(ya, yb, ga, gb):
    return _sigmoid(ga) * ya + _sigmoid(gb) * yb


def _seg_loss(out, x, tgt, w):
    err = x + _rms(out, w) - tgt
    return 0.5 * jnp.sum(jnp.mean(err * err, axis=-1, keepdims=True), axis=0, keepdims=True)


def _post(oa2, ob2, p, x, tgt, gdn_w, gla_w, lnpost, w3, tm=128):
    t = x.shape[0]

    def body(oa_ref, ob_ref, z_ref, gb_ref, ga_ref, gB_ref, x_ref, t_ref, aw_ref, bw_ref, lw_ref, w_ref,
             loss_ref, doa_ref, dob_ref, dz_ref, dgb_ref, dga_ref, dgB_ref, dy_ref,
             dw_ref, daw_ref, dbw_ref, dlw_ref):
        first = pl.program_id(0) == 0
        oa = oa_ref[0] + oa_ref[1]
        ob = ob_ref[0] + ob_ref[1]
        z, gb = z_ref[...], gb_ref[...]
        aw, bw = aw_ref[...], bw_ref[...]
        rs = D // NSHARD

        def mat(a, m):
            return sum(jnp.dot(a[:, s * rs:(s + 1) * rs], w_ref[s, m], preferred_element_type=F32)
                       for s in range(NSHARD))

        def mat_t(g, m):
            return jnp.concatenate([_bdot(g, w_ref[s, m], 1, 1) for s in range(NSHARD)], axis=1)

        def add_dw(a, g, m):
            for s in range(NSHARD):
                dw_ref[s, m] += _bdot(a[:, s * rs:(s + 1) * rs], g, 0, 0)

        pa = [jax.vjp(_seg_gate, oa[:, h * DA:(h + 1) * DA], z[:, h * DA:(h + 1) * DA], aw) for h in range(NA)]
        pb = [jax.vjp(_seg_gate, ob[:, h * DVB:(h + 1) * DVB], gb[:, h * DVB:(h + 1) * DVB], bw)
              for h in range(NB)]
        a1 = jnp.concatenate([v for v, _ in pa], axis=1).astype(BF16)
        a2 = jnp.concatenate([v for v, _ in pb], axis=1).astype(BF16)
        ya = mat(a1, 0)
        yb = mat(a2, 1)
        merged, vjp_m = jax.vjp(_seg_merge, ya, yb, ga_ref[...], gB_ref[...])
        mb = merged.astype(BF16)
        out = mat(mb, 2)
        loss, vjp_l = jax.vjp(_seg_loss, out, x_ref[...], t_ref[...], lw_ref[...])
        dout, dyres, _, dlw = vjp_l(jnp.ones((1, 1), F32))
        dy_ref[...] = dyres
        doutb = dout.astype(BF16)
        dmerged = mat_t(doutb, 2)
        dya, dyb, dga, dgB = vjp_m(dmerged)
        dga_ref[...] = dga.astype(BF16)
        dgB_ref[...] = dgB.astype(BF16)
        dyab, dybb = dya.astype(BF16), dyb.astype(BF16)
        da1 = mat_t(dyab, 0)
        da2 = mat_t(dybb, 1)

        daw = jnp.zeros_like(aw)
        for h in range(NA):
            sl = slice(h * DA, (h + 1) * DA)
            do, dz, dw = pa[h][1](da1[:, sl])
            doa_ref[:, sl] = do
            dz_ref[:, sl] = dz.astype(BF16)
            daw = daw + dw
        dbw = jnp.zeros_like(bw)
        for h in range(NB):
            sl = slice(h * DVB, (h + 1) * DVB)
            do, dg, dw = pb[h][1](da2[:, sl])
            dob_ref[:, sl] = do
            dgb_ref[:, sl] = dg.astype(BF16)
            dbw = dbw + dw

        @pl.when(first)
        def _():
            loss_ref[...] = jnp.zeros_like(loss_ref)
            dw_ref[...] = jnp.zeros_like(dw_ref)
            daw_ref[...] = jnp.zeros_like(daw_ref)
            dbw_ref[...] = jnp.zeros_like(dbw_ref)
            dlw_ref[...] = jnp.zeros_like(dlw_ref)

        loss_ref[...] += jnp.broadcast_to(loss, loss_ref.shape)
        add_dw(a1, dyab, 0)
        add_dw(a2, dybb, 1)
        add_dw(mb, doutb, 2)
        daw_ref[...] += jnp.broadcast_to(daw, daw_ref.shape)
        dbw_ref[...] += jnp.broadcast_to(dbw, dbw_ref.shape)
        dlw_ref[...] += jnp.broadcast_to(dlw, dlw_ref.shape)

    two = pl.BlockSpec((2, tm, D), lambda i: (0, i, 0))
    pcol = lambda c: pl.BlockSpec((tm, D), lambda i: (i, c))
    tok = pl.BlockSpec((tm, D), lambda i: (i, 0))
    row = lambda n: pl.BlockSpec((1, n), lambda i: (0, 0))
    row8 = lambda n: pl.BlockSpec((8, n), lambda i: (0, 0))
    once = pl.Buffered(1)
    tokf = jax.ShapeDtypeStruct((t, D), F32)
    tokb = jax.ShapeDtypeStruct((t, D), BF16)
    wspec = pl.BlockSpec((NSHARD, 3, D // NSHARD, D), lambda i: (0, 0, 0, 0), pipeline_mode=once)
    return pl.pallas_call(
        body, name="post", grid=(t // tm,),
        in_specs=[two, two, pcol(3), pcol(6), pcol(7), pcol(8), tok, tok, row(DA), row(DVB), row(D), wspec],
        out_specs=[row8(128), tok, tok, tok, tok, tok, tok, tok, wspec, row8(DA), row8(DVB), row8(D)],
        out_shape=[jax.ShapeDtypeStruct((8, 128), F32), tokf, tokf, tokb, tokb, tokb, tokb, tokf,
                   jax.ShapeDtypeStruct((NSHARD, 3, D // NSHARD, D), F32),
                   jax.ShapeDtypeStruct((8, DA), F32), jax.ShapeDtypeStruct((8, DVB), F32),
                   jax.ShapeDtypeStruct((8, D), F32)],
        compiler_params=_cparams(("arbitrary",), vmem_mb=56),
    )(oa2, ob2, p, p, p, p, x, tgt, gdn_w, gla_w, lnpost, w3)


def _adam_math(w, g, m, v):
    nm = B1 * m + (1.0 - B1) * g
    nv = B2 * v + (1.0 - B2) * (g * g)
    m_hat = nm / (1.0 - B1 ** STEP)
    v_hat = nv / (1.0 - B2 ** STEP)
    return -LR * (m_hat / (jnp.sqrt(v_hat) + ADAM_EPS) + WD * w), nm, nv


SMALL_SLOTS = (("ln_pre_w", 0, 1024, 0), ("a_log_fwd", 1024, 8, 0), ("a_log_bwd", 1024, 8, 8),
               ("dt_bias_fwd", 1152, 8, 0), ("dt_bias_bwd", 1152, 8, 8), ("gdn_norm_w", 1280, 128, 0),
               ("gk_b2_fwd", 1408, 512, 0), ("gk_b2_bwd", 1920, 512, 0), ("gla_norm_w", 2432, 256, 0),
               ("ln_post_w", 2688, 1024, 0))
SMALL_W = 3840


def _adam_small(gsum, ws, ms, vs):
    nw = len(SMALL_SLOTS)

    def body(g_ref, *refs):
        w_refs, m_refs, v_refs, outs = refs[0:nw], refs[nw:2 * nw], refs[2 * nw:3 * nw], refs[3 * nw:]
        for i, (_, off, n, shift) in enumerate(SMALL_SLOTS):
            slot = g_ref[0:1, off:off + max(n, 128)]
            if shift:
                slot = pltpu.roll(slot, 128 - shift, 1)
            g = slot[:, 0:n]
            d, nm, nv = _adam_math(w_refs[i][...], g, m_refs[i][...], v_refs[i][...])
            for k, val in enumerate((g, d, nm, nv)):
                outs[4 * i + k][...] = val

    vm = pl.BlockSpec(memory_space=pltpu.VMEM)
    res = pl.pallas_call(
        body, name="adam_small", in_specs=[vm] * (1 + 3 * nw), out_specs=[vm] * (4 * nw),
        out_shape=[jax.ShapeDtypeStruct((1, n), F32) for _, _, n, _ in SMALL_SLOTS for _ in range(4)],
    )(gsum, *ws, *ms, *vs)
    return {name: res[4 * i:4 * i + 4] for i, (name, _, _, _) in enumerate(SMALL_SLOTS)}


def _adam(w, mine, got, m, v, tr, tile0=0, name=""):
    rows, cols = w.shape
    nh = mine.shape[0] // tr

    def body(c_ref, w_ref, a_ref, b_ref, m_ref, v_ref, g_ref, d_ref, nm_ref, nv_ref):
        half = (tile0 + pl.program_id(0)) // nh
        g = jnp.where(half == c_ref[0], a_ref[...], b_ref[...])
        d, nm, nv = _adam_math(w_ref[...], g, m_ref[...], v_ref[...])
        g_ref[...] = g
        d_ref[...] = d
        nm_ref[...] = nm
        nv_ref[...] = nv

    blk = pl.BlockSpec((tr, cols), lambda i, cc: (i, 0))
    half = pl.BlockSpec((tr, cols), lambda i, cc: ((tile0 + i) % nh, 0))
    shp = jax.ShapeDtypeStruct((rows, cols), F32)
    return pl.pallas_call(
        body, name=f"adam_{name}{rows}x{cols}",
        grid_spec=pltpu.PrefetchScalarGridSpec(
            num_scalar_prefetch=1, grid=(rows // tr,),
            in_specs=[blk, half, half, blk, blk], out_specs=[blk] * 4),
        out_shape=[shp] * 4,
        compiler_params=_cparams(("parallel",)),
    )(lax.axis_index("c").reshape(1), w, mine, got, m, v)


def _adam_shard_small(mine, got, ws, ms, vs):
    def body(c_ref, a_ref, b_ref, *refs):
        w_refs, m_refs, v_refs, outs = refs[0:3], refs[3:6], refs[6:9], refs[9:]
        south = c_ref[0] == 0
        g = jnp.concatenate([jnp.where(south, a_ref[...], b_ref[...]),
                             jnp.where(south, b_ref[...], a_ref[...])], axis=0)
        grads = (g[0:5], g[8:24, 0:128], g[8:24, 128:256])
        for i, gi in enumerate(grads):
            d, nm, nv = _adam_math(w_refs[i][...], gi, m_refs[i][...], v_refs[i][...])
            for k, val in enumerate((gi, d, nm, nv)):
                outs[4 * i + k][...] = val

    vm = pl.BlockSpec(memory_space=pltpu.VMEM)
    res = pl.pallas_call(
        body, name="adam_shard_small",
        in_specs=[pl.BlockSpec(memory_space=pltpu.SMEM)] + [vm] * 11, out_specs=[vm] * 12,
        out_shape=[jax.ShapeDtypeStruct(w.shape, F32) for w in ws for _ in range(4)],
    )(lax.axis_index("c").reshape(1), mine, got, *ws, *ms, *vs)
    return [res[4 * i:4 * i + 4] for i in range(3)]


def _sum_cast(own, got):
    ns, _, r, c = own.shape
    tr = r // 4 if r % 64 == 0 else r

    def body(c_ref, a_ref, b_ref, f_ref, h_ref):
        s = a_ref[...] + b_ref[...]
        f_ref[...] = s
        h_ref[...] = s.astype(BF16)

    return pl.pallas_call(
        body, name=f"sum_cast_{r}x{c}",
        grid_spec=pltpu.PrefetchScalarGridSpec(
            num_scalar_prefetch=1, grid=(ns, r // tr),
            in_specs=[pl.BlockSpec((None, None, tr, c), lambda s, i, cc: (s, cc[0], i, 0)),
                      pl.BlockSpec((None, tr, c), lambda s, i, cc: (s, i, 0))],
            out_specs=[pl.BlockSpec((None, tr, c), lambda s, i, cc: (s, i, 0)),
                       pl.BlockSpec((None, tr, c), lambda s, i, cc: (s, i, 0))]),
        out_shape=[jax.ShapeDtypeStruct((ns, r, c), F32), jax.ShapeDtypeStruct((ns, r, c), BF16)],
        compiler_params=_cparams(("parallel", "parallel")),
    )(lax.axis_index("c").reshape(1), own, got)


def _sum4(mine, got):
    _, r, c = mine.shape
    tr = r // 4 if r % 64 == 0 else r

    def body(s_ref, a_ref, g_ref, o_ref):
        acc = a_ref[...] + g_ref[0].astype(F32)
        acc = acc + g_ref[1].astype(F32)
        o_ref[...] = acc + g_ref[2].astype(F32)

    shard = (2 * lax.axis_index("x") + lax.axis_index("y")).reshape(1)
    return pl.pallas_call(
        body, name=f"sum4_{r}x{c}",
        grid_spec=pltpu.PrefetchScalarGridSpec(
            num_scalar_prefetch=1, grid=(r // tr,),
            in_specs=[pl.BlockSpec((None, tr, c), lambda i, ss: (ss[0], i, 0)),
                      pl.BlockSpec((3, tr, c), lambda i, ss: (0, i, 0))],
            out_specs=pl.BlockSpec((tr, c), lambda i, ss: (i, 0))),
        out_shape=jax.ShapeDtypeStruct((r, c), F32),
        compiler_params=_cparams(("parallel",)),
    )(shard, mine, got)


def _place():
    x, y, c = lax.axis_index("x"), lax.axis_index("y"), lax.axis_index("c")
    chips = [(1 - x, y), (x, 1 - y), (1 - x, 1 - y)]
    return x, y, c, chips


def _gather_weights(parts):
    npart = len(parts)

    def body(*refs):
        ins, outs = refs[:npart], refs[npart:2 * npart]
        send_sems, recv_sems = refs[2 * npart:]
        x, y, c, chips = _place()
        sibling = (x, y, 1 - c)
        mine = 2 * x + y

        def remote(k, p, shard, half, to, src=None):
            dst = outs[p].at[shard, half]
            return pltpu.make_async_remote_copy(
                src_ref=dst if src is None else src, dst_ref=dst,
                send_sem=send_sems.at[k], recv_sem=recv_sems.at[k], device_id=to, device_id_type=MESH)

        first = [remote(j * npart + p, p, mine, c, (*chip, c), src=ins[p].at[c])
                 for j, chip in enumerate(chips) for p in range(npart)]
        for cp in first:
            cp.start()
        passed = []
        for j, (cx, cy) in enumerate(chips):
            for p in range(npart):
                remote(j * npart + p, p, 2 * cx + cy, c, (x, y, c)).wait_recv()
                fw = remote((3 + j) * npart + p, p, 2 * cx + cy, c, sibling)
                fw.start()
                passed.append(fw)
        for j, (cx, cy) in enumerate(chips):
            for p in range(npart):
                remote((3 + j) * npart + p, p, 2 * cx + cy, 1 - c, (x, y, c)).wait_recv()
        for cp in first + passed:
            cp.wait_send()

    got = pl.pallas_call(
        body, name="gather_weights",
        in_specs=[ANY] * npart, out_specs=[ANY] * npart,
        out_shape=[jax.ShapeDtypeStruct((NSHARD,) + a.shape, a.dtype) for a in parts],
        scratch_shapes=[pltpu.SemaphoreType.DMA((6 * npart,)), pltpu.SemaphoreType.DMA((6 * npart,))],
    )(*parts)
    mine = 2 * lax.axis_index("x") + lax.axis_index("y")
    return [lax.dynamic_update_index_in_dim(g, a, mine, 0) for g, a in zip(got, parts)]


def _swap_halves(parts, tag=""):
    npart = len(parts)

    def body(*refs):
        ins, outs = refs[:npart], refs[npart:2 * npart]
        send_sems, recv_sems = refs[2 * npart:]
        x, y, c, _ = _place()
        cps = [pltpu.make_async_remote_copy(
            src_ref=ins[p].at[s, 1 - c], dst_ref=outs[p].at[s],
            send_sem=send_sems.at[s * npart + p], recv_sem=recv_sems.at[s * npart + p],
            device_id=(x, y, 1 - c), device_id_type=MESH) for s in range(NSHARD) for p in range(npart)]
        for cp in cps:
            cp.start()
        for cp in cps:
            cp.wait()

    return pl.pallas_call(
        body, name="swap_halves" + tag, in_specs=[ANY] * npart, out_specs=[ANY] * npart,
        out_shape=[jax.ShapeDtypeStruct((NSHARD,) + a.shape[2:], a.dtype) for a in parts],
        scratch_shapes=[pltpu.SemaphoreType.DMA((NSHARD * npart,)), pltpu.SemaphoreType.DMA((NSHARD * npart,))],
    )(*parts)


def _scatter_shards(parts):
    npart = len(parts)

    def body(*refs):
        ins, outs = refs[:npart], refs[npart:2 * npart]
        send_sems, recv_sems = refs[2 * npart:]
        x, y, c, chips = _place()
        cps = [pltpu.make_async_remote_copy(
            src_ref=ins[p].at[2 * cx + cy], dst_ref=outs[p].at[j],
            send_sem=send_sems.at[j * npart + p], recv_sem=recv_sems.at[j * npart + p],
            device_id=(cx, cy, c), device_id_type=MESH)
            for j, (cx, cy) in enumerate(chips) for p in range(npart)]
        for cp in cps:
            cp.start()
        for cp in cps:
            cp.wait()

    return pl.pallas_call(
        body, name="scatter_shards", in_specs=[ANY] * npart, out_specs=[ANY] * npart,
        out_shape=[jax.ShapeDtypeStruct((3,) + a.shape[1:], a.dtype) for a in parts],
        scratch_shapes=[pltpu.SemaphoreType.DMA((3 * npart,)), pltpu.SemaphoreType.DMA((3 * npart,))],
    )(*parts)


HBM = pl.BlockSpec(memory_space=pltpu.HBM)
SEM = pl.BlockSpec(memory_space=pltpu.SEMAPHORE)
EFFECT = pltpu.SideEffectType.DATAFLOW_SIDE_EFFECTING


def _scatter_copies(srcs, lands, send_sems, recv_sems, waiting):
    x, y, c, chips = _place()
    n = len(srcs)
    return [pltpu.make_async_remote_copy(
        src_ref=srcs[p].at[2 * cx + cy], dst_ref=lands[p].at[j],
        send_sem=send_sems.at[j * n + p], recv_sem=recv_sems.at[j * n + p],
        device_id=(cx, cy, c), device_id_type=MESH)
        for j, (cx, cy) in enumerate(chips) for p in range(n)]


def _proj_copies(srcs, lands, send_sems, recv_sems, waiting):
    x, y, c, chips = _place()
    mine = 2 * x + y
    return [pltpu.make_async_remote_copy(
        src_ref=srcs[0].at[c], dst_ref=lands[0].at[mine, c],
        send_sem=send_sems.at[2 * j + to], recv_sem=recv_sems.at[2 * j + (to if waiting else c)],
        device_id=(cx, cy, to), device_id_type=MESH)
        for j, (cx, cy) in enumerate(chips) for to in range(2)]


def _start_copies(copies, nsem, parts, lands, name, after=None):
    n = len(parts)
    extra = [] if after is None else [after]

    def body(*refs):
        outs = refs[2 * n + len(extra):]
        for cp in copies(refs[:n], refs[n:2 * n], outs[0], outs[1], False):
            cp.start()
        outs[-1][...] = jnp.zeros_like(outs[-1])

    res = pl.pallas_call(
        body, name=name,
        out_shape=(pltpu.SemaphoreType.DMA((nsem,)), pltpu.SemaphoreType.DMA((nsem,)),
                   *[pltpu.HBM(a.shape, a.dtype) for a in parts], *[pltpu.HBM(a.shape, a.dtype) for a in lands],
                   jax.ShapeDtypeStruct((8, 128), F32)),
        in_specs=[HBM] * (2 * n) + [ANY] * len(extra),
        out_specs=(SEM, SEM, *[HBM] * (2 * n), pl.BlockSpec(memory_space=pltpu.VMEM)),
        input_output_aliases={i: 2 + i for i in range(2 * n)},
        compiler_params=pltpu.CompilerParams(has_side_effects=EFFECT),
    )(*[pltpu.with_memory_space_constraint(a, pltpu.HBM) for a in parts],
      *[pltpu.with_memory_space_constraint(a, pltpu.HBM) for a in lands], *extra)
    return res[0], res[1], res[2:2 + n], res[2 + n:2 + 2 * n], res[-1]


def _wait_copies(copies, started, after, name):
    send_sems, recv_sems, srcs, lands, _ = started
    n = len(srcs)

    def body(*refs):
        for cp in copies(refs[:n], refs[n:2 * n], refs[2 * n], refs[2 * n + 1], True):
            cp.wait_send()
            cp.wait_recv()

    res = pl.pallas_call(
        body, name=name,
        out_shape=tuple(pltpu.HBM(a.shape, a.dtype) for a in (*srcs, *lands)),
        in_specs=[HBM] * (2 * n) + [SEM, SEM, ANY], out_specs=tuple([HBM] * (2 * n)),
        input_output_aliases={i: i for i in range(2 * n)},
        compiler_params=pltpu.CompilerParams(has_side_effects=EFFECT),
    )(*srcs, *lands, send_sems, recv_sems, after)
    return res[n:]


def _join_halves(parts):
    npart = len(parts)

    def body(*refs):
        ins, outs = refs[:npart], refs[npart:2 * npart]
        send_sems, recv_sems = refs[2 * npart:]
        x, y, c, _ = _place()
        cps = [pltpu.make_async_remote_copy(
            src_ref=ins[p], dst_ref=outs[p], send_sem=send_sems.at[p], recv_sem=recv_sems.at[p],
            device_id=(x, y, 1 - c), device_id_type=MESH) for p in range(npart)]
        for cp in cps:
            cp.start()
        for cp in cps:
            cp.wait()

    return pl.pallas_call(
        body, name="join_halves", in_specs=[ANY] * npart, out_specs=[ANY] * npart,
        out_shape=[jax.ShapeDtypeStruct(a.shape, a.dtype) for a in parts],
        scratch_shapes=[pltpu.SemaphoreType.DMA((npart,)), pltpu.SemaphoreType.DMA((npart,))],
    )(*parts)


def _allreduce_small(v):
    r, ncol = v.shape

    def body(v_ref, o_ref, buf, send_sems, recv_sems):
        x, y, c, _ = _place()
        me = 4 * x + 2 * y + c
        buf[me] = v_ref[...]
        cps = []
        for k in range(1, 8):
            px, py, pc = x ^ (k >> 2), y ^ ((k >> 1) & 1), c ^ (k & 1)
            cps.append(pltpu.make_async_remote_copy(
                src_ref=v_ref, dst_ref=buf.at[me], send_sem=send_sems.at[k - 1], recv_sem=recv_sems.at[k - 1],
                device_id=(px, py, pc), device_id_type=MESH))
        for cp in cps:
            cp.start()
        for k in range(1, 8):
            px, py, pc = x ^ (k >> 2), y ^ ((k >> 1) & 1), c ^ (k & 1)
            pltpu.make_async_remote_copy(
                src_ref=v_ref, dst_ref=buf.at[4 * px + 2 * py + pc], send_sem=send_sems.at[k - 1],
                recv_sem=recv_sems.at[k - 1], device_id=(px, py, pc), device_id_type=MESH).wait_recv()
        for cp in cps:
            cp.wait_send()
        acc = buf[0]
        for d in range(1, 8):
            acc = acc + buf[d]
        o_ref[...] = acc

    return pl.pallas_call(
        body, name="allreduce_small",
        in_specs=[pl.BlockSpec(memory_space=pltpu.VMEM)], out_specs=pl.BlockSpec(memory_space=pltpu.VMEM),
        out_shape=jax.ShapeDtypeStruct((r, ncol), F32),
        scratch_shapes=[pltpu.VMEM((8, r, ncol), F32), pltpu.SemaphoreType.DMA((7,)), pltpu.SemaphoreType.DMA((7,))],
    )(v)


def _permute_rows(shards):
    w0, w1, w2, w3 = shards
    zeros = jnp.zeros((NPERM - 9280, w0.shape[1]), w0.dtype)
    return jnp.concatenate([w0, w1[0:1776], w1[1808:2320], w2, w3[0:240], w3[272:2320],
                            w1[1776:1808], w3[240:272], zeros], axis=0)


def _unpermute_rows(g):
    s1 = jnp.concatenate([g[2320:4096], g[9216:9248], g[4096:4608]], axis=0)
    s3 = jnp.concatenate([g[6928:7168], g[9248:9280], g[7168:9216]], axis=0)
    return jnp.stack([g[0:2320], s1, g[4608:6928], s3], axis=0)


def _pack_shard_small(conv, w2f, w2b):
    top = jnp.pad(conv, ((0, 8 - conv.shape[0]), (0, 0)))
    mid = jnp.pad(jnp.concatenate([w2f, w2b], axis=1), ((0, 0), (0, 768 - 256)))
    return jnp.concatenate([top, mid, jnp.zeros((8, 768), conv.dtype)], axis=0)


def _unpack_shard_small(a):
    return a[0:5], a[8:24, 0:128], a[8:24, 128:256]


def kernel(x, ln_pre_w, w_in, conv_w, a_log_fwd, a_log_bwd, dt_bias_fwd, dt_bias_bwd, gdn_norm_w, w_proj_gdn, gk_w2_fwd, gk_b2_fwd, gk_w2_bwd, gk_b2_bwd, gla_norm_w, w_proj_gla, w_out, ln_post_w, loss_target, m_ln_pre_w, m_w_in, m_conv_w, m_a_log_fwd, m_a_log_bwd, m_dt_bias_fwd, m_dt_bias_bwd, m_gdn_norm_w, m_w_proj_gdn, m_gk_w2_fwd, m_gk_b2_fwd, m_gk_w2_bwd, m_gk_b2_bwd, m_gla_norm_w, m_w_proj_gla, m_w_out, m_ln_post_w, v_ln_pre_w, v_w_in, v_conv_w, v_a_log_fwd, v_a_log_bwd, v_dt_bias_fwd, v_dt_bias_bwd, v_gdn_norm_w, v_w_proj_gdn, v_gk_w2_fwd, v_gk_b2_fwd, v_gk_w2_bwd, v_gk_b2_bwd, v_gla_norm_w, v_w_proj_gla, v_w_out, v_ln_post_w):
    t = x.shape[1]
    x2, tgt = x[0], loss_target[0]

    win_l = w_in[0].T.astype(BF16).reshape(2, SHW // 2, D)
    proj_l = jnp.concatenate([w_proj_gdn[0], w_proj_gla[0], w_out[0]], axis=0).astype(BF16).reshape(2, 384, D)
    small_l = _pack_shard_small(conv_w[0], gk_w2_fwd[0], gk_w2_bwd[0]).reshape(2, 16, 768)
    win_g, small_g = _gather_weights([win_l, small_l])
    proj_started = _start_copies(_proj_copies, 6, [proj_l], [lax.empty((NSHARD, 2, 384, D), BF16)],
                                 "gather_proj_start", after=small_g)
    wperm = _permute_rows(win_g.reshape(NSHARD, SHW, D))
    small_g = small_g.reshape(NSHARD, 32, 768)
    convw = small_g[:, 0:8, :].transpose(1, 0, 2).reshape(8, 3 * D)
    w2f = small_g[:, 8:24, 0:128].transpose(1, 0, 2).reshape(16, 512)
    w2b = small_g[:, 8:24, 128:256].transpose(1, 0, 2).reshape(16, 512)
    w2f_pad = jnp.pad(w2f, ((32, 80), (0, 0)))
    w2b_pad = jnp.pad(w2b, ((48, 64), (0, 0)))
    alog_row = jnp.pad(jnp.concatenate([a_log_fwd, a_log_bwd], axis=1), ((0, 0), (0, 112)))
    dt_row = jnp.pad(jnp.concatenate([dt_bias_fwd, dt_bias_bwd], axis=1), ((0, 0), (0, 112)))

    p, h = _inproj(x2, ln_pre_w + proj_started[4][0:1, 0:1], wperm)
    qn, kn, vc = (_qkv_fwd(p, convw, kind) for kind in range(3))
    gsm, gk = _gates_fwd(p, alog_row, dt_row, w2f_pad, gk_b2_fwd, w2b_pad, gk_b2_bwd)
    g2, b2 = _gcum_fwd(gsm)
    u, w, at, qd, kd, el, tinv = _gdn_intra_fwd(qn, kn, vc, g2, b2)
    oa2, sa = _gdn_scan_fwd(u, w, at, qd, kd, el)
    qg, kdb, intra, elb = _gla_intra_fwd(p, gk)
    ob2, sb = _gla_scan_fwd(p, qg, kdb, intra, elb)

    (proj_land,) = _wait_copies(_proj_copies, proj_started, ob2, "gather_proj_wait")
    mine = 2 * lax.axis_index("x") + lax.axis_index("y")
    w3 = lax.dynamic_update_index_in_dim(proj_land, proj_l, mine, 0).reshape(NSHARD, 3, D // NSHARD, D)
    (loss8, doa, dob, dz, dgb, dga, dgB, dyres, dw3, dgdn_w, dgla_w, dlnpost) = _post(
        oa2, ob2, p, x2, tgt, gdn_norm_w, gla_norm_w, ln_post_w, w3)

    g_proj = dw3.reshape(NSHARD, 2, 384, D)
    sum_proj = _sum_cast(g_proj, _swap_halves([g_proj], "_proj")[0])
    started_proj = _start_copies(_scatter_copies, 3, [sum_proj[1]], [lax.empty((3, 384, D), BF16)],
                                 "scatter_proj_start")
    du, dw, dat, dqd, dkd, del_ = _gdn_scan_bwd(u, w, at, qd, kd, el + started_proj[4][0, 0], sa, doa)
    dqn, dkn, dvc, dg2, db2 = _gdn_intra_bwd(qn, kn, vc, g2, b2, tinv, du, dw, dat, dqd, dkd, del_)
    dgsm = _gcum_bwd(gsm, dg2, db2)
    dqg, dkdb, dvs, delb = _gla_scan_bwd(p, qg, kdb, elb, sb, dob)
    dqb, dkb, dvb, dgk = _gla_intra_bwd(p, gk, dqg, dkdb, dvs, delb, dob)
    (dps, dalog8, ddt8, dw2f_pad, db2f8, dw2b_pad, db2b8) = _gates_bwd(
        p, alog_row, dt_row, w2f_pad, gk_b2_fwd, w2b_pad, gk_b2_bwd, dgsm, dgk)
    dpre, dconv = zip(*[_qkv_bwd(p, convw, g, kind) for kind, g in enumerate((dqn, dkn, dvc))])

    pieces = (jnp.concatenate([a.astype(BF16) for a in (*dpre, dz, dqb, dkb, dvb, dgb, dga, dgB, dps)], axis=1),)
    dwperm = _inproj_dw(h, pieces)

    g_in = _unpermute_rows(dwperm).reshape(NSHARD, 2, SHW // 2, D)
    dconv_full = jnp.concatenate(dconv, axis=1)
    dw2f, dw2b = dw2f_pad[32:48], dw2b_pad[48:64]
    g_small = jnp.stack([_pack_shard_small(dconv_full[0:5, 768 * s:768 * (s + 1)],
                                           dw2f[:, 128 * s:128 * (s + 1)], dw2b[:, 128 * s:128 * (s + 1)])
                         for s in range(NSHARD)])
    g_small = g_small.reshape(NSHARD, 2, 16, 768)
    parts = [g_in, g_small]
    got = _swap_halves(parts)
    sums = [_sum_cast(a, b) for a, b in zip(parts, got)]
    hbs = [hb for _, hb in sums]
    started = _start_copies(_scatter_copies, 3 * len(hbs), hbs,
                            [lax.empty((3,) + a.shape[1:], a.dtype) for a in hbs], "scatter_start")
    dx, dlnpre8 = _inproj_dx(pieces, wperm, x2, ln_pre_w + started[4][0:1, 0:1], dyres)

    gsmall = _allreduce_small(jnp.concatenate(
        [dlnpre8, dalog8, ddt8, dgdn_w, db2f8, db2b8, dgla_w, dlnpost, loss8], axis=1))
    smalls = dict(ln_pre_w=(ln_pre_w, m_ln_pre_w, v_ln_pre_w), a_log_fwd=(a_log_fwd, m_a_log_fwd, v_a_log_fwd),
                  a_log_bwd=(a_log_bwd, m_a_log_bwd, v_a_log_bwd),
                  dt_bias_fwd=(dt_bias_fwd, m_dt_bias_fwd, v_dt_bias_fwd),
                  dt_bias_bwd=(dt_bias_bwd, m_dt_bias_bwd, v_dt_bias_bwd),
                  gdn_norm_w=(gdn_norm_w, m_gdn_norm_w, v_gdn_norm_w),
                  gk_b2_fwd=(gk_b2_fwd, m_gk_b2_fwd, v_gk_b2_fwd), gk_b2_bwd=(gk_b2_bwd, m_gk_b2_bwd, v_gk_b2_bwd),
                  gla_norm_w=(gla_norm_w, m_gla_norm_w, v_gla_norm_w), ln_post_w=(ln_post_w, m_ln_post_w, v_ln_post_w))
    names = [name for name, _, _, _ in SMALL_SLOTS]
    small = _adam_small(gsmall, *([smalls[n][i] for n in names] for i in range(3)))

    landed_proj = _wait_copies(_scatter_copies, started_proj, small["ln_pre_w"][1], "scatter_proj_wait")
    landed = _wait_copies(_scatter_copies, started, small["ln_pre_w"][1], "scatter_wait")
    sums = [sums[0], sum_proj, sums[1]]
    halves = [_sum4(f, g) for (f, _), g in zip(sums, [landed[0], landed_proj[0], landed[1]])]
    theirs = _join_halves(halves)

    a_in = [a.T for a in _adam(w_in[0].T, halves[0], theirs[0], m_w_in[0].T, v_w_in[0].T, 232, name="in")]
    a_pr = [_adam(w[0], halves[1], theirs[1], m[0], v[0], 128, tile0=2 * i, name=f"proj{i}")
            for i, (w, m, v) in enumerate(((w_proj_gdn, m_w_proj_gdn, v_w_proj_gdn),
                                           (w_proj_gla, m_w_proj_gla, v_w_proj_gla), (w_out, m_w_out, v_w_out)))]
    a_ss = _adam_shard_small(halves[2], theirs[2], (conv_w[0], gk_w2_fwd[0], gk_w2_bwd[0]),
                             (m_conv_w[0], m_gk_w2_fwd[0], m_gk_w2_bwd[0]),
                             (v_conv_w[0], v_gk_w2_fwd[0], v_gk_w2_bwd[0]))

    def family(k):
        conv, w2f_, w2b_ = a_ss[0][k], a_ss[1][k], a_ss[2][k]
        s = {n: small[n][k] for n in names}
        return [s["ln_pre_w"], a_in[k][None], conv[None], s["a_log_fwd"], s["a_log_bwd"], s["dt_bias_fwd"],
                s["dt_bias_bwd"], s["gdn_norm_w"], a_pr[0][k][None], w2f_[None], s["gk_b2_fwd"], w2b_[None],
                s["gk_b2_bwd"], s["gla_norm_w"], a_pr[1][k][None], a_pr[2][k][None], s["ln_post_w"]]

    return (gsmall[0, SMALL_W - 128], dx[None], *family(0), *family(1), *family(2), *family(3))
```

```python
import functools

import jax
import jax.numpy as jnp
from jax import lax
from jax.experimental import pallas as pl
from jax.experimental.pallas import tpu as pltpu

F32 = jnp.float32
BF16 = jnp.bfloat16
HI = lax.Precision.HIGHEST
MESH = pl.DeviceIdType.MESH

D = 1024
CH = 64
EPS = 1e-6
NA, DA = 8, 128
NB, DKB, DVB = 4, 128, 256
NSHARD = 4
SHW = 2320
NPERM = 9728
PS_BLOCK = 72
LR, B1, B2, ADAM_EPS, WD, STEP = 0.001, 0.9, 0.999, 1e-08, 0.01, 10

ANY = pl.BlockSpec(memory_space=pl.ANY)


def _cparams(sem=None, vmem_mb=48):
    return pltpu.CompilerParams(dimension_semantics=sem, vmem_limit_bytes=vmem_mb << 20)


def _bdot(a, b, ca, cb):
    return lax.dot_general(a.astype(BF16), b.astype(BF16), (((ca,), (cb,)), ((), ())),
                           preferred_element_type=F32)


@jax.custom_vjp
def mm(a, b):
    return _bdot(a, b, 1, 0)


def _mm_fwd(a, b):
    return _bdot(a, b, 1, 0), (a, b)


def _mm_bwd(res, g):
    a, b = res
    return _bdot(g, b, 1, 1), _bdot(a, g, 0, 0)


mm.defvjp(_mm_fwd, _mm_bwd)


@jax.custom_vjp
def mm_nt(a, b):
    return _bdot(a, b, 1, 1)


def _mm_nt_fwd(a, b):
    return _bdot(a, b, 1, 1), (a, b)


def _mm_nt_bwd(res, g):
    a, b = res
    return _bdot(g, b, 1, 0), _bdot(g, a, 0, 0)


mm_nt.defvjp(_mm_nt_fwd, _mm_nt_bwd)


@jax.custom_vjp
def mm_tn(a, b):
    return _bdot(a, b, 0, 0)


def _mm_tn_fwd(a, b):
    return _bdot(a, b, 0, 0), (a, b)


def _mm_tn_bwd(res, g):
    a, b = res
    return _bdot(b, g, 1, 1), _bdot(a, g, 1, 0)


mm_tn.defvjp(_mm_tn_fwd, _mm_tn_bwd)


def dot_hi(a, b):
    return lax.dot_general(a, b, (((1,), (0,)), ((), ())), precision=HI, preferred_element_type=F32)


def _split3(x):
    x1 = x.astype(BF16)
    r = x - x1.astype(F32)
    x2 = r.astype(BF16)
    return x1, x2, (r - x2.astype(F32)).astype(BF16)


def _cdot(c, x, cc, cx, c_first=True):
    parts = _split3(x)
    if c_first:
        return _bdot(c, parts[0], cc, cx) + _bdot(c, parts[1], cc, cx) + _bdot(c, parts[2], cc, cx)
    return _bdot(parts[0], c, cx, cc) + _bdot(parts[1], c, cx, cc) + _bdot(parts[2], c, cx, cc)


@jax.custom_vjp
def cmm(c, x):
    return _cdot(c, x, 1, 0)


def _cmm_fwd(c, x):
    return _cdot(c, x, 1, 0), c


def _cmm_bwd(c, g):
    return jnp.zeros_like(c), _cdot(c, g, 0, 0)


cmm.defvjp(_cmm_fwd, _cmm_bwd)


@jax.custom_vjp
def mmc(x, c):
    return _cdot(c, x, 0, 1, c_first=False)


def _mmc_fwd(x, c):
    return _cdot(c, x, 0, 1, c_first=False), c


def _mmc_bwd(c, g):
    return _cdot(c, g, 1, 1, c_first=False), jnp.zeros_like(c)


mmc.defvjp(_mmc_fwd, _mmc_bwd)


def _sigmoid(x):
    return 1.0 / (1.0 + jnp.exp(-x))


def _silu(x):
    return x * _sigmoid(x)


def _softplus(x):
    return jnp.maximum(x, 0.0) + jnp.log(1.0 + jnp.exp(-jnp.abs(x)))


def _rms(x, w):
    return x * lax.rsqrt(jnp.mean(x * x, axis=-1, keepdims=True) + EPS) * w


SC = 256


class _Consts:
    def __init__(self, rev):
        r = lax.broadcasted_iota(jnp.int32, (SC, SC), 0)
        c = lax.broadcasted_iota(jnp.int32, (SC, SC), 1)
        same = (r >> 6) == (c >> 6)
        a = jnp.where(rev, c, r)
        b = jnp.where(rev, r, c)
        self.incl = same & (a >= b)
        self.strict = same & (a > b)
        self.incl_f = self.incl.astype(F32)
        self.eye = (r == c).astype(F32)
        rows = lax.broadcasted_iota(jnp.int32, (SC, 1), 0)
        self.last_col = ((rows & (CH - 1)) == jnp.where(rev, 0, CH - 1)).astype(F32)
        rr = lax.broadcasted_iota(jnp.int32, (SC, CH), 0)
        cc = lax.broadcasted_iota(jnp.int32, (SC, CH), 1)
        self.fold = ((rr & (CH - 1)) == cc).astype(F32)


def _dot3(a, b, ca=1, cb=0):
    ah, bh = a.astype(BF16), b.astype(BF16)
    al, bl = (a - ah.astype(F32)).astype(BF16), (b - bh.astype(F32)).astype(BF16)
    return _bdot(ah, bh, ca, cb) + (_bdot(ah, bl, ca, cb) + _bdot(al, bh, ca, cb))


TRI_SPLIT_LEVELS = 2


def _tri_inv(low, eye):
    n = -low
    acc = eye + n
    p = n
    for level in range(5):
        dot = _dot3 if level < TRI_SPLIT_LEVELS else (lambda a, b: _bdot(a, b, 1, 0))
        p = dot(p, p)
        acc = acc + dot(acc, p)
    return acc


@jax.custom_vjp
def _solve2(low, rv, rk, tinv):
    x = _dot3(tinv, jnp.concatenate([rv, rk], axis=1))
    return x[:, :DA], x[:, DA:]


def _solve2_fwd(low, rv, rk, tinv):
    x = _dot3(tinv, jnp.concatenate([rv, rk], axis=1))
    return (x[:, :DA], x[:, DA:]), (x, tinv)


def _solve2_bwd(res, g):
    x, tinv = res
    drhs = _dot3(tinv, jnp.concatenate(g, axis=1), 0, 0)
    return -_dot3(drhs, x, 1, 1), drhs[:, :DA], drhs[:, DA:], jnp.zeros_like(tinv)


_solve2.defvjp(_solve2_fwd, _solve2_bwd)


def _chunk_last(x, cs):
    xs = (x * cs.last_col).reshape(SC // CH, CH, x.shape[1])
    return jnp.broadcast_to(jnp.sum(xs, axis=1, keepdims=True), xs.shape).reshape(x.shape)


def _gdn_decay(g, cs):
    gw = jnp.concatenate([g] * (SC // DA), axis=1)
    grow = jnp.sum(cs.eye * gw, axis=0, keepdims=True)
    return jnp.where(cs.incl, jnp.exp(jnp.where(cs.incl, gw - grow, 0.0)), 0.0)


def _gdn_intra(q, k, v, g, bx, tinv, cs):
    decay = _gdn_decay(g, cs)
    kb = k * bx
    low = jnp.where(cs.strict, mm_nt(kb, k) * decay, 0.0)
    eg = jnp.exp(g)
    made = tinv is None
    if made:
        tinv = _tri_inv(low, cs.eye)
    u, w = _solve2(low, v * bx, kb * eg, tinv)
    attn = mmc(mm_nt(q, k) * decay, cs.fold)
    qd = q * eg
    glast = _chunk_last(g, cs)
    kd = k * jnp.exp(glast - g)
    outs = (u, w, attn, qd, kd, jnp.exp(glast))
    return outs + (tinv,) if made else outs


def _gdn_scan(u, w, attn, qd, kd, el, s):
    vn = u - mm(w, s)
    o = mm(qd, s) + mm(attn, vn)
    sn = s * el + mm_tn(kd, vn)
    return o, sn


def _gla_intra(q, k, v, gk, cs):
    gc = cmm(cs.incl_f, gk)
    qg = q * (DKB ** -0.5) * jnp.exp(gc)
    kg = k * jnp.exp(-gc)
    attn = jnp.where(cs.incl, mm_nt(qg, kg), 0.0)
    intra = mm(attn, v)
    glast = _chunk_last(gc, cs)
    kd = k * jnp.exp(glast - gc)
    return qg, kd, intra, jnp.exp(glast)


def _gla_scan(qg, kd, v, el, st):
    o = mm_nt(qg, st)
    stn = st * el + mm_tn(v, kd)
    return o, stn


def _shift_rows(x, s):
    if s == 0:
        return x
    t = x.shape[0]
    rolled = pltpu.roll(x, (-s) % t, 0)
    rows = lax.broadcasted_iota(jnp.int32, x.shape, 0)
    return jnp.where((rows + s >= 0) & (rows + s < t), rolled, 0.0)


@jax.custom_vjp
def _conv5(x, w):
    acc = w[0:1] * _shift_rows(x, -2)
    for j in range(1, 5):
        acc = acc + w[j:j + 1] * _shift_rows(x, j - 2)
    return acc


def _conv5_fwd(x, w):
    return _conv5(x, w), (x, w)


def _conv5_bwd(res, g):
    x, w = res
    dx = w[0:1] * _shift_rows(g, 2)
    for j in range(1, 5):
        dx = dx + w[j:j + 1] * _shift_rows(g, 2 - j)
    rows = lax.broadcasted_iota(jnp.int32, w.shape, 0)
    dw = jnp.zeros_like(w)
    for j in range(5):
        dwj = jnp.sum(g * _shift_rows(x, j - 2), axis=0, keepdims=True)
        dw = dw + jnp.where(rows == j, dwj, 0.0)
    return dx, dw


_conv5.defvjp(_conv5_fwd, _conv5_bwd)


def _qkv_act(kind):
    def f(x, w):
        c = _silu(_conv5(x, w))
        if kind == 2:
            return c
        c = c * lax.rsqrt(jnp.sum(c * c, axis=-1, keepdims=True) + EPS)
        return c * (DA ** -0.5) if kind == 0 else c
    return f


def _inproj(x, lnw, wperm, tn=512):
    t = x.shape[0]
    tm = min(t, 2048)

    def body(x_ref, lnw_ref, w_ref, p_ref, h_ref, hbuf):
        @pl.when(pl.program_id(1) == 0)
        def _():
            hb = _rms(x_ref[...], lnw_ref[...]).astype(BF16)
            hbuf[...] = hb
            h_ref[...] = hb
        p_ref[...] = _bdot(hbuf[...], w_ref[...], 1, 1)

    return pl.pallas_call(
        body, name="inproj", grid=(t // tm, NPERM // tn),
        in_specs=[pl.BlockSpec((tm, D), lambda i, j: (i, 0)),
                  pl.BlockSpec((1, D), lambda i, j: (0, 0)),
                  pl.BlockSpec((tn, D), lambda i, j: (j, 0))],
        out_specs=[pl.BlockSpec((tm, tn), lambda i, j: (i, j)),
                   pl.BlockSpec((tm, D), lambda i, j: (i, 0))],
        out_shape=[jax.ShapeDtypeStruct((t, NPERM), F32),
                   jax.ShapeDtypeStruct((t, D), BF16)],
        scratch_shapes=[pltpu.VMEM((tm, D), BF16)],
        compiler_params=_cparams(("parallel", "arbitrary")),
    )(x, lnw, wperm)


DP_TILE = 512
DP_PIECES = ((0, 19),)


def _piece_specs(tm, j_first):
    specs = []
    for j0, n in DP_PIECES:
        def imap(a, b, j0=j0, n=n):
            j, i = (a, b) if j_first else (b, a)
            inside = (j >= j0) & (j < j0 + n)
            return jnp.where(inside, i, 0), jnp.clip(j - j0, 0, n - 1)
        specs.append(pl.BlockSpec((tm, DP_TILE), imap))
    return specs


def _for_piece(j, refs, fn):
    for (j0, n), ref in zip(DP_PIECES, refs):
        @pl.when((j >= j0) & (j < j0 + n))
        def _(ref=ref):
            fn(ref[...])


def _inproj_dw(h, pieces):
    t = h.shape[0]
    tm = min(t, 2048)
    npc = len(pieces)

    def body(h_ref, *refs):
        dw_ref = refs[npc]

        @pl.when(pl.program_id(1) == 0)
        def _():
            dw_ref[...] = jnp.zeros_like(dw_ref)

        def add(dp):
            dw_ref[...] += _bdot(dp, h_ref[...], 0, 0)
        _for_piece(pl.program_id(0), refs[:npc], add)

    return pl.pallas_call(
        body, name="inproj_dw", grid=(NPERM // DP_TILE, t // tm),
        in_specs=[pl.BlockSpec((tm, D), lambda j, i: (i, 0))] + _piece_specs(tm, True),
        out_specs=pl.BlockSpec((DP_TILE, D), lambda j, i: (j, 0)),
        out_shape=jax.ShapeDtypeStruct((NPERM, D), F32),
        compiler_params=_cparams(("parallel", "arbitrary")),
    )(h, *pieces)


def _inproj_dx(pieces, wperm, x, lnw, dyres):
    t = x.shape[0]
    tm = min(t, 1024)
    tn = DP_TILE
    nj = NPERM // tn
    npc = len(pieces)

    def body(*refs):
        w_ref, x_ref, lnw_ref, dy_ref, dx_ref, dlnw_ref, acc = refs[npc:]
        j = pl.program_id(1)

        @pl.when(j == 0)
        def _():
            acc[...] = jnp.zeros_like(acc)

        def add(dp):
            acc[...] += _bdot(dp, w_ref[...], 1, 0)
        _for_piece(j, refs[:npc], add)

        @pl.when(j == nj - 1)
        def _():
            _, vjp = jax.vjp(_rms, x_ref[...], lnw_ref[...])
            dx, dlnw = vjp(acc[...])
            dx_ref[...] = dx + dy_ref[...]

            @pl.when(pl.program_id(0) == 0)
            def _():
                dlnw_ref[...] = jnp.zeros_like(dlnw_ref)
            dlnw_ref[...] += jnp.broadcast_to(dlnw, dlnw_ref.shape)

    return pl.pallas_call(
        body, name="inproj_dx", grid=(t // tm, nj),
        in_specs=_piece_specs(tm, False) + [
                  pl.BlockSpec((tn, D), lambda i, j: (j, 0)),
                  pl.BlockSpec((tm, D), lambda i, j: (i, 0)),
                  pl.BlockSpec((1, D), lambda i, j: (0, 0)),
                  pl.BlockSpec((tm, D), lambda i, j: (i, 0))],
        out_specs=[pl.BlockSpec((tm, D), lambda i, j: (i, 0)),
                   pl.BlockSpec((8, D), lambda i, j: (0, 0))],
        out_shape=[jax.ShapeDtypeStruct((t, D), F32), jax.ShapeDtypeStruct((8, D), F32)],
        scratch_shapes=[pltpu.VMEM((tm, D), F32)],
        compiler_params=_cparams(("arbitrary", "arbitrary")),
    )(*pieces, wperm, x, lnw, dyres)


def _qkv_fwd(p, convw, kind):
    t = p.shape[0]
    f = _qkv_act(kind)

    def body(p_ref, w_ref, o_ref):
        o_ref[...] = f(p_ref[...], w_ref[...])

    return pl.pallas_call(
        body, name=f"qkv_fwd{kind}", grid=(NA,),
        in_specs=[pl.BlockSpec((t, DA), lambda h: (0, kind * NA + h)),
                  pl.BlockSpec((8, DA), lambda h: (0, kind * NA + h))],
        out_specs=pl.BlockSpec((t, DA), lambda h: (0, h)),
        out_shape=jax.ShapeDtypeStruct((t, D), F32),
        compiler_params=_cparams(("parallel",)),
    )(p, convw)


def _qkv_bwd(p, convw, dout, kind):
    t = p.shape[0]
    f = _qkv_act(kind)

    def body(p_ref, w_ref, g_ref, dx_ref, dw_ref):
        _, vjp = jax.vjp(f, p_ref[...], w_ref[...])
        dx, dw = vjp(g_ref[...])
        dx_ref[...] = dx.astype(BF16)
        dw_ref[...] = dw

    return pl.pallas_call(
        body, name=f"qkv_bwd{kind}", grid=(NA,),
        in_specs=[pl.BlockSpec((t, DA), lambda h: (0, kind * NA + h)),
                  pl.BlockSpec((8, DA), lambda h: (0, kind * NA + h)),
                  pl.BlockSpec((t, DA), lambda h: (0, h))],
        out_specs=[pl.BlockSpec((t, DA), lambda h: (0, h)),
                   pl.BlockSpec((8, DA), lambda h: (0, h))],
        out_shape=[jax.ShapeDtypeStruct((t, D), BF16), jax.ShapeDtypeStruct((8, D), F32)],
        compiler_params=_cparams(("parallel",)),
    )(p, convw, dout)


def _gates_f(ps, alog_row, dt_row, w2f, b2f, w2b, b2b):
    lane = lax.broadcasted_iota(jnp.int32, ps.shape, 1)
    lg = -jnp.exp(alog_row) * _softplus(ps + dt_row)
    gsm = jnp.where(lane < 16, lg, jnp.where(lane < 32, _sigmoid(ps), 0.0))
    gkf = -_softplus(-(mm(ps, w2f) + b2f)) * (1.0 / 16.0)
    gkb = -_softplus(-(mm(ps, w2b) + b2b)) * (1.0 / 16.0)
    return gsm, gkf, gkb


def _gates_fwd(ps, alog_row, dt_row, w2f, b2f, w2b, b2b, tm=512):
    t = ps.shape[0]

    def body(ps_ref, a_ref, d_ref, wf_ref, bf_ref, wb_ref, bb_ref, gsm_ref, gk_ref):
        gsm, gkf, gkb = _gates_f(ps_ref[...], a_ref[...], d_ref[...], wf_ref[...], bf_ref[...],
                                 wb_ref[...], bb_ref[...])
        gsm_ref[...] = gsm
        gk_ref[0] = gkf
        gk_ref[1] = gkb

    row = lambda n: pl.BlockSpec((1, n), lambda i: (0, 0))
    mat = pl.BlockSpec((128, 512), lambda i: (0, 0))
    return pl.pallas_call(
        body, name="gates_fwd", grid=(t // tm,),
        in_specs=[pl.BlockSpec((tm, 128), lambda i: (i, PS_BLOCK)), row(128), row(128), mat, row(512), mat, row(512)],
        out_specs=[pl.BlockSpec((tm, 128), lambda i: (i, 0)),
                   pl.BlockSpec((2, tm, 512), lambda i: (0, i, 0))],
        out_shape=[jax.ShapeDtypeStruct((t, 128), F32), jax.ShapeDtypeStruct((2, t, 512), F32)],
        compiler_params=_cparams(("parallel",)),
    )(ps, alog_row, dt_row, w2f, b2f, w2b, b2b)


def _gates_bwd(ps, alog_row, dt_row, w2f, b2f, w2b, b2b, dgsm, dgk, tm=512):
    t = ps.shape[0]

    def body(ps_ref, a_ref, d_ref, wf_ref, bf_ref, wb_ref, bb_ref, dgsm_ref, dgk_ref,
             dps_ref, da_ref, dd_ref, dwf_ref, dbf_ref, dwb_ref, dbb_ref):
        _, vjp = jax.vjp(_gates_f, ps_ref[...], a_ref[...], d_ref[...], wf_ref[...], bf_ref[...],
                         wb_ref[...], bb_ref[...])
        dps, da, dd, dwf, dbf, dwb, dbb = vjp((dgsm_ref[...], dgk_ref[0], dgk_ref[1]))
        dps_ref[:, 0:128] = dps.astype(BF16)
        dps_ref[:, 128:DP_TILE] = jnp.zeros((tm, DP_TILE - 128), BF16)
        accs = ((da_ref, da), (dd_ref, dd), (dwf_ref, dwf), (dbf_ref, dbf), (dwb_ref, dwb), (dbb_ref, dbb))

        @pl.when(pl.program_id(0) == 0)
        def _():
            for ref, _ in accs:
                ref[...] = jnp.zeros_like(ref)
        for ref, val in accs:
            ref[...] += jnp.broadcast_to(val, ref.shape)

    row = lambda n: pl.BlockSpec((1, n), lambda i: (0, 0))
    row8 = lambda n: pl.BlockSpec((8, n), lambda i: (0, 0))
    mat = pl.BlockSpec((128, 512), lambda i: (0, 0))
    return pl.pallas_call(
        body, name="gates_bwd", grid=(t // tm,),
        in_specs=[pl.BlockSpec((tm, 128), lambda i: (i, PS_BLOCK)), row(128), row(128), mat, row(512), mat, row(512),
                  pl.BlockSpec((tm, 128), lambda i: (i, 0)),
                  pl.BlockSpec((2, tm, 512), lambda i: (0, i, 0))],
        out_specs=[pl.BlockSpec((tm, DP_TILE), lambda i: (i, 0)), row8(128), row8(128), mat, row8(512), mat,
                   row8(512)],
        out_shape=[jax.ShapeDtypeStruct((t, DP_TILE), BF16),
                   jax.ShapeDtypeStruct((8, 128), F32), jax.ShapeDtypeStruct((8, 128), F32),
                   jax.ShapeDtypeStruct((128, 512), F32), jax.ShapeDtypeStruct((8, 512), F32),
                   jax.ShapeDtypeStruct((128, 512), F32), jax.ShapeDtypeStruct((8, 512), F32)],
        compiler_params=_cparams(("arbitrary",)),
    )(ps, alog_row, dt_row, w2f, b2f, w2b, b2b, dgsm, dgk)


def _rows(i):
    return pl.ds(pl.multiple_of(i * CH, CH), CH)


def _srows(i):
    return pl.ds(pl.multiple_of(i * SC, SC), SC)


def _first_row(x):
    row = lax.broadcasted_iota(jnp.int32, (8, x.shape[1]), 0)
    return jnp.where(row == 0, jnp.broadcast_to(x, (8, x.shape[1])), 0.0)


def _chunk_rows(e_ref, i):
    pad = jnp.zeros((CH - 8, 128), F32)
    return jnp.concatenate([x for c in range(SC // CH) for x in (e_ref[(SC // CH) * i + c], pad)], axis=0)


def _gcum_f(gsm, tm):
    i = lax.broadcasted_iota(jnp.int32, (tm, tm), 0)
    j = lax.broadcasted_iota(jnp.int32, (tm, tm), 1)
    same = (i >> 6) == (j >> 6)
    lower = (same & (i >= j)).astype(F32)
    upper = (same & (i <= j)).astype(F32)
    r = lax.broadcasted_iota(jnp.int32, (128, D), 0)
    head = lax.broadcasted_iota(jnp.int32, (128, D), 1) >> 7
    pick = lambda off: (r == head + off).astype(F32)
    lane = lax.broadcasted_iota(jnp.int32, gsm.shape, 1)
    run = jnp.where(lane < 8, cmm(lower, gsm), cmm(upper, gsm))
    return mmc(run, pick(0)), mmc(run, pick(8)), mmc(gsm, pick(16)), mmc(gsm, pick(24))


def _gcum_fwd(gsm, tm=256):
    t = gsm.shape[0]

    def body(s_ref, g_ref, b_ref):
        gf, gb, bf, bb = _gcum_f(s_ref[...], tm)
        g_ref[0] = gf
        g_ref[1] = gb
        b_ref[0] = bf
        b_ref[1] = bb

    two = pl.BlockSpec((2, tm, D), lambda i: (0, i, 0))
    return pl.pallas_call(
        body, name="gcum_fwd", grid=(t // tm,),
        in_specs=[pl.BlockSpec((tm, 128), lambda i: (i, 0))], out_specs=[two, two],
        out_shape=[jax.ShapeDtypeStruct((2, t, D), F32)] * 2,
        compiler_params=_cparams(("parallel",)),
    )(gsm)


def _gcum_bwd(gsm, dg2, db2, tm=256):
    t = gsm.shape[0]

    def body(s_ref, dg_ref, db_ref, ds_ref):
        _, vjp = jax.vjp(lambda s: _gcum_f(s, tm), s_ref[...])
        ds_ref[...] = vjp((dg_ref[0], dg_ref[1], db_ref[0], db_ref[1]))[0]

    two = pl.BlockSpec((2, tm, D), lambda i: (0, i, 0))
    tile = pl.BlockSpec((tm, 128), lambda i: (i, 0))
    return pl.pallas_call(
        body, name="gcum_bwd", grid=(t // tm,),
        in_specs=[tile, two, two], out_specs=tile,
        out_shape=jax.ShapeDtypeStruct((t, 128), F32),
        compiler_params=_cparams(("parallel",)),
    )(gsm, dg2, db2)


def _gdn_intra_fwd(qn, kn, vc, g2, b2):
    t = qn.shape[0]
    n = t // CH

    def body(q_ref, k_ref, v_ref, g_ref, b_ref, u_ref, w_ref, a_ref, qd_ref, kd_ref, e_ref, t_ref):
        cs = _Consts(pl.program_id(0) == 1)

        def step(i, carry):
            r = _srows(i)
            q, k, v, g, bx = q_ref[r, :], k_ref[r, :], v_ref[r, :], g_ref[r, :], b_ref[r, :]
            u, w, a, qd, kd, el, tinv = _gdn_intra(q, k, v, g, bx, None, cs)
            u_ref[r, :] = u
            w_ref[r, :] = w.astype(BF16)
            a_ref[r, :] = a.astype(BF16)
            qd_ref[r, :] = qd.astype(BF16)
            kd_ref[r, :] = kd.astype(BF16)
            t_ref[r, :] = tinv
            for c in range(SC // CH):
                e_ref[(SC // CH) * i + c] = el[c * CH:c * CH + 8]
            return carry

        lax.fori_loop(0, t // SC, step, 0)

    head = pl.BlockSpec((t, DA), lambda d, h: (0, h))
    dh = pl.BlockSpec((None, t, DA), lambda d, h: (d, 0, h))
    sq = lambda w: pl.BlockSpec((None, None, t, w), lambda d, h: (d, h, 0, 0))
    big = jax.ShapeDtypeStruct((2, t, D), F32)
    half = jax.ShapeDtypeStruct((2, t, D), BF16)
    return pl.pallas_call(
        body, name="gdn_intra_fwd", grid=(2, NA),
        in_specs=[head, head, head, dh, dh],
        out_specs=[dh, dh, sq(CH), dh, dh, pl.BlockSpec((None, None, n, 8, 128), lambda d, h: (d, h, 0, 0, 0)),
                   sq(SC)],
        out_shape=[big, half, jax.ShapeDtypeStruct((2, NA, t, CH), BF16), half, half,
                   jax.ShapeDtypeStruct((2, NA, n, 8, 128), F32), jax.ShapeDtypeStruct((2, NA, t, SC), F32)],
        compiler_params=_cparams(("parallel", "parallel")),
    )(qn, kn, vc, g2, b2)


SCAN_TB = 256
SCAN_HB = 8


def _scan_specs(t, width, nheads, hb, along):
    nt = t // SCAN_TB
    nb = SCAN_TB // CH

    def tmap(d, tt):
        fwd = tt + d * (nt - 1 - 2 * tt)
        return fwd if along > 0 else nt - 1 - fwd

    tok = pl.BlockSpec((None, SCAN_TB, hb * width), lambda d, h, tt: (d, tmap(d, tt), h))
    per = lambda *tail: pl.BlockSpec((None, hb, nb) + tail, lambda d, h, tt: (d, h, tmap(d, tt)) + (0,) * len(tail))
    sq = pl.BlockSpec((None, hb, SCAN_TB, CH), lambda d, h, tt: (d, h, tmap(d, tt), 0))
    shared = lambda w: pl.BlockSpec((SCAN_TB, hb * w), lambda d, h, tt: (tmap(d, tt), h))
    return tok, per, sq, shared, (2, nheads // hb, nt), nb


def _gdn_scan_fwd(u, w, a, qd, kd, e):
    t = u.shape[1]
    tok, per, sq, _, grid, nb = _scan_specs(t, DA, NA, SCAN_HB, +1)

    def body(u_ref, w_ref, a_ref, qd_ref, kd_ref, e_ref, o_ref, s_ref, state):
        rev = pl.program_id(0) == 1

        @pl.when(pl.program_id(2) == 0)
        def _():
            state[...] = jnp.zeros_like(state)

        def step(i, ss):
            ci = jnp.where(rev, nb - 1 - i, i)
            r = _rows(ci)
            out = []
            for hh, s in enumerate(ss):
                c = slice(hh * DA, (hh + 1) * DA)
                s_ref[hh, ci] = s
                o, sn = _gdn_scan(u_ref[r, c], w_ref[r, c], a_ref[hh, r, :], qd_ref[r, c], kd_ref[r, c],
                                  e_ref[hh, ci][0:1], s)
                o_ref[r, c] = o
                out.append(sn)
            return tuple(out)

        ss = lax.fori_loop(0, nb, step, tuple(state[hh] for hh in range(SCAN_HB)))
        for hh, s in enumerate(ss):
            state[hh] = s

    return pl.pallas_call(
        body, name="gdn_scan_fwd", grid=grid,
        in_specs=[tok, tok, sq, tok, tok, per(8, 128)],
        out_specs=[tok, per(DA, DA)],
        out_shape=[jax.ShapeDtypeStruct((2, t, D), F32), jax.ShapeDtypeStruct((2, NA, t // CH, DA, DA), F32)],
        scratch_shapes=[pltpu.VMEM((SCAN_HB, DA, DA), F32)],
        compiler_params=_cparams(("parallel", "parallel", "arbitrary")),
    )(u, w, a, qd, kd, e)


def _gdn_scan_bwd(u, w, a, qd, kd, e, ssave, do):
    t = u.shape[1]
    tok, per, sq, shared, grid, nb = _scan_specs(t, DA, NA, SCAN_HB, -1)

    def body(u_ref, w_ref, a_ref, qd_ref, kd_ref, e_ref, s_ref, do_ref,
             du_ref, dw_ref, da_ref, dqd_ref, dkd_ref, de_ref, state):
        rev = pl.program_id(0) == 1

        @pl.when(pl.program_id(2) == 0)
        def _():
            state[...] = jnp.zeros_like(state)

        def step(i, dss):
            ci = jnp.where(rev, i, nb - 1 - i)
            r = _rows(ci)
            out = []
            for hh, ds in enumerate(dss):
                c = slice(hh * DA, (hh + 1) * DA)
                _, vjp = jax.vjp(_gdn_scan, u_ref[r, c], w_ref[r, c].astype(F32), a_ref[hh, r, :].astype(F32),
                                 qd_ref[r, c].astype(F32), kd_ref[r, c].astype(F32),
                                 e_ref[hh, ci][0:1], s_ref[hh, ci])
                du, dw, da, dqd, dkd, de, dsn = vjp((do_ref[r, c], ds))
                du_ref[r, c] = du
                dw_ref[r, c] = dw
                da_ref[hh, r, :] = da
                dqd_ref[r, c] = dqd
                dkd_ref[r, c] = dkd
                de_ref[hh, ci] = _first_row(de)
                out.append(dsn)
            return tuple(out)

        dss = lax.fori_loop(0, nb, step, tuple(state[hh] for hh in range(SCAN_HB)))
        for hh, ds in enumerate(dss):
            state[hh] = ds

    big = jax.ShapeDtypeStruct((2, t, D), F32)
    return pl.pallas_call(
        body, name="gdn_scan_bwd", grid=grid,
        in_specs=[tok, tok, sq, tok, tok, per(8, 128), per(DA, DA), shared(DA)],
        out_specs=[tok, tok, sq, tok, tok, per(8, 128)],
        out_shape=[big, big, jax.ShapeDtypeStruct((2, NA, t, CH), F32), big, big,
                   jax.ShapeDtypeStruct((2, NA, t // CH, 8, 128), F32)],
        scratch_shapes=[pltpu.VMEM((SCAN_HB, DA, DA), F32)],
        compiler_params=_cparams(("parallel", "parallel", "arbitrary")),
    )(u, w, a, qd, kd, e, ssave, do)


def _gdn_intra_bwd(qn, kn, vc, g2, b2, tinv, du, dw, da, dqd, dkd, de):
    t = qn.shape[0]
    n = t // CH

    def body(q_ref, k_ref, v_ref, g_ref, b_ref, t_ref, du_ref, dw_ref, da_ref, dqd_ref, dkd_ref, de_ref,
             dq_ref, dk_ref, dv_ref, dg_ref, db_ref):
        d = pl.program_id(1)
        cs = _Consts(d == 1)

        @pl.when(d == 0)
        def _():
            dq_ref[...] = jnp.zeros_like(dq_ref)
            dk_ref[...] = jnp.zeros_like(dk_ref)
            dv_ref[...] = jnp.zeros_like(dv_ref)

        def step(i, carry):
            r = _srows(i)
            tinv_c = t_ref[r, :]
            f = lambda q, k, v, g, bx: _gdn_intra(q, k, v, g, bx, tinv_c, cs)
            _, vjp = jax.vjp(f, q_ref[r, :], k_ref[r, :], v_ref[r, :], g_ref[r, :], b_ref[r, :])
            dq, dk, dv, dg, dbx = vjp((du_ref[r, :], dw_ref[r, :], da_ref[r, :], dqd_ref[r, :],
                                       dkd_ref[r, :], _chunk_rows(de_ref, i)))
            dq_ref[r, :] += dq
            dk_ref[r, :] += dk
            dv_ref[r, :] += dv
            dg_ref[r, :] = dg
            db_ref[r, :] = dbx
            return carry

        lax.fori_loop(0, t // SC, step, 0)

    head = pl.BlockSpec((t, DA), lambda h, d: (0, h))
    dh = pl.BlockSpec((None, t, DA), lambda h, d: (d, 0, h))
    sq = pl.BlockSpec((None, None, t, CH), lambda h, d: (d, h, 0, 0))
    tq = pl.BlockSpec((None, None, t, SC), lambda h, d: (d, h, 0, 0))
    full = jax.ShapeDtypeStruct((t, D), F32)
    big = jax.ShapeDtypeStruct((2, t, D), F32)
    return pl.pallas_call(
        body, name="gdn_intra_bwd", grid=(NA, 2),
        in_specs=[head, head, head, dh, dh, tq, dh, dh, sq, dh, dh,
                  pl.BlockSpec((None, None, n, 8, 128), lambda h, d: (d, h, 0, 0, 0))],
        out_specs=[head, head, head, dh, dh],
        out_shape=[full, full, full, big, big],
        compiler_params=_cparams(("arbitrary", "arbitrary")),
    )(qn, kn, vc, g2, b2, tinv, du, dw, da, dqd, dkd, de)


def _gla_specs(t, order):
    ix = (lambda d, h: (d, h)) if order == "dh" else (lambda h, d: (d, h))

    def mk(fn):
        return lambda a, b: fn(*ix(a, b))
    q = pl.BlockSpec((t, DKB), mk(lambda d, h: (0, 32 + h)))
    k = pl.BlockSpec((t, DKB), mk(lambda d, h: (0, 36 + h)))
    v = pl.BlockSpec((t, DVB), mk(lambda d, h: (0, 20 + h)))
    dk = pl.BlockSpec((None, t, DKB), mk(lambda d, h: (d, 0, h)))
    dv = pl.BlockSpec((None, t, DVB), mk(lambda d, h: (d, 0, h)))
    e = pl.BlockSpec((None, None, t // CH, 8, 128), mk(lambda d, h: (d, h, 0, 0, 0)))
    s = pl.BlockSpec((None, None, t // CH, DVB, DKB), mk(lambda d, h: (d, h, 0, 0, 0)))
    return q, k, v, dk, dv, e, s


def _gla_intra_fwd(p, gk):
    t = p.shape[0]
    n = t // CH

    def body(q_ref, k_ref, v_ref, g_ref, qg_ref, kd_ref, in_ref, e_ref):
        cs = _Consts(pl.program_id(0) == 1)

        def step(i, carry):
            r = _srows(i)
            qg, kd, intra, el = _gla_intra(q_ref[r, :], k_ref[r, :], v_ref[r, :], g_ref[r, :], cs)
            qg_ref[r, :] = qg
            kd_ref[r, :] = kd
            in_ref[r, :] = intra
            for c in range(SC // CH):
                e_ref[(SC // CH) * i + c] = el[c * CH:c * CH + 8]
            return carry

        lax.fori_loop(0, t // SC, step, 0)

    q, k, v, dk, dv, e, _ = _gla_specs(t, "dh")
    return pl.pallas_call(
        body, name="gla_intra_fwd", grid=(2, NB),
        in_specs=[q, k, v, dk], out_specs=[dk, dk, dv, e],
        out_shape=[jax.ShapeDtypeStruct((2, t, NB * DKB), F32), jax.ShapeDtypeStruct((2, t, NB * DKB), F32),
                   jax.ShapeDtypeStruct((2, t, D), F32), jax.ShapeDtypeStruct((2, NB, n, 8, 128), F32)],
        compiler_params=_cparams(("parallel", "parallel")),
    )(p, p, p, gk)


GLA_HB = 4


def _gla_v_spec(t, along):
    nt = t // SCAN_TB

    def tmap(d, tt):
        fwd = tt + d * (nt - 1 - 2 * tt)
        return fwd if along > 0 else nt - 1 - fwd

    return pl.BlockSpec((SCAN_TB, GLA_HB * DVB), lambda d, h, tt: (tmap(d, tt), 5120 // (GLA_HB * DVB) + h))


def _gla_scan_fwd(p, qg, kd, intra, e):
    t = p.shape[0]
    tokk, per, _, _, grid, nb = _scan_specs(t, DKB, NB, GLA_HB, +1)
    tokv = _scan_specs(t, DVB, NB, GLA_HB, +1)[0]

    def body(v_ref, qg_ref, kd_ref, in_ref, e_ref, o_ref, s_ref, state):
        rev = pl.program_id(0) == 1

        @pl.when(pl.program_id(2) == 0)
        def _():
            state[...] = jnp.zeros_like(state)

        def step(i, sts):
            ci = jnp.where(rev, nb - 1 - i, i)
            r = _rows(ci)
            out = []
            for hh, st in enumerate(sts):
                ck = slice(hh * DKB, (hh + 1) * DKB)
                cv = slice(hh * DVB, (hh + 1) * DVB)
                s_ref[hh, ci] = st
                o, stn = _gla_scan(qg_ref[r, ck], kd_ref[r, ck], v_ref[r, cv], e_ref[hh, ci][0:1], st)
                o_ref[r, cv] = o + in_ref[r, cv]
                out.append(stn)
            return tuple(out)

        sts = lax.fori_loop(0, nb, step, tuple(state[hh] for hh in range(GLA_HB)))
        for hh, st in enumerate(sts):
            state[hh] = st

    return pl.pallas_call(
        body, name="gla_scan_fwd", grid=grid,
        in_specs=[_gla_v_spec(t, +1), tokk, tokk, tokv, per(8, 128)], out_specs=[tokv, per(DVB, DKB)],
        out_shape=[jax.ShapeDtypeStruct((2, t, D), F32), jax.ShapeDtypeStruct((2, NB, t // CH, DVB, DKB), F32)],
        scratch_shapes=[pltpu.VMEM((GLA_HB, DVB, DKB), F32)],
        compiler_params=_cparams(("parallel", "parallel", "arbitrary")),
    )(p, qg, kd, intra, e)


def _gla_scan_bwd(p, qg, kd, e, ssave, do):
    t = p.shape[0]
    tokk, per, _, shared, grid, nb = _scan_specs(t, DKB, NB, GLA_HB, -1)
    tokv = _scan_specs(t, DVB, NB, GLA_HB, -1)[0]

    def body(v_ref, qg_ref, kd_ref, e_ref, s_ref, do_ref, dqg_ref, dkd_ref, dv_ref, de_ref, state):
        rev = pl.program_id(0) == 1

        @pl.when(pl.program_id(2) == 0)
        def _():
            state[...] = jnp.zeros_like(state)

        def step(i, dsts):
            ci = jnp.where(rev, i, nb - 1 - i)
            r = _rows(ci)
            out = []
            for hh, dst in enumerate(dsts):
                ck = slice(hh * DKB, (hh + 1) * DKB)
                cv = slice(hh * DVB, (hh + 1) * DVB)
                _, vjp = jax.vjp(_gla_scan, qg_ref[r, ck], kd_ref[r, ck], v_ref[r, cv], e_ref[hh, ci][0:1],
                                 s_ref[hh, ci])
                dqg, dkd, dv, de, dstn = vjp((do_ref[r, cv], dst))
                dqg_ref[r, ck] = dqg
                dkd_ref[r, ck] = dkd
                dv_ref[r, cv] = dv
                de_ref[hh, ci] = _first_row(de)
                out.append(dstn)
            return tuple(out)

        dsts = lax.fori_loop(0, nb, step, tuple(state[hh] for hh in range(GLA_HB)))
        for hh, dst in enumerate(dsts):
            state[hh] = dst

    return pl.pallas_call(
        body, name="gla_scan_bwd", grid=grid,
        in_specs=[_gla_v_spec(t, -1), tokk, tokk, per(8, 128), per(DVB, DKB), shared(DVB)],
        out_specs=[tokk, tokk, tokv, per(8, 128)],
        out_shape=[jax.ShapeDtypeStruct((2, t, NB * DKB), F32), jax.ShapeDtypeStruct((2, t, NB * DKB), F32),
                   jax.ShapeDtypeStruct((2, t, D), F32), jax.ShapeDtypeStruct((2, NB, t // CH, 8, 128), F32)],
        scratch_shapes=[pltpu.VMEM((GLA_HB, DVB, DKB), F32)],
        compiler_params=_cparams(("parallel", "parallel", "arbitrary")),
    )(p, qg, kd, e, ssave, do)


def _gla_intra_bwd(p, gk, dqg, dkd, dvs, de, do):
    t = p.shape[0]
    n = t // CH

    def body(q_ref, k_ref, v_ref, g_ref, dqg_ref, dkd_ref, dvs_ref, de_ref, do_ref,
             dq_ref, dk_ref, dv_ref, dg_ref):
        d = pl.program_id(1)
        cs = _Consts(d == 1)

        @pl.when(d == 0)
        def _():
            dq_ref[...] = jnp.zeros_like(dq_ref)
            dk_ref[...] = jnp.zeros_like(dk_ref)
            dv_ref[...] = jnp.zeros_like(dv_ref)

        def step(i, carry):
            r = _srows(i)
            f = lambda q, k, v, g: _gla_intra(q, k, v, g, cs)
            _, vjp = jax.vjp(f, q_ref[r, :], k_ref[r, :], v_ref[r, :], g_ref[r, :])
            dq, dk, dv, dg = vjp((dqg_ref[r, :], dkd_ref[r, :], do_ref[r, :], _chunk_rows(de_ref, i)))
            dq_ref[r, :] += dq
            dk_ref[r, :] += dk
            dv_ref[r, :] += dv + dvs_ref[r, :]
            dg_ref[r, :] = dg
            return carry

        lax.fori_loop(0, t // SC, step, 0)

    q, k, v, dk, dv, e_s, _ = _gla_specs(t, "hd")
    hk = pl.BlockSpec((t, DKB), lambda h, d: (0, h))
    hv = pl.BlockSpec((t, DVB), lambda h, d: (0, h))
    return pl.pallas_call(
        body, name="gla_intra_bwd", grid=(NB, 2),
        in_specs=[q, k, v, dk, dk, dk, dv, e_s, hv],
        out_specs=[hk, hk, hv, dk],
        out_shape=[jax.ShapeDtypeStruct((t, NB * DKB), F32), jax.ShapeDtypeStruct((t, NB * DKB), F32),
                   jax.ShapeDtypeStruct((t, D), F32), jax.ShapeDtypeStruct((2, t, NB * DKB), F32)],
        compiler_params=_cparams(("arbitrary", "arbitrary")),
    )(p, p, p, gk, dqg, dkd, dvs, de, do)


def _seg_gate(o, z, w):
    return _rms(o, w) * _silu(z)


def _seg_merge(ya, yb, ga, gb):
    return _sigmoid(ga) * ya + _sigmoid(gb) * yb


def _seg_loss(out, x, tgt, w):
    err = x + _rms(out, w) - tgt
    return 0.5 * jnp.sum(jnp.mean(err * err, axis=-1, keepdims=True), axis=0, keepdims=True)


def _post(oa2, ob2, p, x, tgt, gdn_w, gla_w, lnpost, w3, tm=128):
    t = x.shape[0]

    def body(oa_ref, ob_ref, z_ref, gb_ref, ga_ref, gB_ref, x_ref, t_ref, aw_ref, bw_ref, lw_ref, w_ref,
             loss_ref, doa_ref, dob_ref, dz_ref, dgb_ref, dga_ref, dgB_ref, dy_ref,
             dw_ref, daw_ref, dbw_ref, dlw_ref):
        first = pl.program_id(0) == 0
        oa = oa_ref[0] + oa_ref[1]
        ob = ob_ref[0] + ob_ref[1]
        z, gb = z_ref[...], gb_ref[...]
        aw, bw = aw_ref[...], bw_ref[...]
        rs = D // NSHARD

        def mat(a, m):
            return sum(jnp.dot(a[:, s * rs:(s + 1) * rs], w_ref[s, m], preferred_element_type=F32)
                       for s in range(NSHARD))

        def mat_t(g, m):
            return jnp.concatenate([_bdot(g, w_ref[s, m], 1, 1) for s in range(NSHARD)], axis=1)

        def add_dw(a, g, m):
            for s in range(NSHARD):
                dw_ref[s, m] += _bdot(a[:, s * rs:(s + 1) * rs], g, 0, 0)

        pa = [jax.vjp(_seg_gate, oa[:, h * DA:(h + 1) * DA], z[:, h * DA:(h + 1) * DA], aw) for h in range(NA)]
        pb = [jax.vjp(_seg_gate, ob[:, h * DVB:(h + 1) * DVB], gb[:, h * DVB:(h + 1) * DVB], bw)
              for h in range(NB)]
        a1 = jnp.concatenate([v for v, _ in pa], axis=1).astype(BF16)
        a2 = jnp.concatenate([v for v, _ in pb], axis=1).astype(BF16)
        ya = mat(a1, 0)
        yb = mat(a2, 1)
        merged, vjp_m = jax.vjp(_seg_merge, ya, yb, ga_ref[...], gB_ref[...])
        mb = merged.astype(BF16)
        out = mat(mb, 2)
        loss, vjp_l = jax.vjp(_seg_loss, out, x_ref[...], t_ref[...], lw_ref[...])
        dout, dyres, _, dlw = vjp_l(jnp.ones((1, 1), F32))
        dy_ref[...] = dyres
        doutb = dout.astype(BF16)
        dmerged = mat_t(doutb, 2)
        dya, dyb, dga, dgB = vjp_m(dmerged)
        dga_ref[...] = dga.astype(BF16)
        dgB_ref[...] = dgB.astype(BF16)
        dyab, dybb = dya.astype(BF16), dyb.astype(BF16)
        da1 = mat_t(dyab, 0)
        da2 = mat_t(dybb, 1)

        daw = jnp.zeros_like(aw)
        for h in range(NA):
            sl = slice(h * DA, (h + 1) * DA)
            do, dz, dw = pa[h][1](da1[:, sl])
            doa_ref[:, sl] = do
            dz_ref[:, sl] = dz.astype(BF16)
            daw = daw + dw
        dbw = jnp.zeros_like(bw)
        for h in range(NB):
            sl = slice(h * DVB, (h + 1) * DVB)
            do, dg, dw = pb[h][1](da2[:, sl])
            dob_ref[:, sl] = do
            dgb_ref[:, sl] = dg.astype(BF16)
            dbw = dbw + dw

        @pl.when(first)
        def _():
            loss_ref[...] = jnp.zeros_like(loss_ref)
            dw_ref[...] = jnp.zeros_like(dw_ref)
            daw_ref[...] = jnp.zeros_like(daw_ref)
            dbw_ref[...] = jnp.zeros_like(dbw_ref)
            dlw_ref[...] = jnp.zeros_like(dlw_ref)

        loss_ref[...] += jnp.broadcast_to(loss, loss_ref.shape)
        add_dw(a1, dyab, 0)
        add_dw(a2, dybb, 1)
        add_dw(mb, doutb, 2)
        daw_ref[...] += jnp.broadcast_to(daw, daw_ref.shape)
        dbw_ref[...] += jnp.broadcast_to(dbw, dbw_ref.shape)
        dlw_ref[...] += jnp.broadcast_to(dlw, dlw_ref.shape)

    two = pl.BlockSpec((2, tm, D), lambda i: (0, i, 0))
    pcol = lambda c: pl.BlockSpec((tm, D), lambda i: (i, c))
    tok = pl.BlockSpec((tm, D), lambda i: (i, 0))
    row = lambda n: pl.BlockSpec((1, n), lambda i: (0, 0))
    row8 = lambda n: pl.BlockSpec((8, n), lambda i: (0, 0))
    once = pl.Buffered(1)
    tokf = jax.ShapeDtypeStruct((t, D), F32)
    tokb = jax.ShapeDtypeStruct((t, D), BF16)
    wspec = pl.BlockSpec((NSHARD, 3, D // NSHARD, D), lambda i: (0, 0, 0, 0), pipeline_mode=once)
    return pl.pallas_call(
        body, name="post", grid=(t // tm,),
        in_specs=[two, two, pcol(3), pcol(6), pcol(7), pcol(8), tok, tok, row(DA), row(DVB), row(D), wspec],
        out_specs=[row8(128), tok, tok, tok, tok, tok, tok, tok, wspec, row8(DA), row8(DVB), row8(D)],
        out_shape=[jax.ShapeDtypeStruct((8, 128), F32), tokf, tokf, tokb, tokb, tokb, tokb, tokf,
                   jax.ShapeDtypeStruct((NSHARD, 3, D // NSHARD, D), F32),
                   jax.ShapeDtypeStruct((8, DA), F32), jax.ShapeDtypeStruct((8, DVB), F32),
                   jax.ShapeDtypeStruct((8, D), F32)],
        compiler_params=_cparams(("arbitrary",), vmem_mb=56),
    )(oa2, ob2, p, p, p, p, x, tgt, gdn_w, gla_w, lnpost, w3)


def _adam_math(w, g, m, v):
    nm = B1 * m + (1.0 - B1) * g
    nv = B2 * v + (1.0 - B2) * (g * g)
    m_hat = nm / (1.0 - B1 ** STEP)
    v_hat = nv / (1.0 - B2 ** STEP)
    return -LR * (m_hat / (jnp.sqrt(v_hat) + ADAM_EPS) + WD * w), nm, nv


SMALL_SLOTS = (("ln_pre_w", 0, 1024, 0), ("a_log_fwd", 1024, 8, 0), ("a_log_bwd", 1024, 8, 8),
               ("dt_bias_fwd", 1152, 8, 0), ("dt_bias_bwd", 1152, 8, 8), ("gdn_norm_w", 1280, 128, 0),
               ("gk_b2_fwd", 1408, 512, 0), ("gk_b2_bwd", 1920, 512, 0), ("gla_norm_w", 2432, 256, 0),
               ("ln_post_w", 2688, 1024, 0))
SMALL_W = 3840


def _adam_small(gsum, ws, ms, vs):
    nw = len(SMALL_SLOTS)

    def body(g_ref, *refs):
        w_refs, m_refs, v_refs, outs = refs[0:nw], refs[nw:2 * nw], refs[2 * nw:3 * nw], refs[3 * nw:]
        for i, (_, off, n, shift) in enumerate(SMALL_SLOTS):
            slot = g_ref[0:1, off:off + max(n, 128)]
            if shift:
                slot = pltpu.roll(slot, 128 - shift, 1)
            g = slot[:, 0:n]
            d, nm, nv = _adam_math(w_refs[i][...], g, m_refs[i][...], v_refs[i][...])
            for k, val in enumerate((g, d, nm, nv)):
                outs[4 * i + k][...] = val

    vm = pl.BlockSpec(memory_space=pltpu.VMEM)
    res = pl.pallas_call(
        body, name="adam_small", in_specs=[vm] * (1 + 3 * nw), out_specs=[vm] * (4 * nw),
        out_shape=[jax.ShapeDtypeStruct((1, n), F32) for _, _, n, _ in SMALL_SLOTS for _ in range(4)],
    )(gsum, *ws, *ms, *vs)
    return {name: res[4 * i:4 * i + 4] for i, (name, _, _, _) in enumerate(SMALL_SLOTS)}


def _adam(w, mine, got, m, v, tr, tile0=0, name=""):
    rows, cols = w.shape
    nh = mine.shape[0] // tr

    def body(c_ref, w_ref, a_ref, b_ref, m_ref, v_ref, g_ref, d_ref, nm_ref, nv_ref):
        half = (tile0 + pl.program_id(0)) // nh
        g = jnp.where(half == c_ref[0], a_ref[...], b_ref[...])
        d, nm, nv = _adam_math(w_ref[...], g, m_ref[...], v_ref[...])
        g_ref[...] = g
        d_ref[...] = d
        nm_ref[...] = nm
        nv_ref[...] = nv

    blk = pl.BlockSpec((tr, cols), lambda i, cc: (i, 0))
    half = pl.BlockSpec((tr, cols), lambda i, cc: ((tile0 + i) % nh, 0))
    shp = jax.ShapeDtypeStruct((rows, cols), F32)
    return pl.pallas_call(
        body, name=f"adam_{name}{rows}x{cols}",
        grid_spec=pltpu.PrefetchScalarGridSpec(
            num_scalar_prefetch=1, grid=(rows // tr,),
            in_specs=[blk, half, half, blk, blk], out_specs=[blk] * 4),
        out_shape=[shp] * 4,
        compiler_params=_cparams(("parallel",)),
    )(lax.axis_index("c").reshape(1), w, mine, got, m, v)


def _adam_shard_small(mine, got, ws, ms, vs):
    def body(c_ref, a_ref, b_ref, *refs):
        w_refs, m_refs, v_refs, outs = refs[0:3], refs[3:6], refs[6:9], refs[9:]
        south = c_ref[0] == 0
        g = jnp.concatenate([jnp.where(south, a_ref[...], b_ref[...]),
                             jnp.where(south, b_ref[...], a_ref[...])], axis=0)
        grads = (g[0:5], g[8:24, 0:128], g[8:24, 128:256])
        for i, gi in enumerate(grads):
            d, nm, nv = _adam_math(w_refs[i][...], gi, m_refs[i][...], v_refs[i][...])
            for k, val in enumerate((gi, d, nm, nv)):
                outs[4 * i + k][...] = val

    vm = pl.BlockSpec(memory_space=pltpu.VMEM)
    res = pl.pallas_call(
        body, name="adam_shard_small",
        in_specs=[pl.BlockSpec(memory_space=pltpu.SMEM)] + [vm] * 11, out_specs=[vm] * 12,
        out_shape=[jax.ShapeDtypeStruct(w.shape, F32) for w in ws for _ in range(4)],
    )(lax.axis_index("c").reshape(1), mine, got, *ws, *ms, *vs)
    return [res[4 * i:4 * i + 4] for i in range(3)]


def _sum_cast(own, got):
    ns, _, r, c = own.shape
    tr = r // 4 if r % 64 == 0 else r

    def body(c_ref, a_ref, b_ref, f_ref, h_ref):
        s = a_ref[...] + b_ref[...]
        f_ref[...] = s
        h_ref[...] = s.astype(BF16)

    return pl.pallas_call(
        body, name=f"sum_cast_{r}x{c}",
        grid_spec=pltpu.PrefetchScalarGridSpec(
            num_scalar_prefetch=1, grid=(ns, r // tr),
            in_specs=[pl.BlockSpec((None, None, tr, c), lambda s, i, cc: (s, cc[0], i, 0)),
                      pl.BlockSpec((None, tr, c), lambda s, i, cc: (s, i, 0))],
            out_specs=[pl.BlockSpec((None, tr, c), lambda s, i, cc: (s, i, 0)),
                       pl.BlockSpec((None, tr, c), lambda s, i, cc: (s, i, 0))]),
        out_shape=[jax.ShapeDtypeStruct((ns, r, c), F32), jax.ShapeDtypeStruct((ns, r, c), BF16)],
        compiler_params=_cparams(("parallel", "parallel")),
    )(lax.axis_index("c").reshape(1), own, got)


def _sum4(mine, got):
    _, r, c = mine.shape
    tr = r // 4 if r % 64 == 0 else r

    def body(s_ref, a_ref, g_ref, o_ref):
        acc = a_ref[...] + g_ref[0].astype(F32)
        acc = acc + g_ref[1].astype(F32)
        o_ref[...] = acc + g_ref[2].astype(F32)

    shard = (2 * lax.axis_index("x") + lax.axis_index("y")).reshape(1)
    return pl.pallas_call(
        body, name=f"sum4_{r}x{c}",
        grid_spec=pltpu.PrefetchScalarGridSpec(
            num_scalar_prefetch=1, grid=(r // tr,),
            in_specs=[pl.BlockSpec((None, tr, c), lambda i, ss: (ss[0], i, 0)),
                      pl.BlockSpec((3, tr, c), lambda i, ss: (0, i, 0))],
            out_specs=pl.BlockSpec((tr, c), lambda i, ss: (i, 0))),
        out_shape=jax.ShapeDtypeStruct((r, c), F32),
        compiler_params=_cparams(("parallel",)),
    )(shard, mine, got)


def _place():
    x, y, c = lax.axis_index("x"), lax.axis_index("y"), lax.axis_index("c")
    chips = [(1 - x, y), (x, 1 - y), (1 - x, 1 - y)]
    return x, y, c, chips


def _gather_weights(parts):
    npart = len(parts)

    def body(*refs):
        ins, outs = refs[:npart], refs[npart:2 * npart]
        send_sems, recv_sems = refs[2 * npart:]
        x, y, c, chips = _place()
        sibling = (x, y, 1 - c)
        mine = 2 * x + y

        def remote(k, p, shard, half, to, src=None):
            dst = outs[p].at[shard, half]
            return pltpu.make_async_remote_copy(
                src_ref=dst if src is None else src, dst_ref=dst,
                send_sem=send_sems.at[k], recv_sem=recv_sems.at[k], device_id=to, device_id_type=MESH)

        first = [remote(j * npart + p, p, mine, c, (*chip, c), src=ins[p].at[c])
                 for j, chip in enumerate(chips) for p in range(npart)]
        for cp in first:
            cp.start()
        passed = []
        for j, (cx, cy) in enumerate(chips):
            for p in range(npart):
                remote(j * npart + p, p, 2 * cx + cy, c, (x, y, c)).wait_recv()
                fw = remote((3 + j) * npart + p, p, 2 * cx + cy, c, sibling)
                fw.start()
                passed.append(fw)
        for j, (cx, cy) in enumerate(chips):
            for p in range(npart):
                remote((3 + j) * npart + p, p, 2 * cx + cy, 1 - c, (x, y, c)).wait_recv()
        for cp in first + passed:
            cp.wait_send()

    got = pl.pallas_call(
        body, name="gather_weights",
        in_specs=[ANY] * npart, out_specs=[ANY] * npart,
        out_shape=[jax.ShapeDtypeStruct((NSHARD,) + a.shape, a.dtype) for a in parts],
        scratch_shapes=[pltpu.SemaphoreType.DMA((6 * npart,)), pltpu.SemaphoreType.DMA((6 * npart,))],
    )(*parts)
    mine = 2 * lax.axis_index("x") + lax.axis_index("y")
    return [lax.dynamic_update_index_in_dim(g, a, mine, 0) for g, a in zip(got, parts)]


def _swap_halves(parts, tag=""):
    npart = len(parts)

    def body(*refs):
        ins, outs = refs[:npart], refs[npart:2 * npart]
        send_sems, recv_sems = refs[2 * npart:]
        x, y, c, _ = _place()
        cps = [pltpu.make_async_remote_copy(
            src_ref=ins[p].at[s, 1 - c], dst_ref=outs[p].at[s],
            send_sem=send_sems.at[s * npart + p], recv_sem=recv_sems.at[s * npart + p],
            device_id=(x, y, 1 - c), device_id_type=MESH) for s in range(NSHARD) for p in range(npart)]
        for cp in cps:
            cp.start()
        for cp in cps:
            cp.wait()

    return pl.pallas_call(
        body, name="swap_halves" + tag, in_specs=[ANY] * npart, out_specs=[ANY] * npart,
        out_shape=[jax.ShapeDtypeStruct((NSHARD,) + a.shape[2:], a.dtype) for a in parts],
        scratch_shapes=[pltpu.SemaphoreType.DMA((NSHARD * npart,)), pltpu.SemaphoreType.DMA((NSHARD * npart,))],
    )(*parts)


def _scatter_shards(parts):
    npart = len(parts)

    def body(*refs):
        ins, outs = refs[:npart], refs[npart:2 * npart]
        send_sems, recv_sems = refs[2 * npart:]
        x, y, c, chips = _place()
        cps = [pltpu.make_async_remote_copy(
            src_ref=ins[p].at[2 * cx + cy], dst_ref=outs[p].at[j],
            send_sem=send_sems.at[j * npart + p], recv_sem=recv_sems.at[j * npart + p],
            device_id=(cx, cy, c), device_id_type=MESH)
            for j, (cx, cy) in enumerate(chips) for p in range(npart)]
        for cp in cps:
            cp.start()
        for cp in cps:
            cp.wait()

    return pl.pallas_call(
        body, name="scatter_shards", in_specs=[ANY] * npart, out_specs=[ANY] * npart,
        out_shape=[jax.ShapeDtypeStruct((3,) + a.shape[1:], a.dtype) for a in parts],
        scratch_shapes=[pltpu.SemaphoreType.DMA((3 * npart,)), pltpu.SemaphoreType.DMA((3 * npart,))],
    )(*parts)


HBM = pl.BlockSpec(memory_space=pltpu.HBM)
SEM = pl.BlockSpec(memory_space=pltpu.SEMAPHORE)
EFFECT = pltpu.SideEffectType.DATAFLOW_SIDE_EFFECTING


def _scatter_copies(srcs, lands, send_sems, recv_sems, waiting):
    x, y, c, chips = _place()
    n = len(srcs)
    return [pltpu.make_async_remote_copy(
        src_ref=srcs[p].at[2 * cx + cy], dst_ref=lands[p].at[j],
        send_sem=send_sems.at[j * n + p], recv_sem=recv_sems.at[j * n + p],
        device_id=(cx, cy, c), device_id_type=MESH)
        for j, (cx, cy) in enumerate(chips) for p in range(n)]


def _proj_copies(srcs, lands, send_sems, recv_sems, waiting):
    x, y, c, chips = _place()
    mine = 2 * x + y
    return [pltpu.make_async_remote_copy(
        src_ref=srcs[0].at[c], dst_ref=lands[0].at[mine, c],
        send_sem=send_sems.at[2 * j + to], recv_sem=recv_sems.at[2 * j + (to if waiting else c)],
        device_id=(cx, cy, to), device_id_type=MESH)
        for j, (cx, cy) in enumerate(chips) for to in range(2)]


def _start_copies(copies, nsem, parts, lands, name, after=None):
    n = len(parts)
    extra = [] if after is None else [after]

    def body(*refs):
        outs = refs[2 * n + len(extra):]
        for cp in copies(refs[:n], refs[n:2 * n], outs[0], outs[1], False):
            cp.start()
        outs[-1][...] = jnp.zeros_like(outs[-1])

    res = pl.pallas_call(
        body, name=name,
        out_shape=(pltpu.SemaphoreType.DMA((nsem,)), pltpu.SemaphoreType.DMA((nsem,)),
                   *[pltpu.HBM(a.shape, a.dtype) for a in parts], *[pltpu.HBM(a.shape, a.dtype) for a in lands],
                   jax.ShapeDtypeStruct((8, 128), F32)),
        in_specs=[HBM] * (2 * n) + [ANY] * len(extra),
        out_specs=(SEM, SEM, *[HBM] * (2 * n), pl.BlockSpec(memory_space=pltpu.VMEM)),
        input_output_aliases={i: 2 + i for i in range(2 * n)},
        compiler_params=pltpu.CompilerParams(has_side_effects=EFFECT),
    )(*[pltpu.with_memory_space_constraint(a, pltpu.HBM) for a in parts],
      *[pltpu.with_memory_space_constraint(a, pltpu.HBM) for a in lands], *extra)
    return res[0], res[1], res[2:2 + n], res[2 + n:2 + 2 * n], res[-1]


def _wait_copies(copies, started, after, name):
    send_sems, recv_sems, srcs, lands, _ = started
    n = len(srcs)

    def body(*refs):
        for cp in copies(refs[:n], refs[n:2 * n], refs[2 * n], refs[2 * n + 1], True):
            cp.wait_send()
            cp.wait_recv()

    res = pl.pallas_call(
        body, name=name,
        out_shape=tuple(pltpu.HBM(a.shape, a.dtype) for a in (*srcs, *lands)),
        in_specs=[HBM] * (2 * n) + [SEM, SEM, ANY], out_specs=tuple([HBM] * (2 * n)),
        input_output_aliases={i: i for i in range(2 * n)},
        compiler_params=pltpu.CompilerParams(has_side_effects=EFFECT),
    )(*srcs, *lands, send_sems, recv_sems, after)
    return res[n:]


def _join_halves(parts):
    npart = len(parts)

    def body(*refs):
        ins, outs = refs[:npart], refs[npart:2 * npart]
        send_sems, recv_sems = refs[2 * npart:]
        x, y, c, _ = _place()
        cps = [pltpu.make_async_remote_copy(
            src_ref=ins[p], dst_ref=outs[p], send_sem=send_sems.at[p], recv_sem=recv_sems.at[p],
            device_id=(x, y, 1 - c), device_id_type=MESH) for p in range(npart)]
        for cp in cps:
            cp.start()
        for cp in cps:
            cp.wait()

    return pl.pallas_call(
        body, name="join_halves", in_specs=[ANY] * npart, out_specs=[ANY] * npart,
        out_shape=[jax.ShapeDtypeStruct(a.shape, a.dtype) for a in parts],
        scratch_shapes=[pltpu.SemaphoreType.DMA((npart,)), pltpu.SemaphoreType.DMA((npart,))],
    )(*parts)


def _allreduce_small(v):
    r, ncol = v.shape

    def body(v_ref, o_ref, buf, send_sems, recv_sems):
        x, y, c, _ = _place()
        me = 4 * x + 2 * y + c
        buf[me] = v_ref[...]
        cps = []
        for k in range(1, 8):
            px, py, pc = x ^ (k >> 2), y ^ ((k >> 1) & 1), c ^ (k & 1)
            cps.append(pltpu.make_async_remote_copy(
                src_ref=v_ref, dst_ref=buf.at[me], send_sem=send_sems.at[k - 1], recv_sem=recv_sems.at[k - 1],
                device_id=(px, py, pc), device_id_type=MESH))
        for cp in cps:
            cp.start()
        for k in range(1, 8):
            px, py, pc = x ^ (k >> 2), y ^ ((k >> 1) & 1), c ^ (k & 1)
            pltpu.make_async_remote_copy(
                src_ref=v_ref, dst_ref=buf.at[4 * px + 2 * py + pc], send_sem=send_sems.at[k - 1],
                recv_sem=recv_sems.at[k - 1], device_id=(px, py, pc), device_id_type=MESH).wait_recv()
        for cp in cps:
            cp.wait_send()
        acc = buf[0]
        for d in range(1, 8):
            acc = acc + buf[d]
        o_ref[...] = acc

    return pl.pallas_call(
        body, name="allreduce_small",
        in_specs=[pl.BlockSpec(memory_space=pltpu.VMEM)], out_specs=pl.BlockSpec(memory_space=pltpu.VMEM),
        out_shape=jax.ShapeDtypeStruct((r, ncol), F32),
        scratch_shapes=[pltpu.VMEM((8, r, ncol), F32), pltpu.SemaphoreType.DMA((7,)), pltpu.SemaphoreType.DMA((7,))],
    )(v)


def _permute_rows(shards):
    w0, w1, w2, w3 = shards
    zeros = jnp.zeros((NPERM - 9280, w0.shape[1]), w0.dtype)
    return jnp.concatenate([w0, w1[0:1776], w1[1808:2320], w2, w3[0:240], w3[272:2320],
                            w1[1776:1808], w3[240:272], zeros], axis=0)


def _unpermute_rows(g):
    s1 = jnp.concatenate([g[2320:4096], g[9216:9248], g[4096:4608]], axis=0)
    s3 = jnp.concatenate([g[6928:7168], g[9248:9280], g[7168:9216]], axis=0)
    return jnp.stack([g[0:2320], s1, g[4608:6928], s3], axis=0)


def _pack_shard_small(conv, w2f, w2b):
    top = jnp.pad(conv, ((0, 8 - conv.shape[0]), (0, 0)))
    mid = jnp.pad(jnp.concatenate([w2f, w2b], axis=1), ((0, 0), (0, 768 - 256)))
    return jnp.concatenate([top, mid, jnp.zeros((8, 768), conv.dtype)], axis=0)


def _unpack_shard_small(a):
    return a[0:5], a[8:24, 0:128], a[8:24, 128:256]


def kernel(x, ln_pre_w, w_in, conv_w, a_log_fwd, a_log_bwd, dt_bias_fwd, dt_bias_bwd, gdn_norm_w, w_proj_gdn, gk_w2_fwd, gk_b2_fwd, gk_w2_bwd, gk_b2_bwd, gla_norm_w, w_proj_gla, w_out, ln_post_w, loss_target, m_ln_pre_w, m_w_in, m_conv_w, m_a_log_fwd, m_a_log_bwd, m_dt_bias_fwd, m_dt_bias_bwd, m_gdn_norm_w, m_w_proj_gdn, m_gk_w2_fwd, m_gk_b2_fwd, m_gk_w2_bwd, m_gk_b2_bwd, m_gla_norm_w, m_w_proj_gla, m_w_out, m_ln_post_w, v_ln_pre_w, v_w_in, v_conv_w, v_a_log_fwd, v_a_log_bwd, v_dt_bias_fwd, v_dt_bias_bwd, v_gdn_norm_w, v_w_proj_gdn, v_gk_w2_fwd, v_gk_b2_fwd, v_gk_w2_bwd, v_gk_b2_bwd, v_gla_norm_w, v_w_proj_gla, v_w_out, v_ln_post_w):
    t = x.shape[1]
    x2, tgt = x[0], loss_target[0]

    win_l = w_in[0].T.astype(BF16).reshape(2, SHW // 2, D)
    proj_l = jnp.concatenate([w_proj_gdn[0], w_proj_gla[0], w_out[0]], axis=0).astype(BF16).reshape(2, 384, D)
    small_l = _pack_shard_small(conv_w[0], gk_w2_fwd[0], gk_w2_bwd[0]).reshape(2, 16, 768)
    win_g, small_g = _gather_weights([win_l, small_l])
    proj_started = _start_copies(_proj_copies, 6, [proj_l], [lax.empty((NSHARD, 2, 384, D), BF16)],
                                 "gather_proj_start", after=small_g)
    wperm = _permute_rows(win_g.reshape(NSHARD, SHW, D))
    small_g = small_g.reshape(NSHARD, 32, 768)
    convw = small_g[:, 0:8, :].transpose(1, 0, 2).reshape(8, 3 * D)
    w2f = small_g[:, 8:24, 0:128].transpose(1, 0, 2).reshape(16, 512)
    w2b = small_g[:, 8:24, 128:256].transpose(1, 0, 2).reshape(16, 512)
    w2f_pad = jnp.pad(w2f, ((32, 80), (0, 0)))
    w2b_pad = jnp.pad(w2b, ((48, 64), (0, 0)))
    alog_row = jnp.pad(jnp.concatenate([a_log_fwd, a_log_bwd], axis=1), ((0, 0), (0, 112)))
    dt_row = jnp.pad(jnp.concatenate([dt_bias_fwd, dt_bias_bwd], axis=1), ((0, 0), (0, 112)))

    p, h = _inproj(x2, ln_pre_w + proj_started[4][0:1, 0:1], wperm)
    qn, kn, vc = (_qkv_fwd(p, convw, kind) for kind in range(3))
    gsm, gk = _gates_fwd(p, alog_row, dt_row, w2f_pad, gk_b2_fwd, w2b_pad, gk_b2_bwd)
    g2, b2 = _gcum_fwd(gsm)
    u, w, at, qd, kd, el, tinv = _gdn_intra_fwd(qn, kn, vc, g2, b2)
    oa2, sa = _gdn_scan_fwd(u, w, at, qd, kd, el)
    qg, kdb, intra, elb = _gla_intra_fwd(p, gk)
    ob2, sb = _gla_scan_fwd(p, qg, kdb, intra, elb)

    (proj_land,) = _wait_copies(_proj_copies, proj_started, ob2, "gather_proj_wait")
    mine = 2 * lax.axis_index("x") + lax.axis_index("y")
    w3 = lax.dynamic_update_index_in_dim(proj_land, proj_l, mine, 0).reshape(NSHARD, 3, D // NSHARD, D)
    (loss8, doa, dob, dz, dgb, dga, dgB, dyres, dw3, dgdn_w, dgla_w, dlnpost) = _post(
        oa2, ob2, p, x2, tgt, gdn_norm_w, gla_norm_w, ln_post_w, w3)

    g_proj = dw3.reshape(NSHARD, 2, 384, D)
    sum_proj = _sum_cast(g_proj, _swap_halves([g_proj], "_proj")[0])
    started_proj = _start_copies(_scatter_copies, 3, [sum_proj[1]], [lax.empty((3, 384, D), BF16)],
                                 "scatter_proj_start")
    du, dw, dat, dqd, dkd, del_ = _gdn_scan_bwd(u, w, at, qd, kd, el + started_proj[4][0, 0], sa, doa)
    dqn, dkn, dvc, dg2, db2 = _gdn_intra_bwd(qn, kn, vc, g2, b2, tinv, du, dw, dat, dqd, dkd, del_)
    dgsm = _gcum_bwd(gsm, dg2, db2)
    dqg, dkdb, dvs, delb = _gla_scan_bwd(p, qg, kdb, elb, sb, dob)
    dqb, dkb, dvb, dgk = _gla_intra_bwd(p, gk, dqg, dkdb, dvs, delb, dob)
    (dps, dalog8, ddt8, dw2f_pad, db2f8, dw2b_pad, db2b8) = _gates_bwd(
        p, alog_row, dt_row, w2f_pad, gk_b2_fwd, w2b_pad, gk_b2_bwd, dgsm, dgk)
    dpre, dconv = zip(*[_qkv_bwd(p, convw, g, kind) for kind, g in enumerate((dqn, dkn, dvc))])

    pieces = (jnp.concatenate([a.astype(BF16) for a in (*dpre, dz, dqb, dkb, dvb, dgb, dga, dgB, dps)], axis=1),)
    dwperm = _inproj_dw(h, pieces)

    g_in = _unpermute_rows(dwperm).reshape(NSHARD, 2, SHW // 2, D)
    dconv_full = jnp.concatenate(dconv, axis=1)
    dw2f, dw2b = dw2f_pad[32:48], dw2b_pad[48:64]
    g_small = jnp.stack([_pack_shard_small(dconv_full[0:5, 768 * s:768 * (s + 1)],
                                           dw2f[:, 128 * s:128 * (s + 1)], dw2b[:, 128 * s:128 * (s + 1)])
                         for s in range(NSHARD)])
    g_small = g_small.reshape(NSHARD, 2, 16, 768)
    parts = [g_in, g_small]
    got = _swap_halves(parts)
    sums = [_sum_cast(a, b) for a, b in zip(parts, got)]
    hbs = [hb for _, hb in sums]
    started = _start_copies(_scatter_copies, 3 * len(hbs), hbs,
                            [lax.empty((3,) + a.shape[1:], a.dtype) for a in hbs], "scatter_start")
    dx, dlnpre8 = _inproj_dx(pieces, wperm, x2, ln_pre_w + started[4][0:1, 0:1], dyres)

    gsmall = _allreduce_small(jnp.concatenate(
        [dlnpre8, dalog8, ddt8, dgdn_w, db2f8, db2b8, dgla_w, dlnpost, loss8], axis=1))
    smalls = dict(ln_pre_w=(ln_pre_w, m_ln_pre_w, v_ln_pre_w), a_log_fwd=(a_log_fwd, m_a_log_fwd, v_a_log_fwd),
                  a_log_bwd=(a_log_bwd, m_a_log_bwd, v_a_log_bwd),
                  dt_bias_fwd=(dt_bias_fwd, m_dt_bias_fwd, v_dt_bias_fwd),
                  dt_bias_bwd=(dt_bias_bwd, m_dt_bias_bwd, v_dt_bias_bwd),
                  gdn_norm_w=(gdn_norm_w, m_gdn_norm_w, v_gdn_norm_w),
                  gk_b2_fwd=(gk_b2_fwd, m_gk_b2_fwd, v_gk_b2_fwd), gk_b2_bwd=(gk_b2_bwd, m_gk_b2_bwd, v_gk_b2_bwd),
                  gla_norm_w=(gla_norm_w, m_gla_norm_w, v_gla_norm_w), ln_post_w=(ln_post_w, m_ln_post_w, v_ln_post_w))
    names = [name for name, _, _, _ in SMALL_SLOTS]
    small = _adam_small(gsmall, *([smalls[n][i] for n in names] for i in range(3)))

    landed_proj = _wait_copies(_scatter_copies, started_proj, small["ln_pre_w"][1], "scatter_proj_wait")
    landed = _wait_copies(_scatter_copies, started, small["ln_pre_w"][1], "scatter_wait")
    sums = [sums[0], sum_proj, sums[1]]
    halves = [_sum4(f, g) for (f, _), g in zip(sums, [landed[0], landed_proj[0], landed[1]])]
    theirs = _join_halves(halves)

    a_in = [a.T for a in _adam(w_in[0].T, halves[0], theirs[0], m_w_in[0].T, v_w_in[0].T, 232, name="in")]
    a_pr = [_adam(w[0], halves[1], theirs[1], m[0], v[0], 128, tile0=2 * i, name=f"proj{i}")
            for i, (w, m, v) in enumerate(((w_proj_gdn, m_w_proj_gdn, v_w_proj_gdn),
                                           (w_proj_gla, m_w_proj_gla, v_w_proj_gla), (w_out, m_w_out, v_w_out)))]
    a_ss = _adam_shard_small(halves[2], theirs[2], (conv_w[0], gk_w2_fwd[0], gk_w2_bwd[0]),
                             (m_conv_w[0], m_gk_w2_fwd[0], m_gk_w2_bwd[0]),
                             (v_conv_w[0], v_gk_w2_fwd[0], v_gk_w2_bwd[0]))

    def family(k):
        conv, w2f_, w2b_ = a_ss[0][k], a_ss[1][k], a_ss[2][k]
        s = {n: small[n][k] for n in names}
        return [s["ln_pre_w"], a_in[k][None], conv[None], s["a_log_fwd"], s["a_log_bwd"], s["dt_bias_fwd"],
                s["dt_bias_bwd"], s["gdn_norm_w"], a_pr[0][k][None], w2f_[None], s["gk_b2_fwd"], w2b_[None],
                s["gk_b2_bwd"], s["gla_norm_w"], a_pr[1][k][None], a_pr[2][k][None], s["ln_post_w"]]

    return (gsmall[0, SMALL_W - 128], dx[None], *family(0), *family(1), *family(2), *family(3))
```

```python
import functools

import jax
import jax.numpy as jnp
from jax import lax
from jax.experimental import pallas as pl
from jax.experimental.pallas import tpu as pltpu

F32 = jnp.float32
BF16 = jnp.bfloat16
HI = lax.Precision.HIGHEST
MESH = pl.DeviceIdType.MESH

D = 1024
CH = 64
EPS = 1e-6
NA, DA = 8, 128
NB, DKB, DVB = 4, 128, 256
NSHARD = 4
SHW = 2320
NPERM = 9728
PS_BLOCK = 72
LR, B1, B2, ADAM_EPS, WD, STEP = 0.001, 0.9, 0.999, 1e-08, 0.01, 10

ANY = pl.BlockSpec(memory_space=pl.ANY)


def _cparams(sem=None, vmem_mb=48):
    return pltpu.CompilerParams(dimension_semantics=sem, vmem_limit_bytes=vmem_mb << 20)


def _bdot(a, b, ca, cb):
    return lax.dot_general(a.astype(BF16), b.astype(BF16), (((ca,), (cb,)), ((), ())),
                           preferred_element_type=F32)


@jax.custom_vjp
def mm(a, b):
    return _bdot(a, b, 1, 0)


def _mm_fwd(a, b):
    return _bdot(a, b, 1, 0), (a, b)


def _mm_bwd(res, g):
    a, b = res
    return _bdot(g, b, 1, 1), _bdot(a, g, 0, 0)


mm.defvjp(_mm_fwd, _mm_bwd)


@jax.custom_vjp
def mm_nt(a, b):
    return _bdot(a, b, 1, 1)


def _mm_nt_fwd(a, b):
    return _bdot(a, b, 1, 1), (a, b)


def _mm_nt_bwd(res, g):
    a, b = res
    return _bdot(g, b, 1, 0), _bdot(g, a, 0, 0)


mm_nt.defvjp(_mm_nt_fwd, _mm_nt_bwd)


@jax.custom_vjp
def mm_tn(a, b):
    return _bdot(a, b, 0, 0)


def _mm_tn_fwd(a, b):
    return _bdot(a, b, 0, 0), (a, b)


def _mm_tn_bwd(res, g):
    a, b = res
    return _bdot(b, g, 1, 1), _bdot(a, g, 1, 0)


mm_tn.defvjp(_mm_tn_fwd, _mm_tn_bwd)


def dot_hi(a, b):
    return lax.dot_general(a, b, (((1,), (0,)), ((), ())), precision=HI, preferred_element_type=F32)


def _split3(x):
    x1 = x.astype(BF16)
    r = x - x1.astype(F32)
    x2 = r.astype(BF16)
    return x1, x2, (r - x2.astype(F32)).astype(BF16)


def _cdot(c, x, cc, cx, c_first=True):
    parts = _split3(x)
    if c_first:
        return _bdot(c, parts[0], cc, cx) + _bdot(c, parts[1], cc, cx) + _bdot(c, parts[2], cc, cx)
    return _bdot(parts[0], c, cx, cc) + _bdot(parts[1], c, cx, cc) + _bdot(parts[2], c, cx, cc)


@jax.custom_vjp
def cmm(c, x):
    return _cdot(c, x, 1, 0)


def _cmm_fwd(c, x):
    return _cdot(c, x, 1, 0), c


def _cmm_bwd(c, g):
    return jnp.zeros_like(c), _cdot(c, g, 0, 0)


cmm.defvjp(_cmm_fwd, _cmm_bwd)


@jax.custom_vjp
def mmc(x, c):
    return _cdot(c, x, 0, 1, c_first=False)


def _mmc_fwd(x, c):
    return _cdot(c, x, 0, 1, c_first=False), c


def _mmc_bwd(c, g):
    return _cdot(c, g, 1, 1, c_first=False), jnp.zeros_like(c)


mmc.defvjp(_mmc_fwd, _mmc_bwd)


def _sigmoid(x):
    return 1.0 / (1.0 + jnp.exp(-x))


def _silu(x):
    return x * _sigmoid(x)


def _softplus(x):
    return jnp.maximum(x, 0.0) + jnp.log(1.0 + jnp.exp(-jnp.abs(x)))


def _rms(x, w):
    return x * lax.rsqrt(jnp.mean(x * x, axis=-1, keepdims=True) + EPS) * w


SC = 256


class _Consts:
    def __init__(self, rev):
        r = lax.broadcasted_iota(jnp.int32, (SC, SC), 0)
        c = lax.broadcasted_iota(jnp.int32, (SC, SC), 1)
        same = (r >> 6) == (c >> 6)
        a = jnp.where(rev, c, r)
        b = jnp.where(rev, r, c)
        self.incl = same & (a >= b)
        self.strict = same & (a > b)
        self.incl_f = self.incl.astype(F32)
        self.eye = (r == c).astype(F32)
        rows = lax.broadcasted_iota(jnp.int32, (SC, 1), 0)
        self.last_col = ((rows & (CH - 1)) == jnp.where(rev, 0, CH - 1)).astype(F32)
        rr = lax.broadcasted_iota(jnp.int32, (SC, CH), 0)
        cc = lax.broadcasted_iota(jnp.int32, (SC, CH), 1)
        self.fold = ((rr & (CH - 1)) == cc).astype(F32)


def _dot3(a, b, ca=1, cb=0):
    ah, bh = a.astype(BF16), b.astype(BF16)
    al, bl = (a - ah.astype(F32)).astype(BF16), (b - bh.astype(F32)).astype(BF16)
    return _bdot(ah, bh, ca, cb) + (_bdot(ah, bl, ca, cb) + _bdot(al, bh, ca, cb))


TRI_SPLIT_LEVELS = 2


def _tri_inv(low, eye):
    n = -low
    acc = eye + n
    p = n
    for level in range(5):
        dot = _dot3 if level < TRI_SPLIT_LEVELS else (lambda a, b: _bdot(a, b, 1, 0))
        p = dot(p, p)
        acc = acc + dot(acc, p)
    return acc


@jax.custom_vjp
def _solve2(low, rv, rk, tinv):
    x = _dot3(tinv, jnp.concatenate([rv, rk], axis=1))
    return x[:, :DA], x[:, DA:]


def _solve2_fwd(low, rv, rk, tinv):
    x = _dot3(tinv, jnp.concatenate([rv, rk], axis=1))
    return (x[:, :DA], x[:, DA:]), (x, tinv)


def _solve2_bwd(res, g):
    x, tinv = res
    drhs = _dot3(tinv, jnp.concatenate(g, axis=1), 0, 0)
    return -_dot3(drhs, x, 1, 1), drhs[:, :DA], drhs[:, DA:], jnp.zeros_like(tinv)


_solve2.defvjp(_solve2_fwd, _solve2_bwd)


def _chunk_last(x, cs):
    xs = (x * cs.last_col).reshape(SC // CH, CH, x.shape[1])
    return jnp.broadcast_to(jnp.sum(xs, axis=1, keepdims=True), xs.shape).reshape(x.shape)


def _gdn_decay(g, cs):
    gw = jnp.concatenate([g] * (SC // DA), axis=1)
    grow = jnp.sum(cs.eye * gw, axis=0, keepdims=True)
    return jnp.where(cs.incl, jnp.exp(jnp.where(cs.incl, gw - grow, 0.0)), 0.0)


def _gdn_intra(q, k, v, g, bx, tinv, cs):
    decay = _gdn_decay(g, cs)
    kb = k * bx
    low = jnp.where(cs.strict, mm_nt(kb, k) * decay, 0.0)
    eg = jnp.exp(g)
    made = tinv is None
    if made:
        tinv = _tri_inv(low, cs.eye)
    u, w = _solve2(low, v * bx, kb * eg, tinv)
    attn = mmc(mm_nt(q, k) * decay, cs.fold)
    qd = q * eg
    glast = _chunk_last(g, cs)
    kd = k * jnp.exp(glast - g)
    outs = (u, w, attn, qd, kd, jnp.exp(glast))
    return outs + (tinv,) if made else outs


def _gdn_scan(u, w, attn, qd, kd, el, s):
    vn = u - mm(w, s)
    o = mm(qd, s) + mm(attn, vn)
    sn = s * el + mm_tn(kd, vn)
    return o, sn


def _gla_intra(q, k, v, gk, cs):
    gc = cmm(cs.incl_f, gk)
    qg = q * (DKB ** -0.5) * jnp.exp(gc)
    kg = k * jnp.exp(-gc)
    attn = jnp.where(cs.incl, mm_nt(qg, kg), 0.0)
    intra = mm(attn, v)
    glast = _chunk_last(gc, cs)
    kd = k * jnp.exp(glast - gc)
    return qg, kd, intra, jnp.exp(glast)


def _gla_scan(qg, kd, v, el, st):
    o = mm_nt(qg, st)
    stn = st * el + mm_tn(v, kd)
    return o, stn


def _shift_rows(x, s):
    if s == 0:
        return x
    t = x.shape[0]
    rolled = pltpu.roll(x, (-s) % t, 0)
    rows = lax.broadcasted_iota(jnp.int32, x.shape, 0)
    return jnp.where((rows + s >= 0) & (rows + s < t), rolled, 0.0)


@jax.custom_vjp
def _conv5(x, w):
    acc = w[0:1] * _shift_rows(x, -2)
    for j in range(1, 5):
        acc = acc + w[j:j + 1] * _shift_rows(x, j - 2)
    return acc


def _conv5_fwd(x, w):
    return _conv5(x, w), (x, w)


def _conv5_bwd(res, g):
    x, w = res
    dx = w[0:1] * _shift_rows(g, 2)
    for j in range(1, 5):
        dx = dx + w[j:j + 1] * _shift_rows(g, 2 - j)
    rows = lax.broadcasted_iota(jnp.int32, w.shape, 0)
    dw = jnp.zeros_like(w)
    for j in range(5):
        dwj = jnp.sum(g * _shift_rows(x, j - 2), axis=0, keepdims=True)
        dw = dw + jnp.where(rows == j, dwj, 0.0)
    return dx, dw


_conv5.defvjp(_conv5_fwd, _conv5_bwd)


def _qkv_act(kind):
    def f(x, w):
        c = _silu(_conv5(x, w))
        if kind == 2:
            return c
        c = c * lax.rsqrt(jnp.sum(c * c, axis=-1, keepdims=True) + EPS)
        return c * (DA ** -0.5) if kind == 0 else c
    return f


def _inproj(x, lnw, wperm, tn=512):
    t = x.shape[0]
    tm = min(t, 2048)

    def body(x_ref, lnw_ref, w_ref, p_ref, h_ref, hbuf):
        @pl.when(pl.program_id(1) == 0)
        def _():
            hb = _rms(x_ref[...], lnw_ref[...]).astype(BF16)
            hbuf[...] = hb
            h_ref[...] = hb
        p_ref[...] = _bdot(hbuf[...], w_ref[...], 1, 1)

    return pl.pallas_call(
        body, name="inproj", grid=(t // tm, NPERM // tn),
        in_specs=[pl.BlockSpec((tm, D), lambda i, j: (i, 0)),
                  pl.BlockSpec((1, D), lambda i, j: (0, 0)),
                  pl.BlockSpec((tn, D), lambda i, j: (j, 0))],
        out_specs=[pl.BlockSpec((tm, tn), lambda i, j: (i, j)),
                   pl.BlockSpec((tm, D), lambda i, j: (i, 0))],
        out_shape=[jax.ShapeDtypeStruct((t, NPERM), F32),
                   jax.ShapeDtypeStruct((t, D), BF16)],
        scratch_shapes=[pltpu.VMEM((tm, D), BF16)],
        compiler_params=_cparams(("parallel", "arbitrary")),
    )(x, lnw, wperm)


DP_TILE = 512
DP_PIECES = ((0, 19),)


def _piece_specs(tm, j_first):
    specs = []
    for j0, n in DP_PIECES:
        def imap(a, b, j0=j0, n=n):
            j, i = (a, b) if j_first else (b, a)
            inside = (j >= j0) & (j < j0 + n)
            return jnp.where(inside, i, 0), jnp.clip(j - j0, 0, n - 1)
        specs.append(pl.BlockSpec((tm, DP_TILE), imap))
    return specs


def _for_piece(j, refs, fn):
    for (j0, n), ref in zip(DP_PIECES, refs):
        @pl.when((j >= j0) & (j < j0 + n))
        def _(ref=ref):
            fn(ref[...])


def _inproj_dw(h, pieces):
    t = h.shape[0]
    tm = min(t, 2048)
    npc = len(pieces)

    def body(h_ref, *refs):
        dw_ref = refs[npc]
        single = t == tm and npc == 1

        if not single:
            @pl.when(pl.program_id(1) == 0)
            def _():
                dw_ref[...] = jnp.zeros_like(dw_ref)

        def add(dp):
            if single:
                dw_ref[...] = _bdot(dp, h_ref[...], 0, 0)
            else:
                dw_ref[...] += _bdot(dp, h_ref[...], 0, 0)
        _for_piece(pl.program_id(0), refs[:npc], add)

    return pl.pallas_call(
        body, name="inproj_dw", grid=(NPERM // DP_TILE, t // tm),
        in_specs=[pl.BlockSpec((tm, D), lambda j, i: (i, 0))] + _piece_specs(tm, True),
        out_specs=pl.BlockSpec((DP_TILE, D), lambda j, i: (j, 0)),
        out_shape=jax.ShapeDtypeStruct((NPERM, D), F32),
        compiler_params=_cparams(("parallel", "arbitrary")),
    )(h, *pieces)


def _inproj_dx(pieces, wperm, x, lnw, dyres):
    t = x.shape[0]
    tm = min(t, 1024)
    tn = DP_TILE
    nj = NPERM // tn
    npc = len(pieces)

    def body(*refs):
        w_ref, x_ref, lnw_ref, dy_ref, dx_ref, dlnw_ref, acc = refs[npc:]
        j = pl.program_id(1)

        @pl.when(j == 0)
        def _():
            acc[...] = jnp.zeros_like(acc)

        def add(dp):
            acc[...] += _bdot(dp, w_ref[...], 1, 0)
        _for_piece(j, refs[:npc], add)

        @pl.when(j == nj - 1)
        def _():
            _, vjp = jax.vjp(_rms, x_ref[...], lnw_ref[...])
            dx, dlnw = vjp(acc[...])
            dx_ref[...] = dx + dy_ref[...]

            @pl.when(pl.program_id(0) == 0)
            def _():
                dlnw_ref[...] = jnp.zeros_like(dlnw_ref)
            dlnw_ref[...] += jnp.broadcast_to(dlnw, dlnw_ref.shape)

    return pl.pallas_call(
        body, name="inproj_dx", grid=(t // tm, nj),
        in_specs=_piece_specs(tm, False) + [
                  pl.BlockSpec((tn, D), lambda i, j: (j, 0)),
                  pl.BlockSpec((tm, D), lambda i, j: (i, 0)),
                  pl.BlockSpec((1, D), lambda i, j: (0, 0)),
                  pl.BlockSpec((tm, D), lambda i, j: (i, 0))],
        out_specs=[pl.BlockSpec((tm, D), lambda i, j: (i, 0)),
                   pl.BlockSpec((8, D), lambda i, j: (0, 0))],
        out_shape=[jax.ShapeDtypeStruct((t, D), F32), jax.ShapeDtypeStruct((8, D), F32)],
        scratch_shapes=[pltpu.VMEM((tm, D), F32)],
        compiler_params=_cparams(("arbitrary", "arbitrary")),
    )(*pieces, wperm, x, lnw, dyres)


def _qkv_fwd(p, convw, kind):
    t = p.shape[0]
    f = _qkv_act(kind)

    def body(p_ref, w_ref, o_ref):
        o_ref[...] = f(p_ref[...], w_ref[...])

    return pl.pallas_call(
        body, name=f"qkv_fwd{kind}", grid=(NA,),
        in_specs=[pl.BlockSpec((t, DA), lambda h: (0, kind * NA + h)),
                  pl.BlockSpec((8, DA), lambda h: (0, kind * NA + h))],
        out_specs=pl.BlockSpec((t, DA), lambda h: (0, h)),
        out_shape=jax.ShapeDtypeStruct((t, D), F32),
        compiler_params=_cparams(("parallel",)),
    )(p, convw)


def _qkv_bwd(p, convw, dout, kind):
    t = p.shape[0]
    f = _qkv_act(kind)

    def body(p_ref, w_ref, g_ref, dx_ref, dw_ref):
        _, vjp = jax.vjp(f, p_ref[...], w_ref[...])
        dx, dw = vjp(g_ref[...])
        dx_ref[...] = dx.astype(BF16)
        dw_ref[...] = dw

    return pl.pallas_call(
        body, name=f"qkv_bwd{kind}", grid=(NA,),
        in_specs=[pl.BlockSpec((t, DA), lambda h: (0, kind * NA + h)),
                  pl.BlockSpec((8, DA), lambda h: (0, kind * NA + h)),
                  pl.BlockSpec((t, DA), lambda h: (0, h))],
        out_specs=[pl.BlockSpec((t, DA), lambda h: (0, h)),
                   pl.BlockSpec((8, DA), lambda h: (0, h))],
        out_shape=[jax.ShapeDtypeStruct((t, D), BF16), jax.ShapeDtypeStruct((8, D), F32)],
        compiler_params=_cparams(("parallel",)),
    )(p, convw, dout)


def _gates_f(ps, alog_row, dt_row, w2f, b2f, w2b, b2b):
    lane = lax.broadcasted_iota(jnp.int32, ps.shape, 1)
    lg = -jnp.exp(alog_row) * _softplus(ps + dt_row)
    gsm = jnp.where(lane < 16, lg, jnp.where(lane < 32, _sigmoid(ps), 0.0))
    gkf = -_softplus(-(mm(ps, w2f) + b2f)) * (1.0 / 16.0)
    gkb = -_softplus(-(mm(ps, w2b) + b2b)) * (1.0 / 16.0)
    return gsm, gkf, gkb


def _gates_fwd(ps, alog_row, dt_row, w2f, b2f, w2b, b2b, tm=512):
    t = ps.shape[0]

    def body(ps_ref, a_ref, d_ref, wf_ref, bf_ref, wb_ref, bb_ref, gsm_ref, gk_ref):
        gsm, gkf, gkb = _gates_f(ps_ref[...], a_ref[...], d_ref[...], wf_ref[...], bf_ref[...],
                                 wb_ref[...], bb_ref[...])
        gsm_ref[...] = gsm
        gk_ref[0] = gkf
        gk_ref[1] = gkb

    row = lambda n: pl.BlockSpec((1, n), lambda i: (0, 0))
    mat = pl.BlockSpec((128, 512), lambda i: (0, 0))
    return pl.pallas_call(
        body, name="gates_fwd", grid=(t // tm,),
        in_specs=[pl.BlockSpec((tm, 128), lambda i: (i, PS_BLOCK)), row(128), row(128), mat, row(512), mat, row(512)],
        out_specs=[pl.BlockSpec((tm, 128), lambda i: (i, 0)),
                   pl.BlockSpec((2, tm, 512), lambda i: (0, i, 0))],
        out_shape=[jax.ShapeDtypeStruct((t, 128), F32), jax.ShapeDtypeStruct((2, t, 512), F32)],
        compiler_params=_cparams(("parallel",)),
    )(ps, alog_row, dt_row, w2f, b2f, w2b, b2b)


def _gates_bwd(ps, alog_row, dt_row, w2f, b2f, w2b, b2b, dgsm, dgk, tm=512):
    t = ps.shape[0]

    def body(ps_ref, a_ref, d_ref, wf_ref, bf_ref, wb_ref, bb_ref, dgsm_ref, dgk_ref,
             dps_ref, da_ref, dd_ref, dwf_ref, dbf_ref, dwb_ref, dbb_ref):
        _, vjp = jax.vjp(_gates_f, ps_ref[...], a_ref[...], d_ref[...], wf_ref[...], bf_ref[...],
                         wb_ref[...], bb_ref[...])
        dps, da, dd, dwf, dbf, dwb, dbb = vjp((dgsm_ref[...], dgk_ref[0], dgk_ref[1]))
        dps_ref[:, 0:128] = dps.astype(BF16)
        dps_ref[:, 128:DP_TILE] = jnp.zeros((tm, DP_TILE - 128), BF16)
        accs = ((da_ref, da), (dd_ref, dd), (dwf_ref, dwf), (dbf_ref, dbf), (dwb_ref, dwb), (dbb_ref, dbb))

        @pl.when(pl.program_id(0) == 0)
        def _():
            for ref, _ in accs:
                ref[...] = jnp.zeros_like(ref)
        for ref, val in accs:
            ref[...] += jnp.broadcast_to(val, ref.shape)

    row = lambda n: pl.BlockSpec((1, n), lambda i: (0, 0))
    row8 = lambda n: pl.BlockSpec((8, n), lambda i: (0, 0))
    mat = pl.BlockSpec((128, 512), lambda i: (0, 0))
    return pl.pallas_call(
        body, name="gates_bwd", grid=(t // tm,),
        in_specs=[pl.BlockSpec((tm, 128), lambda i: (i, PS_BLOCK)), row(128), row(128), mat, row(512), mat, row(512),
                  pl.BlockSpec((tm, 128), lambda i: (i, 0)),
                  pl.BlockSpec((2, tm, 512), lambda i: (0, i, 0))],
        out_specs=[pl.BlockSpec((tm, DP_TILE), lambda i: (i, 0)), row8(128), row8(128), mat, row8(512), mat,
                   row8(512)],
        out_shape=[jax.ShapeDtypeStruct((t, DP_TILE), BF16),
                   jax.ShapeDtypeStruct((8, 128), F32), jax.ShapeDtypeStruct((8, 128), F32),
                   jax.ShapeDtypeStruct((128, 512), F32), jax.ShapeDtypeStruct((8, 512), F32),
                   jax.ShapeDtypeStruct((128, 512), F32), jax.ShapeDtypeStruct((8, 512), F32)],
        compiler_params=_cparams(("arbitrary",)),
    )(ps, alog_row, dt_row, w2f, b2f, w2b, b2b, dgsm, dgk)


def _rows(i):
    return pl.ds(pl.multiple_of(i * CH, CH), CH)


def _srows(i):
    return pl.ds(pl.multiple_of(i * SC, SC), SC)


def _first_row(x):
    row = lax.broadcasted_iota(jnp.int32, (8, x.shape[1]), 0)
    return jnp.where(row == 0, jnp.broadcast_to(x, (8, x.shape[1])), 0.0)


def _chunk_rows(e_ref, i):
    pad = jnp.zeros((CH - 8, 128), F32)
    return jnp.concatenate([x for c in range(SC // CH) for x in (e_ref[(SC // CH) * i + c], pad)], axis=0)


def _gcum_f(gsm, tm):
    i = lax.broadcasted_iota(jnp.int32, (tm, tm), 0)
    j = lax.broadcasted_iota(jnp.int32, (tm, tm), 1)
    same = (i >> 6) == (j >> 6)
    lower = (same & (i >= j)).astype(F32)
    upper = (same & (i <= j)).astype(F32)
    r = lax.broadcasted_iota(jnp.int32, (128, D), 0)
    head = lax.broadcasted_iota(jnp.int32, (128, D), 1) >> 7
    pick = lambda off: (r == head + off).astype(F32)
    lane = lax.broadcasted_iota(jnp.int32, gsm.shape, 1)
    run = jnp.where(lane < 8, cmm(lower, gsm), cmm(upper, gsm))
    return mmc(run, pick(0)), mmc(run, pick(8)), mmc(gsm, pick(16)), mmc(gsm, pick(24))


def _gcum_fwd(gsm, tm=256):
    t = gsm.shape[0]

    def body(s_ref, g_ref, b_ref):
        gf, gb, bf, bb = _gcum_f(s_ref[...], tm)
        g_ref[0] = gf
        g_ref[1] = gb
        b_ref[0] = bf
        b_ref[1] = bb

    two = pl.BlockSpec((2, tm, D), lambda i: (0, i, 0))
    return pl.pallas_call(
        body, name="gcum_fwd", grid=(t // tm,),
        in_specs=[pl.BlockSpec((tm, 128), lambda i: (i, 0))], out_specs=[two, two],
        out_shape=[jax.ShapeDtypeStruct((2, t, D), F32)] * 2,
        compiler_params=_cparams(("parallel",)),
    )(gsm)


def _gcum_bwd(gsm, dg2, db2, tm=256):
    t = gsm.shape[0]

    def body(s_ref, dg_ref, db_ref, ds_ref):
        _, vjp = jax.vjp(lambda s: _gcum_f(s, tm), s_ref[...])
        ds_ref[...] = vjp((dg_ref[0], dg_ref[1], db_ref[0], db_ref[1]))[0]

    two = pl.BlockSpec((2, tm, D), lambda i: (0, i, 0))
    tile = pl.BlockSpec((tm, 128), lambda i: (i, 0))
    return pl.pallas_call(
        body, name="gcum_bwd", grid=(t // tm,),
        in_specs=[tile, two, two], out_specs=tile,
        out_shape=jax.ShapeDtypeStruct((t, 128), F32),
        compiler_params=_cparams(("parallel",)),
    )(gsm, dg2, db2)


def _gdn_intra_fwd(qn, kn, vc, g2, b2):
    t = qn.shape[0]
    n = t // CH

    def body(q_ref, k_ref, v_ref, g_ref, b_ref, u_ref, w_ref, a_ref, qd_ref, kd_ref, e_ref, t_ref):
        cs = _Consts(pl.program_id(0) == 1)

        def step(i, carry):
            r = _srows(i)
            q, k, v, g, bx = q_ref[r, :], k_ref[r, :], v_ref[r, :], g_ref[r, :], b_ref[r, :]
            u, w, a, qd, kd, el, tinv = _gdn_intra(q, k, v, g, bx, None, cs)
            u_ref[r, :] = u
            w_ref[r, :] = w
            a_ref[r, :] = a
            qd_ref[r, :] = qd
            kd_ref[r, :] = kd
            t_ref[r, :] = tinv
            for c in range(SC // CH):
                e_ref[(SC // CH) * i + c] = el[c * CH:c * CH + 8]
            return carry

        lax.fori_loop(0, t // SC, step, 0)

    head = pl.BlockSpec((t, DA), lambda d, h: (0, h))
    dh = pl.BlockSpec((None, t, DA), lambda d, h: (d, 0, h))
    sq = lambda w: pl.BlockSpec((None, None, t, w), lambda d, h: (d, h, 0, 0))
    big = jax.ShapeDtypeStruct((2, t, D), F32)
    return pl.pallas_call(
        body, name="gdn_intra_fwd", grid=(2, NA),
        in_specs=[head, head, head, dh, dh],
        out_specs=[dh, dh, sq(CH), dh, dh, pl.BlockSpec((None, None, n, 8, 128), lambda d, h: (d, h, 0, 0, 0)),
                   sq(SC)],
        out_shape=[big, big, jax.ShapeDtypeStruct((2, NA, t, CH), F32), big, big,
                   jax.ShapeDtypeStruct((2, NA, n, 8, 128), F32), jax.ShapeDtypeStruct((2, NA, t, SC), F32)],
        compiler_params=_cparams(("parallel", "parallel")),
    )(qn, kn, vc, g2, b2)


SCAN_TB = 256
SCAN_HB = 8


def _scan_specs(t, width, nheads, hb, along):
    nt = t // SCAN_TB
    nb = SCAN_TB // CH

    def tmap(d, tt):
        fwd = tt + d * (nt - 1 - 2 * tt)
        return fwd if along > 0 else nt - 1 - fwd

    tok = pl.BlockSpec((None, SCAN_TB, hb * width), lambda d, h, tt: (d, tmap(d, tt), h))
    per = lambda *tail: pl.BlockSpec((None, hb, nb) + tail, lambda d, h, tt: (d, h, tmap(d, tt)) + (0,) * len(tail))
    sq = pl.BlockSpec((None, hb, SCAN_TB, CH), lambda d, h, tt: (d, h, tmap(d, tt), 0))
    shared = lambda w: pl.BlockSpec((SCAN_TB, hb * w), lambda d, h, tt: (tmap(d, tt), h))
    return tok, per, sq, shared, (2, nheads // hb, nt), nb


def _gdn_scan_fwd(u, w, a, qd, kd, e):
    t = u.shape[1]
    tok, per, sq, _, grid, nb = _scan_specs(t, DA, NA, SCAN_HB, +1)

    def body(u_ref, w_ref, a_ref, qd_ref, kd_ref, e_ref, o_ref, s_ref, state):
        rev = pl.program_id(0) == 1

        @pl.when(pl.program_id(2) == 0)
        def _():
            state[...] = jnp.zeros_like(state)

        def step(i, ss):
            ci = jnp.where(rev, nb - 1 - i, i)
            r = _rows(ci)
            out = []
            for hh, s in enumerate(ss):
                c = slice(hh * DA, (hh + 1) * DA)
                s_ref[hh, ci] = s
                o, sn = _gdn_scan(u_ref[r, c], w_ref[r, c], a_ref[hh, r, :], qd_ref[r, c], kd_ref[r, c],
                                  e_ref[hh, ci][0:1], s)
                o_ref[r, c] = o
                out.append(sn)
            return tuple(out)

        ss = lax.fori_loop(0, nb, step, tuple(state[hh] for hh in range(SCAN_HB)))
        for hh, s in enumerate(ss):
            state[hh] = s

    return pl.pallas_call(
        body, name="gdn_scan_fwd", grid=grid,
        in_specs=[tok, tok, sq, tok, tok, per(8, 128)],
        out_specs=[tok, per(DA, DA)],
        out_shape=[jax.ShapeDtypeStruct((2, t, D), F32), jax.ShapeDtypeStruct((2, NA, t // CH, DA, DA), F32)],
        scratch_shapes=[pltpu.VMEM((SCAN_HB, DA, DA), F32)],
        compiler_params=_cparams(("parallel", "parallel", "arbitrary")),
    )(u, w, a, qd, kd, e)


def _gdn_scan_bwd(u, w, a, qd, kd, e, ssave, do):
    t = u.shape[1]
    tok, per, sq, shared, grid, nb = _scan_specs(t, DA, NA, SCAN_HB, -1)

    def body(u_ref, w_ref, a_ref, qd_ref, kd_ref, e_ref, s_ref, do_ref,
             du_ref, dw_ref, da_ref, dqd_ref, dkd_ref, de_ref, state):
        rev = pl.program_id(0) == 1

        @pl.when(pl.program_id(2) == 0)
        def _():
            state[...] = jnp.zeros_like(state)

        def step(i, dss):
            ci = jnp.where(rev, i, nb - 1 - i)
            r = _rows(ci)
            out = []
            for hh, ds in enumerate(dss):
                c = slice(hh * DA, (hh + 1) * DA)
                _, vjp = jax.vjp(_gdn_scan, u_ref[r, c], w_ref[r, c], a_ref[hh, r, :], qd_ref[r, c], kd_ref[r, c],
                                 e_ref[hh, ci][0:1], s_ref[hh, ci])
                du, dw, da, dqd, dkd, de, dsn = vjp((do_ref[r, c], ds))
                du_ref[r, c] = du
                dw_ref[r, c] = dw
                da_ref[hh, r, :] = da
                dqd_ref[r, c] = dqd
                dkd_ref[r, c] = dkd
                de_ref[hh, ci] = _first_row(de)
                out.append(dsn)
            return tuple(out)

        dss = lax.fori_loop(0, nb, step, tuple(state[hh] for hh in range(SCAN_HB)))
        for hh, ds in enumerate(dss):
            state[hh] = ds

    big = jax.ShapeDtypeStruct((2, t, D), F32)
    return pl.pallas_call(
        body, name="gdn_scan_bwd", grid=grid,
        in_specs=[tok, tok, sq, tok, tok, per(8, 128), per(DA, DA), shared(DA)],
        out_specs=[tok, tok, sq, tok, tok, per(8, 128)],
        out_shape=[big, big, jax.ShapeDtypeStruct((2, NA, t, CH), F32), big, big,
                   jax.ShapeDtypeStruct((2, NA, t // CH, 8, 128), F32)],
        scratch_shapes=[pltpu.VMEM((SCAN_HB, DA, DA), F32)],
        compiler_params=_cparams(("parallel", "parallel", "arbitrary")),
    )(u, w, a, qd, kd, e, ssave, do)


def _gdn_intra_bwd(qn, kn, vc, g2, b2, tinv, du, dw, da, dqd, dkd, de):
    t = qn.shape[0]
    n = t // CH

    def body(q_ref, k_ref, v_ref, g_ref, b_ref, t_ref, du_ref, dw_ref, da_ref, dqd_ref, dkd_ref, de_ref,
             dq_ref, dk_ref, dv_ref, dg_ref, db_ref):
        d = pl.program_id(1)
        cs = _Consts(d == 1)

        @pl.when(d == 0)
        def _():
            dq_ref[...] = jnp.zeros_like(dq_ref)
            dk_ref[...] = jnp.zeros_like(dk_ref)
            dv_ref[...] = jnp.zeros_like(dv_ref)

        def step(i, carry):
            r = _srows(i)
            tinv_c = t_ref[r, :]
            f = lambda q, k, v, g, bx: _gdn_intra(q, k, v, g, bx, tinv_c, cs)
            _, vjp = jax.vjp(f, q_ref[r, :], k_ref[r, :], v_ref[r, :], g_ref[r, :], b_ref[r, :])
            dq, dk, dv, dg, dbx = vjp((du_ref[r, :], dw_ref[r, :], da_ref[r, :], dqd_ref[r, :],
                                       dkd_ref[r, :], _chunk_rows(de_ref, i)))
            dq_ref[r, :] += dq
            dk_ref[r, :] += dk
            dv_ref[r, :] += dv
            dg_ref[r, :] = dg
            db_ref[r, :] = dbx
            return carry

        lax.fori_loop(0, t // SC, step, 0)

    head = pl.BlockSpec((t, DA), lambda h, d: (0, h))
    dh = pl.BlockSpec((None, t, DA), lambda h, d: (d, 0, h))
    sq = pl.BlockSpec((None, None, t, CH), lambda h, d: (d, h, 0, 0))
    tq = pl.BlockSpec((None, None, t, SC), lambda h, d: (d, h, 0, 0))
    full = jax.ShapeDtypeStruct((t, D), F32)
    big = jax.ShapeDtypeStruct((2, t, D), F32)
    return pl.pallas_call(
        body, name="gdn_intra_bwd", grid=(NA, 2),
        in_specs=[head, head, head, dh, dh, tq, dh, dh, sq, dh, dh,
                  pl.BlockSpec((None, None, n, 8, 128), lambda h, d: (d, h, 0, 0, 0))],
        out_specs=[head, head, head, dh, dh],
        out_shape=[full, full, full, big, big],
        compiler_params=_cparams(("arbitrary", "arbitrary")),
    )(qn, kn, vc, g2, b2, tinv, du, dw, da, dqd, dkd, de)


def _gla_specs(t, order):
    ix = (lambda d, h: (d, h)) if order == "dh" else (lambda h, d: (d, h))

    def mk(fn):
        return lambda a, b: fn(*ix(a, b))
    q = pl.BlockSpec((t, DKB), mk(lambda d, h: (0, 32 + h)))
    k = pl.BlockSpec((t, DKB), mk(lambda d, h: (0, 36 + h)))
    v = pl.BlockSpec((t, DVB), mk(lambda d, h: (0, 20 + h)))
    dk = pl.BlockSpec((None, t, DKB), mk(lambda d, h: (d, 0, h)))
    dv = pl.BlockSpec((None, t, DVB), mk(lambda d, h: (d, 0, h)))
    e = pl.BlockSpec((None, None, t // CH, 8, 128), mk(lambda d, h: (d, h, 0, 0, 0)))
    s = pl.BlockSpec((None, None, t // CH, DVB, DKB), mk(lambda d, h: (d, h, 0, 0, 0)))
    return q, k, v, dk, dv, e, s


def _gla_intra_fwd(p, gk):
    t = p.shape[0]
    n = t // CH

    def body(q_ref, k_ref, v_ref, g_ref, qg_ref, kd_ref, in_ref, e_ref):
        cs = _Consts(pl.program_id(0) == 1)

        def step(i, carry):
            r = _srows(i)
            qg, kd, intra, el = _gla_intra(q_ref[r, :], k_ref[r, :], v_ref[r, :], g_ref[r, :], cs)
            qg_ref[r, :] = qg
            kd_ref[r, :] = kd
            in_ref[r, :] = intra
            for c in range(SC // CH):
                e_ref[(SC // CH) * i + c] = el[c * CH:c * CH + 8]
            return carry

        lax.fori_loop(0, t // SC, step, 0)

    q, k, v, dk, dv, e, _ = _gla_specs(t, "dh")
    return pl.pallas_call(
        body, name="gla_intra_fwd", grid=(2, NB),
        in_specs=[q, k, v, dk], out_specs=[dk, dk, dv, e],
        out_shape=[jax.ShapeDtypeStruct((2, t, NB * DKB), F32), jax.ShapeDtypeStruct((2, t, NB * DKB), F32),
                   jax.ShapeDtypeStruct((2, t, D), F32), jax.ShapeDtypeStruct((2, NB, n, 8, 128), F32)],
        compiler_params=_cparams(("parallel", "parallel")),
    )(p, p, p, gk)


GLA_HB = 4


def _gla_v_spec(t, along):
    nt = t // SCAN_TB

    def tmap(d, tt):
        fwd = tt + d * (nt - 1 - 2 * tt)
        return fwd if along > 0 else nt - 1 - fwd

    return pl.BlockSpec((SCAN_TB, GLA_HB * DVB), lambda d, h, tt: (tmap(d, tt), 5120 // (GLA_HB * DVB) + h))


def _gla_scan_fwd(p, qg, kd, intra, e):
    t = p.shape[0]
    tokk, per, _, _, grid, nb = _scan_specs(t, DKB, NB, GLA_HB, +1)
    tokv = _scan_specs(t, DVB, NB, GLA_HB, +1)[0]

    def body(v_ref, qg_ref, kd_ref, in_ref, e_ref, o_ref, s_ref, state):
        rev = pl.program_id(0) == 1

        @pl.when(pl.program_id(2) == 0)
        def _():
            state[...] = jnp.zeros_like(state)

        def step(i, sts):
            ci = jnp.where(rev, nb - 1 - i, i)
            r = _rows(ci)
            out = []
            for hh, st in enumerate(sts):
                ck = slice(hh * DKB, (hh + 1) * DKB)
                cv = slice(hh * DVB, (hh + 1) * DVB)
                s_ref[hh, ci] = st
                o, stn = _gla_scan(qg_ref[r, ck], kd_ref[r, ck], v_ref[r, cv], e_ref[hh, ci][0:1], st)
                o_ref[r, cv] = o + in_ref[r, cv]
                out.append(stn)
            return tuple(out)

        sts = lax.fori_loop(0, nb, step, tuple(state[hh] for hh in range(GLA_HB)))
        for hh, st in enumerate(sts):
            state[hh] = st

    return pl.pallas_call(
        body, name="gla_scan_fwd", grid=grid,
        in_specs=[_gla_v_spec(t, +1), tokk, tokk, tokv, per(8, 128)], out_specs=[tokv, per(DVB, DKB)],
        out_shape=[jax.ShapeDtypeStruct((2, t, D), F32), jax.ShapeDtypeStruct((2, NB, t // CH, DVB, DKB), F32)],
        scratch_shapes=[pltpu.VMEM((GLA_HB, DVB, DKB), F32)],
        compiler_params=_cparams(("parallel", "parallel", "arbitrary")),
    )(p, qg, kd, intra, e)


def _gla_scan_bwd(p, qg, kd, e, ssave, do):
    t = p.shape[0]
    tokk, per, _, shared, grid, nb = _scan_specs(t, DKB, NB, GLA_HB, -1)
    tokv = _scan_specs(t, DVB, NB, GLA_HB, -1)[0]

    def body(v_ref, qg_ref, kd_ref, e_ref, s_ref, do_ref, dqg_ref, dkd_ref, dv_ref, de_ref, state):
        rev = pl.program_id(0) == 1

        @pl.when(pl.program_id(2) == 0)
        def _():
            state[...] = jnp.zeros_like(state)

        def step(i, dsts):
            ci = jnp.where(rev, i, nb - 1 - i)
            r = _rows(ci)
            out = []
            for hh, dst in enumerate(dsts):
                ck = slice(hh * DKB, (hh + 1) * DKB)
                cv = slice(hh * DVB, (hh + 1) * DVB)
                _, vjp = jax.vjp(_gla_scan, qg_ref[r, ck], kd_ref[r, ck], v_ref[r, cv], e_ref[hh, ci][0:1],
                                 s_ref[hh, ci])
                dqg, dkd, dv, de, dstn = vjp((do_ref[r, cv], dst))
                dqg_ref[r, ck] = dqg
                dkd_ref[r, ck] = dkd
                dv_ref[r, cv] = dv
                de_ref[hh, ci] = _first_row(de)
                out.append(dstn)
            return tuple(out)

        dsts = lax.fori_loop(0, nb, step, tuple(state[hh] for hh in range(GLA_HB)))
        for hh, dst in enumerate(dsts):
            state[hh] = dst

    return pl.pallas_call(
        body, name="gla_scan_bwd", grid=grid,
        in_specs=[_gla_v_spec(t, -1), tokk, tokk, per(8, 128), per(DVB, DKB), shared(DVB)],
        out_specs=[tokk, tokk, tokv, per(8, 128)],
        out_shape=[jax.ShapeDtypeStruct((2, t, NB * DKB), F32), jax.ShapeDtypeStruct((2, t, NB * DKB), F32),
                   jax.ShapeDtypeStruct((2, t, D), F32), jax.ShapeDtypeStruct((2, NB, t // CH, 8, 128), F32)],
        scratch_shapes=[pltpu.VMEM((GLA_HB, DVB, DKB), F32)],
        compiler_params=_cparams(("parallel", "parallel", "arbitrary")),
    )(p, qg, kd, e, ssave, do)


def _gla_intra_bwd(p, gk, dqg, dkd, dvs, de, do):
    t = p.shape[0]
    n = t // CH

    def body(q_ref, k_ref, v_ref, g_ref, dqg_ref, dkd_ref, dvs_ref, de_ref, do_ref,
             dq_ref, dk_ref, dv_ref, dg_ref):
        d = pl.program_id(1)
        cs = _Consts(d == 1)

        @pl.when(d == 0)
        def _():
            dq_ref[...] = jnp.zeros_like(dq_ref)
            dk_ref[...] = jnp.zeros_like(dk_ref)
            dv_ref[...] = jnp.zeros_like(dv_ref)

        def step(i, carry):
            r = _srows(i)
            f = lambda q, k, v, g: _gla_intra(q, k, v, g, cs)
            _, vjp = jax.vjp(f, q_ref[r, :], k_ref[r, :], v_ref[r, :], g_ref[r, :])
            dq, dk, dv, dg = vjp((dqg_ref[r, :], dkd_ref[r, :], do_ref[r, :], _chunk_rows(de_ref, i)))
            dq_ref[r, :] += dq
            dk_ref[r, :] += dk
            dv_ref[r, :] += dv + dvs_ref[r, :]
            dg_ref[r, :] = dg
            return carry

        lax.fori_loop(0, t // SC, step, 0)

    q, k, v, dk, dv, e_s, _ = _gla_specs(t, "hd")
    hk = pl.BlockSpec((t, DKB), lambda h, d: (0, h))
    hv = pl.BlockSpec((t, DVB), lambda h, d: (0, h))
    return pl.pallas_call(
        body, name="gla_intra_bwd", grid=(NB, 2),
        in_specs=[q, k, v, dk, dk, dk, dv, e_s, hv],
        out_specs=[hk, hk, hv, dk],
        out_shape=[jax.ShapeDtypeStruct((t, NB * DKB), F32), jax.ShapeDtypeStruct((t, NB * DKB), F32),
                   jax.ShapeDtypeStruct((t, D), F32), jax.ShapeDtypeStruct((2, t, NB * DKB), F32)],
        compiler_params=_cparams(("arbitrary", "arbitrary")),
    )(p, p, p, gk, dqg, dkd, dvs, de, do)


def _seg_gate(o, z, w):
    return _rms(o, w) * _silu(z)


def _seg_merge(ya, yb, ga, gb):
    return _sigmoid(ga) * ya + _sigmoid(gb) * yb


def _seg_loss(out, x, tgt, w):
    err = x + _rms(out, w) - tgt
    return 0.5 * jnp.sum(jnp.mean(err * err, axis=-1, keepdims=True), axis=0, keepdims=True)


def _post(oa2, ob2, p, x, tgt, gdn_w, gla_w, lnpost, w3, tm=128):
    t = x.shape[0]

    def body(oa_ref, ob_ref, z_ref, gb_ref, ga_ref, gB_ref, x_ref, t_ref, aw_ref, bw_ref, lw_ref, w_ref,
             loss_ref, doa_ref, dob_ref, dz_ref, dgb_ref, dga_ref, dgB_ref, dy_ref,
             dw_ref, daw_ref, dbw_ref, dlw_ref):
        first = pl.program_id(0) == 0
        oa = oa_ref[0] + oa_ref[1]
        ob = ob_ref[0] + ob_ref[1]
        z, gb = z_ref[...], gb_ref[...]
        aw, bw = aw_ref[...], bw_ref[...]
        rs = D // NSHARD

        def mat(a, m):
            return sum(jnp.dot(a[:, s * rs:(s + 1) * rs], w_ref[s, m], preferred_element_type=F32)
                       for s in range(NSHARD))

        def mat_t(g, m):
            return jnp.concatenate([_bdot(g, w_ref[s, m], 1, 1) for s in range(NSHARD)], axis=1)

        def add_dw(a, g, m):
            for s in range(NSHARD):
                dw_ref[s, m] += _bdot(a[:, s * rs:(s + 1) * rs], g, 0, 0)

        pa = [jax.vjp(_seg_gate, oa[:, h * DA:(h + 1) * DA], z[:, h * DA:(h + 1) * DA], aw) for h in range(NA)]
        pb = [jax.vjp(_seg_gate, ob[:, h * DVB:(h + 1) * DVB], gb[:, h * DVB:(h + 1) * DVB], bw)
              for h in range(NB)]
        a1 = jnp.concatenate([v for v, _ in pa], axis=1).astype(BF16)
        a2 = jnp.concatenate([v for v, _ in pb], axis=1).astype(BF16)
        ya = mat(a1, 0)
        yb = mat(a2, 1)
        merged, vjp_m = jax.vjp(_seg_merge, ya, yb, ga_ref[...], gB_ref[...])
        mb = merged.astype(BF16)
        out = mat(mb, 2)
        loss, vjp_l = jax.vjp(_seg_loss, out, x_ref[...], t_ref[...], lw_ref[...])
        dout, dyres, _, dlw = vjp_l(jnp.ones((1, 1), F32))
        dy_ref[...] = dyres
        doutb = dout.astype(BF16)
        dmerged = mat_t(doutb, 2)
        dya, dyb, dga, dgB = vjp_m(dmerged)
        dga_ref[...] = dga.astype(BF16)
        dgB_ref[...] = dgB.astype(BF16)
        dyab, dybb = dya.astype(BF16), dyb.astype(BF16)
        da1 = mat_t(dyab, 0)
        da2 = mat_t(dybb, 1)

        daw = jnp.zeros_like(aw)
        for h in range(NA):
            sl = slice(h * DA, (h + 1) * DA)
            do, dz, dw = pa[h][1](da1[:, sl])
            doa_ref[:, sl] = do
            dz_ref[:, sl] = dz.astype(BF16)
            daw = daw + dw
        dbw = jnp.zeros_like(bw)
        for h in range(NB):
            sl = slice(h * DVB, (h + 1) * DVB)
            do, dg, dw = pb[h][1](da2[:, sl])
            dob_ref[:, sl] = do
            dgb_ref[:, sl] = dg.astype(BF16)
            dbw = dbw + dw

        @pl.when(first)
        def _():
            loss_ref[...] = jnp.zeros_like(loss_ref)
            dw_ref[...] = jnp.zeros_like(dw_ref)
            daw_ref[...] = jnp.zeros_like(daw_ref)
            dbw_ref[...] = jnp.zeros_like(dbw_ref)
            dlw_ref[...] = jnp.zeros_like(dlw_ref)

        loss_ref[...] += jnp.broadcast_to(loss, loss_ref.shape)
        add_dw(a1, dyab, 0)
        add_dw(a2, dybb, 1)
        add_dw(mb, doutb, 2)
        daw_ref[...] += jnp.broadcast_to(daw, daw_ref.shape)
        dbw_ref[...] += jnp.broadcast_to(dbw, dbw_ref.shape)
        dlw_ref[...] += jnp.broadcast_to(dlw, dlw_ref.shape)

    two = pl.BlockSpec((2, tm, D), lambda i: (0, i, 0))
    pcol = lambda c: pl.BlockSpec((tm, D), lambda i: (i, c))
    tok = pl.BlockSpec((tm, D), lambda i: (i, 0))
    row = lambda n: pl.BlockSpec((1, n), lambda i: (0, 0))
    row8 = lambda n: pl.BlockSpec((8, n), lambda i: (0, 0))
    once = pl.Buffered(1)
    tokf = jax.ShapeDtypeStruct((t, D), F32)
    tokb = jax.ShapeDtypeStruct((t, D), BF16)
    wspec = pl.BlockSpec((NSHARD, 3, D // NSHARD, D), lambda i: (0, 0, 0, 0), pipeline_mode=once)
    return pl.pallas_call(
        body, name="post", grid=(t // tm,),
        in_specs=[two, two, pcol(3), pcol(6), pcol(7), pcol(8), tok, tok, row(DA), row(DVB), row(D), wspec],
        out_specs=[row8(128), tok, tok, tok, tok, tok, tok, tok, wspec, row8(DA), row8(DVB), row8(D)],
        out_shape=[jax.ShapeDtypeStruct((8, 128), F32), tokf, tokf, tokb, tokb, tokb, tokb, tokf,
                   jax.ShapeDtypeStruct((NSHARD, 3, D // NSHARD, D), F32),
                   jax.ShapeDtypeStruct((8, DA), F32), jax.ShapeDtypeStruct((8, DVB), F32),
                   jax.ShapeDtypeStruct((8, D), F32)],
        compiler_params=_cparams(("arbitrary",), vmem_mb=56),
    )(oa2, ob2, p, p, p, p, x, tgt, gdn_w, gla_w, lnpost, w3)


def _adam_math(w, g, m, v):
    nm = B1 * m + (1.0 - B1) * g
    nv = B2 * v + (1.0 - B2) * (g * g)
    m_hat = nm / (1.0 - B1 ** STEP)
    v_hat = nv / (1.0 - B2 ** STEP)
    return -LR * (m_hat / (jnp.sqrt(v_hat) + ADAM_EPS) + WD * w), nm, nv


SMALL_SLOTS = (("ln_pre_w", 0, 1024, 0), ("a_log_fwd", 1024, 8, 0), ("a_log_bwd", 1024, 8, 8),
               ("dt_bias_fwd", 1152, 8, 0), ("dt_bias_bwd", 1152, 8, 8), ("gdn_norm_w", 1280, 128, 0),
               ("gk_b2_fwd", 1408, 512, 0), ("gk_b2_bwd", 1920, 512, 0), ("gla_norm_w", 2432, 256, 0),
               ("ln_post_w", 2688, 1024, 0))
SMALL_W = 3840


def _adam_small(gsum, ws, ms, vs):
    nw = len(SMALL_SLOTS)

    def body(g_ref, *refs):
        w_refs, m_refs, v_refs, outs = refs[0:nw], refs[nw:2 * nw], refs[2 * nw:3 * nw], refs[3 * nw:]
        for i, (_, off, n, shift) in enumerate(SMALL_SLOTS):
            slot = g_ref[0:1, off:off + max(n, 128)]
            if shift:
                slot = pltpu.roll(slot, 128 - shift, 1)
            g = slot[:, 0:n]
            d, nm, nv = _adam_math(w_refs[i][...], g, m_refs[i][...], v_refs[i][...])
            for k, val in enumerate((g, d, nm, nv)):
                outs[4 * i + k][...] = val

    vm = pl.BlockSpec(memory_space=pltpu.VMEM)
    res = pl.pallas_call(
        body, name="adam_small", in_specs=[vm] * (1 + 3 * nw), out_specs=[vm] * (4 * nw),
        out_shape=[jax.ShapeDtypeStruct((1, n), F32) for _, _, n, _ in SMALL_SLOTS for _ in range(4)],
    )(gsum, *ws, *ms, *vs)
    return {name: res[4 * i:4 * i + 4] for i, (name, _, _, _) in enumerate(SMALL_SLOTS)}


def _adam(w, mine, got, m, v, tr, tile0=0, name=""):
    rows, cols = w.shape
    nh = mine.shape[0] // tr

    def body(c_ref, w_ref, a_ref, b_ref, m_ref, v_ref, g_ref, d_ref, nm_ref, nv_ref):
        half = (tile0 + pl.program_id(0)) // nh
        g = jnp.where(half == c_ref[0], a_ref[...], b_ref[...])
        d, nm, nv = _adam_math(w_ref[...], g, m_ref[...], v_ref[...])
        g_ref[...] = g
        d_ref[...] = d
        nm_ref[...] = nm
        nv_ref[...] = nv

    blk = pl.BlockSpec((tr, cols), lambda i, cc: (i, 0))
    half = pl.BlockSpec((tr, cols), lambda i, cc: ((tile0 + i) % nh, 0))
    shp = jax.ShapeDtypeStruct((rows, cols), F32)
    return pl.pallas_call(
        body, name=f"adam_{name}{rows}x{cols}",
        grid_spec=pltpu.PrefetchScalarGridSpec(
            num_scalar_prefetch=1, grid=(rows // tr,),
            in_specs=[blk, half, half, blk, blk], out_specs=[blk] * 4),
        out_shape=[shp] * 4,
        compiler_params=_cparams(("parallel",)),
    )(lax.axis_index("c").reshape(1), w, mine, got, m, v)


def _adam_shard_small(mine, got, ws, ms, vs):
    def body(c_ref, a_ref, b_ref, *refs):
        w_refs, m_refs, v_refs, outs = refs[0:3], refs[3:6], refs[6:9], refs[9:]
        south = c_ref[0] == 0
        g = jnp.concatenate([jnp.where(south, a_ref[...], b_ref[...]),
                             jnp.where(south, b_ref[...], a_ref[...])], axis=0)
        grads = (g[0:5], g[8:24, 0:128], g[8:24, 128:256])
        for i, gi in enumerate(grads):
            d, nm, nv = _adam_math(w_refs[i][...], gi, m_refs[i][...], v_refs[i][...])
            for k, val in enumerate((gi, d, nm, nv)):
                outs[4 * i + k][...] = val

    vm = pl.BlockSpec(memory_space=pltpu.VMEM)
    res = pl.pallas_call(
        body, name="adam_shard_small",
        in_specs=[pl.BlockSpec(memory_space=pltpu.SMEM)] + [vm] * 11, out_specs=[vm] * 12,
        out_shape=[jax.ShapeDtypeStruct(w.shape, F32) for w in ws for _ in range(4)],
    )(lax.axis_index("c").reshape(1), mine, got, *ws, *ms, *vs)
    return [res[4 * i:4 * i + 4] for i in range(3)]


def _sum_cast(own, got):
    ns, _, r, c = own.shape
    tr = r // 4 if r % 64 == 0 else r

    def body(c_ref, a_ref, b_ref, f_ref, h_ref):
        s = a_ref[...] + b_ref[...]
        f_ref[...] = s
        h_ref[...] = s.astype(BF16)

    return pl.pallas_call(
        body, name=f"sum_cast_{r}x{c}",
        grid_spec=pltpu.PrefetchScalarGridSpec(
            num_scalar_prefetch=1, grid=(ns, r // tr),
            in_specs=[pl.BlockSpec((None, None, tr, c), lambda s, i, cc: (s, cc[0], i, 0)),
                      pl.BlockSpec((None, tr, c), lambda s, i, cc: (s, i, 0))],
            out_specs=[pl.BlockSpec((None, tr, c), lambda s, i, cc: (s, i, 0)),
                       pl.BlockSpec((None, tr, c), lambda s, i, cc: (s, i, 0))]),
        out_shape=[jax.ShapeDtypeStruct((ns, r, c), F32), jax.ShapeDtypeStruct((ns, r, c), BF16)],
        compiler_params=_cparams(("parallel", "parallel")),
    )(lax.axis_index("c").reshape(1), own, got)


def _sum4(mine, got):
    _, r, c = mine.shape
    tr = r // 4 if r % 64 == 0 else r

    def body(s_ref, a_ref, g_ref, o_ref):
        acc = a_ref[...] + g_ref[0].astype(F32)
        acc = acc + g_ref[1].astype(F32)
        o_ref[...] = acc + g_ref[2].astype(F32)

    shard = (2 * lax.axis_index("x") + lax.axis_index("y")).reshape(1)
    return pl.pallas_call(
        body, name=f"sum4_{r}x{c}",
        grid_spec=pltpu.PrefetchScalarGridSpec(
            num_scalar_prefetch=1, grid=(r // tr,),
            in_specs=[pl.BlockSpec((None, tr, c), lambda i, ss: (ss[0], i, 0)),
                      pl.BlockSpec((3, tr, c), lambda i, ss: (0, i, 0))],
            out_specs=pl.BlockSpec((tr, c), lambda i, ss: (i, 0))),
        out_shape=jax.ShapeDtypeStruct((r, c), F32),
        compiler_params=_cparams(("parallel",)),
    )(shard, mine, got)


def _place():
    x, y, c = lax.axis_index("x"), lax.axis_index("y"), lax.axis_index("c")
    chips = [(1 - x, y), (x, 1 - y), (1 - x, 1 - y)]
    return x, y, c, chips


def _gather_weights(parts):
    npart = len(parts)

    def body(*refs):
        ins, outs = refs[:npart], refs[npart:2 * npart]
        send_sems, recv_sems = refs[2 * npart:]
        x, y, c, chips = _place()
        sibling = (x, y, 1 - c)
        mine = 2 * x + y

        def remote(k, p, shard, half, to, src=None):
            dst = outs[p].at[shard, half]
            return pltpu.make_async_remote_copy(
                src_ref=dst if src is None else src, dst_ref=dst,
                send_sem=send_sems.at[k], recv_sem=recv_sems.at[k], device_id=to, device_id_type=MESH)

        first = [remote(j * npart + p, p, mine, c, (*chip, c), src=ins[p].at[c])
                 for j, chip in enumerate(chips) for p in range(npart)]
        for cp in first:
            cp.start()
        passed = []
        for j, (cx, cy) in enumerate(chips):
            for p in range(npart):
                remote(j * npart + p, p, 2 * cx + cy, c, (x, y, c)).wait_recv()
                fw = remote((3 + j) * npart + p, p, 2 * cx + cy, c, sibling)
                fw.start()
                passed.append(fw)
        for j, (cx, cy) in enumerate(chips):
            for p in range(npart):
                remote((3 + j) * npart + p, p, 2 * cx + cy, 1 - c, (x, y, c)).wait_recv()
        for cp in first + passed:
            cp.wait_send()

    got = pl.pallas_call(
        body, name="gather_weights",
        in_specs=[ANY] * npart, out_specs=[ANY] * npart,
        out_shape=[jax.ShapeDtypeStruct((NSHARD,) + a.shape, a.dtype) for a in parts],
        scratch_shapes=[pltpu.SemaphoreType.DMA((6 * npart,)), pltpu.SemaphoreType.DMA((6 * npart,))],
    )(*parts)
    mine = 2 * lax.axis_index("x") + lax.axis_index("y")
    return [lax.dynamic_update_index_in_dim(g, a, mine, 0) for g, a in zip(got, parts)]


def _swap_halves(parts, tag=""):
    npart = len(parts)

    def body(*refs):
        ins, outs = refs[:npart], refs[npart:2 * npart]
        send_sems, recv_sems = refs[2 * npart:]
        x, y, c, _ = _place()
        cps = [pltpu.make_async_remote_copy(
            src_ref=ins[p].at[s, 1 - c], dst_ref=outs[p].at[s],
            send_sem=send_sems.at[s * npart + p], recv_sem=recv_sems.at[s * npart + p],
            device_id=(x, y, 1 - c), device_id_type=MESH) for s in range(NSHARD) for p in range(npart)]
        for cp in cps:
            cp.start()
        for cp in cps:
            cp.wait()

    return pl.pallas_call(
        body, name="swap_halves" + tag, in_specs=[ANY] * npart, out_specs=[ANY] * npart,
        out_shape=[jax.ShapeDtypeStruct((NSHARD,) + a.shape[2:], a.dtype) for a in parts],
        scratch_shapes=[pltpu.SemaphoreType.DMA((NSHARD * npart,)), pltpu.SemaphoreType.DMA((NSHARD * npart,))],
    )(*parts)


def _scatter_shards(parts):
    npart = len(parts)

    def body(*refs):
        ins, outs = refs[:npart], refs[npart:2 * npart]
        send_sems, recv_sems = refs[2 * npart:]
        x, y, c, chips = _place()
        cps = [pltpu.make_async_remote_copy(
            src_ref=ins[p].at[2 * cx + cy], dst_ref=outs[p].at[j],
            send_sem=send_sems.at[j * npart + p], recv_sem=recv_sems.at[j * npart + p],
            device_id=(cx, cy, c), device_id_type=MESH)
            for j, (cx, cy) in enumerate(chips) for p in range(npart)]
        for cp in cps:
            cp.start()
        for cp in cps:
            cp.wait()

    return pl.pallas_call(
        body, name="scatter_shards", in_specs=[ANY] * npart, out_specs=[ANY] * npart,
        out_shape=[jax.ShapeDtypeStruct((3,) + a.shape[1:], a.dtype) for a in parts],
        scratch_shapes=[pltpu.SemaphoreType.DMA((3 * npart,)), pltpu.SemaphoreType.DMA((3 * npart,))],
    )(*parts)


HBM = pl.BlockSpec(memory_space=pltpu.HBM)
SEM = pl.BlockSpec(memory_space=pltpu.SEMAPHORE)
EFFECT = pltpu.SideEffectType.DATAFLOW_SIDE_EFFECTING


def _scatter_copies(srcs, lands, send_sems, recv_sems, waiting):
    x, y, c, chips = _place()
    n = len(srcs)
    return [pltpu.make_async_remote_copy(
        src_ref=srcs[p].at[2 * cx + cy], dst_ref=lands[p].at[j],
        send_sem=send_sems.at[j * n + p], recv_sem=recv_sems.at[j * n + p],
        device_id=(cx, cy, c), device_id_type=MESH)
        for j, (cx, cy) in enumerate(chips) for p in range(n)]


def _proj_copies(srcs, lands, send_sems, recv_sems, waiting):
    x, y, c, chips = _place()
    mine = 2 * x + y
    return [pltpu.make_async_remote_copy(
        src_ref=srcs[0].at[c], dst_ref=lands[0].at[mine, c],
        send_sem=send_sems.at[2 * j + to], recv_sem=recv_sems.at[2 * j + (to if waiting else c)],
        device_id=(cx, cy, to), device_id_type=MESH)
        for j, (cx, cy) in enumerate(chips) for to in range(2)]


def _start_copies(copies, nsem, parts, lands, name, after=None):
    n = len(parts)
    extra = [] if after is None else [after]

    def body(*refs):
        outs = refs[2 * n + len(extra):]
        for cp in copies(refs[:n], refs[n:2 * n], outs[0], outs[1], False):
            cp.start()
        outs[-1][...] = jnp.zeros_like(outs[-1])

    res = pl.pallas_call(
        body, name=name,
        out_shape=(pltpu.SemaphoreType.DMA((nsem,)), pltpu.SemaphoreType.DMA((nsem,)),
                   *[pltpu.HBM(a.shape, a.dtype) for a in parts], *[pltpu.HBM(a.shape, a.dtype) for a in lands],
                   jax.ShapeDtypeStruct((8, 128), F32)),
        in_specs=[HBM] * (2 * n) + [ANY] * len(extra),
        out_specs=(SEM, SEM, *[HBM] * (2 * n), pl.BlockSpec(memory_space=pltpu.VMEM)),
        input_output_aliases={i: 2 + i for i in range(2 * n)},
        compiler_params=pltpu.CompilerParams(has_side_effects=EFFECT),
    )(*[pltpu.with_memory_space_constraint(a, pltpu.HBM) for a in parts],
      *[pltpu.with_memory_space_constraint(a, pltpu.HBM) for a in lands], *extra)
    return res[0], res[1], res[2:2 + n], res[2 + n:2 + 2 * n], res[-1]


def _wait_copies(copies, started, after, name):
    send_sems, recv_sems, srcs, lands, _ = started
    n = len(srcs)

    def body(*refs):
        for cp in copies(refs[:n], refs[n:2 * n], refs[2 * n], refs[2 * n + 1], True):
            cp.wait_send()
            cp.wait_recv()

    res = pl.pallas_call(
        body, name=name,
        out_shape=tuple(pltpu.HBM(a.shape, a.dtype) for a in (*srcs, *lands)),
        in_specs=[HBM] * (2 * n) + [SEM, SEM, ANY], out_specs=tuple([HBM] * (2 * n)),
        input_output_aliases={i: i for i in range(2 * n)},
        compiler_params=pltpu.CompilerParams(has_side_effects=EFFECT),
    )(*srcs, *lands, send_sems, recv_sems, after)
    return res[n:]


def _join_halves(parts):
    npart = len(parts)

    def body(*refs):
        ins, outs = refs[:npart], refs[npart:2 * npart]
        send_sems, recv_sems = refs[2 * npart:]
        x, y, c, _ = _place()
        cps = [pltpu.make_async_remote_copy(
            src_ref=ins[p], dst_ref=outs[p], send_sem=send_sems.at[p], recv_sem=recv_sems.at[p],
            device_id=(x, y, 1 - c), device_id_type=MESH) for p in range(npart)]
        for cp in cps:
            cp.start()
        for cp in cps:
            cp.wait()

    return pl.pallas_call(
        body, name="join_halves", in_specs=[ANY] * npart, out_specs=[ANY] * npart,
        out_shape=[jax.ShapeDtypeStruct(a.shape, a.dtype) for a in parts],
        scratch_shapes=[pltpu.SemaphoreType.DMA((npart,)), pltpu.SemaphoreType.DMA((npart,))],
    )(*parts)


def _allreduce_small(v):
    r, ncol = v.shape

    def body(v_ref, o_ref, buf, send_sems, recv_sems):
        x, y, c, _ = _place()
        me = 4 * x + 2 * y + c
        buf[me] = v_ref[...]
        cps = []
        for k in range(1, 8):
            px, py, pc = x ^ (k >> 2), y ^ ((k >> 1) & 1), c ^ (k & 1)
            cps.append(pltpu.make_async_remote_copy(
                src_ref=v_ref, dst_ref=buf.at[me], send_sem=send_sems.at[k - 1], recv_sem=recv_sems.at[k - 1],
                device_id=(px, py, pc), device_id_type=MESH))
        for cp in cps:
            cp.start()
        for k in range(1, 8):
            px, py, pc = x ^ (k >> 2), y ^ ((k >> 1) & 1), c ^ (k & 1)
            pltpu.make_async_remote_copy(
                src_ref=v_ref, dst_ref=buf.at[4 * px + 2 * py + pc], send_sem=send_sems.at[k - 1],
                recv_sem=recv_sems.at[k - 1], device_id=(px, py, pc), device_id_type=MESH).wait_recv()
        for cp in cps:
            cp.wait_send()
        acc = buf[0]
        for d in range(1, 8):
            acc = acc + buf[d]
        o_ref[...] = acc

    return pl.pallas_call(
        body, name="allreduce_small",
        in_specs=[pl.BlockSpec(memory_space=pltpu.VMEM)], out_specs=pl.BlockSpec(memory_space=pltpu.VMEM),
        out_shape=jax.ShapeDtypeStruct((r, ncol), F32),
        scratch_shapes=[pltpu.VMEM((8, r, ncol), F32), pltpu.SemaphoreType.DMA((7,)), pltpu.SemaphoreType.DMA((7,))],
    )(v)


def _permute_rows(shards):
    w0, w1, w2, w3 = shards
    zeros = jnp.zeros((NPERM - 9280, w0.shape[1]), w0.dtype)
    return jnp.concatenate([w0, w1[0:1776], w1[1808:2320], w2, w3[0:240], w3[272:2320],
                            w1[1776:1808], w3[240:272], zeros], axis=0)


def _unpermute_rows(g):
    s1 = jnp.concatenate([g[2320:4096], g[9216:9248], g[4096:4608]], axis=0)
    s3 = jnp.concatenate([g[6928:7168], g[9248:9280], g[7168:9216]], axis=0)
    return jnp.stack([g[0:2320], s1, g[4608:6928], s3], axis=0)


def _pack_shard_small(conv, w2f, w2b):
    top = jnp.pad(conv, ((0, 8 - conv.shape[0]), (0, 0)))
    mid = jnp.pad(jnp.concatenate([w2f, w2b], axis=1), ((0, 0), (0, 768 - 256)))
    return jnp.concatenate([top, mid, jnp.zeros((8, 768), conv.dtype)], axis=0)


def _unpack_shard_small(a):
    return a[0:5], a[8:24, 0:128], a[8:24, 128:256]


def kernel(x, ln_pre_w, w_in, conv_w, a_log_fwd, a_log_bwd, dt_bias_fwd, dt_bias_bwd, gdn_norm_w, w_proj_gdn, gk_w2_fwd, gk_b2_fwd, gk_w2_bwd, gk_b2_bwd, gla_norm_w, w_proj_gla, w_out, ln_post_w, loss_target, m_ln_pre_w, m_w_in, m_conv_w, m_a_log_fwd, m_a_log_bwd, m_dt_bias_fwd, m_dt_bias_bwd, m_gdn_norm_w, m_w_proj_gdn, m_gk_w2_fwd, m_gk_b2_fwd, m_gk_w2_bwd, m_gk_b2_bwd, m_gla_norm_w, m_w_proj_gla, m_w_out, m_ln_post_w, v_ln_pre_w, v_w_in, v_conv_w, v_a_log_fwd, v_a_log_bwd, v_dt_bias_fwd, v_dt_bias_bwd, v_gdn_norm_w, v_w_proj_gdn, v_gk_w2_fwd, v_gk_b2_fwd, v_gk_w2_bwd, v_gk_b2_bwd, v_gla_norm_w, v_w_proj_gla, v_w_out, v_ln_post_w):
    t = x.shape[1]
    x2, tgt = x[0], loss_target[0]

    win_l = w_in[0].T.astype(BF16).reshape(2, SHW // 2, D)
    proj_l = jnp.concatenate([w_proj_gdn[0], w_proj_gla[0], w_out[0]], axis=0).astype(BF16).reshape(2, 384, D)
    small_l = _pack_shard_small(conv_w[0], gk_w2_fwd[0], gk_w2_bwd[0]).reshape(2, 16, 768)
    win_g, small_g = _gather_weights([win_l, small_l])
    proj_started = _start_copies(_proj_copies, 6, [proj_l], [lax.empty((NSHARD, 2, 384, D), BF16)],
                                 "gather_proj_start", after=small_g)
    wperm = _permute_rows(win_g.reshape(NSHARD, SHW, D))
    small_g = small_g.reshape(NSHARD, 32, 768)
    convw = small_g[:, 0:8, :].transpose(1, 0, 2).reshape(8, 3 * D)
    w2f = small_g[:, 8:24, 0:128].transpose(1, 0, 2).reshape(16, 512)
    w2b = small_g[:, 8:24, 128:256].transpose(1, 0, 2).reshape(16, 512)
    w2f_pad = jnp.pad(w2f, ((32, 80), (0, 0)))
    w2b_pad = jnp.pad(w2b, ((48, 64), (0, 0)))
    alog_row = jnp.pad(jnp.concatenate([a_log_fwd, a_log_bwd], axis=1), ((0, 0), (0, 112)))
    dt_row = jnp.pad(jnp.concatenate([dt_bias_fwd, dt_bias_bwd], axis=1), ((0, 0), (0, 112)))

    p, h = _inproj(x2, ln_pre_w + proj_started[4][0:1, 0:1], wperm)
    qn, kn, vc = (_qkv_fwd(p, convw, kind) for kind in range(3))
    gsm, gk = _gates_fwd(p, alog_row, dt_row, w2f_pad, gk_b2_fwd, w2b_pad, gk_b2_bwd)
    g2, b2 = _gcum_fwd(gsm)
    u, w, at, qd, kd, el, tinv = _gdn_intra_fwd(qn, kn, vc, g2, b2)
    oa2, sa = _gdn_scan_fwd(u, w, at, qd, kd, el)
    qg, kdb, intra, elb = _gla_intra_fwd(p, gk)
    ob2, sb = _gla_scan_fwd(p, qg, kdb, intra, elb)

    (proj_land,) = _wait_copies(_proj_copies, proj_started, ob2, "gather_proj_wait")
    mine = 2 * lax.axis_index("x") + lax.axis_index("y")
    w3 = lax.dynamic_update_index_in_dim(proj_land, proj_l, mine, 0).reshape(NSHARD, 3, D // NSHARD, D)
    (loss8, doa, dob, dz, dgb, dga, dgB, dyres, dw3, dgdn_w, dgla_w, dlnpost) = _post(
        oa2, ob2, p, x2, tgt, gdn_norm_w, gla_norm_w, ln_post_w, w3)

    g_proj = dw3.reshape(NSHARD, 2, 384, D)
    sum_proj = _sum_cast(g_proj, _swap_halves([g_proj], "_proj")[0])
    started_proj = _start_copies(_scatter_copies, 3, [sum_proj[1]], [lax.empty((3, 384, D), BF16)],
                                 "scatter_proj_start")
    du, dw, dat, dqd, dkd, del_ = _gdn_scan_bwd(u, w, at, qd, kd, el + started_proj[4][0, 0], sa, doa)
    dqn, dkn, dvc, dg2, db2 = _gdn_intra_bwd(qn, kn, vc, g2, b2, tinv, du, dw, dat, dqd, dkd, del_)
    dgsm = _gcum_bwd(gsm, dg2, db2)
    dqg, dkdb, dvs, delb = _gla_scan_bwd(p, qg, kdb, elb, sb, dob)
    dqb, dkb, dvb, dgk = _gla_intra_bwd(p, gk, dqg, dkdb, dvs, delb, dob)
    (dps, dalog8, ddt8, dw2f_pad, db2f8, dw2b_pad, db2b8) = _gates_bwd(
        p, alog_row, dt_row, w2f_pad, gk_b2_fwd, w2b_pad, gk_b2_bwd, dgsm, dgk)
    dpre, dconv = zip(*[_qkv_bwd(p, convw, g, kind) for kind, g in enumerate((dqn, dkn, dvc))])

    pieces = (jnp.concatenate([a.astype(BF16) for a in (*dpre, dz, dqb, dkb, dvb, dgb, dga, dgB, dps)], axis=1),)
    dwperm = _inproj_dw(h, pieces)

    g_in = _unpermute_rows(dwperm).reshape(NSHARD, 2, SHW // 2, D)
    dconv_full = jnp.concatenate(dconv, axis=1)
    dw2f, dw2b = dw2f_pad[32:48], dw2b_pad[48:64]
    g_small = jnp.stack([_pack_shard_small(dconv_full[0:5, 768 * s:768 * (s + 1)],
                                           dw2f[:, 128 * s:128 * (s + 1)], dw2b[:, 128 * s:128 * (s + 1)])
                         for s in range(NSHARD)])
    g_small = g_small.reshape(NSHARD, 2, 16, 768)
    parts = [g_in, g_small]
    got = _swap_halves(parts)
    sums = [_sum_cast(a, b) for a, b in zip(parts, got)]
    hbs = [hb for _, hb in sums]
    started = _start_copies(_scatter_copies, 3 * len(hbs), hbs,
                            [lax.empty((3,) + a.shape[1:], a.dtype) for a in hbs], "scatter_start")
    dx, dlnpre8 = _inproj_dx(pieces, wperm, x2, ln_pre_w + started[4][0:1, 0:1], dyres)

    gsmall = _allreduce_small(jnp.concatenate(
        [dlnpre8, dalog8, ddt8, dgdn_w, db2f8, db2b8, dgla_w, dlnpost, loss8], axis=1))
    smalls = dict(ln_pre_w=(ln_pre_w, m_ln_pre_w, v_ln_pre_w), a_log_fwd=(a_log_fwd, m_a_log_fwd, v_a_log_fwd),
                  a_log_bwd=(a_log_bwd, m_a_log_bwd, v_a_log_bwd),
                  dt_bias_fwd=(dt_bias_fwd, m_dt_bias_fwd, v_dt_bias_fwd),
                  dt_bias_bwd=(dt_bias_bwd, m_dt_bias_bwd, v_dt_bias_bwd),
                  gdn_norm_w=(gdn_norm_w, m_gdn_norm_w, v_gdn_norm_w),
                  gk_b2_fwd=(gk_b2_fwd, m_gk_b2_fwd, v_gk_b2_fwd), gk_b2_bwd=(gk_b2_bwd, m_gk_b2_bwd, v_gk_b2_bwd),
                  gla_norm_w=(gla_norm_w, m_gla_norm_w, v_gla_norm_w), ln_post_w=(ln_post_w, m_ln_post_w, v_ln_post_w))
    names = [name for name, _, _, _ in SMALL_SLOTS]
    small = _adam_small(gsmall, *([smalls[n][i] for n in names] for i in range(3)))

    landed_proj = _wait_copies(_scatter_copies, started_proj, small["ln_pre_w"][1], "scatter_proj_wait")
    landed = _wait_copies(_scatter_copies, started, small["ln_pre_w"][1], "scatter_wait")
    sums = [sums[0], sum_proj, sums[1]]
    halves = [_sum4(f, g) for (f, _), g in zip(sums, [landed[0], landed_proj[0], landed[1]])]
    theirs = _join_halves(halves)

    a_in = [a.T for a in _adam(w_in[0].T, halves[0], theirs[0], m_w_in[0].T, v_w_in[0].T, 232, name="in")]
    a_pr = [_adam(w[0], halves[1], theirs[1], m[0], v[0], 128, tile0=2 * i, name=f"proj{i}")
            for i, (w, m, v) in enumerate(((w_proj_gdn, m_w_proj_gdn, v_w_proj_gdn),
                                           (w_proj_gla, m_w_proj_gla, v_w_proj_gla), (w_out, m_w_out, v_w_out)))]
    a_ss = _adam_shard_small(halves[2], theirs[2], (conv_w[0], gk_w2_fwd[0], gk_w2_bwd[0]),
                             (m_conv_w[0], m_gk_w2_fwd[0], m_gk_w2_bwd[0]),
                             (v_conv_w[0], v_gk_w2_fwd[0], v_gk_w2_bwd[0]))

    def family(k):
        conv, w2f_, w2b_ = a_ss[0][k], a_ss[1][k], a_ss[2][k]
        s = {n: small[n][k] for n in names}
        return [s["ln_pre_w"], a_in[k][None], conv[None], s["a_log_fwd"], s["a_log_bwd"], s["dt_bias_fwd"],
                s["dt_bias_bwd"], s["gdn_norm_w"], a_pr[0][k][None], w2f_[None], s["gk_b2_fwd"], w2b_[None],
                s["gk_b2_bwd"], s["gla_norm_w"], a_pr[1][k][None], a_pr[2][k][None], s["ln_post_w"]]

    return (gsmall[0, SMALL_W - 128], dx[None], *family(0), *family(1), *family(2), *family(3))
```

```python
import functools

import jax
import jax.numpy as jnp
from jax import lax
from jax.experimental import pallas as pl
from jax.experimental.pallas import tpu as pltpu

F32 = jnp.float32
BF16 = jnp.bfloat16
HI = lax.Precision.HIGHEST
MESH = pl.DeviceIdType.MESH

D = 1024
CH = 64
EPS = 1e-6
NA, DA = 8, 128
NB, DKB, DVB = 4, 128, 256
NSHARD = 4
SHW = 2320
NPERM = 9728
PS_BLOCK = 72
LR, B1, B2, ADAM_EPS, WD, STEP = 0.001, 0.9, 0.999, 1e-08, 0.01, 10

ANY = pl.BlockSpec(memory_space=pl.ANY)


def _cparams(sem=None, vmem_mb=48):
    return pltpu.CompilerParams(dimension_semantics=sem, vmem_limit_bytes=vmem_mb << 20)


def _bdot(a, b, ca, cb):
    return lax.dot_general(a.astype(BF16), b.astype(BF16), (((ca,), (cb,)), ((), ())),
                           preferred_element_type=F32)


@jax.custom_vjp
def mm(a, b):
    return _bdot(a, b, 1, 0)


def _mm_fwd(a, b):
    return _bdot(a, b, 1, 0), (a, b)


def _mm_bwd(res, g):
    a, b = res
    return _bdot(g, b, 1, 1), _bdot(a, g, 0, 0)


mm.defvjp(_mm_fwd, _mm_bwd)


@jax.custom_vjp
def mm_nt(a, b):
    return _bdot(a, b, 1, 1)


def _mm_nt_fwd(a, b):
    return _bdot(a, b, 1, 1), (a, b)


def _mm_nt_bwd(res, g):
    a, b = res
    return _bdot(g, b, 1, 0), _bdot(g, a, 0, 0)


mm_nt.defvjp(_mm_nt_fwd, _mm_nt_bwd)


@jax.custom_vjp
def mm_tn(a, b):
    return _bdot(a, b, 0, 0)


def _mm_tn_fwd(a, b):
    return _bdot(a, b, 0, 0), (a, b)


def _mm_tn_bwd(res, g):
    a, b = res
    return _bdot(b, g, 1, 1), _bdot(a, g, 1, 0)


mm_tn.defvjp(_mm_tn_fwd, _mm_tn_bwd)


def dot_hi(a, b):
    return lax.dot_general(a, b, (((1,), (0,)), ((), ())), precision=HI, preferred_element_type=F32)


def _split3(x):
    x1 = x.astype(BF16)
    r = x - x1.astype(F32)
    x2 = r.astype(BF16)
    return x1, x2, (r - x2.astype(F32)).astype(BF16)


def _cdot(c, x, cc, cx, c_first=True):
    parts = _split3(x)
    if c_first:
        return _bdot(c, parts[0], cc, cx) + _bdot(c, parts[1], cc, cx) + _bdot(c, parts[2], cc, cx)
    return _bdot(parts[0], c, cx, cc) + _bdot(parts[1], c, cx, cc) + _bdot(parts[2], c, cx, cc)


@jax.custom_vjp
def cmm(c, x):
    return _cdot(c, x, 1, 0)


def _cmm_fwd(c, x):
    return _cdot(c, x, 1, 0), c


def _cmm_bwd(c, g):
    return jnp.zeros_like(c), _cdot(c, g, 0, 0)


cmm.defvjp(_cmm_fwd, _cmm_bwd)


@jax.custom_vjp
def mmc(x, c):
    return _cdot(c, x, 0, 1, c_first=False)


def _mmc_fwd(x, c):
    return _cdot(c, x, 0, 1, c_first=False), c


def _mmc_bwd(c, g):
    return _cdot(c, g, 1, 1, c_first=False), jnp.zeros_like(c)


mmc.defvjp(_mmc_fwd, _mmc_bwd)


def _sigmoid(x):
    return 1.0 / (1.0 + jnp.exp(-x))


def _silu(x):
    return x * _sigmoid(x)


def _softplus(x):
    return jnp.maximum(x, 0.0) + jnp.log(1.0 + jnp.exp(-jnp.abs(x)))


def _rms(x, w):
    return x * lax.rsqrt(jnp.mean(x * x, axis=-1, keepdims=True) + EPS) * w


SC = 256


class _Consts:
    def __init__(self, rev):
        r = lax.broadcasted_iota(jnp.int32, (SC, SC), 0)
        c = lax.broadcasted_iota(jnp.int32, (SC, SC), 1)
        same = (r >> 6) == (c >> 6)
        a = jnp.where(rev, c, r)
        b = jnp.where(rev, r, c)
        self.incl = same & (a >= b)
        self.strict = same & (a > b)
        self.incl_f = self.incl.astype(F32)
        self.eye = (r == c).astype(F32)
        rows = lax.broadcasted_iota(jnp.int32, (SC, 1), 0)
        self.last_col = ((rows & (CH - 1)) == jnp.where(rev, 0, CH - 1)).astype(F32)
        rr = lax.broadcasted_iota(jnp.int32, (SC, CH), 0)
        cc = lax.broadcasted_iota(jnp.int32, (SC, CH), 1)
        self.fold = ((rr & (CH - 1)) == cc).astype(F32)


def _dot3(a, b, ca=1, cb=0):
    ah, bh = a.astype(BF16), b.astype(BF16)
    al, bl = (a - ah.astype(F32)).astype(BF16), (b - bh.astype(F32)).astype(BF16)
    return _bdot(ah, bh, ca, cb) + (_bdot(ah, bl, ca, cb) + _bdot(al, bh, ca, cb))


TRI_SPLIT_LEVELS = 2


def _tri_inv(low, eye):
    n = -low
    acc = eye + n
    p = n
    for level in range(5):
        dot = _dot3 if level < TRI_SPLIT_LEVELS else (lambda a, b: _bdot(a, b, 1, 0))
        p = dot(p, p)
        acc = acc + dot(acc, p)
    return acc


@jax.custom_vjp
def _solve2(low, rv, rk, tinv):
    x = _dot3(tinv, jnp.concatenate([rv, rk], axis=1))
    return x[:, :DA], x[:, DA:]


def _solve2_fwd(low, rv, rk, tinv):
    x = _dot3(tinv, jnp.concatenate([rv, rk], axis=1))
    return (x[:, :DA], x[:, DA:]), (x, tinv)


def _solve2_bwd(res, g):
    x, tinv = res
    drhs = _dot3(tinv, jnp.concatenate(g, axis=1), 0, 0)
    return -_dot3(drhs, x, 1, 1), drhs[:, :DA], drhs[:, DA:], jnp.zeros_like(tinv)


_solve2.defvjp(_solve2_fwd, _solve2_bwd)


def _chunk_last(x, cs):
    xs = (x * cs.last_col).reshape(SC // CH, CH, x.shape[1])
    return jnp.broadcast_to(jnp.sum(xs, axis=1, keepdims=True), xs.shape).reshape(x.shape)


def _gdn_decay(g, cs):
    gw = jnp.concatenate([g] * (SC // DA), axis=1)
    grow = jnp.sum(cs.eye * gw, axis=0, keepdims=True)
    return jnp.where(cs.incl, jnp.exp(jnp.where(cs.incl, gw - grow, 0.0)), 0.0)


def _gdn_intra(q, k, v, g, bx, tinv, cs):
    decay = _gdn_decay(g, cs)
    kb = k * bx
    low = jnp.where(cs.strict, mm_nt(kb, k) * decay, 0.0)
    eg = jnp.exp(g)
    made = tinv is None
    if made:
        tinv = _tri_inv(low, cs.eye)
    u, w = _solve2(low, v * bx, kb * eg, tinv)
    attn = mmc(mm_nt(q, k) * decay, cs.fold)
    qd = q * eg
    glast = _chunk_last(g, cs)
    kd = k * jnp.exp(glast - g)
    outs = (u, w, attn, qd, kd, jnp.exp(glast))
    return outs + (tinv,) if made else outs


def _gdn_scan(u, w, attn, qd, kd, el, s):
    vn = u - mm(w, s)
    o = mm(qd, s) + mm(attn, vn)
    sn = s * el + mm_tn(kd, vn)
    return o, sn


def _gla_intra(q, k, v, gk, cs):
    gc = cmm(cs.incl_f, gk)
    qg = q * (DKB ** -0.5) * jnp.exp(gc)
    kg = k * jnp.exp(-gc)
    attn = jnp.where(cs.incl, mm_nt(qg, kg), 0.0)
    intra = mm(attn, v)
    glast = _chunk_last(gc, cs)
    kd = k * jnp.exp(glast - gc)
    return qg, kd, intra, jnp.exp(glast)


def _gla_scan(qg, kd, v, el, st):
    o = mm_nt(qg, st)
    stn = st * el + mm_tn(v, kd)
    return o, stn


def _shift_rows(x, s):
    if s == 0:
        return x
    t = x.shape[0]
    rolled = pltpu.roll(x, (-s) % t, 0)
    rows = lax.broadcasted_iota(jnp.int32, x.shape, 0)
    return jnp.where((rows + s >= 0) & (rows + s < t), rolled, 0.0)


@jax.custom_vjp
def _conv5(x, w):
    acc = w[0:1] * _shift_rows(x, -2)
    for j in range(1, 5):
        acc = acc + w[j:j + 1] * _shift_rows(x, j - 2)
    return acc


def _conv5_fwd(x, w):
    return _conv5(x, w), (x, w)


def _conv5_bwd(res, g):
    x, w = res
    dx = w[0:1] * _shift_rows(g, 2)
    for j in range(1, 5):
        dx = dx + w[j:j + 1] * _shift_rows(g, 2 - j)
    rows = lax.broadcasted_iota(jnp.int32, w.shape, 0)
    dw = jnp.zeros_like(w)
    for j in range(5):
        dwj = jnp.sum(g * _shift_rows(x, j - 2), axis=0, keepdims=True)
        dw = dw + jnp.where(rows == j, dwj, 0.0)
    return dx, dw


_conv5.defvjp(_conv5_fwd, _conv5_bwd)


def _qkv_act(kind):
    def f(x, w):
        c = _silu(_conv5(x, w))
        if kind == 2:
            return c
        c = c * lax.rsqrt(jnp.sum(c * c, axis=-1, keepdims=True) + EPS)
        return c * (DA ** -0.5) if kind == 0 else c
    return f


def _inproj(x, lnw, wperm, tn=512):
    t = x.shape[0]
    tm = min(t, 2048)

    def body(x_ref, lnw_ref, w_ref, p_ref, h_ref, hbuf):
        @pl.when(pl.program_id(1) == 0)
        def _():
            hb = _rms(x_ref[...], lnw_ref[...]).astype(BF16)
            hbuf[...] = hb
            h_ref[...] = hb
        p_ref[...] = _bdot(hbuf[...], w_ref[...], 1, 1)

    return pl.pallas_call(
        body, name="inproj", grid=(t // tm, NPERM // tn),
        in_specs=[pl.BlockSpec((tm, D), lambda i, j: (i, 0)),
                  pl.BlockSpec((1, D), lambda i, j: (0, 0)),
                  pl.BlockSpec((tn, D), lambda i, j: (j, 0))],
        out_specs=[pl.BlockSpec((tm, tn), lambda i, j: (i, j)),
                   pl.BlockSpec((tm, D), lambda i, j: (i, 0))],
        out_shape=[jax.ShapeDtypeStruct((t, NPERM), F32),
                   jax.ShapeDtypeStruct((t, D), BF16)],
        scratch_shapes=[pltpu.VMEM((tm, D), BF16)],
        compiler_params=_cparams(("parallel", "arbitrary")),
    )(x, lnw, wperm)


DP_TILE = 512
DP_PIECES = ((0, 19),)


def _piece_specs(tm, j_first):
    specs = []
    for j0, n in DP_PIECES:
        def imap(a, b, j0=j0, n=n):
            j, i = (a, b) if j_first else (b, a)
            inside = (j >= j0) & (j < j0 + n)
            return jnp.where(inside, i, 0), jnp.clip(j - j0, 0, n - 1)
        specs.append(pl.BlockSpec((tm, DP_TILE), imap))
    return specs


def _for_piece(j, refs, fn):
    for (j0, n), ref in zip(DP_PIECES, refs):
        @pl.when((j >= j0) & (j < j0 + n))
        def _(ref=ref):
            fn(ref[...])


def _inproj_dw(h, pieces):
    t = h.shape[0]
    tm = min(t, 2048)
    npc = len(pieces)

    def body(h_ref, *refs):
        dw_ref = refs[npc]
        single = t == tm and npc == 1

        if not single:
            @pl.when(pl.program_id(1) == 0)
            def _():
                dw_ref[...] = jnp.zeros_like(dw_ref)

        def add(dp):
            if single:
                dw_ref[...] = _bdot(dp, h_ref[...], 0, 0)
            else:
                dw_ref[...] += _bdot(dp, h_ref[...], 0, 0)
        _for_piece(pl.program_id(0), refs[:npc], add)

    return pl.pallas_call(
        body, name="inproj_dw", grid=(NPERM // DP_TILE, t // tm),
        in_specs=[pl.BlockSpec((tm, D), lambda j, i: (i, 0))] + _piece_specs(tm, True),
        out_specs=pl.BlockSpec((DP_TILE, D), lambda j, i: (j, 0)),
        out_shape=jax.ShapeDtypeStruct((NPERM, D), F32),
        compiler_params=_cparams(("parallel", "arbitrary")),
    )(h, *pieces)


def _inproj_dx(pieces, wperm, x, lnw, dyres):
    t = x.shape[0]
    tm = min(t, 1024)
    tn = DP_TILE
    nj = NPERM // tn
    npc = len(pieces)

    def body(*refs):
        w_ref, x_ref, lnw_ref, dy_ref, dx_ref, dlnw_ref, acc = refs[npc:]
        j = pl.program_id(1)

        @pl.when(j == 0)
        def _():
            acc[...] = jnp.zeros_like(acc)

        def add(dp):
            acc[...] += _bdot(dp, w_ref[...], 1, 0)
        _for_piece(j, refs[:npc], add)

        @pl.when(j == nj - 1)
        def _():
            _, vjp = jax.vjp(_rms, x_ref[...], lnw_ref[...])
            dx, dlnw = vjp(acc[...])
            dx_ref[...] = dx + dy_ref[...]

            @pl.when(pl.program_id(0) == 0)
            def _():
                dlnw_ref[...] = jnp.zeros_like(dlnw_ref)
            dlnw_ref[...] += jnp.broadcast_to(dlnw, dlnw_ref.shape)

    return pl.pallas_call(
        body, name="inproj_dx", grid=(t // tm, nj),
        in_specs=_piece_specs(tm, False) + [
                  pl.BlockSpec((tn, D), lambda i, j: (j, 0)),
                  pl.BlockSpec((tm, D), lambda i, j: (i, 0)),
                  pl.BlockSpec((1, D), lambda i, j: (0, 0)),
                  pl.BlockSpec((tm, D), lambda i, j: (i, 0))],
        out_specs=[pl.BlockSpec((tm, D), lambda i, j: (i, 0)),
                   pl.BlockSpec((8, D), lambda i, j: (0, 0))],
        out_shape=[jax.ShapeDtypeStruct((t, D), F32), jax.ShapeDtypeStruct((8, D), F32)],
        scratch_shapes=[pltpu.VMEM((tm, D), F32)],
        compiler_params=_cparams(("arbitrary", "arbitrary")),
    )(*pieces, wperm, x, lnw, dyres)


def _qkv_fwd(p, convw, kind):
    t = p.shape[0]
    f = _qkv_act(kind)

    def body(p_ref, w_ref, o_ref):
        o_ref[...] = f(p_ref[...], w_ref[...])

    return pl.pallas_call(
        body, name=f"qkv_fwd{kind}", grid=(NA,),
        in_specs=[pl.BlockSpec((t, DA), lambda h: (0, kind * NA + h)),
                  pl.BlockSpec((8, DA), lambda h: (0, kind * NA + h))],
        out_specs=pl.BlockSpec((t, DA), lambda h: (0, h)),
        out_shape=jax.ShapeDtypeStruct((t, D), F32),
        compiler_params=_cparams(("parallel",)),
    )(p, convw)


def _qkv_bwd(p, convw, dout, kind):
    t = p.shape[0]
    f = _qkv_act(kind)

    def body(p_ref, w_ref, g_ref, dx_ref, dw_ref):
        _, vjp = jax.vjp(f, p_ref[...], w_ref[...])
        dx, dw = vjp(g_ref[...])
        dx_ref[...] = dx.astype(BF16)
        dw_ref[...] = dw

    return pl.pallas_call(
        body, name=f"qkv_bwd{kind}", grid=(NA,),
        in_specs=[pl.BlockSpec((t, DA), lambda h: (0, kind * NA + h)),
                  pl.BlockSpec((8, DA), lambda h: (0, kind * NA + h)),
                  pl.BlockSpec((t, DA), lambda h: (0, h))],
        out_specs=[pl.BlockSpec((t, DA), lambda h: (0, h)),
                   pl.BlockSpec((8, DA), lambda h: (0, h))],
        out_shape=[jax.ShapeDtypeStruct((t, D), BF16), jax.ShapeDtypeStruct((8, D), F32)],
        compiler_params=_cparams(("parallel",)),
    )(p, convw, dout)


def _gates_f(ps, alog_row, dt_row, w2f, b2f, w2b, b2b):
    lane = lax.broadcasted_iota(jnp.int32, ps.shape, 1)
    lg = -jnp.exp(alog_row) * _softplus(ps + dt_row)
    gsm = jnp.where(lane < 16, lg, jnp.where(lane < 32, _sigmoid(ps), 0.0))
    gkf = -_softplus(-(mm(ps, w2f) + b2f)) * (1.0 / 16.0)
    gkb = -_softplus(-(mm(ps, w2b) + b2b)) * (1.0 / 16.0)
    return gsm, gkf, gkb


def _gates_fwd(ps, alog_row, dt_row, w2f, b2f, w2b, b2b, tm=512):
    t = ps.shape[0]

    def body(ps_ref, a_ref, d_ref, wf_ref, bf_ref, wb_ref, bb_ref, gsm_ref, gk_ref):
        gsm, gkf, gkb = _gates_f(ps_ref[...], a_ref[...], d_ref[...], wf_ref[...], bf_ref[...],
                                 wb_ref[...], bb_ref[...])
        gsm_ref[...] = gsm
        gk_ref[0] = gkf
        gk_ref[1] = gkb

    row = lambda n: pl.BlockSpec((1, n), lambda i: (0, 0))
    mat = pl.BlockSpec((128, 512), lambda i: (0, 0))
    return pl.pallas_call(
        body, name="gates_fwd", grid=(t // tm,),
        in_specs=[pl.BlockSpec((tm, 128), lambda i: (i, PS_BLOCK)), row(128), row(128), mat, row(512), mat, row(512)],
        out_specs=[pl.BlockSpec((tm, 128), lambda i: (i, 0)),
                   pl.BlockSpec((2, tm, 512), lambda i: (0, i, 0))],
        out_shape=[jax.ShapeDtypeStruct((t, 128), F32), jax.ShapeDtypeStruct((2, t, 512), F32)],
        compiler_params=_cparams(("parallel",)),
    )(ps, alog_row, dt_row, w2f, b2f, w2b, b2b)


def _gates_bwd(ps, alog_row, dt_row, w2f, b2f, w2b, b2b, dgsm, dgk, tm=512):
    t = ps.shape[0]

    def body(ps_ref, a_ref, d_ref, wf_ref, bf_ref, wb_ref, bb_ref, dgsm_ref, dgk_ref,
             dps_ref, da_ref, dd_ref, dwf_ref, dbf_ref, dwb_ref, dbb_ref):
        _, vjp = jax.vjp(_gates_f, ps_ref[...], a_ref[...], d_ref[...], wf_ref[...], bf_ref[...],
                         wb_ref[...], bb_ref[...])
        dps, da, dd, dwf, dbf, dwb, dbb = vjp((dgsm_ref[...], dgk_ref[0], dgk_ref[1]))
        dps_ref[:, 0:128] = dps.astype(BF16)
        dps_ref[:, 128:DP_TILE] = jnp.zeros((tm, DP_TILE - 128), BF16)
        accs = ((da_ref, da), (dd_ref, dd), (dwf_ref, dwf), (dbf_ref, dbf), (dwb_ref, dwb), (dbb_ref, dbb))

        @pl.when(pl.program_id(0) == 0)
        def _():
            for ref, _ in accs:
                ref[...] = jnp.zeros_like(ref)
        for ref, val in accs:
            ref[...] += jnp.broadcast_to(val, ref.shape)

    row = lambda n: pl.BlockSpec((1, n), lambda i: (0, 0))
    row8 = lambda n: pl.BlockSpec((8, n), lambda i: (0, 0))
    mat = pl.BlockSpec((128, 512), lambda i: (0, 0))
    return pl.pallas_call(
        body, name="gates_bwd", grid=(t // tm,),
        in_specs=[pl.BlockSpec((tm, 128), lambda i: (i, PS_BLOCK)), row(128), row(128), mat, row(512), mat, row(512),
                  pl.BlockSpec((tm, 128), lambda i: (i, 0)),
                  pl.BlockSpec((2, tm, 512), lambda i: (0, i, 0))],
        out_specs=[pl.BlockSpec((tm, DP_TILE), lambda i: (i, 0)), row8(128), row8(128), mat, row8(512), mat,
                   row8(512)],
        out_shape=[jax.ShapeDtypeStruct((t, DP_TILE), BF16),
                   jax.ShapeDtypeStruct((8, 128), F32), jax.ShapeDtypeStruct((8, 128), F32),
                   jax.ShapeDtypeStruct((128, 512), F32), jax.ShapeDtypeStruct((8, 512), F32),
                   jax.ShapeDtypeStruct((128, 512), F32), jax.ShapeDtypeStruct((8, 512), F32)],
        compiler_params=_cparams(("arbitrary",)),
    )(ps, alog_row, dt_row, w2f, b2f, w2b, b2b, dgsm, dgk)


def _rows(i):
    return pl.ds(pl.multiple_of(i * CH, CH), CH)


def _srows(i):
    return pl.ds(pl.multiple_of(i * SC, SC), SC)


def _first_row(x):
    row = lax.broadcasted_iota(jnp.int32, (8, x.shape[1]), 0)
    return jnp.where(row == 0, jnp.broadcast_to(x, (8, x.shape[1])), 0.0)


def _chunk_rows(e_ref, i):
    pad = jnp.zeros((CH - 8, 128), F32)
    return jnp.concatenate([x for c in range(SC // CH) for x in (e_ref[(SC // CH) * i + c], pad)], axis=0)


def _gcum_f(gsm, tm):
    i = lax.broadcasted_iota(jnp.int32, (tm, tm), 0)
    j = lax.broadcasted_iota(jnp.int32, (tm, tm), 1)
    same = (i >> 6) == (j >> 6)
    lower = (same & (i >= j)).astype(F32)
    upper = (same & (i <= j)).astype(F32)
    r = lax.broadcasted_iota(jnp.int32, (128, D), 0)
    head = lax.broadcasted_iota(jnp.int32, (128, D), 1) >> 7
    pick = lambda off: (r == head + off).astype(F32)
    lane = lax.broadcasted_iota(jnp.int32, gsm.shape, 1)
    run = jnp.where(lane < 8, cmm(lower, gsm), cmm(upper, gsm))
    return mmc(run, pick(0)), mmc(run, pick(8)), mmc(gsm, pick(16)), mmc(gsm, pick(24))


def _gcum_fwd(gsm, tm=256):
    t = gsm.shape[0]

    def body(s_ref, g_ref, b_ref):
        gf, gb, bf, bb = _gcum_f(s_ref[...], tm)
        g_ref[0] = gf
        g_ref[1] = gb
        b_ref[0] = bf
        b_ref[1] = bb

    two = pl.BlockSpec((2, tm, D), lambda i: (0, i, 0))
    return pl.pallas_call(
        body, name="gcum_fwd", grid=(t // tm,),
        in_specs=[pl.BlockSpec((tm, 128), lambda i: (i, 0))], out_specs=[two, two],
        out_shape=[jax.ShapeDtypeStruct((2, t, D), F32)] * 2,
        compiler_params=_cparams(("parallel",)),
    )(gsm)


def _gcum_bwd(gsm, dg2, db2, tm=256):
    t = gsm.shape[0]

    def body(s_ref, dg_ref, db_ref, ds_ref):
        _, vjp = jax.vjp(lambda s: _gcum_f(s, tm), s_ref[...])
        ds_ref[...] = vjp((dg_ref[0], dg_ref[1], db_ref[0], db_ref[1]))[0]

    two = pl.BlockSpec((2, tm, D), lambda i: (0, i, 0))
    tile = pl.BlockSpec((tm, 128), lambda i: (i, 0))
    return pl.pallas_call(
        body, name="gcum_bwd", grid=(t // tm,),
        in_specs=[tile, two, two], out_specs=tile,
        out_shape=jax.ShapeDtypeStruct((t, 128), F32),
        compiler_params=_cparams(("parallel",)),
    )(gsm, dg2, db2)


def _gdn_intra_fwd(qn, kn, vc, g2, b2):
    t = qn.shape[0]
    n = t // CH

    def body(q_ref, k_ref, v_ref, g_ref, b_ref, u_ref, w_ref, a_ref, qd_ref, kd_ref, e_ref, t_ref):
        cs = _Consts(pl.program_id(0) == 1)

        def step(i, carry):
            r = _srows(i)
            q, k, v, g, bx = q_ref[r, :], k_ref[r, :], v_ref[r, :], g_ref[r, :], b_ref[r, :]
            u, w, a, qd, kd, el, tinv = _gdn_intra(q, k, v, g, bx, None, cs)
            u_ref[r, :] = u
            w_ref[r, :] = w
            a_ref[r, :] = a
            qd_ref[r, :] = qd
            kd_ref[r, :] = kd
            t_ref[r, :] = tinv
            for c in range(SC // CH):
                e_ref[(SC // CH) * i + c] = el[c * CH:c * CH + 8]
            return carry

        lax.fori_loop(0, t // SC, step, 0)

    head = pl.BlockSpec((t, DA), lambda d, h: (0, h))
    dh = pl.BlockSpec((None, t, DA), lambda d, h: (d, 0, h))
    sq = lambda w: pl.BlockSpec((None, None, t, w), lambda d, h: (d, h, 0, 0))
    big = jax.ShapeDtypeStruct((2, t, D), F32)
    return pl.pallas_call(
        body, name="gdn_intra_fwd", grid=(2, NA),
        in_specs=[head, head, head, dh, dh],
        out_specs=[dh, dh, sq(CH), dh, dh, pl.BlockSpec((None, None, n, 8, 128), lambda d, h: (d, h, 0, 0, 0)),
                   sq(SC)],
        out_shape=[big, big, jax.ShapeDtypeStruct((2, NA, t, CH), F32), big, big,
                   jax.ShapeDtypeStruct((2, NA, n, 8, 128), F32), jax.ShapeDtypeStruct((2, NA, t, SC), F32)],
        compiler_params=_cparams(("parallel", "parallel")),
    )(qn, kn, vc, g2, b2)


SCAN_TB = 256
SCAN_HB = 8


def _scan_specs(t, width, nheads, hb, along):
    nt = t // SCAN_TB
    nb = SCAN_TB // CH

    def tmap(d, tt):
        fwd = tt + d * (nt - 1 - 2 * tt)
        return fwd if along > 0 else nt - 1 - fwd

    tok = pl.BlockSpec((None, SCAN_TB, hb * width), lambda d, h, tt: (d, tmap(d, tt), h))
    per = lambda *tail: pl.BlockSpec((None, hb, nb) + tail, lambda d, h, tt: (d, h, tmap(d, tt)) + (0,) * len(tail))
    sq = pl.BlockSpec((None, hb, SCAN_TB, CH), lambda d, h, tt: (d, h, tmap(d, tt), 0))
    shared = lambda w: pl.BlockSpec((SCAN_TB, hb * w), lambda d, h, tt: (tmap(d, tt), h))
    return tok, per, sq, shared, (2, nheads // hb, nt), nb


def _gdn_scan_fwd(u, w, a, qd, kd, e):
    t = u.shape[1]
    tok, per, sq, _, grid, nb = _scan_specs(t, DA, NA, SCAN_HB, +1)

    def body(u_ref, w_ref, a_ref, qd_ref, kd_ref, e_ref, o_ref, s_ref, state):
        rev = pl.program_id(0) == 1

        @pl.when(pl.program_id(2) == 0)
        def _():
            state[...] = jnp.zeros_like(state)

        def step(i, ss):
            ci = jnp.where(rev, nb - 1 - i, i)
            r = _rows(ci)
            out = []
            for hh, s in enumerate(ss):
                c = slice(hh * DA, (hh + 1) * DA)
                s_ref[hh, ci] = s
                o, sn = _gdn_scan(u_ref[r, c], w_ref[r, c], a_ref[hh, r, :], qd_ref[r, c], kd_ref[r, c],
                                  e_ref[hh, ci][0:1], s)
                o_ref[r, c] = o
                out.append(sn)
            return tuple(out)

        ss = lax.fori_loop(0, nb, step, tuple(state[hh] for hh in range(SCAN_HB)))
        for hh, s in enumerate(ss):
            state[hh] = s

    return pl.pallas_call(
        body, name="gdn_scan_fwd", grid=grid,
        in_specs=[tok, tok, sq, tok, tok, per(8, 128)],
        out_specs=[tok, per(DA, DA)],
        out_shape=[jax.ShapeDtypeStruct((2, t, D), F32), jax.ShapeDtypeStruct((2, NA, t // CH, DA, DA), F32)],
        scratch_shapes=[pltpu.VMEM((SCAN_HB, DA, DA), F32)],
        compiler_params=_cparams(("parallel", "parallel", "arbitrary")),
    )(u, w, a, qd, kd, e)


def _gdn_scan_bwd(u, w, a, qd, kd, e, ssave, do):
    t = u.shape[1]
    tok, per, sq, shared, grid, nb = _scan_specs(t, DA, NA, SCAN_HB, -1)

    def body(u_ref, w_ref, a_ref, qd_ref, kd_ref, e_ref, s_ref, do_ref,
             du_ref, dw_ref, da_ref, dqd_ref, dkd_ref, de_ref, state):
        rev = pl.program_id(0) == 1

        @pl.when(pl.program_id(2) == 0)
        def _():
            state[...] = jnp.zeros_like(state)

        def step(i, dss):
            ci = jnp.where(rev, i, nb - 1 - i)
            r = _rows(ci)
            out = []
            for hh, ds in enumerate(dss):
                c = slice(hh * DA, (hh + 1) * DA)
                _, vjp = jax.vjp(_gdn_scan, u_ref[r, c], w_ref[r, c], a_ref[hh, r, :], qd_ref[r, c], kd_ref[r, c],
                                 e_ref[hh, ci][0:1], s_ref[hh, ci])
                du, dw, da, dqd, dkd, de, dsn = vjp((do_ref[r, c], ds))
                du_ref[r, c] = du
                dw_ref[r, c] = dw
                da_ref[hh, r, :] = da
                dqd_ref[r, c] = dqd
                dkd_ref[r, c] = dkd
                de_ref[hh, ci] = _first_row(de)
                out.append(dsn)
            return tuple(out)

        dss = lax.fori_loop(0, nb, step, tuple(state[hh] for hh in range(SCAN_HB)))
        for hh, ds in enumerate(dss):
            state[hh] = ds

    big = jax.ShapeDtypeStruct((2, t, D), F32)
    return pl.pallas_call(
        body, name="gdn_scan_bwd", grid=grid,
        in_specs=[tok, tok, sq, tok, tok, per(8, 128), per(DA, DA), shared(DA)],
        out_specs=[tok, tok, sq, tok, tok, per(8, 128)],
        out_shape=[big, big, jax.ShapeDtypeStruct((2, NA, t, CH), F32), big, big,
                   jax.ShapeDtypeStruct((2, NA, t // CH, 8, 128), F32)],
        scratch_shapes=[pltpu.VMEM((SCAN_HB, DA, DA), F32)],
        compiler_params=_cparams(("parallel", "parallel", "arbitrary")),
    )(u, w, a, qd, kd, e, ssave, do)


def _gdn_intra_bwd(qn, kn, vc, g2, b2, tinv, du, dw, da, dqd, dkd, de):
    t = qn.shape[0]
    n = t // CH

    def body(q_ref, k_ref, v_ref, g_ref, b_ref, t_ref, du_ref, dw_ref, da_ref, dqd_ref, dkd_ref, de_ref,
             dq_ref, dk_ref, dv_ref, dg_ref, db_ref):
        d = pl.program_id(1)
        cs = _Consts(d == 1)

        def step(i, carry):
            r = _srows(i)
            tinv_c = t_ref[r, :]
            f = lambda q, k, v, g, bx: _gdn_intra(q, k, v, g, bx, tinv_c, cs)
            _, vjp = jax.vjp(f, q_ref[r, :], k_ref[r, :], v_ref[r, :], g_ref[r, :], b_ref[r, :])
            dq, dk, dv, dg, dbx = vjp((du_ref[r, :], dw_ref[r, :], da_ref[r, :], dqd_ref[r, :],
                                       dkd_ref[r, :], _chunk_rows(de_ref, i)))

            @pl.when(d == 0)
            def _():
                dq_ref[r, :] = dq
                dk_ref[r, :] = dk
                dv_ref[r, :] = dv

            @pl.when(d != 0)
            def _():
                dq_ref[r, :] += dq
                dk_ref[r, :] += dk
                dv_ref[r, :] += dv
            dg_ref[r, :] = dg
            db_ref[r, :] = dbx
            return carry

        lax.fori_loop(0, t // SC, step, 0)

    head = pl.BlockSpec((t, DA), lambda h, d: (0, h))
    dh = pl.BlockSpec((None, t, DA), lambda h, d: (d, 0, h))
    sq = pl.BlockSpec((None, None, t, CH), lambda h, d: (d, h, 0, 0))
    tq = pl.BlockSpec((None, None, t, SC), lambda h, d: (d, h, 0, 0))
    full = jax.ShapeDtypeStruct((t, D), F32)
    big = jax.ShapeDtypeStruct((2, t, D), F32)
    return pl.pallas_call(
        body, name="gdn_intra_bwd", grid=(NA, 2),
        in_specs=[head, head, head, dh, dh, tq, dh, dh, sq, dh, dh,
                  pl.BlockSpec((None, None, n, 8, 128), lambda h, d: (d, h, 0, 0, 0))],
        out_specs=[head, head, head, dh, dh],
        out_shape=[full, full, full, big, big],
        compiler_params=_cparams(("arbitrary", "arbitrary")),
    )(qn, kn, vc, g2, b2, tinv, du, dw, da, dqd, dkd, de)


def _gla_specs(t, order):
    ix = (lambda d, h: (d, h)) if order == "dh" else (lambda h, d: (d, h))

    def mk(fn):
        return lambda a, b: fn(*ix(a, b))
    q = pl.BlockSpec((t, DKB), mk(lambda d, h: (0, 32 + h)))
    k = pl.BlockSpec((t, DKB), mk(lambda d, h: (0, 36 + h)))
    v = pl.BlockSpec((t, DVB), mk(lambda d, h: (0, 20 + h)))
    dk = pl.BlockSpec((None, t, DKB), mk(lambda d, h: (d, 0, h)))
    dv = pl.BlockSpec((None, t, DVB), mk(lambda d, h: (d, 0, h)))
    e = pl.BlockSpec((None, None, t // CH, 8, 128), mk(lambda d, h: (d, h, 0, 0, 0)))
    s = pl.BlockSpec((None, None, t // CH, DVB, DKB), mk(lambda d, h: (d, h, 0, 0, 0)))
    return q, k, v, dk, dv, e, s


def _gla_intra_fwd(p, gk):
    t = p.shape[0]
    n = t // CH

    def body(q_ref, k_ref, v_ref, g_ref, qg_ref, kd_ref, in_ref, e_ref):
        cs = _Consts(pl.program_id(0) == 1)

        def step(i, carry):
            r = _srows(i)
            qg, kd, intra, el = _gla_intra(q_ref[r, :], k_ref[r, :], v_ref[r, :], g_ref[r, :], cs)
            qg_ref[r, :] = qg
            kd_ref[r, :] = kd
            in_ref[r, :] = intra
            for c in range(SC // CH):
                e_ref[(SC // CH) * i + c] = el[c * CH:c * CH + 8]
            return carry

        lax.fori_loop(0, t // SC, step, 0)

    q, k, v, dk, dv, e, _ = _gla_specs(t, "dh")
    return pl.pallas_call(
        body, name="gla_intra_fwd", grid=(2, NB),
        in_specs=[q, k, v, dk], out_specs=[dk, dk, dv, e],
        out_shape=[jax.ShapeDtypeStruct((2, t, NB * DKB), F32), jax.ShapeDtypeStruct((2, t, NB * DKB), F32),
                   jax.ShapeDtypeStruct((2, t, D), F32), jax.ShapeDtypeStruct((2, NB, n, 8, 128), F32)],
        compiler_params=_cparams(("parallel", "parallel")),
    )(p, p, p, gk)


GLA_HB = 4


def _gla_v_spec(t, along):
    nt = t // SCAN_TB

    def tmap(d, tt):
        fwd = tt + d * (nt - 1 - 2 * tt)
        return fwd if along > 0 else nt - 1 - fwd

    return pl.BlockSpec((SCAN_TB, GLA_HB * DVB), lambda d, h, tt: (tmap(d, tt), 5120 // (GLA_HB * DVB) + h))


def _gla_scan_fwd(p, qg, kd, intra, e):
    t = p.shape[0]
    tokk, per, _, _, grid, nb = _scan_specs(t, DKB, NB, GLA_HB, +1)
    tokv = _scan_specs(t, DVB, NB, GLA_HB, +1)[0]

    def body(v_ref, qg_ref, kd_ref, in_ref, e_ref, o_ref, s_ref, state):
        rev = pl.program_id(0) == 1

        @pl.when(pl.program_id(2) == 0)
        def _():
            state[...] = jnp.zeros_like(state)

        def step(i, sts):
            ci = jnp.where(rev, nb - 1 - i, i)
            r = _rows(ci)
            out = []
            for hh, st in enumerate(sts):
                ck = slice(hh * DKB, (hh + 1) * DKB)
                cv = slice(hh * DVB, (hh + 1) * DVB)
                s_ref[hh, ci] = st
                o, stn = _gla_scan(qg_ref[r, ck], kd_ref[r, ck], v_ref[r, cv], e_ref[hh, ci][0:1], st)
                o_ref[r, cv] = o + in_ref[r, cv]
                out.append(stn)
            return tuple(out)

        sts = lax.fori_loop(0, nb, step, tuple(state[hh] for hh in range(GLA_HB)))
        for hh, st in enumerate(sts):
            state[hh] = st

    return pl.pallas_call(
        body, name="gla_scan_fwd", grid=grid,
        in_specs=[_gla_v_spec(t, +1), tokk, tokk, tokv, per(8, 128)], out_specs=[tokv, per(DVB, DKB)],
        out_shape=[jax.ShapeDtypeStruct((2, t, D), F32), jax.ShapeDtypeStruct((2, NB, t // CH, DVB, DKB), F32)],
        scratch_shapes=[pltpu.VMEM((GLA_HB, DVB, DKB), F32)],
        compiler_params=_cparams(("parallel", "parallel", "arbitrary")),
    )(p, qg, kd, intra, e)


def _gla_scan_bwd(p, qg, kd, e, ssave, do):
    t = p.shape[0]
    tokk, per, _, shared, grid, nb = _scan_specs(t, DKB, NB, GLA_HB, -1)
    tokv = _scan_specs(t, DVB, NB, GLA_HB, -1)[0]

    def body(v_ref, qg_ref, kd_ref, e_ref, s_ref, do_ref, dqg_ref, dkd_ref, dv_ref, de_ref, state):
        rev = pl.program_id(0) == 1

        @pl.when(pl.program_id(2) == 0)
        def _():
            state[...] = jnp.zeros_like(state)

        def step(i, dsts):
            ci = jnp.where(rev, i, nb - 1 - i)
            r = _rows(ci)
            out = []
            for hh, dst in enumerate(dsts):
                ck = slice(hh * DKB, (hh + 1) * DKB)
                cv = slice(hh * DVB, (hh + 1) * DVB)
                _, vjp = jax.vjp(_gla_scan, qg_ref[r, ck], kd_ref[r, ck], v_ref[r, cv], e_ref[hh, ci][0:1],
                                 s_ref[hh, ci])
                dqg, dkd, dv, de, dstn = vjp((do_ref[r, cv], dst))
                dqg_ref[r, ck] = dqg
                dkd_ref[r, ck] = dkd
                dv_ref[r, cv] = dv
                de_ref[hh, ci] = _first_row(de)
                out.append(dstn)
            return tuple(out)

        dsts = lax.fori_loop(0, nb, step, tuple(state[hh] for hh in range(GLA_HB)))
        for hh, dst in enumerate(dsts):
            state[hh] = dst

    return pl.pallas_call(
        body, name="gla_scan_bwd", grid=grid,
        in_specs=[_gla_v_spec(t, -1), tokk, tokk, per(8, 128), per(DVB, DKB), shared(DVB)],
        out_specs=[tokk, tokk, tokv, per(8, 128)],
        out_shape=[jax.ShapeDtypeStruct((2, t, NB * DKB), F32), jax.ShapeDtypeStruct((2, t, NB * DKB), F32),
                   jax.ShapeDtypeStruct((2, t, D), F32), jax.ShapeDtypeStruct((2, NB, t // CH, 8, 128), F32)],
        scratch_shapes=[pltpu.VMEM((GLA_HB, DVB, DKB), F32)],
        compiler_params=_cparams(("parallel", "parallel", "arbitrary")),
    )(p, qg, kd, e, ssave, do)


def _gla_intra_bwd(p, gk, dqg, dkd, dvs, de, do):
    t = p.shape[0]
    n = t // CH

    def body(q_ref, k_ref, v_ref, g_ref, dqg_ref, dkd_ref, dvs_ref, de_ref, do_ref,
             dq_ref, dk_ref, dv_ref, dg_ref):
        d = pl.program_id(1)
        cs = _Consts(d == 1)

        @pl.when(d == 0)
        def _():
            dq_ref[...] = jnp.zeros_like(dq_ref)
            dk_ref[...] = jnp.zeros_like(dk_ref)
            dv_ref[...] = jnp.zeros_like(dv_ref)

        def step(i, carry):
            r = _srows(i)
            f = lambda q, k, v, g: _gla_intra(q, k, v, g, cs)
            _, vjp = jax.vjp(f, q_ref[r, :], k_ref[r, :], v_ref[r, :], g_ref[r, :])
            dq, dk, dv, dg = vjp((dqg_ref[r, :], dkd_ref[r, :], do_ref[r, :], _chunk_rows(de_ref, i)))
            dq_ref[r, :] += dq
            dk_ref[r, :] += dk
            dv_ref[r, :] += dv + dvs_ref[r, :]
            dg_ref[r, :] = dg
            return carry

        lax.fori_loop(0, t // SC, step, 0)

    q, k, v, dk, dv, e_s, _ = _gla_specs(t, "hd")
    hk = pl.BlockSpec((t, DKB), lambda h, d: (0, h))
    hv = pl.BlockSpec((t, DVB), lambda h, d: (0, h))
    return pl.pallas_call(
        body, name="gla_intra_bwd", grid=(NB, 2),
        in_specs=[q, k, v, dk, dk, dk, dv, e_s, hv],
        out_specs=[hk, hk, hv, dk],
        out_shape=[jax.ShapeDtypeStruct((t, NB * DKB), F32), jax.ShapeDtypeStruct((t, NB * DKB), F32),
                   jax.ShapeDtypeStruct((t, D), F32), jax.ShapeDtypeStruct((2, t, NB * DKB), F32)],
        compiler_params=_cparams(("arbitrary", "arbitrary")),
    )(p, p, p, gk, dqg, dkd, dvs, de, do)


def _seg_gate(o, z, w):
    return _rms(o, w) * _silu(z)


def _seg_merge(ya, yb, ga, gb):
    return _sigmoid(ga) * ya + _sigmoid(gb) * yb


def _seg_loss(out, x, tgt, w):
    err = x + _rms(out, w) - tgt
    return 0.5 * jnp.sum(jnp.mean(err * err, axis=-1, keepdims=True), axis=0, keepdims=True)


def _post(oa2, ob2, p, x, tgt, gdn_w, gla_w, lnpost, w3, tm=128):
    t = x.shape[0]

    def body(oa_ref, ob_ref, z_ref, gb_ref, ga_ref, gB_ref, x_ref, t_ref, aw_ref, bw_ref, lw_ref, w_ref,
             loss_ref, doa_ref, dob_ref, dz_ref, dgb_ref, dga_ref, dgB_ref, dy_ref,
             dw_ref, daw_ref, dbw_ref, dlw_ref):
        first = pl.program_id(0) == 0
        oa = oa_ref[0] + oa_ref[1]
        ob = ob_ref[0] + ob_ref[1]
        z, gb = z_ref[...], gb_ref[...]
        aw, bw = aw_ref[...], bw_ref[...]
        rs = D // NSHARD

        def mat(a, m):
            return sum(jnp.dot(a[:, s * rs:(s + 1) * rs], w_ref[s, m], preferred_element_type=F32)
                       for s in range(NSHARD))

        def mat_t(g, m):
            return jnp.concatenate([_bdot(g, w_ref[s, m], 1, 1) for s in range(NSHARD)], axis=1)

        def add_dw(a, g, m):
            for s in range(NSHARD):
                dw_ref[s, m] += _bdot(a[:, s * rs:(s + 1) * rs], g, 0, 0)

        pa = [jax.vjp(_seg_gate, oa[:, h * DA:(h + 1) * DA], z[:, h * DA:(h + 1) * DA], aw) for h in range(NA)]
        pb = [jax.vjp(_seg_gate, ob[:, h * DVB:(h + 1) * DVB], gb[:, h * DVB:(h + 1) * DVB], bw)
              for h in range(NB)]
        a1 = jnp.concatenate([v for v, _ in pa], axis=1).astype(BF16)
        a2 = jnp.concatenate([v for v, _ in pb], axis=1).astype(BF16)
        ya = mat(a1, 0)
        yb = mat(a2, 1)
        merged, vjp_m = jax.vjp(_seg_merge, ya, yb, ga_ref[...], gB_ref[...])
        mb = merged.astype(BF16)
        out = mat(mb, 2)
        loss, vjp_l = jax.vjp(_seg_loss, out, x_ref[...], t_ref[...], lw_ref[...])
        dout, dyres, _, dlw = vjp_l(jnp.ones((1, 1), F32))
        dy_ref[...] = dyres
        doutb = dout.astype(BF16)
        dmerged = mat_t(doutb, 2)
        dya, dyb, dga, dgB = vjp_m(dmerged)
        dga_ref[...] = dga.astype(BF16)
        dgB_ref[...] = dgB.astype(BF16)
        dyab, dybb = dya.astype(BF16), dyb.astype(BF16)
        da1 = mat_t(dyab, 0)
        da2 = mat_t(dybb, 1)

        daw = jnp.zeros_like(aw)
        for h in range(NA):
            sl = slice(h * DA, (h + 1) * DA)
            do, dz, dw = pa[h][1](da1[:, sl])
            doa_ref[:, sl] = do
            dz_ref[:, sl] = dz.astype(BF16)
            daw = daw + dw
        dbw = jnp.zeros_like(bw)
        for h in range(NB):
            sl = slice(h * DVB, (h + 1) * DVB)
            do, dg, dw = pb[h][1](da2[:, sl])
            dob_ref[:, sl] = do
            dgb_ref[:, sl] = dg.astype(BF16)
            dbw = dbw + dw

        @pl.when(first)
        def _():
            loss_ref[...] = jnp.zeros_like(loss_ref)
            dw_ref[...] = jnp.zeros_like(dw_ref)
            daw_ref[...] = jnp.zeros_like(daw_ref)
            dbw_ref[...] = jnp.zeros_like(dbw_ref)
            dlw_ref[...] = jnp.zeros_like(dlw_ref)

        loss_ref[...] += jnp.broadcast_to(loss, loss_ref.shape)
        add_dw(a1, dyab, 0)
        add_dw(a2, dybb, 1)
        add_dw(mb, doutb, 2)
        daw_ref[...] += jnp.broadcast_to(daw, daw_ref.shape)
        dbw_ref[...] += jnp.broadcast_to(dbw, dbw_ref.shape)
        dlw_ref[...] += jnp.broadcast_to(dlw, dlw_ref.shape)

    two = pl.BlockSpec((2, tm, D), lambda i: (0, i, 0))
    pcol = lambda c: pl.BlockSpec((tm, D), lambda i: (i, c))
    tok = pl.BlockSpec((tm, D), lambda i: (i, 0))
    row = lambda n: pl.BlockSpec((1, n), lambda i: (0, 0))
    row8 = lambda n: pl.BlockSpec((8, n), lambda i: (0, 0))
    once = pl.Buffered(1)
    tokf = jax.ShapeDtypeStruct((t, D), F32)
    tokb = jax.ShapeDtypeStruct((t, D), BF16)
    wspec = pl.BlockSpec((NSHARD, 3, D // NSHARD, D), lambda i: (0, 0, 0, 0), pipeline_mode=once)
    return pl.pallas_call(
        body, name="post", grid=(t // tm,),
        in_specs=[two, two, pcol(3), pcol(6), pcol(7), pcol(8), tok, tok, row(DA), row(DVB), row(D), wspec],
        out_specs=[row8(128), tok, tok, tok, tok, tok, tok, tok, wspec, row8(DA), row8(DVB), row8(D)],
        out_shape=[jax.ShapeDtypeStruct((8, 128), F32), tokf, tokf, tokb, tokb, tokb, tokb, tokf,
                   jax.ShapeDtypeStruct((NSHARD, 3, D // NSHARD, D), F32),
                   jax.ShapeDtypeStruct((8, DA), F32), jax.ShapeDtypeStruct((8, DVB), F32),
                   jax.ShapeDtypeStruct((8, D), F32)],
        compiler_params=_cparams(("arbitrary",), vmem_mb=56),
    )(oa2, ob2, p, p, p, p, x, tgt, gdn_w, gla_w, lnpost, w3)


def _adam_math(w, g, m, v):
    nm = B1 * m + (1.0 - B1) * g
    nv = B2 * v + (1.0 - B2) * (g * g)
    m_hat = nm / (1.0 - B1 ** STEP)
    v_hat = nv / (1.0 - B2 ** STEP)
    return -LR * (m_hat / (jnp.sqrt(v_hat) + ADAM_EPS) + WD * w), nm, nv


SMALL_SLOTS = (("ln_pre_w", 0, 1024, 0), ("a_log_fwd", 1024, 8, 0), ("a_log_bwd", 1024, 8, 8),
               ("dt_bias_fwd", 1152, 8, 0), ("dt_bias_bwd", 1152, 8, 8), ("gdn_norm_w", 1280, 128, 0),
               ("gk_b2_fwd", 1408, 512, 0), ("gk_b2_bwd", 1920, 512, 0), ("gla_norm_w", 2432, 256, 0),
               ("ln_post_w", 2688, 1024, 0))
SMALL_W = 3840


def _adam_small(gsum, ws, ms, vs):
    nw = len(SMALL_SLOTS)

    def body(g_ref, *refs):
        w_refs, m_refs, v_refs, outs = refs[0:nw], refs[nw:2 * nw], refs[2 * nw:3 * nw], refs[3 * nw:]
        for i, (_, off, n, shift) in enumerate(SMALL_SLOTS):
            slot = g_ref[0:1, off:off + max(n, 128)]
            if shift:
                slot = pltpu.roll(slot, 128 - shift, 1)
            g = slot[:, 0:n]
            d, nm, nv = _adam_math(w_refs[i][...], g, m_refs[i][...], v_refs[i][...])
            for k, val in enumerate((g, d, nm, nv)):
                outs[4 * i + k][...] = val

    vm = pl.BlockSpec(memory_space=pltpu.VMEM)
    res = pl.pallas_call(
        body, name="adam_small", in_specs=[vm] * (1 + 3 * nw), out_specs=[vm] * (4 * nw),
        out_shape=[jax.ShapeDtypeStruct((1, n), F32) for _, _, n, _ in SMALL_SLOTS for _ in range(4)],
    )(gsum, *ws, *ms, *vs)
    return {name: res[4 * i:4 * i + 4] for i, (name, _, _, _) in enumerate(SMALL_SLOTS)}


def _adam(w, mine, got, m, v, tr, tile0=0, name=""):
    rows, cols = w.shape
    nh = mine.shape[0] // tr

    def body(c_ref, w_ref, a_ref, b_ref, m_ref, v_ref, g_ref, d_ref, nm_ref, nv_ref):
        half = (tile0 + pl.program_id(0)) // nh
        g = jnp.where(half == c_ref[0], a_ref[...], b_ref[...])
        d, nm, nv = _adam_math(w_ref[...], g, m_ref[...], v_ref[...])
        g_ref[...] = g
        d_ref[...] = d
        nm_ref[...] = nm
        nv_ref[...] = nv

    blk = pl.BlockSpec((tr, cols), lambda i, cc: (i, 0))
    half = pl.BlockSpec((tr, cols), lambda i, cc: ((tile0 + i) % nh, 0))
    shp = jax.ShapeDtypeStruct((rows, cols), F32)
    return pl.pallas_call(
        body, name=f"adam_{name}{rows}x{cols}",
        grid_spec=pltpu.PrefetchScalarGridSpec(
            num_scalar_prefetch=1, grid=(rows // tr,),
            in_specs=[blk, half, half, blk, blk], out_specs=[blk] * 4),
        out_shape=[shp] * 4,
        compiler_params=_cparams(("parallel",)),
    )(lax.axis_index("c").reshape(1), w, mine, got, m, v)


def _adam_shard_small(mine, got, ws, ms, vs):
    def body(c_ref, a_ref, b_ref, *refs):
        w_refs, m_refs, v_refs, outs = refs[0:3], refs[3:6], refs[6:9], refs[9:]
        south = c_ref[0] == 0
        g = jnp.concatenate([jnp.where(south, a_ref[...], b_ref[...]),
                             jnp.where(south, b_ref[...], a_ref[...])], axis=0)
        grads = (g[0:5], g[8:24, 0:128], g[8:24, 128:256])
        for i, gi in enumerate(grads):
            d, nm, nv = _adam_math(w_refs[i][...], gi, m_refs[i][...], v_refs[i][...])
            for k, val in enumerate((gi, d, nm, nv)):
                outs[4 * i + k][...] = val

    vm = pl.BlockSpec(memory_space=pltpu.VMEM)
    res = pl.pallas_call(
        body, name="adam_shard_small",
        in_specs=[pl.BlockSpec(memory_space=pltpu.SMEM)] + [vm] * 11, out_specs=[vm] * 12,
        out_shape=[jax.ShapeDtypeStruct(w.shape, F32) for w in ws for _ in range(4)],
    )(lax.axis_index("c").reshape(1), mine, got, *ws, *ms, *vs)
    return [res[4 * i:4 * i + 4] for i in range(3)]


def _sum_cast(own, got):
    ns, _, r, c = own.shape
    tr = r // 4 if r % 64 == 0 else r

    def body(c_ref, a_ref, b_ref, f_ref, h_ref):
        s = a_ref[...] + b_ref[...]
        f_ref[...] = s
        h_ref[...] = s.astype(BF16)

    return pl.pallas_call(
        body, name=f"sum_cast_{r}x{c}",
        grid_spec=pltpu.PrefetchScalarGridSpec(
            num_scalar_prefetch=1, grid=(ns, r // tr),
            in_specs=[pl.BlockSpec((None, None, tr, c), lambda s, i, cc: (s, cc[0], i, 0)),
                      pl.BlockSpec((None, tr, c), lambda s, i, cc: (s, i, 0))],
            out_specs=[pl.BlockSpec((None, tr, c), lambda s, i, cc: (s, i, 0)),
                       pl.BlockSpec((None, tr, c), lambda s, i, cc: (s, i, 0))]),
        out_shape=[jax.ShapeDtypeStruct((ns, r, c), F32), jax.ShapeDtypeStruct((ns, r, c), BF16)],
        compiler_params=_cparams(("parallel", "parallel")),
    )(lax.axis_index("c").reshape(1), own, got)


def _sum4(mine, got):
    _, r, c = mine.shape
    tr = r // 4 if r % 64 == 0 else r

    def body(s_ref, a_ref, g_ref, o_ref):
        acc = a_ref[...] + g_ref[0].astype(F32)
        acc = acc + g_ref[1].astype(F32)
        o_ref[...] = acc + g_ref[2].astype(F32)

    shard = (2 * lax.axis_index("x") + lax.axis_index("y")).reshape(1)
    return pl.pallas_call(
        body, name=f"sum4_{r}x{c}",
        grid_spec=pltpu.PrefetchScalarGridSpec(
            num_scalar_prefetch=1, grid=(r // tr,),
            in_specs=[pl.BlockSpec((None, tr, c), lambda i, ss: (ss[0], i, 0)),
                      pl.BlockSpec((3, tr, c), lambda i, ss: (0, i, 0))],
            out_specs=pl.BlockSpec((tr, c), lambda i, ss: (i, 0))),
        out_shape=jax.ShapeDtypeStruct((r, c), F32),
        compiler_params=_cparams(("parallel",)),
    )(shard, mine, got)


def _place():
    x, y, c = lax.axis_index("x"), lax.axis_index("y"), lax.axis_index("c")
    chips = [(1 - x, y), (x, 1 - y), (1 - x, 1 - y)]
    return x, y, c, chips


def _gather_weights(parts):
    npart = len(parts)

    def body(*refs):
        ins, outs = refs[:npart], refs[npart:2 * npart]
        send_sems, recv_sems = refs[2 * npart:]
        x, y, c, chips = _place()
        sibling = (x, y, 1 - c)
        mine = 2 * x + y

        def remote(k, p, shard, half, to, src=None):
            dst = outs[p].at[shard, half]
            return pltpu.make_async_remote_copy(
                src_ref=dst if src is None else src, dst_ref=dst,
                send_sem=send_sems.at[k], recv_sem=recv_sems.at[k], device_id=to, device_id_type=MESH)

        first = [remote(j * npart + p, p, mine, c, (*chip, c), src=ins[p].at[c])
                 for j, chip in enumerate(chips) for p in range(npart)]
        for cp in first:
            cp.start()
        passed = []
        for j, (cx, cy) in enumerate(chips):
            for p in range(npart):
                remote(j * npart + p, p, 2 * cx + cy, c, (x, y, c)).wait_recv()
                fw = remote((3 + j) * npart + p, p, 2 * cx + cy, c, sibling)
                fw.start()
                passed.append(fw)
        for j, (cx, cy) in enumerate(chips):
            for p in range(npart):
                remote((3 + j) * npart + p, p, 2 * cx + cy, 1 - c, (x, y, c)).wait_recv()
        for cp in first + passed:
            cp.wait_send()

    got = pl.pallas_call(
        body, name="gather_weights",
        in_specs=[ANY] * npart, out_specs=[ANY] * npart,
        out_shape=[jax.ShapeDtypeStruct((NSHARD,) + a.shape, a.dtype) for a in parts],
        scratch_shapes=[pltpu.SemaphoreType.DMA((6 * npart,)), pltpu.SemaphoreType.DMA((6 * npart,))],
    )(*parts)
    mine = 2 * lax.axis_index("x") + lax.axis_index("y")
    return [lax.dynamic_update_index_in_dim(g, a, mine, 0) for g, a in zip(got, parts)]


def _swap_halves(parts, tag=""):
    npart = len(parts)

    def body(*refs):
        ins, outs = refs[:npart], refs[npart:2 * npart]
        send_sems, recv_sems = refs[2 * npart:]
        x, y, c, _ = _place()
        cps = [pltpu.make_async_remote_copy(
            src_ref=ins[p].at[s, 1 - c], dst_ref=outs[p].at[s],
            send_sem=send_sems.at[s * npart + p], recv_sem=recv_sems.at[s * npart + p],
            device_id=(x, y, 1 - c), device_id_type=MESH) for s in range(NSHARD) for p in range(npart)]
        for cp in cps:
            cp.start()
        for cp in cps:
            cp.wait()

    return pl.pallas_call(
        body, name="swap_halves" + tag, in_specs=[ANY] * npart, out_specs=[ANY] * npart,
        out_shape=[jax.ShapeDtypeStruct((NSHARD,) + a.shape[2:], a.dtype) for a in parts],
        scratch_shapes=[pltpu.SemaphoreType.DMA((NSHARD * npart,)), pltpu.SemaphoreType.DMA((NSHARD * npart,))],
    )(*parts)


def _scatter_shards(parts):
    npart = len(parts)

    def body(*refs):
        ins, outs = refs[:npart], refs[npart:2 * npart]
        send_sems, recv_sems = refs[2 * npart:]
        x, y, c, chips = _place()
        cps = [pltpu.make_async_remote_copy(
            src_ref=ins[p].at[2 * cx + cy], dst_ref=outs[p].at[j],
            send_sem=send_sems.at[j * npart + p], recv_sem=recv_sems.at[j * npart + p],
            device_id=(cx, cy, c), device_id_type=MESH)
            for j, (cx, cy) in enumerate(chips) for p in range(npart)]
        for cp in cps:
            cp.start()
        for cp in cps:
            cp.wait()

    return pl.pallas_call(
        body, name="scatter_shards", in_specs=[ANY] * npart, out_specs=[ANY] * npart,
        out_shape=[jax.ShapeDtypeStruct((3,) + a.shape[1:], a.dtype) for a in parts],
        scratch_shapes=[pltpu.SemaphoreType.DMA((3 * npart,)), pltpu.SemaphoreType.DMA((3 * npart,))],
    )(*parts)


HBM = pl.BlockSpec(memory_space=pltpu.HBM)
SEM = pl.BlockSpec(memory_space=pltpu.SEMAPHORE)
EFFECT = pltpu.SideEffectType.DATAFLOW_SIDE_EFFECTING


def _scatter_copies(srcs, lands, send_sems, recv_sems, waiting):
    x, y, c, chips = _place()
    n = len(srcs)
    return [pltpu.make_async_remote_copy(
        src_ref=srcs[p].at[2 * cx + cy], dst_ref=lands[p].at[j],
        send_sem=send_sems.at[j * n + p], recv_sem=recv_sems.at[j * n + p],
        device_id=(cx, cy, c), device_id_type=MESH)
        for j, (cx, cy) in enumerate(chips) for p in range(n)]


def _proj_copies(srcs, lands, send_sems, recv_sems, waiting):
    x, y, c, chips = _place()
    mine = 2 * x + y
    return [pltpu.make_async_remote_copy(
        src_ref=srcs[0].at[c], dst_ref=lands[0].at[mine, c],
        send_sem=send_sems.at[2 * j + to], recv_sem=recv_sems.at[2 * j + (to if waiting else c)],
        device_id=(cx, cy, to), device_id_type=MESH)
        for j, (cx, cy) in enumerate(chips) for to in range(2)]


def _start_copies(copies, nsem, parts, lands, name, after=None):
    n = len(parts)
    extra = [] if after is None else [after]

    def body(*refs):
        outs = refs[2 * n + len(extra):]
        for cp in copies(refs[:n], refs[n:2 * n], outs[0], outs[1], False):
            cp.start()
        outs[-1][...] = jnp.zeros_like(outs[-1])

    res = pl.pallas_call(
        body, name=name,
        out_shape=(pltpu.SemaphoreType.DMA((nsem,)), pltpu.SemaphoreType.DMA((nsem,)),
                   *[pltpu.HBM(a.shape, a.dtype) for a in parts], *[pltpu.HBM(a.shape, a.dtype) for a in lands],
                   jax.ShapeDtypeStruct((8, 128), F32)),
        in_specs=[HBM] * (2 * n) + [ANY] * len(extra),
        out_specs=(SEM, SEM, *[HBM] * (2 * n), pl.BlockSpec(memory_space=pltpu.VMEM)),
        input_output_aliases={i: 2 + i for i in range(2 * n)},
        compiler_params=pltpu.CompilerParams(has_side_effects=EFFECT),
    )(*[pltpu.with_memory_space_constraint(a, pltpu.HBM) for a in parts],
      *[pltpu.with_memory_space_constraint(a, pltpu.HBM) for a in lands], *extra)
    return res[0], res[1], res[2:2 + n], res[2 + n:2 + 2 * n], res[-1]


def _wait_copies(copies, started, after, name):
    send_sems, recv_sems, srcs, lands, _ = started
    n = len(srcs)

    def body(*refs):
        for cp in copies(refs[:n], refs[n:2 * n], refs[2 * n], refs[2 * n + 1], True):
            cp.wait_send()
            cp.wait_recv()

    res = pl.pallas_call(
        body, name=name,
        out_shape=tuple(pltpu.HBM(a.shape, a.dtype) for a in (*srcs, *lands)),
        in_specs=[HBM] * (2 * n) + [SEM, SEM, ANY], out_specs=tuple([HBM] * (2 * n)),
        input_output_aliases={i: i for i in range(2 * n)},
        compiler_params=pltpu.CompilerParams(has_side_effects=EFFECT),
    )(*srcs, *lands, send_sems, recv_sems, after)
    return res[n:]


def _join_halves(parts):
    npart = len(parts)

    def body(*refs):
        ins, outs = refs[:npart], refs[npart:2 * npart]
        send_sems, recv_sems = refs[2 * npart:]
        x, y, c, _ = _place()
        cps = [pltpu.make_async_remote_copy(
            src_ref=ins[p], dst_ref=outs[p], send_sem=send_sems.at[p], recv_sem=recv_sems.at[p],
            device_id=(x, y, 1 - c), device_id_type=MESH) for p in range(npart)]
        for cp in cps:
            cp.start()
        for cp in cps:
            cp.wait()

    return pl.pallas_call(
        body, name="join_halves", in_specs=[ANY] * npart, out_specs=[ANY] * npart,
        out_shape=[jax.ShapeDtypeStruct(a.shape, a.dtype) for a in parts],
        scratch_shapes=[pltpu.SemaphoreType.DMA((npart,)), pltpu.SemaphoreType.DMA((npart,))],
    )(*parts)


def _allreduce_small(v):
    r, ncol = v.shape

    def body(v_ref, o_ref, buf, send_sems, recv_sems):
        x, y, c, _ = _place()
        me = 4 * x + 2 * y + c
        buf[me] = v_ref[...]
        cps = []
        for k in range(1, 8):
            px, py, pc = x ^ (k >> 2), y ^ ((k >> 1) & 1), c ^ (k & 1)
            cps.append(pltpu.make_async_remote_copy(
                src_ref=v_ref, dst_ref=buf.at[me], send_sem=send_sems.at[k - 1], recv_sem=recv_sems.at[k - 1],
                device_id=(px, py, pc), device_id_type=MESH))
        for cp in cps:
            cp.start()
        for k in range(1, 8):
            px, py, pc = x ^ (k >> 2), y ^ ((k >> 1) & 1), c ^ (k & 1)
            pltpu.make_async_remote_copy(
                src_ref=v_ref, dst_ref=buf.at[4 * px + 2 * py + pc], send_sem=send_sems.at[k - 1],
                recv_sem=recv_sems.at[k - 1], device_id=(px, py, pc), device_id_type=MESH).wait_recv()
        for cp in cps:
            cp.wait_send()
        acc = buf[0]
        for d in range(1, 8):
            acc = acc + buf[d]
        o_ref[...] = acc

    return pl.pallas_call(
        body, name="allreduce_small",
        in_specs=[pl.BlockSpec(memory_space=pltpu.VMEM)], out_specs=pl.BlockSpec(memory_space=pltpu.VMEM),
        out_shape=jax.ShapeDtypeStruct((r, ncol), F32),
        scratch_shapes=[pltpu.VMEM((8, r, ncol), F32), pltpu.SemaphoreType.DMA((7,)), pltpu.SemaphoreType.DMA((7,))],
    )(v)


def _permute_rows(shards):
    w0, w1, w2, w3 = shards
    zeros = jnp.zeros((NPERM - 9280, w0.shape[1]), w0.dtype)
    return jnp.concatenate([w0, w1[0:1776], w1[1808:2320], w2, w3[0:240], w3[272:2320],
                            w1[1776:1808], w3[240:272], zeros], axis=0)


def _unpermute_rows(g):
    s1 = jnp.concatenate([g[2320:4096], g[9216:9248], g[4096:4608]], axis=0)
    s3 = jnp.concatenate([g[6928:7168], g[9248:9280], g[7168:9216]], axis=0)
    return jnp.stack([g[0:2320], s1, g[4608:6928], s3], axis=0)


def _pack_shard_small(conv, w2f, w2b):
    top = jnp.pad(conv, ((0, 8 - conv.shape[0]), (0, 0)))
    mid = jnp.pad(jnp.concatenate([w2f, w2b], axis=1), ((0, 0), (0, 768 - 256)))
    return jnp.concatenate([top, mid, jnp.zeros((8, 768), conv.dtype)], axis=0)


def _unpack_shard_small(a):
    return a[0:5], a[8:24, 0:128], a[8:24, 128:256]


def kernel(x, ln_pre_w, w_in, conv_w, a_log_fwd, a_log_bwd, dt_bias_fwd, dt_bias_bwd, gdn_norm_w, w_proj_gdn, gk_w2_fwd, gk_b2_fwd, gk_w2_bwd, gk_b2_bwd, gla_norm_w, w_proj_gla, w_out, ln_post_w, loss_target, m_ln_pre_w, m_w_in, m_conv_w, m_a_log_fwd, m_a_log_bwd, m_dt_bias_fwd, m_dt_bias_bwd, m_gdn_norm_w, m_w_proj_gdn, m_gk_w2_fwd, m_gk_b2_fwd, m_gk_w2_bwd, m_gk_b2_bwd, m_gla_norm_w, m_w_proj_gla, m_w_out, m_ln_post_w, v_ln_pre_w, v_w_in, v_conv_w, v_a_log_fwd, v_a_log_bwd, v_dt_bias_fwd, v_dt_bias_bwd, v_gdn_norm_w, v_w_proj_gdn, v_gk_w2_fwd, v_gk_b2_fwd, v_gk_w2_bwd, v_gk_b2_bwd, v_gla_norm_w, v_w_proj_gla, v_w_out, v_ln_post_w):
    t = x.shape[1]
    x2, tgt = x[0], loss_target[0]

    win_l = w_in[0].T.astype(BF16).reshape(2, SHW // 2, D)
    proj_l = jnp.concatenate([w_proj_gdn[0], w_proj_gla[0], w_out[0]], axis=0).astype(BF16).reshape(2, 384, D)
    small_l = _pack_shard_small(conv_w[0], gk_w2_fwd[0], gk_w2_bwd[0]).reshape(2, 16, 768)
    win_g, small_g = _gather_weights([win_l, small_l])
    proj_started = _start_copies(_proj_copies, 6, [proj_l], [lax.empty((NSHARD, 2, 384, D), BF16)],
                                 "gather_proj_start", after=small_g)
    wperm = _permute_rows(win_g.reshape(NSHARD, SHW, D))
    small_g = small_g.reshape(NSHARD, 32, 768)
    convw = small_g[:, 0:8, :].transpose(1, 0, 2).reshape(8, 3 * D)
    w2f = small_g[:, 8:24, 0:128].transpose(1, 0, 2).reshape(16, 512)
    w2b = small_g[:, 8:24, 128:256].transpose(1, 0, 2).reshape(16, 512)
    w2f_pad = jnp.pad(w2f, ((32, 80), (0, 0)))
    w2b_pad = jnp.pad(w2b, ((48, 64), (0, 0)))
    alog_row = jnp.pad(jnp.concatenate([a_log_fwd, a_log_bwd], axis=1), ((0, 0), (0, 112)))
    dt_row = jnp.pad(jnp.concatenate([dt_bias_fwd, dt_bias_bwd], axis=1), ((0, 0), (0, 112)))

    p, h = _inproj(x2, ln_pre_w + proj_started[4][0:1, 0:1], wperm)
    qn, kn, vc = (_qkv_fwd(p, convw, kind) for kind in range(3))
    gsm, gk = _gates_fwd(p, alog_row, dt_row, w2f_pad, gk_b2_fwd, w2b_pad, gk_b2_bwd)
    g2, b2 = _gcum_fwd(gsm)
    u, w, at, qd, kd, el, tinv = _gdn_intra_fwd(qn, kn, vc, g2, b2)
    oa2, sa = _gdn_scan_fwd(u, w, at, qd, kd, el)
    qg, kdb, intra, elb = _gla_intra_fwd(p, gk)
    ob2, sb = _gla_scan_fwd(p, qg, kdb, intra, elb)

    (proj_land,) = _wait_copies(_proj_copies, proj_started, ob2, "gather_proj_wait")
    mine = 2 * lax.axis_index("x") + lax.axis_index("y")
    w3 = lax.dynamic_update_index_in_dim(proj_land, proj_l, mine, 0).reshape(NSHARD, 3, D // NSHARD, D)
    (loss8, doa, dob, dz, dgb, dga, dgB, dyres, dw3, dgdn_w, dgla_w, dlnpost) = _post(
        oa2, ob2, p, x2, tgt, gdn_norm_w, gla_norm_w, ln_post_w, w3)

    g_proj = dw3.reshape(NSHARD, 2, 384, D)
    sum_proj = _sum_cast(g_proj, _swap_halves([g_proj], "_proj")[0])
    started_proj = _start_copies(_scatter_copies, 3, [sum_proj[1]], [lax.empty((3, 384, D), BF16)],
                                 "scatter_proj_start")
    du, dw, dat, dqd, dkd, del_ = _gdn_scan_bwd(u, w, at, qd, kd, el + started_proj[4][0, 0], sa, doa)
    dqn, dkn, dvc, dg2, db2 = _gdn_intra_bwd(qn, kn, vc, g2, b2, tinv, du, dw, dat, dqd, dkd, del_)
    dgsm = _gcum_bwd(gsm, dg2, db2)
    dqg, dkdb, dvs, delb = _gla_scan_bwd(p, qg, kdb, elb, sb, dob)
    dqb, dkb, dvb, dgk = _gla_intra_bwd(p, gk, dqg, dkdb, dvs, delb, dob)
    (dps, dalog8, ddt8, dw2f_pad, db2f8, dw2b_pad, db2b8) = _gates_bwd(
        p, alog_row, dt_row, w2f_pad, gk_b2_fwd, w2b_pad, gk_b2_bwd, dgsm, dgk)
    dpre, dconv = zip(*[_qkv_bwd(p, convw, g, kind) for kind, g in enumerate((dqn, dkn, dvc))])

    pieces = (jnp.concatenate([a.astype(BF16) for a in (*dpre, dz, dqb, dkb, dvb, dgb, dga, dgB, dps)], axis=1),)
    dwperm = _inproj_dw(h, pieces)

    g_in = _unpermute_rows(dwperm).reshape(NSHARD, 2, SHW // 2, D)
    dconv_full = jnp.concatenate(dconv, axis=1)
    dw2f, dw2b = dw2f_pad[32:48], dw2b_pad[48:64]
    g_small = jnp.stack([_pack_shard_small(dconv_full[0:5, 768 * s:768 * (s + 1)],
                                           dw2f[:, 128 * s:128 * (s + 1)], dw2b[:, 128 * s:128 * (s + 1)])
                         for s in range(NSHARD)])
    g_small = g_small.reshape(NSHARD, 2, 16, 768)
    parts = [g_in, g_small]
    got = _swap_halves(parts)
    sums = [_sum_cast(a, b) for a, b in zip(parts, got)]
    hbs = [hb for _, hb in sums]
    started = _start_copies(_scatter_copies, 3 * len(hbs), hbs,
                            [lax.empty((3,) + a.shape[1:], a.dtype) for a in hbs], "scatter_start")
    dx, dlnpre8 = _inproj_dx(pieces, wperm, x2, ln_pre_w + started[4][0:1, 0:1], dyres)

    gsmall = _allreduce_small(jnp.concatenate(
        [dlnpre8, dalog8, ddt8, dgdn_w, db2f8, db2b8, dgla_w, dlnpost, loss8], axis=1))
    smalls = dict(ln_pre_w=(ln_pre_w, m_ln_pre_w, v_ln_pre_w), a_log_fwd=(a_log_fwd, m_a_log_fwd, v_a_log_fwd),
                  a_log_bwd=(a_log_bwd, m_a_log_bwd, v_a_log_bwd),
                  dt_bias_fwd=(dt_bias_fwd, m_dt_bias_fwd, v_dt_bias_fwd),
                  dt_bias_bwd=(dt_bias_bwd, m_dt_bias_bwd, v_dt_bias_bwd),
                  gdn_norm_w=(gdn_norm_w, m_gdn_norm_w, v_gdn_norm_w),
                  gk_b2_fwd=(gk_b2_fwd, m_gk_b2_fwd, v_gk_b2_fwd), gk_b2_bwd=(gk_b2_bwd, m_gk_b2_bwd, v_gk_b2_bwd),
                  gla_norm_w=(gla_norm_w, m_gla_norm_w, v_gla_norm_w), ln_post_w=(ln_post_w, m_ln_post_w, v_ln_post_w))
    names = [name for name, _, _, _ in SMALL_SLOTS]
    small = _adam_small(gsmall, *([smalls[n][i] for n in names] for i in range(3)))

    landed_proj = _wait_copies(_scatter_copies, started_proj, small["ln_pre_w"][1], "scatter_proj_wait")
    landed = _wait_copies(_scatter_copies, started, small["ln_pre_w"][1], "scatter_wait")
    sums = [sums[0], sum_proj, sums[1]]
    halves = [_sum4(f, g) for (f, _), g in zip(sums, [landed[0], landed_proj[0], landed[1]])]
    theirs = _join_halves(halves)

    a_in = [a.T for a in _adam(w_in[0].T, halves[0], theirs[0], m_w_in[0].T, v_w_in[0].T, 232, name="in")]
    a_pr = [_adam(w[0], halves[1], theirs[1], m[0], v[0], 128, tile0=2 * i, name=f"proj{i}")
            for i, (w, m, v) in enumerate(((w_proj_gdn, m_w_proj_gdn, v_w_proj_gdn),
                                           (w_proj_gla, m_w_proj_gla, v_w_proj_gla), (w_out, m_w_out, v_w_out)))]
    a_ss = _adam_shard_small(halves[2], theirs[2], (conv_w[0], gk_w2_fwd[0], gk_w2_bwd[0]),
                             (m_conv_w[0], m_gk_w2_fwd[0], m_gk_w2_bwd[0]),
                             (v_conv_w[0], v_gk_w2_fwd[0], v_gk_w2_bwd[0]))

    def family(k):
        conv, w2f_, w2b_ = a_ss[0][k], a_ss[1][k], a_ss[2][k]
        s = {n: small[n][k] for n in names}
        return [s["ln_pre_w"], a_in[k][None], conv[None], s["a_log_fwd"], s["a_log_bwd"], s["dt_bias_fwd"],
                s["dt_bias_bwd"], s["gdn_norm_w"], a_pr[0][k][None], w2f_[None], s["gk_b2_fwd"], w2b_[None],
                s["gk_b2_bwd"], s["gla_norm_w"], a_pr[1][k][None], a_pr[2][k][None], s["ln_post_w"]]

    return (gsmall[0, SMALL_W - 128], dx[None], *family(0), *family(1), *family(2), *family(3))
```
